```python
import math
import jax, jax.numpy as jnp
from jax import lax
import numpy as np

D_MODEL = 1024
BATCH = 32
SEQ = 2048
DEPTH = 2

HEAD_DIM = 64
A_Q_HEADS = 8
A_KV_HEADS = 2
A_GROUP = A_Q_HEADS // A_KV_HEADS
WINDOW = 128
BLOCK = 128
B_HEADS = 8
A_WIDTH = A_Q_HEADS * HEAD_DIM
A_KV_WIDTH = A_KV_HEADS * HEAD_DIM
B_WIDTH = B_HEADS * HEAD_DIM
ATTN_WIDTH = A_WIDTH + B_WIDTH
ATTN_SPLITS = (A_WIDTH, A_KV_WIDTH, A_KV_WIDTH, B_WIDTH, B_WIDTH, B_WIDTH, B_HEADS, ATTN_WIDTH)
ATTN_IN = sum(ATTN_SPLITS)
REL_BUCKETS = 32
REL_MAX_EXACT = 16
REL_MAX_DIST = 128
LRU_WIDTH = D_MODEL
LRU_BLOCKS = 8
LRU_BLOCK_W = LRU_WIDTH // LRU_BLOCKS
CONV_WIDTH = 4
LRU_C = 8.0
N_ATTN_LAYERS = (DEPTH + 1) // 2
N_LRU_LAYERS = DEPTH // 2
EPS = 1e-6

kernel_name = "hybrid_swa_fox_rglru_adaln"


def rmsnorm(x, g):
    x32 = x.astype(jnp.float32)
    y = x32 * lax.rsqrt(jnp.mean(x32 * x32, axis=-1, keepdims=True) + EPS)
    return (y * g.astype(jnp.float32)).astype(x.dtype)


def t5_causal_bucket(rel):
    n = jnp.maximum(rel, 0)
    nf = jnp.maximum(n, 1).astype(jnp.float32)
    large = REL_MAX_EXACT + (jnp.log(nf / REL_MAX_EXACT) / math.log(REL_MAX_DIST / REL_MAX_EXACT)
                             * (REL_BUCKETS - REL_MAX_EXACT)).astype(jnp.int32)
    large = jnp.minimum(large, REL_BUCKETS - 1)
    return jnp.where(n < REL_MAX_EXACT, n, large)


def swa_sink_attention(q, k, v, sinks, rel_bias):
    B, S = q.shape[0], q.shape[1]
    nb = S // BLOCK
    qb = q.reshape(B, nb, BLOCK, A_KV_HEADS, A_GROUP, HEAD_DIM)
    pad = ((0, 0), (BLOCK, 0), (0, 0), (0, 0))
    kp = jnp.pad(k, pad)[:, :S].reshape(B, nb, BLOCK, A_KV_HEADS, HEAD_DIM)
    vp = jnp.pad(v, pad)[:, :S].reshape(B, nb, BLOCK, A_KV_HEADS, HEAD_DIM)
    kb = jnp.concatenate([kp, k.reshape(B, nb, BLOCK, A_KV_HEADS, HEAD_DIM)], axis=2)
    vb = jnp.concatenate([vp, v.reshape(B, nb, BLOCK, A_KV_HEADS, HEAD_DIM)], axis=2)
    scores = jnp.einsum('bnqhgd,bnkhd->bhgnqk', qb, kb).astype(jnp.float32) * (HEAD_DIM ** -0.5)
    qi = jnp.arange(BLOCK)[:, None]
    kj = jnp.arange(2 * BLOCK)[None, :]
    rel = qi - kj + BLOCK
    bias = rel_bias.astype(jnp.float32)[t5_causal_bucket(rel)]
    bias = jnp.transpose(bias, (2, 0, 1)).reshape(A_KV_HEADS, A_GROUP, 1, BLOCK, 2 * BLOCK)
    valid = (rel >= 0) & (rel < WINDOW)
    first = (jnp.arange(nb)[:, None, None] == 0) & (kj[None] < BLOCK)
    mask = valid[None] & ~first
    logits = jnp.where(mask, scores + bias, -jnp.inf)
    sink = jnp.broadcast_to(sinks.astype(jnp.float32).reshape(1, A_KV_HEADS, A_GROUP, 1, 1, 1),
                            logits.shape[:-1] + (1,))
    probs = jax.nn.softmax(jnp.concatenate([logits, sink], axis=-1), axis=-1)[..., :-1]
    out = jnp.einsum('bhgnqk,bnkhd->bnqhgd', probs.astype(v.dtype), vb)
    return out.reshape(B, S, A_WIDTH)


def forgetting_attention(q, k, v, log_f):
    B, S = q.shape[0], q.shape[1]
    nb = S // BLOCK
    F = jnp.cumsum(log_f, axis=1)
    outs = []
    for n in range(nb):
        q0, kend = n * BLOCK, (n + 1) * BLOCK
        qs = q[:, q0:kend]
        ks, vs = k[:, :kend], v[:, :kend]
        s = jnp.einsum('bqhd,bkhd->bhqk', qs, ks).astype(jnp.float32) * (HEAD_DIM ** -0.5)
        decay = jnp.transpose(F[:, q0:kend], (0, 2, 1))[..., :, None] - jnp.transpose(F[:, :kend], (0, 2, 1))[..., None, :]
        tpos = q0 + jnp.arange(BLOCK)[:, None]
        spos = jnp.arange(kend)[None, :]
        p = jax.nn.softmax(jnp.where(spos <= tpos, s + decay, -jnp.inf), axis=-1)
        outs.append(jnp.einsum('bhqk,bkhd->bqhd', p.astype(v.dtype), vs))
    return jnp.concatenate(outs, axis=1).reshape(B, S, B_WIDTH)


def attention_mixer(h, w_in, sinks, b_f, w_out, rel_bias):
    B, S, _ = h.shape
    proj = h @ w_in
    idx = np.cumsum(ATTN_SPLITS)[:-1].tolist()
    a_q, a_k, a_v, b_q, b_k, b_v, f_logit, gate = jnp.split(proj, idx, axis=-1)
    a_out = swa_sink_attention(a_q.reshape(B, S, A_Q_HEADS, HEAD_DIM),
                               a_k.reshape(B, S, A_KV_HEADS, HEAD_DIM),
                               a_v.reshape(B, S, A_KV_HEADS, HEAD_DIM), sinks, rel_bias)
    log_f = jax.nn.log_sigmoid((f_logit + b_f).astype(jnp.float32))
    b_out = forgetting_attention(b_q.reshape(B, S, B_HEADS, HEAD_DIM),
                                 b_k.reshape(B, S, B_HEADS, HEAD_DIM),
                                 b_v.reshape(B, S, B_HEADS, HEAD_DIM), log_f)
    y = jnp.concatenate([a_out, b_out], axis=-1) * jax.nn.silu(gate)
    return y @ w_out


def rglru_mixer(h, w_in, conv_w, conv_b, w_a, b_a, w_x, b_x, lam, w_out):
    B, S, _ = h.shape
    proj = h @ w_in
    xr, gate = proj[..., :LRU_WIDTH], proj[..., LRU_WIDTH:]
    xp = jnp.pad(xr, ((0, 0), (CONV_WIDTH - 1, 0), (0, 0)))
    xc = conv_b
    for j in range(CONV_WIDTH):
        xc = xc + xp[:, j:j + S] * conv_w[j]
    xblk = xc.reshape(B, S, LRU_BLOCKS, LRU_BLOCK_W)
    r = jax.nn.sigmoid((jnp.einsum('bsnw,nwv->bsnv', xblk, w_a).reshape(B, S, LRU_WIDTH) + b_a).astype(jnp.float32))
    i = jax.nn.sigmoid((jnp.einsum('bsnw,nwv->bsnv', xblk, w_x).reshape(B, S, LRU_WIDTH) + b_x).astype(jnp.float32))
    log_a = -LRU_C * r * jax.nn.softplus(-lam.astype(jnp.float32))
    a = jnp.exp(log_a)
    u = jnp.sqrt(-jnp.expm1(2.0 * log_a)) * (i * xc.astype(jnp.float32))

    def step(state, inp):
        a_t, u_t = inp
        state = a_t * state + u_t
        return state, state

    _, hs = lax.scan(step, jnp.zeros((B, LRU_WIDTH), jnp.float32),
                     (jnp.transpose(a, (1, 0, 2)), jnp.transpose(u, (1, 0, 2))))
    y = jnp.transpose(hs, (1, 0, 2)).astype(h.dtype) * jax.nn.silu(gate)
    return y @ w_out


def _fwd_setup_inputs(seed: int = 0) -> dict:
    key = jax.random.key(seed)
    ks = jax.random.split(key, 22)
    nrm = lambda k, shape, s: jax.random.normal(k, shape, jnp.float32) * s
    a0 = jax.random.uniform(ks[19], (N_LRU_LAYERS, LRU_WIDTH), jnp.float32, 0.9, 0.999) ** (1.0 / LRU_C)
    return {
        "x": nrm(ks[0], (BATCH, SEQ, D_MODEL), 1.0),
        "c": nrm(ks[1], (BATCH, D_MODEL), 1.0),
        "rel_bias": nrm(ks[2], (REL_BUCKETS, A_Q_HEADS), 0.2),
        "norm_g": 1.0 + nrm(ks[3], (DEPTH, D_MODEL), 0.05),
        "ada_w": nrm(ks[4], (DEPTH, D_MODEL, 3 * D_MODEL), 0.3 * D_MODEL ** -0.5),
        "ada_b": nrm(ks[5], (DEPTH, 3 * D_MODEL), 0.02),
        "attn_w_in": nrm(ks[6], (N_ATTN_LAYERS, D_MODEL, ATTN_IN), D_MODEL ** -0.5),
        "attn_sinks": nrm(ks[7], (N_ATTN_LAYERS, A_Q_HEADS), 0.5),
        "attn_b_f": jax.random.uniform(ks[8], (N_ATTN_LAYERS, B_HEADS), jnp.float32, 1.0, 4.0),
        "attn_w_out": nrm(ks[9], (N_ATTN_LAYERS, ATTN_WIDTH, D_MODEL), ATTN_WIDTH ** -0.5),
        "lru_w_in": nrm(ks[10], (N_LRU_LAYERS, D_MODEL, 2 * LRU_WIDTH), D_MODEL ** -0.5),
        "lru_conv_w": nrm(ks[11], (N_LRU_LAYERS, CONV_WIDTH, LRU_WIDTH), CONV_WIDTH ** -0.5),
        "lru_conv_b": nrm(ks[12], (N_LRU_LAYERS, LRU_WIDTH), 0.02),
        "lru_w_a": nrm(ks[13], (N_LRU_LAYERS, LRU_BLOCKS, LRU_BLOCK_W, LRU_BLOCK_W), LRU_BLOCK_W ** -0.5),
        "lru_b_a": nrm(ks[14], (N_LRU_LAYERS, LRU_WIDTH), 0.02),
        "lru_w_x": nrm(ks[15], (N_LRU_LAYERS, LRU_BLOCKS, LRU_BLOCK_W, LRU_BLOCK_W), LRU_BLOCK_W ** -0.5),
        "lru_b_x": nrm(ks[16], (N_LRU_LAYERS, LRU_WIDTH), 0.02),
        "lru_lambda": jnp.log(a0) - jnp.log1p(-a0),
        "lru_w_out": nrm(ks[17], (N_LRU_LAYERS, LRU_WIDTH, D_MODEL), LRU_WIDTH ** -0.5),
        "final_g": 1.0 + nrm(ks[18], (D_MODEL,), 0.05),
    }


def _fwd_reference(x, c, rel_bias, norm_g, ada_w, ada_b, attn_w_in, attn_sinks, attn_b_f, attn_w_out,
              lru_w_in, lru_conv_w, lru_conv_b, lru_w_a, lru_b_a, lru_w_x, lru_b_x, lru_lambda,
              lru_w_out, final_g):
    c_act = jax.nn.silu(c)
    for layer in range(DEPTH):
        mod = c_act @ ada_w[layer] + ada_b[layer]
        shift, scale, gate = jnp.split(mod, 3, axis=-1)
        h = rmsnorm(x, norm_g[layer]) * (1.0 + scale[:, None, :]) + shift[:, None, :]
        if layer % 2 == 0:
            j = layer // 2
            y = attention_mixer(h, attn_w_in[j], attn_sinks[j], attn_b_f[j], attn_w_out[j], rel_bias)
        else:
            j = layer // 2
            y = rglru_mixer(h, lru_w_in[j], lru_conv_w[j], lru_conv_b[j], lru_w_a[j], lru_b_a[j],
                            lru_w_x[j], lru_b_x[j], lru_lambda[j], lru_w_out[j])
        x = x + gate[:, None, :] * y
    return rmsnorm(x, final_g)


import jax as _jax
import jax.numpy as _jnp

TWIN_FORMAT = 'train_step'
FWD_PARAMS = ['x', 'c', 'rel_bias', 'norm_g', 'ada_w', 'ada_b', 'attn_w_in', 'attn_sinks', 'attn_b_f', 'attn_w_out', 'lru_w_in', 'lru_conv_w', 'lru_conv_b', 'lru_w_a', 'lru_b_a', 'lru_w_x', 'lru_b_x', 'lru_lambda', 'lru_w_out', 'final_g']
TWIN_WEIGHTS = ['rel_bias', 'norm_g', 'ada_w', 'ada_b', 'attn_w_in', 'attn_sinks', 'attn_b_f', 'attn_w_out', 'lru_w_in', 'lru_conv_w', 'lru_conv_b', 'lru_w_a', 'lru_b_a', 'lru_w_x', 'lru_b_x', 'lru_lambda', 'lru_w_out', 'final_g']
TWIN_DIFF_INPUT = 'x'
TWIN_INPUTS = ['x', 'c', 'rel_bias', 'norm_g', 'ada_w', 'ada_b', 'attn_w_in', 'attn_sinks', 'attn_b_f', 'attn_w_out', 'lru_w_in', 'lru_conv_w', 'lru_conv_b', 'lru_w_a', 'lru_b_a', 'lru_w_x', 'lru_b_x', 'lru_lambda', 'lru_w_out', 'final_g', 'loss_target', 'm_rel_bias', 'm_norm_g', 'm_ada_w', 'm_ada_b', 'm_attn_w_in', 'm_attn_sinks', 'm_attn_b_f', 'm_attn_w_out', 'm_lru_w_in', 'm_lru_conv_w', 'm_lru_conv_b', 'm_lru_w_a', 'm_lru_b_a', 'm_lru_w_x', 'm_lru_b_x', 'm_lru_lambda', 'm_lru_w_out', 'm_final_g', 'v_rel_bias', 'v_norm_g', 'v_ada_w', 'v_ada_b', 'v_attn_w_in', 'v_attn_sinks', 'v_attn_b_f', 'v_attn_w_out', 'v_lru_w_in', 'v_lru_conv_w', 'v_lru_conv_b', 'v_lru_w_a', 'v_lru_b_a', 'v_lru_w_x', 'v_lru_b_x', 'v_lru_lambda', 'v_lru_w_out', 'v_final_g']
TWIN_OUTPUTS = ['loss', 'grad_x', 'grad_rel_bias', 'grad_norm_g', 'grad_ada_w', 'grad_ada_b', 'grad_attn_w_in', 'grad_attn_sinks', 'grad_attn_b_f', 'grad_attn_w_out', 'grad_lru_w_in', 'grad_lru_conv_w', 'grad_lru_conv_b', 'grad_lru_w_a', 'grad_lru_b_a', 'grad_lru_w_x', 'grad_lru_b_x', 'grad_lru_lambda', 'grad_lru_w_out', 'grad_final_g', 'delta_rel_bias', 'delta_norm_g', 'delta_ada_w', 'delta_ada_b', 'delta_attn_w_in', 'delta_attn_sinks', 'delta_attn_b_f', 'delta_attn_w_out', 'delta_lru_w_in', 'delta_lru_conv_w', 'delta_lru_conv_b', 'delta_lru_w_a', 'delta_lru_b_a', 'delta_lru_w_x', 'delta_lru_b_x', 'delta_lru_lambda', 'delta_lru_w_out', 'delta_final_g', 'new_m_rel_bias', 'new_m_norm_g', 'new_m_ada_w', 'new_m_ada_b', 'new_m_attn_w_in', 'new_m_attn_sinks', 'new_m_attn_b_f', 'new_m_attn_w_out', 'new_m_lru_w_in', 'new_m_lru_conv_w', 'new_m_lru_conv_b', 'new_m_lru_w_a', 'new_m_lru_b_a', 'new_m_lru_w_x', 'new_m_lru_b_x', 'new_m_lru_lambda', 'new_m_lru_w_out', 'new_m_final_g', 'new_v_rel_bias', 'new_v_norm_g', 'new_v_ada_w', 'new_v_ada_b', 'new_v_attn_w_in', 'new_v_attn_sinks', 'new_v_attn_b_f', 'new_v_attn_w_out', 'new_v_lru_w_in', 'new_v_lru_conv_w', 'new_v_lru_conv_b', 'new_v_lru_w_a', 'new_v_lru_b_a', 'new_v_lru_w_x', 'new_v_lru_b_x', 'new_v_lru_lambda', 'new_v_lru_w_out', 'new_v_final_g']
TWIN_LEAF_KINDS = {'loss': 'loss', 'grad_x': 'grad_x', 'grad_rel_bias': 'grad_w', 'grad_norm_g': 'grad_w', 'grad_ada_w': 'grad_w', 'grad_ada_b': 'grad_w', 'grad_attn_w_in': 'grad_w', 'grad_attn_sinks': 'grad_w', 'grad_attn_b_f': 'grad_w', 'grad_attn_w_out': 'grad_w', 'grad_lru_w_in': 'grad_w', 'grad_lru_conv_w': 'grad_w', 'grad_lru_conv_b': 'grad_w', 'grad_lru_w_a': 'grad_w', 'grad_lru_b_a': 'grad_w', 'grad_lru_w_x': 'grad_w', 'grad_lru_b_x': 'grad_w', 'grad_lru_lambda': 'grad_w', 'grad_lru_w_out': 'grad_w', 'grad_final_g': 'grad_w', 'delta_rel_bias': 'delta_w', 'delta_norm_g': 'delta_w', 'delta_ada_w': 'delta_w', 'delta_ada_b': 'delta_w', 'delta_attn_w_in': 'delta_w', 'delta_attn_sinks': 'delta_w', 'delta_attn_b_f': 'delta_w', 'delta_attn_w_out': 'delta_w', 'delta_lru_w_in': 'delta_w', 'delta_lru_conv_w': 'delta_w', 'delta_lru_conv_b': 'delta_w', 'delta_lru_w_a': 'delta_w', 'delta_lru_b_a': 'delta_w', 'delta_lru_w_x': 'delta_w', 'delta_lru_b_x': 'delta_w', 'delta_lru_lambda': 'delta_w', 'delta_lru_w_out': 'delta_w', 'delta_final_g': 'delta_w', 'new_m_rel_bias': 'new_m', 'new_m_norm_g': 'new_m', 'new_m_ada_w': 'new_m', 'new_m_ada_b': 'new_m', 'new_m_attn_w_in': 'new_m', 'new_m_attn_sinks': 'new_m', 'new_m_attn_b_f': 'new_m', 'new_m_attn_w_out': 'new_m', 'new_m_lru_w_in': 'new_m', 'new_m_lru_conv_w': 'new_m', 'new_m_lru_conv_b': 'new_m', 'new_m_lru_w_a': 'new_m', 'new_m_lru_b_a': 'new_m', 'new_m_lru_w_x': 'new_m', 'new_m_lru_b_x': 'new_m', 'new_m_lru_lambda': 'new_m', 'new_m_lru_w_out': 'new_m', 'new_m_final_g': 'new_m', 'new_v_rel_bias': 'new_v', 'new_v_norm_g': 'new_v', 'new_v_ada_w': 'new_v', 'new_v_ada_b': 'new_v', 'new_v_attn_w_in': 'new_v', 'new_v_attn_sinks': 'new_v', 'new_v_attn_b_f': 'new_v', 'new_v_attn_w_out': 'new_v', 'new_v_lru_w_in': 'new_v', 'new_v_lru_conv_w': 'new_v', 'new_v_lru_conv_b': 'new_v', 'new_v_lru_w_a': 'new_v', 'new_v_lru_b_a': 'new_v', 'new_v_lru_w_x': 'new_v', 'new_v_lru_b_x': 'new_v', 'new_v_lru_lambda': 'new_v', 'new_v_lru_w_out': 'new_v', 'new_v_final_g': 'new_v'}


def _forward(args):
    return _fwd_reference(*[args[k] for k in FWD_PARAMS])


def _output_shape():
    out = _jax.eval_shape(lambda: _forward(_fwd_setup_inputs(0)))
    return out.shape, out.dtype

N_MICROBATCH = 1
ADAM_LR = 0.001
ADAM_B1 = 0.9
ADAM_B2 = 0.999
ADAM_EPS = 1e-08
ADAM_WD = 0.01
ADAM_STEP = 10
PER_EXAMPLE_BATCH_AXIS = {'x': 0, 'c': 0, 'loss_target': 0}
SHARED_INPUTS = []
_WEIGHT_DTYPES = {'rel_bias': _jnp.float32, 'norm_g': _jnp.float32, 'ada_w': _jnp.float32, 'ada_b': _jnp.float32, 'attn_w_in': _jnp.float32, 'attn_sinks': _jnp.float32, 'attn_b_f': _jnp.float32, 'attn_w_out': _jnp.float32, 'lru_w_in': _jnp.float32, 'lru_conv_w': _jnp.float32, 'lru_conv_b': _jnp.float32, 'lru_w_a': _jnp.float32, 'lru_b_a': _jnp.float32, 'lru_w_x': _jnp.float32, 'lru_b_x': _jnp.float32, 'lru_lambda': _jnp.float32, 'lru_w_out': _jnp.float32, 'final_g': _jnp.float32}
MOMENT_SCALE = {'rel_bias': 7.613743e-03, 'norm_g': 3.575216e-02, 'ada_w': 1.073820e-01, 'ada_b': 1.806040e-01, 'attn_w_in': 1.136312e-02, 'attn_sinks': 4.552539e-03, 'attn_b_f': 9.236978e-02, 'attn_w_out': 1.162797e-02, 'lru_w_in': 4.885277e-02, 'lru_conv_w': 5.934172e-02, 'lru_conv_b': 3.122780e-01, 'lru_w_a': 7.727527e-03, 'lru_b_a': 1.241150e-02, 'lru_w_x': 1.484605e-02, 'lru_b_x': 3.012938e-02, 'lru_lambda': 3.178959e-02, 'lru_w_out': 5.026595e-02, 'final_g': 6.406185e+01}


def _to_microbatches(a, axis):
    t = _jnp.moveaxis(a, axis, 0)
    t = t.reshape((N_MICROBATCH, t.shape[0] // N_MICROBATCH) + t.shape[1:])
    return _jnp.moveaxis(t, 1, axis + 1)


def setup_inputs(seed: int = 0) -> dict:
    inp = _fwd_setup_inputs(seed)
    key = _jax.random.fold_in(_jax.random.key(seed), 7919)
    shape, _ = _output_shape()
    out = dict(inp)
    out["loss_target"] = _jax.random.normal(_jax.random.fold_in(key, 0), shape, _jnp.float32)
    for i, name in enumerate(TWIN_WEIGHTS):
        w = inp[name].astype(_jnp.float32)
        if MOMENT_SCALE is None:
            s = _jnp.sqrt(_jnp.mean(_jnp.square(w)) + 1e-30)
        else:
            s = MOMENT_SCALE[name]
        km, kv = _jax.random.split(_jax.random.fold_in(key, i + 1))
        out[name] = w
        out["m_" + name] = s * _jax.random.normal(km, w.shape, _jnp.float32)
        out["v_" + name] = (s * s) * _jax.random.uniform(kv, w.shape, _jnp.float32, 0.5, 1.5)
    if N_MICROBATCH > 1:
        for name, axis in PER_EXAMPLE_BATCH_AXIS.items():
            out[name] = _to_microbatches(out[name], axis)
    return {'x': out['x'], 'c': out['c'], 'rel_bias': out['rel_bias'], 'norm_g': out['norm_g'], 'ada_w': out['ada_w'], 'ada_b': out['ada_b'], 'attn_w_in': out['attn_w_in'], 'attn_sinks': out['attn_sinks'], 'attn_b_f': out['attn_b_f'], 'attn_w_out': out['attn_w_out'], 'lru_w_in': out['lru_w_in'], 'lru_conv_w': out['lru_conv_w'], 'lru_conv_b': out['lru_conv_b'], 'lru_w_a': out['lru_w_a'], 'lru_b_a': out['lru_b_a'], 'lru_w_x': out['lru_w_x'], 'lru_b_x': out['lru_b_x'], 'lru_lambda': out['lru_lambda'], 'lru_w_out': out['lru_w_out'], 'final_g': out['final_g'], 'loss_target': out['loss_target'], 'm_rel_bias': out['m_rel_bias'], 'm_norm_g': out['m_norm_g'], 'm_ada_w': out['m_ada_w'], 'm_ada_b': out['m_ada_b'], 'm_attn_w_in': out['m_attn_w_in'], 'm_attn_sinks': out['m_attn_sinks'], 'm_attn_b_f': out['m_attn_b_f'], 'm_attn_w_out': out['m_attn_w_out'], 'm_lru_w_in': out['m_lru_w_in'], 'm_lru_conv_w': out['m_lru_conv_w'], 'm_lru_conv_b': out['m_lru_conv_b'], 'm_lru_w_a': out['m_lru_w_a'], 'm_lru_b_a': out['m_lru_b_a'], 'm_lru_w_x': out['m_lru_w_x'], 'm_lru_b_x': out['m_lru_b_x'], 'm_lru_lambda': out['m_lru_lambda'], 'm_lru_w_out': out['m_lru_w_out'], 'm_final_g': out['m_final_g'], 'v_rel_bias': out['v_rel_bias'], 'v_norm_g': out['v_norm_g'], 'v_ada_w': out['v_ada_w'], 'v_ada_b': out['v_ada_b'], 'v_attn_w_in': out['v_attn_w_in'], 'v_attn_sinks': out['v_attn_sinks'], 'v_attn_b_f': out['v_attn_b_f'], 'v_attn_w_out': out['v_attn_w_out'], 'v_lru_w_in': out['v_lru_w_in'], 'v_lru_conv_w': out['v_lru_conv_w'], 'v_lru_conv_b': out['v_lru_conv_b'], 'v_lru_w_a': out['v_lru_w_a'], 'v_lru_b_a': out['v_lru_b_a'], 'v_lru_w_x': out['v_lru_w_x'], 'v_lru_b_x': out['v_lru_b_x'], 'v_lru_lambda': out['v_lru_lambda'], 'v_lru_w_out': out['v_lru_w_out'], 'v_final_g': out['v_final_g']}


def _loss(weights, diff, rest, loss_target):
    with _jax.named_scope("forward"):
        args = {**rest, TWIN_DIFF_INPUT: diff, **{k: w.astype(_WEIGHT_DTYPES[k]) for k, w in weights.items()}}
        y = _forward(args)
    with _jax.named_scope("loss_head"):
        err = _jnp.square(y.astype(_jnp.float32) - loss_target)
        return 0.5 * _jnp.sum(_jnp.mean(err, axis=-1)) if err.ndim else 0.5 * err


def _adamw(w, g, m, v):
    m = ADAM_B1 * m + (1.0 - ADAM_B1) * g
    v = ADAM_B2 * v + (1.0 - ADAM_B2) * _jnp.square(g)
    m_hat = m / (1.0 - ADAM_B1 ** ADAM_STEP)
    v_hat = v / (1.0 - ADAM_B2 ** ADAM_STEP)
    delta = -ADAM_LR * (m_hat / (_jnp.sqrt(v_hat) + ADAM_EPS) + ADAM_WD * w)
    return delta, m, v


def reference(x, c, rel_bias, norm_g, ada_w, ada_b, attn_w_in, attn_sinks, attn_b_f, attn_w_out, lru_w_in, lru_conv_w, lru_conv_b, lru_w_a, lru_b_a, lru_w_x, lru_b_x, lru_lambda, lru_w_out, final_g, loss_target, m_rel_bias, m_norm_g, m_ada_w, m_ada_b, m_attn_w_in, m_attn_sinks, m_attn_b_f, m_attn_w_out, m_lru_w_in, m_lru_conv_w, m_lru_conv_b, m_lru_w_a, m_lru_b_a, m_lru_w_x, m_lru_b_x, m_lru_lambda, m_lru_w_out, m_final_g, v_rel_bias, v_norm_g, v_ada_w, v_ada_b, v_attn_w_in, v_attn_sinks, v_attn_b_f, v_attn_w_out, v_lru_w_in, v_lru_conv_w, v_lru_conv_b, v_lru_w_a, v_lru_b_a, v_lru_w_x, v_lru_b_x, v_lru_lambda, v_lru_w_out, v_final_g):
    given = dict(x=x, c=c, rel_bias=rel_bias, norm_g=norm_g, ada_w=ada_w, ada_b=ada_b, attn_w_in=attn_w_in, attn_sinks=attn_sinks, attn_b_f=attn_b_f, attn_w_out=attn_w_out, lru_w_in=lru_w_in, lru_conv_w=lru_conv_w, lru_conv_b=lru_conv_b, lru_w_a=lru_w_a, lru_b_a=lru_b_a, lru_w_x=lru_w_x, lru_b_x=lru_b_x, lru_lambda=lru_lambda, lru_w_out=lru_w_out, final_g=final_g, loss_target=loss_target, m_rel_bias=m_rel_bias, m_norm_g=m_norm_g, m_ada_w=m_ada_w, m_ada_b=m_ada_b, m_attn_w_in=m_attn_w_in, m_attn_sinks=m_attn_sinks, m_attn_b_f=m_attn_b_f, m_attn_w_out=m_attn_w_out, m_lru_w_in=m_lru_w_in, m_lru_conv_w=m_lru_conv_w, m_lru_conv_b=m_lru_conv_b, m_lru_w_a=m_lru_w_a, m_lru_b_a=m_lru_b_a, m_lru_w_x=m_lru_w_x, m_lru_b_x=m_lru_b_x, m_lru_lambda=m_lru_lambda, m_lru_w_out=m_lru_w_out, m_final_g=m_final_g, v_rel_bias=v_rel_bias, v_norm_g=v_norm_g, v_ada_w=v_ada_w, v_ada_b=v_ada_b, v_attn_w_in=v_attn_w_in, v_attn_sinks=v_attn_sinks, v_attn_b_f=v_attn_b_f, v_attn_w_out=v_attn_w_out, v_lru_w_in=v_lru_w_in, v_lru_conv_w=v_lru_conv_w, v_lru_conv_b=v_lru_conv_b, v_lru_w_a=v_lru_w_a, v_lru_b_a=v_lru_b_a, v_lru_w_x=v_lru_w_x, v_lru_b_x=v_lru_b_x, v_lru_lambda=v_lru_lambda, v_lru_w_out=v_lru_w_out, v_final_g=v_final_g)
    weights = {n: given[n] for n in TWIN_WEIGHTS}
    shared = {n: given[n] for n in SHARED_INPUTS}
    per_example = {n: given[n] for n in ['x', 'c']}
    grad_fn = _jax.value_and_grad(_loss, argnums=(0, 1))

    def one_microbatch(ex, loss_target):
        ex = dict(ex)
        diff = ex.pop(TWIN_DIFF_INPUT)
        return grad_fn(weights, diff, {**shared, **ex}, loss_target)

    if N_MICROBATCH == 1:
        loss, (grad_w, grad_x) = one_microbatch(per_example, given["loss_target"])
    else:
        def body(carry, xs):
            loss_sum, grad_sum = carry
            l_k, (gw_k, gx_k) = one_microbatch(xs[0], xs[1])
            with _jax.named_scope("update"):
                return (loss_sum + l_k, _jax.tree.map(_jnp.add, grad_sum, gw_k)), gx_k

        init = (_jnp.zeros((), _jnp.float32), _jax.tree.map(_jnp.zeros_like, weights))
        (loss, grad_w), grad_x = _jax.lax.scan(body, init, (per_example, given["loss_target"]))
    with _jax.named_scope("update"):
        delta_w, new_m, new_v = {}, {}, {}
        for n in TWIN_WEIGHTS:
            delta_w[n], new_m[n], new_v[n] = _adamw(weights[n], grad_w[n], given["m_" + n], given["v_" + n])
    return (loss, grad_x, *[grad_w[n] for n in TWIN_WEIGHTS], *[delta_w[n] for n in TWIN_WEIGHTS],
            *[new_m[n] for n in TWIN_WEIGHTS], *[new_v[n] for n in TWIN_WEIGHTS])
```

```python
import functools
import math

import numpy as np
import jax
import jax.numpy as jnp
from jax import lax
from jax.experimental import pallas as pl
from jax.experimental.pallas import tpu as pltpu

F32 = jnp.float32
BF16 = jnp.bfloat16
MESH = pl.DeviceIdType.MESH

N_DEV = 8
N_CHIP = 4
HEAD_DIM = 64
BLK = 128
A_Q_HEADS = 8
A_KV_HEADS = 2
A_GROUP = A_Q_HEADS // A_KV_HEADS
B_HEADS = 8
REL_BUCKETS = 32
REL_MAX_EXACT = 16
REL_MAX_DIST = 128
LRU_BLOCKS = 8
LRU_C = 8.0
CONV_WIDTH = 4
EPS = 1e-6
NEG = -1e30
SCALE = HEAD_DIM ** -0.5
LANE = 128
SUBLANE = 8
VMEM_LIMIT = 56 * 1024 * 1024
SCAN_CHUNK = 256
ADAM_LR = 0.001
ADAM_B1 = 0.9
ADAM_B2 = 0.999
ADAM_EPS = 1e-08
ADAM_WD = 0.01
ADAM_STEP = 10
HI = lax.Precision.HIGHEST


def _cp(sem=None):
    return pltpu.CompilerParams(dimension_semantics=sem, vmem_limit_bytes=VMEM_LIMIT)


def _dot(a, b):
    return jnp.dot(a, b, preferred_element_type=F32)


def _dot_nt(a, b):
    return lax.dot_general(a, b, (((1,), (1,)), ((), ())), preferred_element_type=F32)


def _dot_tn(a, b):
    return lax.dot_general(a, b, (((0,), (0,)), ((), ())), preferred_element_type=F32)


def _sigmoid(z):
    return 1.0 / (1.0 + jnp.exp(-z))


def _row_tile(rows, cap):
    if rows <= cap:
        return rows
    best = SUBLANE
    t = SUBLANE
    while t <= cap:
        if rows % t == 0:
            best = t
        t += SUBLANE
    return best


def _all_gather8(x_shard, name, space):
    m_per, n = x_shard.shape

    def body(x_ref, out_ref, send_sems, recv_sems, local_sem):
        x, y, c = lax.axis_index("x"), lax.axis_index("y"), lax.axis_index("c")
        me, sibling = (x, y, c), (x, y, 1 - c)
        chips = [(1 - x, y), (x, 1 - y), (1 - x, 1 - y)]

        def rows(px, py, pc):
            return out_ref.at[pl.ds((4 * px + 2 * py + pc) * m_per, m_per), :]

        def copy(k, block, to, src=None):
            return pltpu.make_async_remote_copy(
                src_ref=rows(*block) if src is None else src, dst_ref=rows(*block),
                send_sem=send_sems.at[k], recv_sem=recv_sems.at[k], device_id=to, device_id_type=MESH)

        mine = pltpu.make_async_copy(x_ref, rows(*me), local_sem)
        mine.start()
        first = [copy(0, me, sibling, src=x_ref)]
        first += [copy(1 + j, me, (*chip, c), src=x_ref) for j, chip in enumerate(chips)]
        for cp in first:
            cp.start()
        passed = [copy(4 + j, (*chip, c), sibling) for j, chip in enumerate(chips)]
        for j, chip in enumerate(chips):
            copy(1 + j, (*chip, c), me).wait_recv()
            passed[j].start()
        copy(0, sibling, me).wait_recv()
        for j, chip in enumerate(chips):
            copy(4 + j, (*chip, 1 - c), me).wait_recv()
        for cp in first + passed:
            cp.wait_send()
        mine.wait()

    return pl.pallas_call(
        body, name=name,
        out_shape=jax.ShapeDtypeStruct((N_DEV * m_per, n), x_shard.dtype),
        in_specs=[pl.BlockSpec(memory_space=space)],
        out_specs=pl.BlockSpec(memory_space=space),
        scratch_shapes=[pltpu.SemaphoreType.DMA((7,)), pltpu.SemaphoreType.DMA((7,)), pltpu.SemaphoreType.DMA],
        compiler_params=pltpu.CompilerParams(vmem_limit_bytes=VMEM_LIMIT),
    )(x_shard)


def _all_to_all8(pieces, name):
    _, m, n = pieces.shape

    def body(x_ref, out_ref, send_sems, recv_sems, local_sem):
        x, y, c = lax.axis_index("x"), lax.axis_index("y"), lax.axis_index("c")
        me = 4 * x + 2 * y + c
        mine = pltpu.make_async_copy(x_ref.at[me], out_ref.at[me], local_sem)
        mine.start()
        copies = []
        for k in range(1, N_DEV):
            fx, fy, fc = (k >> 2) & 1, (k >> 1) & 1, k & 1
            px, py, pc = x ^ fx, y ^ fy, c ^ fc
            peer = 4 * px + 2 * py + pc
            copies.append(pltpu.make_async_remote_copy(
                src_ref=x_ref.at[peer], dst_ref=out_ref.at[me],
                send_sem=send_sems.at[k - 1], recv_sem=recv_sems.at[k - 1],
                device_id=(px, py, pc), device_id_type=MESH))
        for cp in copies:
            cp.start()
        for cp in copies:
            cp.wait_recv()
        for cp in copies:
            cp.wait_send()
        mine.wait()

    hbm = pl.BlockSpec(memory_space=pltpu.HBM)
    return pl.pallas_call(
        body, name=name,
        out_shape=jax.ShapeDtypeStruct(pieces.shape, pieces.dtype),
        in_specs=[hbm], out_specs=hbm,
        scratch_shapes=[pltpu.SemaphoreType.DMA((7,)), pltpu.SemaphoreType.DMA((7,)), pltpu.SemaphoreType.DMA],
    )(pieces)


def _sibling_swap(half, name):
    m, n = half.shape

    def body(x_ref, out_ref, send_sem, recv_sem, local_sem):
        x, y, c = lax.axis_index("x"), lax.axis_index("y"), lax.axis_index("c")
        mine = pltpu.make_async_copy(x_ref, out_ref.at[c], local_sem)
        mine.start()
        cp = pltpu.make_async_remote_copy(
            src_ref=x_ref, dst_ref=out_ref.at[c], send_sem=send_sem, recv_sem=recv_sem,
            device_id=(x, y, 1 - c), device_id_type=MESH)
        cp.start()
        cp.wait_recv()
        cp.wait_send()
        mine.wait()

    hbm = pl.BlockSpec(memory_space=pltpu.HBM)
    return pl.pallas_call(
        body, name=name,
        out_shape=jax.ShapeDtypeStruct((2, m, n), half.dtype),
        in_specs=[hbm], out_specs=hbm,
        scratch_shapes=[pltpu.SemaphoreType.DMA, pltpu.SemaphoreType.DMA, pltpu.SemaphoreType.DMA],
    )(half)


def _sum_slots(slots, name):
    k, m, n = slots.shape
    tr = _row_tile(m, 328)

    def body(s_ref, o_ref):
        acc = s_ref[0]
        for j in range(1, k):
            acc = acc + s_ref[j]
        o_ref[...] = acc

    return pl.pallas_call(
        body, name=name, grid=(m // tr,),
        out_shape=jax.ShapeDtypeStruct((m, n), F32),
        in_specs=[pl.BlockSpec((k, tr, n), lambda i: (0, i, 0))],
        out_specs=pl.BlockSpec((tr, n), lambda i: (i, 0)),
        compiler_params=_cp(("parallel",)),
    )(slots)


def _ada_fwd(c_all, w, b, name):
    r, _ = c_all.shape
    n = w.shape[1]

    def body(c_ref, w_ref, b_ref, o_ref):
        cv = c_ref[...]
        act = cv * _sigmoid(cv)
        o_ref[...] = jnp.dot(act, w_ref[...], precision=HI, preferred_element_type=F32) + b_ref[...]

    return pl.pallas_call(body, name=name, out_shape=jax.ShapeDtypeStruct((r, n), F32),
                          compiler_params=_cp())(c_all, w, b)


def _ada_bwd(c_all, dmod_chip, dmod_all, name):
    r, d = c_all.shape
    nl, _, n = dmod_chip.shape

    def body(c_ref, dm_ref, da_ref, gw_ref, gb_ref):
        cv = c_ref[...]
        act = cv * _sigmoid(cv)
        for l in range(nl):
            gw_ref[l] = lax.dot_general(act, dm_ref[l], (((0,), (0,)), ((), ())), precision=HI,
                                        preferred_element_type=F32)
        gb_ref[...] = jnp.sum(da_ref[...], axis=0, keepdims=True)

    return pl.pallas_call(
        body, name=name,
        out_shape=(jax.ShapeDtypeStruct((nl, d, n), F32), jax.ShapeDtypeStruct((1, dmod_all.shape[1]), F32)),
        compiler_params=_cp())(c_all, dmod_chip, dmod_all)


def _norm_proj(x, g, scale, shift, w, name):
    b, s, d = x.shape
    n = w.shape[1]
    tm = min(s, 256)

    def body(x_ref, g_ref, sc_ref, sh_ref, w_ref, proj_ref, h_ref):
        xv = x_ref[...]
        rstd = lax.rsqrt(jnp.mean(xv * xv, axis=-1, keepdims=True) + EPS)
        h = (xv * rstd) * g_ref[...] * (1.0 + sc_ref[...]) + sh_ref[...]
        hb = h.astype(BF16)
        h_ref[...] = hb
        proj_ref[...] = _dot(hb, w_ref[...])

    return pl.pallas_call(
        body, name=name, grid=(b, s // tm),
        out_shape=(jax.ShapeDtypeStruct((b, s, n), F32), jax.ShapeDtypeStruct((b, s, d), BF16)),
        in_specs=[pl.BlockSpec((None, tm, d), lambda i, j: (i, j, 0)),
                  pl.BlockSpec((1, d), lambda i, j: (0, 0)),
                  pl.BlockSpec((None, 1, d), lambda i, j: (i, 0, 0)),
                  pl.BlockSpec((None, 1, d), lambda i, j: (i, 0, 0)),
                  pl.BlockSpec((d, n), lambda i, j: (0, 0))],
        out_specs=(pl.BlockSpec((None, tm, n), lambda i, j: (i, j, 0)),
                   pl.BlockSpec((None, tm, d), lambda i, j: (i, j, 0))),
        compiler_params=_cp(("parallel", "parallel")),
    )(x, g, scale, shift, w)


def _gate_outproj(mix, proj, gate_blk, w_out, x, gmod, name):
    b, s, wd = mix.shape
    d = w_out.shape[1]
    tm = min(s, 256)

    def body(mix_ref, gate_ref, w_ref, x_ref, gm_ref, xo_ref, o_ref):
        gt = gate_ref[...]
        y = (mix_ref[...] * (gt * _sigmoid(gt))).astype(BF16)
        o = _dot(y, w_ref[...])
        o_ref[...] = o.astype(BF16)
        xo_ref[...] = x_ref[...] + gm_ref[...] * o

    return pl.pallas_call(
        body, name=name, grid=(b, s // tm),
        out_shape=(jax.ShapeDtypeStruct((b, s, d), F32), jax.ShapeDtypeStruct((b, s, d), BF16)),
        in_specs=[pl.BlockSpec((None, tm, wd), lambda i, j: (i, j, 0)),
                  pl.BlockSpec((None, tm, wd), lambda i, j: (i, j, gate_blk)),
                  pl.BlockSpec((wd, d), lambda i, j: (0, 0)),
                  pl.BlockSpec((None, tm, d), lambda i, j: (i, j, 0)),
                  pl.BlockSpec((None, 1, d), lambda i, j: (i, 0, 0))],
        out_specs=(pl.BlockSpec((None, tm, d), lambda i, j: (i, j, 0)),
                   pl.BlockSpec((None, tm, d), lambda i, j: (i, j, 0))),
        compiler_params=_cp(("parallel", "parallel")),
    )(mix, proj, w_out, x, gmod)


def _final_loss(x, g, target, name):
    b, s, d = x.shape
    tm = min(s, 256)

    def body(x_ref, g_ref, t_ref, loss_ref, dx_ref, dg_ref):
        first = jnp.logical_and(pl.program_id(0) == 0, pl.program_id(1) == 0)

        @pl.when(first)
        def _():
            loss_ref[...] = jnp.zeros_like(loss_ref)
            dg_ref[...] = jnp.zeros_like(dg_ref)

        xv = x_ref[...]
        gv = g_ref[...]
        rstd = lax.rsqrt(jnp.mean(xv * xv, axis=-1, keepdims=True) + EPS)
        xhat = xv * rstd
        err = xhat * gv - t_ref[...]
        row = jnp.mean(err * err, axis=-1, keepdims=True)
        loss_ref[...] += 0.5 * jnp.sum(row, axis=0, keepdims=True)
        dy = err * (1.0 / d)
        dg_ref[...] += jnp.sum(dy * xhat, axis=0, keepdims=True)
        dxh = dy * gv
        dx_ref[...] = rstd * (dxh - xhat * jnp.mean(dxh * xhat, axis=-1, keepdims=True))

    return pl.pallas_call(
        body, name=name, grid=(b, s // tm),
        out_shape=(jax.ShapeDtypeStruct((1, LANE), F32), jax.ShapeDtypeStruct((b, s, d), F32),
                   jax.ShapeDtypeStruct((1, d), F32)),
        in_specs=[pl.BlockSpec((None, tm, d), lambda i, j: (i, j, 0)),
                  pl.BlockSpec((1, d), lambda i, j: (0, 0)),
                  pl.BlockSpec((None, tm, d), lambda i, j: (i, j, 0))],
        out_specs=(pl.BlockSpec((1, LANE), lambda i, j: (0, 0)),
                   pl.BlockSpec((None, tm, d), lambda i, j: (i, j, 0)),
                   pl.BlockSpec((1, d), lambda i, j: (0, 0))),
        compiler_params=_cp(("arbitrary", "arbitrary")),
    )(x, g, target)


def _bwd_out(dxo, gmod, o, mix, proj, gate_blk, w_out_t, name):
    b, s, d = dxo.shape
    wd = mix.shape[2]
    tm = min(s, 256)

    def body(dx_ref, gm_ref, o_ref, mix_ref, gate_ref, wt_ref, dmix_ref, dgate_ref, do_ref, y_ref, dgm_ref):
        @pl.when(pl.program_id(1) == 0)
        def _():
            dgm_ref[...] = jnp.zeros_like(dgm_ref)

        dx = dx_ref[...]
        dgm_ref[...] += jnp.sum(dx * o_ref[...].astype(F32), axis=0, keepdims=True)
        dob = (gm_ref[...] * dx).astype(BF16)
        do_ref[...] = dob
        dy = _dot(dob, wt_ref[...])
        gt = gate_ref[...]
        sg = _sigmoid(gt)
        silu = gt * sg
        mx = mix_ref[...]
        y_ref[...] = (mx * silu).astype(BF16)
        dmix_ref[...] = dy * silu
        dgate_ref[...] = (dy * mx * (sg * (1.0 + gt * (1.0 - sg)))).astype(BF16)

    row = lambda i, j: (i, j, 0)
    return pl.pallas_call(
        body, name=name, grid=(b, s // tm),
        out_shape=(jax.ShapeDtypeStruct((b, s, wd), F32), jax.ShapeDtypeStruct((b, s, wd), BF16),
                   jax.ShapeDtypeStruct((b, s, d), BF16), jax.ShapeDtypeStruct((b, s, wd), BF16),
                   jax.ShapeDtypeStruct((b, 1, d), F32)),
        in_specs=[pl.BlockSpec((None, tm, d), row),
                  pl.BlockSpec((None, 1, d), lambda i, j: (i, 0, 0)),
                  pl.BlockSpec((None, tm, d), row),
                  pl.BlockSpec((None, tm, wd), row),
                  pl.BlockSpec((None, tm, wd), lambda i, j: (i, j, gate_blk)),
                  pl.BlockSpec((d, wd), lambda i, j: (0, 0))],
        out_specs=(pl.BlockSpec((None, tm, wd), row), pl.BlockSpec((None, tm, wd), row),
                   pl.BlockSpec((None, tm, d), row), pl.BlockSpec((None, tm, wd), row),
                   pl.BlockSpec((None, 1, d), lambda i, j: (i, 0, 0))),
        compiler_params=_cp(("parallel", "arbitrary")),
    )(dxo, gmod, o, mix, proj, w_out_t)


def _bwd_in(dproj, w_in_t, x, g, scale, dxo, name):
    b, s, d = x.shape
    n = dproj.shape[2]
    tm = min(s, 256)

    def body(dp_ref, wt_ref, x_ref, g_ref, sc_ref, dxo_ref, dx_ref, dsh_ref, dsc_ref, dg_ref):
        @pl.when(jnp.logical_and(pl.program_id(0) == 0, pl.program_id(1) == 0))
        def _():
            dg_ref[...] = jnp.zeros_like(dg_ref)

        @pl.when(pl.program_id(1) == 0)
        def _():
            dsh_ref[...] = jnp.zeros_like(dsh_ref)
            dsc_ref[...] = jnp.zeros_like(dsc_ref)

        dh = _dot(dp_ref[...], wt_ref[...])
        xv = x_ref[...]
        gv = g_ref[...]
        one_sc = 1.0 + sc_ref[...]
        rstd = lax.rsqrt(jnp.mean(xv * xv, axis=-1, keepdims=True) + EPS)
        xhat = xv * rstd
        dsh_ref[...] += jnp.sum(dh, axis=0, keepdims=True)
        dsc_ref[...] += jnp.sum(dh * (xhat * gv), axis=0, keepdims=True)
        dhs = dh * one_sc
        dg_ref[...] += jnp.sum(dhs * xhat, axis=0, keepdims=True)
        dxh = dhs * gv
        dx_ref[...] = dxo_ref[...] + rstd * (dxh - xhat * jnp.mean(dxh * xhat, axis=-1, keepdims=True))

    row = lambda i, j: (i, j, 0)
    per_b = lambda i, j: (i, 0, 0)
    return pl.pallas_call(
        body, name=name, grid=(b, s // tm),
        out_shape=(jax.ShapeDtypeStruct((b, s, d), F32), jax.ShapeDtypeStruct((b, 1, d), F32),
                   jax.ShapeDtypeStruct((b, 1, d), F32), jax.ShapeDtypeStruct((1, d), F32)),
        in_specs=[pl.BlockSpec((None, tm, n), row),
                  pl.BlockSpec((n, d), lambda i, j: (0, 0)),
                  pl.BlockSpec((None, tm, d), row),
                  pl.BlockSpec((1, d), lambda i, j: (0, 0)),
                  pl.BlockSpec((None, 1, d), per_b),
                  pl.BlockSpec((None, tm, d), row)],
        out_specs=(pl.BlockSpec((None, tm, d), row), pl.BlockSpec((None, 1, d), per_b),
                   pl.BlockSpec((None, 1, d), per_b), pl.BlockSpec((1, d), lambda i, j: (0, 0))),
        compiler_params=_cp(("arbitrary", "arbitrary")),
    )(dproj, w_in_t, x, g, scale, dxo)


def _matmul_tn(a, bm, name):
    t, m = a.shape
    n = bm.shape[1]
    tk = next(c for c in (512, 256, 128) if t % c == 0)
    tn = n
    for cand in (1152, 1024, 896, 768, 640, 512, 384, 256, 128):
        if n % cand == 0:
            tn = cand
            break
    nk = t // tk

    def body(a_ref, b_ref, o_ref, acc_ref):
        k = pl.program_id(1)

        @pl.when(k == 0)
        def _():
            acc_ref[...] = jnp.zeros_like(acc_ref)

        acc_ref[...] += _dot_tn(a_ref[...], b_ref[...])

        @pl.when(k == nk - 1)
        def _():
            o_ref[...] = acc_ref[...]

    return pl.pallas_call(
        body, name=name, grid=(n // tn, nk),
        out_shape=jax.ShapeDtypeStruct((m, n), F32),
        in_specs=[pl.BlockSpec((tk, m), lambda j, k: (k, 0)),
                  pl.BlockSpec((tk, tn), lambda j, k: (k, j))],
        out_specs=pl.BlockSpec((m, tn), lambda j, k: (0, j)),
        scratch_shapes=[pltpu.VMEM((m, tn), F32)],
        compiler_params=_cp(("parallel", "arbitrary")),
    )(a, bm)


def _rel_buckets():
    qi = np.arange(BLK)[:, None]
    kj = np.arange(2 * BLK)[None, :]
    rel = qi - kj + BLK
    n = np.maximum(rel, 0)
    nf = np.maximum(n, 1).astype(np.float32)
    large = REL_MAX_EXACT + (np.log(nf / REL_MAX_EXACT) / math.log(REL_MAX_DIST / REL_MAX_EXACT)
                             * (REL_BUCKETS - REL_MAX_EXACT)).astype(np.int32)
    large = np.minimum(large, REL_BUCKETS - 1)
    bucket = np.where(n < REL_MAX_EXACT, n, large).astype(np.int32)
    valid = ((rel >= 0) & (rel < BLK)).astype(np.int32)
    return bucket, valid


def _swa_bias(rel_bias_t, bucket, valid, name):
    nh = rel_bias_t.shape[0]

    def body(rb_ref, bk_ref, vl_ref, o_ref):
        h = pl.program_id(0)
        bk = bk_ref[...]
        acc = jnp.zeros(bk.shape, F32)
        for i in range(REL_BUCKETS):
            acc = jnp.where(bk == i, rb_ref[h, i], acc)
        o_ref[...] = jnp.where(vl_ref[...] > 0, acc, NEG)

    return pl.pallas_call(
        body, name=name, grid=(nh,),
        out_shape=jax.ShapeDtypeStruct((nh, BLK, 2 * BLK), F32),
        in_specs=[pl.BlockSpec(memory_space=pltpu.SMEM),
                  pl.BlockSpec((BLK, 2 * BLK), lambda h: (0, 0)),
                  pl.BlockSpec((BLK, 2 * BLK), lambda h: (0, 0))],
        out_specs=pl.BlockSpec((None, BLK, 2 * BLK), lambda h: (h, 0, 0)),
        compiler_params=_cp(("arbitrary",)),
    )(rel_bias_t, bucket, valid)


def _swa_scores(n, qn, kc, kp, bias_ref):
    s_c = _dot_nt(qn, kc) * SCALE + bias_ref[:, BLK:]
    s_p = _dot_nt(qn, kp) * SCALE + bias_ref[:, :BLK]
    s_p = jnp.where(n > 0, s_p, NEG)
    return s_c, s_p


def _swa_fwd(q, k, v, bias, sinks, name):
    b, hq, s, hd = q.shape
    nb = s // BLK

    def body(sink_ref, q_ref, k_ref, v_ref, bias_ref, o_ref, l_ref):
        sink = sink_ref[pl.program_id(1)]

        def step(n, carry):
            r0 = pl.multiple_of(n * BLK, BLK)
            p0 = pl.multiple_of(jnp.maximum(n - 1, 0) * BLK, BLK)
            qn = q_ref[pl.ds(r0, BLK), :]
            s_c, s_p = _swa_scores(n, qn, k_ref[pl.ds(r0, BLK), :], k_ref[pl.ds(p0, BLK), :], bias_ref)
            m = jnp.maximum(jnp.maximum(jnp.max(s_c, axis=1, keepdims=True),
                                        jnp.max(s_p, axis=1, keepdims=True)), sink)
            e_c = jnp.exp(s_c - m)
            e_p = jnp.exp(s_p - m)
            den = jnp.sum(e_c, axis=1, keepdims=True) + jnp.sum(e_p, axis=1, keepdims=True) + jnp.exp(sink - m)
            inv = 1.0 / den
            o = _dot((e_c * inv).astype(BF16), v_ref[pl.ds(r0, BLK), :])
            o = o + _dot((e_p * inv).astype(BF16), v_ref[pl.ds(p0, BLK), :])
            o_ref[pl.ds(r0, BLK), :] = o
            l_ref[pl.ds(r0, BLK), :] = m + jnp.log(den)
            return carry

        lax.fori_loop(0, nb, step, 0)

    qspec = pl.BlockSpec((None, None, s, hd), lambda i, h: (i, h, 0, 0))
    kspec = pl.BlockSpec((None, None, s, hd), lambda i, h: (i, h // A_GROUP, 0, 0))
    return pl.pallas_call(
        body, name=name, grid=(b, hq),
        out_shape=(jax.ShapeDtypeStruct((b, hq, s, hd), F32), jax.ShapeDtypeStruct((b, hq, s, 1), F32)),
        in_specs=[pl.BlockSpec(memory_space=pltpu.SMEM), qspec, kspec, kspec,
                  pl.BlockSpec((None, BLK, 2 * BLK), lambda i, h: (h, 0, 0))],
        out_specs=(qspec, pl.BlockSpec((None, None, s, 1), lambda i, h: (i, h, 0, 0))),
        compiler_params=_cp(("parallel", "parallel")),
    )(sinks, q, k, v, bias)


def _swa_bwd(q, k, v, bias, sinks, do, lse, name):
    b, hq, s, hd = q.shape
    hkv = k.shape[1]
    nb = s // BLK

    def body(sink_ref, q_ref, k_ref, v_ref, bias_ref, do_ref, l_ref,
             dq_ref, dk_ref, dv_ref, db_ref, dsk_ref):
        g = pl.program_id(2)
        sink = sink_ref[pl.program_id(1) * A_GROUP + g]

        @pl.when(g == 0)
        def _():
            dk_ref[...] = jnp.zeros_like(dk_ref)
            dv_ref[...] = jnp.zeros_like(dv_ref)

        db_ref[...] = jnp.zeros_like(db_ref)

        def step(n, dsink):
            r0 = pl.multiple_of(n * BLK, BLK)
            p0 = pl.multiple_of(jnp.maximum(n - 1, 0) * BLK, BLK)
            qn = q_ref[pl.ds(r0, BLK), :]
            kc = k_ref[pl.ds(r0, BLK), :]
            kp = k_ref[pl.ds(p0, BLK), :]
            s_c, s_p = _swa_scores(n, qn, kc, kp, bias_ref)
            ln = l_ref[pl.ds(r0, BLK), :]
            p_c = jnp.exp(s_c - ln)
            p_p = jnp.exp(s_p - ln)
            p_s = jnp.exp(sink - ln)
            dob = do_ref[pl.ds(r0, BLK), :].astype(BF16)
            dp_c = _dot_nt(dob, v_ref[pl.ds(r0, BLK), :])
            dp_p = _dot_nt(dob, v_ref[pl.ds(p0, BLK), :])
            delta = jnp.sum(p_c * dp_c, axis=1, keepdims=True) + jnp.sum(p_p * dp_p, axis=1, keepdims=True)
            ds_c = p_c * (dp_c - delta)
            ds_p = p_p * (dp_p - delta)
            db_ref[:, BLK:] += ds_c
            db_ref[:, :BLK] += ds_p
            dsb_c = ds_c.astype(BF16)
            dsb_p = ds_p.astype(BF16)
            dq_ref[pl.ds(r0, BLK), :] = (_dot(dsb_c, kc) + _dot(dsb_p, kp)) * SCALE
            dk_ref[pl.ds(r0, BLK), :] += _dot_tn(dsb_c, qn) * SCALE
            dk_ref[pl.ds(p0, BLK), :] += _dot_tn(dsb_p, qn) * SCALE
            dv_ref[pl.ds(r0, BLK), :] += _dot_tn(p_c.astype(BF16), dob)
            dv_ref[pl.ds(p0, BLK), :] += _dot_tn(p_p.astype(BF16), dob)
            return dsink - jnp.sum(p_s * delta, axis=0, keepdims=True)

        dsink = lax.fori_loop(0, nb, step, jnp.zeros((1, 1), F32))
        dsk_ref[...] = jnp.broadcast_to(dsink, dsk_ref.shape)

    qspec = pl.BlockSpec((None, None, s, hd), lambda i, kv, g: (i, kv * A_GROUP + g, 0, 0))
    kspec = pl.BlockSpec((None, None, s, hd), lambda i, kv, g: (i, kv, 0, 0))
    return pl.pallas_call(
        body, name=name, grid=(b, hkv, A_GROUP),
        out_shape=(jax.ShapeDtypeStruct((b, hq, s, hd), F32), jax.ShapeDtypeStruct((b, hkv, s, hd), F32),
                   jax.ShapeDtypeStruct((b, hkv, s, hd), F32), jax.ShapeDtypeStruct((b, hq, BLK, 2 * BLK), F32),
                   jax.ShapeDtypeStruct((b, hq, 1, LANE), F32)),
        in_specs=[pl.BlockSpec(memory_space=pltpu.SMEM), qspec, kspec, kspec,
                  pl.BlockSpec((None, BLK, 2 * BLK), lambda i, kv, g: (kv * A_GROUP + g, 0, 0)),
                  qspec,
                  pl.BlockSpec((None, None, s, 1), lambda i, kv, g: (i, kv * A_GROUP + g, 0, 0))],
        out_specs=(qspec, kspec, kspec,
                   pl.BlockSpec((None, None, BLK, 2 * BLK), lambda i, kv, g: (i, kv * A_GROUP + g, 0, 0)),
                   pl.BlockSpec((None, None, 1, LANE), lambda i, kv, g: (i, kv * A_GROUP + g, 0, 0))),
        compiler_params=_cp(("parallel", "parallel", "arbitrary")),
    )(sinks, q, k, v, bias, do, lse)


def _swa_small_grads(db, dsk, bucket, name):
    b, nh = db.shape[0], db.shape[1]

    def body(db_ref, dsk_ref, bk_ref, gb_ref, gs_ref):
        acc = db_ref[0]
        sk = dsk_ref[0]
        for i in range(1, b):
            acc = acc + db_ref[i]
            sk = sk + dsk_ref[i]
        gs_ref[...] = sk
        bk = bk_ref[...]
        for i in range(REL_BUCKETS):
            part = jnp.sum(jnp.where(bk == i, acc, 0.0), axis=1, keepdims=True)
            tot = jnp.sum(part, axis=0, keepdims=True)
            gb_ref[i:i + 1, :] = jnp.broadcast_to(tot, (1, LANE))

    return pl.pallas_call(
        body, name=name, grid=(nh,),
        out_shape=(jax.ShapeDtypeStruct((nh, REL_BUCKETS, LANE), F32), jax.ShapeDtypeStruct((nh, 1, LANE), F32)),
        in_specs=[pl.BlockSpec((b, None, BLK, 2 * BLK), lambda h: (0, h, 0, 0)),
                  pl.BlockSpec((b, None, 1, LANE), lambda h: (0, h, 0, 0)),
                  pl.BlockSpec((BLK, 2 * BLK), lambda h: (0, 0))],
        out_specs=(pl.BlockSpec((None, REL_BUCKETS, LANE), lambda h: (h, 0, 0)),
                   pl.BlockSpec((None, 1, LANE), lambda h: (h, 0, 0))),
        compiler_params=_cp(("parallel",)),
    )(db, dsk, bucket)


def _log_sigmoid(z):
    return jnp.minimum(z, 0.0) - jnp.log(1.0 + jnp.exp(-jnp.abs(z)))


def _fox_decay(z, bf, name):
    b, s, w = z.shape
    nb = s // BLK

    def body(z_ref, bf_ref, f_ref):
        r = lax.broadcasted_iota(jnp.int32, (BLK, BLK), 0)
        c = lax.broadcasted_iota(jnp.int32, (BLK, BLK), 1)
        tri = (c <= r).astype(F32)

        def step(n, carry):
            r0 = pl.multiple_of(n * BLK, BLK)
            lf = _log_sigmoid(z_ref[pl.ds(r0, BLK), :] + bf_ref[...])
            f_ref[pl.ds(r0, BLK), :] = jnp.dot(tri, lf, precision=HI, preferred_element_type=F32) + carry
            return carry + jnp.sum(lf, axis=0, keepdims=True)

        lax.fori_loop(0, nb, step, jnp.zeros((1, w), F32))

    spec = pl.BlockSpec((None, s, w), lambda i: (i, 0, 0))
    return pl.pallas_call(
        body, name=name, grid=(b,), out_shape=jax.ShapeDtypeStruct((b, s, w), F32),
        in_specs=[spec, pl.BlockSpec((1, w), lambda i: (0, 0))], out_specs=spec,
        compiler_params=_cp(("parallel",)),
    )(z, bf)


def _fox_dgate(df, z, bf, nheads, name):
    b, s, w = z.shape
    nb = s // BLK

    def body(df_ref, z_ref, bf_ref, dz_ref, dbf_ref):
        @pl.when(pl.program_id(0) == 0)
        def _():
            dbf_ref[...] = jnp.zeros_like(dbf_ref)

        r = lax.broadcasted_iota(jnp.int32, (BLK, BLK), 0)
        c = lax.broadcasted_iota(jnp.int32, (BLK, BLK), 1)
        tri = (c >= r).astype(F32)
        lane = lax.broadcasted_iota(jnp.int32, (BLK, w), 1)

        def step(i, carry):
            tail, dbf = carry
            r0 = pl.multiple_of((nb - 1 - i) * BLK, BLK)
            dfb = df_ref[pl.ds(r0, BLK), :]
            dlf = jnp.dot(tri, dfb, precision=HI, preferred_element_type=F32) + tail
            dz = jnp.where(lane < nheads, dlf * _sigmoid(-(z_ref[pl.ds(r0, BLK), :] + bf_ref[...])), 0.0)
            dz_ref[pl.ds(r0, BLK), :] = dz
            return tail + jnp.sum(dfb, axis=0, keepdims=True), dbf + jnp.sum(dz, axis=0, keepdims=True)

        zero = jnp.zeros((1, w), F32)
        _, dbf = lax.fori_loop(0, nb, step, (zero, zero))
        dbf_ref[...] += dbf

    spec = pl.BlockSpec((None, s, w), lambda i: (i, 0, 0))
    one = pl.BlockSpec((1, w), lambda i: (0, 0))
    return pl.pallas_call(
        body, name=name, grid=(b,),
        out_shape=(jax.ShapeDtypeStruct((b, s, w), F32), jax.ShapeDtypeStruct((1, w), F32)),
        in_specs=[spec, spec, one], out_specs=(spec, one),
        compiler_params=_cp(("arbitrary",)),
    )(df, z, bf)


def _fox_scores(n, j, qn, kj, fq, fk):
    s = _dot_nt(qn, kj) * SCALE + (fq - fk)
    rowpos = n * BLK + lax.broadcasted_iota(jnp.int32, (BLK, BLK), 0)
    colpos = j * BLK + lax.broadcasted_iota(jnp.int32, (BLK, BLK), 1)
    return jnp.where(colpos <= rowpos, s, NEG)


def _fox_fwd(q, k, v, fcol, frow, name):
    b, nh, s, hd = q.shape
    nb = s // BLK

    def body(q_ref, k_ref, v_ref, fc_ref, fr_ref, o_ref, l_ref):
        def qstep(n, carry):
            r0 = pl.multiple_of(n * BLK, BLK)
            qn = q_ref[pl.ds(r0, BLK), :]
            fq = fc_ref[pl.ds(r0, BLK), :]

            def kstep(j, st):
                m, l, acc = st
                c0 = pl.multiple_of(j * BLK, BLK)
                sc = _fox_scores(n, j, qn, k_ref[pl.ds(c0, BLK), :], fq, fr_ref[j])
                m_new = jnp.maximum(m, jnp.max(sc, axis=1, keepdims=True))
                alpha = jnp.exp(m - m_new)
                p = jnp.exp(sc - m_new)
                l = alpha * l + jnp.sum(p, axis=1, keepdims=True)
                acc = alpha * acc + _dot(p.astype(BF16), v_ref[pl.ds(c0, BLK), :])
                return m_new, l, acc

            init = (jnp.full((BLK, 1), NEG, F32), jnp.zeros((BLK, 1), F32), jnp.zeros((BLK, hd), F32))
            m, l, acc = lax.fori_loop(0, n + 1, kstep, init)
            o_ref[pl.ds(r0, BLK), :] = acc / l
            l_ref[pl.ds(r0, BLK), :] = m + jnp.log(l)
            return carry

        lax.fori_loop(0, nb, qstep, 0)

    spec = pl.BlockSpec((None, None, s, hd), lambda i, h: (i, h, 0, 0))
    col = pl.BlockSpec((None, None, s, 1), lambda i, h: (i, h, 0, 0))
    rowspec = pl.BlockSpec((None, None, nb, 1, BLK), lambda i, h: (i, h, 0, 0, 0))
    return pl.pallas_call(
        body, name=name, grid=(b, nh),
        out_shape=(jax.ShapeDtypeStruct((b, nh, s, hd), F32), jax.ShapeDtypeStruct((b, nh, s, 1), F32)),
        in_specs=[spec, spec, spec, col, rowspec], out_specs=(spec, col),
        compiler_params=_cp(("parallel", "parallel")),
    )(q, k, v, fcol, frow)


def _fox_bwd(q, k, v, fcol, frow, do, lse, name):
    b, nh, s, hd = q.shape
    nb = s // BLK

    def body(q_ref, k_ref, v_ref, fc_ref, fr_ref, do_ref, l_ref, dq_ref, dk_ref, dv_ref, dfr_ref, p_ref, dp_ref):
        dk_ref[...] = jnp.zeros_like(dk_ref)
        dv_ref[...] = jnp.zeros_like(dv_ref)
        dfr_ref[...] = jnp.zeros_like(dfr_ref)

        def nstep(n, carry):
            r0 = pl.multiple_of(n * BLK, BLK)
            qn = q_ref[pl.ds(r0, BLK), :]
            dob = do_ref[pl.ds(r0, BLK), :].astype(BF16)
            fq = fc_ref[pl.ds(r0, BLK), :]
            ln = l_ref[pl.ds(r0, BLK), :]

            def probs(j, delta):
                c0 = pl.multiple_of(j * BLK, BLK)
                p = jnp.exp(_fox_scores(n, j, qn, k_ref[pl.ds(c0, BLK), :], fq, fr_ref[j]) - ln)
                dp = _dot_nt(dob, v_ref[pl.ds(c0, BLK), :])
                p_ref[j] = p
                dp_ref[j] = dp
                return delta + jnp.sum(p * dp, axis=1, keepdims=True)

            delta = lax.fori_loop(0, n + 1, probs, jnp.zeros((BLK, 1), F32))

            def grads(j, dq):
                c0 = pl.multiple_of(j * BLK, BLK)
                p = p_ref[j]
                ds = p * (dp_ref[j] - delta)
                dsb = ds.astype(BF16)
                dk_ref[pl.ds(c0, BLK), :] += _dot_tn(dsb, qn) * SCALE
                dv_ref[pl.ds(c0, BLK), :] += _dot_tn(p.astype(BF16), dob)
                dfr_ref[j] -= jnp.sum(ds, axis=0, keepdims=True)
                return dq + _dot(dsb, k_ref[pl.ds(c0, BLK), :])

            dq = lax.fori_loop(0, n + 1, grads, jnp.zeros((BLK, hd), F32))
            dq_ref[pl.ds(r0, BLK), :] = dq * SCALE
            return carry

        lax.fori_loop(0, nb, nstep, 0)

    spec = pl.BlockSpec((None, None, s, hd), lambda i, h: (i, h, 0, 0))
    col = pl.BlockSpec((None, None, s, 1), lambda i, h: (i, h, 0, 0))
    rowspec = pl.BlockSpec((None, None, nb, 1, BLK), lambda i, h: (i, h, 0, 0, 0))
    shp = jax.ShapeDtypeStruct((b, nh, s, hd), F32)
    return pl.pallas_call(
        body, name=name, grid=(b, nh),
        out_shape=(shp, shp, shp, jax.ShapeDtypeStruct((b, nh, nb, 1, BLK), F32)),
        in_specs=[spec, spec, spec, col, rowspec, spec, col],
        out_specs=(spec, spec, spec, rowspec),
        scratch_shapes=[pltpu.VMEM((nb, BLK, BLK), F32), pltpu.VMEM((nb, BLK, BLK), F32)],
        compiler_params=_cp(("parallel", "parallel")),
    )(q, k, v, fcol, frow, do, lse)


def _expm1(x):
    poly = x * (1.0 + x * (1.0 / 2.0) * (1.0 + x * (1.0 / 3.0) * (1.0 + x * (1.0 / 4.0) * (1.0 + x * (1.0 / 5.0)
                                                                                          * (1.0 + x * (1.0 / 6.0))))))
    return jnp.where(x > -0.1, poly, jnp.exp(x) - 1.0)


def _softplus(z):
    return jnp.maximum(z, 0.0) + jnp.log(1.0 + jnp.exp(-jnp.abs(z)))


def _scan_up(a, u, carry, row):
    tc = a.shape[0]
    d = 1
    while d < tc:
        keep = row >= d
        a_sh = jnp.where(keep, pltpu.roll(a, d, 0), 1.0)
        u_sh = jnp.where(keep, pltpu.roll(u, d, 0), 0.0)
        u = a * u_sh + u
        a = a * a_sh
        d *= 2
    return u + a * carry


def _scan_down(bnext, g, carry, row):
    tc = g.shape[0]
    a, u = bnext, g
    d = 1
    while d < tc:
        keep = row < tc - d
        a_sh = jnp.where(keep, pltpu.roll(a, tc - d, 0), 1.0)
        u_sh = jnp.where(keep, pltpu.roll(u, tc - d, 0), 0.0)
        u = a * u_sh + u
        a = a * a_sh
        d *= 2
    return u + a * carry


def _pick_row(val, row, which):
    return jnp.sum(jnp.where(row == which, val, 0.0), axis=0, keepdims=True)


def _lru_gates(xpad_ref, t0, tc, cw_ref, cb_ref, wa, ba_ref, wx, bx_ref, sp):
    xw = xpad_ref[pl.ds(t0, tc + SUBLANE), :]
    xc = cb_ref[...]
    for j in range(CONV_WIDTH):
        sh = CONV_WIDTH - 1 - j
        xs = xw if sh == 0 else pltpu.roll(xw, sh, 0)
        xc = xc + xs[SUBLANE:, :] * cw_ref[j:j + 1, :]
    xcb = xc.astype(BF16)
    r = _sigmoid(_dot(xcb, wa) + ba_ref[...])
    i = _sigmoid(_dot(xcb, wx) + bx_ref[...])
    la = -LRU_C * r * sp
    return xc, r, i, la


def _lru_specs(s, cb):
    seq = lambda bi, ni: (bi, 0, ni)
    return dict(
        seq=pl.BlockSpec((None, s, cb), seq),
        cw=pl.BlockSpec((CONV_WIDTH, cb), lambda bi, ni: (0, ni)),
        vec=pl.BlockSpec((1, cb), lambda bi, ni: (0, ni)),
        wblk=pl.BlockSpec((None, cb, cb), lambda bi, ni: (ni, 0, 0)),
    )


def _lru_fwd(proj, cw, cb_, wa, ba, wx, bx, lam, name):
    b, s, _ = proj.shape
    nblk, cb, _ = wa.shape
    tc = min(s, SCAN_CHUNK)
    nc = s // tc

    def body(x_ref, cw_ref, cb_ref, wa_ref, ba_ref, wx_ref, bx_ref, lam_ref, hs_ref, xpad_ref):
        xpad_ref[0:SUBLANE, :] = jnp.zeros((SUBLANE, cb), F32)
        xpad_ref[SUBLANE:, :] = x_ref[...]
        wa_b = wa_ref[...].astype(BF16)
        wx_b = wx_ref[...].astype(BF16)
        sp = _softplus(-lam_ref[...])
        row = lax.broadcasted_iota(jnp.int32, (tc, cb), 0)

        def chunk(ci, carry):
            t0 = pl.multiple_of(ci * tc, tc)
            xc, r, i, la = _lru_gates(xpad_ref, t0, tc, cw_ref, cb_ref, wa_b, ba_ref, wx_b, bx_ref, sp)
            a = jnp.exp(la)
            u = jnp.sqrt(-_expm1(2.0 * la)) * (i * xc)
            h = _scan_up(a, u, carry, row)
            hs_ref[pl.ds(t0, tc), :] = h
            return _pick_row(h, row, tc - 1)

        lax.fori_loop(0, nc, chunk, jnp.zeros((1, cb), F32))

    sp_ = _lru_specs(s, cb)
    return pl.pallas_call(
        body, name=name, grid=(b, nblk),
        out_shape=jax.ShapeDtypeStruct((b, s, nblk * cb), F32),
        in_specs=[sp_["seq"], sp_["cw"], sp_["vec"], sp_["wblk"], sp_["vec"], sp_["wblk"], sp_["vec"], sp_["vec"]],
        out_specs=sp_["seq"],
        scratch_shapes=[pltpu.VMEM((s + SUBLANE, cb), F32)],
        compiler_params=_cp(("parallel", "parallel")),
    )(proj, cw, cb_, wa, ba, wx, bx, lam)


def _lru_bwd(proj, hs, dhs, cw, cb_, wa, ba, wx, bx, lam, name):
    b, s, _ = proj.shape
    nblk, cb, _ = wa.shape
    tc = min(s, SCAN_CHUNK)
    nc = s // tc

    def body(x_ref, hs_ref, dhs_ref, cw_ref, cb_ref, wa_ref, ba_ref, wx_ref, bx_ref, lam_ref,
             dx_ref, dcw_ref, dcb_ref, dwa_ref, dba_ref, dwx_ref, dbx_ref, dlam_ref,
             xpad_ref, hpad_ref, dcpad_ref, xc_ref, r_ref, i_ref, a_ref):
        @pl.when(pl.program_id(1) == 0)
        def _():
            for ref in (dcw_ref, dcb_ref, dwa_ref, dba_ref, dwx_ref, dbx_ref, dlam_ref):
                ref[...] = jnp.zeros_like(ref)

        zeros8 = jnp.zeros((SUBLANE, cb), F32)
        xpad_ref[0:SUBLANE, :] = zeros8
        xpad_ref[SUBLANE:, :] = x_ref[...]
        hpad_ref[0:SUBLANE, :] = zeros8
        hpad_ref[SUBLANE:, :] = hs_ref[...]
        dcpad_ref[s:s + SUBLANE, :] = zeros8
        wa_b = wa_ref[...].astype(BF16)
        wx_b = wx_ref[...].astype(BF16)
        lam_v = lam_ref[...]
        sp = _softplus(-lam_v)
        dsp_dlam = -_sigmoid(-lam_v)
        row = lax.broadcasted_iota(jnp.int32, (tc, cb), 0)

        def recompute(ci, carry):
            t0 = pl.multiple_of(ci * tc, tc)
            xc, r, i, la = _lru_gates(xpad_ref, t0, tc, cw_ref, cb_ref, wa_b, ba_ref, wx_b, bx_ref, sp)
            xc_ref[pl.ds(t0, tc), :] = xc
            r_ref[pl.ds(t0, tc), :] = r
            i_ref[pl.ds(t0, tc), :] = i
            a_ref[pl.ds(t0, tc), :] = jnp.exp(la)
            return carry

        lax.fori_loop(0, nc, recompute, 0)

        def adjoint(k, carry):
            g_next, a_first_next = carry
            t0 = pl.multiple_of((nc - 1 - k) * tc, tc)
            a = a_ref[pl.ds(t0, tc), :]
            a_next = jnp.where(row == tc - 1, a_first_next, pltpu.roll(a, tc - 1, 0))
            gg = _scan_down(a_next, dhs_ref[pl.ds(t0, tc), :], g_next, row)
            h_prev = pltpu.roll(hpad_ref[pl.ds(t0, tc + SUBLANE), :], 1, 0)[SUBLANE:, :]
            xc = xc_ref[pl.ds(t0, tc), :]
            r = r_ref[pl.ds(t0, tc), :]
            i = i_ref[pl.ds(t0, tc), :]
            mult = jnp.sqrt(-_expm1(-2.0 * LRU_C * r * sp))
            d_mult = gg * i * xc
            d_i = gg * mult * xc
            d_xc = gg * mult * i
            d_la = gg * h_prev * a - d_mult * (a * a) / mult
            d_zr = (d_la * (-LRU_C * sp)) * r * (1.0 - r)
            d_zi = d_i * i * (1.0 - i)
            dlam_ref[...] += jnp.sum(d_la * (-LRU_C * r), axis=0, keepdims=True) * dsp_dlam
            dzr_b = d_zr.astype(BF16)
            dzi_b = d_zi.astype(BF16)
            xcb = xc.astype(BF16)
            d_xc = d_xc + _dot_nt(dzr_b, wa_b) + _dot_nt(dzi_b, wx_b)
            dwa_ref[...] += _dot_tn(xcb, dzr_b)
            dwx_ref[...] += _dot_tn(xcb, dzi_b)
            dba_ref[...] += jnp.sum(d_zr, axis=0, keepdims=True)
            dbx_ref[...] += jnp.sum(d_zi, axis=0, keepdims=True)
            dcb_ref[...] += jnp.sum(d_xc, axis=0, keepdims=True)
            dcpad_ref[pl.ds(t0, tc), :] = d_xc
            return _pick_row(gg, row, 0), _pick_row(a, row, 0)

        zero = jnp.zeros((1, cb), F32)
        lax.fori_loop(0, nc, adjoint, (zero, zero))

        def conv_back(ci, carry):
            t0 = pl.multiple_of(ci * tc, tc)
            dw = dcpad_ref[pl.ds(t0, tc + SUBLANE), :]
            xw = xpad_ref[pl.ds(t0, tc + SUBLANE), :]
            d_xc = dw[:tc, :]
            dxr = jnp.zeros((tc, cb), F32)
            for j in range(CONV_WIDTH):
                sh = CONV_WIDTH - 1 - j
                dsh = dw if sh == 0 else pltpu.roll(dw, tc + SUBLANE - sh, 0)
                dxr = dxr + dsh[:tc, :] * cw_ref[j:j + 1, :]
                xs = xw if sh == 0 else pltpu.roll(xw, sh, 0)
                dcw_ref[j:j + 1, :] += jnp.sum(d_xc * xs[SUBLANE:, :], axis=0, keepdims=True)
            dx_ref[pl.ds(t0, tc), :] = dxr.astype(BF16)
            return carry

        lax.fori_loop(0, nc, conv_back, 0)

    seq = lambda ni, bi: (bi, 0, ni)
    seqspec = pl.BlockSpec((None, s, cb), seq)
    cwspec = pl.BlockSpec((CONV_WIDTH, cb), lambda ni, bi: (0, ni))
    vec = pl.BlockSpec((1, cb), lambda ni, bi: (0, ni))
    wblk = pl.BlockSpec((None, cb, cb), lambda ni, bi: (ni, 0, 0))
    w = nblk * cb
    return pl.pallas_call(
        body, name=name, grid=(nblk, b),
        out_shape=(jax.ShapeDtypeStruct((b, s, w), BF16), jax.ShapeDtypeStruct((CONV_WIDTH, w), F32),
                   jax.ShapeDtypeStruct((1, w), F32), jax.ShapeDtypeStruct((nblk, cb, cb), F32),
                   jax.ShapeDtypeStruct((1, w), F32), jax.ShapeDtypeStruct((nblk, cb, cb), F32),
                   jax.ShapeDtypeStruct((1, w), F32), jax.ShapeDtypeStruct((1, w), F32)),
        in_specs=[seqspec, seqspec, seqspec, cwspec, vec, wblk, vec, wblk, vec, vec],
        out_specs=(seqspec, cwspec, vec, wblk, vec, wblk, vec, vec),
        scratch_shapes=[pltpu.VMEM((s + SUBLANE, cb), F32)] * 3 + [pltpu.VMEM((s, cb), F32)] * 4,
        compiler_params=_cp(("parallel", "arbitrary")),
    )(proj, hs, dhs, cw, cb_, wa, ba, wx, bx, lam)


def _adamw(w, g, m, v, name):
    shape = w.shape
    total = int(np.prod(shape))
    cols = 1024
    rows = -(-total // cols)
    rows = -(-rows // SUBLANE) * SUBLANE
    tr = _row_tile(rows, 512)
    pad = rows * cols - total

    def flat(a):
        a = a.reshape(-1)
        if pad:
            a = jnp.pad(a, (0, pad))
        return a.reshape(rows, cols)

    c1 = 1.0 - ADAM_B1 ** ADAM_STEP
    c2 = 1.0 - ADAM_B2 ** ADAM_STEP

    def body(w_ref, g_ref, m_ref, v_ref, d_ref, nm_ref, nv_ref):
        gv = g_ref[...]
        nm = ADAM_B1 * m_ref[...] + (1.0 - ADAM_B1) * gv
        nv = ADAM_B2 * v_ref[...] + (1.0 - ADAM_B2) * (gv * gv)
        nm_ref[...] = nm
        nv_ref[...] = nv
        d_ref[...] = -ADAM_LR * ((nm / c1) / (jnp.sqrt(nv / c2) + ADAM_EPS) + ADAM_WD * w_ref[...])

    spec = pl.BlockSpec((tr, cols), lambda i: (i, 0))
    shp = jax.ShapeDtypeStruct((rows, cols), F32)
    outs = pl.pallas_call(
        body, name=name, grid=(rows // tr,), out_shape=(shp, shp, shp),
        in_specs=[spec] * 4, out_specs=(spec,) * 3,
        compiler_params=_cp(("parallel",)),
    )(flat(w), flat(g), flat(m), flat(v))
    return tuple(o.reshape(-1)[:total].reshape(shape) for o in outs)


def _to_heads(t, nh):
    b, s, _ = t.shape
    return t.reshape(b, s, nh, HEAD_DIM).transpose(0, 2, 1, 3)


def _from_heads(t):
    b, nh, s, hd = t.shape
    return t.transpose(0, 2, 1, 3).reshape(b, s, nh * hd)


def _pad_rows(a, mult):
    r = a.shape[0]
    p = (-r) % mult
    return jnp.pad(a, ((0, p), (0, 0))) if p else a


def kernel(x, c, rel_bias, norm_g, ada_w, ada_b, attn_w_in, attn_sinks, attn_b_f, attn_w_out, lru_w_in, lru_conv_w, lru_conv_b, lru_w_a, lru_b_a, lru_w_x, lru_b_x, lru_lambda, lru_w_out, final_g, loss_target, m_rel_bias, m_norm_g, m_ada_w, m_ada_b, m_attn_w_in, m_attn_sinks, m_attn_b_f, m_attn_w_out, m_lru_w_in, m_lru_conv_w, m_lru_conv_b, m_lru_w_a, m_lru_b_a, m_lru_w_x, m_lru_b_x, m_lru_lambda, m_lru_w_out, m_final_g, v_rel_bias, v_norm_g, v_ada_w, v_ada_b, v_attn_w_in, v_attn_sinks, v_attn_b_f, v_attn_w_out, v_lru_w_in, v_lru_conv_w, v_lru_conv_b, v_lru_w_a, v_lru_b_a, v_lru_w_x, v_lru_b_x, v_lru_lambda, v_lru_w_out, v_final_g):
    bl, s, d = x.shape
    ix, iy, ic = lax.axis_index("x"), lax.axis_index("y"), lax.axis_index("c")
    chip = 2 * ix + iy
    me = 2 * chip + ic
    nb = s // BLK
    aw = A_Q_HEADS * HEAD_DIM
    akv = A_KV_HEADS * HEAD_DIM
    bw = B_HEADS * HEAD_DIM
    mixw = aw + bw
    qkv_w = aw + 2 * akv + 3 * bw
    n_in = attn_w_in.shape[2] * N_CHIP
    lw = lru_lambda.shape[1] * N_CHIP
    n0 = mixw + qkv_w + LANE

    rows_pad = -(-bl // SUBLANE) * SUBLANE
    c_all = _all_gather8(_pad_rows(c, SUBLANE), "gather_c", pltpu.VMEM)
    c_all = c_all.reshape(N_DEV, rows_pad, d)[:, :bl].reshape(N_DEV * bl, d)
    ncol = ada_w.shape[2]
    ada_w_l = lax.dynamic_index_in_dim(ada_w, ic, 0, keepdims=False)
    ada_b_l = lax.dynamic_slice(ada_b, (ic, chip * ncol), (1, ncol))
    mod_part = _ada_fwd(c_all, ada_w_l, ada_b_l, "ada_fwd")
    mod_all = _all_gather8(_pad_rows(mod_part, SUBLANE), "gather_mod", pltpu.VMEM)
    mrows = -(-(N_DEV * bl) // SUBLANE) * SUBLANE
    mod_all = mod_all.reshape(N_CHIP, 2, mrows, ncol)[:, :, :N_DEV * bl]
    mod_all = mod_all.transpose(1, 2, 0, 3).reshape(2, N_DEV * bl, N_CHIP * ncol)
    mod = lax.dynamic_slice_in_dim(mod_all, me * bl, bl, axis=1)
    shift = [mod[l, :, 0:d].reshape(bl, 1, d) for l in range(2)]
    scale = [mod[l, :, d:2 * d].reshape(bl, 1, d) for l in range(2)]
    gmod = [mod[l, :, 2 * d:3 * d].reshape(bl, 1, d) for l in range(2)]

    big = [attn_w_in[0], attn_w_out[0], lru_w_in[0], lru_w_out[0]]
    big_sizes = [int(np.prod(a.shape)) for a in big]
    big_total = sum(big_sizes)
    half_rows = -(-(-(-big_total // 2) // 1024) // 16) * 16
    half_len = half_rows * 1024

    def pack_halves(parts, dtype):
        flat = jnp.concatenate([p.reshape(-1).astype(dtype) for p in parts])
        flat = jnp.pad(flat, (0, 2 * half_len - big_total))
        return flat.reshape(2, half_rows, 1024)

    def unpack_chip(flat):
        outs, off = [], 0
        for a, n in zip(big, big_sizes):
            outs.append(flat[off:off + n].reshape(a.shape))
            off += n
        return outs

    my_half = lax.dynamic_index_in_dim(pack_halves(big, BF16), ic, 0, keepdims=False)
    gathered = _all_gather8(my_half, "gather_weights", pltpu.HBM).reshape(N_CHIP, 2 * half_len)
    per_chip = [unpack_chip(gathered[k]) for k in range(N_CHIP)]
    w_in0 = jnp.concatenate([per_chip[k][0] for k in range(N_CHIP)], axis=1)
    w_out0 = jnp.concatenate([per_chip[k][1] for k in range(N_CHIP)], axis=0)
    w_in1 = jnp.concatenate([per_chip[k][2] for k in range(N_CHIP)], axis=1)
    w_out1 = jnp.concatenate([per_chip[k][3] for k in range(N_CHIP)], axis=0)
    w_cat0 = jnp.concatenate([w_in0[:, qkv_w + B_HEADS:], w_in0[:, :qkv_w + B_HEADS],
                              jnp.zeros((d, n0 - n_in), BF16)], axis=1)

    proj0, h0 = _norm_proj(x, norm_g[0:1], scale[0], shift[0], w_cat0, "norm_proj0")
    o_a = mixw
    aq = _to_heads(proj0[:, :, o_a:o_a + aw].astype(BF16), A_Q_HEADS)
    ak = _to_heads(proj0[:, :, o_a + aw:o_a + aw + akv].astype(BF16), A_KV_HEADS)
    av = _to_heads(proj0[:, :, o_a + aw + akv:o_a + aw + 2 * akv].astype(BF16), A_KV_HEADS)
    o_b = o_a + aw + 2 * akv
    bq = _to_heads(proj0[:, :, o_b:o_b + bw].astype(BF16), B_HEADS)
    bk = _to_heads(proj0[:, :, o_b + bw:o_b + 2 * bw].astype(BF16), B_HEADS)
    bv = _to_heads(proj0[:, :, o_b + 2 * bw:o_b + 3 * bw].astype(BF16), B_HEADS)
    zf = proj0[:, :, o_b + 3 * bw:]
    bucket_np, valid_np = _rel_buckets()
    bucket = jnp.asarray(bucket_np)
    bias = _swa_bias(rel_bias.T, bucket, jnp.asarray(valid_np), "swa_bias")
    sinks = attn_sinks[0]
    a_out, a_lse = _swa_fwd(aq, ak, av, bias, sinks, "swa_fwd")
    bf_pad = jnp.pad(attn_b_f, ((0, 0), (0, LANE - B_HEADS)))
    fsum = _fox_decay(zf, bf_pad, "fox_decay")
    fh = fsum[:, :, :B_HEADS].transpose(0, 2, 1)
    fcol = fh.reshape(bl, B_HEADS, s, 1)
    frow = fh.reshape(bl, B_HEADS, nb, 1, BLK)
    b_out, b_lse = _fox_fwd(bq, bk, bv, fcol, frow, "fox_fwd")
    mix0 = jnp.concatenate([_from_heads(a_out), _from_heads(b_out)], axis=-1)
    x1, o0 = _gate_outproj(mix0, proj0, 0, w_out0, x, gmod[0], "gate_outproj0")

    proj1, h1 = _norm_proj(x1, norm_g[1:2], scale[1], shift[1], w_in1, "norm_proj1")
    vec_rows = jnp.concatenate([lru_conv_w[0], lru_conv_b, lru_b_a, lru_b_x, lru_lambda], axis=0)
    vec_all = _all_gather8(vec_rows, "gather_lru_vectors", pltpu.VMEM)
    vec_all = vec_all.reshape(N_CHIP, 2, SUBLANE, lw // N_CHIP)[:, 0]
    vec_all = vec_all.transpose(1, 0, 2).reshape(SUBLANE, lw)
    cw_f, cb_f, ba_f, bx_f, lam_f = vec_all[0:4], vec_all[4:5], vec_all[5:6], vec_all[6:7], vec_all[7:8]
    hs = _lru_fwd(proj1, cw_f, cb_f, lru_w_a[0], ba_f, lru_w_x[0], bx_f, lam_f, "lru_fwd")
    x2, o1 = _gate_outproj(hs, proj1, 1, w_out1, x1, gmod[1], "gate_outproj1")

    loss_vec, dx2, g_final = _final_loss(x2, final_g.reshape(1, d), loss_target, "final_loss")
    loss = lax.psum(loss_vec[0, 0], ("x", "y", "c"))

    dhs, dgate1, do1, y1, dgm1 = _bwd_out(dx2, gmod[1], o1, hs, proj1, 1, w_out1.T, "bwd_out1")
    g_w_out1 = _matmul_tn(y1.reshape(bl * s, lw), do1.reshape(bl * s, d), "grad_w_out1")
    (dxr, g_cw, g_cb, g_wa, g_ba, g_wx, g_bx, g_lam) = _lru_bwd(
        proj1, hs, dhs, cw_f, cb_f, lru_w_a[0], ba_f, lru_w_x[0], bx_f, lam_f, "lru_bwd")
    dproj1 = jnp.concatenate([dxr, dgate1], axis=-1)
    g_w_in1 = _matmul_tn(h1.reshape(bl * s, d), dproj1.reshape(bl * s, 2 * lw), "grad_w_in1")
    dx1, dsh1, dsc1, g_ng1 = _bwd_in(dproj1, w_in1.T, x1, norm_g[1:2], scale[1], dx2, "bwd_in1")

    dmix0, dgate0, do0, y0, dgm0 = _bwd_out(dx1, gmod[0], o0, mix0, proj0, 0, w_out0.T, "bwd_out0")
    g_w_out0 = _matmul_tn(y0.reshape(bl * s, mixw), do0.reshape(bl * s, d), "grad_w_out0")
    da_out = _to_heads(dmix0[:, :, :aw], A_Q_HEADS)
    db_out = _to_heads(dmix0[:, :, aw:], B_HEADS)
    daq, dak, dav, dbias, dsink = _swa_bwd(aq, ak, av, bias, sinks, da_out, a_lse, "swa_bwd")
    dbq, dbk, dbv, dfrow = _fox_bwd(bq, bk, bv, fcol, frow, db_out, b_lse, "fox_bwd")
    df = dfrow.reshape(bl, B_HEADS, s).transpose(0, 2, 1)
    df = jnp.pad(df, ((0, 0), (0, 0), (0, LANE - B_HEADS)))
    dzf, g_bf = _fox_dgate(df, zf, bf_pad, B_HEADS, "fox_dgate")
    dproj0 = jnp.concatenate(
        [dgate0] + [_from_heads(t).astype(BF16) for t in (daq, dak, dav, dbq, dbk, dbv)] + [dzf.astype(BF16)],
        axis=-1)
    g_w_cat0 = _matmul_tn(h0.reshape(bl * s, d), dproj0.reshape(bl * s, n0), "grad_w_in0")
    g_w_in0 = jnp.concatenate([g_w_cat0[:, mixw:mixw + qkv_w + B_HEADS], g_w_cat0[:, :mixw]], axis=1)
    dx0, dsh0, dsc0, g_ng0 = _bwd_in(dproj0, w_cat0.T, x, norm_g[0:1], scale[0], dx1, "bwd_in0")
    g_relb, g_sink = _swa_small_grads(dbias, dsink, bucket, "swa_small_grads")

    dmod = jnp.concatenate([jnp.concatenate([dsh0, dsc0, dgm0], axis=-1),
                            jnp.concatenate([dsh1, dsc1, dgm1], axis=-1)], axis=1)
    dmod_all = _all_gather8(_pad_rows(dmod.reshape(bl, 6 * d), SUBLANE), "gather_dmod", pltpu.VMEM)
    dmod_all = dmod_all.reshape(N_DEV, rows_pad, 6 * d)[:, :bl].reshape(N_DEV * bl, 6 * d)
    dmod_chip = lax.dynamic_slice_in_dim(dmod_all.reshape(N_DEV * bl, 2, 3 * d), chip * ncol, ncol, axis=2)
    g_ada_w, g_ada_b = _ada_bwd(c_all, dmod_chip.transpose(1, 0, 2), dmod_all, "ada_bwd")
    g_ada_b = g_ada_b.reshape(2, 3 * d)

    big_grads = [g_w_in0, g_w_out0, g_w_in1, g_w_out1]

    def chip_shard(gfull, like, k):
        axis = 1 if like.shape[0] == gfull.shape[0] else 0
        width = like.shape[axis]
        return lax.slice_in_dim(gfull, k * width, (k + 1) * width, axis=axis)

    small_parts = [g_relb[:, :, 0].T, jnp.concatenate([g_ng0, g_ng1], axis=0), g_sink[:, 0, 0], g_bf[0, :B_HEADS],
                   g_wa, g_wx, g_final, g_cw, g_cb, g_ba, g_bx, g_lam]
    small_sizes = [int(np.prod(p.shape)) for p in small_parts]
    small_total = sum(small_sizes)
    piece_rows = -(-(-(-small_total // N_DEV) // 1024) // SUBLANE) * SUBLANE
    small_flat = jnp.concatenate([p.reshape(-1) for p in small_parts])
    small_flat = jnp.pad(small_flat, (0, N_DEV * piece_rows * 1024 - small_total))
    small_pieces = small_flat.reshape(N_DEV, piece_rows, 1024)
    big_pieces = jnp.stack([pack_halves([chip_shard(g, a, k) for g, a in zip(big_grads, big)], F32)
                            for k in range(N_CHIP)]).reshape(N_DEV, half_rows, 1024)
    pieces = jnp.concatenate([big_pieces, small_pieces], axis=1)
    slots = _all_to_all8(pieces, "exchange_grads")
    reduced = _sum_slots(slots, "sum_grads")
    both = _sibling_swap(reduced[:half_rows], "swap_halves")
    g_big = unpack_chip(both.reshape(-1))
    small_all = _all_gather8(reduced[half_rows:], "gather_small_grads", pltpu.VMEM).reshape(-1)
    g_small, off = [], 0
    for p, n in zip(small_parts, small_sizes):
        g_small.append(small_all[off:off + n].reshape(p.shape))
        off += n
    (g_rel_bias, g_norm_g, g_sinks, g_b_f, g_w_a, g_w_x, g_fin, g_cw_r, g_cb_r, g_ba_r, g_bx_r, g_lam_r) = g_small
    cw4 = lw // N_CHIP

    def my_cols(a):
        return lax.dynamic_slice_in_dim(a, chip * cw4, cw4, axis=1)

    grads = {
        "rel_bias": g_rel_bias, "norm_g": g_norm_g, "ada_w": g_ada_w, "ada_b": g_ada_b,
        "attn_w_in": g_big[0][None], "attn_sinks": g_sinks[None], "attn_b_f": g_b_f[None],
        "attn_w_out": g_big[1][None], "lru_w_in": g_big[2][None], "lru_conv_w": my_cols(g_cw_r)[None],
        "lru_conv_b": my_cols(g_cb_r), "lru_w_a": g_w_a[None], "lru_b_a": my_cols(g_ba_r),
        "lru_w_x": g_w_x[None], "lru_b_x": my_cols(g_bx_r), "lru_lambda": my_cols(g_lam_r),
        "lru_w_out": g_big[3][None], "final_g": g_fin.reshape(d),
    }
    weights = dict(rel_bias=rel_bias, norm_g=norm_g, ada_w=ada_w, ada_b=ada_b, attn_w_in=attn_w_in,
                   attn_sinks=attn_sinks, attn_b_f=attn_b_f, attn_w_out=attn_w_out, lru_w_in=lru_w_in,
                   lru_conv_w=lru_conv_w, lru_conv_b=lru_conv_b, lru_w_a=lru_w_a, lru_b_a=lru_b_a,
                   lru_w_x=lru_w_x, lru_b_x=lru_b_x, lru_lambda=lru_lambda, lru_w_out=lru_w_out, final_g=final_g)
    moms = dict(rel_bias=(m_rel_bias, v_rel_bias), norm_g=(m_norm_g, v_norm_g), ada_w=(m_ada_w, v_ada_w),
                ada_b=(m_ada_b, v_ada_b), attn_w_in=(m_attn_w_in, v_attn_w_in),
                attn_sinks=(m_attn_sinks, v_attn_sinks), attn_b_f=(m_attn_b_f, v_attn_b_f),
                attn_w_out=(m_attn_w_out, v_attn_w_out), lru_w_in=(m_lru_w_in, v_lru_w_in),
                lru_conv_w=(m_lru_conv_w, v_lru_conv_w), lru_conv_b=(m_lru_conv_b, v_lru_conv_b),
                lru_w_a=(m_lru_w_a, v_lru_w_a), lru_b_a=(m_lru_b_a, v_lru_b_a), lru_w_x=(m_lru_w_x, v_lru_w_x),
                lru_b_x=(m_lru_b_x, v_lru_b_x), lru_lambda=(m_lru_lambda, v_lru_lambda),
                lru_w_out=(m_lru_w_out, v_lru_w_out), final_g=(m_final_g, v_final_g))
    names = list(weights)
    big_names = [n for n in names if weights[n].size >= 65536]
    small_names = [n for n in names if weights[n].size < 65536]
    delta, new_m, new_v = {}, {}, {}
    for n in big_names:
        delta[n], new_m[n], new_v[n] = _adamw(weights[n], grads[n].reshape(weights[n].shape),
                                              moms[n][0], moms[n][1], "adamw_" + n)
    cat = lambda arrs: jnp.concatenate([a.reshape(-1) for a in arrs])
    sd, sm, sv = _adamw(cat([weights[n] for n in small_names]), cat([grads[n] for n in small_names]),
                        cat([moms[n][0] for n in small_names]), cat([moms[n][1] for n in small_names]),
                        "adamw_small")
    off = 0
    for n in small_names:
        sz = weights[n].size
        shp = weights[n].shape
        delta[n], new_m[n], new_v[n] = (sd[off:off + sz].reshape(shp), sm[off:off + sz].reshape(shp),
                                        sv[off:off + sz].reshape(shp))
        off += sz
    out_grads = [grads[n].reshape(weights[n].shape) for n in names]
    return (loss, dx0, *out_grads, *[delta[n] for n in names], *[new_m[n] for n in names],
            *[new_v[n] for n in names])
```

```python
import functools
import math

import numpy as np
import jax
import jax.numpy as jnp
from jax import lax
from jax.experimental import pallas as pl
from jax.experimental.pallas import tpu as pltpu

F32 = jnp.float32
BF16 = jnp.bfloat16
MESH = pl.DeviceIdType.MESH

N_DEV = 8
N_CHIP = 4
HEAD_DIM = 64
BLK = 128
A_Q_HEADS = 8
A_KV_HEADS = 2
A_GROUP = A_Q_HEADS // A_KV_HEADS
B_HEADS = 8
REL_BUCKETS = 32
REL_MAX_EXACT = 16
REL_MAX_DIST = 128
LRU_BLOCKS = 8
LRU_C = 8.0
CONV_WIDTH = 4
EPS = 1e-6
NEG = -1e30
SCALE = HEAD_DIM ** -0.5
LANE = 128
SUBLANE = 8
VMEM_LIMIT = 56 * 1024 * 1024
SCAN_CHUNK = 256
ADAM_LR = 0.001
ADAM_B1 = 0.9
ADAM_B2 = 0.999
ADAM_EPS = 1e-08
ADAM_WD = 0.01
ADAM_STEP = 10
HI = lax.Precision.HIGHEST


def _cp(sem=None):
    return pltpu.CompilerParams(dimension_semantics=sem, vmem_limit_bytes=VMEM_LIMIT)


def _dot(a, b):
    return jnp.dot(a, b, preferred_element_type=F32)


def _dot_nt(a, b):
    return lax.dot_general(a, b, (((1,), (1,)), ((), ())), preferred_element_type=F32)


def _dot_tn(a, b):
    return lax.dot_general(a, b, (((0,), (0,)), ((), ())), preferred_element_type=F32)


def _sigmoid(z):
    return 1.0 / (1.0 + jnp.exp(-z))


def _row_tile(rows, cap):
    if rows <= cap:
        return rows
    best = SUBLANE
    t = SUBLANE
    while t <= cap:
        if rows % t == 0:
            best = t
        t += SUBLANE
    return best


def _all_gather8(x_shard, name, space):
    m_per, n = x_shard.shape

    def body(x_ref, out_ref, send_sems, recv_sems, local_sem):
        x, y, c = lax.axis_index("x"), lax.axis_index("y"), lax.axis_index("c")
        me, sibling = (x, y, c), (x, y, 1 - c)
        chips = [(1 - x, y), (x, 1 - y), (1 - x, 1 - y)]

        def rows(px, py, pc):
            return out_ref.at[pl.ds((4 * px + 2 * py + pc) * m_per, m_per), :]

        def copy(k, block, to, src=None):
            return pltpu.make_async_remote_copy(
                src_ref=rows(*block) if src is None else src, dst_ref=rows(*block),
                send_sem=send_sems.at[k], recv_sem=recv_sems.at[k], device_id=to, device_id_type=MESH)

        mine = pltpu.make_async_copy(x_ref, rows(*me), local_sem)
        mine.start()
        first = [copy(0, me, sibling, src=x_ref)]
        first += [copy(1 + j, me, (*chip, c), src=x_ref) for j, chip in enumerate(chips)]
        for cp in first:
            cp.start()
        passed = [copy(4 + j, (*chip, c), sibling) for j, chip in enumerate(chips)]
        for j, chip in enumerate(chips):
            copy(1 + j, (*chip, c), me).wait_recv()
            passed[j].start()
        copy(0, sibling, me).wait_recv()
        for j, chip in enumerate(chips):
            copy(4 + j, (*chip, 1 - c), me).wait_recv()
        for cp in first + passed:
            cp.wait_send()
        mine.wait()

    return pl.pallas_call(
        body, name=name,
        out_shape=jax.ShapeDtypeStruct((N_DEV * m_per, n), x_shard.dtype),
        in_specs=[pl.BlockSpec(memory_space=space)],
        out_specs=pl.BlockSpec(memory_space=space),
        scratch_shapes=[pltpu.SemaphoreType.DMA((7,)), pltpu.SemaphoreType.DMA((7,)), pltpu.SemaphoreType.DMA],
        compiler_params=pltpu.CompilerParams(vmem_limit_bytes=VMEM_LIMIT),
    )(x_shard)


def _all_to_all8(pieces, name):
    _, m, n = pieces.shape

    def body(x_ref, out_ref, send_sems, recv_sems, local_sem):
        x, y, c = lax.axis_index("x"), lax.axis_index("y"), lax.axis_index("c")
        me = 4 * x + 2 * y + c
        mine = pltpu.make_async_copy(x_ref.at[me], out_ref.at[me], local_sem)
        mine.start()
        copies = []
        for k in range(1, N_DEV):
            fx, fy, fc = (k >> 2) & 1, (k >> 1) & 1, k & 1
            px, py, pc = x ^ fx, y ^ fy, c ^ fc
            peer = 4 * px + 2 * py + pc
            copies.append(pltpu.make_async_remote_copy(
                src_ref=x_ref.at[peer], dst_ref=out_ref.at[me],
                send_sem=send_sems.at[k - 1], recv_sem=recv_sems.at[k - 1],
                device_id=(px, py, pc), device_id_type=MESH))
        for cp in copies:
            cp.start()
        for cp in copies:
            cp.wait_recv()
        for cp in copies:
            cp.wait_send()
        mine.wait()

    hbm = pl.BlockSpec(memory_space=pltpu.HBM)
    return pl.pallas_call(
        body, name=name,
        out_shape=jax.ShapeDtypeStruct(pieces.shape, pieces.dtype),
        in_specs=[hbm], out_specs=hbm,
        scratch_shapes=[pltpu.SemaphoreType.DMA((7,)), pltpu.SemaphoreType.DMA((7,)), pltpu.SemaphoreType.DMA],
    )(pieces)


def _sibling_swap(half, name):
    m, n = half.shape

    def body(x_ref, out_ref, send_sem, recv_sem, local_sem):
        x, y, c = lax.axis_index("x"), lax.axis_index("y"), lax.axis_index("c")
        mine = pltpu.make_async_copy(x_ref, out_ref.at[c], local_sem)
        mine.start()
        cp = pltpu.make_async_remote_copy(
            src_ref=x_ref, dst_ref=out_ref.at[c], send_sem=send_sem, recv_sem=recv_sem,
            device_id=(x, y, 1 - c), device_id_type=MESH)
        cp.start()
        cp.wait_recv()
        cp.wait_send()
        mine.wait()

    hbm = pl.BlockSpec(memory_space=pltpu.HBM)
    return pl.pallas_call(
        body, name=name,
        out_shape=jax.ShapeDtypeStruct((2, m, n), half.dtype),
        in_specs=[hbm], out_specs=hbm,
        scratch_shapes=[pltpu.SemaphoreType.DMA, pltpu.SemaphoreType.DMA, pltpu.SemaphoreType.DMA],
    )(half)


def _sum_slots(slots, name):
    k, m, n = slots.shape
    tr = _row_tile(m, 328)

    def body(s_ref, o_ref):
        acc = s_ref[0]
        for j in range(1, k):
            acc = acc + s_ref[j]
        o_ref[...] = acc

    return pl.pallas_call(
        body, name=name, grid=(m // tr,),
        out_shape=jax.ShapeDtypeStruct((m, n), F32),
        in_specs=[pl.BlockSpec((k, tr, n), lambda i: (0, i, 0))],
        out_specs=pl.BlockSpec((tr, n), lambda i: (i, 0)),
        compiler_params=_cp(("parallel",)),
    )(slots)


def _ada_fwd(c_all, w, b, name):
    r, _ = c_all.shape
    n = w.shape[1]

    def body(c_ref, w_ref, b_ref, o_ref):
        cv = c_ref[...]
        act = cv * _sigmoid(cv)
        o_ref[...] = jnp.dot(act, w_ref[...], precision=HI, preferred_element_type=F32) + b_ref[...]

    return pl.pallas_call(body, name=name, out_shape=jax.ShapeDtypeStruct((r, n), F32),
                          compiler_params=_cp())(c_all, w, b)


def _ada_bwd(c_all, dmod_chip, dmod_all, name):
    r, d = c_all.shape
    nl, _, n = dmod_chip.shape

    def body(c_ref, dm_ref, da_ref, gw_ref, gb_ref):
        cv = c_ref[...]
        act = cv * _sigmoid(cv)
        for l in range(nl):
            gw_ref[l] = lax.dot_general(act, dm_ref[l], (((0,), (0,)), ((), ())), precision=HI,
                                        preferred_element_type=F32)
        gb_ref[...] = jnp.sum(da_ref[...], axis=0, keepdims=True)

    return pl.pallas_call(
        body, name=name,
        out_shape=(jax.ShapeDtypeStruct((nl, d, n), F32), jax.ShapeDtypeStruct((1, dmod_all.shape[1]), F32)),
        compiler_params=_cp())(c_all, dmod_chip, dmod_all)


def _norm_proj(x, g, scale, shift, w, name):
    b, s, d = x.shape
    n = w.shape[1]
    tm = min(s, 256)

    def body(x_ref, g_ref, sc_ref, sh_ref, w_ref, proj_ref, h_ref):
        xv = x_ref[...]
        rstd = lax.rsqrt(jnp.mean(xv * xv, axis=-1, keepdims=True) + EPS)
        h = (xv * rstd) * g_ref[...] * (1.0 + sc_ref[...]) + sh_ref[...]
        hb = h.astype(BF16)
        h_ref[...] = hb
        proj_ref[...] = _dot(hb, w_ref[...])

    return pl.pallas_call(
        body, name=name, grid=(b, s // tm),
        out_shape=(jax.ShapeDtypeStruct((b, s, n), F32), jax.ShapeDtypeStruct((b, s, d), BF16)),
        in_specs=[pl.BlockSpec((None, tm, d), lambda i, j: (i, j, 0)),
                  pl.BlockSpec((1, d), lambda i, j: (0, 0)),
                  pl.BlockSpec((None, 1, d), lambda i, j: (i, 0, 0)),
                  pl.BlockSpec((None, 1, d), lambda i, j: (i, 0, 0)),
                  pl.BlockSpec((d, n), lambda i, j: (0, 0))],
        out_specs=(pl.BlockSpec((None, tm, n), lambda i, j: (i, j, 0)),
                   pl.BlockSpec((None, tm, d), lambda i, j: (i, j, 0))),
        compiler_params=_cp(("parallel", "parallel")),
    )(x, g, scale, shift, w)


def _gate_outproj(mix, proj, gate_blk, w_out, x, gmod, name):
    b, s, wd = mix.shape
    d = w_out.shape[1]
    tm = min(s, 256)

    def body(mix_ref, gate_ref, w_ref, x_ref, gm_ref, xo_ref, o_ref):
        gt = gate_ref[...]
        y = (mix_ref[...] * (gt * _sigmoid(gt))).astype(BF16)
        o = _dot(y, w_ref[...])
        o_ref[...] = o.astype(BF16)
        xo_ref[...] = x_ref[...] + gm_ref[...] * o

    return pl.pallas_call(
        body, name=name, grid=(b, s // tm),
        out_shape=(jax.ShapeDtypeStruct((b, s, d), F32), jax.ShapeDtypeStruct((b, s, d), BF16)),
        in_specs=[pl.BlockSpec((None, tm, wd), lambda i, j: (i, j, 0)),
                  pl.BlockSpec((None, tm, wd), lambda i, j: (i, j, gate_blk)),
                  pl.BlockSpec((wd, d), lambda i, j: (0, 0)),
                  pl.BlockSpec((None, tm, d), lambda i, j: (i, j, 0)),
                  pl.BlockSpec((None, 1, d), lambda i, j: (i, 0, 0))],
        out_specs=(pl.BlockSpec((None, tm, d), lambda i, j: (i, j, 0)),
                   pl.BlockSpec((None, tm, d), lambda i, j: (i, j, 0))),
        compiler_params=_cp(("parallel", "parallel")),
    )(mix, proj, w_out, x, gmod)


def _final_loss(x, g, target, name):
    b, s, d = x.shape
    tm = min(s, 256)

    def body(x_ref, g_ref, t_ref, loss_ref, dx_ref, dg_ref):
        first = jnp.logical_and(pl.program_id(0) == 0, pl.program_id(1) == 0)

        @pl.when(first)
        def _():
            loss_ref[...] = jnp.zeros_like(loss_ref)
            dg_ref[...] = jnp.zeros_like(dg_ref)

        xv = x_ref[...]
        gv = g_ref[...]
        rstd = lax.rsqrt(jnp.mean(xv * xv, axis=-1, keepdims=True) + EPS)
        xhat = xv * rstd
        err = xhat * gv - t_ref[...]
        row = jnp.mean(err * err, axis=-1, keepdims=True)
        loss_ref[...] += 0.5 * jnp.sum(row, axis=0, keepdims=True)
        dy = err * (1.0 / d)
        dg_ref[...] += jnp.sum(dy * xhat, axis=0, keepdims=True)
        dxh = dy * gv
        dx_ref[...] = rstd * (dxh - xhat * jnp.mean(dxh * xhat, axis=-1, keepdims=True))

    return pl.pallas_call(
        body, name=name, grid=(b, s // tm),
        out_shape=(jax.ShapeDtypeStruct((1, LANE), F32), jax.ShapeDtypeStruct((b, s, d), F32),
                   jax.ShapeDtypeStruct((1, d), F32)),
        in_specs=[pl.BlockSpec((None, tm, d), lambda i, j: (i, j, 0)),
                  pl.BlockSpec((1, d), lambda i, j: (0, 0)),
                  pl.BlockSpec((None, tm, d), lambda i, j: (i, j, 0))],
        out_specs=(pl.BlockSpec((1, LANE), lambda i, j: (0, 0)),
                   pl.BlockSpec((None, tm, d), lambda i, j: (i, j, 0)),
                   pl.BlockSpec((1, d), lambda i, j: (0, 0))),
        compiler_params=_cp(("arbitrary", "arbitrary")),
    )(x, g, target)


def _bwd_out(dxo, gmod, o, mix, proj, gate_blk, w_out_t, name):
    b, s, d = dxo.shape
    wd = mix.shape[2]
    tm = min(s, 256)

    def body(dx_ref, gm_ref, o_ref, mix_ref, gate_ref, wt_ref, dmix_ref, dgate_ref, do_ref, y_ref, dgm_ref):
        @pl.when(pl.program_id(1) == 0)
        def _():
            dgm_ref[...] = jnp.zeros_like(dgm_ref)

        dx = dx_ref[...]
        dgm_ref[...] += jnp.sum(dx * o_ref[...].astype(F32), axis=0, keepdims=True)
        dob = (gm_ref[...] * dx).astype(BF16)
        do_ref[...] = dob
        dy = _dot(dob, wt_ref[...])
        gt = gate_ref[...]
        sg = _sigmoid(gt)
        silu = gt * sg
        mx = mix_ref[...]
        y_ref[...] = (mx * silu).astype(BF16)
        dmix_ref[...] = dy * silu
        dgate_ref[...] = (dy * mx * (sg * (1.0 + gt * (1.0 - sg)))).astype(BF16)

    row = lambda i, j: (i, j, 0)
    return pl.pallas_call(
        body, name=name, grid=(b, s // tm),
        out_shape=(jax.ShapeDtypeStruct((b, s, wd), F32), jax.ShapeDtypeStruct((b, s, wd), BF16),
                   jax.ShapeDtypeStruct((b, s, d), BF16), jax.ShapeDtypeStruct((b, s, wd), BF16),
                   jax.ShapeDtypeStruct((b, 1, d), F32)),
        in_specs=[pl.BlockSpec((None, tm, d), row),
                  pl.BlockSpec((None, 1, d), lambda i, j: (i, 0, 0)),
                  pl.BlockSpec((None, tm, d), row),
                  pl.BlockSpec((None, tm, wd), row),
                  pl.BlockSpec((None, tm, wd), lambda i, j: (i, j, gate_blk)),
                  pl.BlockSpec((d, wd), lambda i, j: (0, 0))],
        out_specs=(pl.BlockSpec((None, tm, wd), row), pl.BlockSpec((None, tm, wd), row),
                   pl.BlockSpec((None, tm, d), row), pl.BlockSpec((None, tm, wd), row),
                   pl.BlockSpec((None, 1, d), lambda i, j: (i, 0, 0))),
        compiler_params=_cp(("parallel", "arbitrary")),
    )(dxo, gmod, o, mix, proj, w_out_t)


def _bwd_in(dproj, w_in_t, x, g, scale, dxo, name):
    b, s, d = x.shape
    n = dproj.shape[2]
    tm = min(s, 256)

    def body(dp_ref, wt_ref, x_ref, g_ref, sc_ref, dxo_ref, dx_ref, dsh_ref, dsc_ref, dg_ref):
        @pl.when(jnp.logical_and(pl.program_id(0) == 0, pl.program_id(1) == 0))
        def _():
            dg_ref[...] = jnp.zeros_like(dg_ref)

        @pl.when(pl.program_id(1) == 0)
        def _():
            dsh_ref[...] = jnp.zeros_like(dsh_ref)
            dsc_ref[...] = jnp.zeros_like(dsc_ref)

        dh = _dot(dp_ref[...], wt_ref[...])
        xv = x_ref[...]
        gv = g_ref[...]
        one_sc = 1.0 + sc_ref[...]
        rstd = lax.rsqrt(jnp.mean(xv * xv, axis=-1, keepdims=True) + EPS)
        xhat = xv * rstd
        dsh_ref[...] += jnp.sum(dh, axis=0, keepdims=True)
        dsc_ref[...] += jnp.sum(dh * (xhat * gv), axis=0, keepdims=True)
        dhs = dh * one_sc
        dg_ref[...] += jnp.sum(dhs * xhat, axis=0, keepdims=True)
        dxh = dhs * gv
        dx_ref[...] = dxo_ref[...] + rstd * (dxh - xhat * jnp.mean(dxh * xhat, axis=-1, keepdims=True))

    row = lambda i, j: (i, j, 0)
    per_b = lambda i, j: (i, 0, 0)
    return pl.pallas_call(
        body, name=name, grid=(b, s // tm),
        out_shape=(jax.ShapeDtypeStruct((b, s, d), F32), jax.ShapeDtypeStruct((b, 1, d), F32),
                   jax.ShapeDtypeStruct((b, 1, d), F32), jax.ShapeDtypeStruct((1, d), F32)),
        in_specs=[pl.BlockSpec((None, tm, n), row),
                  pl.BlockSpec((n, d), lambda i, j: (0, 0)),
                  pl.BlockSpec((None, tm, d), row),
                  pl.BlockSpec((1, d), lambda i, j: (0, 0)),
                  pl.BlockSpec((None, 1, d), per_b),
                  pl.BlockSpec((None, tm, d), row)],
        out_specs=(pl.BlockSpec((None, tm, d), row), pl.BlockSpec((None, 1, d), per_b),
                   pl.BlockSpec((None, 1, d), per_b), pl.BlockSpec((1, d), lambda i, j: (0, 0))),
        compiler_params=_cp(("arbitrary", "arbitrary")),
    )(dproj, w_in_t, x, g, scale, dxo)


def _matmul_tn(a, bm, name):
    t, m = a.shape
    n = bm.shape[1]
    tk = next(c for c in (512, 256, 128) if t % c == 0)
    tn = n
    for cand in (1152, 1024, 896, 768, 640, 512, 384, 256, 128):
        if n % cand == 0:
            tn = cand
            break
    nk = t // tk

    def body(a_ref, b_ref, o_ref, acc_ref):
        k = pl.program_id(1)

        @pl.when(k == 0)
        def _():
            acc_ref[...] = jnp.zeros_like(acc_ref)

        acc_ref[...] += _dot_tn(a_ref[...], b_ref[...])

        @pl.when(k == nk - 1)
        def _():
            o_ref[...] = acc_ref[...]

    return pl.pallas_call(
        body, name=name, grid=(n // tn, nk),
        out_shape=jax.ShapeDtypeStruct((m, n), F32),
        in_specs=[pl.BlockSpec((tk, m), lambda j, k: (k, 0)),
                  pl.BlockSpec((tk, tn), lambda j, k: (k, j))],
        out_specs=pl.BlockSpec((m, tn), lambda j, k: (0, j)),
        scratch_shapes=[pltpu.VMEM((m, tn), F32)],
        compiler_params=_cp(("parallel", "arbitrary")),
    )(a, bm)


def _rel_buckets():
    qi = np.arange(BLK)[:, None]
    kj = np.arange(2 * BLK)[None, :]
    rel = qi - kj + BLK
    n = np.maximum(rel, 0)
    nf = np.maximum(n, 1).astype(np.float32)
    large = REL_MAX_EXACT + (np.log(nf / REL_MAX_EXACT) / math.log(REL_MAX_DIST / REL_MAX_EXACT)
                             * (REL_BUCKETS - REL_MAX_EXACT)).astype(np.int32)
    large = np.minimum(large, REL_BUCKETS - 1)
    bucket = np.where(n < REL_MAX_EXACT, n, large).astype(np.int32)
    valid = ((rel >= 0) & (rel < BLK)).astype(np.int32)
    return bucket, valid


def _swa_bias(rel_bias_t, bucket, valid, name):
    nh = rel_bias_t.shape[0]

    def body(rb_ref, bk_ref, vl_ref, o_ref):
        h = pl.program_id(0)
        bk = bk_ref[...]
        acc = jnp.zeros(bk.shape, F32)
        for i in range(REL_BUCKETS):
            acc = jnp.where(bk == i, rb_ref[h, i], acc)
        o_ref[...] = jnp.where(vl_ref[...] > 0, acc, NEG)

    return pl.pallas_call(
        body, name=name, grid=(nh,),
        out_shape=jax.ShapeDtypeStruct((nh, BLK, 2 * BLK), F32),
        in_specs=[pl.BlockSpec(memory_space=pltpu.SMEM),
                  pl.BlockSpec((BLK, 2 * BLK), lambda h: (0, 0)),
                  pl.BlockSpec((BLK, 2 * BLK), lambda h: (0, 0))],
        out_specs=pl.BlockSpec((None, BLK, 2 * BLK), lambda h: (h, 0, 0)),
        compiler_params=_cp(("arbitrary",)),
    )(rel_bias_t, bucket, valid)


def _swa_scores(n, qn, kc, kp, bias_ref):
    s_c = _dot_nt(qn, kc) * SCALE + bias_ref[:, BLK:]
    s_p = _dot_nt(qn, kp) * SCALE + bias_ref[:, :BLK]
    s_p = jnp.where(n > 0, s_p, NEG)
    return s_c, s_p


def _swa_fwd(q, k, v, bias, sinks, name):
    b, hq, s, hd = q.shape
    nb = s // BLK

    def body(sink_ref, q_ref, k_ref, v_ref, bias_ref, o_ref, l_ref):
        sink = sink_ref[pl.program_id(1)]

        def step(n, carry):
            r0 = pl.multiple_of(n * BLK, BLK)
            p0 = pl.multiple_of(jnp.maximum(n - 1, 0) * BLK, BLK)
            qn = q_ref[pl.ds(r0, BLK), :]
            s_c, s_p = _swa_scores(n, qn, k_ref[pl.ds(r0, BLK), :], k_ref[pl.ds(p0, BLK), :], bias_ref)
            m = jnp.maximum(jnp.maximum(jnp.max(s_c, axis=1, keepdims=True),
                                        jnp.max(s_p, axis=1, keepdims=True)), sink)
            e_c = jnp.exp(s_c - m)
            e_p = jnp.exp(s_p - m)
            den = jnp.sum(e_c, axis=1, keepdims=True) + jnp.sum(e_p, axis=1, keepdims=True) + jnp.exp(sink - m)
            inv = 1.0 / den
            o = _dot((e_c * inv).astype(BF16), v_ref[pl.ds(r0, BLK), :])
            o = o + _dot((e_p * inv).astype(BF16), v_ref[pl.ds(p0, BLK), :])
            o_ref[pl.ds(r0, BLK), :] = o
            l_ref[pl.ds(r0, BLK), :] = m + jnp.log(den)
            return carry

        lax.fori_loop(0, nb, step, 0)

    qspec = pl.BlockSpec((None, None, s, hd), lambda i, h: (i, h, 0, 0))
    kspec = pl.BlockSpec((None, None, s, hd), lambda i, h: (i, h // A_GROUP, 0, 0))
    return pl.pallas_call(
        body, name=name, grid=(b, hq),
        out_shape=(jax.ShapeDtypeStruct((b, hq, s, hd), F32), jax.ShapeDtypeStruct((b, hq, s, 1), F32)),
        in_specs=[pl.BlockSpec(memory_space=pltpu.SMEM), qspec, kspec, kspec,
                  pl.BlockSpec((None, BLK, 2 * BLK), lambda i, h: (h, 0, 0))],
        out_specs=(qspec, pl.BlockSpec((None, None, s, 1), lambda i, h: (i, h, 0, 0))),
        compiler_params=_cp(("parallel", "parallel")),
    )(sinks, q, k, v, bias)


def _swa_bwd(q, k, v, bias, sinks, do, lse, name):
    b, hq, s, hd = q.shape
    hkv = k.shape[1]
    nb = s // BLK

    def body(sink_ref, q_ref, k_ref, v_ref, bias_ref, do_ref, l_ref,
             dq_ref, dk_ref, dv_ref, db_ref, dsk_ref):
        g = pl.program_id(2)
        sink = sink_ref[pl.program_id(1) * A_GROUP + g]

        @pl.when(g == 0)
        def _():
            dk_ref[...] = jnp.zeros_like(dk_ref)
            dv_ref[...] = jnp.zeros_like(dv_ref)

        db_ref[...] = jnp.zeros_like(db_ref)

        def step(n, dsink):
            r0 = pl.multiple_of(n * BLK, BLK)
            p0 = pl.multiple_of(jnp.maximum(n - 1, 0) * BLK, BLK)
            qn = q_ref[pl.ds(r0, BLK), :]
            kc = k_ref[pl.ds(r0, BLK), :]
            kp = k_ref[pl.ds(p0, BLK), :]
            s_c, s_p = _swa_scores(n, qn, kc, kp, bias_ref)
            ln = l_ref[pl.ds(r0, BLK), :]
            p_c = jnp.exp(s_c - ln)
            p_p = jnp.exp(s_p - ln)
            p_s = jnp.exp(sink - ln)
            dob = do_ref[pl.ds(r0, BLK), :].astype(BF16)
            dp_c = _dot_nt(dob, v_ref[pl.ds(r0, BLK), :])
            dp_p = _dot_nt(dob, v_ref[pl.ds(p0, BLK), :])
            delta = jnp.sum(p_c * dp_c, axis=1, keepdims=True) + jnp.sum(p_p * dp_p, axis=1, keepdims=True)
            ds_c = p_c * (dp_c - delta)
            ds_p = p_p * (dp_p - delta)
            db_ref[:, BLK:] += ds_c
            db_ref[:, :BLK] += ds_p
            dsb_c = ds_c.astype(BF16)
            dsb_p = ds_p.astype(BF16)
            dq_ref[pl.ds(r0, BLK), :] = (_dot(dsb_c, kc) + _dot(dsb_p, kp)) * SCALE
            dk_ref[pl.ds(r0, BLK), :] += _dot_tn(dsb_c, qn) * SCALE
            dk_ref[pl.ds(p0, BLK), :] += _dot_tn(dsb_p, qn) * SCALE
            dv_ref[pl.ds(r0, BLK), :] += _dot_tn(p_c.astype(BF16), dob)
            dv_ref[pl.ds(p0, BLK), :] += _dot_tn(p_p.astype(BF16), dob)
            return dsink - jnp.sum(p_s * delta, axis=0, keepdims=True)

        dsink = lax.fori_loop(0, nb, step, jnp.zeros((1, 1), F32))
        dsk_ref[...] = jnp.broadcast_to(dsink, dsk_ref.shape)

    qspec = pl.BlockSpec((None, None, s, hd), lambda i, kv, g: (i, kv * A_GROUP + g, 0, 0))
    kspec = pl.BlockSpec((None, None, s, hd), lambda i, kv, g: (i, kv, 0, 0))
    return pl.pallas_call(
        body, name=name, grid=(b, hkv, A_GROUP),
        out_shape=(jax.ShapeDtypeStruct((b, hq, s, hd), F32), jax.ShapeDtypeStruct((b, hkv, s, hd), F32),
                   jax.ShapeDtypeStruct((b, hkv, s, hd), F32), jax.ShapeDtypeStruct((b, hq, BLK, 2 * BLK), F32),
                   jax.ShapeDtypeStruct((b, hq, 1, LANE), F32)),
        in_specs=[pl.BlockSpec(memory_space=pltpu.SMEM), qspec, kspec, kspec,
                  pl.BlockSpec((None, BLK, 2 * BLK), lambda i, kv, g: (kv * A_GROUP + g, 0, 0)),
                  qspec,
                  pl.BlockSpec((None, None, s, 1), lambda i, kv, g: (i, kv * A_GROUP + g, 0, 0))],
        out_specs=(qspec, kspec, kspec,
                   pl.BlockSpec((None, None, BLK, 2 * BLK), lambda i, kv, g: (i, kv * A_GROUP + g, 0, 0)),
                   pl.BlockSpec((None, None, 1, LANE), lambda i, kv, g: (i, kv * A_GROUP + g, 0, 0))),
        compiler_params=_cp(("parallel", "parallel", "arbitrary")),
    )(sinks, q, k, v, bias, do, lse)


def _swa_small_grads(db, dsk, bucket, name):
    b, nh = db.shape[0], db.shape[1]

    def body(db_ref, dsk_ref, bk_ref, gb_ref, gs_ref):
        acc = db_ref[0]
        sk = dsk_ref[0]
        for i in range(1, b):
            acc = acc + db_ref[i]
            sk = sk + dsk_ref[i]
        gs_ref[...] = sk
        bk = bk_ref[...]
        for i in range(REL_BUCKETS):
            part = jnp.sum(jnp.where(bk == i, acc, 0.0), axis=1, keepdims=True)
            tot = jnp.sum(part, axis=0, keepdims=True)
            gb_ref[i:i + 1, :] = jnp.broadcast_to(tot, (1, LANE))

    return pl.pallas_call(
        body, name=name, grid=(nh,),
        out_shape=(jax.ShapeDtypeStruct((nh, REL_BUCKETS, LANE), F32), jax.ShapeDtypeStruct((nh, 1, LANE), F32)),
        in_specs=[pl.BlockSpec((b, None, BLK, 2 * BLK), lambda h: (0, h, 0, 0)),
                  pl.BlockSpec((b, None, 1, LANE), lambda h: (0, h, 0, 0)),
                  pl.BlockSpec((BLK, 2 * BLK), lambda h: (0, 0))],
        out_specs=(pl.BlockSpec((None, REL_BUCKETS, LANE), lambda h: (h, 0, 0)),
                   pl.BlockSpec((None, 1, LANE), lambda h: (h, 0, 0))),
        compiler_params=_cp(("parallel",)),
    )(db, dsk, bucket)


def _log_sigmoid(z):
    return jnp.minimum(z, 0.0) - jnp.log(1.0 + jnp.exp(-jnp.abs(z)))


def _fox_decay(z, bf, name):
    b, s, w = z.shape
    nb = s // BLK

    def body(z_ref, bf_ref, f_ref):
        r = lax.broadcasted_iota(jnp.int32, (BLK, BLK), 0)
        c = lax.broadcasted_iota(jnp.int32, (BLK, BLK), 1)
        tri = (c <= r).astype(F32)

        def step(n, carry):
            r0 = pl.multiple_of(n * BLK, BLK)
            lf = _log_sigmoid(z_ref[pl.ds(r0, BLK), :] + bf_ref[...])
            f_ref[pl.ds(r0, BLK), :] = jnp.dot(tri, lf, precision=HI, preferred_element_type=F32) + carry
            return carry + jnp.sum(lf, axis=0, keepdims=True)

        lax.fori_loop(0, nb, step, jnp.zeros((1, w), F32))

    spec = pl.BlockSpec((None, s, w), lambda i: (i, 0, 0))
    return pl.pallas_call(
        body, name=name, grid=(b,), out_shape=jax.ShapeDtypeStruct((b, s, w), F32),
        in_specs=[spec, pl.BlockSpec((1, w), lambda i: (0, 0))], out_specs=spec,
        compiler_params=_cp(("parallel",)),
    )(z, bf)


def _fox_dgate(df, z, bf, nheads, name):
    b, s, w = z.shape
    nb = s // BLK

    def body(df_ref, z_ref, bf_ref, dz_ref, dbf_ref):
        @pl.when(pl.program_id(0) == 0)
        def _():
            dbf_ref[...] = jnp.zeros_like(dbf_ref)

        r = lax.broadcasted_iota(jnp.int32, (BLK, BLK), 0)
        c = lax.broadcasted_iota(jnp.int32, (BLK, BLK), 1)
        tri = (c >= r).astype(F32)
        lane = lax.broadcasted_iota(jnp.int32, (BLK, w), 1)

        def step(i, carry):
            tail, dbf = carry
            r0 = pl.multiple_of((nb - 1 - i) * BLK, BLK)
            dfb = df_ref[pl.ds(r0, BLK), :]
            dlf = jnp.dot(tri, dfb, precision=HI, preferred_element_type=F32) + tail
            dz = jnp.where(lane < nheads, dlf * _sigmoid(-(z_ref[pl.ds(r0, BLK), :] + bf_ref[...])), 0.0)
            dz_ref[pl.ds(r0, BLK), :] = dz
            return tail + jnp.sum(dfb, axis=0, keepdims=True), dbf + jnp.sum(dz, axis=0, keepdims=True)

        zero = jnp.zeros((1, w), F32)
        _, dbf = lax.fori_loop(0, nb, step, (zero, zero))
        dbf_ref[...] += dbf

    spec = pl.BlockSpec((None, s, w), lambda i: (i, 0, 0))
    one = pl.BlockSpec((1, w), lambda i: (0, 0))
    return pl.pallas_call(
        body, name=name, grid=(b,),
        out_shape=(jax.ShapeDtypeStruct((b, s, w), F32), jax.ShapeDtypeStruct((1, w), F32)),
        in_specs=[spec, spec, one], out_specs=(spec, one),
        compiler_params=_cp(("arbitrary",)),
    )(df, z, bf)


def _fox_segments(nb):
    per = max(1, nb // 4)
    return per, nb // per


def _fox_fwd(q, kt, v, fcol, frow, name):
    b, nh, s, hd = q.shape
    per, nseg = _fox_segments(s // BLK)

    def body(q_ref, kt_ref, v_ref, fc_ref, fr_ref, o_ref, l_ref):
        for seg in range(nseg):
            w = (seg + 1) * per * BLK
            causal = (lax.broadcasted_iota(jnp.int32, (BLK, w), 1)
                      - lax.broadcasted_iota(jnp.int32, (BLK, w), 0))

            def qstep(n, carry):
                r0 = pl.multiple_of(n * BLK, BLK)
                sc = _dot(q_ref[pl.ds(r0, BLK), :], kt_ref[:, :w]) * SCALE
                sc = sc + (fc_ref[pl.ds(r0, BLK), :] - fr_ref[:, :w])
                sc = jnp.where(causal <= n * BLK, sc, NEG)
                m = jnp.max(sc, axis=1, keepdims=True)
                e = jnp.exp(sc - m)
                l = jnp.sum(e, axis=1, keepdims=True)
                o_ref[pl.ds(r0, BLK), :] = _dot((e * (1.0 / l)).astype(BF16), v_ref[:w, :])
                l_ref[pl.ds(r0, BLK), :] = m + jnp.log(l)
                return carry

            lax.fori_loop(seg * per, (seg + 1) * per, qstep, 0, unroll=True)

    spec = pl.BlockSpec((None, None, s, hd), lambda i, h: (i, h, 0, 0))
    tspec = pl.BlockSpec((None, None, hd, s), lambda i, h: (i, h, 0, 0))
    col = pl.BlockSpec((None, None, s, 1), lambda i, h: (i, h, 0, 0))
    rowspec = pl.BlockSpec((None, None, 1, s), lambda i, h: (i, h, 0, 0))
    return pl.pallas_call(
        body, name=name, grid=(b, nh),
        out_shape=(jax.ShapeDtypeStruct((b, nh, s, hd), F32), jax.ShapeDtypeStruct((b, nh, s, 1), F32)),
        in_specs=[spec, tspec, spec, col, rowspec], out_specs=(spec, col),
        compiler_params=_cp(("parallel", "parallel")),
    )(q, kt, v, fcol, frow)


def _fox_bwd(q, qtb, k, kt, v, vt, fcol, frow, frowb, do, dotb, lse, lserowb, name):
    b, nh, s, hd = q.shape
    nb = s // BLK
    per, nseg = _fox_segments(nb)

    def body(q_ref, qt_ref, k_ref, kt_ref, v_ref, vt_ref, fc_ref, fr_ref, frb_ref, do_ref, dot_ref, l_ref, lrb_ref,
             dq_ref, dk_ref, dv_ref, dfr_ref):
        dk_ref[...] = jnp.zeros_like(dk_ref)
        dv_ref[...] = jnp.zeros_like(dv_ref)
        dfr_ref[...] = jnp.zeros_like(dfr_ref)
        for seg in range(nseg):
            w = (seg + 1) * per * BLK
            causal = (lax.broadcasted_iota(jnp.int32, (BLK, w), 1)
                      - lax.broadcasted_iota(jnp.int32, (BLK, w), 0))
            causal_t = (lax.broadcasted_iota(jnp.int32, (w, BLK), 0)
                        - lax.broadcasted_iota(jnp.int32, (w, BLK), 1))

            def nstep(n, carry):
                r0 = pl.multiple_of(n * BLK, BLK)
                qn = q_ref[pl.ds(r0, BLK), :]
                dob = do_ref[pl.ds(r0, BLK), :]
                sc = _dot(qn, kt_ref[:, :w]) * SCALE + (fc_ref[pl.ds(r0, BLK), :] - fr_ref[:, :w])
                p = jnp.exp(jnp.where(causal <= n * BLK, sc, NEG) - l_ref[pl.ds(r0, BLK), :])
                dp = _dot(dob, vt_ref[:, :w])
                ds = p * (dp - jnp.sum(p * dp, axis=1, keepdims=True))
                dq_ref[pl.ds(r0, BLK), :] = _dot(ds.astype(BF16), k_ref[:w, :]) * SCALE
                dfr_ref[:, :w] -= jnp.sum(ds, axis=0, keepdims=True)
                sct = _dot(k_ref[:w, :], qt_ref[n]) * SCALE + (frb_ref[n] - fc_ref[:w, :])
                pt = jnp.exp(jnp.where(causal_t <= n * BLK, sct, NEG) - lrb_ref[n])
                dpt = _dot(v_ref[:w, :], dot_ref[n])
                dst = pt * (dpt - jnp.sum(pt * dpt, axis=0, keepdims=True))
                dk_ref[:w, :] += _dot(dst.astype(BF16), qn) * SCALE
                dv_ref[:w, :] += _dot(pt.astype(BF16), dob)
                return carry

            lax.fori_loop(seg * per, (seg + 1) * per, nstep, 0, unroll=2)

    spec = pl.BlockSpec((None, None, s, hd), lambda i, h: (i, h, 0, 0))
    tspec = pl.BlockSpec((None, None, hd, s), lambda i, h: (i, h, 0, 0))
    tbspec = pl.BlockSpec((None, None, nb, hd, BLK), lambda i, h: (i, h, 0, 0, 0))
    col = pl.BlockSpec((None, None, s, 1), lambda i, h: (i, h, 0, 0))
    rowspec = pl.BlockSpec((None, None, 1, s), lambda i, h: (i, h, 0, 0))
    rowbspec = pl.BlockSpec((None, None, nb, 1, BLK), lambda i, h: (i, h, 0, 0, 0))
    shp = jax.ShapeDtypeStruct((b, nh, s, hd), F32)
    return pl.pallas_call(
        body, name=name, grid=(b, nh),
        out_shape=(shp, shp, shp, jax.ShapeDtypeStruct((b, nh, 1, s), F32)),
        in_specs=[spec, tbspec, spec, tspec, spec, tspec, col, rowspec, rowbspec, spec, tbspec, col, rowbspec],
        out_specs=(spec, spec, spec, rowspec),
        compiler_params=_cp(("parallel", "parallel")),
    )(q, qtb, k, kt, v, vt, fcol, frow, frowb, do, dotb, lse, lserowb)


def _expm1(x):
    poly = x * (1.0 + x * (1.0 / 2.0) * (1.0 + x * (1.0 / 3.0) * (1.0 + x * (1.0 / 4.0) * (1.0 + x * (1.0 / 5.0)
                                                                                          * (1.0 + x * (1.0 / 6.0))))))
    return jnp.where(x > -0.1, poly, jnp.exp(x) - 1.0)


def _softplus(z):
    return jnp.maximum(z, 0.0) + jnp.log(1.0 + jnp.exp(-jnp.abs(z)))


def _scan_up(a, u, carry, row):
    tc = a.shape[0]
    d = 1
    while d < tc:
        keep = row >= d
        a_sh = jnp.where(keep, pltpu.roll(a, d, 0), 1.0)
        u_sh = jnp.where(keep, pltpu.roll(u, d, 0), 0.0)
        u = a * u_sh + u
        a = a * a_sh
        d *= 2
    return u + a * carry


def _scan_down(bnext, g, carry, row):
    tc = g.shape[0]
    a, u = bnext, g
    d = 1
    while d < tc:
        keep = row < tc - d
        a_sh = jnp.where(keep, pltpu.roll(a, tc - d, 0), 1.0)
        u_sh = jnp.where(keep, pltpu.roll(u, tc - d, 0), 0.0)
        u = a * u_sh + u
        a = a * a_sh
        d *= 2
    return u + a * carry


def _pick_row(val, row, which):
    return jnp.sum(jnp.where(row == which, val, 0.0), axis=0, keepdims=True)


def _lru_gates(xpad_ref, t0, tc, cw_ref, cb_ref, wa, ba_ref, wx, bx_ref, sp):
    xw = xpad_ref[pl.ds(t0, tc + SUBLANE), :]
    xc = cb_ref[...]
    for j in range(CONV_WIDTH):
        sh = CONV_WIDTH - 1 - j
        xs = xw if sh == 0 else pltpu.roll(xw, sh, 0)
        xc = xc + xs[SUBLANE:, :] * cw_ref[j:j + 1, :]
    xcb = xc.astype(BF16)
    r = _sigmoid(_dot(xcb, wa) + ba_ref[...])
    i = _sigmoid(_dot(xcb, wx) + bx_ref[...])
    la = -LRU_C * r * sp
    return xc, r, i, la


def _lru_specs(s, cb):
    seq = lambda bi, ni: (bi, 0, ni)
    return dict(
        seq=pl.BlockSpec((None, s, cb), seq),
        cw=pl.BlockSpec((CONV_WIDTH, cb), lambda bi, ni: (0, ni)),
        vec=pl.BlockSpec((1, cb), lambda bi, ni: (0, ni)),
        wblk=pl.BlockSpec((None, cb, cb), lambda bi, ni: (ni, 0, 0)),
    )


def _lru_fwd(proj, cw, cb_, wa, ba, wx, bx, lam, name):
    b, s, _ = proj.shape
    nblk, cb, _ = wa.shape
    tc = min(s, SCAN_CHUNK)
    nc = s // tc

    def body(x_ref, cw_ref, cb_ref, wa_ref, ba_ref, wx_ref, bx_ref, lam_ref, hs_ref, xpad_ref):
        xpad_ref[0:SUBLANE, :] = jnp.zeros((SUBLANE, cb), F32)
        xpad_ref[SUBLANE:, :] = x_ref[...]
        wa_b = wa_ref[...].astype(BF16)
        wx_b = wx_ref[...].astype(BF16)
        sp = _softplus(-lam_ref[...])
        row = lax.broadcasted_iota(jnp.int32, (tc, cb), 0)

        def chunk(ci, carry):
            t0 = pl.multiple_of(ci * tc, tc)
            xc, r, i, la = _lru_gates(xpad_ref, t0, tc, cw_ref, cb_ref, wa_b, ba_ref, wx_b, bx_ref, sp)
            a = jnp.exp(la)
            u = jnp.sqrt(-_expm1(2.0 * la)) * (i * xc)
            h = _scan_up(a, u, carry, row)
            hs_ref[pl.ds(t0, tc), :] = h
            return _pick_row(h, row, tc - 1)

        lax.fori_loop(0, nc, chunk, jnp.zeros((1, cb), F32))

    sp_ = _lru_specs(s, cb)
    return pl.pallas_call(
        body, name=name, grid=(b, nblk),
        out_shape=jax.ShapeDtypeStruct((b, s, nblk * cb), F32),
        in_specs=[sp_["seq"], sp_["cw"], sp_["vec"], sp_["wblk"], sp_["vec"], sp_["wblk"], sp_["vec"], sp_["vec"]],
        out_specs=sp_["seq"],
        scratch_shapes=[pltpu.VMEM((s + SUBLANE, cb), F32)],
        compiler_params=_cp(("parallel", "parallel")),
    )(proj, cw, cb_, wa, ba, wx, bx, lam)


def _lru_bwd(proj, hs, dhs, cw, cb_, wa, ba, wx, bx, lam, name):
    b, s, _ = proj.shape
    nblk, cb, _ = wa.shape
    tc = min(s, SCAN_CHUNK)
    nc = s // tc

    def body(x_ref, hs_ref, dhs_ref, cw_ref, cb_ref, wa_ref, ba_ref, wx_ref, bx_ref, lam_ref,
             dx_ref, dcw_ref, dcb_ref, dwa_ref, dba_ref, dwx_ref, dbx_ref, dlam_ref,
             xpad_ref, hpad_ref, dcpad_ref, xc_ref, r_ref, i_ref, a_ref):
        @pl.when(pl.program_id(1) == 0)
        def _():
            for ref in (dcw_ref, dcb_ref, dwa_ref, dba_ref, dwx_ref, dbx_ref, dlam_ref):
                ref[...] = jnp.zeros_like(ref)

        zeros8 = jnp.zeros((SUBLANE, cb), F32)
        xpad_ref[0:SUBLANE, :] = zeros8
        xpad_ref[SUBLANE:, :] = x_ref[...]
        hpad_ref[0:SUBLANE, :] = zeros8
        hpad_ref[SUBLANE:, :] = hs_ref[...]
        dcpad_ref[s:s + SUBLANE, :] = zeros8
        wa_b = wa_ref[...].astype(BF16)
        wx_b = wx_ref[...].astype(BF16)
        lam_v = lam_ref[...]
        sp = _softplus(-lam_v)
        dsp_dlam = -_sigmoid(-lam_v)
        row = lax.broadcasted_iota(jnp.int32, (tc, cb), 0)

        def recompute(ci, carry):
            t0 = pl.multiple_of(ci * tc, tc)
            xc, r, i, la = _lru_gates(xpad_ref, t0, tc, cw_ref, cb_ref, wa_b, ba_ref, wx_b, bx_ref, sp)
            xc_ref[pl.ds(t0, tc), :] = xc
            r_ref[pl.ds(t0, tc), :] = r
            i_ref[pl.ds(t0, tc), :] = i
            a_ref[pl.ds(t0, tc), :] = jnp.exp(la)
            return carry

        lax.fori_loop(0, nc, recompute, 0)

        def adjoint(k, carry):
            g_next, a_first_next = carry
            t0 = pl.multiple_of((nc - 1 - k) * tc, tc)
            a = a_ref[pl.ds(t0, tc), :]
            a_next = jnp.where(row == tc - 1, a_first_next, pltpu.roll(a, tc - 1, 0))
            gg = _scan_down(a_next, dhs_ref[pl.ds(t0, tc), :], g_next, row)
            h_prev = pltpu.roll(hpad_ref[pl.ds(t0, tc + SUBLANE), :], 1, 0)[SUBLANE:, :]
            xc = xc_ref[pl.ds(t0, tc), :]
            r = r_ref[pl.ds(t0, tc), :]
            i = i_ref[pl.ds(t0, tc), :]
            mult = jnp.sqrt(-_expm1(-2.0 * LRU_C * r * sp))
            d_mult = gg * i * xc
            d_i = gg * mult * xc
            d_xc = gg * mult * i
            d_la = gg * h_prev * a - d_mult * (a * a) / mult
            d_zr = (d_la * (-LRU_C * sp)) * r * (1.0 - r)
            d_zi = d_i * i * (1.0 - i)
            dlam_ref[...] += jnp.sum(d_la * (-LRU_C * r), axis=0, keepdims=True) * dsp_dlam
            dzr_b = d_zr.astype(BF16)
            dzi_b = d_zi.astype(BF16)
            xcb = xc.astype(BF16)
            d_xc = d_xc + _dot_nt(dzr_b, wa_b) + _dot_nt(dzi_b, wx_b)
            dwa_ref[...] += _dot_tn(xcb, dzr_b)
            dwx_ref[...] += _dot_tn(xcb, dzi_b)
            dba_ref[...] += jnp.sum(d_zr, axis=0, keepdims=True)
            dbx_ref[...] += jnp.sum(d_zi, axis=0, keepdims=True)
            dcb_ref[...] += jnp.sum(d_xc, axis=0, keepdims=True)
            dcpad_ref[pl.ds(t0, tc), :] = d_xc
            return _pick_row(gg, row, 0), _pick_row(a, row, 0)

        zero = jnp.zeros((1, cb), F32)
        lax.fori_loop(0, nc, adjoint, (zero, zero))

        def conv_back(ci, carry):
            t0 = pl.multiple_of(ci * tc, tc)
            dw = dcpad_ref[pl.ds(t0, tc + SUBLANE), :]
            xw = xpad_ref[pl.ds(t0, tc + SUBLANE), :]
            d_xc = dw[:tc, :]
            dxr = jnp.zeros((tc, cb), F32)
            for j in range(CONV_WIDTH):
                sh = CONV_WIDTH - 1 - j
                dsh = dw if sh == 0 else pltpu.roll(dw, tc + SUBLANE - sh, 0)
                dxr = dxr + dsh[:tc, :] * cw_ref[j:j + 1, :]
                xs = xw if sh == 0 else pltpu.roll(xw, sh, 0)
                dcw_ref[j:j + 1, :] += jnp.sum(d_xc * xs[SUBLANE:, :], axis=0, keepdims=True)
            dx_ref[pl.ds(t0, tc), :] = dxr.astype(BF16)
            return carry

        lax.fori_loop(0, nc, conv_back, 0)

    seq = lambda ni, bi: (bi, 0, ni)
    seqspec = pl.BlockSpec((None, s, cb), seq)
    cwspec = pl.BlockSpec((CONV_WIDTH, cb), lambda ni, bi: (0, ni))
    vec = pl.BlockSpec((1, cb), lambda ni, bi: (0, ni))
    wblk = pl.BlockSpec((None, cb, cb), lambda ni, bi: (ni, 0, 0))
    w = nblk * cb
    return pl.pallas_call(
        body, name=name, grid=(nblk, b),
        out_shape=(jax.ShapeDtypeStruct((b, s, w), BF16), jax.ShapeDtypeStruct((CONV_WIDTH, w), F32),
                   jax.ShapeDtypeStruct((1, w), F32), jax.ShapeDtypeStruct((nblk, cb, cb), F32),
                   jax.ShapeDtypeStruct((1, w), F32), jax.ShapeDtypeStruct((nblk, cb, cb), F32),
                   jax.ShapeDtypeStruct((1, w), F32), jax.ShapeDtypeStruct((1, w), F32)),
        in_specs=[seqspec, seqspec, seqspec, cwspec, vec, wblk, vec, wblk, vec, vec],
        out_specs=(seqspec, cwspec, vec, wblk, vec, wblk, vec, vec),
        scratch_shapes=[pltpu.VMEM((s + SUBLANE, cb), F32)] * 3 + [pltpu.VMEM((s, cb), F32)] * 4,
        compiler_params=_cp(("parallel", "arbitrary")),
    )(proj, hs, dhs, cw, cb_, wa, ba, wx, bx, lam)


def _adamw(w, g, m, v, name):
    shape = w.shape
    total = int(np.prod(shape))
    cols = 1024
    rows = -(-total // cols)
    rows = -(-rows // SUBLANE) * SUBLANE
    tr = _row_tile(rows, 512)
    pad = rows * cols - total

    def flat(a):
        a = a.reshape(-1)
        if pad:
            a = jnp.pad(a, (0, pad))
        return a.reshape(rows, cols)

    c1 = 1.0 - ADAM_B1 ** ADAM_STEP
    c2 = 1.0 - ADAM_B2 ** ADAM_STEP

    def body(w_ref, g_ref, m_ref, v_ref, d_ref, nm_ref, nv_ref):
        gv = g_ref[...]
        nm = ADAM_B1 * m_ref[...] + (1.0 - ADAM_B1) * gv
        nv = ADAM_B2 * v_ref[...] + (1.0 - ADAM_B2) * (gv * gv)
        nm_ref[...] = nm
        nv_ref[...] = nv
        d_ref[...] = -ADAM_LR * ((nm / c1) / (jnp.sqrt(nv / c2) + ADAM_EPS) + ADAM_WD * w_ref[...])

    spec = pl.BlockSpec((tr, cols), lambda i: (i, 0))
    shp = jax.ShapeDtypeStruct((rows, cols), F32)
    outs = pl.pallas_call(
        body, name=name, grid=(rows // tr,), out_shape=(shp, shp, shp),
        in_specs=[spec] * 4, out_specs=(spec,) * 3,
        compiler_params=_cp(("parallel",)),
    )(flat(w), flat(g), flat(m), flat(v))
    return tuple(o.reshape(-1)[:total].reshape(shape) for o in outs)


def _to_heads(t, nh):
    b, s, _ = t.shape
    return t.reshape(b, s, nh, HEAD_DIM).transpose(0, 2, 1, 3)


def _to_heads_t(t, nh):
    b, s, _ = t.shape
    return t.reshape(b, s, nh, HEAD_DIM).transpose(0, 2, 3, 1)


def _to_blocks_t(t, nh):
    b, s, _ = t.shape
    return t.reshape(b, s // BLK, BLK, nh, HEAD_DIM).transpose(0, 3, 1, 4, 2)


def _from_heads(t):
    b, nh, s, hd = t.shape
    return t.transpose(0, 2, 1, 3).reshape(b, s, nh * hd)


def _pad_rows(a, mult):
    r = a.shape[0]
    p = (-r) % mult
    return jnp.pad(a, ((0, p), (0, 0))) if p else a


def kernel(x, c, rel_bias, norm_g, ada_w, ada_b, attn_w_in, attn_sinks, attn_b_f, attn_w_out, lru_w_in, lru_conv_w, lru_conv_b, lru_w_a, lru_b_a, lru_w_x, lru_b_x, lru_lambda, lru_w_out, final_g, loss_target, m_rel_bias, m_norm_g, m_ada_w, m_ada_b, m_attn_w_in, m_attn_sinks, m_attn_b_f, m_attn_w_out, m_lru_w_in, m_lru_conv_w, m_lru_conv_b, m_lru_w_a, m_lru_b_a, m_lru_w_x, m_lru_b_x, m_lru_lambda, m_lru_w_out, m_final_g, v_rel_bias, v_norm_g, v_ada_w, v_ada_b, v_attn_w_in, v_attn_sinks, v_attn_b_f, v_attn_w_out, v_lru_w_in, v_lru_conv_w, v_lru_conv_b, v_lru_w_a, v_lru_b_a, v_lru_w_x, v_lru_b_x, v_lru_lambda, v_lru_w_out, v_final_g):
    bl, s, d = x.shape
    ix, iy, ic = lax.axis_index("x"), lax.axis_index("y"), lax.axis_index("c")
    chip = 2 * ix + iy
    me = 2 * chip + ic
    nb = s // BLK
    aw = A_Q_HEADS * HEAD_DIM
    akv = A_KV_HEADS * HEAD_DIM
    bw = B_HEADS * HEAD_DIM
    mixw = aw + bw
    qkv_w = aw + 2 * akv + 3 * bw
    n_in = attn_w_in.shape[2] * N_CHIP
    lw = lru_lambda.shape[1] * N_CHIP
    n0 = mixw + qkv_w + LANE

    rows_pad = -(-bl // SUBLANE) * SUBLANE
    c_all = _all_gather8(_pad_rows(c, SUBLANE), "gather_c", pltpu.VMEM)
    c_all = c_all.reshape(N_DEV, rows_pad, d)[:, :bl].reshape(N_DEV * bl, d)
    ncol = ada_w.shape[2]
    ada_w_l = lax.dynamic_index_in_dim(ada_w, ic, 0, keepdims=False)
    ada_b_l = lax.dynamic_slice(ada_b, (ic, chip * ncol), (1, ncol))
    mod_part = _ada_fwd(c_all, ada_w_l, ada_b_l, "ada_fwd")
    mod_all = _all_gather8(_pad_rows(mod_part, SUBLANE), "gather_mod", pltpu.VMEM)
    mrows = -(-(N_DEV * bl) // SUBLANE) * SUBLANE
    mod_all = mod_all.reshape(N_CHIP, 2, mrows, ncol)[:, :, :N_DEV * bl]
    mod_all = mod_all.transpose(1, 2, 0, 3).reshape(2, N_DEV * bl, N_CHIP * ncol)
    mod = lax.dynamic_slice_in_dim(mod_all, me * bl, bl, axis=1)
    shift = [mod[l, :, 0:d].reshape(bl, 1, d) for l in range(2)]
    scale = [mod[l, :, d:2 * d].reshape(bl, 1, d) for l in range(2)]
    gmod = [mod[l, :, 2 * d:3 * d].reshape(bl, 1, d) for l in range(2)]

    big = [attn_w_in[0], attn_w_out[0], lru_w_in[0], lru_w_out[0]]
    big_sizes = [int(np.prod(a.shape)) for a in big]
    big_total = sum(big_sizes)
    half_rows = -(-(-(-big_total // 2) // 1024) // 16) * 16
    half_len = half_rows * 1024

    def pack_halves(parts, dtype):
        flat = jnp.concatenate([p.reshape(-1).astype(dtype) for p in parts])
        flat = jnp.pad(flat, (0, 2 * half_len - big_total))
        return flat.reshape(2, half_rows, 1024)

    def unpack_chip(flat):
        outs, off = [], 0
        for a, n in zip(big, big_sizes):
            outs.append(flat[off:off + n].reshape(a.shape))
            off += n
        return outs

    my_half = lax.dynamic_index_in_dim(pack_halves(big, BF16), ic, 0, keepdims=False)
    gathered = _all_gather8(my_half, "gather_weights", pltpu.HBM).reshape(N_CHIP, 2 * half_len)
    per_chip = [unpack_chip(gathered[k]) for k in range(N_CHIP)]
    w_in0 = jnp.concatenate([per_chip[k][0] for k in range(N_CHIP)], axis=1)
    w_out0 = jnp.concatenate([per_chip[k][1] for k in range(N_CHIP)], axis=0)
    w_in1 = jnp.concatenate([per_chip[k][2] for k in range(N_CHIP)], axis=1)
    w_out1 = jnp.concatenate([per_chip[k][3] for k in range(N_CHIP)], axis=0)
    w_cat0 = jnp.concatenate([w_in0[:, qkv_w + B_HEADS:], w_in0[:, :qkv_w + B_HEADS],
                              jnp.zeros((d, n0 - n_in), BF16)], axis=1)

    proj0, h0 = _norm_proj(x, norm_g[0:1], scale[0], shift[0], w_cat0, "norm_proj0")
    o_a = mixw
    aq = _to_heads(proj0[:, :, o_a:o_a + aw].astype(BF16), A_Q_HEADS)
    ak = _to_heads(proj0[:, :, o_a + aw:o_a + aw + akv].astype(BF16), A_KV_HEADS)
    av = _to_heads(proj0[:, :, o_a + aw + akv:o_a + aw + 2 * akv].astype(BF16), A_KV_HEADS)
    o_b = o_a + aw + 2 * akv
    bq_tok = proj0[:, :, o_b:o_b + bw].astype(BF16)
    bk_tok = proj0[:, :, o_b + bw:o_b + 2 * bw].astype(BF16)
    bv_tok = proj0[:, :, o_b + 2 * bw:o_b + 3 * bw].astype(BF16)
    bq, bk, bv = (_to_heads(t, B_HEADS) for t in (bq_tok, bk_tok, bv_tok))
    bkt, bvt = _to_heads_t(bk_tok, B_HEADS), _to_heads_t(bv_tok, B_HEADS)
    bqtb = _to_blocks_t(bq_tok, B_HEADS)
    zf = proj0[:, :, o_b + 3 * bw:]
    bucket_np, valid_np = _rel_buckets()
    bucket = jnp.asarray(bucket_np)
    bias = _swa_bias(rel_bias.T, bucket, jnp.asarray(valid_np), "swa_bias")
    sinks = attn_sinks[0]
    a_out, a_lse = _swa_fwd(aq, ak, av, bias, sinks, "swa_fwd")
    bf_pad = jnp.pad(attn_b_f, ((0, 0), (0, LANE - B_HEADS)))
    fsum = _fox_decay(zf, bf_pad, "fox_decay")
    fh = fsum[:, :, :B_HEADS].transpose(0, 2, 1)
    fcol = fh.reshape(bl, B_HEADS, s, 1)
    frow = fh.reshape(bl, B_HEADS, 1, s)
    frowb = fh.reshape(bl, B_HEADS, nb, 1, BLK)
    b_out, b_lse = _fox_fwd(bq, bkt, bv, fcol, frow, "fox_fwd")
    mix0 = jnp.concatenate([_from_heads(a_out), _from_heads(b_out)], axis=-1)
    x1, o0 = _gate_outproj(mix0, proj0, 0, w_out0, x, gmod[0], "gate_outproj0")

    proj1, h1 = _norm_proj(x1, norm_g[1:2], scale[1], shift[1], w_in1, "norm_proj1")
    vec_rows = jnp.concatenate([lru_conv_w[0], lru_conv_b, lru_b_a, lru_b_x, lru_lambda], axis=0)
    vec_all = _all_gather8(vec_rows, "gather_lru_vectors", pltpu.VMEM)
    vec_all = vec_all.reshape(N_CHIP, 2, SUBLANE, lw // N_CHIP)[:, 0]
    vec_all = vec_all.transpose(1, 0, 2).reshape(SUBLANE, lw)
    cw_f, cb_f, ba_f, bx_f, lam_f = vec_all[0:4], vec_all[4:5], vec_all[5:6], vec_all[6:7], vec_all[7:8]
    hs = _lru_fwd(proj1, cw_f, cb_f, lru_w_a[0], ba_f, lru_w_x[0], bx_f, lam_f, "lru_fwd")
    x2, o1 = _gate_outproj(hs, proj1, 1, w_out1, x1, gmod[1], "gate_outproj1")

    loss_vec, dx2, g_final = _final_loss(x2, final_g.reshape(1, d), loss_target, "final_loss")
    loss = lax.psum(loss_vec[0, 0], ("x", "y", "c"))

    dhs, dgate1, do1, y1, dgm1 = _bwd_out(dx2, gmod[1], o1, hs, proj1, 1, w_out1.T, "bwd_out1")
    g_w_out1 = _matmul_tn(y1.reshape(bl * s, lw), do1.reshape(bl * s, d), "grad_w_out1")
    (dxr, g_cw, g_cb, g_wa, g_ba, g_wx, g_bx, g_lam) = _lru_bwd(
        proj1, hs, dhs, cw_f, cb_f, lru_w_a[0], ba_f, lru_w_x[0], bx_f, lam_f, "lru_bwd")
    dproj1 = jnp.concatenate([dxr, dgate1], axis=-1)
    g_w_in1 = _matmul_tn(h1.reshape(bl * s, d), dproj1.reshape(bl * s, 2 * lw), "grad_w_in1")
    dx1, dsh1, dsc1, g_ng1 = _bwd_in(dproj1, w_in1.T, x1, norm_g[1:2], scale[1], dx2, "bwd_in1")

    dmix0, dgate0, do0, y0, dgm0 = _bwd_out(dx1, gmod[0], o0, mix0, proj0, 0, w_out0.T, "bwd_out0")
    g_w_out0 = _matmul_tn(y0.reshape(bl * s, mixw), do0.reshape(bl * s, d), "grad_w_out0")
    da_out = _to_heads(dmix0[:, :, :aw], A_Q_HEADS)
    db_tok = dmix0[:, :, aw:].astype(BF16)
    db_out, db_outtb = _to_heads(db_tok, B_HEADS), _to_blocks_t(db_tok, B_HEADS)
    daq, dak, dav, dbias, dsink = _swa_bwd(aq, ak, av, bias, sinks, da_out, a_lse, "swa_bwd")
    dbq, dbk, dbv, dfrow = _fox_bwd(bq, bqtb, bk, bkt, bv, bvt, fcol, frow, frowb, db_out, db_outtb, b_lse,
                                    b_lse.reshape(bl, B_HEADS, nb, 1, BLK), "fox_bwd")
    df = dfrow.reshape(bl, B_HEADS, s).transpose(0, 2, 1)
    df = jnp.pad(df, ((0, 0), (0, 0), (0, LANE - B_HEADS)))
    dzf, g_bf = _fox_dgate(df, zf, bf_pad, B_HEADS, "fox_dgate")
    dproj0 = jnp.concatenate(
        [dgate0] + [_from_heads(t).astype(BF16) for t in (daq, dak, dav, dbq, dbk, dbv)] + [dzf.astype(BF16)],
        axis=-1)
    g_w_cat0 = _matmul_tn(h0.reshape(bl * s, d), dproj0.reshape(bl * s, n0), "grad_w_in0")
    g_w_in0 = jnp.concatenate([g_w_cat0[:, mixw:mixw + qkv_w + B_HEADS], g_w_cat0[:, :mixw]], axis=1)
    dx0, dsh0, dsc0, g_ng0 = _bwd_in(dproj0, w_cat0.T, x, norm_g[0:1], scale[0], dx1, "bwd_in0")
    g_relb, g_sink = _swa_small_grads(dbias, dsink, bucket, "swa_small_grads")

    dmod = jnp.concatenate([jnp.concatenate([dsh0, dsc0, dgm0], axis=-1),
                            jnp.concatenate([dsh1, dsc1, dgm1], axis=-1)], axis=1)
    dmod_all = _all_gather8(_pad_rows(dmod.reshape(bl, 6 * d), SUBLANE), "gather_dmod", pltpu.VMEM)
    dmod_all = dmod_all.reshape(N_DEV, rows_pad, 6 * d)[:, :bl].reshape(N_DEV * bl, 6 * d)
    dmod_chip = lax.dynamic_slice_in_dim(dmod_all.reshape(N_DEV * bl, 2, 3 * d), chip * ncol, ncol, axis=2)
    g_ada_w, g_ada_b = _ada_bwd(c_all, dmod_chip.transpose(1, 0, 2), dmod_all, "ada_bwd")
    g_ada_b = g_ada_b.reshape(2, 3 * d)

    big_grads = [g_w_in0, g_w_out0, g_w_in1, g_w_out1]

    def chip_shard(gfull, like, k):
        axis = 1 if like.shape[0] == gfull.shape[0] else 0
        width = like.shape[axis]
        return lax.slice_in_dim(gfull, k * width, (k + 1) * width, axis=axis)

    small_parts = [g_relb[:, :, 0].T, jnp.concatenate([g_ng0, g_ng1], axis=0), g_sink[:, 0, 0], g_bf[0, :B_HEADS],
                   g_wa, g_wx, g_final, g_cw, g_cb, g_ba, g_bx, g_lam]
    small_sizes = [int(np.prod(p.shape)) for p in small_parts]
    small_total = sum(small_sizes)
    piece_rows = -(-(-(-small_total // N_DEV) // 1024) // SUBLANE) * SUBLANE
    small_flat = jnp.concatenate([p.reshape(-1) for p in small_parts])
    small_flat = jnp.pad(small_flat, (0, N_DEV * piece_rows * 1024 - small_total))
    small_pieces = small_flat.reshape(N_DEV, piece_rows, 1024)
    big_pieces = jnp.stack([pack_halves([chip_shard(g, a, k) for g, a in zip(big_grads, big)], F32)
                            for k in range(N_CHIP)]).reshape(N_DEV, half_rows, 1024)
    pieces = jnp.concatenate([big_pieces, small_pieces], axis=1)
    slots = _all_to_all8(pieces, "exchange_grads")
    reduced = _sum_slots(slots, "sum_grads")
    both = _sibling_swap(reduced[:half_rows], "swap_halves")
    g_big = unpack_chip(both.reshape(-1))
    small_all = _all_gather8(reduced[half_rows:], "gather_small_grads", pltpu.VMEM).reshape(-1)
    g_small, off = [], 0
    for p, n in zip(small_parts, small_sizes):
        g_small.append(small_all[off:off + n].reshape(p.shape))
        off += n
    (g_rel_bias, g_norm_g, g_sinks, g_b_f, g_w_a, g_w_x, g_fin, g_cw_r, g_cb_r, g_ba_r, g_bx_r, g_lam_r) = g_small
    cw4 = lw // N_CHIP

    def my_cols(a):
        return lax.dynamic_slice_in_dim(a, chip * cw4, cw4, axis=1)

    grads = {
        "rel_bias": g_rel_bias, "norm_g": g_norm_g, "ada_w": g_ada_w, "ada_b": g_ada_b,
        "attn_w_in": g_big[0][None], "attn_sinks": g_sinks[None], "attn_b_f": g_b_f[None],
        "attn_w_out": g_big[1][None], "lru_w_in": g_big[2][None], "lru_conv_w": my_cols(g_cw_r)[None],
        "lru_conv_b": my_cols(g_cb_r), "lru_w_a": g_w_a[None], "lru_b_a": my_cols(g_ba_r),
        "lru_w_x": g_w_x[None], "lru_b_x": my_cols(g_bx_r), "lru_lambda": my_cols(g_lam_r),
        "lru_w_out": g_big[3][None], "final_g": g_fin.reshape(d),
    }
    weights = dict(rel_bias=rel_bias, norm_g=norm_g, ada_w=ada_w, ada_b=ada_b, attn_w_in=attn_w_in,
                   attn_sinks=attn_sinks, attn_b_f=attn_b_f, attn_w_out=attn_w_out, lru_w_in=lru_w_in,
                   lru_conv_w=lru_conv_w, lru_conv_b=lru_conv_b, lru_w_a=lru_w_a, lru_b_a=lru_b_a,
                   lru_w_x=lru_w_x, lru_b_x=lru_b_x, lru_lambda=lru_lambda, lru_w_out=lru_w_out, final_g=final_g)
    moms = dict(rel_bias=(m_rel_bias, v_rel_bias), norm_g=(m_norm_g, v_norm_g), ada_w=(m_ada_w, v_ada_w),
                ada_b=(m_ada_b, v_ada_b), attn_w_in=(m_attn_w_in, v_attn_w_in),
                attn_sinks=(m_attn_sinks, v_attn_sinks), attn_b_f=(m_attn_b_f, v_attn_b_f),
                attn_w_out=(m_attn_w_out, v_attn_w_out), lru_w_in=(m_lru_w_in, v_lru_w_in),
                lru_conv_w=(m_lru_conv_w, v_lru_conv_w), lru_conv_b=(m_lru_conv_b, v_lru_conv_b),
                lru_w_a=(m_lru_w_a, v_lru_w_a), lru_b_a=(m_lru_b_a, v_lru_b_a), lru_w_x=(m_lru_w_x, v_lru_w_x),
                lru_b_x=(m_lru_b_x, v_lru_b_x), lru_lambda=(m_lru_lambda, v_lru_lambda),
                lru_w_out=(m_lru_w_out, v_lru_w_out), final_g=(m_final_g, v_final_g))
    names = list(weights)
    big_names = [n for n in names if weights[n].size >= 65536]
    small_names = [n for n in names if weights[n].size < 65536]
    delta, new_m, new_v = {}, {}, {}
    for n in big_names:
        delta[n], new_m[n], new_v[n] = _adamw(weights[n], grads[n].reshape(weights[n].shape),
                                              moms[n][0], moms[n][1], "adamw_" + n)
    cat = lambda arrs: jnp.concatenate([a.reshape(-1) for a in arrs])
    sd, sm, sv = _adamw(cat([weights[n] for n in small_names]), cat([grads[n] for n in small_names]),
                        cat([moms[n][0] for n in small_names]), cat([moms[n][1] for n in small_names]),
                        "adamw_small")
    off = 0
    for n in small_names:
        sz = weights[n].size
        shp = weights[n].shape
        delta[n], new_m[n], new_v[n] = (sd[off:off + sz].reshape(shp), sm[off:off + sz].reshape(shp),
                                        sv[off:off + sz].reshape(shp))
        off += sz
    out_grads = [grads[n].reshape(weights[n].shape) for n in names]
    return (loss, dx0, *out_grads, *[delta[n] for n in names], *[new_m[n] for n in names],
            *[new_v[n] for n in names])
```

```python
import functools
import math

import numpy as np
import jax
import jax.numpy as jnp
from jax import lax
from jax.experimental import pallas as pl
from jax.experimental.pallas import tpu as pltpu

F32 = jnp.float32
BF16 = jnp.bfloat16
MESH = pl.DeviceIdType.MESH

N_DEV = 8
N_CHIP = 4
HEAD_DIM = 64
BLK = 128
A_Q_HEADS = 8
A_KV_HEADS = 2
A_GROUP = A_Q_HEADS // A_KV_HEADS
B_HEADS = 8
REL_BUCKETS = 32
REL_MAX_EXACT = 16
REL_MAX_DIST = 128
LRU_BLOCKS = 8
LRU_C = 8.0
CONV_WIDTH = 4
EPS = 1e-6
NEG = -1e30
SCALE = HEAD_DIM ** -0.5
LANE = 128
SUBLANE = 8
VMEM_LIMIT = 56 * 1024 * 1024
SCAN_CHUNK = 256
ADAM_LR = 0.001
ADAM_B1 = 0.9
ADAM_B2 = 0.999
ADAM_EPS = 1e-08
ADAM_WD = 0.01
ADAM_STEP = 10
HI = lax.Precision.HIGHEST


def _cp(sem=None):
    return pltpu.CompilerParams(dimension_semantics=sem, vmem_limit_bytes=VMEM_LIMIT)


def _dot(a, b):
    return jnp.dot(a, b, preferred_element_type=F32)


def _dot_nt(a, b):
    return lax.dot_general(a, b, (((1,), (1,)), ((), ())), preferred_element_type=F32)


def _dot_tn(a, b):
    return lax.dot_general(a, b, (((0,), (0,)), ((), ())), preferred_element_type=F32)


def _sigmoid(z):
    return 1.0 / (1.0 + jnp.exp(-z))


def _row_tile(rows, cap):
    if rows <= cap:
        return rows
    best = SUBLANE
    t = SUBLANE
    while t <= cap:
        if rows % t == 0:
            best = t
        t += SUBLANE
    return best


def _all_gather8(x_shard, name, space):
    m_per, n = x_shard.shape

    def body(x_ref, out_ref, send_sems, recv_sems, local_sem):
        x, y, c = lax.axis_index("x"), lax.axis_index("y"), lax.axis_index("c")
        me, sibling = (x, y, c), (x, y, 1 - c)
        chips = [(1 - x, y), (x, 1 - y), (1 - x, 1 - y)]

        def rows(px, py, pc):
            return out_ref.at[pl.ds((4 * px + 2 * py + pc) * m_per, m_per), :]

        def copy(k, block, to, src=None):
            return pltpu.make_async_remote_copy(
                src_ref=rows(*block) if src is None else src, dst_ref=rows(*block),
                send_sem=send_sems.at[k], recv_sem=recv_sems.at[k], device_id=to, device_id_type=MESH)

        mine = pltpu.make_async_copy(x_ref, rows(*me), local_sem)
        mine.start()
        first = [copy(0, me, sibling, src=x_ref)]
        first += [copy(1 + j, me, (*chip, c), src=x_ref) for j, chip in enumerate(chips)]
        for cp in first:
            cp.start()
        passed = [copy(4 + j, (*chip, c), sibling) for j, chip in enumerate(chips)]
        for j, chip in enumerate(chips):
            copy(1 + j, (*chip, c), me).wait_recv()
            passed[j].start()
        copy(0, sibling, me).wait_recv()
        for j, chip in enumerate(chips):
            copy(4 + j, (*chip, 1 - c), me).wait_recv()
        for cp in first + passed:
            cp.wait_send()
        mine.wait()

    return pl.pallas_call(
        body, name=name,
        out_shape=jax.ShapeDtypeStruct((N_DEV * m_per, n), x_shard.dtype),
        in_specs=[pl.BlockSpec(memory_space=space)],
        out_specs=pl.BlockSpec(memory_space=space),
        scratch_shapes=[pltpu.SemaphoreType.DMA((7,)), pltpu.SemaphoreType.DMA((7,)), pltpu.SemaphoreType.DMA],
        compiler_params=pltpu.CompilerParams(vmem_limit_bytes=VMEM_LIMIT),
    )(x_shard)


def _all_to_all8(pieces, name):
    _, m, n = pieces.shape

    def body(x_ref, out_ref, send_sems, recv_sems, local_sem):
        x, y, c = lax.axis_index("x"), lax.axis_index("y"), lax.axis_index("c")
        me = 4 * x + 2 * y + c
        mine = pltpu.make_async_copy(x_ref.at[me], out_ref.at[me], local_sem)
        mine.start()
        copies = []
        for k in range(1, N_DEV):
            fx, fy, fc = (k >> 2) & 1, (k >> 1) & 1, k & 1
            px, py, pc = x ^ fx, y ^ fy, c ^ fc
            peer = 4 * px + 2 * py + pc
            copies.append(pltpu.make_async_remote_copy(
                src_ref=x_ref.at[peer], dst_ref=out_ref.at[me],
                send_sem=send_sems.at[k - 1], recv_sem=recv_sems.at[k - 1],
                device_id=(px, py, pc), device_id_type=MESH))
        for cp in copies:
            cp.start()
        for cp in copies:
            cp.wait_recv()
        for cp in copies:
            cp.wait_send()
        mine.wait()

    hbm = pl.BlockSpec(memory_space=pltpu.HBM)
    return pl.pallas_call(
        body, name=name,
        out_shape=jax.ShapeDtypeStruct(pieces.shape, pieces.dtype),
        in_specs=[hbm], out_specs=hbm,
        scratch_shapes=[pltpu.SemaphoreType.DMA((7,)), pltpu.SemaphoreType.DMA((7,)), pltpu.SemaphoreType.DMA],
    )(pieces)


def _sibling_swap(half, name):
    m, n = half.shape

    def body(x_ref, out_ref, send_sem, recv_sem, local_sem):
        x, y, c = lax.axis_index("x"), lax.axis_index("y"), lax.axis_index("c")
        mine = pltpu.make_async_copy(x_ref, out_ref.at[c], local_sem)
        mine.start()
        cp = pltpu.make_async_remote_copy(
            src_ref=x_ref, dst_ref=out_ref.at[c], send_sem=send_sem, recv_sem=recv_sem,
            device_id=(x, y, 1 - c), device_id_type=MESH)
        cp.start()
        cp.wait_recv()
        cp.wait_send()
        mine.wait()

    hbm = pl.BlockSpec(memory_space=pltpu.HBM)
    return pl.pallas_call(
        body, name=name,
        out_shape=jax.ShapeDtypeStruct((2, m, n), half.dtype),
        in_specs=[hbm], out_specs=hbm,
        scratch_shapes=[pltpu.SemaphoreType.DMA, pltpu.SemaphoreType.DMA, pltpu.SemaphoreType.DMA],
    )(half)


def _sum_slots(slots, name):
    k, m, n = slots.shape
    tr = _row_tile(m, 328)

    def body(s_ref, o_ref):
        acc = s_ref[0]
        for j in range(1, k):
            acc = acc + s_ref[j]
        o_ref[...] = acc

    return pl.pallas_call(
        body, name=name, grid=(m // tr,),
        out_shape=jax.ShapeDtypeStruct((m, n), F32),
        in_specs=[pl.BlockSpec((k, tr, n), lambda i: (0, i, 0))],
        out_specs=pl.BlockSpec((tr, n), lambda i: (i, 0)),
        compiler_params=_cp(("parallel",)),
    )(slots)


def _ada_fwd(c_all, w, b, name):
    r, _ = c_all.shape
    n = w.shape[1]

    def body(c_ref, w_ref, b_ref, o_ref):
        cv = c_ref[...]
        act = cv * _sigmoid(cv)
        o_ref[...] = jnp.dot(act, w_ref[...], precision=HI, preferred_element_type=F32) + b_ref[...]

    return pl.pallas_call(body, name=name, out_shape=jax.ShapeDtypeStruct((r, n), F32),
                          compiler_params=_cp())(c_all, w, b)


def _ada_bwd(c_all, dmod_chip, dmod_all, name):
    r, d = c_all.shape
    nl, _, n = dmod_chip.shape

    def body(c_ref, dm_ref, da_ref, gw_ref, gb_ref):
        cv = c_ref[...]
        act = cv * _sigmoid(cv)
        for l in range(nl):
            gw_ref[l] = lax.dot_general(act, dm_ref[l], (((0,), (0,)), ((), ())), precision=HI,
                                        preferred_element_type=F32)
        gb_ref[...] = jnp.sum(da_ref[...], axis=0, keepdims=True)

    return pl.pallas_call(
        body, name=name,
        out_shape=(jax.ShapeDtypeStruct((nl, d, n), F32), jax.ShapeDtypeStruct((1, dmod_all.shape[1]), F32)),
        compiler_params=_cp())(c_all, dmod_chip, dmod_all)


def _norm_proj(x, g, scale, shift, w, name):
    b, s, d = x.shape
    n = w.shape[1]
    tm = min(s, 256)

    def body(x_ref, g_ref, sc_ref, sh_ref, w_ref, proj_ref, h_ref):
        xv = x_ref[...]
        rstd = lax.rsqrt(jnp.mean(xv * xv, axis=-1, keepdims=True) + EPS)
        h = (xv * rstd) * g_ref[...] * (1.0 + sc_ref[...]) + sh_ref[...]
        hb = h.astype(BF16)
        h_ref[...] = hb
        proj_ref[...] = _dot(hb, w_ref[...])

    return pl.pallas_call(
        body, name=name, grid=(b, s // tm),
        out_shape=(jax.ShapeDtypeStruct((b, s, n), F32), jax.ShapeDtypeStruct((b, s, d), BF16)),
        in_specs=[pl.BlockSpec((None, tm, d), lambda i, j: (i, j, 0)),
                  pl.BlockSpec((1, d), lambda i, j: (0, 0)),
                  pl.BlockSpec((None, 1, d), lambda i, j: (i, 0, 0)),
                  pl.BlockSpec((None, 1, d), lambda i, j: (i, 0, 0)),
                  pl.BlockSpec((d, n), lambda i, j: (0, 0))],
        out_specs=(pl.BlockSpec((None, tm, n), lambda i, j: (i, j, 0)),
                   pl.BlockSpec((None, tm, d), lambda i, j: (i, j, 0))),
        compiler_params=_cp(("parallel", "parallel")),
    )(x, g, scale, shift, w)


def _cat_refs(refs):
    vals = [r[...] for r in refs]
    return vals[0] if len(vals) == 1 else jnp.concatenate(vals, axis=-1)


def _gate_outproj(mix_parts, proj, gate_blk, w_out, x, gmod, name):
    b, s, _ = x.shape
    wd, d = w_out.shape
    tm = min(s, 256)
    npart = len(mix_parts)

    def body(*refs):
        mix_refs = refs[:npart]
        gate_ref, w_ref, x_ref, gm_ref, xo_ref, o_ref = refs[npart:]
        gt = gate_ref[...]
        y = (_cat_refs(mix_refs) * (gt * _sigmoid(gt))).astype(BF16)
        o = _dot(y, w_ref[...])
        o_ref[...] = o.astype(BF16)
        xo_ref[...] = x_ref[...] + gm_ref[...] * o

    return pl.pallas_call(
        body, name=name, grid=(b, s // tm),
        out_shape=(jax.ShapeDtypeStruct((b, s, d), F32), jax.ShapeDtypeStruct((b, s, d), BF16)),
        in_specs=[pl.BlockSpec((None, tm, p.shape[2]), lambda i, j: (i, j, 0)) for p in mix_parts] + [
                  pl.BlockSpec((None, tm, wd), lambda i, j: (i, j, gate_blk)),
                  pl.BlockSpec((wd, d), lambda i, j: (0, 0)),
                  pl.BlockSpec((None, tm, d), lambda i, j: (i, j, 0)),
                  pl.BlockSpec((None, 1, d), lambda i, j: (i, 0, 0))],
        out_specs=(pl.BlockSpec((None, tm, d), lambda i, j: (i, j, 0)),
                   pl.BlockSpec((None, tm, d), lambda i, j: (i, j, 0))),
        compiler_params=_cp(("parallel", "parallel")),
    )(*mix_parts, proj, w_out, x, gmod)


def _final_loss(x, g, target, name):
    b, s, d = x.shape
    tm = min(s, 256)

    def body(x_ref, g_ref, t_ref, loss_ref, dx_ref, dg_ref):
        first = jnp.logical_and(pl.program_id(0) == 0, pl.program_id(1) == 0)

        @pl.when(first)
        def _():
            loss_ref[...] = jnp.zeros_like(loss_ref)
            dg_ref[...] = jnp.zeros_like(dg_ref)

        xv = x_ref[...]
        gv = g_ref[...]
        rstd = lax.rsqrt(jnp.mean(xv * xv, axis=-1, keepdims=True) + EPS)
        xhat = xv * rstd
        err = xhat * gv - t_ref[...]
        row = jnp.mean(err * err, axis=-1, keepdims=True)
        loss_ref[...] += 0.5 * jnp.sum(row, axis=0, keepdims=True)
        dy = err * (1.0 / d)
        dg_ref[...] += jnp.sum(dy * xhat, axis=0, keepdims=True)
        dxh = dy * gv
        dx_ref[...] = rstd * (dxh - xhat * jnp.mean(dxh * xhat, axis=-1, keepdims=True))

    return pl.pallas_call(
        body, name=name, grid=(b, s // tm),
        out_shape=(jax.ShapeDtypeStruct((1, LANE), F32), jax.ShapeDtypeStruct((b, s, d), F32),
                   jax.ShapeDtypeStruct((1, d), F32)),
        in_specs=[pl.BlockSpec((None, tm, d), lambda i, j: (i, j, 0)),
                  pl.BlockSpec((1, d), lambda i, j: (0, 0)),
                  pl.BlockSpec((None, tm, d), lambda i, j: (i, j, 0))],
        out_specs=(pl.BlockSpec((1, LANE), lambda i, j: (0, 0)),
                   pl.BlockSpec((None, tm, d), lambda i, j: (i, j, 0)),
                   pl.BlockSpec((1, d), lambda i, j: (0, 0))),
        compiler_params=_cp(("arbitrary", "arbitrary")),
    )(x, g, target)


def _bwd_out(dxo, gmod, o, mix_parts, proj, gate_blk, w_out_t, name):
    b, s, d = dxo.shape
    wd = w_out_t.shape[1]
    tm = min(s, 256)
    npart = len(mix_parts)

    def body(dx_ref, gm_ref, o_ref, *refs):
        mix_refs = refs[:npart]
        gate_ref, wt_ref, dmix_ref, dgate_ref, do_ref, y_ref, dgm_ref = refs[npart:]

        @pl.when(pl.program_id(1) == 0)
        def _():
            dgm_ref[...] = jnp.zeros_like(dgm_ref)

        dx = dx_ref[...]
        dgm_ref[...] += jnp.sum(dx * o_ref[...].astype(F32), axis=0, keepdims=True)
        dob = (gm_ref[...] * dx).astype(BF16)
        do_ref[...] = dob
        dy = _dot(dob, wt_ref[...])
        gt = gate_ref[...]
        sg = _sigmoid(gt)
        silu = gt * sg
        mx = _cat_refs(mix_refs)
        y_ref[...] = (mx * silu).astype(BF16)
        dmix_ref[...] = dy * silu
        dgate_ref[...] = (dy * mx * (sg * (1.0 + gt * (1.0 - sg)))).astype(BF16)

    row = lambda i, j: (i, j, 0)
    return pl.pallas_call(
        body, name=name, grid=(b, s // tm),
        out_shape=(jax.ShapeDtypeStruct((b, s, wd), F32), jax.ShapeDtypeStruct((b, s, wd), BF16),
                   jax.ShapeDtypeStruct((b, s, d), BF16), jax.ShapeDtypeStruct((b, s, wd), BF16),
                   jax.ShapeDtypeStruct((b, 1, d), F32)),
        in_specs=[pl.BlockSpec((None, tm, d), row),
                  pl.BlockSpec((None, 1, d), lambda i, j: (i, 0, 0)),
                  pl.BlockSpec((None, tm, d), row)] + [
                  pl.BlockSpec((None, tm, p.shape[2]), row) for p in mix_parts] + [
                  pl.BlockSpec((None, tm, wd), lambda i, j: (i, j, gate_blk)),
                  pl.BlockSpec((d, wd), lambda i, j: (0, 0))],
        out_specs=(pl.BlockSpec((None, tm, wd), row), pl.BlockSpec((None, tm, wd), row),
                   pl.BlockSpec((None, tm, d), row), pl.BlockSpec((None, tm, wd), row),
                   pl.BlockSpec((None, 1, d), lambda i, j: (i, 0, 0))),
        compiler_params=_cp(("parallel", "arbitrary")),
    )(dxo, gmod, o, *mix_parts, proj, w_out_t)


def _bwd_in(dproj, w_in_t, x, g, scale, dxo, name):
    b, s, d = x.shape
    n = dproj.shape[2]
    tm = min(s, 256)

    def body(dp_ref, wt_ref, x_ref, g_ref, sc_ref, dxo_ref, dx_ref, dsh_ref, dsc_ref, dg_ref):
        @pl.when(jnp.logical_and(pl.program_id(0) == 0, pl.program_id(1) == 0))
        def _():
            dg_ref[...] = jnp.zeros_like(dg_ref)

        @pl.when(pl.program_id(1) == 0)
        def _():
            dsh_ref[...] = jnp.zeros_like(dsh_ref)
            dsc_ref[...] = jnp.zeros_like(dsc_ref)

        dh = _dot(dp_ref[...], wt_ref[...])
        xv = x_ref[...]
        gv = g_ref[...]
        one_sc = 1.0 + sc_ref[...]
        rstd = lax.rsqrt(jnp.mean(xv * xv, axis=-1, keepdims=True) + EPS)
        xhat = xv * rstd
        dsh_ref[...] += jnp.sum(dh, axis=0, keepdims=True)
        dsc_ref[...] += jnp.sum(dh * (xhat * gv), axis=0, keepdims=True)
        dhs = dh * one_sc
        dg_ref[...] += jnp.sum(dhs * xhat, axis=0, keepdims=True)
        dxh = dhs * gv
        dx_ref[...] = dxo_ref[...] + rstd * (dxh - xhat * jnp.mean(dxh * xhat, axis=-1, keepdims=True))

    row = lambda i, j: (i, j, 0)
    per_b = lambda i, j: (i, 0, 0)
    return pl.pallas_call(
        body, name=name, grid=(b, s // tm),
        out_shape=(jax.ShapeDtypeStruct((b, s, d), F32), jax.ShapeDtypeStruct((b, 1, d), F32),
                   jax.ShapeDtypeStruct((b, 1, d), F32), jax.ShapeDtypeStruct((1, d), F32)),
        in_specs=[pl.BlockSpec((None, tm, n), row),
                  pl.BlockSpec((n, d), lambda i, j: (0, 0)),
                  pl.BlockSpec((None, tm, d), row),
                  pl.BlockSpec((1, d), lambda i, j: (0, 0)),
                  pl.BlockSpec((None, 1, d), per_b),
                  pl.BlockSpec((None, tm, d), row)],
        out_specs=(pl.BlockSpec((None, tm, d), row), pl.BlockSpec((None, 1, d), per_b),
                   pl.BlockSpec((None, 1, d), per_b), pl.BlockSpec((1, d), lambda i, j: (0, 0))),
        compiler_params=_cp(("arbitrary", "arbitrary")),
    )(dproj, w_in_t, x, g, scale, dxo)


def _matmul_tn(a, bm, name):
    t, m = a.shape
    n = bm.shape[1]
    tk = next(c for c in (512, 256, 128) if t % c == 0)
    tn = n
    for cand in (1152, 1024, 896, 768, 640, 512, 384, 256, 128):
        if n % cand == 0:
            tn = cand
            break
    nk = t // tk

    def body(a_ref, b_ref, o_ref, acc_ref):
        k = pl.program_id(1)

        @pl.when(k == 0)
        def _():
            acc_ref[...] = jnp.zeros_like(acc_ref)

        acc_ref[...] += _dot_tn(a_ref[...], b_ref[...])

        @pl.when(k == nk - 1)
        def _():
            o_ref[...] = acc_ref[...]

    return pl.pallas_call(
        body, name=name, grid=(n // tn, nk),
        out_shape=jax.ShapeDtypeStruct((m, n), F32),
        in_specs=[pl.BlockSpec((tk, m), lambda j, k: (k, 0)),
                  pl.BlockSpec((tk, tn), lambda j, k: (k, j))],
        out_specs=pl.BlockSpec((m, tn), lambda j, k: (0, j)),
        scratch_shapes=[pltpu.VMEM((m, tn), F32)],
        compiler_params=_cp(("parallel", "arbitrary")),
    )(a, bm)


def _rel_buckets():
    qi = np.arange(BLK)[:, None]
    kj = np.arange(2 * BLK)[None, :]
    rel = qi - kj + BLK
    n = np.maximum(rel, 0)
    nf = np.maximum(n, 1).astype(np.float32)
    large = REL_MAX_EXACT + (np.log(nf / REL_MAX_EXACT) / math.log(REL_MAX_DIST / REL_MAX_EXACT)
                             * (REL_BUCKETS - REL_MAX_EXACT)).astype(np.int32)
    large = np.minimum(large, REL_BUCKETS - 1)
    bucket = np.where(n < REL_MAX_EXACT, n, large).astype(np.int32)
    valid = ((rel >= 0) & (rel < BLK)).astype(np.int32)
    return bucket, valid


def _swa_bias(rel_bias_t, bucket, valid, name):
    nh = rel_bias_t.shape[0]

    def body(rb_ref, bk_ref, vl_ref, o_ref):
        h = pl.program_id(0)
        bk = bk_ref[...]
        acc = jnp.zeros(bk.shape, F32)
        for i in range(REL_BUCKETS):
            acc = jnp.where(bk == i, rb_ref[h, i], acc)
        o_ref[...] = jnp.where(vl_ref[...] > 0, acc, NEG)

    return pl.pallas_call(
        body, name=name, grid=(nh,),
        out_shape=jax.ShapeDtypeStruct((nh, BLK, 2 * BLK), F32),
        in_specs=[pl.BlockSpec(memory_space=pltpu.SMEM),
                  pl.BlockSpec((BLK, 2 * BLK), lambda h: (0, 0)),
                  pl.BlockSpec((BLK, 2 * BLK), lambda h: (0, 0))],
        out_specs=pl.BlockSpec((None, BLK, 2 * BLK), lambda h: (h, 0, 0)),
        compiler_params=_cp(("arbitrary",)),
    )(rel_bias_t, bucket, valid)


def _swa_scores(n, qn, kc, kp, bias_ref):
    s_c = _dot_nt(qn, kc) * SCALE + bias_ref[:, BLK:]
    s_p = _dot_nt(qn, kp) * SCALE + bias_ref[:, :BLK]
    s_p = jnp.where(n > 0, s_p, NEG)
    return s_c, s_p


def _swa_fwd(q, k, v, bias, sinks, name):
    b, hq, s, hd = q.shape
    nb = s // BLK

    def body(sink_ref, q_ref, k_ref, v_ref, bias_ref, o_ref, l_ref):
        sink = sink_ref[pl.program_id(1)]

        def step(n, carry):
            r0 = pl.multiple_of(n * BLK, BLK)
            p0 = pl.multiple_of(jnp.maximum(n - 1, 0) * BLK, BLK)
            qn = q_ref[pl.ds(r0, BLK), :]
            s_c, s_p = _swa_scores(n, qn, k_ref[pl.ds(r0, BLK), :], k_ref[pl.ds(p0, BLK), :], bias_ref)
            m = jnp.maximum(jnp.maximum(jnp.max(s_c, axis=1, keepdims=True),
                                        jnp.max(s_p, axis=1, keepdims=True)), sink)
            e_c = jnp.exp(s_c - m)
            e_p = jnp.exp(s_p - m)
            den = jnp.sum(e_c, axis=1, keepdims=True) + jnp.sum(e_p, axis=1, keepdims=True) + jnp.exp(sink - m)
            inv = 1.0 / den
            o = _dot((e_c * inv).astype(BF16), v_ref[pl.ds(r0, BLK), :])
            o = o + _dot((e_p * inv).astype(BF16), v_ref[pl.ds(p0, BLK), :])
            o_ref[pl.ds(r0, BLK), :] = o
            l_ref[pl.ds(r0, BLK), :] = m + jnp.log(den)
            return carry

        lax.fori_loop(0, nb, step, 0)

    qspec = pl.BlockSpec((None, None, s, hd), lambda i, h: (i, h, 0, 0))
    kspec = pl.BlockSpec((None, None, s, hd), lambda i, h: (i, h // A_GROUP, 0, 0))
    return pl.pallas_call(
        body, name=name, grid=(b, hq),
        out_shape=(jax.ShapeDtypeStruct((b, hq, s, hd), F32), jax.ShapeDtypeStruct((b, hq, s, 1), F32)),
        in_specs=[pl.BlockSpec(memory_space=pltpu.SMEM), qspec, kspec, kspec,
                  pl.BlockSpec((None, BLK, 2 * BLK), lambda i, h: (h, 0, 0))],
        out_specs=(qspec, pl.BlockSpec((None, None, s, 1), lambda i, h: (i, h, 0, 0))),
        compiler_params=_cp(("parallel", "parallel")),
    )(sinks, q, k, v, bias)


def _swa_bwd(q, k, v, bias, sinks, do, lse, name):
    b, hq, s, hd = q.shape
    hkv = k.shape[1]
    nb = s // BLK

    def body(sink_ref, q_ref, k_ref, v_ref, bias_ref, do_ref, l_ref,
             dq_ref, dk_ref, dv_ref, db_ref, dsk_ref):
        g = pl.program_id(2)
        sink = sink_ref[pl.program_id(1) * A_GROUP + g]

        @pl.when(g == 0)
        def _():
            dk_ref[...] = jnp.zeros_like(dk_ref)
            dv_ref[...] = jnp.zeros_like(dv_ref)

        db_ref[...] = jnp.zeros_like(db_ref)

        def step(n, dsink):
            r0 = pl.multiple_of(n * BLK, BLK)
            p0 = pl.multiple_of(jnp.maximum(n - 1, 0) * BLK, BLK)
            qn = q_ref[pl.ds(r0, BLK), :]
            kc = k_ref[pl.ds(r0, BLK), :]
            kp = k_ref[pl.ds(p0, BLK), :]
            s_c, s_p = _swa_scores(n, qn, kc, kp, bias_ref)
            ln = l_ref[pl.ds(r0, BLK), :]
            p_c = jnp.exp(s_c - ln)
            p_p = jnp.exp(s_p - ln)
            p_s = jnp.exp(sink - ln)
            dob = do_ref[pl.ds(r0, BLK), :].astype(BF16)
            dp_c = _dot_nt(dob, v_ref[pl.ds(r0, BLK), :])
            dp_p = _dot_nt(dob, v_ref[pl.ds(p0, BLK), :])
            delta = jnp.sum(p_c * dp_c, axis=1, keepdims=True) + jnp.sum(p_p * dp_p, axis=1, keepdims=True)
            ds_c = p_c * (dp_c - delta)
            ds_p = p_p * (dp_p - delta)
            db_ref[:, BLK:] += ds_c
            db_ref[:, :BLK] += ds_p
            dsb_c = ds_c.astype(BF16)
            dsb_p = ds_p.astype(BF16)
            dq_ref[pl.ds(r0, BLK), :] = (_dot(dsb_c, kc) + _dot(dsb_p, kp)) * SCALE
            dk_ref[pl.ds(r0, BLK), :] += _dot_tn(dsb_c, qn) * SCALE
            dk_ref[pl.ds(p0, BLK), :] += _dot_tn(dsb_p, qn) * SCALE
            dv_ref[pl.ds(r0, BLK), :] += _dot_tn(p_c.astype(BF16), dob)
            dv_ref[pl.ds(p0, BLK), :] += _dot_tn(p_p.astype(BF16), dob)
            return dsink - jnp.sum(p_s * delta, axis=0, keepdims=True)

        dsink = lax.fori_loop(0, nb, step, jnp.zeros((1, 1), F32))
        dsk_ref[...] = jnp.broadcast_to(dsink, dsk_ref.shape)

    qspec = pl.BlockSpec((None, None, s, hd), lambda i, kv, g: (i, kv * A_GROUP + g, 0, 0))
    kspec = pl.BlockSpec((None, None, s, hd), lambda i, kv, g: (i, kv, 0, 0))
    return pl.pallas_call(
        body, name=name, grid=(b, hkv, A_GROUP),
        out_shape=(jax.ShapeDtypeStruct((b, hq, s, hd), F32), jax.ShapeDtypeStruct((b, hkv, s, hd), F32),
                   jax.ShapeDtypeStruct((b, hkv, s, hd), F32), jax.ShapeDtypeStruct((b, hq, BLK, 2 * BLK), F32),
                   jax.ShapeDtypeStruct((b, hq, 1, LANE), F32)),
        in_specs=[pl.BlockSpec(memory_space=pltpu.SMEM), qspec, kspec, kspec,
                  pl.BlockSpec((None, BLK, 2 * BLK), lambda i, kv, g: (kv * A_GROUP + g, 0, 0)),
                  qspec,
                  pl.BlockSpec((None, None, s, 1), lambda i, kv, g: (i, kv * A_GROUP + g, 0, 0))],
        out_specs=(qspec, kspec, kspec,
                   pl.BlockSpec((None, None, BLK, 2 * BLK), lambda i, kv, g: (i, kv * A_GROUP + g, 0, 0)),
                   pl.BlockSpec((None, None, 1, LANE), lambda i, kv, g: (i, kv * A_GROUP + g, 0, 0))),
        compiler_params=_cp(("parallel", "parallel", "arbitrary")),
    )(sinks, q, k, v, bias, do, lse)


def _swa_small_grads(db, dsk, bucket, name):
    b, nh = db.shape[0], db.shape[1]

    def body(db_ref, dsk_ref, bk_ref, gb_ref, gs_ref):
        acc = db_ref[0]
        sk = dsk_ref[0]
        for i in range(1, b):
            acc = acc + db_ref[i]
            sk = sk + dsk_ref[i]
        gs_ref[...] = sk
        bk = bk_ref[...]
        for i in range(REL_BUCKETS):
            part = jnp.sum(jnp.where(bk == i, acc, 0.0), axis=1, keepdims=True)
            tot = jnp.sum(part, axis=0, keepdims=True)
            gb_ref[i:i + 1, :] = jnp.broadcast_to(tot, (1, LANE))

    return pl.pallas_call(
        body, name=name, grid=(nh,),
        out_shape=(jax.ShapeDtypeStruct((nh, REL_BUCKETS, LANE), F32), jax.ShapeDtypeStruct((nh, 1, LANE), F32)),
        in_specs=[pl.BlockSpec((b, None, BLK, 2 * BLK), lambda h: (0, h, 0, 0)),
                  pl.BlockSpec((b, None, 1, LANE), lambda h: (0, h, 0, 0)),
                  pl.BlockSpec((BLK, 2 * BLK), lambda h: (0, 0))],
        out_specs=(pl.BlockSpec((None, REL_BUCKETS, LANE), lambda h: (h, 0, 0)),
                   pl.BlockSpec((None, 1, LANE), lambda h: (h, 0, 0))),
        compiler_params=_cp(("parallel",)),
    )(db, dsk, bucket)


def _log_sigmoid(z):
    return jnp.minimum(z, 0.0) - jnp.log(1.0 + jnp.exp(-jnp.abs(z)))


def _fox_decay(z, bf, name):
    b, s, w = z.shape
    nb = s // BLK

    def body(z_ref, bf_ref, f_ref):
        r = lax.broadcasted_iota(jnp.int32, (BLK, BLK), 0)
        c = lax.broadcasted_iota(jnp.int32, (BLK, BLK), 1)
        tri = (c <= r).astype(F32)

        def step(n, carry):
            r0 = pl.multiple_of(n * BLK, BLK)
            lf = _log_sigmoid(z_ref[pl.ds(r0, BLK), :] + bf_ref[...])
            f_ref[pl.ds(r0, BLK), :] = jnp.dot(tri, lf, precision=HI, preferred_element_type=F32) + carry
            return carry + jnp.sum(lf, axis=0, keepdims=True)

        lax.fori_loop(0, nb, step, jnp.zeros((1, w), F32))

    spec = pl.BlockSpec((None, s, w), lambda i: (i, 0, 0))
    return pl.pallas_call(
        body, name=name, grid=(b,), out_shape=jax.ShapeDtypeStruct((b, s, w), F32),
        in_specs=[spec, pl.BlockSpec((1, w), lambda i: (0, 0))], out_specs=spec,
        compiler_params=_cp(("parallel",)),
    )(z, bf)


def _fox_dgate(df, z, bf, nheads, name):
    b, s, w = z.shape
    nb = s // BLK

    def body(df_ref, z_ref, bf_ref, dz_ref, dbf_ref):
        @pl.when(pl.program_id(0) == 0)
        def _():
            dbf_ref[...] = jnp.zeros_like(dbf_ref)

        r = lax.broadcasted_iota(jnp.int32, (BLK, BLK), 0)
        c = lax.broadcasted_iota(jnp.int32, (BLK, BLK), 1)
        tri = (c >= r).astype(F32)
        lane = lax.broadcasted_iota(jnp.int32, (BLK, w), 1)

        def step(i, carry):
            tail, dbf = carry
            r0 = pl.multiple_of((nb - 1 - i) * BLK, BLK)
            dfb = df_ref[pl.ds(r0, BLK), :]
            dlf = jnp.dot(tri, dfb, precision=HI, preferred_element_type=F32) + tail
            dz = jnp.where(lane < nheads, dlf * _sigmoid(-(z_ref[pl.ds(r0, BLK), :] + bf_ref[...])), 0.0)
            dz_ref[pl.ds(r0, BLK), :] = dz
            return tail + jnp.sum(dfb, axis=0, keepdims=True), dbf + jnp.sum(dz, axis=0, keepdims=True)

        zero = jnp.zeros((1, w), F32)
        _, dbf = lax.fori_loop(0, nb, step, (zero, zero))
        dbf_ref[...] += dbf

    spec = pl.BlockSpec((None, s, w), lambda i: (i, 0, 0))
    one = pl.BlockSpec((1, w), lambda i: (0, 0))
    return pl.pallas_call(
        body, name=name, grid=(b,),
        out_shape=(jax.ShapeDtypeStruct((b, s, w), F32), jax.ShapeDtypeStruct((1, w), F32)),
        in_specs=[spec, spec, one], out_specs=(spec, one),
        compiler_params=_cp(("arbitrary",)),
    )(df, z, bf)


def _fox_segments(nb):
    per = max(1, nb // 4)
    return per, nb // per


def _head_masks(shape, axis):
    idx = lax.broadcasted_iota(jnp.int32, shape, axis)
    return idx < HEAD_DIM, idx >= HEAD_DIM


def _fox_fwd(proj, qblk, kblk, vblk, fcol, frow, name):
    b, s, _ = proj.shape
    nh = fcol.shape[1]
    npair = nh // 2
    per, nseg = _fox_segments(s // BLK)

    def body(q_ref, k_ref, v_ref, fc_ref, fr_ref, o_ref, l_ref, qm_ref, kt_ref, vb_ref):
        lo, hi = _head_masks((s, LANE), 1)
        qv = q_ref[...]
        qm_ref[0] = jnp.where(lo, qv, 0.0).astype(BF16)
        qm_ref[1] = jnp.where(hi, qv, 0.0).astype(BF16)
        kt_ref[...] = k_ref[...].T.astype(BF16)
        vb_ref[...] = v_ref[...].astype(BF16)
        lane_lo = lax.broadcasted_iota(jnp.int32, (BLK, LANE), 1) < HEAD_DIM
        for seg in range(nseg):
            w = (seg + 1) * per * BLK
            causal = (lax.broadcasted_iota(jnp.int32, (BLK, w), 1)
                      - lax.broadcasted_iota(jnp.int32, (BLK, w), 0))

            def qstep(n, carry):
                r0 = pl.multiple_of(n * BLK, BLK)
                outs = []
                for hh in range(2):
                    sc = _dot(qm_ref[hh, pl.ds(r0, BLK), :], kt_ref[:, :w]) * SCALE
                    sc = sc + (fc_ref[hh, pl.ds(r0, BLK), :] - fr_ref[hh, :, :w])
                    sc = jnp.where(causal <= n * BLK, sc, NEG)
                    m = jnp.max(sc, axis=1, keepdims=True)
                    e = jnp.exp(sc - m)
                    l = jnp.sum(e, axis=1, keepdims=True)
                    outs.append(_dot((e * (1.0 / l)).astype(BF16), vb_ref[:w, :]))
                    l_ref[hh, pl.ds(r0, BLK), :] = m + jnp.log(l)
                o_ref[pl.ds(r0, BLK), :] = jnp.where(lane_lo, outs[0], outs[1])
                return carry

            lax.fori_loop(seg * per, (seg + 1) * per, qstep, 0)

    def tok(blk):
        return pl.BlockSpec((None, s, LANE), lambda i, p: (i, 0, blk + p))

    col = pl.BlockSpec((None, 2, s, 1), lambda i, p: (i, p, 0, 0))
    rowspec = pl.BlockSpec((None, 2, 1, s), lambda i, p: (i, p, 0, 0))
    return pl.pallas_call(
        body, name=name, grid=(b, npair),
        out_shape=(jax.ShapeDtypeStruct((b, s, nh * HEAD_DIM), F32), jax.ShapeDtypeStruct((b, nh, s, 1), F32)),
        in_specs=[tok(qblk), tok(kblk), tok(vblk), col, rowspec],
        out_specs=(pl.BlockSpec((None, s, LANE), lambda i, p: (i, 0, p)), col),
        scratch_shapes=[pltpu.VMEM((2, s, LANE), BF16), pltpu.VMEM((LANE, s), BF16), pltpu.VMEM((s, LANE), BF16)],
        compiler_params=_cp(("parallel", "parallel")),
    )(proj, proj, proj, fcol, frow)


def _fox_bwd(proj, qblk, kblk, vblk, dmix, doblk, fcol, frow, frowb, lse, lserowb, name):
    b, s, _ = proj.shape
    nh = fcol.shape[1]
    npair = nh // 2
    nb = s // BLK
    per, nseg = _fox_segments(nb)

    def body(q_ref, k_ref, v_ref, do_ref, fc_ref, fr_ref, frb_ref, l_ref, lrb_ref,
             dq_ref, dk_ref, dv_ref, dfr_ref,
             qm_ref, dom_ref, kb_ref, vb_ref, kt_ref, vt_ref, qtm_ref, dotm_ref, dka_ref, dva_ref):
        lo, hi = _head_masks((s, LANE), 1)
        qv = q_ref[...]
        dov = do_ref[...]
        for hh, msk in enumerate((lo, hi)):
            qm_ref[hh] = jnp.where(msk, qv, 0.0).astype(BF16)
            dom_ref[hh] = jnp.where(msk, dov, 0.0).astype(BF16)
        kv = k_ref[...]
        vv = v_ref[...]
        kb_ref[...] = kv.astype(BF16)
        vb_ref[...] = vv.astype(BF16)
        kt_ref[...] = kv.T.astype(BF16)
        vt_ref[...] = vv.T.astype(BF16)
        rlo, rhi = _head_masks((LANE, BLK), 0)

        def tstep(n, carry):
            r0 = pl.multiple_of(n * BLK, BLK)
            qt = q_ref[pl.ds(r0, BLK), :].T
            dt = do_ref[pl.ds(r0, BLK), :].T
            for hh, msk in enumerate((rlo, rhi)):
                qtm_ref[hh, n] = jnp.where(msk, qt, 0.0).astype(BF16)
                dotm_ref[hh, n] = jnp.where(msk, dt, 0.0).astype(BF16)
            return carry

        lax.fori_loop(0, nb, tstep, 0)
        dka_ref[...] = jnp.zeros_like(dka_ref)
        dva_ref[...] = jnp.zeros_like(dva_ref)
        dfr_ref[...] = jnp.zeros_like(dfr_ref)
        lane_lo = lax.broadcasted_iota(jnp.int32, (BLK, LANE), 1) < HEAD_DIM
        for seg in range(nseg):
            w = (seg + 1) * per * BLK
            causal = (lax.broadcasted_iota(jnp.int32, (BLK, w), 1)
                      - lax.broadcasted_iota(jnp.int32, (BLK, w), 0))
            causal_t = (lax.broadcasted_iota(jnp.int32, (w, BLK), 0)
                        - lax.broadcasted_iota(jnp.int32, (w, BLK), 1))

            def nstep(n, carry):
                r0 = pl.multiple_of(n * BLK, BLK)
                dqs = []
                for hh in range(2):
                    qn = qm_ref[hh, pl.ds(r0, BLK), :]
                    don = dom_ref[hh, pl.ds(r0, BLK), :]
                    sc = _dot(qn, kt_ref[:, :w]) * SCALE + (fc_ref[hh, pl.ds(r0, BLK), :] - fr_ref[hh, :, :w])
                    p = jnp.exp(jnp.where(causal <= n * BLK, sc, NEG) - l_ref[hh, pl.ds(r0, BLK), :])
                    dp = _dot(don, vt_ref[:, :w])
                    ds = p * (dp - jnp.sum(p * dp, axis=1, keepdims=True))
                    dqs.append(_dot(ds.astype(BF16), kb_ref[:w, :]))
                    dfr_ref[hh, :, :w] -= jnp.sum(ds, axis=0, keepdims=True)
                    sct = _dot(kb_ref[:w, :], qtm_ref[hh, n]) * SCALE + (frb_ref[hh, n] - fc_ref[hh, :w, :])
                    pt = jnp.exp(jnp.where(causal_t <= n * BLK, sct, NEG) - lrb_ref[hh, n])
                    dpt = _dot(vb_ref[:w, :], dotm_ref[hh, n])
                    dst = pt * (dpt - jnp.sum(pt * dpt, axis=0, keepdims=True))
                    dka_ref[:w, :] += _dot(dst.astype(BF16), qn)
                    dva_ref[:w, :] += _dot(pt.astype(BF16), don)
                dq_ref[pl.ds(r0, BLK), :] = (jnp.where(lane_lo, dqs[0], dqs[1]) * SCALE).astype(BF16)
                return carry

            lax.fori_loop(seg * per, (seg + 1) * per, nstep, 0)
        dk_ref[...] = (dka_ref[...] * SCALE).astype(BF16)
        dv_ref[...] = dva_ref[...].astype(BF16)

    def tok(blk):
        return pl.BlockSpec((None, s, LANE), lambda i, p: (i, 0, blk + p))

    col = pl.BlockSpec((None, 2, s, 1), lambda i, p: (i, p, 0, 0))
    rowspec = pl.BlockSpec((None, 2, 1, s), lambda i, p: (i, p, 0, 0))
    rowbspec = pl.BlockSpec((None, 2, nb, 1, BLK), lambda i, p: (i, p, 0, 0, 0))
    outtok = pl.BlockSpec((None, s, LANE), lambda i, p: (i, 0, p))
    shp = jax.ShapeDtypeStruct((b, s, nh * HEAD_DIM), BF16)
    return pl.pallas_call(
        body, name=name, grid=(b, npair),
        out_shape=(shp, shp, shp, jax.ShapeDtypeStruct((b, nh, 1, s), F32)),
        in_specs=[tok(qblk), tok(kblk), tok(vblk),
                  pl.BlockSpec((None, s, LANE), lambda i, p: (i, 0, doblk + p)),
                  col, rowspec, rowbspec, col, rowbspec],
        out_specs=(outtok, outtok, outtok, rowspec),
        scratch_shapes=[pltpu.VMEM((2, s, LANE), BF16), pltpu.VMEM((2, s, LANE), BF16),
                        pltpu.VMEM((s, LANE), BF16), pltpu.VMEM((s, LANE), BF16),
                        pltpu.VMEM((LANE, s), BF16), pltpu.VMEM((LANE, s), BF16),
                        pltpu.VMEM((2, nb, LANE, BLK), BF16), pltpu.VMEM((2, nb, LANE, BLK), BF16),
                        pltpu.VMEM((s, LANE), F32), pltpu.VMEM((s, LANE), F32)],
        compiler_params=_cp(("parallel", "parallel")),
    )(proj, proj, proj, dmix, fcol, frow, frowb, lse, lserowb)


def _expm1(x):
    poly = x * (1.0 + x * (1.0 / 2.0) * (1.0 + x * (1.0 / 3.0) * (1.0 + x * (1.0 / 4.0) * (1.0 + x * (1.0 / 5.0)
                                                                                          * (1.0 + x * (1.0 / 6.0))))))
    return jnp.where(x > -0.1, poly, jnp.exp(x) - 1.0)


def _softplus(z):
    return jnp.maximum(z, 0.0) + jnp.log(1.0 + jnp.exp(-jnp.abs(z)))


def _scan_up(a, u, carry, row):
    tc = a.shape[0]
    d = 1
    while d < tc:
        keep = row >= d
        a_sh = jnp.where(keep, pltpu.roll(a, d, 0), 1.0)
        u_sh = jnp.where(keep, pltpu.roll(u, d, 0), 0.0)
        u = a * u_sh + u
        a = a * a_sh
        d *= 2
    return u + a * carry


def _scan_down(bnext, g, carry, row):
    tc = g.shape[0]
    a, u = bnext, g
    d = 1
    while d < tc:
        keep = row < tc - d
        a_sh = jnp.where(keep, pltpu.roll(a, tc - d, 0), 1.0)
        u_sh = jnp.where(keep, pltpu.roll(u, tc - d, 0), 0.0)
        u = a * u_sh + u
        a = a * a_sh
        d *= 2
    return u + a * carry


def _pick_row(val, row, which):
    return jnp.sum(jnp.where(row == which, val, 0.0), axis=0, keepdims=True)


def _lru_gates(xpad_ref, t0, tc, cw_ref, cb_ref, wa, ba_ref, wx, bx_ref, sp):
    xw = xpad_ref[pl.ds(t0, tc + SUBLANE), :]
    xc = cb_ref[...]
    for j in range(CONV_WIDTH):
        sh = CONV_WIDTH - 1 - j
        xs = xw if sh == 0 else pltpu.roll(xw, sh, 0)
        xc = xc + xs[SUBLANE:, :] * cw_ref[j:j + 1, :]
    xcb = xc.astype(BF16)
    r = _sigmoid(_dot(xcb, wa) + ba_ref[...])
    i = _sigmoid(_dot(xcb, wx) + bx_ref[...])
    la = -LRU_C * r * sp
    return xc, r, i, la


def _lru_specs(s, cb):
    seq = lambda bi, ni: (bi, 0, ni)
    return dict(
        seq=pl.BlockSpec((None, s, cb), seq),
        cw=pl.BlockSpec((CONV_WIDTH, cb), lambda bi, ni: (0, ni)),
        vec=pl.BlockSpec((1, cb), lambda bi, ni: (0, ni)),
        wblk=pl.BlockSpec((None, cb, cb), lambda bi, ni: (ni, 0, 0)),
    )


def _lru_fwd(proj, cw, cb_, wa, ba, wx, bx, lam, name):
    b, s, _ = proj.shape
    nblk, cb, _ = wa.shape
    tc = min(s, SCAN_CHUNK)
    nc = s // tc

    def body(x_ref, cw_ref, cb_ref, wa_ref, ba_ref, wx_ref, bx_ref, lam_ref, hs_ref, xpad_ref):
        xpad_ref[0:SUBLANE, :] = jnp.zeros((SUBLANE, cb), F32)
        xpad_ref[SUBLANE:, :] = x_ref[...]
        wa_b = wa_ref[...].astype(BF16)
        wx_b = wx_ref[...].astype(BF16)
        sp = _softplus(-lam_ref[...])
        row = lax.broadcasted_iota(jnp.int32, (tc, cb), 0)

        def chunk(ci, carry):
            t0 = pl.multiple_of(ci * tc, tc)
            xc, r, i, la = _lru_gates(xpad_ref, t0, tc, cw_ref, cb_ref, wa_b, ba_ref, wx_b, bx_ref, sp)
            a = jnp.exp(la)
            u = jnp.sqrt(-_expm1(2.0 * la)) * (i * xc)
            h = _scan_up(a, u, carry, row)
            hs_ref[pl.ds(t0, tc), :] = h
            return _pick_row(h, row, tc - 1)

        lax.fori_loop(0, nc, chunk, jnp.zeros((1, cb), F32))

    sp_ = _lru_specs(s, cb)
    return pl.pallas_call(
        body, name=name, grid=(b, nblk),
        out_shape=jax.ShapeDtypeStruct((b, s, nblk * cb), F32),
        in_specs=[sp_["seq"], sp_["cw"], sp_["vec"], sp_["wblk"], sp_["vec"], sp_["wblk"], sp_["vec"], sp_["vec"]],
        out_specs=sp_["seq"],
        scratch_shapes=[pltpu.VMEM((s + SUBLANE, cb), F32)],
        compiler_params=_cp(("parallel", "parallel")),
    )(proj, cw, cb_, wa, ba, wx, bx, lam)


def _lru_bwd(proj, hs, dhs, cw, cb_, wa, ba, wx, bx, lam, name):
    b, s, _ = proj.shape
    nblk, cb, _ = wa.shape
    tc = min(s, SCAN_CHUNK)
    nc = s // tc

    def body(x_ref, hs_ref, dhs_ref, cw_ref, cb_ref, wa_ref, ba_ref, wx_ref, bx_ref, lam_ref,
             dx_ref, dcw_ref, dcb_ref, dwa_ref, dba_ref, dwx_ref, dbx_ref, dlam_ref,
             xpad_ref, hpad_ref, dcpad_ref, xc_ref, r_ref, i_ref, a_ref):
        @pl.when(pl.program_id(1) == 0)
        def _():
            for ref in (dcw_ref, dcb_ref, dwa_ref, dba_ref, dwx_ref, dbx_ref, dlam_ref):
                ref[...] = jnp.zeros_like(ref)

        zeros8 = jnp.zeros((SUBLANE, cb), F32)
        xpad_ref[0:SUBLANE, :] = zeros8
        xpad_ref[SUBLANE:, :] = x_ref[...]
        hpad_ref[0:SUBLANE, :] = zeros8
        hpad_ref[SUBLANE:, :] = hs_ref[...]
        dcpad_ref[s:s + SUBLANE, :] = zeros8
        wa_b = wa_ref[...].astype(BF16)
        wx_b = wx_ref[...].astype(BF16)
        lam_v = lam_ref[...]
        sp = _softplus(-lam_v)
        dsp_dlam = -_sigmoid(-lam_v)
        row = lax.broadcasted_iota(jnp.int32, (tc, cb), 0)

        def recompute(ci, carry):
            t0 = pl.multiple_of(ci * tc, tc)
            xc, r, i, la = _lru_gates(xpad_ref, t0, tc, cw_ref, cb_ref, wa_b, ba_ref, wx_b, bx_ref, sp)
            xc_ref[pl.ds(t0, tc), :] = xc
            r_ref[pl.ds(t0, tc), :] = r
            i_ref[pl.ds(t0, tc), :] = i
            a_ref[pl.ds(t0, tc), :] = jnp.exp(la)
            return carry

        lax.fori_loop(0, nc, recompute, 0)

        def adjoint(k, carry):
            g_next, a_first_next = carry
            t0 = pl.multiple_of((nc - 1 - k) * tc, tc)
            a = a_ref[pl.ds(t0, tc), :]
            a_next = jnp.where(row == tc - 1, a_first_next, pltpu.roll(a, tc - 1, 0))
            gg = _scan_down(a_next, dhs_ref[pl.ds(t0, tc), :], g_next, row)
            h_prev = pltpu.roll(hpad_ref[pl.ds(t0, tc + SUBLANE), :], 1, 0)[SUBLANE:, :]
            xc = xc_ref[pl.ds(t0, tc), :]
            r = r_ref[pl.ds(t0, tc), :]
            i = i_ref[pl.ds(t0, tc), :]
            mult = jnp.sqrt(-_expm1(-2.0 * LRU_C * r * sp))
            d_mult = gg * i * xc
            d_i = gg * mult * xc
            d_xc = gg * mult * i
            d_la = gg * h_prev * a - d_mult * (a * a) / mult
            d_zr = (d_la * (-LRU_C * sp)) * r * (1.0 - r)
            d_zi = d_i * i * (1.0 - i)
            dlam_ref[...] += jnp.sum(d_la * (-LRU_C * r), axis=0, keepdims=True) * dsp_dlam
            dzr_b = d_zr.astype(BF16)
            dzi_b = d_zi.astype(BF16)
            xcb = xc.astype(BF16)
            d_xc = d_xc + _dot_nt(dzr_b, wa_b) + _dot_nt(dzi_b, wx_b)
            dwa_ref[...] += _dot_tn(xcb, dzr_b)
            dwx_ref[...] += _dot_tn(xcb, dzi_b)
            dba_ref[...] += jnp.sum(d_zr, axis=0, keepdims=True)
            dbx_ref[...] += jnp.sum(d_zi, axis=0, keepdims=True)
            dcb_ref[...] += jnp.sum(d_xc, axis=0, keepdims=True)
            dcpad_ref[pl.ds(t0, tc), :] = d_xc
            return _pick_row(gg, row, 0), _pick_row(a, row, 0)

        zero = jnp.zeros((1, cb), F32)
        lax.fori_loop(0, nc, adjoint, (zero, zero))

        def conv_back(ci, carry):
            t0 = pl.multiple_of(ci * tc, tc)
            dw = dcpad_ref[pl.ds(t0, tc + SUBLANE), :]
            xw = xpad_ref[pl.ds(t0, tc + SUBLANE), :]
            d_xc = dw[:tc, :]
            dxr = jnp.zeros((tc, cb), F32)
            for j in range(CONV_WIDTH):
                sh = CONV_WIDTH - 1 - j
                dsh = dw if sh == 0 else pltpu.roll(dw, tc + SUBLANE - sh, 0)
                dxr = dxr + dsh[:tc, :] * cw_ref[j:j + 1, :]
                xs = xw if sh == 0 else pltpu.roll(xw, sh, 0)
                dcw_ref[j:j + 1, :] += jnp.sum(d_xc * xs[SUBLANE:, :], axis=0, keepdims=True)
            dx_ref[pl.ds(t0, tc), :] = dxr.astype(BF16)
            return carry

        lax.fori_loop(0, nc, conv_back, 0)

    seq = lambda ni, bi: (bi, 0, ni)
    seqspec = pl.BlockSpec((None, s, cb), seq)
    cwspec = pl.BlockSpec((CONV_WIDTH, cb), lambda ni, bi: (0, ni))
    vec = pl.BlockSpec((1, cb), lambda ni, bi: (0, ni))
    wblk = pl.BlockSpec((None, cb, cb), lambda ni, bi: (ni, 0, 0))
    w = nblk * cb
    return pl.pallas_call(
        body, name=name, grid=(nblk, b),
        out_shape=(jax.ShapeDtypeStruct((b, s, w), BF16), jax.ShapeDtypeStruct((CONV_WIDTH, w), F32),
                   jax.ShapeDtypeStruct((1, w), F32), jax.ShapeDtypeStruct((nblk, cb, cb), F32),
                   jax.ShapeDtypeStruct((1, w), F32), jax.ShapeDtypeStruct((nblk, cb, cb), F32),
                   jax.ShapeDtypeStruct((1, w), F32), jax.ShapeDtypeStruct((1, w), F32)),
        in_specs=[seqspec, seqspec, seqspec, cwspec, vec, wblk, vec, wblk, vec, vec],
        out_specs=(seqspec, cwspec, vec, wblk, vec, wblk, vec, vec),
        scratch_shapes=[pltpu.VMEM((s + SUBLANE, cb), F32)] * 3 + [pltpu.VMEM((s, cb), F32)] * 4,
        compiler_params=_cp(("parallel", "arbitrary")),
    )(proj, hs, dhs, cw, cb_, wa, ba, wx, bx, lam)


def _adamw(w, g, m, v, name):
    shape = w.shape
    total = int(np.prod(shape))
    cols = 1024
    rows = -(-total // cols)
    rows = -(-rows // SUBLANE) * SUBLANE
    tr = _row_tile(rows, 512)
    pad = rows * cols - total

    def flat(a):
        a = a.reshape(-1)
        if pad:
            a = jnp.pad(a, (0, pad))
        return a.reshape(rows, cols)

    c1 = 1.0 - ADAM_B1 ** ADAM_STEP
    c2 = 1.0 - ADAM_B2 ** ADAM_STEP

    def body(w_ref, g_ref, m_ref, v_ref, d_ref, nm_ref, nv_ref):
        gv = g_ref[...]
        nm = ADAM_B1 * m_ref[...] + (1.0 - ADAM_B1) * gv
        nv = ADAM_B2 * v_ref[...] + (1.0 - ADAM_B2) * (gv * gv)
        nm_ref[...] = nm
        nv_ref[...] = nv
        d_ref[...] = -ADAM_LR * ((nm / c1) / (jnp.sqrt(nv / c2) + ADAM_EPS) + ADAM_WD * w_ref[...])

    spec = pl.BlockSpec((tr, cols), lambda i: (i, 0))
    shp = jax.ShapeDtypeStruct((rows, cols), F32)
    outs = pl.pallas_call(
        body, name=name, grid=(rows // tr,), out_shape=(shp, shp, shp),
        in_specs=[spec] * 4, out_specs=(spec,) * 3,
        compiler_params=_cp(("parallel",)),
    )(flat(w), flat(g), flat(m), flat(v))
    return tuple(o.reshape(-1)[:total].reshape(shape) for o in outs)


def _to_heads(t, nh):
    b, s, _ = t.shape
    return t.reshape(b, s, nh, HEAD_DIM).transpose(0, 2, 1, 3)


def _to_heads_t(t, nh):
    b, s, _ = t.shape
    return t.reshape(b, s, nh, HEAD_DIM).transpose(0, 2, 3, 1)


def _to_blocks_t(t, nh):
    b, s, _ = t.shape
    return t.reshape(b, s // BLK, BLK, nh, HEAD_DIM).transpose(0, 3, 1, 4, 2)


def _from_heads(t):
    b, nh, s, hd = t.shape
    return t.transpose(0, 2, 1, 3).reshape(b, s, nh * hd)


def _pad_rows(a, mult):
    r = a.shape[0]
    p = (-r) % mult
    return jnp.pad(a, ((0, p), (0, 0))) if p else a


def kernel(x, c, rel_bias, norm_g, ada_w, ada_b, attn_w_in, attn_sinks, attn_b_f, attn_w_out, lru_w_in, lru_conv_w, lru_conv_b, lru_w_a, lru_b_a, lru_w_x, lru_b_x, lru_lambda, lru_w_out, final_g, loss_target, m_rel_bias, m_norm_g, m_ada_w, m_ada_b, m_attn_w_in, m_attn_sinks, m_attn_b_f, m_attn_w_out, m_lru_w_in, m_lru_conv_w, m_lru_conv_b, m_lru_w_a, m_lru_b_a, m_lru_w_x, m_lru_b_x, m_lru_lambda, m_lru_w_out, m_final_g, v_rel_bias, v_norm_g, v_ada_w, v_ada_b, v_attn_w_in, v_attn_sinks, v_attn_b_f, v_attn_w_out, v_lru_w_in, v_lru_conv_w, v_lru_conv_b, v_lru_w_a, v_lru_b_a, v_lru_w_x, v_lru_b_x, v_lru_lambda, v_lru_w_out, v_final_g):
    bl, s, d = x.shape
    ix, iy, ic = lax.axis_index("x"), lax.axis_index("y"), lax.axis_index("c")
    chip = 2 * ix + iy
    me = 2 * chip + ic
    nb = s // BLK
    aw = A_Q_HEADS * HEAD_DIM
    akv = A_KV_HEADS * HEAD_DIM
    bw = B_HEADS * HEAD_DIM
    mixw = aw + bw
    qkv_w = aw + 2 * akv + 3 * bw
    n_in = attn_w_in.shape[2] * N_CHIP
    lw = lru_lambda.shape[1] * N_CHIP
    n0 = mixw + qkv_w + LANE

    rows_pad = -(-bl // SUBLANE) * SUBLANE
    c_all = _all_gather8(_pad_rows(c, SUBLANE), "gather_c", pltpu.VMEM)
    c_all = c_all.reshape(N_DEV, rows_pad, d)[:, :bl].reshape(N_DEV * bl, d)
    ncol = ada_w.shape[2]
    ada_w_l = lax.dynamic_index_in_dim(ada_w, ic, 0, keepdims=False)
    ada_b_l = lax.dynamic_slice(ada_b, (ic, chip * ncol), (1, ncol))
    mod_part = _ada_fwd(c_all, ada_w_l, ada_b_l, "ada_fwd")
    mod_all = _all_gather8(_pad_rows(mod_part, SUBLANE), "gather_mod", pltpu.VMEM)
    mrows = -(-(N_DEV * bl) // SUBLANE) * SUBLANE
    mod_all = mod_all.reshape(N_CHIP, 2, mrows, ncol)[:, :, :N_DEV * bl]
    mod_all = mod_all.transpose(1, 2, 0, 3).reshape(2, N_DEV * bl, N_CHIP * ncol)
    mod = lax.dynamic_slice_in_dim(mod_all, me * bl, bl, axis=1)
    shift = [mod[l, :, 0:d].reshape(bl, 1, d) for l in range(2)]
    scale = [mod[l, :, d:2 * d].reshape(bl, 1, d) for l in range(2)]
    gmod = [mod[l, :, 2 * d:3 * d].reshape(bl, 1, d) for l in range(2)]

    big = [attn_w_in[0], attn_w_out[0], lru_w_in[0], lru_w_out[0]]
    big_sizes = [int(np.prod(a.shape)) for a in big]
    big_total = sum(big_sizes)
    half_rows = -(-(-(-big_total // 2) // 1024) // 16) * 16
    half_len = half_rows * 1024

    def pack_halves(parts, dtype):
        flat = jnp.concatenate([p.reshape(-1).astype(dtype) for p in parts])
        flat = jnp.pad(flat, (0, 2 * half_len - big_total))
        return flat.reshape(2, half_rows, 1024)

    def unpack_chip(flat):
        outs, off = [], 0
        for a, n in zip(big, big_sizes):
            outs.append(flat[off:off + n].reshape(a.shape))
            off += n
        return outs

    my_half = lax.dynamic_index_in_dim(pack_halves(big, BF16), ic, 0, keepdims=False)
    gathered = _all_gather8(my_half, "gather_weights", pltpu.HBM).reshape(N_CHIP, 2 * half_len)
    per_chip = [unpack_chip(gathered[k]) for k in range(N_CHIP)]
    w_in0 = jnp.concatenate([per_chip[k][0] for k in range(N_CHIP)], axis=1)
    w_out0 = jnp.concatenate([per_chip[k][1] for k in range(N_CHIP)], axis=0)
    w_in1 = jnp.concatenate([per_chip[k][2] for k in range(N_CHIP)], axis=1)
    w_out1 = jnp.concatenate([per_chip[k][3] for k in range(N_CHIP)], axis=0)
    w_cat0 = jnp.concatenate([w_in0[:, qkv_w + B_HEADS:], w_in0[:, :qkv_w + B_HEADS],
                              jnp.zeros((d, n0 - n_in), BF16)], axis=1)

    proj0, h0 = _norm_proj(x, norm_g[0:1], scale[0], shift[0], w_cat0, "norm_proj0")
    o_a = mixw
    aq = _to_heads(proj0[:, :, o_a:o_a + aw].astype(BF16), A_Q_HEADS)
    ak = _to_heads(proj0[:, :, o_a + aw:o_a + aw + akv].astype(BF16), A_KV_HEADS)
    av = _to_heads(proj0[:, :, o_a + aw + akv:o_a + aw + 2 * akv].astype(BF16), A_KV_HEADS)
    o_b = o_a + aw + 2 * akv
    fox_blks = (o_b // LANE, (o_b + bw) // LANE, (o_b + 2 * bw) // LANE)
    zf = proj0[:, :, o_b + 3 * bw:]
    bucket_np, valid_np = _rel_buckets()
    bucket = jnp.asarray(bucket_np)
    bias = _swa_bias(rel_bias.T, bucket, jnp.asarray(valid_np), "swa_bias")
    sinks = attn_sinks[0]
    a_out, a_lse = _swa_fwd(aq, ak, av, bias, sinks, "swa_fwd")
    bf_pad = jnp.pad(attn_b_f, ((0, 0), (0, LANE - B_HEADS)))
    fsum = _fox_decay(zf, bf_pad, "fox_decay")
    fh = fsum[:, :, :B_HEADS].transpose(0, 2, 1)
    fcol = fh.reshape(bl, B_HEADS, s, 1)
    frow = fh.reshape(bl, B_HEADS, 1, s)
    frowb = fh.reshape(bl, B_HEADS, nb, 1, BLK)
    b_out, b_lse = _fox_fwd(proj0, *fox_blks, fcol, frow, "fox_fwd")
    mix0 = [_from_heads(a_out), b_out]
    x1, o0 = _gate_outproj(mix0, proj0, 0, w_out0, x, gmod[0], "gate_outproj0")

    proj1, h1 = _norm_proj(x1, norm_g[1:2], scale[1], shift[1], w_in1, "norm_proj1")
    vec_rows = jnp.concatenate([lru_conv_w[0], lru_conv_b, lru_b_a, lru_b_x, lru_lambda], axis=0)
    vec_all = _all_gather8(vec_rows, "gather_lru_vectors", pltpu.VMEM)
    vec_all = vec_all.reshape(N_CHIP, 2, SUBLANE, lw // N_CHIP)[:, 0]
    vec_all = vec_all.transpose(1, 0, 2).reshape(SUBLANE, lw)
    cw_f, cb_f, ba_f, bx_f, lam_f = vec_all[0:4], vec_all[4:5], vec_all[5:6], vec_all[6:7], vec_all[7:8]
    hs = _lru_fwd(proj1, cw_f, cb_f, lru_w_a[0], ba_f, lru_w_x[0], bx_f, lam_f, "lru_fwd")
    x2, o1 = _gate_outproj([hs], proj1, 1, w_out1, x1, gmod[1], "gate_outproj1")

    loss_vec, dx2, g_final = _final_loss(x2, final_g.reshape(1, d), loss_target, "final_loss")
    loss = lax.psum(loss_vec[0, 0], ("x", "y", "c"))

    dhs, dgate1, do1, y1, dgm1 = _bwd_out(dx2, gmod[1], o1, [hs], proj1, 1, w_out1.T, "bwd_out1")
    g_w_out1 = _matmul_tn(y1.reshape(bl * s, lw), do1.reshape(bl * s, d), "grad_w_out1")
    (dxr, g_cw, g_cb, g_wa, g_ba, g_wx, g_bx, g_lam) = _lru_bwd(
        proj1, hs, dhs, cw_f, cb_f, lru_w_a[0], ba_f, lru_w_x[0], bx_f, lam_f, "lru_bwd")
    dproj1 = jnp.concatenate([dxr, dgate1], axis=-1)
    g_w_in1 = _matmul_tn(h1.reshape(bl * s, d), dproj1.reshape(bl * s, 2 * lw), "grad_w_in1")
    dx1, dsh1, dsc1, g_ng1 = _bwd_in(dproj1, w_in1.T, x1, norm_g[1:2], scale[1], dx2, "bwd_in1")

    dmix0, dgate0, do0, y0, dgm0 = _bwd_out(dx1, gmod[0], o0, mix0, proj0, 0, w_out0.T, "bwd_out0")
    g_w_out0 = _matmul_tn(y0.reshape(bl * s, mixw), do0.reshape(bl * s, d), "grad_w_out0")
    da_out = _to_heads(dmix0[:, :, :aw], A_Q_HEADS)
    daq, dak, dav, dbias, dsink = _swa_bwd(aq, ak, av, bias, sinks, da_out, a_lse, "swa_bwd")
    dbq, dbk, dbv, dfrow = _fox_bwd(proj0, *fox_blks, dmix0, aw // LANE, fcol, frow, frowb, b_lse,
                                    b_lse.reshape(bl, B_HEADS, nb, 1, BLK), "fox_bwd")
    df = dfrow.reshape(bl, B_HEADS, s).transpose(0, 2, 1)
    df = jnp.pad(df, ((0, 0), (0, 0), (0, LANE - B_HEADS)))
    dzf, g_bf = _fox_dgate(df, zf, bf_pad, B_HEADS, "fox_dgate")
    dproj0 = jnp.concatenate(
        [dgate0] + [_from_heads(t).astype(BF16) for t in (daq, dak, dav)] + [dbq, dbk, dbv, dzf.astype(BF16)],
        axis=-1)
    g_w_cat0 = _matmul_tn(h0.reshape(bl * s, d), dproj0.reshape(bl * s, n0), "grad_w_in0")
    g_w_in0 = jnp.concatenate([g_w_cat0[:, mixw:mixw + qkv_w + B_HEADS], g_w_cat0[:, :mixw]], axis=1)
    dx0, dsh0, dsc0, g_ng0 = _bwd_in(dproj0, w_cat0.T, x, norm_g[0:1], scale[0], dx1, "bwd_in0")
    g_relb, g_sink = _swa_small_grads(dbias, dsink, bucket, "swa_small_grads")

    dmod = jnp.concatenate([jnp.concatenate([dsh0, dsc0, dgm0], axis=-1),
                            jnp.concatenate([dsh1, dsc1, dgm1], axis=-1)], axis=1)
    dmod_all = _all_gather8(_pad_rows(dmod.reshape(bl, 6 * d), SUBLANE), "gather_dmod", pltpu.VMEM)
    dmod_all = dmod_all.reshape(N_DEV, rows_pad, 6 * d)[:, :bl].reshape(N_DEV * bl, 6 * d)
    dmod_chip = lax.dynamic_slice_in_dim(dmod_all.reshape(N_DEV * bl, 2, 3 * d), chip * ncol, ncol, axis=2)
    g_ada_w, g_ada_b = _ada_bwd(c_all, dmod_chip.transpose(1, 0, 2), dmod_all, "ada_bwd")
    g_ada_b = g_ada_b.reshape(2, 3 * d)

    big_grads = [g_w_in0, g_w_out0, g_w_in1, g_w_out1]

    def chip_shard(gfull, like, k):
        axis = 1 if like.shape[0] == gfull.shape[0] else 0
        width = like.shape[axis]
        return lax.slice_in_dim(gfull, k * width, (k + 1) * width, axis=axis)

    small_parts = [g_relb[:, :, 0].T, jnp.concatenate([g_ng0, g_ng1], axis=0), g_sink[:, 0, 0], g_bf[0, :B_HEADS],
                   g_wa, g_wx, g_final, g_cw, g_cb, g_ba, g_bx, g_lam]
    small_sizes = [int(np.prod(p.shape)) for p in small_parts]
    small_total = sum(small_sizes)
    piece_rows = -(-(-(-small_total // N_DEV) // 1024) // SUBLANE) * SUBLANE
    small_flat = jnp.concatenate([p.reshape(-1) for p in small_parts])
    small_flat = jnp.pad(small_flat, (0, N_DEV * piece_rows * 1024 - small_total))
    small_pieces = small_flat.reshape(N_DEV, piece_rows, 1024)
    big_pieces = jnp.stack([pack_halves([chip_shard(g, a, k) for g, a in zip(big_grads, big)], F32)
                            for k in range(N_CHIP)]).reshape(N_DEV, half_rows, 1024)
    pieces = jnp.concatenate([big_pieces, small_pieces], axis=1)
    slots = _all_to_all8(pieces, "exchange_grads")
    reduced = _sum_slots(slots, "sum_grads")
    both = _sibling_swap(reduced[:half_rows], "swap_halves")
    g_big = unpack_chip(both.reshape(-1))
    small_all = _all_gather8(reduced[half_rows:], "gather_small_grads", pltpu.VMEM).reshape(-1)
    g_small, off = [], 0
    for p, n in zip(small_parts, small_sizes):
        g_small.append(small_all[off:off + n].reshape(p.shape))
        off += n
    (g_rel_bias, g_norm_g, g_sinks, g_b_f, g_w_a, g_w_x, g_fin, g_cw_r, g_cb_r, g_ba_r, g_bx_r, g_lam_r) = g_small
    cw4 = lw // N_CHIP

    def my_cols(a):
        return lax.dynamic_slice_in_dim(a, chip * cw4, cw4, axis=1)

    grads = {
        "rel_bias": g_rel_bias, "norm_g": g_norm_g, "ada_w": g_ada_w, "ada_b": g_ada_b,
        "attn_w_in": g_big[0][None], "attn_sinks": g_sinks[None], "attn_b_f": g_b_f[None],
        "attn_w_out": g_big[1][None], "lru_w_in": g_big[2][None], "lru_conv_w": my_cols(g_cw_r)[None],
        "lru_conv_b": my_cols(g_cb_r), "lru_w_a": g_w_a[None], "lru_b_a": my_cols(g_ba_r),
        "lru_w_x": g_w_x[None], "lru_b_x": my_cols(g_bx_r), "lru_lambda": my_cols(g_lam_r),
        "lru_w_out": g_big[3][None], "final_g": g_fin.reshape(d),
    }
    weights = dict(rel_bias=rel_bias, norm_g=norm_g, ada_w=ada_w, ada_b=ada_b, attn_w_in=attn_w_in,
                   attn_sinks=attn_sinks, attn_b_f=attn_b_f, attn_w_out=attn_w_out, lru_w_in=lru_w_in,
                   lru_conv_w=lru_conv_w, lru_conv_b=lru_conv_b, lru_w_a=lru_w_a, lru_b_a=lru_b_a,
                   lru_w_x=lru_w_x, lru_b_x=lru_b_x, lru_lambda=lru_lambda, lru_w_out=lru_w_out, final_g=final_g)
    moms = dict(rel_bias=(m_rel_bias, v_rel_bias), norm_g=(m_norm_g, v_norm_g), ada_w=(m_ada_w, v_ada_w),
                ada_b=(m_ada_b, v_ada_b), attn_w_in=(m_attn_w_in, v_attn_w_in),
                attn_sinks=(m_attn_sinks, v_attn_sinks), attn_b_f=(m_attn_b_f, v_attn_b_f),
                attn_w_out=(m_attn_w_out, v_attn_w_out), lru_w_in=(m_lru_w_in, v_lru_w_in),
                lru_conv_w=(m_lru_conv_w, v_lru_conv_w), lru_conv_b=(m_lru_conv_b, v_lru_conv_b),
                lru_w_a=(m_lru_w_a, v_lru_w_a), lru_b_a=(m_lru_b_a, v_lru_b_a), lru_w_x=(m_lru_w_x, v_lru_w_x),
                lru_b_x=(m_lru_b_x, v_lru_b_x), lru_lambda=(m_lru_lambda, v_lru_lambda),
                lru_w_out=(m_lru_w_out, v_lru_w_out), final_g=(m_final_g, v_final_g))
    names = list(weights)
    big_names = [n for n in names if weights[n].size >= 65536]
    small_names = [n for n in names if weights[n].size < 65536]
    delta, new_m, new_v = {}, {}, {}
    for n in big_names:
        delta[n], new_m[n], new_v[n] = _adamw(weights[n], grads[n].reshape(weights[n].shape),
                                              moms[n][0], moms[n][1], "adamw_" + n)
    cat = lambda arrs: jnp.concatenate([a.reshape(-1) for a in arrs])
    sd, sm, sv = _adamw(cat([weights[n] for n in small_names]), cat([grads[n] for n in small_names]),
                        cat([moms[n][0] for n in small_names]), cat([moms[n][1] for n in small_names]),
                        "adamw_small")
    off = 0
    for n in small_names:
        sz = weights[n].size
        shp = weights[n].shape
        delta[n], new_m[n], new_v[n] = (sd[off:off + sz].reshape(shp), sm[off:off + sz].reshape(shp),
                                        sv[off:off + sz].reshape(shp))
        off += sz
    out_grads = [grads[n].reshape(weights[n].shape) for n in names]
    return (loss, dx0, *out_grads, *[delta[n] for n in names], *[new_m[n] for n in names],
            *[new_v[n] for n in names])
```

```python
import functools
import math

import numpy as np
import jax
import jax.numpy as jnp
from jax import lax
from jax.experimental import pallas as pl
from jax.experimental.pallas import tpu as pltpu

F32 = jnp.float32
BF16 = jnp.bfloat16
MESH = pl.DeviceIdType.MESH

N_DEV = 8
N_CHIP = 4
HEAD_DIM = 64
BLK = 128
A_Q_HEADS = 8
A_KV_HEADS = 2
A_GROUP = A_Q_HEADS // A_KV_HEADS
B_HEADS = 8
REL_BUCKETS = 32
REL_MAX_EXACT = 16
REL_MAX_DIST = 128
LRU_BLOCKS = 8
LRU_C = 8.0
CONV_WIDTH = 4
EPS = 1e-6
NEG = -1e30
SCALE = HEAD_DIM ** -0.5
LANE = 128
SUBLANE = 8
VMEM_LIMIT = 56 * 1024 * 1024
SCAN_CHUNK = 256
ADAM_LR = 0.001
ADAM_B1 = 0.9
ADAM_B2 = 0.999
ADAM_EPS = 1e-08
ADAM_WD = 0.01
ADAM_STEP = 10
HI = lax.Precision.HIGHEST


def _cp(sem=None):
    return pltpu.CompilerParams(dimension_semantics=sem, vmem_limit_bytes=VMEM_LIMIT)


def _dot(a, b):
    return jnp.dot(a, b, preferred_element_type=F32)


def _dot_nt(a, b):
    return lax.dot_general(a, b, (((1,), (1,)), ((), ())), preferred_element_type=F32)


def _dot_tn(a, b):
    return lax.dot_general(a, b, (((0,), (0,)), ((), ())), preferred_element_type=F32)


def _sigmoid(z):
    return 1.0 / (1.0 + jnp.exp(-z))


def _row_tile(rows, cap):
    if rows <= cap:
        return rows
    best = SUBLANE
    t = SUBLANE
    while t <= cap:
        if rows % t == 0:
            best = t
        t += SUBLANE
    return best


def _all_gather8(x_shard, name, space):
    m_per, n = x_shard.shape

    def body(x_ref, out_ref, send_sems, recv_sems, local_sem):
        x, y, c = lax.axis_index("x"), lax.axis_index("y"), lax.axis_index("c")
        me, sibling = (x, y, c), (x, y, 1 - c)
        chips = [(1 - x, y), (x, 1 - y), (1 - x, 1 - y)]

        def rows(px, py, pc):
            return out_ref.at[pl.ds((4 * px + 2 * py + pc) * m_per, m_per), :]

        def copy(k, block, to, src=None):
            return pltpu.make_async_remote_copy(
                src_ref=rows(*block) if src is None else src, dst_ref=rows(*block),
                send_sem=send_sems.at[k], recv_sem=recv_sems.at[k], device_id=to, device_id_type=MESH)

        mine = pltpu.make_async_copy(x_ref, rows(*me), local_sem)
        mine.start()
        first = [copy(0, me, sibling, src=x_ref)]
        first += [copy(1 + j, me, (*chip, c), src=x_ref) for j, chip in enumerate(chips)]
        for cp in first:
            cp.start()
        passed = [copy(4 + j, (*chip, c), sibling) for j, chip in enumerate(chips)]
        for j, chip in enumerate(chips):
            copy(1 + j, (*chip, c), me).wait_recv()
            passed[j].start()
        copy(0, sibling, me).wait_recv()
        for j, chip in enumerate(chips):
            copy(4 + j, (*chip, 1 - c), me).wait_recv()
        for cp in first + passed:
            cp.wait_send()
        mine.wait()

    return pl.pallas_call(
        body, name=name,
        out_shape=jax.ShapeDtypeStruct((N_DEV * m_per, n), x_shard.dtype),
        in_specs=[pl.BlockSpec(memory_space=space)],
        out_specs=pl.BlockSpec(memory_space=space),
        scratch_shapes=[pltpu.SemaphoreType.DMA((7,)), pltpu.SemaphoreType.DMA((7,)), pltpu.SemaphoreType.DMA],
        compiler_params=pltpu.CompilerParams(vmem_limit_bytes=VMEM_LIMIT),
    )(x_shard)


def _sibling_push(blocks, pick_other, name):
    nblk = blocks.shape[0]
    m, n = blocks.shape[-2:]

    def body(x_ref, out_ref, send_sems, recv_sems):
        x, y, c = lax.axis_index("x"), lax.axis_index("y"), lax.axis_index("c")
        copies = []
        for k in range(nblk):
            src = x_ref.at[k, 1 - c] if pick_other else x_ref.at[k]
            copies.append(pltpu.make_async_remote_copy(
                src_ref=src, dst_ref=out_ref.at[k], send_sem=send_sems.at[k], recv_sem=recv_sems.at[k],
                device_id=(x, y, 1 - c), device_id_type=MESH))
        for cp in copies:
            cp.start()
        for cp in copies:
            cp.wait_recv()
        for cp in copies:
            cp.wait_send()

    hbm = pl.BlockSpec(memory_space=pltpu.HBM)
    return pl.pallas_call(
        body, name=name,
        out_shape=jax.ShapeDtypeStruct((nblk, m, n), blocks.dtype),
        in_specs=[hbm], out_specs=hbm,
        scratch_shapes=[pltpu.SemaphoreType.DMA((nblk,)), pltpu.SemaphoreType.DMA((nblk,))],
    )(blocks)


def _chip_all_to_all(parts, name):
    _, m, n = parts.shape

    def body(x_ref, out_ref, send_sems, recv_sems, local_sem):
        x, y, c = lax.axis_index("x"), lax.axis_index("y"), lax.axis_index("c")
        me = 2 * x + y
        mine = pltpu.make_async_copy(x_ref.at[me], out_ref.at[me], local_sem)
        mine.start()
        copies = []
        for k in range(1, N_CHIP):
            px, py = x ^ ((k >> 1) & 1), y ^ (k & 1)
            copies.append(pltpu.make_async_remote_copy(
                src_ref=x_ref.at[2 * px + py], dst_ref=out_ref.at[me],
                send_sem=send_sems.at[k - 1], recv_sem=recv_sems.at[k - 1],
                device_id=(px, py, c), device_id_type=MESH))
        for cp in copies:
            cp.start()
        for cp in copies:
            cp.wait_recv()
        for cp in copies:
            cp.wait_send()
        mine.wait()

    hbm = pl.BlockSpec(memory_space=pltpu.HBM)
    return pl.pallas_call(
        body, name=name,
        out_shape=jax.ShapeDtypeStruct(parts.shape, parts.dtype),
        in_specs=[hbm], out_specs=hbm,
        scratch_shapes=[pltpu.SemaphoreType.DMA((N_CHIP - 1,)), pltpu.SemaphoreType.DMA((N_CHIP - 1,)),
                        pltpu.SemaphoreType.DMA],
    )(parts)


def _pair_sum(core, pieces, theirs, name):
    nblk, _, m, n = pieces.shape
    tr = _row_tile(m, 536)

    def body(c_ref, p_ref, t_ref, o_ref):
        o_ref[...] = (p_ref[...] + t_ref[...]).astype(BF16)

    return pl.pallas_call(
        body, name=name,
        grid_spec=pltpu.PrefetchScalarGridSpec(
            num_scalar_prefetch=1, grid=(nblk, m // tr),
            in_specs=[pl.BlockSpec((None, None, tr, n), lambda k, i, c_ref: (k, c_ref[0], i, 0)),
                      pl.BlockSpec((None, tr, n), lambda k, i, c_ref: (k, i, 0))],
            out_specs=pl.BlockSpec((None, tr, n), lambda k, i, c_ref: (k, i, 0))),
        out_shape=jax.ShapeDtypeStruct((nblk, m, n), BF16),
        compiler_params=_cp(("parallel", "parallel")),
    )(core, pieces, theirs)


def _sum_slots(slots, name):
    k, m, n = slots.shape
    tr = _row_tile(m, 536)

    def body(s_ref, o_ref):
        acc = s_ref[0].astype(F32)
        for j in range(1, k):
            acc = acc + s_ref[j].astype(F32)
        o_ref[...] = acc

    return pl.pallas_call(
        body, name=name, grid=(m // tr,),
        out_shape=jax.ShapeDtypeStruct((m, n), F32),
        in_specs=[pl.BlockSpec((k, tr, n), lambda i: (0, i, 0))],
        out_specs=pl.BlockSpec((tr, n), lambda i: (i, 0)),
        compiler_params=_cp(("parallel",)),
    )(slots)


def _ada_fwd(c_all, w, b, name):
    r, _ = c_all.shape
    n = w.shape[1]

    def body(c_ref, w_ref, b_ref, o_ref):
        cv = c_ref[...]
        act = cv * _sigmoid(cv)
        o_ref[...] = jnp.dot(act, w_ref[...], precision=HI, preferred_element_type=F32) + b_ref[...]

    return pl.pallas_call(body, name=name, out_shape=jax.ShapeDtypeStruct((r, n), F32),
                          compiler_params=_cp())(c_all, w, b)


def _ada_bwd(c_all, dmod_chip, dmod_all, name):
    r, d = c_all.shape
    nl, _, n = dmod_chip.shape

    def body(c_ref, dm_ref, da_ref, gw_ref, gb_ref):
        cv = c_ref[...]
        act = cv * _sigmoid(cv)
        for l in range(nl):
            gw_ref[l] = lax.dot_general(act, dm_ref[l], (((0,), (0,)), ((), ())), precision=HI,
                                        preferred_element_type=F32)
        gb_ref[...] = jnp.sum(da_ref[...], axis=0, keepdims=True)

    return pl.pallas_call(
        body, name=name,
        out_shape=(jax.ShapeDtypeStruct((nl, d, n), F32), jax.ShapeDtypeStruct((1, dmod_all.shape[1]), F32)),
        compiler_params=_cp())(c_all, dmod_chip, dmod_all)


def _norm_proj(x, g, scale, shift, w, name):
    b, s, d = x.shape
    n = w.shape[1]
    tm = min(s, 256)

    def body(x_ref, g_ref, sc_ref, sh_ref, w_ref, proj_ref, h_ref):
        xv = x_ref[...]
        rstd = lax.rsqrt(jnp.mean(xv * xv, axis=-1, keepdims=True) + EPS)
        h = (xv * rstd) * g_ref[...] * (1.0 + sc_ref[...]) + sh_ref[...]
        hb = h.astype(BF16)
        h_ref[...] = hb
        proj_ref[...] = _dot(hb, w_ref[...])

    return pl.pallas_call(
        body, name=name, grid=(b, s // tm),
        out_shape=(jax.ShapeDtypeStruct((b, s, n), F32), jax.ShapeDtypeStruct((b, s, d), BF16)),
        in_specs=[pl.BlockSpec((None, tm, d), lambda i, j: (i, j, 0)),
                  pl.BlockSpec((1, d), lambda i, j: (0, 0)),
                  pl.BlockSpec((None, 1, d), lambda i, j: (i, 0, 0)),
                  pl.BlockSpec((None, 1, d), lambda i, j: (i, 0, 0)),
                  pl.BlockSpec((d, n), lambda i, j: (0, 0))],
        out_specs=(pl.BlockSpec((None, tm, n), lambda i, j: (i, j, 0)),
                   pl.BlockSpec((None, tm, d), lambda i, j: (i, j, 0))),
        compiler_params=_cp(("parallel", "parallel")),
    )(x, g, scale, shift, w)


def _cat_refs(refs):
    vals = [r[...] for r in refs]
    return vals[0] if len(vals) == 1 else jnp.concatenate(vals, axis=-1)


def _gate_outproj(mix_parts, proj, gate_blk, w_out, x, gmod, name):
    b, s, _ = x.shape
    wd, d = w_out.shape
    tm = min(s, 256)
    npart = len(mix_parts)

    def body(*refs):
        mix_refs = refs[:npart]
        gate_ref, w_ref, x_ref, gm_ref, xo_ref, o_ref = refs[npart:]
        gt = gate_ref[...]
        y = (_cat_refs(mix_refs) * (gt * _sigmoid(gt))).astype(BF16)
        o = _dot(y, w_ref[...])
        o_ref[...] = o.astype(BF16)
        xo_ref[...] = x_ref[...] + gm_ref[...] * o

    return pl.pallas_call(
        body, name=name, grid=(b, s // tm),
        out_shape=(jax.ShapeDtypeStruct((b, s, d), F32), jax.ShapeDtypeStruct((b, s, d), BF16)),
        in_specs=[pl.BlockSpec((None, tm, p.shape[2]), lambda i, j: (i, j, 0)) for p in mix_parts] + [
                  pl.BlockSpec((None, tm, wd), lambda i, j: (i, j, gate_blk)),
                  pl.BlockSpec((wd, d), lambda i, j: (0, 0)),
                  pl.BlockSpec((None, tm, d), lambda i, j: (i, j, 0)),
                  pl.BlockSpec((None, 1, d), lambda i, j: (i, 0, 0))],
        out_specs=(pl.BlockSpec((None, tm, d), lambda i, j: (i, j, 0)),
                   pl.BlockSpec((None, tm, d), lambda i, j: (i, j, 0))),
        compiler_params=_cp(("parallel", "parallel")),
    )(*mix_parts, proj, w_out, x, gmod)


def _final_loss(x, g, target, name):
    b, s, d = x.shape
    tm = min(s, 256)

    def body(x_ref, g_ref, t_ref, loss_ref, dx_ref, dg_ref):
        first = jnp.logical_and(pl.program_id(0) == 0, pl.program_id(1) == 0)

        @pl.when(first)
        def _():
            loss_ref[...] = jnp.zeros_like(loss_ref)
            dg_ref[...] = jnp.zeros_like(dg_ref)

        xv = x_ref[...]
        gv = g_ref[...]
        rstd = lax.rsqrt(jnp.mean(xv * xv, axis=-1, keepdims=True) + EPS)
        xhat = xv * rstd
        err = xhat * gv - t_ref[...]
        row = jnp.mean(err * err, axis=-1, keepdims=True)
        loss_ref[...] += 0.5 * jnp.sum(row, axis=0, keepdims=True)
        dy = err * (1.0 / d)
        dg_ref[...] += jnp.sum(dy * xhat, axis=0, keepdims=True)
        dxh = dy * gv
        dx_ref[...] = rstd * (dxh - xhat * jnp.mean(dxh * xhat, axis=-1, keepdims=True))

    return pl.pallas_call(
        body, name=name, grid=(b, s // tm),
        out_shape=(jax.ShapeDtypeStruct((1, LANE), F32), jax.ShapeDtypeStruct((b, s, d), F32),
                   jax.ShapeDtypeStruct((1, d), F32)),
        in_specs=[pl.BlockSpec((None, tm, d), lambda i, j: (i, j, 0)),
                  pl.BlockSpec((1, d), lambda i, j: (0, 0)),
                  pl.BlockSpec((None, tm, d), lambda i, j: (i, j, 0))],
        out_specs=(pl.BlockSpec((1, LANE), lambda i, j: (0, 0)),
                   pl.BlockSpec((None, tm, d), lambda i, j: (i, j, 0)),
                   pl.BlockSpec((1, d), lambda i, j: (0, 0))),
        compiler_params=_cp(("arbitrary", "arbitrary")),
    )(x, g, target)


def _bwd_out(dxo, gmod, o, mix_parts, proj, gate_blk, w_out_t, name):
    b, s, d = dxo.shape
    wd = w_out_t.shape[1]
    tm = min(s, 256)
    npart = len(mix_parts)

    def body(dx_ref, gm_ref, o_ref, *refs):
        mix_refs = refs[:npart]
        gate_ref, wt_ref, dmix_ref, dgate_ref, do_ref, y_ref, dgm_ref = refs[npart:]

        @pl.when(pl.program_id(1) == 0)
        def _():
            dgm_ref[...] = jnp.zeros_like(dgm_ref)

        dx = dx_ref[...]
        dgm_ref[...] += jnp.sum(dx * o_ref[...].astype(F32), axis=0, keepdims=True)
        dob = (gm_ref[...] * dx).astype(BF16)
        do_ref[...] = dob
        dy = _dot(dob, wt_ref[...])
        gt = gate_ref[...]
        sg = _sigmoid(gt)
        silu = gt * sg
        mx = _cat_refs(mix_refs)
        y_ref[...] = (mx * silu).astype(BF16)
        dmix_ref[...] = dy * silu
        dgate_ref[...] = (dy * mx * (sg * (1.0 + gt * (1.0 - sg)))).astype(BF16)

    row = lambda i, j: (i, j, 0)
    return pl.pallas_call(
        body, name=name, grid=(b, s // tm),
        out_shape=(jax.ShapeDtypeStruct((b, s, wd), F32), jax.ShapeDtypeStruct((b, s, wd), BF16),
                   jax.ShapeDtypeStruct((b, s, d), BF16), jax.ShapeDtypeStruct((b, s, wd), BF16),
                   jax.ShapeDtypeStruct((b, 1, d), F32)),
        in_specs=[pl.BlockSpec((None, tm, d), row),
                  pl.BlockSpec((None, 1, d), lambda i, j: (i, 0, 0)),
                  pl.BlockSpec((None, tm, d), row)] + [
                  pl.BlockSpec((None, tm, p.shape[2]), row) for p in mix_parts] + [
                  pl.BlockSpec((None, tm, wd), lambda i, j: (i, j, gate_blk)),
                  pl.BlockSpec((d, wd), lambda i, j: (0, 0))],
        out_specs=(pl.BlockSpec((None, tm, wd), row), pl.BlockSpec((None, tm, wd), row),
                   pl.BlockSpec((None, tm, d), row), pl.BlockSpec((None, tm, wd), row),
                   pl.BlockSpec((None, 1, d), lambda i, j: (i, 0, 0))),
        compiler_params=_cp(("parallel", "arbitrary")),
    )(dxo, gmod, o, *mix_parts, proj, w_out_t)


def _bwd_in(dproj_parts, w_in_t, x, g, scale, dxo, name):
    b, s, d = x.shape
    n = w_in_t.shape[0]
    tm = min(s, 256)
    npart = len(dproj_parts)

    def body(*refs):
        dp_refs = refs[:npart]
        wt_ref, x_ref, g_ref, sc_ref, dxo_ref, dx_ref, dsh_ref, dsc_ref, dg_ref = refs[npart:]

        @pl.when(jnp.logical_and(pl.program_id(0) == 0, pl.program_id(1) == 0))
        def _():
            dg_ref[...] = jnp.zeros_like(dg_ref)

        @pl.when(pl.program_id(1) == 0)
        def _():
            dsh_ref[...] = jnp.zeros_like(dsh_ref)
            dsc_ref[...] = jnp.zeros_like(dsc_ref)

        dh = _dot(_cat_refs(dp_refs), wt_ref[...])
        xv = x_ref[...]
        gv = g_ref[...]
        one_sc = 1.0 + sc_ref[...]
        rstd = lax.rsqrt(jnp.mean(xv * xv, axis=-1, keepdims=True) + EPS)
        xhat = xv * rstd
        dsh_ref[...] += jnp.sum(dh, axis=0, keepdims=True)
        dsc_ref[...] += jnp.sum(dh * (xhat * gv), axis=0, keepdims=True)
        dhs = dh * one_sc
        dg_ref[...] += jnp.sum(dhs * xhat, axis=0, keepdims=True)
        dxh = dhs * gv
        dx_ref[...] = dxo_ref[...] + rstd * (dxh - xhat * jnp.mean(dxh * xhat, axis=-1, keepdims=True))

    row = lambda i, j: (i, j, 0)
    per_b = lambda i, j: (i, 0, 0)
    return pl.pallas_call(
        body, name=name, grid=(b, s // tm),
        out_shape=(jax.ShapeDtypeStruct((b, s, d), F32), jax.ShapeDtypeStruct((b, 1, d), F32),
                   jax.ShapeDtypeStruct((b, 1, d), F32), jax.ShapeDtypeStruct((1, d), F32)),
        in_specs=[pl.BlockSpec((None, tm, p.shape[2]), row) for p in dproj_parts] + [
                  pl.BlockSpec((n, d), lambda i, j: (0, 0)),
                  pl.BlockSpec((None, tm, d), row),
                  pl.BlockSpec((1, d), lambda i, j: (0, 0)),
                  pl.BlockSpec((None, 1, d), per_b),
                  pl.BlockSpec((None, tm, d), row)],
        out_specs=(pl.BlockSpec((None, tm, d), row), pl.BlockSpec((None, 1, d), per_b),
                   pl.BlockSpec((None, 1, d), per_b), pl.BlockSpec((1, d), lambda i, j: (0, 0))),
        compiler_params=_cp(("arbitrary", "arbitrary")),
    )(*dproj_parts, w_in_t, x, g, scale, dxo)


def _matmul_tn(a, b_parts, name):
    bsz, s, m = a.shape
    n = sum(p.shape[2] for p in b_parts)
    tk = next(c for c in (512, 256, 128) if s % c == 0)
    npart = len(b_parts)

    def body(a_ref, *refs):
        b_refs, o_ref = refs[:npart], refs[npart]

        @pl.when(jnp.logical_and(pl.program_id(0) == 0, pl.program_id(1) == 0))
        def _():
            o_ref[...] = jnp.zeros_like(o_ref)

        o_ref[...] += _dot_tn(a_ref[...], _cat_refs(b_refs))

    row = lambda i, k: (i, k, 0)
    return pl.pallas_call(
        body, name=name, grid=(bsz, s // tk),
        out_shape=jax.ShapeDtypeStruct((m, n), F32),
        in_specs=[pl.BlockSpec((None, tk, m), row)] + [pl.BlockSpec((None, tk, p.shape[2]), row) for p in b_parts],
        out_specs=pl.BlockSpec((m, n), lambda i, k: (0, 0)),
        compiler_params=_cp(("arbitrary", "arbitrary")),
    )(a, *b_parts)


def _rel_buckets():
    qi = np.arange(BLK)[:, None]
    kj = np.arange(2 * BLK)[None, :]
    rel = qi - kj + BLK
    n = np.maximum(rel, 0)
    nf = np.maximum(n, 1).astype(np.float32)
    large = REL_MAX_EXACT + (np.log(nf / REL_MAX_EXACT) / math.log(REL_MAX_DIST / REL_MAX_EXACT)
                             * (REL_BUCKETS - REL_MAX_EXACT)).astype(np.int32)
    large = np.minimum(large, REL_BUCKETS - 1)
    bucket = np.where(n < REL_MAX_EXACT, n, large).astype(np.int32)
    valid = ((rel >= 0) & (rel < BLK)).astype(np.int32)
    return bucket, valid


def _swa_bias(rel_bias_t, bucket, valid, name):
    nh = rel_bias_t.shape[0]

    def body(rb_ref, bk_ref, vl_ref, o_ref):
        h = pl.program_id(0)
        bk = bk_ref[...]
        acc = jnp.zeros(bk.shape, F32)
        for i in range(REL_BUCKETS):
            acc = jnp.where(bk == i, rb_ref[h, i], acc)
        o_ref[...] = jnp.where(vl_ref[...] > 0, acc, NEG)

    return pl.pallas_call(
        body, name=name, grid=(nh,),
        out_shape=jax.ShapeDtypeStruct((nh, BLK, 2 * BLK), F32),
        in_specs=[pl.BlockSpec(memory_space=pltpu.SMEM),
                  pl.BlockSpec((BLK, 2 * BLK), lambda h: (0, 0)),
                  pl.BlockSpec((BLK, 2 * BLK), lambda h: (0, 0))],
        out_specs=pl.BlockSpec((None, BLK, 2 * BLK), lambda h: (h, 0, 0)),
        compiler_params=_cp(("arbitrary",)),
    )(rel_bias_t, bucket, valid)


def _swa_scores(n, qn, kc, kp, bias_ref):
    s_c = _dot_nt(qn, kc) * SCALE + bias_ref[:, BLK:]
    s_p = _dot_nt(qn, kp) * SCALE + bias_ref[:, :BLK]
    s_p = jnp.where(n > 0, s_p, NEG)
    return s_c, s_p


def _swa_fwd(q, k, v, bias, sinks, name):
    b, hq, s, hd = q.shape
    nb = s // BLK

    def body(sink_ref, q_ref, k_ref, v_ref, bias_ref, o_ref, l_ref):
        sink = sink_ref[pl.program_id(1)]

        def step(n, carry):
            r0 = pl.multiple_of(n * BLK, BLK)
            p0 = pl.multiple_of(jnp.maximum(n - 1, 0) * BLK, BLK)
            qn = q_ref[pl.ds(r0, BLK), :]
            s_c, s_p = _swa_scores(n, qn, k_ref[pl.ds(r0, BLK), :], k_ref[pl.ds(p0, BLK), :], bias_ref)
            m = jnp.maximum(jnp.maximum(jnp.max(s_c, axis=1, keepdims=True),
                                        jnp.max(s_p, axis=1, keepdims=True)), sink)
            e_c = jnp.exp(s_c - m)
            e_p = jnp.exp(s_p - m)
            den = jnp.sum(e_c, axis=1, keepdims=True) + jnp.sum(e_p, axis=1, keepdims=True) + jnp.exp(sink - m)
            inv = 1.0 / den
            o = _dot((e_c * inv).astype(BF16), v_ref[pl.ds(r0, BLK), :])
            o = o + _dot((e_p * inv).astype(BF16), v_ref[pl.ds(p0, BLK), :])
            o_ref[pl.ds(r0, BLK), :] = o
            l_ref[pl.ds(r0, BLK), :] = m + jnp.log(den)
            return carry

        lax.fori_loop(0, nb, step, 0)

    qspec = pl.BlockSpec((None, None, s, hd), lambda i, h: (i, h, 0, 0))
    kspec = pl.BlockSpec((None, None, s, hd), lambda i, h: (i, h // A_GROUP, 0, 0))
    return pl.pallas_call(
        body, name=name, grid=(b, hq),
        out_shape=(jax.ShapeDtypeStruct((b, hq, s, hd), F32), jax.ShapeDtypeStruct((b, hq, s, 1), F32)),
        in_specs=[pl.BlockSpec(memory_space=pltpu.SMEM), qspec, kspec, kspec,
                  pl.BlockSpec((None, BLK, 2 * BLK), lambda i, h: (h, 0, 0))],
        out_specs=(qspec, pl.BlockSpec((None, None, s, 1), lambda i, h: (i, h, 0, 0))),
        compiler_params=_cp(("parallel", "parallel")),
    )(sinks, q, k, v, bias)


def _swa_bwd(q, k, v, bias, sinks, do, lse, name):
    b, hq, s, hd = q.shape
    hkv = k.shape[1]
    nb = s // BLK

    def body(sink_ref, q_ref, k_ref, v_ref, bias_ref, do_ref, l_ref,
             dq_ref, dk_ref, dv_ref, db_ref, dsk_ref):
        g = pl.program_id(2)
        sink = sink_ref[pl.program_id(1) * A_GROUP + g]

        @pl.when(g == 0)
        def _():
            dk_ref[...] = jnp.zeros_like(dk_ref)
            dv_ref[...] = jnp.zeros_like(dv_ref)

        db_ref[...] = jnp.zeros_like(db_ref)

        def step(n, dsink):
            r0 = pl.multiple_of(n * BLK, BLK)
            p0 = pl.multiple_of(jnp.maximum(n - 1, 0) * BLK, BLK)
            qn = q_ref[pl.ds(r0, BLK), :]
            kc = k_ref[pl.ds(r0, BLK), :]
            kp = k_ref[pl.ds(p0, BLK), :]
            s_c, s_p = _swa_scores(n, qn, kc, kp, bias_ref)
            ln = l_ref[pl.ds(r0, BLK), :]
            p_c = jnp.exp(s_c - ln)
            p_p = jnp.exp(s_p - ln)
            p_s = jnp.exp(sink - ln)
            dob = do_ref[pl.ds(r0, BLK), :].astype(BF16)
            dp_c = _dot_nt(dob, v_ref[pl.ds(r0, BLK), :])
            dp_p = _dot_nt(dob, v_ref[pl.ds(p0, BLK), :])
            delta = jnp.sum(p_c * dp_c, axis=1, keepdims=True) + jnp.sum(p_p * dp_p, axis=1, keepdims=True)
            ds_c = p_c * (dp_c - delta)
            ds_p = p_p * (dp_p - delta)
            db_ref[:, BLK:] += ds_c
            db_ref[:, :BLK] += ds_p
            dsb_c = ds_c.astype(BF16)
            dsb_p = ds_p.astype(BF16)
            dq_ref[pl.ds(r0, BLK), :] = (_dot(dsb_c, kc) + _dot(dsb_p, kp)) * SCALE
            dk_ref[pl.ds(r0, BLK), :] += _dot_tn(dsb_c, qn) * SCALE
            dk_ref[pl.ds(p0, BLK), :] += _dot_tn(dsb_p, qn) * SCALE
            dv_ref[pl.ds(r0, BLK), :] += _dot_tn(p_c.astype(BF16), dob)
            dv_ref[pl.ds(p0, BLK), :] += _dot_tn(p_p.astype(BF16), dob)
            return dsink - jnp.sum(p_s * delta, axis=0, keepdims=True)

        dsink = lax.fori_loop(0, nb, step, jnp.zeros((1, 1), F32))
        dsk_ref[...] = jnp.broadcast_to(dsink, dsk_ref.shape)

    qspec = pl.BlockSpec((None, None, s, hd), lambda i, kv, g: (i, kv * A_GROUP + g, 0, 0))
    kspec = pl.BlockSpec((None, None, s, hd), lambda i, kv, g: (i, kv, 0, 0))
    return pl.pallas_call(
        body, name=name, grid=(b, hkv, A_GROUP),
        out_shape=(jax.ShapeDtypeStruct((b, hq, s, hd), F32), jax.ShapeDtypeStruct((b, hkv, s, hd), F32),
                   jax.ShapeDtypeStruct((b, hkv, s, hd), F32), jax.ShapeDtypeStruct((b, hq, BLK, 2 * BLK), F32),
                   jax.ShapeDtypeStruct((b, hq, 1, LANE), F32)),
        in_specs=[pl.BlockSpec(memory_space=pltpu.SMEM), qspec, kspec, kspec,
                  pl.BlockSpec((None, BLK, 2 * BLK), lambda i, kv, g: (kv * A_GROUP + g, 0, 0)),
                  qspec,
                  pl.BlockSpec((None, None, s, 1), lambda i, kv, g: (i, kv * A_GROUP + g, 0, 0))],
        out_specs=(qspec, kspec, kspec,
                   pl.BlockSpec((None, None, BLK, 2 * BLK), lambda i, kv, g: (i, kv * A_GROUP + g, 0, 0)),
                   pl.BlockSpec((None, None, 1, LANE), lambda i, kv, g: (i, kv * A_GROUP + g, 0, 0))),
        compiler_params=_cp(("parallel", "parallel", "arbitrary")),
    )(sinks, q, k, v, bias, do, lse)


def _swa_small_grads(db, dsk, bucket, name):
    b, nh = db.shape[0], db.shape[1]

    def body(db_ref, dsk_ref, bk_ref, gb_ref, gs_ref):
        acc = db_ref[0]
        sk = dsk_ref[0]
        for i in range(1, b):
            acc = acc + db_ref[i]
            sk = sk + dsk_ref[i]
        gs_ref[...] = sk
        bk = bk_ref[...]
        for i in range(REL_BUCKETS):
            part = jnp.sum(jnp.where(bk == i, acc, 0.0), axis=1, keepdims=True)
            tot = jnp.sum(part, axis=0, keepdims=True)
            gb_ref[i:i + 1, :] = jnp.broadcast_to(tot, (1, LANE))

    return pl.pallas_call(
        body, name=name, grid=(nh,),
        out_shape=(jax.ShapeDtypeStruct((nh, REL_BUCKETS, LANE), F32), jax.ShapeDtypeStruct((nh, 1, LANE), F32)),
        in_specs=[pl.BlockSpec((b, None, BLK, 2 * BLK), lambda h: (0, h, 0, 0)),
                  pl.BlockSpec((b, None, 1, LANE), lambda h: (0, h, 0, 0)),
                  pl.BlockSpec((BLK, 2 * BLK), lambda h: (0, 0))],
        out_specs=(pl.BlockSpec((None, REL_BUCKETS, LANE), lambda h: (h, 0, 0)),
                   pl.BlockSpec((None, 1, LANE), lambda h: (h, 0, 0))),
        compiler_params=_cp(("parallel",)),
    )(db, dsk, bucket)


def _log_sigmoid(z):
    return jnp.minimum(z, 0.0) - jnp.log(1.0 + jnp.exp(-jnp.abs(z)))


def _fox_decay(z, bf, name):
    b, s, w = z.shape
    nb = s // BLK

    def body(z_ref, bf_ref, f_ref):
        r = lax.broadcasted_iota(jnp.int32, (BLK, BLK), 0)
        c = lax.broadcasted_iota(jnp.int32, (BLK, BLK), 1)
        tri = (c <= r).astype(F32)

        def step(n, carry):
            r0 = pl.multiple_of(n * BLK, BLK)
            lf = _log_sigmoid(z_ref[pl.ds(r0, BLK), :] + bf_ref[...])
            f_ref[pl.ds(r0, BLK), :] = jnp.dot(tri, lf, precision=HI, preferred_element_type=F32) + carry
            return carry + jnp.sum(lf, axis=0, keepdims=True)

        lax.fori_loop(0, nb, step, jnp.zeros((1, w), F32))

    spec = pl.BlockSpec((None, s, w), lambda i: (i, 0, 0))
    return pl.pallas_call(
        body, name=name, grid=(b,), out_shape=jax.ShapeDtypeStruct((b, s, w), F32),
        in_specs=[spec, pl.BlockSpec((1, w), lambda i: (0, 0))], out_specs=spec,
        compiler_params=_cp(("parallel",)),
    )(z, bf)


def _fox_dgate(df, z, bf, nheads, name):
    b, s, w = z.shape
    nb = s // BLK

    def body(df_ref, z_ref, bf_ref, dz_ref, dbf_ref):
        @pl.when(pl.program_id(0) == 0)
        def _():
            dbf_ref[...] = jnp.zeros_like(dbf_ref)

        r = lax.broadcasted_iota(jnp.int32, (BLK, BLK), 0)
        c = lax.broadcasted_iota(jnp.int32, (BLK, BLK), 1)
        tri = (c >= r).astype(F32)
        lane = lax.broadcasted_iota(jnp.int32, (BLK, w), 1)

        def step(i, carry):
            tail, dbf = carry
            r0 = pl.multiple_of((nb - 1 - i) * BLK, BLK)
            dfb = df_ref[pl.ds(r0, BLK), :]
            dlf = jnp.dot(tri, dfb, precision=HI, preferred_element_type=F32) + tail
            dz = jnp.where(lane < nheads, dlf * _sigmoid(-(z_ref[pl.ds(r0, BLK), :] + bf_ref[...])), 0.0)
            dz_ref[pl.ds(r0, BLK), :] = dz
            return tail + jnp.sum(dfb, axis=0, keepdims=True), dbf + jnp.sum(dz, axis=0, keepdims=True)

        zero = jnp.zeros((1, w), F32)
        _, dbf = lax.fori_loop(0, nb, step, (zero, zero))
        dbf_ref[...] += dbf

    spec = pl.BlockSpec((None, s, w), lambda i: (i, 0, 0))
    one = pl.BlockSpec((1, w), lambda i: (0, 0))
    return pl.pallas_call(
        body, name=name, grid=(b,),
        out_shape=(jax.ShapeDtypeStruct((b, s, w), F32), jax.ShapeDtypeStruct((1, w), F32)),
        in_specs=[spec, spec, one], out_specs=(spec, one),
        compiler_params=_cp(("arbitrary",)),
    )(df, z, bf)


def _fox_segments(nb):
    per = max(1, nb // 4)
    return per, nb // per


def _head_masks(shape, axis):
    idx = lax.broadcasted_iota(jnp.int32, shape, axis)
    return idx < HEAD_DIM, idx >= HEAD_DIM


def _fox_fwd(proj, qblk, kblk, vblk, fcol, frow, name):
    b, s, _ = proj.shape
    nh = fcol.shape[1]
    npair = nh // 2
    per, nseg = _fox_segments(s // BLK)

    def body(q_ref, k_ref, v_ref, fc_ref, fr_ref, o_ref, l_ref, qm_ref, kt_ref, vb_ref):
        lo, hi = _head_masks((s, LANE), 1)
        qv = q_ref[...]
        qm_ref[0] = jnp.where(lo, qv, 0.0).astype(BF16)
        qm_ref[1] = jnp.where(hi, qv, 0.0).astype(BF16)
        kt_ref[...] = k_ref[...].T.astype(BF16)
        vb_ref[...] = v_ref[...].astype(BF16)
        lane_lo = lax.broadcasted_iota(jnp.int32, (BLK, LANE), 1) < HEAD_DIM
        for seg in range(nseg):
            w = (seg + 1) * per * BLK
            causal = (lax.broadcasted_iota(jnp.int32, (BLK, w), 1)
                      - lax.broadcasted_iota(jnp.int32, (BLK, w), 0))

            def qstep(n, carry):
                r0 = pl.multiple_of(n * BLK, BLK)
                outs = []
                for hh in range(2):
                    sc = _dot(qm_ref[hh, pl.ds(r0, BLK), :], kt_ref[:, :w]) * SCALE
                    sc = sc + (fc_ref[hh, pl.ds(r0, BLK), :] - fr_ref[hh, :, :w])
                    sc = jnp.where(causal <= n * BLK, sc, NEG)
                    m = jnp.max(sc, axis=1, keepdims=True)
                    e = jnp.exp(sc - m)
                    l = jnp.sum(e, axis=1, keepdims=True)
                    outs.append(_dot((e * (1.0 / l)).astype(BF16), vb_ref[:w, :]))
                    l_ref[hh, pl.ds(r0, BLK), :] = m + jnp.log(l)
                o_ref[pl.ds(r0, BLK), :] = jnp.where(lane_lo, outs[0], outs[1])
                return carry

            lax.fori_loop(seg * per, (seg + 1) * per, qstep, 0)

    def tok(blk):
        return pl.BlockSpec((None, s, LANE), lambda i, p: (i, 0, blk + p))

    col = pl.BlockSpec((None, 2, s, 1), lambda i, p: (i, p, 0, 0))
    rowspec = pl.BlockSpec((None, 2, 1, s), lambda i, p: (i, p, 0, 0))
    return pl.pallas_call(
        body, name=name, grid=(b, npair),
        out_shape=(jax.ShapeDtypeStruct((b, s, nh * HEAD_DIM), F32), jax.ShapeDtypeStruct((b, nh, s, 1), F32)),
        in_specs=[tok(qblk), tok(kblk), tok(vblk), col, rowspec],
        out_specs=(pl.BlockSpec((None, s, LANE), lambda i, p: (i, 0, p)), col),
        scratch_shapes=[pltpu.VMEM((2, s, LANE), BF16), pltpu.VMEM((LANE, s), BF16), pltpu.VMEM((s, LANE), BF16)],
        compiler_params=_cp(("parallel", "parallel")),
    )(proj, proj, proj, fcol, frow)


def _fox_bwd(proj, qblk, kblk, vblk, dmix, doblk, fcol, frow, frowb, lse, lserowb, name):
    b, s, _ = proj.shape
    nh = fcol.shape[1]
    npair = nh // 2
    nb = s // BLK
    per, nseg = _fox_segments(nb)

    def body(q_ref, k_ref, v_ref, do_ref, fc_ref, fr_ref, frb_ref, l_ref, lrb_ref,
             dq_ref, dk_ref, dv_ref, dfr_ref,
             qm_ref, dom_ref, kb_ref, vb_ref, kt_ref, vt_ref, qtm_ref, dotm_ref, dka_ref, dva_ref):
        lo, hi = _head_masks((s, LANE), 1)
        qv = q_ref[...]
        dov = do_ref[...]
        for hh, msk in enumerate((lo, hi)):
            qm_ref[hh] = jnp.where(msk, qv, 0.0).astype(BF16)
            dom_ref[hh] = jnp.where(msk, dov, 0.0).astype(BF16)
        kv = k_ref[...]
        vv = v_ref[...]
        kb_ref[...] = kv.astype(BF16)
        vb_ref[...] = vv.astype(BF16)
        kt_ref[...] = kv.T.astype(BF16)
        vt_ref[...] = vv.T.astype(BF16)
        rlo, rhi = _head_masks((LANE, BLK), 0)

        def tstep(n, carry):
            r0 = pl.multiple_of(n * BLK, BLK)
            qt = q_ref[pl.ds(r0, BLK), :].T
            dt = do_ref[pl.ds(r0, BLK), :].T
            for hh, msk in enumerate((rlo, rhi)):
                qtm_ref[hh, n] = jnp.where(msk, qt, 0.0).astype(BF16)
                dotm_ref[hh, n] = jnp.where(msk, dt, 0.0).astype(BF16)
            return carry

        lax.fori_loop(0, nb, tstep, 0)
        dka_ref[...] = jnp.zeros_like(dka_ref)
        dva_ref[...] = jnp.zeros_like(dva_ref)
        dfr_ref[...] = jnp.zeros_like(dfr_ref)
        lane_lo = lax.broadcasted_iota(jnp.int32, (BLK, LANE), 1) < HEAD_DIM
        for seg in range(nseg):
            w = (seg + 1) * per * BLK
            causal = (lax.broadcasted_iota(jnp.int32, (BLK, w), 1)
                      - lax.broadcasted_iota(jnp.int32, (BLK, w), 0))
            causal_t = (lax.broadcasted_iota(jnp.int32, (w, BLK), 0)
                        - lax.broadcasted_iota(jnp.int32, (w, BLK), 1))

            def nstep(n, carry):
                r0 = pl.multiple_of(n * BLK, BLK)
                dqs = []
                for hh in range(2):
                    qn = qm_ref[hh, pl.ds(r0, BLK), :]
                    don = dom_ref[hh, pl.ds(r0, BLK), :]
                    sc = _dot(qn, kt_ref[:, :w]) * SCALE + (fc_ref[hh, pl.ds(r0, BLK), :] - fr_ref[hh, :, :w])
                    p = jnp.exp(jnp.where(causal <= n * BLK, sc, NEG) - l_ref[hh, pl.ds(r0, BLK), :])
                    dp = _dot(don, vt_ref[:, :w])
                    ds = p * (dp - jnp.sum(p * dp, axis=1, keepdims=True))
                    dqs.append(_dot(ds.astype(BF16), kb_ref[:w, :]))
                    dfr_ref[hh, :, :w] -= jnp.sum(ds, axis=0, keepdims=True)
                    sct = _dot(kb_ref[:w, :], qtm_ref[hh, n]) * SCALE + (frb_ref[hh, n] - fc_ref[hh, :w, :])
                    pt = jnp.exp(jnp.where(causal_t <= n * BLK, sct, NEG) - lrb_ref[hh, n])
                    dpt = _dot(vb_ref[:w, :], dotm_ref[hh, n])
                    dst = pt * (dpt - jnp.sum(pt * dpt, axis=0, keepdims=True))
                    dka_ref[:w, :] += _dot(dst.astype(BF16), qn)
                    dva_ref[:w, :] += _dot(pt.astype(BF16), don)
                dq_ref[pl.ds(r0, BLK), :] = (jnp.where(lane_lo, dqs[0], dqs[1]) * SCALE).astype(BF16)
                return carry

            lax.fori_loop(seg * per, (seg + 1) * per, nstep, 0)
        dk_ref[...] = (dka_ref[...] * SCALE).astype(BF16)
        dv_ref[...] = dva_ref[...].astype(BF16)

    def tok(blk):
        return pl.BlockSpec((None, s, LANE), lambda i, p: (i, 0, blk + p))

    col = pl.BlockSpec((None, 2, s, 1), lambda i, p: (i, p, 0, 0))
    rowspec = pl.BlockSpec((None, 2, 1, s), lambda i, p: (i, p, 0, 0))
    rowbspec = pl.BlockSpec((None, 2, nb, 1, BLK), lambda i, p: (i, p, 0, 0, 0))
    outtok = pl.BlockSpec((None, s, LANE), lambda i, p: (i, 0, p))
    shp = jax.ShapeDtypeStruct((b, s, nh * HEAD_DIM), BF16)
    return pl.pallas_call(
        body, name=name, grid=(b, npair),
        out_shape=(shp, shp, shp, jax.ShapeDtypeStruct((b, nh, 1, s), F32)),
        in_specs=[tok(qblk), tok(kblk), tok(vblk),
                  pl.BlockSpec((None, s, LANE), lambda i, p: (i, 0, doblk + p)),
                  col, rowspec, rowbspec, col, rowbspec],
        out_specs=(outtok, outtok, outtok, rowspec),
        scratch_shapes=[pltpu.VMEM((2, s, LANE), BF16), pltpu.VMEM((2, s, LANE), BF16),
                        pltpu.VMEM((s, LANE), BF16), pltpu.VMEM((s, LANE), BF16),
                        pltpu.VMEM((LANE, s), BF16), pltpu.VMEM((LANE, s), BF16),
                        pltpu.VMEM((2, nb, LANE, BLK), BF16), pltpu.VMEM((2, nb, LANE, BLK), BF16),
                        pltpu.VMEM((s, LANE), F32), pltpu.VMEM((s, LANE), F32)],
        compiler_params=_cp(("parallel", "parallel")),
    )(proj, proj, proj, dmix, fcol, frow, frowb, lse, lserowb)


def _expm1(x):
    poly = x * (1.0 + x * (1.0 / 2.0) * (1.0 + x * (1.0 / 3.0) * (1.0 + x * (1.0 / 4.0) * (1.0 + x * (1.0 / 5.0)
                                                                                          * (1.0 + x * (1.0 / 6.0))))))
    return jnp.where(x > -0.1, poly, jnp.exp(x) - 1.0)


def _softplus(z):
    return jnp.maximum(z, 0.0) + jnp.log(1.0 + jnp.exp(-jnp.abs(z)))


def _scan_up(a, u, carry, row):
    tc = a.shape[0]
    d = 1
    while d < tc:
        keep = row >= d
        a_sh = jnp.where(keep, pltpu.roll(a, d, 0), 1.0)
        u_sh = jnp.where(keep, pltpu.roll(u, d, 0), 0.0)
        u = a * u_sh + u
        a = a * a_sh
        d *= 2
    return u + a * carry


def _scan_down(bnext, g, carry, row):
    tc = g.shape[0]
    a, u = bnext, g
    d = 1
    while d < tc:
        keep = row < tc - d
        a_sh = jnp.where(keep, pltpu.roll(a, tc - d, 0), 1.0)
        u_sh = jnp.where(keep, pltpu.roll(u, tc - d, 0), 0.0)
        u = a * u_sh + u
        a = a * a_sh
        d *= 2
    return u + a * carry


def _pick_row(val, row, which):
    return jnp.sum(jnp.where(row == which, val, 0.0), axis=0, keepdims=True)


def _lru_gates(xpad_ref, t0, tc, cw_ref, cb_ref, wa, ba_ref, wx, bx_ref, sp):
    xw = xpad_ref[pl.ds(t0, tc + SUBLANE), :]
    xc = cb_ref[...]
    for j in range(CONV_WIDTH):
        sh = CONV_WIDTH - 1 - j
        xs = xw if sh == 0 else pltpu.roll(xw, sh, 0)
        xc = xc + xs[SUBLANE:, :] * cw_ref[j:j + 1, :]
    xcb = xc.astype(BF16)
    r = _sigmoid(_dot(xcb, wa) + ba_ref[...])
    i = _sigmoid(_dot(xcb, wx) + bx_ref[...])
    la = -LRU_C * r * sp
    return xc, r, i, la


def _lru_specs(s, cb):
    seq = lambda bi, ni: (bi, 0, ni)
    return dict(
        seq=pl.BlockSpec((None, s, cb), seq),
        cw=pl.BlockSpec((CONV_WIDTH, cb), lambda bi, ni: (0, ni)),
        vec=pl.BlockSpec((1, cb), lambda bi, ni: (0, ni)),
        wblk=pl.BlockSpec((None, cb, cb), lambda bi, ni: (ni, 0, 0)),
    )


def _lru_fwd(proj, cw, cb_, wa, ba, wx, bx, lam, name):
    b, s, _ = proj.shape
    nblk, cb, _ = wa.shape
    tc = min(s, SCAN_CHUNK)
    nc = s // tc

    def body(x_ref, cw_ref, cb_ref, wa_ref, ba_ref, wx_ref, bx_ref, lam_ref, hs_ref, xpad_ref):
        xpad_ref[0:SUBLANE, :] = jnp.zeros((SUBLANE, cb), F32)
        xpad_ref[SUBLANE:, :] = x_ref[...]
        wa_b = wa_ref[...].astype(BF16)
        wx_b = wx_ref[...].astype(BF16)
        sp = _softplus(-lam_ref[...])
        row = lax.broadcasted_iota(jnp.int32, (tc, cb), 0)

        def chunk(ci, carry):
            t0 = pl.multiple_of(ci * tc, tc)
            xc, r, i, la = _lru_gates(xpad_ref, t0, tc, cw_ref, cb_ref, wa_b, ba_ref, wx_b, bx_ref, sp)
            a = jnp.exp(la)
            u = jnp.sqrt(-_expm1(2.0 * la)) * (i * xc)
            h = _scan_up(a, u, carry, row)
            hs_ref[pl.ds(t0, tc), :] = h
            return _pick_row(h, row, tc - 1)

        lax.fori_loop(0, nc, chunk, jnp.zeros((1, cb), F32))

    sp_ = _lru_specs(s, cb)
    return pl.pallas_call(
        body, name=name, grid=(b, nblk),
        out_shape=jax.ShapeDtypeStruct((b, s, nblk * cb), F32),
        in_specs=[sp_["seq"], sp_["cw"], sp_["vec"], sp_["wblk"], sp_["vec"], sp_["wblk"], sp_["vec"], sp_["vec"]],
        out_specs=sp_["seq"],
        scratch_shapes=[pltpu.VMEM((s + SUBLANE, cb), F32)],
        compiler_params=_cp(("parallel", "parallel")),
    )(proj, cw, cb_, wa, ba, wx, bx, lam)


def _lru_bwd(proj, hs, dhs, cw, cb_, wa, ba, wx, bx, lam, name):
    b, s, _ = proj.shape
    nblk, cb, _ = wa.shape
    tc = min(s, SCAN_CHUNK)
    nc = s // tc

    def body(x_ref, hs_ref, dhs_ref, cw_ref, cb_ref, wa_ref, ba_ref, wx_ref, bx_ref, lam_ref,
             dx_ref, dcw_ref, dcb_ref, dwa_ref, dba_ref, dwx_ref, dbx_ref, dlam_ref,
             xpad_ref, hpad_ref, dcpad_ref, xc_ref, r_ref, i_ref, a_ref):
        @pl.when(pl.program_id(1) == 0)
        def _():
            for ref in (dcw_ref, dcb_ref, dwa_ref, dba_ref, dwx_ref, dbx_ref, dlam_ref):
                ref[...] = jnp.zeros_like(ref)

        zeros8 = jnp.zeros((SUBLANE, cb), F32)
        xpad_ref[0:SUBLANE, :] = zeros8
        xpad_ref[SUBLANE:, :] = x_ref[...]
        hpad_ref[0:SUBLANE, :] = zeros8
        hpad_ref[SUBLANE:, :] = hs_ref[...]
        dcpad_ref[s:s + SUBLANE, :] = zeros8
        wa_b = wa_ref[...].astype(BF16)
        wx_b = wx_ref[...].astype(BF16)
        lam_v = lam_ref[...]
        sp = _softplus(-lam_v)
        dsp_dlam = -_sigmoid(-lam_v)
        row = lax.broadcasted_iota(jnp.int32, (tc, cb), 0)

        def recompute(ci, carry):
            t0 = pl.multiple_of(ci * tc, tc)
            xc, r, i, la = _lru_gates(xpad_ref, t0, tc, cw_ref, cb_ref, wa_b, ba_ref, wx_b, bx_ref, sp)
            xc_ref[pl.ds(t0, tc), :] = xc
            r_ref[pl.ds(t0, tc), :] = r
            i_ref[pl.ds(t0, tc), :] = i
            a_ref[pl.ds(t0, tc), :] = jnp.exp(la)
            return carry

        lax.fori_loop(0, nc, recompute, 0)

        def adjoint(k, carry):
            g_next, a_first_next = carry
            t0 = pl.multiple_of((nc - 1 - k) * tc, tc)
            a = a_ref[pl.ds(t0, tc), :]
            a_next = jnp.where(row == tc - 1, a_first_next, pltpu.roll(a, tc - 1, 0))
            gg = _scan_down(a_next, dhs_ref[pl.ds(t0, tc), :], g_next, row)
            h_prev = pltpu.roll(hpad_ref[pl.ds(t0, tc + SUBLANE), :], 1, 0)[SUBLANE:, :]
            xc = xc_ref[pl.ds(t0, tc), :]
            r = r_ref[pl.ds(t0, tc), :]
            i = i_ref[pl.ds(t0, tc), :]
            mult = jnp.sqrt(-_expm1(-2.0 * LRU_C * r * sp))
            d_mult = gg * i * xc
            d_i = gg * mult * xc
            d_xc = gg * mult * i
            d_la = gg * h_prev * a - d_mult * (a * a) / mult
            d_zr = (d_la * (-LRU_C * sp)) * r * (1.0 - r)
            d_zi = d_i * i * (1.0 - i)
            dlam_ref[...] += jnp.sum(d_la * (-LRU_C * r), axis=0, keepdims=True) * dsp_dlam
            dzr_b = d_zr.astype(BF16)
            dzi_b = d_zi.astype(BF16)
            xcb = xc.astype(BF16)
            d_xc = d_xc + _dot_nt(dzr_b, wa_b) + _dot_nt(dzi_b, wx_b)
            dwa_ref[...] += _dot_tn(xcb, dzr_b)
            dwx_ref[...] += _dot_tn(xcb, dzi_b)
            dba_ref[...] += jnp.sum(d_zr, axis=0, keepdims=True)
            dbx_ref[...] += jnp.sum(d_zi, axis=0, keepdims=True)
            dcb_ref[...] += jnp.sum(d_xc, axis=0, keepdims=True)
            dcpad_ref[pl.ds(t0, tc), :] = d_xc
            return _pick_row(gg, row, 0), _pick_row(a, row, 0)

        zero = jnp.zeros((1, cb), F32)
        lax.fori_loop(0, nc, adjoint, (zero, zero))

        def conv_back(ci, carry):
            t0 = pl.multiple_of(ci * tc, tc)
            dw = dcpad_ref[pl.ds(t0, tc + SUBLANE), :]
            xw = xpad_ref[pl.ds(t0, tc + SUBLANE), :]
            d_xc = dw[:tc, :]
            dxr = jnp.zeros((tc, cb), F32)
            for j in range(CONV_WIDTH):
                sh = CONV_WIDTH - 1 - j
                dsh = dw if sh == 0 else pltpu.roll(dw, tc + SUBLANE - sh, 0)
                dxr = dxr + dsh[:tc, :] * cw_ref[j:j + 1, :]
                xs = xw if sh == 0 else pltpu.roll(xw, sh, 0)
                dcw_ref[j:j + 1, :] += jnp.sum(d_xc * xs[SUBLANE:, :], axis=0, keepdims=True)
            dx_ref[pl.ds(t0, tc), :] = dxr.astype(BF16)
            return carry

        lax.fori_loop(0, nc, conv_back, 0)

    seq = lambda ni, bi: (bi, 0, ni)
    seqspec = pl.BlockSpec((None, s, cb), seq)
    cwspec = pl.BlockSpec((CONV_WIDTH, cb), lambda ni, bi: (0, ni))
    vec = pl.BlockSpec((1, cb), lambda ni, bi: (0, ni))
    wblk = pl.BlockSpec((None, cb, cb), lambda ni, bi: (ni, 0, 0))
    w = nblk * cb
    return pl.pallas_call(
        body, name=name, grid=(nblk, b),
        out_shape=(jax.ShapeDtypeStruct((b, s, w), BF16), jax.ShapeDtypeStruct((CONV_WIDTH, w), F32),
                   jax.ShapeDtypeStruct((1, w), F32), jax.ShapeDtypeStruct((nblk, cb, cb), F32),
                   jax.ShapeDtypeStruct((1, w), F32), jax.ShapeDtypeStruct((nblk, cb, cb), F32),
                   jax.ShapeDtypeStruct((1, w), F32), jax.ShapeDtypeStruct((1, w), F32)),
        in_specs=[seqspec, seqspec, seqspec, cwspec, vec, wblk, vec, wblk, vec, vec],
        out_specs=(seqspec, cwspec, vec, wblk, vec, wblk, vec, vec),
        scratch_shapes=[pltpu.VMEM((s + SUBLANE, cb), F32)] * 3 + [pltpu.VMEM((s, cb), F32)] * 4,
        compiler_params=_cp(("parallel", "arbitrary")),
    )(proj, hs, dhs, cw, cb_, wa, ba, wx, bx, lam)


def _adamw(w, g, m, v, name):
    shape = w.shape
    total = int(np.prod(shape))
    if w.ndim >= 2 and shape[-2] % SUBLANE == 0:
        cols = shape[-1]
    else:
        cols = 1024
    rows = -(-total // cols)
    rows = -(-rows // SUBLANE) * SUBLANE
    tr = _row_tile(rows, 512)
    pad = rows * cols - total

    def flat(a):
        if pad:
            a = jnp.pad(a.reshape(-1), (0, pad))
        return a.reshape(rows, cols)

    c1 = 1.0 - ADAM_B1 ** ADAM_STEP
    c2 = 1.0 - ADAM_B2 ** ADAM_STEP

    def body(w_ref, g_ref, m_ref, v_ref, d_ref, nm_ref, nv_ref):
        gv = g_ref[...]
        nm = ADAM_B1 * m_ref[...] + (1.0 - ADAM_B1) * gv
        nv = ADAM_B2 * v_ref[...] + (1.0 - ADAM_B2) * (gv * gv)
        nm_ref[...] = nm
        nv_ref[...] = nv
        d_ref[...] = -ADAM_LR * ((nm / c1) / (jnp.sqrt(nv / c2) + ADAM_EPS) + ADAM_WD * w_ref[...])

    spec = pl.BlockSpec((tr, cols), lambda i: (i, 0))
    shp = jax.ShapeDtypeStruct((rows, cols), F32)
    outs = pl.pallas_call(
        body, name=name, grid=(rows // tr,), out_shape=(shp, shp, shp),
        in_specs=[spec] * 4, out_specs=(spec,) * 3,
        compiler_params=_cp(("parallel",)),
    )(flat(w), flat(g), flat(m), flat(v))
    if pad:
        return tuple(o.reshape(-1)[:total].reshape(shape) for o in outs)
    return tuple(o.reshape(shape) for o in outs)


def _to_heads(t, nh):
    b, s, _ = t.shape
    return t.reshape(b, s, nh, HEAD_DIM).transpose(0, 2, 1, 3)


def _from_heads(t):
    b, nh, s, hd = t.shape
    return t.transpose(0, 2, 1, 3).reshape(b, s, nh * hd)


def _pad_rows(a, mult):
    r = a.shape[0]
    p = (-r) % mult
    return jnp.pad(a, ((0, p), (0, 0))) if p else a


def kernel(x, c, rel_bias, norm_g, ada_w, ada_b, attn_w_in, attn_sinks, attn_b_f, attn_w_out, lru_w_in, lru_conv_w, lru_conv_b, lru_w_a, lru_b_a, lru_w_x, lru_b_x, lru_lambda, lru_w_out, final_g, loss_target, m_rel_bias, m_norm_g, m_ada_w, m_ada_b, m_attn_w_in, m_attn_sinks, m_attn_b_f, m_attn_w_out, m_lru_w_in, m_lru_conv_w, m_lru_conv_b, m_lru_w_a, m_lru_b_a, m_lru_w_x, m_lru_b_x, m_lru_lambda, m_lru_w_out, m_final_g, v_rel_bias, v_norm_g, v_ada_w, v_ada_b, v_attn_w_in, v_attn_sinks, v_attn_b_f, v_attn_w_out, v_lru_w_in, v_lru_conv_w, v_lru_conv_b, v_lru_w_a, v_lru_b_a, v_lru_w_x, v_lru_b_x, v_lru_lambda, v_lru_w_out, v_final_g):
    bl, s, d = x.shape
    ix, iy, ic = lax.axis_index("x"), lax.axis_index("y"), lax.axis_index("c")
    chip = 2 * ix + iy
    me = 2 * chip + ic
    nb = s // BLK
    aw = A_Q_HEADS * HEAD_DIM
    akv = A_KV_HEADS * HEAD_DIM
    bw = B_HEADS * HEAD_DIM
    mixw = aw + bw
    qkv_w = aw + 2 * akv + 3 * bw
    n_in = attn_w_in.shape[2] * N_CHIP
    lw = lru_lambda.shape[1] * N_CHIP
    n0 = mixw + qkv_w + LANE

    rows_pad = -(-bl // SUBLANE) * SUBLANE
    c_all = _all_gather8(_pad_rows(c, SUBLANE), "gather_c", pltpu.VMEM)
    c_all = c_all.reshape(N_DEV, rows_pad, d)[:, :bl].reshape(N_DEV * bl, d)
    ncol = ada_w.shape[2]
    ada_w_l = lax.dynamic_index_in_dim(ada_w, ic, 0, keepdims=False)
    ada_b_l = lax.dynamic_slice(ada_b, (ic, chip * ncol), (1, ncol))
    mod_part = _ada_fwd(c_all, ada_w_l, ada_b_l, "ada_fwd")
    mod_all = _all_gather8(_pad_rows(mod_part, SUBLANE), "gather_mod", pltpu.VMEM)
    mrows = -(-(N_DEV * bl) // SUBLANE) * SUBLANE
    mod_all = mod_all.reshape(N_CHIP, 2, mrows, ncol)[:, :, :N_DEV * bl]
    mod_all = mod_all.transpose(1, 2, 0, 3).reshape(2, N_DEV * bl, N_CHIP * ncol)
    mod = lax.dynamic_slice_in_dim(mod_all, me * bl, bl, axis=1)
    shift = [mod[l, :, 0:d].reshape(bl, 1, d) for l in range(2)]
    scale = [mod[l, :, d:2 * d].reshape(bl, 1, d) for l in range(2)]
    gmod = [mod[l, :, 2 * d:3 * d].reshape(bl, 1, d) for l in range(2)]

    c_in0 = n_in // N_CHIP
    c_in1 = 2 * lw // N_CHIP
    assert c_in0 <= d and 2 * c_in1 == d
    r_in0, r_out0, r_in1, r_out1 = d // 2, mixw // N_CHIP // 2, d // 4, lw // N_CHIP // 2
    o_out0, o_in1, o_out1 = r_in0, r_in0 + r_out0, r_in0 + r_out0 + r_in1
    big_rows = o_out1 + r_out1

    def half_of(a, rows):
        return lax.dynamic_slice_in_dim(a, ic * rows, rows, axis=0)

    h_in1 = half_of(lru_w_in[0], r_in0).astype(BF16)
    my_half = jnp.concatenate([
        jnp.pad(half_of(attn_w_in[0], r_in0).astype(BF16), ((0, 0), (0, d - c_in0))),
        half_of(attn_w_out[0], r_out0).astype(BF16),
        jnp.concatenate([h_in1[:r_in1], h_in1[r_in1:]], axis=1),
        half_of(lru_w_out[0], r_out1).astype(BF16)], axis=0)
    gat = _all_gather8(my_half, "gather_weights", pltpu.HBM).reshape(N_CHIP, 2, big_rows, d)
    w_in0 = gat[:, :, :r_in0, :c_in0].transpose(1, 2, 0, 3).reshape(d, n_in)
    w_out0 = gat[:, :, o_out0:o_in1].reshape(mixw, d)
    w_in1 = gat[:, :, o_in1:o_out1].reshape(N_CHIP, 2, r_in1, 2, c_in1)
    w_in1 = w_in1.transpose(1, 3, 2, 0, 4).reshape(d, 2 * lw)
    w_out1 = gat[:, :, o_out1:].reshape(lw, d)
    w_cat0 = jnp.concatenate([w_in0[:, qkv_w + B_HEADS:], w_in0[:, :qkv_w + B_HEADS],
                              jnp.zeros((d, n0 - n_in), BF16)], axis=1)

    proj0, h0 = _norm_proj(x, norm_g[0:1], scale[0], shift[0], w_cat0, "norm_proj0")
    o_a = mixw
    aq = _to_heads(proj0[:, :, o_a:o_a + aw].astype(BF16), A_Q_HEADS)
    ak = _to_heads(proj0[:, :, o_a + aw:o_a + aw + akv].astype(BF16), A_KV_HEADS)
    av = _to_heads(proj0[:, :, o_a + aw + akv:o_a + aw + 2 * akv].astype(BF16), A_KV_HEADS)
    o_b = o_a + aw + 2 * akv
    fox_blks = (o_b // LANE, (o_b + bw) // LANE, (o_b + 2 * bw) // LANE)
    zf = proj0[:, :, o_b + 3 * bw:]
    bucket_np, valid_np = _rel_buckets()
    bucket = jnp.asarray(bucket_np)
    bias = _swa_bias(rel_bias.T, bucket, jnp.asarray(valid_np), "swa_bias")
    sinks = attn_sinks[0]
    a_out, a_lse = _swa_fwd(aq, ak, av, bias, sinks, "swa_fwd")
    bf_pad = jnp.pad(attn_b_f, ((0, 0), (0, LANE - B_HEADS)))
    fsum = _fox_decay(zf, bf_pad, "fox_decay")
    fh = fsum[:, :, :B_HEADS].transpose(0, 2, 1)
    fcol = fh.reshape(bl, B_HEADS, s, 1)
    frow = fh.reshape(bl, B_HEADS, 1, s)
    frowb = fh.reshape(bl, B_HEADS, nb, 1, BLK)
    b_out, b_lse = _fox_fwd(proj0, *fox_blks, fcol, frow, "fox_fwd")
    mix0 = [_from_heads(a_out), b_out]
    x1, o0 = _gate_outproj(mix0, proj0, 0, w_out0, x, gmod[0], "gate_outproj0")

    proj1, h1 = _norm_proj(x1, norm_g[1:2], scale[1], shift[1], w_in1, "norm_proj1")
    vec_rows = jnp.concatenate([lru_conv_w[0], lru_conv_b, lru_b_a, lru_b_x, lru_lambda], axis=0)
    vec_all = _all_gather8(vec_rows, "gather_lru_vectors", pltpu.VMEM)
    vec_all = vec_all.reshape(N_CHIP, 2, SUBLANE, lw // N_CHIP)[:, 0]
    vec_all = vec_all.transpose(1, 0, 2).reshape(SUBLANE, lw)
    cw_f, cb_f, ba_f, bx_f, lam_f = vec_all[0:4], vec_all[4:5], vec_all[5:6], vec_all[6:7], vec_all[7:8]
    hs = _lru_fwd(proj1, cw_f, cb_f, lru_w_a[0], ba_f, lru_w_x[0], bx_f, lam_f, "lru_fwd")
    x2, o1 = _gate_outproj([hs], proj1, 1, w_out1, x1, gmod[1], "gate_outproj1")

    loss_vec, dx2, g_final = _final_loss(x2, final_g.reshape(1, d), loss_target, "final_loss")
    loss = lax.psum(loss_vec[0, 0], ("x", "y", "c"))

    dhs, dgate1, do1, y1, dgm1 = _bwd_out(dx2, gmod[1], o1, [hs], proj1, 1, w_out1.T, "bwd_out1")
    g_w_out1 = _matmul_tn(y1, [do1], "grad_w_out1")
    (dxr, g_cw, g_cb, g_wa, g_ba, g_wx, g_bx, g_lam) = _lru_bwd(
        proj1, hs, dhs, cw_f, cb_f, lru_w_a[0], ba_f, lru_w_x[0], bx_f, lam_f, "lru_bwd")
    dproj1 = [dxr, dgate1]
    g_w_in1 = _matmul_tn(h1, dproj1, "grad_w_in1")
    dx1, dsh1, dsc1, g_ng1 = _bwd_in(dproj1, w_in1.T, x1, norm_g[1:2], scale[1], dx2, "bwd_in1")

    dmix0, dgate0, do0, y0, dgm0 = _bwd_out(dx1, gmod[0], o0, mix0, proj0, 0, w_out0.T, "bwd_out0")
    g_w_out0 = _matmul_tn(y0, [do0], "grad_w_out0")
    da_out = _to_heads(dmix0[:, :, :aw], A_Q_HEADS)
    daq, dak, dav, dbias, dsink = _swa_bwd(aq, ak, av, bias, sinks, da_out, a_lse, "swa_bwd")
    dbq, dbk, dbv, dfrow = _fox_bwd(proj0, *fox_blks, dmix0, aw // LANE, fcol, frow, frowb, b_lse,
                                    b_lse.reshape(bl, B_HEADS, nb, 1, BLK), "fox_bwd")
    df = dfrow.reshape(bl, B_HEADS, s).transpose(0, 2, 1)
    df = jnp.pad(df, ((0, 0), (0, 0), (0, LANE - B_HEADS)))
    dzf, g_bf = _fox_dgate(df, zf, bf_pad, B_HEADS, "fox_dgate")
    dproj0 = ([dgate0] + [_from_heads(t).astype(BF16) for t in (daq, dak, dav)]
              + [dbq, dbk, dbv, dzf.astype(BF16)])
    g_w_cat0 = _matmul_tn(h0, dproj0, "grad_w_in0")
    g_w_in0 = jnp.concatenate([g_w_cat0[:, mixw:mixw + qkv_w + B_HEADS], g_w_cat0[:, :mixw]], axis=1)
    dx0, dsh0, dsc0, g_ng0 = _bwd_in(dproj0, w_cat0.T, x, norm_g[0:1], scale[0], dx1, "bwd_in0")
    g_relb, g_sink = _swa_small_grads(dbias, dsink, bucket, "swa_small_grads")

    dmod = jnp.concatenate([jnp.concatenate([dsh0, dsc0, dgm0], axis=-1),
                            jnp.concatenate([dsh1, dsc1, dgm1], axis=-1)], axis=1)
    dmod_all = _all_gather8(_pad_rows(dmod.reshape(bl, 6 * d), SUBLANE), "gather_dmod", pltpu.VMEM)
    dmod_all = dmod_all.reshape(N_DEV, rows_pad, 6 * d)[:, :bl].reshape(N_DEV * bl, 6 * d)
    dmod_chip = lax.dynamic_slice_in_dim(dmod_all.reshape(N_DEV * bl, 2, 3 * d), chip * ncol, ncol, axis=2)
    g_ada_w, g_ada_b = _ada_bwd(c_all, dmod_chip.transpose(1, 0, 2), dmod_all, "ada_bwd")
    g_ada_b = g_ada_b.reshape(2, 3 * d)

    small_parts = [g_relb[:, :, 0].T, jnp.concatenate([g_ng0, g_ng1], axis=0), g_sink[:, 0, 0], g_bf[0, :B_HEADS],
                   g_wa, g_wx, g_final, g_cw, g_cb, g_ba, g_bx, g_lam]
    small_sizes = [int(np.prod(p.shape)) for p in small_parts]
    small_total = sum(small_sizes)
    piece_rows = -(-(-(-small_total // N_DEV) // 1024) // SUBLANE) * SUBLANE
    small_flat = jnp.concatenate([p.reshape(-1) for p in small_parts])
    small_flat = jnp.pad(small_flat, (0, N_DEV * piece_rows * 1024 - small_total))
    small_pieces = small_flat.reshape(N_CHIP, 2, piece_rows, 1024)
    p_in0 = jnp.pad(g_w_in0.reshape(2, r_in0, N_CHIP, c_in0).transpose(2, 0, 1, 3),
                    ((0, 0), (0, 0), (0, 0), (0, d - c_in0)))
    p_in1 = g_w_in1.reshape(2, 2, r_in1, N_CHIP, c_in1).transpose(3, 0, 2, 1, 4).reshape(N_CHIP, 2, r_in1, d)
    pieces = jnp.concatenate([p_in0, g_w_out0.reshape(N_CHIP, 2, r_out0, d), p_in1,
                              g_w_out1.reshape(N_CHIP, 2, r_out1, d), small_pieces], axis=2)
    theirs = _sibling_push(pieces, True, "push_sibling_halves")
    partial = _pair_sum(jnp.reshape(ic, (1,)).astype(jnp.int32), pieces, theirs, "sum_chip")
    slots = _chip_all_to_all(partial, "exchange_grads")
    reduced = _sum_slots(slots, "sum_grads")
    mine_big = reduced[:big_rows]
    other_big = _sibling_push(mine_big[None], False, "swap_halves")[0]
    both = jnp.stack([jnp.where(ic == 0, mine_big, other_big), jnp.where(ic == 0, other_big, mine_big)])
    g_big = [both[:, :r_in0, :c_in0].reshape(d, c_in0),
             both[:, o_out0:o_in1].reshape(2 * r_out0, d),
             both[:, o_in1:o_out1].reshape(2, r_in1, 2, c_in1).transpose(0, 2, 1, 3).reshape(d, c_in1),
             both[:, o_out1:].reshape(2 * r_out1, d)]
    small_all = _all_gather8(reduced[big_rows:], "gather_small_grads", pltpu.VMEM).reshape(-1)
    g_small, off = [], 0
    for p, n in zip(small_parts, small_sizes):
        g_small.append(small_all[off:off + n].reshape(p.shape))
        off += n
    (g_rel_bias, g_norm_g, g_sinks, g_b_f, g_w_a, g_w_x, g_fin, g_cw_r, g_cb_r, g_ba_r, g_bx_r, g_lam_r) = g_small
    cw4 = lw // N_CHIP

    def my_cols(a):
        return lax.dynamic_slice_in_dim(a, chip * cw4, cw4, axis=1)

    grads = {
        "rel_bias": g_rel_bias, "norm_g": g_norm_g, "ada_w": g_ada_w, "ada_b": g_ada_b,
        "attn_w_in": g_big[0][None], "attn_sinks": g_sinks[None], "attn_b_f": g_b_f[None],
        "attn_w_out": g_big[1][None], "lru_w_in": g_big[2][None], "lru_conv_w": my_cols(g_cw_r)[None],
        "lru_conv_b": my_cols(g_cb_r), "lru_w_a": g_w_a[None], "lru_b_a": my_cols(g_ba_r),
        "lru_w_x": g_w_x[None], "lru_b_x": my_cols(g_bx_r), "lru_lambda": my_cols(g_lam_r),
        "lru_w_out": g_big[3][None], "final_g": g_fin.reshape(d),
    }
    weights = dict(rel_bias=rel_bias, norm_g=norm_g, ada_w=ada_w, ada_b=ada_b, attn_w_in=attn_w_in,
                   attn_sinks=attn_sinks, attn_b_f=attn_b_f, attn_w_out=attn_w_out, lru_w_in=lru_w_in,
                   lru_conv_w=lru_conv_w, lru_conv_b=lru_conv_b, lru_w_a=lru_w_a, lru_b_a=lru_b_a,
                   lru_w_x=lru_w_x, lru_b_x=lru_b_x, lru_lambda=lru_lambda, lru_w_out=lru_w_out, final_g=final_g)
    moms = dict(rel_bias=(m_rel_bias, v_rel_bias), norm_g=(m_norm_g, v_norm_g), ada_w=(m_ada_w, v_ada_w),
                ada_b=(m_ada_b, v_ada_b), attn_w_in=(m_attn_w_in, v_attn_w_in),
                attn_sinks=(m_attn_sinks, v_attn_sinks), attn_b_f=(m_attn_b_f, v_attn_b_f),
                attn_w_out=(m_attn_w_out, v_attn_w_out), lru_w_in=(m_lru_w_in, v_lru_w_in),
                lru_conv_w=(m_lru_conv_w, v_lru_conv_w), lru_conv_b=(m_lru_conv_b, v_lru_conv_b),
                lru_w_a=(m_lru_w_a, v_lru_w_a), lru_b_a=(m_lru_b_a, v_lru_b_a), lru_w_x=(m_lru_w_x, v_lru_w_x),
                lru_b_x=(m_lru_b_x, v_lru_b_x), lru_lambda=(m_lru_lambda, v_lru_lambda),
                lru_w_out=(m_lru_w_out, v_lru_w_out), final_g=(m_final_g, v_final_g))
    names = list(weights)
    big_names = [n for n in names if weights[n].size >= 65536]
    small_names = [n for n in names if weights[n].size < 65536]
    delta, new_m, new_v = {}, {}, {}
    for n in big_names:
        delta[n], new_m[n], new_v[n] = _adamw(weights[n], grads[n].reshape(weights[n].shape),
                                              moms[n][0], moms[n][1], "adamw_" + n)
    cat = lambda arrs: jnp.concatenate([a.reshape(-1) for a in arrs])
    sd, sm, sv = _adamw(cat([weights[n] for n in small_names]), cat([grads[n] for n in small_names]),
                        cat([moms[n][0] for n in small_names]), cat([moms[n][1] for n in small_names]),
                        "adamw_small")
    off = 0
    for n in small_names:
        sz = weights[n].size
        shp = weights[n].shape
        delta[n], new_m[n], new_v[n] = (sd[off:off + sz].reshape(shp), sm[off:off + sz].reshape(shp),
                                        sv[off:off + sz].reshape(shp))
        off += sz
    out_grads = [grads[n].reshape(weights[n].shape) for n in names]
    return (loss, dx0, *out_grads, *[delta[n] for n in names], *[new_m[n] for n in names],
            *[new_v[n] for n in names])
```

```python
import functools
import math

import numpy as np
import jax
import jax.numpy as jnp
from jax import lax
from jax.experimental import pallas as pl
from jax.experimental.pallas import tpu as pltpu

F32 = jnp.float32
BF16 = jnp.bfloat16
MESH = pl.DeviceIdType.MESH

N_DEV = 8
N_CHIP = 4
HEAD_DIM = 64
BLK = 128
A_Q_HEADS = 8
A_KV_HEADS = 2
A_GROUP = A_Q_HEADS // A_KV_HEADS
B_HEADS = 8
REL_BUCKETS = 32
REL_MAX_EXACT = 16
REL_MAX_DIST = 128
LRU_BLOCKS = 8
LRU_C = 8.0
CONV_WIDTH = 4
EPS = 1e-6
NEG = -1e30
SCALE = HEAD_DIM ** -0.5
LANE = 128
SUBLANE = 8
VMEM_LIMIT = 56 * 1024 * 1024
SCAN_CHUNK = 256
ADAM_LR = 0.001
ADAM_B1 = 0.9
ADAM_B2 = 0.999
ADAM_EPS = 1e-08
ADAM_WD = 0.01
ADAM_STEP = 10
HI = lax.Precision.HIGHEST


def _cp(sem=None):
    return pltpu.CompilerParams(dimension_semantics=sem, vmem_limit_bytes=VMEM_LIMIT)


def _dot(a, b):
    return jnp.dot(a, b, preferred_element_type=F32)


def _dot_nt(a, b):
    return lax.dot_general(a, b, (((1,), (1,)), ((), ())), preferred_element_type=F32)


def _dot_tn(a, b):
    return lax.dot_general(a, b, (((0,), (0,)), ((), ())), preferred_element_type=F32)


def _sigmoid(z):
    return 1.0 / (1.0 + jnp.exp(-z))


def _row_tile(rows, cap):
    if rows <= cap:
        return rows
    best = SUBLANE
    t = SUBLANE
    while t <= cap:
        if rows % t == 0:
            best = t
        t += SUBLANE
    return best


def _all_gather8(x_shard, name, space):
    m_per, n = x_shard.shape

    def body(x_ref, out_ref, send_sems, recv_sems, local_sem):
        x, y, c = lax.axis_index("x"), lax.axis_index("y"), lax.axis_index("c")
        me, sibling = (x, y, c), (x, y, 1 - c)
        chips = [(1 - x, y), (x, 1 - y), (1 - x, 1 - y)]

        def rows(px, py, pc):
            return out_ref.at[pl.ds((4 * px + 2 * py + pc) * m_per, m_per), :]

        def copy(k, block, to, src=None):
            return pltpu.make_async_remote_copy(
                src_ref=rows(*block) if src is None else src, dst_ref=rows(*block),
                send_sem=send_sems.at[k], recv_sem=recv_sems.at[k], device_id=to, device_id_type=MESH)

        mine = pltpu.make_async_copy(x_ref, rows(*me), local_sem)
        mine.start()
        first = [copy(0, me, sibling, src=x_ref)]
        first += [copy(1 + j, me, (*chip, c), src=x_ref) for j, chip in enumerate(chips)]
        for cp in first:
            cp.start()
        passed = [copy(4 + j, (*chip, c), sibling) for j, chip in enumerate(chips)]
        for j, chip in enumerate(chips):
            copy(1 + j, (*chip, c), me).wait_recv()
            passed[j].start()
        copy(0, sibling, me).wait_recv()
        for j, chip in enumerate(chips):
            copy(4 + j, (*chip, 1 - c), me).wait_recv()
        for cp in first + passed:
            cp.wait_send()
        mine.wait()

    return pl.pallas_call(
        body, name=name,
        out_shape=jax.ShapeDtypeStruct((N_DEV * m_per, n), x_shard.dtype),
        in_specs=[pl.BlockSpec(memory_space=space)],
        out_specs=pl.BlockSpec(memory_space=space),
        scratch_shapes=[pltpu.SemaphoreType.DMA((7,)), pltpu.SemaphoreType.DMA((7,)), pltpu.SemaphoreType.DMA],
        compiler_params=pltpu.CompilerParams(vmem_limit_bytes=VMEM_LIMIT),
    )(x_shard)


def _sibling_push(blocks, pick_other, name):
    nblk = blocks.shape[0]
    m, n = blocks.shape[-2:]

    def body(x_ref, out_ref, send_sems, recv_sems):
        x, y, c = lax.axis_index("x"), lax.axis_index("y"), lax.axis_index("c")
        copies = []
        for k in range(nblk):
            src = x_ref.at[k, 1 - c] if pick_other else x_ref.at[k]
            copies.append(pltpu.make_async_remote_copy(
                src_ref=src, dst_ref=out_ref.at[k], send_sem=send_sems.at[k], recv_sem=recv_sems.at[k],
                device_id=(x, y, 1 - c), device_id_type=MESH))
        for cp in copies:
            cp.start()
        for cp in copies:
            cp.wait_recv()
        for cp in copies:
            cp.wait_send()

    hbm = pl.BlockSpec(memory_space=pltpu.HBM)
    return pl.pallas_call(
        body, name=name,
        out_shape=jax.ShapeDtypeStruct((nblk, m, n), blocks.dtype),
        in_specs=[hbm], out_specs=hbm,
        scratch_shapes=[pltpu.SemaphoreType.DMA((nblk,)), pltpu.SemaphoreType.DMA((nblk,))],
    )(blocks)


def _chip_all_to_all(parts, name):
    _, m, n = parts.shape

    def body(x_ref, out_ref, send_sems, recv_sems, local_sem):
        x, y, c = lax.axis_index("x"), lax.axis_index("y"), lax.axis_index("c")
        me = 2 * x + y
        mine = pltpu.make_async_copy(x_ref.at[me], out_ref.at[me], local_sem)
        mine.start()
        copies = []
        for k in range(1, N_CHIP):
            px, py = x ^ ((k >> 1) & 1), y ^ (k & 1)
            copies.append(pltpu.make_async_remote_copy(
                src_ref=x_ref.at[2 * px + py], dst_ref=out_ref.at[me],
                send_sem=send_sems.at[k - 1], recv_sem=recv_sems.at[k - 1],
                device_id=(px, py, c), device_id_type=MESH))
        for cp in copies:
            cp.start()
        for cp in copies:
            cp.wait_recv()
        for cp in copies:
            cp.wait_send()
        mine.wait()

    hbm = pl.BlockSpec(memory_space=pltpu.HBM)
    return pl.pallas_call(
        body, name=name,
        out_shape=jax.ShapeDtypeStruct(parts.shape, parts.dtype),
        in_specs=[hbm], out_specs=hbm,
        scratch_shapes=[pltpu.SemaphoreType.DMA((N_CHIP - 1,)), pltpu.SemaphoreType.DMA((N_CHIP - 1,)),
                        pltpu.SemaphoreType.DMA],
    )(parts)


def _pair_sum(core, pieces, theirs, name):
    nblk, _, m, n = pieces.shape
    tr = _row_tile(m, 536)

    def body(c_ref, p_ref, t_ref, o_ref):
        o_ref[...] = (p_ref[...] + t_ref[...]).astype(BF16)

    return pl.pallas_call(
        body, name=name,
        grid_spec=pltpu.PrefetchScalarGridSpec(
            num_scalar_prefetch=1, grid=(nblk, m // tr),
            in_specs=[pl.BlockSpec((None, None, tr, n), lambda k, i, c_ref: (k, c_ref[0], i, 0)),
                      pl.BlockSpec((None, tr, n), lambda k, i, c_ref: (k, i, 0))],
            out_specs=pl.BlockSpec((None, tr, n), lambda k, i, c_ref: (k, i, 0))),
        out_shape=jax.ShapeDtypeStruct((nblk, m, n), BF16),
        compiler_params=_cp(("parallel", "parallel")),
    )(core, pieces, theirs)


def _sum_slots(slots, name):
    k, m, n = slots.shape
    tr = _row_tile(m, 536)

    def body(s_ref, o_ref):
        acc = s_ref[0].astype(F32)
        for j in range(1, k):
            acc = acc + s_ref[j].astype(F32)
        o_ref[...] = acc

    return pl.pallas_call(
        body, name=name, grid=(m // tr,),
        out_shape=jax.ShapeDtypeStruct((m, n), F32),
        in_specs=[pl.BlockSpec((k, tr, n), lambda i: (0, i, 0))],
        out_specs=pl.BlockSpec((tr, n), lambda i: (i, 0)),
        compiler_params=_cp(("parallel",)),
    )(slots)


def _ada_fwd(c_all, w, b, name):
    r, _ = c_all.shape
    n = w.shape[1]

    def body(c_ref, w_ref, b_ref, o_ref):
        cv = c_ref[...]
        act = cv * _sigmoid(cv)
        o_ref[...] = jnp.dot(act, w_ref[...], precision=HI, preferred_element_type=F32) + b_ref[...]

    return pl.pallas_call(body, name=name, out_shape=jax.ShapeDtypeStruct((r, n), F32),
                          compiler_params=_cp())(c_all, w, b)


def _ada_bwd(c_all, dmod_chip, dmod_all, name):
    r, d = c_all.shape
    nl, _, n = dmod_chip.shape

    def body(c_ref, dm_ref, da_ref, gw_ref, gb_ref):
        cv = c_ref[...]
        act = cv * _sigmoid(cv)
        for l in range(nl):
            gw_ref[l] = lax.dot_general(act, dm_ref[l], (((0,), (0,)), ((), ())), precision=HI,
                                        preferred_element_type=F32)
        gb_ref[...] = jnp.sum(da_ref[...], axis=0, keepdims=True)

    return pl.pallas_call(
        body, name=name,
        out_shape=(jax.ShapeDtypeStruct((nl, d, n), F32), jax.ShapeDtypeStruct((1, dmod_all.shape[1]), F32)),
        compiler_params=_cp())(c_all, dmod_chip, dmod_all)


def _norm_proj(x, g, scale, shift, w, name):
    b, s, d = x.shape
    n = w.shape[1]
    tm = min(s, 256)

    def body(x_ref, g_ref, sc_ref, sh_ref, w_ref, proj_ref, h_ref):
        xv = x_ref[...]
        rstd = lax.rsqrt(jnp.mean(xv * xv, axis=-1, keepdims=True) + EPS)
        h = (xv * rstd) * g_ref[...] * (1.0 + sc_ref[...]) + sh_ref[...]
        hb = h.astype(BF16)
        h_ref[...] = hb
        proj_ref[...] = _dot(hb, w_ref[...])

    return pl.pallas_call(
        body, name=name, grid=(b, s // tm),
        out_shape=(jax.ShapeDtypeStruct((b, s, n), F32), jax.ShapeDtypeStruct((b, s, d), BF16)),
        in_specs=[pl.BlockSpec((None, tm, d), lambda i, j: (i, j, 0)),
                  pl.BlockSpec((1, d), lambda i, j: (0, 0)),
                  pl.BlockSpec((None, 1, d), lambda i, j: (i, 0, 0)),
                  pl.BlockSpec((None, 1, d), lambda i, j: (i, 0, 0)),
                  pl.BlockSpec((d, n), lambda i, j: (0, 0))],
        out_specs=(pl.BlockSpec((None, tm, n), lambda i, j: (i, j, 0)),
                   pl.BlockSpec((None, tm, d), lambda i, j: (i, j, 0))),
        compiler_params=_cp(("parallel", "parallel")),
    )(x, g, scale, shift, w)


def _cat_refs(refs):
    vals = [r[...] for r in refs]
    return vals[0] if len(vals) == 1 else jnp.concatenate(vals, axis=-1)


def _gate_outproj(mix_parts, proj, gate_blk, w_out, x, gmod, name):
    b, s, _ = x.shape
    wd, d = w_out.shape
    tm = min(s, 256)
    npart = len(mix_parts)

    def body(*refs):
        mix_refs = refs[:npart]
        gate_ref, w_ref, x_ref, gm_ref, xo_ref, o_ref = refs[npart:]
        gt = gate_ref[...]
        y = (_cat_refs(mix_refs) * (gt * _sigmoid(gt))).astype(BF16)
        o = _dot(y, w_ref[...])
        o_ref[...] = o.astype(BF16)
        xo_ref[...] = x_ref[...] + gm_ref[...] * o

    return pl.pallas_call(
        body, name=name, grid=(b, s // tm),
        out_shape=(jax.ShapeDtypeStruct((b, s, d), F32), jax.ShapeDtypeStruct((b, s, d), BF16)),
        in_specs=[pl.BlockSpec((None, tm, p.shape[2]), lambda i, j: (i, j, 0)) for p in mix_parts] + [
                  pl.BlockSpec((None, tm, wd), lambda i, j: (i, j, gate_blk)),
                  pl.BlockSpec((wd, d), lambda i, j: (0, 0)),
                  pl.BlockSpec((None, tm, d), lambda i, j: (i, j, 0)),
                  pl.BlockSpec((None, 1, d), lambda i, j: (i, 0, 0))],
        out_specs=(pl.BlockSpec((None, tm, d), lambda i, j: (i, j, 0)),
                   pl.BlockSpec((None, tm, d), lambda i, j: (i, j, 0))),
        compiler_params=_cp(("parallel", "parallel")),
    )(*mix_parts, proj, w_out, x, gmod)


def _final_loss(x, g, target, name):
    b, s, d = x.shape
    tm = min(s, 256)

    def body(x_ref, g_ref, t_ref, loss_ref, dx_ref, dg_ref):
        first = jnp.logical_and(pl.program_id(0) == 0, pl.program_id(1) == 0)

        @pl.when(first)
        def _():
            loss_ref[...] = jnp.zeros_like(loss_ref)
            dg_ref[...] = jnp.zeros_like(dg_ref)

        xv = x_ref[...]
        gv = g_ref[...]
        rstd = lax.rsqrt(jnp.mean(xv * xv, axis=-1, keepdims=True) + EPS)
        xhat = xv * rstd
        err = xhat * gv - t_ref[...]
        row = jnp.mean(err * err, axis=-1, keepdims=True)
        loss_ref[...] += 0.5 * jnp.sum(row, axis=0, keepdims=True)
        dy = err * (1.0 / d)
        dg_ref[...] += jnp.sum(dy * xhat, axis=0, keepdims=True)
        dxh = dy * gv
        dx_ref[...] = rstd * (dxh - xhat * jnp.mean(dxh * xhat, axis=-1, keepdims=True))

    return pl.pallas_call(
        body, name=name, grid=(b, s // tm),
        out_shape=(jax.ShapeDtypeStruct((1, LANE), F32), jax.ShapeDtypeStruct((b, s, d), F32),
                   jax.ShapeDtypeStruct((1, d), F32)),
        in_specs=[pl.BlockSpec((None, tm, d), lambda i, j: (i, j, 0)),
                  pl.BlockSpec((1, d), lambda i, j: (0, 0)),
                  pl.BlockSpec((None, tm, d), lambda i, j: (i, j, 0))],
        out_specs=(pl.BlockSpec((1, LANE), lambda i, j: (0, 0)),
                   pl.BlockSpec((None, tm, d), lambda i, j: (i, j, 0)),
                   pl.BlockSpec((1, d), lambda i, j: (0, 0))),
        compiler_params=_cp(("arbitrary", "arbitrary")),
    )(x, g, target)


def _bwd_out(dxo, gmod, o, mix_parts, proj, gate_blk, w_out_t, name):
    b, s, d = dxo.shape
    wd = w_out_t.shape[1]
    tm = min(s, 256)
    npart = len(mix_parts)

    def body(dx_ref, gm_ref, o_ref, *refs):
        mix_refs = refs[:npart]
        gate_ref, wt_ref, dmix_ref, dgate_ref, do_ref, y_ref, dgm_ref = refs[npart:]

        @pl.when(pl.program_id(1) == 0)
        def _():
            dgm_ref[...] = jnp.zeros_like(dgm_ref)

        dx = dx_ref[...]
        dgm_ref[...] += jnp.sum(dx * o_ref[...].astype(F32), axis=0, keepdims=True)
        dob = (gm_ref[...] * dx).astype(BF16)
        do_ref[...] = dob
        dy = _dot(dob, wt_ref[...])
        gt = gate_ref[...]
        sg = _sigmoid(gt)
        silu = gt * sg
        mx = _cat_refs(mix_refs)
        y_ref[...] = (mx * silu).astype(BF16)
        dmix_ref[...] = dy * silu
        dgate_ref[...] = (dy * mx * (sg * (1.0 + gt * (1.0 - sg)))).astype(BF16)

    row = lambda i, j: (i, j, 0)
    return pl.pallas_call(
        body, name=name, grid=(b, s // tm),
        out_shape=(jax.ShapeDtypeStruct((b, s, wd), F32), jax.ShapeDtypeStruct((b, s, wd), BF16),
                   jax.ShapeDtypeStruct((b, s, d), BF16), jax.ShapeDtypeStruct((b, s, wd), BF16),
                   jax.ShapeDtypeStruct((b, 1, d), F32)),
        in_specs=[pl.BlockSpec((None, tm, d), row),
                  pl.BlockSpec((None, 1, d), lambda i, j: (i, 0, 0)),
                  pl.BlockSpec((None, tm, d), row)] + [
                  pl.BlockSpec((None, tm, p.shape[2]), row) for p in mix_parts] + [
                  pl.BlockSpec((None, tm, wd), lambda i, j: (i, j, gate_blk)),
                  pl.BlockSpec((d, wd), lambda i, j: (0, 0))],
        out_specs=(pl.BlockSpec((None, tm, wd), row), pl.BlockSpec((None, tm, wd), row),
                   pl.BlockSpec((None, tm, d), row), pl.BlockSpec((None, tm, wd), row),
                   pl.BlockSpec((None, 1, d), lambda i, j: (i, 0, 0))),
        compiler_params=_cp(("parallel", "arbitrary")),
    )(dxo, gmod, o, *mix_parts, proj, w_out_t)


def _bwd_in(dproj_parts, w_in_t, x, g, scale, dxo, name):
    b, s, d = x.shape
    n = w_in_t.shape[0]
    tm = min(s, 256)
    npart = len(dproj_parts)

    def body(*refs):
        dp_refs = refs[:npart]
        wt_ref, x_ref, g_ref, sc_ref, dxo_ref, dx_ref, dsh_ref, dsc_ref, dg_ref = refs[npart:]

        @pl.when(jnp.logical_and(pl.program_id(0) == 0, pl.program_id(1) == 0))
        def _():
            dg_ref[...] = jnp.zeros_like(dg_ref)

        @pl.when(pl.program_id(1) == 0)
        def _():
            dsh_ref[...] = jnp.zeros_like(dsh_ref)
            dsc_ref[...] = jnp.zeros_like(dsc_ref)

        dh = _dot(_cat_refs(dp_refs), wt_ref[...])
        xv = x_ref[...]
        gv = g_ref[...]
        one_sc = 1.0 + sc_ref[...]
        rstd = lax.rsqrt(jnp.mean(xv * xv, axis=-1, keepdims=True) + EPS)
        xhat = xv * rstd
        dsh_ref[...] += jnp.sum(dh, axis=0, keepdims=True)
        dsc_ref[...] += jnp.sum(dh * (xhat * gv), axis=0, keepdims=True)
        dhs = dh * one_sc
        dg_ref[...] += jnp.sum(dhs * xhat, axis=0, keepdims=True)
        dxh = dhs * gv
        dx_ref[...] = dxo_ref[...] + rstd * (dxh - xhat * jnp.mean(dxh * xhat, axis=-1, keepdims=True))

    row = lambda i, j: (i, j, 0)
    per_b = lambda i, j: (i, 0, 0)
    return pl.pallas_call(
        body, name=name, grid=(b, s // tm),
        out_shape=(jax.ShapeDtypeStruct((b, s, d), F32), jax.ShapeDtypeStruct((b, 1, d), F32),
                   jax.ShapeDtypeStruct((b, 1, d), F32), jax.ShapeDtypeStruct((1, d), F32)),
        in_specs=[pl.BlockSpec((None, tm, p.shape[2]), row) for p in dproj_parts] + [
                  pl.BlockSpec((n, d), lambda i, j: (0, 0)),
                  pl.BlockSpec((None, tm, d), row),
                  pl.BlockSpec((1, d), lambda i, j: (0, 0)),
                  pl.BlockSpec((None, 1, d), per_b),
                  pl.BlockSpec((None, tm, d), row)],
        out_specs=(pl.BlockSpec((None, tm, d), row), pl.BlockSpec((None, 1, d), per_b),
                   pl.BlockSpec((None, 1, d), per_b), pl.BlockSpec((1, d), lambda i, j: (0, 0))),
        compiler_params=_cp(("arbitrary", "arbitrary")),
    )(*dproj_parts, w_in_t, x, g, scale, dxo)


def _matmul_tn(a, b_parts, name):
    bsz, s, m = a.shape
    n = sum(p.shape[2] for p in b_parts)
    tk = next(c for c in (512, 256, 128) if s % c == 0)
    npart = len(b_parts)

    def body(a_ref, *refs):
        b_refs, o_ref = refs[:npart], refs[npart]

        @pl.when(jnp.logical_and(pl.program_id(0) == 0, pl.program_id(1) == 0))
        def _():
            o_ref[...] = jnp.zeros_like(o_ref)

        o_ref[...] += _dot_tn(a_ref[...], _cat_refs(b_refs))

    row = lambda i, k: (i, k, 0)
    return pl.pallas_call(
        body, name=name, grid=(bsz, s // tk),
        out_shape=jax.ShapeDtypeStruct((m, n), F32),
        in_specs=[pl.BlockSpec((None, tk, m), row)] + [pl.BlockSpec((None, tk, p.shape[2]), row) for p in b_parts],
        out_specs=pl.BlockSpec((m, n), lambda i, k: (0, 0)),
        compiler_params=_cp(("arbitrary", "arbitrary")),
    )(a, *b_parts)


def _rel_buckets():
    qi = np.arange(BLK)[:, None]
    kj = np.arange(2 * BLK)[None, :]
    rel = qi - kj + BLK
    n = np.maximum(rel, 0)
    nf = np.maximum(n, 1).astype(np.float32)
    large = REL_MAX_EXACT + (np.log(nf / REL_MAX_EXACT) / math.log(REL_MAX_DIST / REL_MAX_EXACT)
                             * (REL_BUCKETS - REL_MAX_EXACT)).astype(np.int32)
    large = np.minimum(large, REL_BUCKETS - 1)
    bucket = np.where(n < REL_MAX_EXACT, n, large).astype(np.int32)
    valid = ((rel >= 0) & (rel < BLK)).astype(np.int32)
    return bucket, valid


def _swa_bias(rel_bias_t, bucket, valid, name):
    nh = rel_bias_t.shape[0]

    def body(rb_ref, bk_ref, vl_ref, o_ref):
        h = pl.program_id(0)
        bk = bk_ref[...]
        acc = jnp.zeros(bk.shape, F32)
        for i in range(REL_BUCKETS):
            acc = jnp.where(bk == i, rb_ref[h, i], acc)
        o_ref[...] = jnp.where(vl_ref[...] > 0, acc, NEG)

    return pl.pallas_call(
        body, name=name, grid=(nh,),
        out_shape=jax.ShapeDtypeStruct((nh, BLK, 2 * BLK), F32),
        in_specs=[pl.BlockSpec(memory_space=pltpu.SMEM),
                  pl.BlockSpec((BLK, 2 * BLK), lambda h: (0, 0)),
                  pl.BlockSpec((BLK, 2 * BLK), lambda h: (0, 0))],
        out_specs=pl.BlockSpec((None, BLK, 2 * BLK), lambda h: (h, 0, 0)),
        compiler_params=_cp(("arbitrary",)),
    )(rel_bias_t, bucket, valid)


def _swa_scores(n, q, kw, bias_ref):
    sc = _dot_nt(q, kw) * SCALE + bias_ref[...]
    second = lax.broadcasted_iota(jnp.int32, sc.shape, 1) >= BLK
    return jnp.where(jnp.logical_or(n > 0, second), sc, NEG)


def _pad_front(dst_ref, src_ref):
    dst_ref[0:BLK, :] = jnp.zeros((BLK, dst_ref.shape[1]), dst_ref.dtype)
    dst_ref[BLK:, :] = src_ref[...]


def _swa_fwd(q, k, v, bias, sinks, name):
    b, hkv, nb, rows, hd = q.shape
    s = nb * BLK

    def body(q_ref, k_ref, v_ref, bias_ref, sink_ref, o_ref, l_ref, kpad_ref, vpad_ref):
        _pad_front(kpad_ref, k_ref)
        _pad_front(vpad_ref, v_ref)
        sink = sink_ref[...]

        def step(n, carry):
            w0 = pl.multiple_of(n * BLK, BLK)
            sc = _swa_scores(n, q_ref[n], kpad_ref[pl.ds(w0, 2 * BLK), :], bias_ref)
            m = jnp.maximum(jnp.max(sc, axis=1, keepdims=True), sink)
            e = jnp.exp(sc - m)
            den = jnp.sum(e, axis=1, keepdims=True) + jnp.exp(sink - m)
            o_ref[n] = _dot((e * (1.0 / den)).astype(BF16), vpad_ref[pl.ds(w0, 2 * BLK), :])
            l_ref[n] = m + jnp.log(den)
            return carry

        lax.fori_loop(0, nb, step, 0)

    qspec = pl.BlockSpec((None, None, nb, rows, hd), lambda i, kv: (i, kv, 0, 0, 0))
    kspec = pl.BlockSpec((None, None, s, hd), lambda i, kv: (i, kv, 0, 0))
    return pl.pallas_call(
        body, name=name, grid=(b, hkv),
        out_shape=(jax.ShapeDtypeStruct((b, hkv, nb, rows, hd), F32), jax.ShapeDtypeStruct((b, hkv, nb, rows, 1), F32)),
        in_specs=[qspec, kspec, kspec,
                  pl.BlockSpec((None, rows, 2 * BLK), lambda i, kv: (kv, 0, 0)),
                  pl.BlockSpec((None, rows, 1), lambda i, kv: (kv, 0, 0))],
        out_specs=(qspec, pl.BlockSpec((None, None, nb, rows, 1), lambda i, kv: (i, kv, 0, 0, 0))),
        scratch_shapes=[pltpu.VMEM((s + BLK, hd), BF16), pltpu.VMEM((s + BLK, hd), BF16)],
        compiler_params=_cp(("parallel", "parallel")),
    )(q, k, v, bias, sinks)


def _swa_bwd(q, k, v, bias, sinks, do, lse, name):
    b, hkv, nb, rows, hd = q.shape
    s = nb * BLK

    def body(q_ref, k_ref, v_ref, bias_ref, sink_ref, do_ref, l_ref,
             dq_ref, dk_ref, dv_ref, db_ref, dsk_ref, kpad_ref, vpad_ref, dkpad_ref, dvpad_ref):
        _pad_front(kpad_ref, k_ref)
        _pad_front(vpad_ref, v_ref)
        dkpad_ref[...] = jnp.zeros_like(dkpad_ref)
        dvpad_ref[...] = jnp.zeros_like(dvpad_ref)
        db_ref[...] = jnp.zeros_like(db_ref)
        sink = sink_ref[...]

        def step(n, dsink):
            w0 = pl.multiple_of(n * BLK, BLK)
            win = pl.ds(w0, 2 * BLK)
            qn = q_ref[n]
            kw = kpad_ref[win, :]
            ln = l_ref[n]
            p = jnp.exp(_swa_scores(n, qn, kw, bias_ref) - ln)
            dob = do_ref[n]
            dp = _dot_nt(dob, vpad_ref[win, :])
            delta = jnp.sum(p * dp, axis=1, keepdims=True)
            ds = p * (dp - delta)
            db_ref[...] += ds
            dsb = ds.astype(BF16)
            dq_ref[n] = (_dot(dsb, kw) * SCALE).astype(BF16)
            dkpad_ref[win, :] += _dot_tn(dsb, qn)
            dvpad_ref[win, :] += _dot_tn(p.astype(BF16), dob)
            return dsink - jnp.exp(sink - ln) * delta

        dsink = lax.fori_loop(0, nb, step, jnp.zeros((rows, 1), F32))
        for g in range(A_GROUP):
            tot = jnp.sum(dsink[g * BLK:(g + 1) * BLK, :], axis=0, keepdims=True)
            dsk_ref[g] = jnp.broadcast_to(tot, (1, LANE))
        dk_ref[...] = (dkpad_ref[BLK:, :] * SCALE).astype(BF16)
        dv_ref[...] = dvpad_ref[BLK:, :].astype(BF16)

    qspec = pl.BlockSpec((None, None, nb, rows, hd), lambda i, kv: (i, kv, 0, 0, 0))
    kspec = pl.BlockSpec((None, None, s, hd), lambda i, kv: (i, kv, 0, 0))
    return pl.pallas_call(
        body, name=name, grid=(b, hkv),
        out_shape=(jax.ShapeDtypeStruct((b, hkv, nb, rows, hd), BF16), jax.ShapeDtypeStruct((b, hkv, s, hd), BF16),
                   jax.ShapeDtypeStruct((b, hkv, s, hd), BF16), jax.ShapeDtypeStruct((b, hkv, rows, 2 * BLK), F32),
                   jax.ShapeDtypeStruct((b, hkv, A_GROUP, 1, LANE), F32)),
        in_specs=[qspec, kspec, kspec,
                  pl.BlockSpec((None, rows, 2 * BLK), lambda i, kv: (kv, 0, 0)),
                  pl.BlockSpec((None, rows, 1), lambda i, kv: (kv, 0, 0)),
                  qspec,
                  pl.BlockSpec((None, None, nb, rows, 1), lambda i, kv: (i, kv, 0, 0, 0))],
        out_specs=(qspec, kspec, kspec,
                   pl.BlockSpec((None, None, rows, 2 * BLK), lambda i, kv: (i, kv, 0, 0)),
                   pl.BlockSpec((None, None, A_GROUP, 1, LANE), lambda i, kv: (i, kv, 0, 0, 0))),
        scratch_shapes=[pltpu.VMEM((s + BLK, hd), BF16), pltpu.VMEM((s + BLK, hd), BF16),
                        pltpu.VMEM((s + BLK, hd), F32), pltpu.VMEM((s + BLK, hd), F32)],
        compiler_params=_cp(("parallel", "parallel")),
    )(q, k, v, bias, sinks, do, lse)


def _swa_small_grads(db, dsk, bucket, name):
    b, nh = db.shape[0], db.shape[1]

    def body(db_ref, dsk_ref, bk_ref, gb_ref, gs_ref):
        acc = db_ref[0]
        sk = dsk_ref[0]
        for i in range(1, b):
            acc = acc + db_ref[i]
            sk = sk + dsk_ref[i]
        gs_ref[...] = sk
        bk = bk_ref[...]
        for i in range(REL_BUCKETS):
            part = jnp.sum(jnp.where(bk == i, acc, 0.0), axis=1, keepdims=True)
            tot = jnp.sum(part, axis=0, keepdims=True)
            gb_ref[i:i + 1, :] = jnp.broadcast_to(tot, (1, LANE))

    return pl.pallas_call(
        body, name=name, grid=(nh,),
        out_shape=(jax.ShapeDtypeStruct((nh, REL_BUCKETS, LANE), F32), jax.ShapeDtypeStruct((nh, 1, LANE), F32)),
        in_specs=[pl.BlockSpec((b, None, BLK, 2 * BLK), lambda h: (0, h, 0, 0)),
                  pl.BlockSpec((b, None, 1, LANE), lambda h: (0, h, 0, 0)),
                  pl.BlockSpec((BLK, 2 * BLK), lambda h: (0, 0))],
        out_specs=(pl.BlockSpec((None, REL_BUCKETS, LANE), lambda h: (h, 0, 0)),
                   pl.BlockSpec((None, 1, LANE), lambda h: (h, 0, 0))),
        compiler_params=_cp(("parallel",)),
    )(db, dsk, bucket)


def _log_sigmoid(z):
    return jnp.minimum(z, 0.0) - jnp.log(1.0 + jnp.exp(-jnp.abs(z)))


def _fox_decay(z, bf, name):
    b, s, w = z.shape
    nb = s // BLK

    def body(z_ref, bf_ref, f_ref):
        r = lax.broadcasted_iota(jnp.int32, (BLK, BLK), 0)
        c = lax.broadcasted_iota(jnp.int32, (BLK, BLK), 1)
        tri = (c <= r).astype(F32)

        def step(n, carry):
            r0 = pl.multiple_of(n * BLK, BLK)
            lf = _log_sigmoid(z_ref[pl.ds(r0, BLK), :] + bf_ref[...])
            f_ref[pl.ds(r0, BLK), :] = jnp.dot(tri, lf, precision=HI, preferred_element_type=F32) + carry
            return carry + jnp.sum(lf, axis=0, keepdims=True)

        lax.fori_loop(0, nb, step, jnp.zeros((1, w), F32))

    spec = pl.BlockSpec((None, s, w), lambda i: (i, 0, 0))
    return pl.pallas_call(
        body, name=name, grid=(b,), out_shape=jax.ShapeDtypeStruct((b, s, w), F32),
        in_specs=[spec, pl.BlockSpec((1, w), lambda i: (0, 0))], out_specs=spec,
        compiler_params=_cp(("parallel",)),
    )(z, bf)


def _fox_dgate(df, z, bf, nheads, name):
    b, s, w = z.shape
    nb = s // BLK

    def body(df_ref, z_ref, bf_ref, dz_ref, dbf_ref):
        @pl.when(pl.program_id(0) == 0)
        def _():
            dbf_ref[...] = jnp.zeros_like(dbf_ref)

        r = lax.broadcasted_iota(jnp.int32, (BLK, BLK), 0)
        c = lax.broadcasted_iota(jnp.int32, (BLK, BLK), 1)
        tri = (c >= r).astype(F32)
        lane = lax.broadcasted_iota(jnp.int32, (BLK, w), 1)

        def step(i, carry):
            tail, dbf = carry
            r0 = pl.multiple_of((nb - 1 - i) * BLK, BLK)
            dfb = df_ref[pl.ds(r0, BLK), :]
            dlf = jnp.dot(tri, dfb, precision=HI, preferred_element_type=F32) + tail
            dz = jnp.where(lane < nheads, dlf * _sigmoid(-(z_ref[pl.ds(r0, BLK), :] + bf_ref[...])), 0.0)
            dz_ref[pl.ds(r0, BLK), :] = dz
            return tail + jnp.sum(dfb, axis=0, keepdims=True), dbf + jnp.sum(dz, axis=0, keepdims=True)

        zero = jnp.zeros((1, w), F32)
        _, dbf = lax.fori_loop(0, nb, step, (zero, zero))
        dbf_ref[...] += dbf

    spec = pl.BlockSpec((None, s, w), lambda i: (i, 0, 0))
    one = pl.BlockSpec((1, w), lambda i: (0, 0))
    return pl.pallas_call(
        body, name=name, grid=(b,),
        out_shape=(jax.ShapeDtypeStruct((b, s, w), F32), jax.ShapeDtypeStruct((1, w), F32)),
        in_specs=[spec, spec, one], out_specs=(spec, one),
        compiler_params=_cp(("arbitrary",)),
    )(df, z, bf)


def _fox_segments(nb):
    per = max(1, nb // 4)
    return per, nb // per


def _head_masks(shape, axis):
    idx = lax.broadcasted_iota(jnp.int32, shape, axis)
    return idx < HEAD_DIM, idx >= HEAD_DIM


def _fox_fwd(proj, qblk, kblk, vblk, fcol, frow, name):
    b, s, _ = proj.shape
    nh = fcol.shape[1]
    npair = nh // 2
    per, nseg = _fox_segments(s // BLK)

    def body(q_ref, k_ref, v_ref, fc_ref, fr_ref, o_ref, l_ref, qm_ref, kt_ref, vb_ref):
        lo, hi = _head_masks((s, LANE), 1)
        qv = q_ref[...] * SCALE
        qm_ref[0] = jnp.where(lo, qv, 0.0).astype(BF16)
        qm_ref[1] = jnp.where(hi, qv, 0.0).astype(BF16)
        kt_ref[...] = k_ref[...].T.astype(BF16)
        vb_ref[...] = v_ref[...].astype(BF16)
        lane_lo = lax.broadcasted_iota(jnp.int32, (BLK, LANE), 1) < HEAD_DIM
        tail = per * BLK
        causal = (lax.broadcasted_iota(jnp.int32, (BLK, tail), 1)
                  - lax.broadcasted_iota(jnp.int32, (BLK, tail), 0))
        for seg in range(nseg):
            w = (seg + 1) * tail

            def qstep(n, carry):
                r0 = pl.multiple_of(n * BLK, BLK)
                outs = []
                for hh in range(2):
                    sc = _dot(qm_ref[hh, pl.ds(r0, BLK), :], kt_ref[:, :w])
                    sc = sc + (fc_ref[hh, pl.ds(r0, BLK), :] - fr_ref[hh, :, :w])
                    masked = jnp.where(causal <= (n - seg * per) * BLK, sc[:, w - tail:], NEG)
                    sc = masked if seg == 0 else jnp.concatenate([sc[:, :w - tail], masked], axis=1)
                    m = jnp.max(sc, axis=1, keepdims=True)
                    e = jnp.exp(sc - m)
                    l = jnp.sum(e, axis=1, keepdims=True)
                    outs.append(_dot((e * (1.0 / l)).astype(BF16), vb_ref[:w, :]))
                    l_ref[hh, pl.ds(r0, BLK), :] = m + jnp.log(l)
                o_ref[pl.ds(r0, BLK), :] = jnp.where(lane_lo, outs[0], outs[1])
                return carry

            lax.fori_loop(seg * per, (seg + 1) * per, qstep, 0)

    def tok(blk):
        return pl.BlockSpec((None, s, LANE), lambda i, p: (i, 0, blk + p))

    col = pl.BlockSpec((None, 2, s, 1), lambda i, p: (i, p, 0, 0))
    rowspec = pl.BlockSpec((None, 2, 1, s), lambda i, p: (i, p, 0, 0))
    return pl.pallas_call(
        body, name=name, grid=(b, npair),
        out_shape=(jax.ShapeDtypeStruct((b, s, nh * HEAD_DIM), F32), jax.ShapeDtypeStruct((b, nh, s, 1), F32)),
        in_specs=[tok(qblk), tok(kblk), tok(vblk), col, rowspec],
        out_specs=(pl.BlockSpec((None, s, LANE), lambda i, p: (i, 0, p)), col),
        scratch_shapes=[pltpu.VMEM((2, s, LANE), BF16), pltpu.VMEM((LANE, s), BF16), pltpu.VMEM((s, LANE), BF16)],
        compiler_params=_cp(("parallel", "parallel")),
    )(proj, proj, proj, fcol, frow)


def _fox_bwd(proj, qblk, kblk, vblk, dmix, doblk, fcol, frow, frowb, lse, lserowb, name):
    b, s, _ = proj.shape
    nh = fcol.shape[1]
    npair = nh // 2
    nb = s // BLK
    per, nseg = _fox_segments(nb)

    def body(q_ref, k_ref, v_ref, do_ref, fc_ref, fr_ref, frb_ref, l_ref, lrb_ref,
             dq_ref, dk_ref, dv_ref, dfr_ref,
             qm_ref, dom_ref, kb_ref, vb_ref, kt_ref, vt_ref, qtm_ref, dotm_ref, dka_ref, dva_ref):
        lo, hi = _head_masks((s, LANE), 1)
        qv = q_ref[...] * SCALE
        dov = do_ref[...]
        for hh, msk in enumerate((lo, hi)):
            qm_ref[hh] = jnp.where(msk, qv, 0.0).astype(BF16)
            dom_ref[hh] = jnp.where(msk, dov, 0.0).astype(BF16)
        kv = k_ref[...]
        vv = v_ref[...]
        kb_ref[...] = kv.astype(BF16)
        vb_ref[...] = vv.astype(BF16)
        kt_ref[...] = kv.T.astype(BF16)
        vt_ref[...] = vv.T.astype(BF16)
        rlo, rhi = _head_masks((LANE, BLK), 0)

        def tstep(n, carry):
            r0 = pl.multiple_of(n * BLK, BLK)
            qt = (q_ref[pl.ds(r0, BLK), :] * SCALE).T
            dt = do_ref[pl.ds(r0, BLK), :].T
            for hh, msk in enumerate((rlo, rhi)):
                qtm_ref[hh, n] = jnp.where(msk, qt, 0.0).astype(BF16)
                dotm_ref[hh, n] = jnp.where(msk, dt, 0.0).astype(BF16)
            return carry

        lax.fori_loop(0, nb, tstep, 0)
        dka_ref[...] = jnp.zeros_like(dka_ref)
        dva_ref[...] = jnp.zeros_like(dva_ref)
        dfr_ref[...] = jnp.zeros_like(dfr_ref)
        lane_lo = lax.broadcasted_iota(jnp.int32, (BLK, LANE), 1) < HEAD_DIM
        tail = per * BLK
        causal = (lax.broadcasted_iota(jnp.int32, (BLK, tail), 1)
                  - lax.broadcasted_iota(jnp.int32, (BLK, tail), 0))
        causal_t = (lax.broadcasted_iota(jnp.int32, (tail, BLK), 0)
                    - lax.broadcasted_iota(jnp.int32, (tail, BLK), 1))
        for seg in range(nseg):
            w = (seg + 1) * tail

            def nstep(n, carry):
                r0 = pl.multiple_of(n * BLK, BLK)
                lim = (n - seg * per) * BLK
                dqs = []
                for hh in range(2):
                    qn = qm_ref[hh, pl.ds(r0, BLK), :]
                    don = dom_ref[hh, pl.ds(r0, BLK), :]
                    sc = _dot(qn, kt_ref[:, :w]) + ((fc_ref[hh, pl.ds(r0, BLK), :] - l_ref[hh, pl.ds(r0, BLK), :])
                                                   - fr_ref[hh, :, :w])
                    masked = jnp.where(causal <= lim, sc[:, w - tail:], NEG)
                    p = jnp.exp(masked if seg == 0 else jnp.concatenate([sc[:, :w - tail], masked], axis=1))
                    dp = _dot(don, vt_ref[:, :w])
                    ds = p * (dp - jnp.sum(p * dp, axis=1, keepdims=True))
                    dqs.append(_dot(ds.astype(BF16), kb_ref[:w, :]))
                    dfr_ref[hh, :, :w] -= jnp.sum(ds, axis=0, keepdims=True)
                    sct = _dot(kb_ref[:w, :], qtm_ref[hh, n]) + ((frb_ref[hh, n] - lrb_ref[hh, n]) - fc_ref[hh, :w, :])
                    masked_t = jnp.where(causal_t <= lim, sct[w - tail:, :], NEG)
                    pt = jnp.exp(masked_t if seg == 0 else jnp.concatenate([sct[:w - tail, :], masked_t], axis=0))
                    dpt = _dot(vb_ref[:w, :], dotm_ref[hh, n])
                    dst = pt * (dpt - jnp.sum(pt * dpt, axis=0, keepdims=True))
                    dka_ref[:w, :] += _dot(dst.astype(BF16), qn)
                    dva_ref[:w, :] += _dot(pt.astype(BF16), don)
                dq_ref[pl.ds(r0, BLK), :] = (jnp.where(lane_lo, dqs[0], dqs[1]) * SCALE).astype(BF16)
                return carry

            lax.fori_loop(seg * per, (seg + 1) * per, nstep, 0)
        dk_ref[...] = dka_ref[...].astype(BF16)
        dv_ref[...] = dva_ref[...].astype(BF16)

    def tok(blk):
        return pl.BlockSpec((None, s, LANE), lambda i, p: (i, 0, blk + p))

    col = pl.BlockSpec((None, 2, s, 1), lambda i, p: (i, p, 0, 0))
    rowspec = pl.BlockSpec((None, 2, 1, s), lambda i, p: (i, p, 0, 0))
    rowbspec = pl.BlockSpec((None, 2, nb, 1, BLK), lambda i, p: (i, p, 0, 0, 0))
    outtok = pl.BlockSpec((None, s, LANE), lambda i, p: (i, 0, p))
    shp = jax.ShapeDtypeStruct((b, s, nh * HEAD_DIM), BF16)
    return pl.pallas_call(
        body, name=name, grid=(b, npair),
        out_shape=(shp, shp, shp, jax.ShapeDtypeStruct((b, nh, 1, s), F32)),
        in_specs=[tok(qblk), tok(kblk), tok(vblk),
                  pl.BlockSpec((None, s, LANE), lambda i, p: (i, 0, doblk + p)),
                  col, rowspec, rowbspec, col, rowbspec],
        out_specs=(outtok, outtok, outtok, rowspec),
        scratch_shapes=[pltpu.VMEM((2, s, LANE), BF16), pltpu.VMEM((2, s, LANE), BF16),
                        pltpu.VMEM((s, LANE), BF16), pltpu.VMEM((s, LANE), BF16),
                        pltpu.VMEM((LANE, s), BF16), pltpu.VMEM((LANE, s), BF16),
                        pltpu.VMEM((2, nb, LANE, BLK), BF16), pltpu.VMEM((2, nb, LANE, BLK), BF16),
                        pltpu.VMEM((s, LANE), F32), pltpu.VMEM((s, LANE), F32)],
        compiler_params=_cp(("parallel", "parallel")),
    )(proj, proj, proj, dmix, fcol, frow, frowb, lse, lserowb)


def _expm1(x):
    poly = x * (1.0 + x * (1.0 / 2.0) * (1.0 + x * (1.0 / 3.0) * (1.0 + x * (1.0 / 4.0) * (1.0 + x * (1.0 / 5.0)
                                                                                          * (1.0 + x * (1.0 / 6.0))))))
    return jnp.where(x > -0.1, poly, jnp.exp(x) - 1.0)


def _softplus(z):
    return jnp.maximum(z, 0.0) + jnp.log(1.0 + jnp.exp(-jnp.abs(z)))


def _scan_up(a, u, carry, row):
    tc = a.shape[0]
    d = 1
    while d < tc:
        keep = row >= d
        a_sh = jnp.where(keep, pltpu.roll(a, d, 0), 1.0)
        u_sh = jnp.where(keep, pltpu.roll(u, d, 0), 0.0)
        u = a * u_sh + u
        a = a * a_sh
        d *= 2
    return u + a * carry


def _scan_down(bnext, g, carry, row):
    tc = g.shape[0]
    a, u = bnext, g
    d = 1
    while d < tc:
        keep = row < tc - d
        a_sh = jnp.where(keep, pltpu.roll(a, tc - d, 0), 1.0)
        u_sh = jnp.where(keep, pltpu.roll(u, tc - d, 0), 0.0)
        u = a * u_sh + u
        a = a * a_sh
        d *= 2
    return u + a * carry


def _pick_row(val, row, which):
    return jnp.sum(jnp.where(row == which, val, 0.0), axis=0, keepdims=True)


def _lru_gates(xpad_ref, t0, tc, cw_ref, cb_ref, wa, ba_ref, wx, bx_ref, sp):
    xw = xpad_ref[pl.ds(t0, tc + SUBLANE), :]
    xc = cb_ref[...]
    for j in range(CONV_WIDTH):
        sh = CONV_WIDTH - 1 - j
        xs = xw if sh == 0 else pltpu.roll(xw, sh, 0)
        xc = xc + xs[SUBLANE:, :] * cw_ref[j:j + 1, :]
    xcb = xc.astype(BF16)
    r = _sigmoid(_dot(xcb, wa) + ba_ref[...])
    i = _sigmoid(_dot(xcb, wx) + bx_ref[...])
    la = -LRU_C * r * sp
    return xc, r, i, la


def _lru_specs(s, cb):
    seq = lambda bi, ni: (bi, 0, ni)
    return dict(
        seq=pl.BlockSpec((None, s, cb), seq),
        cw=pl.BlockSpec((CONV_WIDTH, cb), lambda bi, ni: (0, ni)),
        vec=pl.BlockSpec((1, cb), lambda bi, ni: (0, ni)),
        wblk=pl.BlockSpec((None, cb, cb), lambda bi, ni: (ni, 0, 0)),
    )


def _lru_fwd(proj, cw, cb_, wa, ba, wx, bx, lam, name):
    b, s, _ = proj.shape
    nblk, cb, _ = wa.shape
    tc = min(s, SCAN_CHUNK)
    nc = s // tc

    def body(x_ref, cw_ref, cb_ref, wa_ref, ba_ref, wx_ref, bx_ref, lam_ref, hs_ref, xpad_ref):
        xpad_ref[0:SUBLANE, :] = jnp.zeros((SUBLANE, cb), F32)
        xpad_ref[SUBLANE:, :] = x_ref[...]
        wa_b = wa_ref[...].astype(BF16)
        wx_b = wx_ref[...].astype(BF16)
        sp = _softplus(-lam_ref[...])
        row = lax.broadcasted_iota(jnp.int32, (tc, cb), 0)

        def chunk(ci, carry):
            t0 = pl.multiple_of(ci * tc, tc)
            xc, r, i, la = _lru_gates(xpad_ref, t0, tc, cw_ref, cb_ref, wa_b, ba_ref, wx_b, bx_ref, sp)
            a = jnp.exp(la)
            u = jnp.sqrt(-_expm1(2.0 * la)) * (i * xc)
            h = _scan_up(a, u, carry, row)
            hs_ref[pl.ds(t0, tc), :] = h
            return _pick_row(h, row, tc - 1)

        lax.fori_loop(0, nc, chunk, jnp.zeros((1, cb), F32))

    sp_ = _lru_specs(s, cb)
    return pl.pallas_call(
        body, name=name, grid=(b, nblk),
        out_shape=jax.ShapeDtypeStruct((b, s, nblk * cb), F32),
        in_specs=[sp_["seq"], sp_["cw"], sp_["vec"], sp_["wblk"], sp_["vec"], sp_["wblk"], sp_["vec"], sp_["vec"]],
        out_specs=sp_["seq"],
        scratch_shapes=[pltpu.VMEM((s + SUBLANE, cb), F32)],
        compiler_params=_cp(("parallel", "parallel")),
    )(proj, cw, cb_, wa, ba, wx, bx, lam)


def _lru_bwd(proj, hs, dhs, cw, cb_, wa, ba, wx, bx, lam, name):
    b, s, _ = proj.shape
    nblk, cb, _ = wa.shape
    tc = min(s, SCAN_CHUNK)
    nc = s // tc

    def body(x_ref, hs_ref, dhs_ref, cw_ref, cb_ref, wa_ref, ba_ref, wx_ref, bx_ref, lam_ref,
             dx_ref, dcw_ref, dcb_ref, dwa_ref, dba_ref, dwx_ref, dbx_ref, dlam_ref,
             xpad_ref, hpad_ref, dcpad_ref, xc_ref, r_ref, i_ref, a_ref):
        @pl.when(pl.program_id(1) == 0)
        def _():
            for ref in (dcw_ref, dcb_ref, dwa_ref, dba_ref, dwx_ref, dbx_ref, dlam_ref):
                ref[...] = jnp.zeros_like(ref)

        zeros8 = jnp.zeros((SUBLANE, cb), F32)
        xpad_ref[0:SUBLANE, :] = zeros8
        xpad_ref[SUBLANE:, :] = x_ref[...]
        hpad_ref[0:SUBLANE, :] = zeros8
        hpad_ref[SUBLANE:, :] = hs_ref[...]
        dcpad_ref[s:s + SUBLANE, :] = zeros8
        wa_b = wa_ref[...].astype(BF16)
        wx_b = wx_ref[...].astype(BF16)
        lam_v = lam_ref[...]
        sp = _softplus(-lam_v)
        dsp_dlam = -_sigmoid(-lam_v)
        row = lax.broadcasted_iota(jnp.int32, (tc, cb), 0)

        def recompute(ci, carry):
            t0 = pl.multiple_of(ci * tc, tc)
            xc, r, i, la = _lru_gates(xpad_ref, t0, tc, cw_ref, cb_ref, wa_b, ba_ref, wx_b, bx_ref, sp)
            xc_ref[pl.ds(t0, tc), :] = xc
            r_ref[pl.ds(t0, tc), :] = r
            i_ref[pl.ds(t0, tc), :] = i
            a_ref[pl.ds(t0, tc), :] = jnp.exp(la)
            return carry

        lax.fori_loop(0, nc, recompute, 0)

        def adjoint(k, carry):
            g_next, a_first_next = carry
            t0 = pl.multiple_of((nc - 1 - k) * tc, tc)
            a = a_ref[pl.ds(t0, tc), :]
            a_next = jnp.where(row == tc - 1, a_first_next, pltpu.roll(a, tc - 1, 0))
            gg = _scan_down(a_next, dhs_ref[pl.ds(t0, tc), :], g_next, row)
            h_prev = pltpu.roll(hpad_ref[pl.ds(t0, tc + SUBLANE), :], 1, 0)[SUBLANE:, :]
            xc = xc_ref[pl.ds(t0, tc), :]
            r = r_ref[pl.ds(t0, tc), :]
            i = i_ref[pl.ds(t0, tc), :]
            mult = jnp.sqrt(-_expm1(-2.0 * LRU_C * r * sp))
            d_mult = gg * i * xc
            d_i = gg * mult * xc
            d_xc = gg * mult * i
            d_la = gg * h_prev * a - d_mult * (a * a) / mult
            d_zr = (d_la * (-LRU_C * sp)) * r * (1.0 - r)
            d_zi = d_i * i * (1.0 - i)
            dlam_ref[...] += jnp.sum(d_la * (-LRU_C * r), axis=0, keepdims=True) * dsp_dlam
            dzr_b = d_zr.astype(BF16)
            dzi_b = d_zi.astype(BF16)
            xcb = xc.astype(BF16)
            d_xc = d_xc + _dot_nt(dzr_b, wa_b) + _dot_nt(dzi_b, wx_b)
            dwa_ref[...] += _dot_tn(xcb, dzr_b)
            dwx_ref[...] += _dot_tn(xcb, dzi_b)
            dba_ref[...] += jnp.sum(d_zr, axis=0, keepdims=True)
            dbx_ref[...] += jnp.sum(d_zi, axis=0, keepdims=True)
            dcb_ref[...] += jnp.sum(d_xc, axis=0, keepdims=True)
            dcpad_ref[pl.ds(t0, tc), :] = d_xc
            return _pick_row(gg, row, 0), _pick_row(a, row, 0)

        zero = jnp.zeros((1, cb), F32)
        lax.fori_loop(0, nc, adjoint, (zero, zero))

        def conv_back(ci, carry):
            t0 = pl.multiple_of(ci * tc, tc)
            dw = dcpad_ref[pl.ds(t0, tc + SUBLANE), :]
            xw = xpad_ref[pl.ds(t0, tc + SUBLANE), :]
            d_xc = dw[:tc, :]
            dxr = jnp.zeros((tc, cb), F32)
            for j in range(CONV_WIDTH):
                sh = CONV_WIDTH - 1 - j
                dsh = dw if sh == 0 else pltpu.roll(dw, tc + SUBLANE - sh, 0)
                dxr = dxr + dsh[:tc, :] * cw_ref[j:j + 1, :]
                xs = xw if sh == 0 else pltpu.roll(xw, sh, 0)
                dcw_ref[j:j + 1, :] += jnp.sum(d_xc * xs[SUBLANE:, :], axis=0, keepdims=True)
            dx_ref[pl.ds(t0, tc), :] = dxr.astype(BF16)
            return carry

        lax.fori_loop(0, nc, conv_back, 0)

    seq = lambda ni, bi: (bi, 0, ni)
    seqspec = pl.BlockSpec((None, s, cb), seq)
    cwspec = pl.BlockSpec((CONV_WIDTH, cb), lambda ni, bi: (0, ni))
    vec = pl.BlockSpec((1, cb), lambda ni, bi: (0, ni))
    wblk = pl.BlockSpec((None, cb, cb), lambda ni, bi: (ni, 0, 0))
    w = nblk * cb
    return pl.pallas_call(
        body, name=name, grid=(nblk, b),
        out_shape=(jax.ShapeDtypeStruct((b, s, w), BF16), jax.ShapeDtypeStruct((CONV_WIDTH, w), F32),
                   jax.ShapeDtypeStruct((1, w), F32), jax.ShapeDtypeStruct((nblk, cb, cb), F32),
                   jax.ShapeDtypeStruct((1, w), F32), jax.ShapeDtypeStruct((nblk, cb, cb), F32),
                   jax.ShapeDtypeStruct((1, w), F32), jax.ShapeDtypeStruct((1, w), F32)),
        in_specs=[seqspec, seqspec, seqspec, cwspec, vec, wblk, vec, wblk, vec, vec],
        out_specs=(seqspec, cwspec, vec, wblk, vec, wblk, vec, vec),
        scratch_shapes=[pltpu.VMEM((s + SUBLANE, cb), F32)] * 3 + [pltpu.VMEM((s, cb), F32)] * 4,
        compiler_params=_cp(("parallel", "arbitrary")),
    )(proj, hs, dhs, cw, cb_, wa, ba, wx, bx, lam)


def _adamw(w, g, m, v, name):
    shape = w.shape
    total = int(np.prod(shape))
    if w.ndim >= 2 and shape[-2] % SUBLANE == 0:
        cols = shape[-1]
    else:
        cols = 1024
    rows = -(-total // cols)
    rows = -(-rows // SUBLANE) * SUBLANE
    tr = _row_tile(rows, 512)
    pad = rows * cols - total

    def flat(a):
        if pad:
            a = jnp.pad(a.reshape(-1), (0, pad))
        return a.reshape(rows, cols)

    c1 = 1.0 - ADAM_B1 ** ADAM_STEP
    c2 = 1.0 - ADAM_B2 ** ADAM_STEP

    def body(w_ref, g_ref, m_ref, v_ref, d_ref, nm_ref, nv_ref):
        gv = g_ref[...]
        nm = ADAM_B1 * m_ref[...] + (1.0 - ADAM_B1) * gv
        nv = ADAM_B2 * v_ref[...] + (1.0 - ADAM_B2) * (gv * gv)
        nm_ref[...] = nm
        nv_ref[...] = nv
        d_ref[...] = -ADAM_LR * ((nm / c1) / (jnp.sqrt(nv / c2) + ADAM_EPS) + ADAM_WD * w_ref[...])

    spec = pl.BlockSpec((tr, cols), lambda i: (i, 0))
    shp = jax.ShapeDtypeStruct((rows, cols), F32)
    outs = pl.pallas_call(
        body, name=name, grid=(rows // tr,), out_shape=(shp, shp, shp),
        in_specs=[spec] * 4, out_specs=(spec,) * 3,
        compiler_params=_cp(("parallel",)),
    )(flat(w), flat(g), flat(m), flat(v))
    if pad:
        return tuple(o.reshape(-1)[:total].reshape(shape) for o in outs)
    return tuple(o.reshape(shape) for o in outs)


def _to_heads(t, nh):
    b, s, _ = t.shape
    return t.reshape(b, s, nh, HEAD_DIM).transpose(0, 2, 1, 3)


def _stack_heads(t):
    b, s, _ = t.shape
    t = t.reshape(b, s // BLK, BLK, A_KV_HEADS, A_GROUP, HEAD_DIM).transpose(0, 3, 1, 4, 2, 5)
    return t.reshape(b, A_KV_HEADS, s // BLK, A_GROUP * BLK, HEAD_DIM)


def _unstack_heads(t):
    b, hkv, nb, rows, hd = t.shape
    t = t.reshape(b, hkv, nb, A_GROUP, BLK, hd).transpose(0, 2, 4, 1, 3, 5)
    return t.reshape(b, nb * BLK, hkv * A_GROUP * hd)


def _from_heads(t):
    b, nh, s, hd = t.shape
    return t.transpose(0, 2, 1, 3).reshape(b, s, nh * hd)


def _pad_rows(a, mult):
    r = a.shape[0]
    p = (-r) % mult
    return jnp.pad(a, ((0, p), (0, 0))) if p else a


def kernel(x, c, rel_bias, norm_g, ada_w, ada_b, attn_w_in, attn_sinks, attn_b_f, attn_w_out, lru_w_in, lru_conv_w, lru_conv_b, lru_w_a, lru_b_a, lru_w_x, lru_b_x, lru_lambda, lru_w_out, final_g, loss_target, m_rel_bias, m_norm_g, m_ada_w, m_ada_b, m_attn_w_in, m_attn_sinks, m_attn_b_f, m_attn_w_out, m_lru_w_in, m_lru_conv_w, m_lru_conv_b, m_lru_w_a, m_lru_b_a, m_lru_w_x, m_lru_b_x, m_lru_lambda, m_lru_w_out, m_final_g, v_rel_bias, v_norm_g, v_ada_w, v_ada_b, v_attn_w_in, v_attn_sinks, v_attn_b_f, v_attn_w_out, v_lru_w_in, v_lru_conv_w, v_lru_conv_b, v_lru_w_a, v_lru_b_a, v_lru_w_x, v_lru_b_x, v_lru_lambda, v_lru_w_out, v_final_g):
    bl, s, d = x.shape
    ix, iy, ic = lax.axis_index("x"), lax.axis_index("y"), lax.axis_index("c")
    chip = 2 * ix + iy
    me = 2 * chip + ic
    nb = s // BLK
    aw = A_Q_HEADS * HEAD_DIM
    akv = A_KV_HEADS * HEAD_DIM
    bw = B_HEADS * HEAD_DIM
    mixw = aw + bw
    qkv_w = aw + 2 * akv + 3 * bw
    n_in = attn_w_in.shape[2] * N_CHIP
    lw = lru_lambda.shape[1] * N_CHIP
    n0 = mixw + qkv_w + LANE

    rows_pad = -(-bl // SUBLANE) * SUBLANE
    c_all = _all_gather8(_pad_rows(c, SUBLANE), "gather_c", pltpu.VMEM)
    c_all = c_all.reshape(N_DEV, rows_pad, d)[:, :bl].reshape(N_DEV * bl, d)
    ncol = ada_w.shape[2]
    ada_w_l = lax.dynamic_index_in_dim(ada_w, ic, 0, keepdims=False)
    ada_b_l = lax.dynamic_slice(ada_b, (ic, chip * ncol), (1, ncol))
    mod_part = _ada_fwd(c_all, ada_w_l, ada_b_l, "ada_fwd")
    mod_all = _all_gather8(_pad_rows(mod_part, SUBLANE), "gather_mod", pltpu.VMEM)
    mrows = -(-(N_DEV * bl) // SUBLANE) * SUBLANE
    mod_all = mod_all.reshape(N_CHIP, 2, mrows, ncol)[:, :, :N_DEV * bl]
    mod_all = mod_all.transpose(1, 2, 0, 3).reshape(2, N_DEV * bl, N_CHIP * ncol)
    mod = lax.dynamic_slice_in_dim(mod_all, me * bl, bl, axis=1)
    shift = [mod[l, :, 0:d].reshape(bl, 1, d) for l in range(2)]
    scale = [mod[l, :, d:2 * d].reshape(bl, 1, d) for l in range(2)]
    gmod = [mod[l, :, 2 * d:3 * d].reshape(bl, 1, d) for l in range(2)]

    c_in0 = n_in // N_CHIP
    c_in1 = 2 * lw // N_CHIP
    assert c_in0 <= d and 2 * c_in1 == d
    r_in0, r_out0, r_in1, r_out1 = d // 2, mixw // N_CHIP // 2, d // 4, lw // N_CHIP // 2
    o_out0, o_in1, o_out1 = r_in0, r_in0 + r_out0, r_in0 + r_out0 + r_in1
    big_rows = o_out1 + r_out1

    def half_of(a, rows):
        return lax.dynamic_slice_in_dim(a, ic * rows, rows, axis=0)

    h_in1 = half_of(lru_w_in[0], r_in0).astype(BF16)
    my_half = jnp.concatenate([
        jnp.pad(half_of(attn_w_in[0], r_in0).astype(BF16), ((0, 0), (0, d - c_in0))),
        half_of(attn_w_out[0], r_out0).astype(BF16),
        jnp.concatenate([h_in1[:r_in1], h_in1[r_in1:]], axis=1),
        half_of(lru_w_out[0], r_out1).astype(BF16)], axis=0)
    gat = _all_gather8(my_half, "gather_weights", pltpu.HBM).reshape(N_CHIP, 2, big_rows, d)
    w_in0 = gat[:, :, :r_in0, :c_in0].transpose(1, 2, 0, 3).reshape(d, n_in)
    w_out0 = gat[:, :, o_out0:o_in1].reshape(mixw, d)
    w_in1 = gat[:, :, o_in1:o_out1].reshape(N_CHIP, 2, r_in1, 2, c_in1)
    w_in1 = w_in1.transpose(1, 3, 2, 0, 4).reshape(d, 2 * lw)
    w_out1 = gat[:, :, o_out1:].reshape(lw, d)
    w_cat0 = jnp.concatenate([w_in0[:, qkv_w + B_HEADS:], w_in0[:, :qkv_w + B_HEADS],
                              jnp.zeros((d, n0 - n_in), BF16)], axis=1)

    proj0, h0 = _norm_proj(x, norm_g[0:1], scale[0], shift[0], w_cat0, "norm_proj0")
    o_a = mixw
    aq = _stack_heads(proj0[:, :, o_a:o_a + aw].astype(BF16))
    ak = _to_heads(proj0[:, :, o_a + aw:o_a + aw + akv].astype(BF16), A_KV_HEADS)
    av = _to_heads(proj0[:, :, o_a + aw + akv:o_a + aw + 2 * akv].astype(BF16), A_KV_HEADS)
    o_b = o_a + aw + 2 * akv
    fox_blks = (o_b // LANE, (o_b + bw) // LANE, (o_b + 2 * bw) // LANE)
    zf = proj0[:, :, o_b + 3 * bw:]
    bucket_np, valid_np = _rel_buckets()
    bucket = jnp.asarray(bucket_np)
    bias = _swa_bias(rel_bias.T, bucket, jnp.asarray(valid_np), "swa_bias")
    bias = bias.reshape(A_KV_HEADS, A_GROUP * BLK, 2 * BLK)
    sinks = jnp.repeat(attn_sinks[0].reshape(A_KV_HEADS, A_GROUP), BLK, axis=1).reshape(A_KV_HEADS, A_GROUP * BLK, 1)
    a_out, a_lse = _swa_fwd(aq, ak, av, bias, sinks, "swa_fwd")
    bf_pad = jnp.pad(attn_b_f, ((0, 0), (0, LANE - B_HEADS)))
    fsum = _fox_decay(zf, bf_pad, "fox_decay")
    fh = fsum[:, :, :B_HEADS].transpose(0, 2, 1)
    fcol = fh.reshape(bl, B_HEADS, s, 1)
    frow = fh.reshape(bl, B_HEADS, 1, s)
    frowb = fh.reshape(bl, B_HEADS, nb, 1, BLK)
    b_out, b_lse = _fox_fwd(proj0, *fox_blks, fcol, frow, "fox_fwd")
    mix0 = [_unstack_heads(a_out), b_out]
    x1, o0 = _gate_outproj(mix0, proj0, 0, w_out0, x, gmod[0], "gate_outproj0")

    proj1, h1 = _norm_proj(x1, norm_g[1:2], scale[1], shift[1], w_in1, "norm_proj1")
    vec_rows = jnp.concatenate([lru_conv_w[0], lru_conv_b, lru_b_a, lru_b_x, lru_lambda], axis=0)
    vec_all = _all_gather8(vec_rows, "gather_lru_vectors", pltpu.VMEM)
    vec_all = vec_all.reshape(N_CHIP, 2, SUBLANE, lw // N_CHIP)[:, 0]
    vec_all = vec_all.transpose(1, 0, 2).reshape(SUBLANE, lw)
    cw_f, cb_f, ba_f, bx_f, lam_f = vec_all[0:4], vec_all[4:5], vec_all[5:6], vec_all[6:7], vec_all[7:8]
    hs = _lru_fwd(proj1, cw_f, cb_f, lru_w_a[0], ba_f, lru_w_x[0], bx_f, lam_f, "lru_fwd")
    x2, o1 = _gate_outproj([hs], proj1, 1, w_out1, x1, gmod[1], "gate_outproj1")

    loss_vec, dx2, g_final = _final_loss(x2, final_g.reshape(1, d), loss_target, "final_loss")
    loss = lax.psum(loss_vec[0, 0], ("x", "y", "c"))

    dhs, dgate1, do1, y1, dgm1 = _bwd_out(dx2, gmod[1], o1, [hs], proj1, 1, w_out1.T, "bwd_out1")
    g_w_out1 = _matmul_tn(y1, [do1], "grad_w_out1")
    (dxr, g_cw, g_cb, g_wa, g_ba, g_wx, g_bx, g_lam) = _lru_bwd(
        proj1, hs, dhs, cw_f, cb_f, lru_w_a[0], ba_f, lru_w_x[0], bx_f, lam_f, "lru_bwd")
    dproj1 = [dxr, dgate1]
    g_w_in1 = _matmul_tn(h1, dproj1, "grad_w_in1")
    dx1, dsh1, dsc1, g_ng1 = _bwd_in(dproj1, w_in1.T, x1, norm_g[1:2], scale[1], dx2, "bwd_in1")

    dmix0, dgate0, do0, y0, dgm0 = _bwd_out(dx1, gmod[0], o0, mix0, proj0, 0, w_out0.T, "bwd_out0")
    g_w_out0 = _matmul_tn(y0, [do0], "grad_w_out0")
    da_out = _stack_heads(dmix0[:, :, :aw].astype(BF16))
    daq, dak, dav, dbias, dsink = _swa_bwd(aq, ak, av, bias, sinks, da_out, a_lse, "swa_bwd")
    dbq, dbk, dbv, dfrow = _fox_bwd(proj0, *fox_blks, dmix0, aw // LANE, fcol, frow, frowb, b_lse,
                                    b_lse.reshape(bl, B_HEADS, nb, 1, BLK), "fox_bwd")
    df = dfrow.reshape(bl, B_HEADS, s).transpose(0, 2, 1)
    df = jnp.pad(df, ((0, 0), (0, 0), (0, LANE - B_HEADS)))
    dzf, g_bf = _fox_dgate(df, zf, bf_pad, B_HEADS, "fox_dgate")
    dproj0 = ([dgate0, _unstack_heads(daq), _from_heads(dak), _from_heads(dav)]
              + [dbq, dbk, dbv, dzf.astype(BF16)])
    g_w_cat0 = _matmul_tn(h0, dproj0, "grad_w_in0")
    g_w_in0 = jnp.concatenate([g_w_cat0[:, mixw:mixw + qkv_w + B_HEADS], g_w_cat0[:, :mixw]], axis=1)
    dx0, dsh0, dsc0, g_ng0 = _bwd_in(dproj0, w_cat0.T, x, norm_g[0:1], scale[0], dx1, "bwd_in0")
    g_relb, g_sink = _swa_small_grads(dbias.reshape(bl, A_Q_HEADS, BLK, 2 * BLK),
                                      dsink.reshape(bl, A_Q_HEADS, 1, LANE), bucket, "swa_small_grads")

    dmod = jnp.concatenate([jnp.concatenate([dsh0, dsc0, dgm0], axis=-1),
                            jnp.concatenate([dsh1, dsc1, dgm1], axis=-1)], axis=1)
    dmod_all = _all_gather8(_pad_rows(dmod.reshape(bl, 6 * d), SUBLANE), "gather_dmod", pltpu.VMEM)
    dmod_all = dmod_all.reshape(N_DEV, rows_pad, 6 * d)[:, :bl].reshape(N_DEV * bl, 6 * d)
    dmod_chip = lax.dynamic_slice_in_dim(dmod_all.reshape(N_DEV * bl, 2, 3 * d), chip * ncol, ncol, axis=2)
    g_ada_w, g_ada_b = _ada_bwd(c_all, dmod_chip.transpose(1, 0, 2), dmod_all, "ada_bwd")
    g_ada_b = g_ada_b.reshape(2, 3 * d)

    small_parts = [g_relb[:, :, 0].T, jnp.concatenate([g_ng0, g_ng1], axis=0), g_sink[:, 0, 0], g_bf[0, :B_HEADS],
                   g_wa, g_wx, g_final, g_cw, g_cb, g_ba, g_bx, g_lam]
    small_sizes = [int(np.prod(p.shape)) for p in small_parts]
    small_total = sum(small_sizes)
    piece_rows = -(-(-(-small_total // N_DEV) // 1024) // SUBLANE) * SUBLANE
    small_flat = jnp.concatenate([p.reshape(-1) for p in small_parts])
    small_flat = jnp.pad(small_flat, (0, N_DEV * piece_rows * 1024 - small_total))
    small_pieces = small_flat.reshape(N_CHIP, 2, piece_rows, 1024)
    p_in0 = jnp.pad(g_w_in0.reshape(2, r_in0, N_CHIP, c_in0).transpose(2, 0, 1, 3),
                    ((0, 0), (0, 0), (0, 0), (0, d - c_in0)))
    p_in1 = g_w_in1.reshape(2, 2, r_in1, N_CHIP, c_in1).transpose(3, 0, 2, 1, 4).reshape(N_CHIP, 2, r_in1, d)
    pieces = jnp.concatenate([p_in0, g_w_out0.reshape(N_CHIP, 2, r_out0, d), p_in1,
                              g_w_out1.reshape(N_CHIP, 2, r_out1, d), small_pieces], axis=2)
    theirs = _sibling_push(pieces, True, "push_sibling_halves")
    partial = _pair_sum(jnp.reshape(ic, (1,)).astype(jnp.int32), pieces, theirs, "sum_chip")
    slots = _chip_all_to_all(partial, "exchange_grads")
    reduced = _sum_slots(slots, "sum_grads")
    mine_big = reduced[:big_rows]
    other_big = _sibling_push(mine_big[None], False, "swap_halves")[0]
    both = jnp.stack([jnp.where(ic == 0, mine_big, other_big), jnp.where(ic == 0, other_big, mine_big)])
    g_big = [both[:, :r_in0, :c_in0].reshape(d, c_in0),
             both[:, o_out0:o_in1].reshape(2 * r_out0, d),
             both[:, o_in1:o_out1].reshape(2, r_in1, 2, c_in1).transpose(0, 2, 1, 3).reshape(d, c_in1),
             both[:, o_out1:].reshape(2 * r_out1, d)]
    small_all = _all_gather8(reduced[big_rows:], "gather_small_grads", pltpu.VMEM).reshape(-1)
    g_small, off = [], 0
    for p, n in zip(small_parts, small_sizes):
        g_small.append(small_all[off:off + n].reshape(p.shape))
        off += n
    (g_rel_bias, g_norm_g, g_sinks, g_b_f, g_w_a, g_w_x, g_fin, g_cw_r, g_cb_r, g_ba_r, g_bx_r, g_lam_r) = g_small
    cw4 = lw // N_CHIP

    def my_cols(a):
        return lax.dynamic_slice_in_dim(a, chip * cw4, cw4, axis=1)

    grads = {
        "rel_bias": g_rel_bias, "norm_g": g_norm_g, "ada_w": g_ada_w, "ada_b": g_ada_b,
        "attn_w_in": g_big[0][None], "attn_sinks": g_sinks[None], "attn_b_f": g_b_f[None],
        "attn_w_out": g_big[1][None], "lru_w_in": g_big[2][None], "lru_conv_w": my_cols(g_cw_r)[None],
        "lru_conv_b": my_cols(g_cb_r), "lru_w_a": g_w_a[None], "lru_b_a": my_cols(g_ba_r),
        "lru_w_x": g_w_x[None], "lru_b_x": my_cols(g_bx_r), "lru_lambda": my_cols(g_lam_r),
        "lru_w_out": g_big[3][None], "final_g": g_fin.reshape(d),
    }
    weights = dict(rel_bias=rel_bias, norm_g=norm_g, ada_w=ada_w, ada_b=ada_b, attn_w_in=attn_w_in,
                   attn_sinks=attn_sinks, attn_b_f=attn_b_f, attn_w_out=attn_w_out, lru_w_in=lru_w_in,
                   lru_conv_w=lru_conv_w, lru_conv_b=lru_conv_b, lru_w_a=lru_w_a, lru_b_a=lru_b_a,
                   lru_w_x=lru_w_x, lru_b_x=lru_b_x, lru_lambda=lru_lambda, lru_w_out=lru_w_out, final_g=final_g)
    moms = dict(rel_bias=(m_rel_bias, v_rel_bias), norm_g=(m_norm_g, v_norm_g), ada_w=(m_ada_w, v_ada_w),
                ada_b=(m_ada_b, v_ada_b), attn_w_in=(m_attn_w_in, v_attn_w_in),
                attn_sinks=(m_attn_sinks, v_attn_sinks), attn_b_f=(m_attn_b_f, v_attn_b_f),
                attn_w_out=(m_attn_w_out, v_attn_w_out), lru_w_in=(m_lru_w_in, v_lru_w_in),
                lru_conv_w=(m_lru_conv_w, v_lru_conv_w), lru_conv_b=(m_lru_conv_b, v_lru_conv_b),
                lru_w_a=(m_lru_w_a, v_lru_w_a), lru_b_a=(m_lru_b_a, v_lru_b_a), lru_w_x=(m_lru_w_x, v_lru_w_x),
                lru_b_x=(m_lru_b_x, v_lru_b_x), lru_lambda=(m_lru_lambda, v_lru_lambda),
                lru_w_out=(m_lru_w_out, v_lru_w_out), final_g=(m_final_g, v_final_g))
    names = list(weights)
    big_names = [n for n in names if weights[n].size >= 65536]
    small_names = [n for n in names if weights[n].size < 65536]
    delta, new_m, new_v = {}, {}, {}
    for n in big_names:
        delta[n], new_m[n], new_v[n] = _adamw(weights[n], grads[n].reshape(weights[n].shape),
                                              moms[n][0], moms[n][1], "adamw_" + n)
    cat = lambda arrs: jnp.concatenate([a.reshape(-1) for a in arrs])
    sd, sm, sv = _adamw(cat([weights[n] for n in small_names]), cat([grads[n] for n in small_names]),
                        cat([moms[n][0] for n in small_names]), cat([moms[n][1] for n in small_names]),
                        "adamw_small")
    off = 0
    for n in small_names:
        sz = weights[n].size
        shp = weights[n].shape
        delta[n], new_m[n], new_v[n] = (sd[off:off + sz].reshape(shp), sm[off:off + sz].reshape(shp),
                                        sv[off:off + sz].reshape(shp))
        off += sz
    out_grads = [grads[n].reshape(weights[n].shape) for n in names]
    return (loss, dx0, *out_grads, *[delta[n] for n in names], *[new_m[n] for n in names],
            *[new_v[n] for n in names])
```

```python
import functools
import math

import numpy as np
import jax
import jax.numpy as jnp
from jax import lax
from jax.experimental import pallas as pl
from jax.experimental.pallas import tpu as pltpu

F32 = jnp.float32
BF16 = jnp.bfloat16
MESH = pl.DeviceIdType.MESH

N_DEV = 8
N_CHIP = 4
HEAD_DIM = 64
BLK = 128
A_Q_HEADS = 8
A_KV_HEADS = 2
A_GROUP = A_Q_HEADS // A_KV_HEADS
B_HEADS = 8
REL_BUCKETS = 32
REL_MAX_EXACT = 16
REL_MAX_DIST = 128
LRU_BLOCKS = 8
LRU_C = 8.0
CONV_WIDTH = 4
EPS = 1e-6
NEG = -1e30
SCALE = HEAD_DIM ** -0.5
LANE = 128
SUBLANE = 8
VMEM_LIMIT = 56 * 1024 * 1024
SCAN_CHUNK = 256
ADAM_LR = 0.001
ADAM_B1 = 0.9
ADAM_B2 = 0.999
ADAM_EPS = 1e-08
ADAM_WD = 0.01
ADAM_STEP = 10
HI = lax.Precision.HIGHEST


def _cp(sem=None):
    return pltpu.CompilerParams(dimension_semantics=sem, vmem_limit_bytes=VMEM_LIMIT)


def _dot(a, b):
    return jnp.dot(a, b, preferred_element_type=F32)


def _dot_nt(a, b):
    return lax.dot_general(a, b, (((1,), (1,)), ((), ())), preferred_element_type=F32)


def _dot_tn(a, b):
    return lax.dot_general(a, b, (((0,), (0,)), ((), ())), preferred_element_type=F32)


def _sigmoid(z):
    return 1.0 / (1.0 + jnp.exp(-z))


def _row_tile(rows, cap):
    if rows <= cap:
        return rows
    best = SUBLANE
    t = SUBLANE
    while t <= cap:
        if rows % t == 0:
            best = t
        t += SUBLANE
    return best


def _all_gather8(x_shard, name, space):
    m_per, n = x_shard.shape

    def body(x_ref, out_ref, send_sems, recv_sems, local_sem):
        x, y, c = lax.axis_index("x"), lax.axis_index("y"), lax.axis_index("c")
        me, sibling = (x, y, c), (x, y, 1 - c)
        chips = [(1 - x, y), (x, 1 - y), (1 - x, 1 - y)]

        def rows(px, py, pc):
            return out_ref.at[pl.ds((4 * px + 2 * py + pc) * m_per, m_per), :]

        def copy(k, block, to, src=None):
            return pltpu.make_async_remote_copy(
                src_ref=rows(*block) if src is None else src, dst_ref=rows(*block),
                send_sem=send_sems.at[k], recv_sem=recv_sems.at[k], device_id=to, device_id_type=MESH)

        mine = pltpu.make_async_copy(x_ref, rows(*me), local_sem)
        mine.start()
        first = [copy(0, me, sibling, src=x_ref)]
        first += [copy(1 + j, me, (*chip, c), src=x_ref) for j, chip in enumerate(chips)]
        for cp in first:
            cp.start()
        passed = [copy(4 + j, (*chip, c), sibling) for j, chip in enumerate(chips)]
        for j, chip in enumerate(chips):
            copy(1 + j, (*chip, c), me).wait_recv()
            passed[j].start()
        copy(0, sibling, me).wait_recv()
        for j, chip in enumerate(chips):
            copy(4 + j, (*chip, 1 - c), me).wait_recv()
        for cp in first + passed:
            cp.wait_send()
        mine.wait()

    return pl.pallas_call(
        body, name=name,
        out_shape=jax.ShapeDtypeStruct((N_DEV * m_per, n), x_shard.dtype),
        in_specs=[pl.BlockSpec(memory_space=space)],
        out_specs=pl.BlockSpec(memory_space=space),
        scratch_shapes=[pltpu.SemaphoreType.DMA((7,)), pltpu.SemaphoreType.DMA((7,)), pltpu.SemaphoreType.DMA],
        compiler_params=pltpu.CompilerParams(vmem_limit_bytes=VMEM_LIMIT),
    )(x_shard)


def _sibling_push(blocks, pick_other, name):
    nblk = blocks.shape[0]
    m, n = blocks.shape[-2:]

    def body(x_ref, out_ref, send_sems, recv_sems):
        x, y, c = lax.axis_index("x"), lax.axis_index("y"), lax.axis_index("c")
        copies = []
        for k in range(nblk):
            src = x_ref.at[k, 1 - c] if pick_other else x_ref.at[k]
            copies.append(pltpu.make_async_remote_copy(
                src_ref=src, dst_ref=out_ref.at[k], send_sem=send_sems.at[k], recv_sem=recv_sems.at[k],
                device_id=(x, y, 1 - c), device_id_type=MESH))
        for cp in copies:
            cp.start()
        for cp in copies:
            cp.wait_recv()
        for cp in copies:
            cp.wait_send()

    hbm = pl.BlockSpec(memory_space=pltpu.HBM)
    return pl.pallas_call(
        body, name=name,
        out_shape=jax.ShapeDtypeStruct((nblk, m, n), blocks.dtype),
        in_specs=[hbm], out_specs=hbm,
        scratch_shapes=[pltpu.SemaphoreType.DMA((nblk,)), pltpu.SemaphoreType.DMA((nblk,))],
    )(blocks)


def _chip_all_to_all(parts, name):
    _, m, n = parts.shape

    def body(x_ref, out_ref, send_sems, recv_sems, local_sem):
        x, y, c = lax.axis_index("x"), lax.axis_index("y"), lax.axis_index("c")
        me = 2 * x + y
        mine = pltpu.make_async_copy(x_ref.at[me], out_ref.at[me], local_sem)
        mine.start()
        copies = []
        for k in range(1, N_CHIP):
            px, py = x ^ ((k >> 1) & 1), y ^ (k & 1)
            copies.append(pltpu.make_async_remote_copy(
                src_ref=x_ref.at[2 * px + py], dst_ref=out_ref.at[me],
                send_sem=send_sems.at[k - 1], recv_sem=recv_sems.at[k - 1],
                device_id=(px, py, c), device_id_type=MESH))
        for cp in copies:
            cp.start()
        for cp in copies:
            cp.wait_recv()
        for cp in copies:
            cp.wait_send()
        mine.wait()

    hbm = pl.BlockSpec(memory_space=pltpu.HBM)
    return pl.pallas_call(
        body, name=name,
        out_shape=jax.ShapeDtypeStruct(parts.shape, parts.dtype),
        in_specs=[hbm], out_specs=hbm,
        scratch_shapes=[pltpu.SemaphoreType.DMA((N_CHIP - 1,)), pltpu.SemaphoreType.DMA((N_CHIP - 1,)),
                        pltpu.SemaphoreType.DMA],
    )(parts)


def _pair_sum(core, pieces, theirs, name):
    nblk, _, m, n = pieces.shape
    tr = _row_tile(m, 536)

    def body(c_ref, p_ref, t_ref, o_ref):
        o_ref[...] = (p_ref[...] + t_ref[...]).astype(BF16)

    return pl.pallas_call(
        body, name=name,
        grid_spec=pltpu.PrefetchScalarGridSpec(
            num_scalar_prefetch=1, grid=(nblk, m // tr),
            in_specs=[pl.BlockSpec((None, None, tr, n), lambda k, i, c_ref: (k, c_ref[0], i, 0)),
                      pl.BlockSpec((None, tr, n), lambda k, i, c_ref: (k, i, 0))],
            out_specs=pl.BlockSpec((None, tr, n), lambda k, i, c_ref: (k, i, 0))),
        out_shape=jax.ShapeDtypeStruct((nblk, m, n), BF16),
        compiler_params=_cp(("parallel", "parallel")),
    )(core, pieces, theirs)


def _sum_slots(slots, name):
    k, m, n = slots.shape
    tr = _row_tile(m, 536)

    def body(s_ref, o_ref):
        acc = s_ref[0].astype(F32)
        for j in range(1, k):
            acc = acc + s_ref[j].astype(F32)
        o_ref[...] = acc

    return pl.pallas_call(
        body, name=name, grid=(m // tr,),
        out_shape=jax.ShapeDtypeStruct((m, n), F32),
        in_specs=[pl.BlockSpec((k, tr, n), lambda i: (0, i, 0))],
        out_specs=pl.BlockSpec((tr, n), lambda i: (i, 0)),
        compiler_params=_cp(("parallel",)),
    )(slots)


def _ada_fwd(c_all, w, b, name):
    r, _ = c_all.shape
    n = w.shape[1]

    def body(c_ref, w_ref, b_ref, o_ref):
        cv = c_ref[...]
        act = cv * _sigmoid(cv)
        o_ref[...] = jnp.dot(act, w_ref[...], precision=HI, preferred_element_type=F32) + b_ref[...]

    return pl.pallas_call(body, name=name, out_shape=jax.ShapeDtypeStruct((r, n), F32),
                          compiler_params=_cp())(c_all, w, b)


def _ada_bwd(c_all, dmod_chip, dmod_all, name):
    r, d = c_all.shape
    nl, _, n = dmod_chip.shape

    def body(c_ref, dm_ref, da_ref, gw_ref, gb_ref):
        cv = c_ref[...]
        act = cv * _sigmoid(cv)
        for l in range(nl):
            gw_ref[l] = lax.dot_general(act, dm_ref[l], (((0,), (0,)), ((), ())), precision=HI,
                                        preferred_element_type=F32)
        gb_ref[...] = jnp.sum(da_ref[...], axis=0, keepdims=True)

    return pl.pallas_call(
        body, name=name,
        out_shape=(jax.ShapeDtypeStruct((nl, d, n), F32), jax.ShapeDtypeStruct((1, dmod_all.shape[1]), F32)),
        compiler_params=_cp())(c_all, dmod_chip, dmod_all)


def _norm_proj(x, g, scale, shift, w, f32_cols, name):
    b, s, d = x.shape
    n = w.shape[1]
    tm = min(s, 256)

    def body(x_ref, g_ref, sc_ref, sh_ref, w_ref, proj_ref, h_ref, *aux_ref):
        xv = x_ref[...]
        rstd = lax.rsqrt(jnp.mean(xv * xv, axis=-1, keepdims=True) + EPS)
        h = (xv * rstd) * g_ref[...] * (1.0 + sc_ref[...]) + sh_ref[...]
        hb = h.astype(BF16)
        h_ref[...] = hb
        proj = _dot(hb, w_ref[...])
        proj_ref[...] = proj.astype(BF16)
        if f32_cols:
            aux_ref[0][...] = proj[:, n - f32_cols:]

    row = lambda i, j: (i, j, 0)
    out_shape = [jax.ShapeDtypeStruct((b, s, n), BF16), jax.ShapeDtypeStruct((b, s, d), BF16)]
    out_specs = [pl.BlockSpec((None, tm, n), row), pl.BlockSpec((None, tm, d), row)]
    if f32_cols:
        out_shape.append(jax.ShapeDtypeStruct((b, s, f32_cols), F32))
        out_specs.append(pl.BlockSpec((None, tm, f32_cols), row))
    return pl.pallas_call(
        body, name=name, grid=(b, s // tm),
        out_shape=tuple(out_shape),
        in_specs=[pl.BlockSpec((None, tm, d), row),
                  pl.BlockSpec((1, d), lambda i, j: (0, 0)),
                  pl.BlockSpec((None, 1, d), lambda i, j: (i, 0, 0)),
                  pl.BlockSpec((None, 1, d), lambda i, j: (i, 0, 0)),
                  pl.BlockSpec((d, n), lambda i, j: (0, 0))],
        out_specs=tuple(out_specs),
        compiler_params=_cp(("parallel", "parallel")),
    )(x, g, scale, shift, w)


def _cat_refs(refs):
    vals = [r[...] for r in refs]
    return vals[0] if len(vals) == 1 else jnp.concatenate(vals, axis=-1)


def _gate_outproj(mix_parts, proj, gate_blk, w_out, x, gmod, name):
    b, s, _ = x.shape
    wd, d = w_out.shape
    tm = min(s, 256)
    npart = len(mix_parts)

    def body(*refs):
        mix_refs = refs[:npart]
        gate_ref, w_ref, x_ref, gm_ref, xo_ref, o_ref = refs[npart:]
        gt = gate_ref[...].astype(F32)
        y = (_cat_refs(mix_refs) * (gt * _sigmoid(gt))).astype(BF16)
        o = _dot(y, w_ref[...])
        o_ref[...] = o.astype(BF16)
        xo_ref[...] = x_ref[...] + gm_ref[...] * o

    return pl.pallas_call(
        body, name=name, grid=(b, s // tm),
        out_shape=(jax.ShapeDtypeStruct((b, s, d), F32), jax.ShapeDtypeStruct((b, s, d), BF16)),
        in_specs=[pl.BlockSpec((None, tm, p.shape[2]), lambda i, j: (i, j, 0)) for p in mix_parts] + [
                  pl.BlockSpec((None, tm, wd), lambda i, j: (i, j, gate_blk)),
                  pl.BlockSpec((wd, d), lambda i, j: (0, 0)),
                  pl.BlockSpec((None, tm, d), lambda i, j: (i, j, 0)),
                  pl.BlockSpec((None, 1, d), lambda i, j: (i, 0, 0))],
        out_specs=(pl.BlockSpec((None, tm, d), lambda i, j: (i, j, 0)),
                   pl.BlockSpec((None, tm, d), lambda i, j: (i, j, 0))),
        compiler_params=_cp(("parallel", "parallel")),
    )(*mix_parts, proj, w_out, x, gmod)


def _final_loss(x, g, target, name):
    b, s, d = x.shape
    tm = min(s, 256)

    def body(x_ref, g_ref, t_ref, loss_ref, dx_ref, dg_ref):
        first = jnp.logical_and(pl.program_id(0) == 0, pl.program_id(1) == 0)

        @pl.when(first)
        def _():
            loss_ref[...] = jnp.zeros_like(loss_ref)
            dg_ref[...] = jnp.zeros_like(dg_ref)

        xv = x_ref[...]
        gv = g_ref[...]
        rstd = lax.rsqrt(jnp.mean(xv * xv, axis=-1, keepdims=True) + EPS)
        xhat = xv * rstd
        err = xhat * gv - t_ref[...]
        row = jnp.mean(err * err, axis=-1, keepdims=True)
        loss_ref[...] += 0.5 * jnp.sum(row, axis=0, keepdims=True)
        dy = err * (1.0 / d)
        dg_ref[...] += jnp.sum(dy * xhat, axis=0, keepdims=True)
        dxh = dy * gv
        dx_ref[...] = rstd * (dxh - xhat * jnp.mean(dxh * xhat, axis=-1, keepdims=True))

    return pl.pallas_call(
        body, name=name, grid=(b, s // tm),
        out_shape=(jax.ShapeDtypeStruct((1, LANE), F32), jax.ShapeDtypeStruct((b, s, d), F32),
                   jax.ShapeDtypeStruct((1, d), F32)),
        in_specs=[pl.BlockSpec((None, tm, d), lambda i, j: (i, j, 0)),
                  pl.BlockSpec((1, d), lambda i, j: (0, 0)),
                  pl.BlockSpec((None, tm, d), lambda i, j: (i, j, 0))],
        out_specs=(pl.BlockSpec((1, LANE), lambda i, j: (0, 0)),
                   pl.BlockSpec((None, tm, d), lambda i, j: (i, j, 0)),
                   pl.BlockSpec((1, d), lambda i, j: (0, 0))),
        compiler_params=_cp(("arbitrary", "arbitrary")),
    )(x, g, target)


def _bwd_out(dxo, gmod, o, mix_parts, proj, gate_blk, w_out_t, name):
    b, s, d = dxo.shape
    wd = w_out_t.shape[1]
    tm = min(s, 256)
    npart = len(mix_parts)

    def body(dx_ref, gm_ref, o_ref, *refs):
        mix_refs = refs[:npart]
        gate_ref, wt_ref, dmix_ref, dgate_ref, do_ref, y_ref, dgm_ref = refs[npart:]

        @pl.when(pl.program_id(1) == 0)
        def _():
            dgm_ref[...] = jnp.zeros_like(dgm_ref)

        dx = dx_ref[...]
        dgm_ref[...] += jnp.sum(dx * o_ref[...].astype(F32), axis=0, keepdims=True)
        dob = (gm_ref[...] * dx).astype(BF16)
        do_ref[...] = dob
        dy = _dot(dob, wt_ref[...])
        gt = gate_ref[...].astype(F32)
        sg = _sigmoid(gt)
        silu = gt * sg
        mx = _cat_refs(mix_refs)
        y_ref[...] = (mx * silu).astype(BF16)
        dmix_ref[...] = dy * silu
        dgate_ref[...] = (dy * mx * (sg * (1.0 + gt * (1.0 - sg)))).astype(BF16)

    row = lambda i, j: (i, j, 0)
    return pl.pallas_call(
        body, name=name, grid=(b, s // tm),
        out_shape=(jax.ShapeDtypeStruct((b, s, wd), F32), jax.ShapeDtypeStruct((b, s, wd), BF16),
                   jax.ShapeDtypeStruct((b, s, d), BF16), jax.ShapeDtypeStruct((b, s, wd), BF16),
                   jax.ShapeDtypeStruct((b, 1, d), F32)),
        in_specs=[pl.BlockSpec((None, tm, d), row),
                  pl.BlockSpec((None, 1, d), lambda i, j: (i, 0, 0)),
                  pl.BlockSpec((None, tm, d), row)] + [
                  pl.BlockSpec((None, tm, p.shape[2]), row) for p in mix_parts] + [
                  pl.BlockSpec((None, tm, wd), lambda i, j: (i, j, gate_blk)),
                  pl.BlockSpec((d, wd), lambda i, j: (0, 0))],
        out_specs=(pl.BlockSpec((None, tm, wd), row), pl.BlockSpec((None, tm, wd), row),
                   pl.BlockSpec((None, tm, d), row), pl.BlockSpec((None, tm, wd), row),
                   pl.BlockSpec((None, 1, d), lambda i, j: (i, 0, 0))),
        compiler_params=_cp(("parallel", "arbitrary")),
    )(dxo, gmod, o, *mix_parts, proj, w_out_t)


def _bwd_in(dproj_parts, w_in_t, x, g, scale, dxo, name):
    b, s, d = x.shape
    n = w_in_t.shape[0]
    tm = min(s, 256)
    npart = len(dproj_parts)

    def body(*refs):
        dp_refs = refs[:npart]
        wt_ref, x_ref, g_ref, sc_ref, dxo_ref, dx_ref, dsh_ref, dsc_ref, dg_ref = refs[npart:]

        @pl.when(jnp.logical_and(pl.program_id(0) == 0, pl.program_id(1) == 0))
        def _():
            dg_ref[...] = jnp.zeros_like(dg_ref)

        @pl.when(pl.program_id(1) == 0)
        def _():
            dsh_ref[...] = jnp.zeros_like(dsh_ref)
            dsc_ref[...] = jnp.zeros_like(dsc_ref)

        dh = _dot(_cat_refs(dp_refs), wt_ref[...])
        xv = x_ref[...]
        gv = g_ref[...]
        one_sc = 1.0 + sc_ref[...]
        rstd = lax.rsqrt(jnp.mean(xv * xv, axis=-1, keepdims=True) + EPS)
        xhat = xv * rstd
        dsh_ref[...] += jnp.sum(dh, axis=0, keepdims=True)
        dsc_ref[...] += jnp.sum(dh * (xhat * gv), axis=0, keepdims=True)
        dhs = dh * one_sc
        dg_ref[...] += jnp.sum(dhs * xhat, axis=0, keepdims=True)
        dxh = dhs * gv
        dx_ref[...] = dxo_ref[...] + rstd * (dxh - xhat * jnp.mean(dxh * xhat, axis=-1, keepdims=True))

    row = lambda i, j: (i, j, 0)
    per_b = lambda i, j: (i, 0, 0)
    return pl.pallas_call(
        body, name=name, grid=(b, s // tm),
        out_shape=(jax.ShapeDtypeStruct((b, s, d), F32), jax.ShapeDtypeStruct((b, 1, d), F32),
                   jax.ShapeDtypeStruct((b, 1, d), F32), jax.ShapeDtypeStruct((1, d), F32)),
        in_specs=[pl.BlockSpec((None, tm, p.shape[2]), row) for p in dproj_parts] + [
                  pl.BlockSpec((n, d), lambda i, j: (0, 0)),
                  pl.BlockSpec((None, tm, d), row),
                  pl.BlockSpec((1, d), lambda i, j: (0, 0)),
                  pl.BlockSpec((None, 1, d), per_b),
                  pl.BlockSpec((None, tm, d), row)],
        out_specs=(pl.BlockSpec((None, tm, d), row), pl.BlockSpec((None, 1, d), per_b),
                   pl.BlockSpec((None, 1, d), per_b), pl.BlockSpec((1, d), lambda i, j: (0, 0))),
        compiler_params=_cp(("arbitrary", "arbitrary")),
    )(*dproj_parts, w_in_t, x, g, scale, dxo)


def _matmul_tn(a, b_parts, name):
    bsz, s, m = a.shape
    n = sum(p.shape[2] for p in b_parts)
    tk = next(c for c in (512, 256, 128) if s % c == 0)
    npart = len(b_parts)

    def body(a_ref, *refs):
        b_refs, o_ref = refs[:npart], refs[npart]

        @pl.when(jnp.logical_and(pl.program_id(0) == 0, pl.program_id(1) == 0))
        def _():
            o_ref[...] = jnp.zeros_like(o_ref)

        o_ref[...] += _dot_tn(a_ref[...], _cat_refs(b_refs))

    row = lambda i, k: (i, k, 0)
    return pl.pallas_call(
        body, name=name, grid=(bsz, s // tk),
        out_shape=jax.ShapeDtypeStruct((m, n), F32),
        in_specs=[pl.BlockSpec((None, tk, m), row)] + [pl.BlockSpec((None, tk, p.shape[2]), row) for p in b_parts],
        out_specs=pl.BlockSpec((m, n), lambda i, k: (0, 0)),
        compiler_params=_cp(("arbitrary", "arbitrary")),
    )(a, *b_parts)


def _rel_buckets():
    qi = np.arange(BLK)[:, None]
    kj = np.arange(2 * BLK)[None, :]
    rel = qi - kj + BLK
    n = np.maximum(rel, 0)
    nf = np.maximum(n, 1).astype(np.float32)
    large = REL_MAX_EXACT + (np.log(nf / REL_MAX_EXACT) / math.log(REL_MAX_DIST / REL_MAX_EXACT)
                             * (REL_BUCKETS - REL_MAX_EXACT)).astype(np.int32)
    large = np.minimum(large, REL_BUCKETS - 1)
    bucket = np.where(n < REL_MAX_EXACT, n, large).astype(np.int32)
    valid = ((rel >= 0) & (rel < BLK)).astype(np.int32)
    return bucket, valid


def _swa_bias(rel_bias_t, bucket, valid, name):
    nh = rel_bias_t.shape[0]

    def body(rb_ref, bk_ref, vl_ref, o_ref):
        h = pl.program_id(0)
        bk = bk_ref[...]
        acc = jnp.zeros(bk.shape, F32)
        for i in range(REL_BUCKETS):
            acc = jnp.where(bk == i, rb_ref[h, i], acc)
        o_ref[...] = jnp.where(vl_ref[...] > 0, acc, NEG)

    return pl.pallas_call(
        body, name=name, grid=(nh,),
        out_shape=jax.ShapeDtypeStruct((nh, BLK, 2 * BLK), F32),
        in_specs=[pl.BlockSpec(memory_space=pltpu.SMEM),
                  pl.BlockSpec((BLK, 2 * BLK), lambda h: (0, 0)),
                  pl.BlockSpec((BLK, 2 * BLK), lambda h: (0, 0))],
        out_specs=pl.BlockSpec((None, BLK, 2 * BLK), lambda h: (h, 0, 0)),
        compiler_params=_cp(("arbitrary",)),
    )(rel_bias_t, bucket, valid)


def _swa_scores(n, q, kw, bias_ref):
    sc = _dot_nt(q, kw) * SCALE + bias_ref[...]
    second = lax.broadcasted_iota(jnp.int32, sc.shape, 1) >= BLK
    return jnp.where(jnp.logical_or(n > 0, second), sc, NEG)


def _pad_front(dst_ref, src_ref):
    dst_ref[0:BLK, :] = jnp.zeros((BLK, dst_ref.shape[1]), dst_ref.dtype)
    dst_ref[BLK:, :] = src_ref[...]


def _swa_fwd(q, k, v, bias, sinks, name):
    b, hkv, nb, rows, hd = q.shape
    s = nb * BLK

    def body(q_ref, k_ref, v_ref, bias_ref, sink_ref, o_ref, l_ref, kpad_ref, vpad_ref):
        _pad_front(kpad_ref, k_ref)
        _pad_front(vpad_ref, v_ref)
        sink = sink_ref[...]

        def step(n, carry):
            w0 = pl.multiple_of(n * BLK, BLK)
            sc = _swa_scores(n, q_ref[n], kpad_ref[pl.ds(w0, 2 * BLK), :], bias_ref)
            m = jnp.maximum(jnp.max(sc, axis=1, keepdims=True), sink)
            e = jnp.exp(sc - m)
            den = jnp.sum(e, axis=1, keepdims=True) + jnp.exp(sink - m)
            o_ref[n] = _dot((e * (1.0 / den)).astype(BF16), vpad_ref[pl.ds(w0, 2 * BLK), :])
            l_ref[n] = m + jnp.log(den)
            return carry

        lax.fori_loop(0, nb, step, 0)

    qspec = pl.BlockSpec((None, None, nb, rows, hd), lambda i, kv: (i, kv, 0, 0, 0))
    kspec = pl.BlockSpec((None, None, s, hd), lambda i, kv: (i, kv, 0, 0))
    return pl.pallas_call(
        body, name=name, grid=(b, hkv),
        out_shape=(jax.ShapeDtypeStruct((b, hkv, nb, rows, hd), F32), jax.ShapeDtypeStruct((b, hkv, nb, rows, 1), F32)),
        in_specs=[qspec, kspec, kspec,
                  pl.BlockSpec((None, rows, 2 * BLK), lambda i, kv: (kv, 0, 0)),
                  pl.BlockSpec((None, rows, 1), lambda i, kv: (kv, 0, 0))],
        out_specs=(qspec, pl.BlockSpec((None, None, nb, rows, 1), lambda i, kv: (i, kv, 0, 0, 0))),
        scratch_shapes=[pltpu.VMEM((s + BLK, hd), BF16), pltpu.VMEM((s + BLK, hd), BF16)],
        compiler_params=_cp(("parallel", "parallel")),
    )(q, k, v, bias, sinks)


def _swa_bwd(q, k, v, bias, sinks, do, lse, name):
    b, hkv, nb, rows, hd = q.shape
    s = nb * BLK

    def body(q_ref, k_ref, v_ref, bias_ref, sink_ref, do_ref, l_ref,
             dq_ref, dk_ref, dv_ref, db_ref, dsk_ref, kpad_ref, vpad_ref, dkpad_ref, dvpad_ref):
        _pad_front(kpad_ref, k_ref)
        _pad_front(vpad_ref, v_ref)
        dkpad_ref[...] = jnp.zeros_like(dkpad_ref)
        dvpad_ref[...] = jnp.zeros_like(dvpad_ref)
        db_ref[...] = jnp.zeros_like(db_ref)
        sink = sink_ref[...]

        def step(n, dsink):
            w0 = pl.multiple_of(n * BLK, BLK)
            win = pl.ds(w0, 2 * BLK)
            qn = q_ref[n]
            kw = kpad_ref[win, :]
            ln = l_ref[n]
            p = jnp.exp(_swa_scores(n, qn, kw, bias_ref) - ln)
            dob = do_ref[n]
            dp = _dot_nt(dob, vpad_ref[win, :])
            delta = jnp.sum(p * dp, axis=1, keepdims=True)
            ds = p * (dp - delta)
            db_ref[...] += ds
            dsb = ds.astype(BF16)
            dq_ref[n] = (_dot(dsb, kw) * SCALE).astype(BF16)
            dkpad_ref[win, :] += _dot_tn(dsb, qn)
            dvpad_ref[win, :] += _dot_tn(p.astype(BF16), dob)
            return dsink - jnp.exp(sink - ln) * delta

        dsink = lax.fori_loop(0, nb, step, jnp.zeros((rows, 1), F32))
        for g in range(A_GROUP):
            tot = jnp.sum(dsink[g * BLK:(g + 1) * BLK, :], axis=0, keepdims=True)
            dsk_ref[g] = jnp.broadcast_to(tot, (1, LANE))
        dk_ref[...] = (dkpad_ref[BLK:, :] * SCALE).astype(BF16)
        dv_ref[...] = dvpad_ref[BLK:, :].astype(BF16)

    qspec = pl.BlockSpec((None, None, nb, rows, hd), lambda i, kv: (i, kv, 0, 0, 0))
    kspec = pl.BlockSpec((None, None, s, hd), lambda i, kv: (i, kv, 0, 0))
    return pl.pallas_call(
        body, name=name, grid=(b, hkv),
        out_shape=(jax.ShapeDtypeStruct((b, hkv, nb, rows, hd), BF16), jax.ShapeDtypeStruct((b, hkv, s, hd), BF16),
                   jax.ShapeDtypeStruct((b, hkv, s, hd), BF16), jax.ShapeDtypeStruct((b, hkv, rows, 2 * BLK), F32),
                   jax.ShapeDtypeStruct((b, hkv, A_GROUP, 1, LANE), F32)),
        in_specs=[qspec, kspec, kspec,
                  pl.BlockSpec((None, rows, 2 * BLK), lambda i, kv: (kv, 0, 0)),
                  pl.BlockSpec((None, rows, 1), lambda i, kv: (kv, 0, 0)),
                  qspec,
                  pl.BlockSpec((None, None, nb, rows, 1), lambda i, kv: (i, kv, 0, 0, 0))],
        out_specs=(qspec, kspec, kspec,
                   pl.BlockSpec((None, None, rows, 2 * BLK), lambda i, kv: (i, kv, 0, 0)),
                   pl.BlockSpec((None, None, A_GROUP, 1, LANE), lambda i, kv: (i, kv, 0, 0, 0))),
        scratch_shapes=[pltpu.VMEM((s + BLK, hd), BF16), pltpu.VMEM((s + BLK, hd), BF16),
                        pltpu.VMEM((s + BLK, hd), F32), pltpu.VMEM((s + BLK, hd), F32)],
        compiler_params=_cp(("parallel", "parallel")),
    )(q, k, v, bias, sinks, do, lse)


def _swa_small_grads(db, dsk, bucket, name):
    b, nh = db.shape[0], db.shape[1]

    def body(db_ref, dsk_ref, bk_ref, gb_ref, gs_ref):
        acc = db_ref[0]
        sk = dsk_ref[0]
        for i in range(1, b):
            acc = acc + db_ref[i]
            sk = sk + dsk_ref[i]
        gs_ref[...] = sk
        bk = bk_ref[...]
        for i in range(REL_BUCKETS):
            part = jnp.sum(jnp.where(bk == i, acc, 0.0), axis=1, keepdims=True)
            tot = jnp.sum(part, axis=0, keepdims=True)
            gb_ref[i:i + 1, :] = jnp.broadcast_to(tot, (1, LANE))

    return pl.pallas_call(
        body, name=name, grid=(nh,),
        out_shape=(jax.ShapeDtypeStruct((nh, REL_BUCKETS, LANE), F32), jax.ShapeDtypeStruct((nh, 1, LANE), F32)),
        in_specs=[pl.BlockSpec((b, None, BLK, 2 * BLK), lambda h: (0, h, 0, 0)),
                  pl.BlockSpec((b, None, 1, LANE), lambda h: (0, h, 0, 0)),
                  pl.BlockSpec((BLK, 2 * BLK), lambda h: (0, 0))],
        out_specs=(pl.BlockSpec((None, REL_BUCKETS, LANE), lambda h: (h, 0, 0)),
                   pl.BlockSpec((None, 1, LANE), lambda h: (h, 0, 0))),
        compiler_params=_cp(("parallel",)),
    )(db, dsk, bucket)


def _log_sigmoid(z):
    return jnp.minimum(z, 0.0) - jnp.log(1.0 + jnp.exp(-jnp.abs(z)))


def _fox_decay(z, bf, name):
    b, s, w = z.shape
    nb = s // BLK

    def body(z_ref, bf_ref, f_ref):
        r = lax.broadcasted_iota(jnp.int32, (BLK, BLK), 0)
        c = lax.broadcasted_iota(jnp.int32, (BLK, BLK), 1)
        tri = (c <= r).astype(F32)

        def step(n, carry):
            r0 = pl.multiple_of(n * BLK, BLK)
            lf = _log_sigmoid(z_ref[pl.ds(r0, BLK), :] + bf_ref[...])
            f_ref[pl.ds(r0, BLK), :] = jnp.dot(tri, lf, precision=HI, preferred_element_type=F32) + carry
            return carry + jnp.sum(lf, axis=0, keepdims=True)

        lax.fori_loop(0, nb, step, jnp.zeros((1, w), F32))

    spec = pl.BlockSpec((None, s, w), lambda i: (i, 0, 0))
    return pl.pallas_call(
        body, name=name, grid=(b,), out_shape=jax.ShapeDtypeStruct((b, s, w), F32),
        in_specs=[spec, pl.BlockSpec((1, w), lambda i: (0, 0))], out_specs=spec,
        compiler_params=_cp(("parallel",)),
    )(z, bf)


def _fox_dgate(df, z, bf, nheads, name):
    b, s, w = z.shape
    nb = s // BLK

    def body(df_ref, z_ref, bf_ref, dz_ref, dbf_ref):
        @pl.when(pl.program_id(0) == 0)
        def _():
            dbf_ref[...] = jnp.zeros_like(dbf_ref)

        r = lax.broadcasted_iota(jnp.int32, (BLK, BLK), 0)
        c = lax.broadcasted_iota(jnp.int32, (BLK, BLK), 1)
        tri = (c >= r).astype(F32)
        lane = lax.broadcasted_iota(jnp.int32, (BLK, w), 1)

        def step(i, carry):
            tail, dbf = carry
            r0 = pl.multiple_of((nb - 1 - i) * BLK, BLK)
            dfb = df_ref[pl.ds(r0, BLK), :]
            dlf = jnp.dot(tri, dfb, precision=HI, preferred_element_type=F32) + tail
            dz = jnp.where(lane < nheads, dlf * _sigmoid(-(z_ref[pl.ds(r0, BLK), :] + bf_ref[...])), 0.0)
            dz_ref[pl.ds(r0, BLK), :] = dz
            return tail + jnp.sum(dfb, axis=0, keepdims=True), dbf + jnp.sum(dz, axis=0, keepdims=True)

        zero = jnp.zeros((1, w), F32)
        _, dbf = lax.fori_loop(0, nb, step, (zero, zero))
        dbf_ref[...] += dbf

    spec = pl.BlockSpec((None, s, w), lambda i: (i, 0, 0))
    one = pl.BlockSpec((1, w), lambda i: (0, 0))
    return pl.pallas_call(
        body, name=name, grid=(b,),
        out_shape=(jax.ShapeDtypeStruct((b, s, w), F32), jax.ShapeDtypeStruct((1, w), F32)),
        in_specs=[spec, spec, one], out_specs=(spec, one),
        compiler_params=_cp(("arbitrary",)),
    )(df, z, bf)


def _fox_segments(nb):
    per = max(1, nb // 4)
    return per, nb // per


def _head_masks(shape, axis):
    idx = lax.broadcasted_iota(jnp.int32, shape, axis)
    return idx < HEAD_DIM, idx >= HEAD_DIM


def _fox_fwd(proj, qblk, kblk, vblk, fcol, frow, name):
    b, s, _ = proj.shape
    nh = fcol.shape[1]
    npair = nh // 2
    per, nseg = _fox_segments(s // BLK)

    def body(q_ref, k_ref, v_ref, fc_ref, fr_ref, o_ref, l_ref, qm_ref, kt_ref, vb_ref):
        lo, hi = _head_masks((s, LANE), 1)
        qv = q_ref[...].astype(F32) * SCALE
        qm_ref[0] = jnp.where(lo, qv, 0.0).astype(BF16)
        qm_ref[1] = jnp.where(hi, qv, 0.0).astype(BF16)
        kt_ref[...] = k_ref[...].astype(F32).T.astype(BF16)
        vb_ref[...] = v_ref[...].astype(BF16)
        lane_lo = lax.broadcasted_iota(jnp.int32, (BLK, LANE), 1) < HEAD_DIM
        tail = per * BLK
        causal = (lax.broadcasted_iota(jnp.int32, (BLK, tail), 1)
                  - lax.broadcasted_iota(jnp.int32, (BLK, tail), 0))
        for seg in range(nseg):
            w = (seg + 1) * tail

            def qstep(n, carry):
                r0 = pl.multiple_of(n * BLK, BLK)
                outs = []
                for hh in range(2):
                    sc = _dot(qm_ref[hh, pl.ds(r0, BLK), :], kt_ref[:, :w])
                    sc = sc + (fc_ref[hh, pl.ds(r0, BLK), :] - fr_ref[hh, :, :w])
                    masked = jnp.where(causal <= (n - seg * per) * BLK, sc[:, w - tail:], NEG)
                    sc = masked if seg == 0 else jnp.concatenate([sc[:, :w - tail], masked], axis=1)
                    m = jnp.max(sc, axis=1, keepdims=True)
                    e = jnp.exp(sc - m)
                    l = jnp.sum(e, axis=1, keepdims=True)
                    outs.append(_dot((e * (1.0 / l)).astype(BF16), vb_ref[:w, :]))
                    l_ref[hh, pl.ds(r0, BLK), :] = m + jnp.log(l)
                o_ref[pl.ds(r0, BLK), :] = jnp.where(lane_lo, outs[0], outs[1])
                return carry

            lax.fori_loop(seg * per, (seg + 1) * per, qstep, 0)

    def tok(blk):
        return pl.BlockSpec((None, s, LANE), lambda i, p: (i, 0, blk + p))

    col = pl.BlockSpec((None, 2, s, 1), lambda i, p: (i, p, 0, 0))
    rowspec = pl.BlockSpec((None, 2, 1, s), lambda i, p: (i, p, 0, 0))
    return pl.pallas_call(
        body, name=name, grid=(b, npair),
        out_shape=(jax.ShapeDtypeStruct((b, s, nh * HEAD_DIM), F32), jax.ShapeDtypeStruct((b, nh, s, 1), F32)),
        in_specs=[tok(qblk), tok(kblk), tok(vblk), col, rowspec],
        out_specs=(pl.BlockSpec((None, s, LANE), lambda i, p: (i, 0, p)), col),
        scratch_shapes=[pltpu.VMEM((2, s, LANE), BF16), pltpu.VMEM((LANE, s), BF16), pltpu.VMEM((s, LANE), BF16)],
        compiler_params=_cp(("parallel", "parallel")),
    )(proj, proj, proj, fcol, frow)


def _fox_bwd(proj, qblk, kblk, vblk, dmix, doblk, fcol, frow, frowb, lse, lserowb, name):
    b, s, _ = proj.shape
    nh = fcol.shape[1]
    npair = nh // 2
    nb = s // BLK
    per, nseg = _fox_segments(nb)

    def body(q_ref, k_ref, v_ref, do_ref, fc_ref, fr_ref, frb_ref, l_ref, lrb_ref,
             dq_ref, dk_ref, dv_ref, dfr_ref,
             qm_ref, dom_ref, kb_ref, vb_ref, kt_ref, vt_ref, qtm_ref, dotm_ref, dka_ref, dva_ref):
        lo, hi = _head_masks((s, LANE), 1)
        qv = q_ref[...].astype(F32) * SCALE
        dov = do_ref[...]
        for hh, msk in enumerate((lo, hi)):
            qm_ref[hh] = jnp.where(msk, qv, 0.0).astype(BF16)
            dom_ref[hh] = jnp.where(msk, dov, 0.0).astype(BF16)
        kv = k_ref[...].astype(F32)
        vv = v_ref[...].astype(F32)
        kb_ref[...] = kv.astype(BF16)
        vb_ref[...] = vv.astype(BF16)
        kt_ref[...] = kv.T.astype(BF16)
        vt_ref[...] = vv.T.astype(BF16)
        rlo, rhi = _head_masks((LANE, BLK), 0)

        def tstep(n, carry):
            r0 = pl.multiple_of(n * BLK, BLK)
            qt = (q_ref[pl.ds(r0, BLK), :].astype(F32) * SCALE).T
            dt = do_ref[pl.ds(r0, BLK), :].T
            for hh, msk in enumerate((rlo, rhi)):
                qtm_ref[hh, n] = jnp.where(msk, qt, 0.0).astype(BF16)
                dotm_ref[hh, n] = jnp.where(msk, dt, 0.0).astype(BF16)
            return carry

        lax.fori_loop(0, nb, tstep, 0)
        dka_ref[...] = jnp.zeros_like(dka_ref)
        dva_ref[...] = jnp.zeros_like(dva_ref)
        dfr_ref[...] = jnp.zeros_like(dfr_ref)
        lane_lo = lax.broadcasted_iota(jnp.int32, (BLK, LANE), 1) < HEAD_DIM
        tail = per * BLK
        causal = (lax.broadcasted_iota(jnp.int32, (BLK, tail), 1)
                  - lax.broadcasted_iota(jnp.int32, (BLK, tail), 0))
        causal_t = (lax.broadcasted_iota(jnp.int32, (tail, BLK), 0)
                    - lax.broadcasted_iota(jnp.int32, (tail, BLK), 1))
        for seg in range(nseg):
            w = (seg + 1) * tail

            def nstep(n, carry):
                r0 = pl.multiple_of(n * BLK, BLK)
                lim = (n - seg * per) * BLK
                dqs = []
                for hh in range(2):
                    qn = qm_ref[hh, pl.ds(r0, BLK), :]
                    don = dom_ref[hh, pl.ds(r0, BLK), :]
                    sc = _dot(qn, kt_ref[:, :w]) + ((fc_ref[hh, pl.ds(r0, BLK), :] - l_ref[hh, pl.ds(r0, BLK), :])
                                                   - fr_ref[hh, :, :w])
                    masked = jnp.where(causal <= lim, sc[:, w - tail:], NEG)
                    p = jnp.exp(masked if seg == 0 else jnp.concatenate([sc[:, :w - tail], masked], axis=1))
                    dp = _dot(don, vt_ref[:, :w])
                    ds = p * (dp - jnp.sum(p * dp, axis=1, keepdims=True))
                    dqs.append(_dot(ds.astype(BF16), kb_ref[:w, :]))
                    dfr_ref[hh, :, :w] -= jnp.sum(ds, axis=0, keepdims=True)
                    sct = _dot(kb_ref[:w, :], qtm_ref[hh, n]) + ((frb_ref[hh, n] - lrb_ref[hh, n]) - fc_ref[hh, :w, :])
                    masked_t = jnp.where(causal_t <= lim, sct[w - tail:, :], NEG)
                    pt = jnp.exp(masked_t if seg == 0 else jnp.concatenate([sct[:w - tail, :], masked_t], axis=0))
                    dpt = _dot(vb_ref[:w, :], dotm_ref[hh, n])
                    dst = pt * (dpt - jnp.sum(pt * dpt, axis=0, keepdims=True))
                    dka_ref[:w, :] += _dot(dst.astype(BF16), qn)
                    dva_ref[:w, :] += _dot(pt.astype(BF16), don)
                dq_ref[pl.ds(r0, BLK), :] = (jnp.where(lane_lo, dqs[0], dqs[1]) * SCALE).astype(BF16)
                return carry

            lax.fori_loop(seg * per, (seg + 1) * per, nstep, 0)
        dk_ref[...] = dka_ref[...].astype(BF16)
        dv_ref[...] = dva_ref[...].astype(BF16)

    def tok(blk):
        return pl.BlockSpec((None, s, LANE), lambda i, p: (i, 0, blk + p))

    col = pl.BlockSpec((None, 2, s, 1), lambda i, p: (i, p, 0, 0))
    rowspec = pl.BlockSpec((None, 2, 1, s), lambda i, p: (i, p, 0, 0))
    rowbspec = pl.BlockSpec((None, 2, nb, 1, BLK), lambda i, p: (i, p, 0, 0, 0))
    outtok = pl.BlockSpec((None, s, LANE), lambda i, p: (i, 0, p))
    shp = jax.ShapeDtypeStruct((b, s, nh * HEAD_DIM), BF16)
    return pl.pallas_call(
        body, name=name, grid=(b, npair),
        out_shape=(shp, shp, shp, jax.ShapeDtypeStruct((b, nh, 1, s), F32)),
        in_specs=[tok(qblk), tok(kblk), tok(vblk),
                  pl.BlockSpec((None, s, LANE), lambda i, p: (i, 0, doblk + p)),
                  col, rowspec, rowbspec, col, rowbspec],
        out_specs=(outtok, outtok, outtok, rowspec),
        scratch_shapes=[pltpu.VMEM((2, s, LANE), BF16), pltpu.VMEM((2, s, LANE), BF16),
                        pltpu.VMEM((s, LANE), BF16), pltpu.VMEM((s, LANE), BF16),
                        pltpu.VMEM((LANE, s), BF16), pltpu.VMEM((LANE, s), BF16),
                        pltpu.VMEM((2, nb, LANE, BLK), BF16), pltpu.VMEM((2, nb, LANE, BLK), BF16),
                        pltpu.VMEM((s, LANE), F32), pltpu.VMEM((s, LANE), F32)],
        compiler_params=_cp(("parallel", "parallel")),
    )(proj, proj, proj, dmix, fcol, frow, frowb, lse, lserowb)


def _expm1(x):
    poly = x * (1.0 + x * (1.0 / 2 + x * (1.0 / 6 + x * (1.0 / 24 + x * (1.0 / 120 + x * (1.0 / 720))))))
    return jnp.where(x > -0.1, poly, jnp.exp(x) - 1.0)


def _softplus(z):
    return jnp.maximum(z, 0.0) + jnp.log(1.0 + jnp.exp(-jnp.abs(z)))


def _scan_rows(a, u, carry, row, up):
    tc, c = a.shape
    d = 1
    while d < tc:
        if d < SUBLANE:
            keep = (row >= d) if up else (row < tc - d)
            shift = d if up else tc - d
            a_sh = jnp.where(keep, pltpu.roll(a, shift, 0), 1.0)
            u_sh = jnp.where(keep, pltpu.roll(u, shift, 0), 0.0)
        elif up:
            a_sh = jnp.concatenate([jnp.ones((d, c), F32), a[:tc - d]], axis=0)
            u_sh = jnp.concatenate([jnp.zeros((d, c), F32), u[:tc - d]], axis=0)
        else:
            a_sh = jnp.concatenate([a[d:], jnp.ones((d, c), F32)], axis=0)
            u_sh = jnp.concatenate([u[d:], jnp.zeros((d, c), F32)], axis=0)
        u = a * u_sh + u
        a = a * a_sh
        d *= 2
    return u + a * carry


def _scan_up(a, u, carry, row):
    return _scan_rows(a, u, carry, row, True)


def _scan_down(bnext, g, carry, row):
    return _scan_rows(bnext, g, carry, row, False)


def _pick_row(val, row, which):
    return jnp.sum(jnp.where(row == which, val, 0.0), axis=0, keepdims=True)


def _lru_gates(xpad_ref, t0, tc, cw_ref, cb_ref, wa, ba_ref, wx, bx_ref, sp):
    xw = xpad_ref[pl.ds(t0, tc + SUBLANE), :]
    xc = cb_ref[...]
    for j in range(CONV_WIDTH):
        sh = CONV_WIDTH - 1 - j
        xs = xw if sh == 0 else pltpu.roll(xw, sh, 0)
        xc = xc + xs[SUBLANE:, :] * cw_ref[j:j + 1, :]
    xcb = xc.astype(BF16)
    r = _sigmoid(_dot(xcb, wa) + ba_ref[...])
    i = _sigmoid(_dot(xcb, wx) + bx_ref[...])
    la = -LRU_C * r * sp
    return xc, r, i, la


def _lru_specs(s, cb):
    seq = lambda bi, ni: (bi, 0, ni)
    return dict(
        seq=pl.BlockSpec((None, s, cb), seq),
        cw=pl.BlockSpec((CONV_WIDTH, cb), lambda bi, ni: (0, ni)),
        vec=pl.BlockSpec((1, cb), lambda bi, ni: (0, ni)),
        wblk=pl.BlockSpec((None, cb, cb), lambda bi, ni: (ni, 0, 0)),
    )


def _lru_fwd(proj, cw, cb_, wa, ba, wx, bx, lam, name):
    b, s, _ = proj.shape
    nblk, cb, _ = wa.shape
    tc = min(s, SCAN_CHUNK)
    nc = s // tc

    def body(x_ref, cw_ref, cb_ref, wa_ref, ba_ref, wx_ref, bx_ref, lam_ref, hs_ref, xpad_ref):
        xpad_ref[0:SUBLANE, :] = jnp.zeros((SUBLANE, cb), F32)
        xpad_ref[SUBLANE:, :] = x_ref[...].astype(F32)
        wa_b = wa_ref[...].astype(BF16)
        wx_b = wx_ref[...].astype(BF16)
        sp = _softplus(-lam_ref[...])
        row = lax.broadcasted_iota(jnp.int32, (tc, cb), 0)

        def chunk(ci, carry):
            t0 = pl.multiple_of(ci * tc, tc)
            xc, r, i, la = _lru_gates(xpad_ref, t0, tc, cw_ref, cb_ref, wa_b, ba_ref, wx_b, bx_ref, sp)
            a = jnp.exp(la)
            u = jnp.sqrt(-_expm1(2.0 * la)) * (i * xc)
            h = _scan_up(a, u, carry, row)
            hs_ref[pl.ds(t0, tc), :] = h
            return _pick_row(h, row, tc - 1)

        lax.fori_loop(0, nc, chunk, jnp.zeros((1, cb), F32))

    sp_ = _lru_specs(s, cb)
    return pl.pallas_call(
        body, name=name, grid=(b, nblk),
        out_shape=jax.ShapeDtypeStruct((b, s, nblk * cb), F32),
        in_specs=[sp_["seq"], sp_["cw"], sp_["vec"], sp_["wblk"], sp_["vec"], sp_["wblk"], sp_["vec"], sp_["vec"]],
        out_specs=sp_["seq"],
        scratch_shapes=[pltpu.VMEM((s + SUBLANE, cb), F32)],
        compiler_params=_cp(("parallel", "parallel")),
    )(proj, cw, cb_, wa, ba, wx, bx, lam)


def _lru_bwd(proj, hs, dhs, cw, cb_, wa, ba, wx, bx, lam, name):
    b, s, _ = proj.shape
    nblk, cb, _ = wa.shape
    tc = min(s, SCAN_CHUNK)
    nc = s // tc

    def body(x_ref, hs_ref, dhs_ref, cw_ref, cb_ref, wa_ref, ba_ref, wx_ref, bx_ref, lam_ref,
             dx_ref, dcw_ref, dcb_ref, dwa_ref, dba_ref, dwx_ref, dbx_ref, dlam_ref,
             xpad_ref, hpad_ref, dcpad_ref, xc_ref, r_ref, i_ref, a_ref):
        @pl.when(pl.program_id(1) == 0)
        def _():
            for ref in (dcw_ref, dcb_ref, dwa_ref, dba_ref, dwx_ref, dbx_ref, dlam_ref):
                ref[...] = jnp.zeros_like(ref)

        zeros8 = jnp.zeros((SUBLANE, cb), F32)
        xpad_ref[0:SUBLANE, :] = zeros8
        xpad_ref[SUBLANE:, :] = x_ref[...].astype(F32)
        hpad_ref[0:SUBLANE, :] = zeros8
        hpad_ref[SUBLANE:, :] = hs_ref[...]
        dcpad_ref[s:s + SUBLANE, :] = zeros8
        wa_b = wa_ref[...].astype(BF16)
        wx_b = wx_ref[...].astype(BF16)
        lam_v = lam_ref[...]
        sp = _softplus(-lam_v)
        dsp_dlam = -_sigmoid(-lam_v)
        row = lax.broadcasted_iota(jnp.int32, (tc, cb), 0)

        def recompute(ci, carry):
            t0 = pl.multiple_of(ci * tc, tc)
            xc, r, i, la = _lru_gates(xpad_ref, t0, tc, cw_ref, cb_ref, wa_b, ba_ref, wx_b, bx_ref, sp)
            xc_ref[pl.ds(t0, tc), :] = xc
            r_ref[pl.ds(t0, tc), :] = r
            i_ref[pl.ds(t0, tc), :] = i
            a_ref[pl.ds(t0, tc), :] = jnp.exp(la)
            return carry

        lax.fori_loop(0, nc, recompute, 0)

        def adjoint(k, carry):
            g_next, a_first_next = carry
            t0 = pl.multiple_of((nc - 1 - k) * tc, tc)
            a = a_ref[pl.ds(t0, tc), :]
            a_next = jnp.where(row == tc - 1, a_first_next, pltpu.roll(a, tc - 1, 0))
            gg = _scan_down(a_next, dhs_ref[pl.ds(t0, tc), :], g_next, row)
            h_prev = pltpu.roll(hpad_ref[pl.ds(t0, tc + SUBLANE), :], 1, 0)[SUBLANE:, :]
            xc = xc_ref[pl.ds(t0, tc), :]
            r = r_ref[pl.ds(t0, tc), :]
            i = i_ref[pl.ds(t0, tc), :]
            mult = jnp.sqrt(-_expm1(-2.0 * LRU_C * r * sp))
            d_mult = gg * i * xc
            d_i = gg * mult * xc
            d_xc = gg * mult * i
            d_la = gg * h_prev * a - d_mult * (a * a) / mult
            d_zr = (d_la * (-LRU_C * sp)) * r * (1.0 - r)
            d_zi = d_i * i * (1.0 - i)
            dlam_ref[...] += jnp.sum(d_la * (-LRU_C * r), axis=0, keepdims=True) * dsp_dlam
            dzr_b = d_zr.astype(BF16)
            dzi_b = d_zi.astype(BF16)
            xcb = xc.astype(BF16)
            d_xc = d_xc + _dot_nt(dzr_b, wa_b) + _dot_nt(dzi_b, wx_b)
            dwa_ref[...] += _dot_tn(xcb, dzr_b)
            dwx_ref[...] += _dot_tn(xcb, dzi_b)
            dba_ref[...] += jnp.sum(d_zr, axis=0, keepdims=True)
            dbx_ref[...] += jnp.sum(d_zi, axis=0, keepdims=True)
            dcb_ref[...] += jnp.sum(d_xc, axis=0, keepdims=True)
            dcpad_ref[pl.ds(t0, tc), :] = d_xc
            return _pick_row(gg, row, 0), _pick_row(a, row, 0)

        zero = jnp.zeros((1, cb), F32)
        lax.fori_loop(0, nc, adjoint, (zero, zero))

        def conv_back(ci, carry):
            t0 = pl.multiple_of(ci * tc, tc)
            dw = dcpad_ref[pl.ds(t0, tc + SUBLANE), :]
            xw = xpad_ref[pl.ds(t0, tc + SUBLANE), :]
            d_xc = dw[:tc, :]
            dxr = jnp.zeros((tc, cb), F32)
            for j in range(CONV_WIDTH):
                sh = CONV_WIDTH - 1 - j
                dsh = dw if sh == 0 else pltpu.roll(dw, tc + SUBLANE - sh, 0)
                dxr = dxr + dsh[:tc, :] * cw_ref[j:j + 1, :]
                xs = xw if sh == 0 else pltpu.roll(xw, sh, 0)
                dcw_ref[j:j + 1, :] += jnp.sum(d_xc * xs[SUBLANE:, :], axis=0, keepdims=True)
            dx_ref[pl.ds(t0, tc), :] = dxr.astype(BF16)
            return carry

        lax.fori_loop(0, nc, conv_back, 0)

    seq = lambda ni, bi: (bi, 0, ni)
    seqspec = pl.BlockSpec((None, s, cb), seq)
    cwspec = pl.BlockSpec((CONV_WIDTH, cb), lambda ni, bi: (0, ni))
    vec = pl.BlockSpec((1, cb), lambda ni, bi: (0, ni))
    wblk = pl.BlockSpec((None, cb, cb), lambda ni, bi: (ni, 0, 0))
    w = nblk * cb
    return pl.pallas_call(
        body, name=name, grid=(nblk, b),
        out_shape=(jax.ShapeDtypeStruct((b, s, w), BF16), jax.ShapeDtypeStruct((CONV_WIDTH, w), F32),
                   jax.ShapeDtypeStruct((1, w), F32), jax.ShapeDtypeStruct((nblk, cb, cb), F32),
                   jax.ShapeDtypeStruct((1, w), F32), jax.ShapeDtypeStruct((nblk, cb, cb), F32),
                   jax.ShapeDtypeStruct((1, w), F32), jax.ShapeDtypeStruct((1, w), F32)),
        in_specs=[seqspec, seqspec, seqspec, cwspec, vec, wblk, vec, wblk, vec, vec],
        out_specs=(seqspec, cwspec, vec, wblk, vec, wblk, vec, vec),
        scratch_shapes=[pltpu.VMEM((s + SUBLANE, cb), F32)] * 3 + [pltpu.VMEM((s, cb), F32)] * 4,
        compiler_params=_cp(("parallel", "arbitrary")),
    )(proj, hs, dhs, cw, cb_, wa, ba, wx, bx, lam)


def _adamw(w, g, m, v, name):
    shape = w.shape
    total = int(np.prod(shape))
    if w.ndim >= 2 and shape[-2] % SUBLANE == 0:
        cols = shape[-1]
    else:
        cols = 1024
    rows = -(-total // cols)
    rows = -(-rows // SUBLANE) * SUBLANE
    tr = _row_tile(rows, 512)
    pad = rows * cols - total

    def flat(a):
        if pad:
            a = jnp.pad(a.reshape(-1), (0, pad))
        return a.reshape(rows, cols)

    c1 = 1.0 - ADAM_B1 ** ADAM_STEP
    c2 = 1.0 - ADAM_B2 ** ADAM_STEP

    def body(w_ref, g_ref, m_ref, v_ref, d_ref, nm_ref, nv_ref):
        gv = g_ref[...]
        nm = ADAM_B1 * m_ref[...] + (1.0 - ADAM_B1) * gv
        nv = ADAM_B2 * v_ref[...] + (1.0 - ADAM_B2) * (gv * gv)
        nm_ref[...] = nm
        nv_ref[...] = nv
        d_ref[...] = -ADAM_LR * ((nm / c1) / (jnp.sqrt(nv / c2) + ADAM_EPS) + ADAM_WD * w_ref[...])

    spec = pl.BlockSpec((tr, cols), lambda i: (i, 0))
    shp = jax.ShapeDtypeStruct((rows, cols), F32)
    outs = pl.pallas_call(
        body, name=name, grid=(rows // tr,), out_shape=(shp, shp, shp),
        in_specs=[spec] * 4, out_specs=(spec,) * 3,
        compiler_params=_cp(("parallel",)),
    )(flat(w), flat(g), flat(m), flat(v))
    if pad:
        return tuple(o.reshape(-1)[:total].reshape(shape) for o in outs)
    return tuple(o.reshape(shape) for o in outs)


def _to_heads(t, nh):
    b, s, _ = t.shape
    return t.reshape(b, s, nh, HEAD_DIM).transpose(0, 2, 1, 3)


def _stack_heads(t):
    b, s, _ = t.shape
    t = t.reshape(b, s // BLK, BLK, A_KV_HEADS, A_GROUP, HEAD_DIM).transpose(0, 3, 1, 4, 2, 5)
    return t.reshape(b, A_KV_HEADS, s // BLK, A_GROUP * BLK, HEAD_DIM)


def _unstack_heads(t):
    b, hkv, nb, rows, hd = t.shape
    t = t.reshape(b, hkv, nb, A_GROUP, BLK, hd).transpose(0, 2, 4, 1, 3, 5)
    return t.reshape(b, nb * BLK, hkv * A_GROUP * hd)


def _from_heads(t):
    b, nh, s, hd = t.shape
    return t.transpose(0, 2, 1, 3).reshape(b, s, nh * hd)


def _pad_rows(a, mult):
    r = a.shape[0]
    p = (-r) % mult
    return jnp.pad(a, ((0, p), (0, 0))) if p else a


def kernel(x, c, rel_bias, norm_g, ada_w, ada_b, attn_w_in, attn_sinks, attn_b_f, attn_w_out, lru_w_in, lru_conv_w, lru_conv_b, lru_w_a, lru_b_a, lru_w_x, lru_b_x, lru_lambda, lru_w_out, final_g, loss_target, m_rel_bias, m_norm_g, m_ada_w, m_ada_b, m_attn_w_in, m_attn_sinks, m_attn_b_f, m_attn_w_out, m_lru_w_in, m_lru_conv_w, m_lru_conv_b, m_lru_w_a, m_lru_b_a, m_lru_w_x, m_lru_b_x, m_lru_lambda, m_lru_w_out, m_final_g, v_rel_bias, v_norm_g, v_ada_w, v_ada_b, v_attn_w_in, v_attn_sinks, v_attn_b_f, v_attn_w_out, v_lru_w_in, v_lru_conv_w, v_lru_conv_b, v_lru_w_a, v_lru_b_a, v_lru_w_x, v_lru_b_x, v_lru_lambda, v_lru_w_out, v_final_g):
    bl, s, d = x.shape
    ix, iy, ic = lax.axis_index("x"), lax.axis_index("y"), lax.axis_index("c")
    chip = 2 * ix + iy
    me = 2 * chip + ic
    nb = s // BLK
    aw = A_Q_HEADS * HEAD_DIM
    akv = A_KV_HEADS * HEAD_DIM
    bw = B_HEADS * HEAD_DIM
    mixw = aw + bw
    qkv_w = aw + 2 * akv + 3 * bw
    n_in = attn_w_in.shape[2] * N_CHIP
    lw = lru_lambda.shape[1] * N_CHIP
    n0 = mixw + qkv_w + LANE

    rows_pad = -(-bl // SUBLANE) * SUBLANE
    c_all = _all_gather8(_pad_rows(c, SUBLANE), "gather_c", pltpu.VMEM)
    c_all = c_all.reshape(N_DEV, rows_pad, d)[:, :bl].reshape(N_DEV * bl, d)
    ncol = ada_w.shape[2]
    ada_w_l = lax.dynamic_index_in_dim(ada_w, ic, 0, keepdims=False)
    ada_b_l = lax.dynamic_slice(ada_b, (ic, chip * ncol), (1, ncol))
    mod_part = _ada_fwd(c_all, ada_w_l, ada_b_l, "ada_fwd")
    mod_all = _all_gather8(_pad_rows(mod_part, SUBLANE), "gather_mod", pltpu.VMEM)
    mrows = -(-(N_DEV * bl) // SUBLANE) * SUBLANE
    mod_all = mod_all.reshape(N_CHIP, 2, mrows, ncol)[:, :, :N_DEV * bl]
    mod_all = mod_all.transpose(1, 2, 0, 3).reshape(2, N_DEV * bl, N_CHIP * ncol)
    mod = lax.dynamic_slice_in_dim(mod_all, me * bl, bl, axis=1)
    shift = [mod[l, :, 0:d].reshape(bl, 1, d) for l in range(2)]
    scale = [mod[l, :, d:2 * d].reshape(bl, 1, d) for l in range(2)]
    gmod = [mod[l, :, 2 * d:3 * d].reshape(bl, 1, d) for l in range(2)]

    c_in0 = n_in // N_CHIP
    c_in1 = 2 * lw // N_CHIP
    assert c_in0 <= d and 2 * c_in1 == d
    r_in0, r_out0, r_in1, r_out1 = d // 2, mixw // N_CHIP // 2, d // 4, lw // N_CHIP // 2
    o_out0, o_in1, o_out1 = r_in0, r_in0 + r_out0, r_in0 + r_out0 + r_in1
    big_rows = o_out1 + r_out1

    def half_of(a, rows):
        return lax.dynamic_slice_in_dim(a, ic * rows, rows, axis=0)

    h_in1 = half_of(lru_w_in[0], r_in0).astype(BF16)
    my_half = jnp.concatenate([
        jnp.pad(half_of(attn_w_in[0], r_in0).astype(BF16), ((0, 0), (0, d - c_in0))),
        half_of(attn_w_out[0], r_out0).astype(BF16),
        jnp.concatenate([h_in1[:r_in1], h_in1[r_in1:]], axis=1),
        half_of(lru_w_out[0], r_out1).astype(BF16)], axis=0)
    gat = _all_gather8(my_half, "gather_weights", pltpu.HBM).reshape(N_CHIP, 2, big_rows, d)
    w_in0 = gat[:, :, :r_in0, :c_in0].transpose(1, 2, 0, 3).reshape(d, n_in)
    w_out0 = gat[:, :, o_out0:o_in1].reshape(mixw, d)
    w_in1 = gat[:, :, o_in1:o_out1].reshape(N_CHIP, 2, r_in1, 2, c_in1)
    w_in1 = w_in1.transpose(1, 3, 2, 0, 4).reshape(d, 2 * lw)
    w_out1 = gat[:, :, o_out1:].reshape(lw, d)
    w_cat0 = jnp.concatenate([w_in0[:, qkv_w + B_HEADS:], w_in0[:, :qkv_w + B_HEADS],
                              jnp.zeros((d, n0 - n_in), BF16)], axis=1)

    proj0, h0, zf = _norm_proj(x, norm_g[0:1], scale[0], shift[0], w_cat0, LANE, "norm_proj0")
    o_a = mixw
    aq = _stack_heads(proj0[:, :, o_a:o_a + aw].astype(BF16))
    ak = _to_heads(proj0[:, :, o_a + aw:o_a + aw + akv].astype(BF16), A_KV_HEADS)
    av = _to_heads(proj0[:, :, o_a + aw + akv:o_a + aw + 2 * akv].astype(BF16), A_KV_HEADS)
    o_b = o_a + aw + 2 * akv
    fox_blks = (o_b // LANE, (o_b + bw) // LANE, (o_b + 2 * bw) // LANE)
    bucket_np, valid_np = _rel_buckets()
    bucket = jnp.asarray(bucket_np)
    bias = _swa_bias(rel_bias.T, bucket, jnp.asarray(valid_np), "swa_bias")
    bias = bias.reshape(A_KV_HEADS, A_GROUP * BLK, 2 * BLK)
    sinks = jnp.repeat(attn_sinks[0].reshape(A_KV_HEADS, A_GROUP), BLK, axis=1).reshape(A_KV_HEADS, A_GROUP * BLK, 1)
    a_out, a_lse = _swa_fwd(aq, ak, av, bias, sinks, "swa_fwd")
    bf_pad = jnp.pad(attn_b_f, ((0, 0), (0, LANE - B_HEADS)))
    fsum = _fox_decay(zf, bf_pad, "fox_decay")
    fh = fsum[:, :, :B_HEADS].transpose(0, 2, 1)
    fcol = fh.reshape(bl, B_HEADS, s, 1)
    frow = fh.reshape(bl, B_HEADS, 1, s)
    frowb = fh.reshape(bl, B_HEADS, nb, 1, BLK)
    b_out, b_lse = _fox_fwd(proj0, *fox_blks, fcol, frow, "fox_fwd")
    mix0 = [_unstack_heads(a_out), b_out]
    x1, o0 = _gate_outproj(mix0, proj0, 0, w_out0, x, gmod[0], "gate_outproj0")

    proj1, h1 = _norm_proj(x1, norm_g[1:2], scale[1], shift[1], w_in1, 0, "norm_proj1")
    vec_rows = jnp.concatenate([lru_conv_w[0], lru_conv_b, lru_b_a, lru_b_x, lru_lambda], axis=0)
    vec_all = _all_gather8(vec_rows, "gather_lru_vectors", pltpu.VMEM)
    vec_all = vec_all.reshape(N_CHIP, 2, SUBLANE, lw // N_CHIP)[:, 0]
    vec_all = vec_all.transpose(1, 0, 2).reshape(SUBLANE, lw)
    cw_f, cb_f, ba_f, bx_f, lam_f = vec_all[0:4], vec_all[4:5], vec_all[5:6], vec_all[6:7], vec_all[7:8]
    hs = _lru_fwd(proj1, cw_f, cb_f, lru_w_a[0], ba_f, lru_w_x[0], bx_f, lam_f, "lru_fwd")
    x2, o1 = _gate_outproj([hs], proj1, 1, w_out1, x1, gmod[1], "gate_outproj1")

    loss_vec, dx2, g_final = _final_loss(x2, final_g.reshape(1, d), loss_target, "final_loss")
    loss = lax.psum(loss_vec[0, 0], ("x", "y", "c"))

    dhs, dgate1, do1, y1, dgm1 = _bwd_out(dx2, gmod[1], o1, [hs], proj1, 1, w_out1.T, "bwd_out1")
    g_w_out1 = _matmul_tn(y1, [do1], "grad_w_out1")
    (dxr, g_cw, g_cb, g_wa, g_ba, g_wx, g_bx, g_lam) = _lru_bwd(
        proj1, hs, dhs, cw_f, cb_f, lru_w_a[0], ba_f, lru_w_x[0], bx_f, lam_f, "lru_bwd")
    dproj1 = [dxr, dgate1]
    g_w_in1 = _matmul_tn(h1, dproj1, "grad_w_in1")
    dx1, dsh1, dsc1, g_ng1 = _bwd_in(dproj1, w_in1.T, x1, norm_g[1:2], scale[1], dx2, "bwd_in1")

    dmix0, dgate0, do0, y0, dgm0 = _bwd_out(dx1, gmod[0], o0, mix0, proj0, 0, w_out0.T, "bwd_out0")
    g_w_out0 = _matmul_tn(y0, [do0], "grad_w_out0")
    da_out = _stack_heads(dmix0[:, :, :aw].astype(BF16))
    daq, dak, dav, dbias, dsink = _swa_bwd(aq, ak, av, bias, sinks, da_out, a_lse, "swa_bwd")
    dbq, dbk, dbv, dfrow = _fox_bwd(proj0, *fox_blks, dmix0, aw // LANE, fcol, frow, frowb, b_lse,
                                    b_lse.reshape(bl, B_HEADS, nb, 1, BLK), "fox_bwd")
    df = dfrow.reshape(bl, B_HEADS, s).transpose(0, 2, 1)
    df = jnp.pad(df, ((0, 0), (0, 0), (0, LANE - B_HEADS)))
    dzf, g_bf = _fox_dgate(df, zf, bf_pad, B_HEADS, "fox_dgate")
    dproj0 = ([dgate0, _unstack_heads(daq), _from_heads(dak), _from_heads(dav)]
              + [dbq, dbk, dbv, dzf.astype(BF16)])
    g_w_cat0 = _matmul_tn(h0, dproj0, "grad_w_in0")
    g_w_in0 = jnp.concatenate([g_w_cat0[:, mixw:mixw + qkv_w + B_HEADS], g_w_cat0[:, :mixw]], axis=1)
    dx0, dsh0, dsc0, g_ng0 = _bwd_in(dproj0, w_cat0.T, x, norm_g[0:1], scale[0], dx1, "bwd_in0")
    g_relb, g_sink = _swa_small_grads(dbias.reshape(bl, A_Q_HEADS, BLK, 2 * BLK),
                                      dsink.reshape(bl, A_Q_HEADS, 1, LANE), bucket, "swa_small_grads")

    dmod = jnp.concatenate([jnp.concatenate([dsh0, dsc0, dgm0], axis=-1),
                            jnp.concatenate([dsh1, dsc1, dgm1], axis=-1)], axis=1)
    dmod_all = _all_gather8(_pad_rows(dmod.reshape(bl, 6 * d), SUBLANE), "gather_dmod", pltpu.VMEM)
    dmod_all = dmod_all.reshape(N_DEV, rows_pad, 6 * d)[:, :bl].reshape(N_DEV * bl, 6 * d)
    dmod_chip = lax.dynamic_slice_in_dim(dmod_all.reshape(N_DEV * bl, 2, 3 * d), chip * ncol, ncol, axis=2)
    g_ada_w, g_ada_b = _ada_bwd(c_all, dmod_chip.transpose(1, 0, 2), dmod_all, "ada_bwd")
    g_ada_b = g_ada_b.reshape(2, 3 * d)

    small_parts = [g_relb[:, :, 0].T, jnp.concatenate([g_ng0, g_ng1], axis=0), g_sink[:, 0, 0], g_bf[0, :B_HEADS],
                   g_wa, g_wx, g_final, g_cw, g_cb, g_ba, g_bx, g_lam]
    small_sizes = [int(np.prod(p.shape)) for p in small_parts]
    small_total = sum(small_sizes)
    piece_rows = -(-(-(-small_total // N_DEV) // 1024) // SUBLANE) * SUBLANE
    small_flat = jnp.concatenate([p.reshape(-1) for p in small_parts])
    small_flat = jnp.pad(small_flat, (0, N_DEV * piece_rows * 1024 - small_total))
    small_pieces = small_flat.reshape(N_CHIP, 2, piece_rows, 1024)
    p_in0 = jnp.pad(g_w_in0.reshape(2, r_in0, N_CHIP, c_in0).transpose(2, 0, 1, 3),
                    ((0, 0), (0, 0), (0, 0), (0, d - c_in0)))
    p_in1 = g_w_in1.reshape(2, 2, r_in1, N_CHIP, c_in1).transpose(3, 0, 2, 1, 4).reshape(N_CHIP, 2, r_in1, d)
    pieces = jnp.concatenate([p_in0, g_w_out0.reshape(N_CHIP, 2, r_out0, d), p_in1,
                              g_w_out1.reshape(N_CHIP, 2, r_out1, d), small_pieces], axis=2)
    theirs = _sibling_push(pieces, True, "push_sibling_halves")
    partial = _pair_sum(jnp.reshape(ic, (1,)).astype(jnp.int32), pieces, theirs, "sum_chip")
    slots = _chip_all_to_all(partial, "exchange_grads")
    reduced = _sum_slots(slots, "sum_grads")
    mine_big = reduced[:big_rows]
    other_big = _sibling_push(mine_big[None], False, "swap_halves")[0]
    both = jnp.stack([jnp.where(ic == 0, mine_big, other_big), jnp.where(ic == 0, other_big, mine_big)])
    g_big = [both[:, :r_in0, :c_in0].reshape(d, c_in0),
             both[:, o_out0:o_in1].reshape(2 * r_out0, d),
             both[:, o_in1:o_out1].reshape(2, r_in1, 2, c_in1).transpose(0, 2, 1, 3).reshape(d, c_in1),
             both[:, o_out1:].reshape(2 * r_out1, d)]
    small_all = _all_gather8(reduced[big_rows:], "gather_small_grads", pltpu.VMEM).reshape(-1)
    g_small, off = [], 0
    for p, n in zip(small_parts, small_sizes):
        g_small.append(small_all[off:off + n].reshape(p.shape))
        off += n
    (g_rel_bias, g_norm_g, g_sinks, g_b_f, g_w_a, g_w_x, g_fin, g_cw_r, g_cb_r, g_ba_r, g_bx_r, g_lam_r) = g_small
    cw4 = lw // N_CHIP

    def my_cols(a):
        return lax.dynamic_slice_in_dim(a, chip * cw4, cw4, axis=1)

    grads = {
        "rel_bias": g_rel_bias, "norm_g": g_norm_g, "ada_w": g_ada_w, "ada_b": g_ada_b,
        "attn_w_in": g_big[0][None], "attn_sinks": g_sinks[None], "attn_b_f": g_b_f[None],
        "attn_w_out": g_big[1][None], "lru_w_in": g_big[2][None], "lru_conv_w": my_cols(g_cw_r)[None],
        "lru_conv_b": my_cols(g_cb_r), "lru_w_a": g_w_a[None], "lru_b_a": my_cols(g_ba_r),
        "lru_w_x": g_w_x[None], "lru_b_x": my_cols(g_bx_r), "lru_lambda": my_cols(g_lam_r),
        "lru_w_out": g_big[3][None], "final_g": g_fin.reshape(d),
    }
    weights = dict(rel_bias=rel_bias, norm_g=norm_g, ada_w=ada_w, ada_b=ada_b, attn_w_in=attn_w_in,
                   attn_sinks=attn_sinks, attn_b_f=attn_b_f, attn_w_out=attn_w_out, lru_w_in=lru_w_in,
                   lru_conv_w=lru_conv_w, lru_conv_b=lru_conv_b, lru_w_a=lru_w_a, lru_b_a=lru_b_a,
                   lru_w_x=lru_w_x, lru_b_x=lru_b_x, lru_lambda=lru_lambda, lru_w_out=lru_w_out, final_g=final_g)
    moms = dict(rel_bias=(m_rel_bias, v_rel_bias), norm_g=(m_norm_g, v_norm_g), ada_w=(m_ada_w, v_ada_w),
                ada_b=(m_ada_b, v_ada_b), attn_w_in=(m_attn_w_in, v_attn_w_in),
                attn_sinks=(m_attn_sinks, v_attn_sinks), attn_b_f=(m_attn_b_f, v_attn_b_f),
                attn_w_out=(m_attn_w_out, v_attn_w_out), lru_w_in=(m_lru_w_in, v_lru_w_in),
                lru_conv_w=(m_lru_conv_w, v_lru_conv_w), lru_conv_b=(m_lru_conv_b, v_lru_conv_b),
                lru_w_a=(m_lru_w_a, v_lru_w_a), lru_b_a=(m_lru_b_a, v_lru_b_a), lru_w_x=(m_lru_w_x, v_lru_w_x),
                lru_b_x=(m_lru_b_x, v_lru_b_x), lru_lambda=(m_lru_lambda, v_lru_lambda),
                lru_w_out=(m_lru_w_out, v_lru_w_out), final_g=(m_final_g, v_final_g))
    names = list(weights)
    big_names = [n for n in names if weights[n].size >= 65536]
    small_names = [n for n in names if weights[n].size < 65536]
    delta, new_m, new_v = {}, {}, {}
    for n in big_names:
        delta[n], new_m[n], new_v[n] = _adamw(weights[n], grads[n].reshape(weights[n].shape),
                                              moms[n][0], moms[n][1], "adamw_" + n)
    cat = lambda arrs: jnp.concatenate([a.reshape(-1) for a in arrs])
    sd, sm, sv = _adamw(cat([weights[n] for n in small_names]), cat([grads[n] for n in small_names]),
                        cat([moms[n][0] for n in small_names]), cat([moms[n][1] for n in small_names]),
                        "adamw_small")
    off = 0
    for n in small_names:
        sz = weights[n].size
        shp = weights[n].shape
        delta[n], new_m[n], new_v[n] = (sd[off:off + sz].reshape(shp), sm[off:off + sz].reshape(shp),
                                        sv[off:off + sz].reshape(shp))
        off += sz
    out_grads = [grads[n].reshape(weights[n].shape) for n in names]
    return (loss, dx0, *out_grads, *[delta[n] for n in names], *[new_m[n] for n in names],
            *[new_v[n] for n in names])
```

```python
import functools
import math

import numpy as np
import jax
import jax.numpy as jnp
from jax import lax
from jax.experimental import pallas as pl
from jax.experimental.pallas import tpu as pltpu

F32 = jnp.float32
BF16 = jnp.bfloat16
MESH = pl.DeviceIdType.MESH

N_DEV = 8
N_CHIP = 4
HEAD_DIM = 64
BLK = 128
A_Q_HEADS = 8
A_KV_HEADS = 2
A_GROUP = A_Q_HEADS // A_KV_HEADS
B_HEADS = 8
REL_BUCKETS = 32
REL_MAX_EXACT = 16
REL_MAX_DIST = 128
LRU_BLOCKS = 8
LRU_C = 8.0
CONV_WIDTH = 4
EPS = 1e-6
NEG = -1e30
SCALE = HEAD_DIM ** -0.5
LANE = 128
SUBLANE = 8
VMEM_LIMIT = 56 * 1024 * 1024
SCAN_CHUNK = 256
FOX_BQ = 512
ADAM_LR = 0.001
ADAM_B1 = 0.9
ADAM_B2 = 0.999
ADAM_EPS = 1e-08
ADAM_WD = 0.01
ADAM_STEP = 10
HI = lax.Precision.HIGHEST


def _cp(sem=None):
    return pltpu.CompilerParams(dimension_semantics=sem, vmem_limit_bytes=VMEM_LIMIT)


def _dot(a, b):
    return jnp.dot(a, b, preferred_element_type=F32)


def _dot_nt(a, b):
    return lax.dot_general(a, b, (((1,), (1,)), ((), ())), preferred_element_type=F32)


def _dot_tn(a, b):
    return lax.dot_general(a, b, (((0,), (0,)), ((), ())), preferred_element_type=F32)


def _sigmoid(z):
    return 1.0 / (1.0 + jnp.exp(-z))


def _row_tile(rows, cap):
    if rows <= cap:
        return rows
    best = SUBLANE
    t = SUBLANE
    while t <= cap:
        if rows % t == 0:
            best = t
        t += SUBLANE
    return best


def _all_gather8(x_shard, name, space):
    m_per, n = x_shard.shape
    n_own = 8 if (space == pltpu.HBM and m_per % 128 == 0) else 1
    own_rows = m_per // n_own

    def body(x_ref, out_ref, send_sems, recv_sems, local_sems):
        x, y, c = lax.axis_index("x"), lax.axis_index("y"), lax.axis_index("c")
        me, sibling = (x, y, c), (x, y, 1 - c)
        chips = [(1 - x, y), (x, 1 - y), (1 - x, 1 - y)]

        def rows(px, py, pc):
            return out_ref.at[pl.ds((4 * px + 2 * py + pc) * m_per, m_per), :]

        def copy(k, block, to, src=None):
            return pltpu.make_async_remote_copy(
                src_ref=rows(*block) if src is None else src, dst_ref=rows(*block),
                send_sem=send_sems.at[k], recv_sem=recv_sems.at[k], device_id=to, device_id_type=MESH)

        base = (4 * x + 2 * y + c) * m_per
        mine = [pltpu.make_async_copy(x_ref.at[pl.ds(i * own_rows, own_rows), :],
                                      out_ref.at[pl.ds(base + i * own_rows, own_rows), :], local_sems.at[i])
                for i in range(n_own)]
        for cp in mine:
            cp.start()
        first = [copy(0, me, sibling, src=x_ref)]
        first += [copy(1 + j, me, (*chip, c), src=x_ref) for j, chip in enumerate(chips)]
        for cp in first:
            cp.start()
        passed = [copy(4 + j, (*chip, c), sibling) for j, chip in enumerate(chips)]
        for j, chip in enumerate(chips):
            copy(1 + j, (*chip, c), me).wait_recv()
            passed[j].start()
        copy(0, sibling, me).wait_recv()
        for j, chip in enumerate(chips):
            copy(4 + j, (*chip, 1 - c), me).wait_recv()
        for cp in first + passed:
            cp.wait_send()
        for cp in mine:
            cp.wait()

    return pl.pallas_call(
        body, name=name,
        out_shape=jax.ShapeDtypeStruct((N_DEV * m_per, n), x_shard.dtype),
        in_specs=[pl.BlockSpec(memory_space=space)],
        out_specs=pl.BlockSpec(memory_space=space),
        scratch_shapes=[pltpu.SemaphoreType.DMA((7,)), pltpu.SemaphoreType.DMA((7,)),
                        pltpu.SemaphoreType.DMA((n_own,))],
        compiler_params=pltpu.CompilerParams(vmem_limit_bytes=VMEM_LIMIT),
    )(x_shard)


def _sibling_push(blocks, pick_other, name):
    nblk = blocks.shape[0]
    m, n = blocks.shape[-2:]

    def body(x_ref, out_ref, send_sems, recv_sems):
        x, y, c = lax.axis_index("x"), lax.axis_index("y"), lax.axis_index("c")
        copies = []
        for k in range(nblk):
            src = x_ref.at[k, 1 - c] if pick_other else x_ref.at[k]
            copies.append(pltpu.make_async_remote_copy(
                src_ref=src, dst_ref=out_ref.at[k], send_sem=send_sems.at[k], recv_sem=recv_sems.at[k],
                device_id=(x, y, 1 - c), device_id_type=MESH))
        for cp in copies:
            cp.start()
        for cp in copies:
            cp.wait_recv()
        for cp in copies:
            cp.wait_send()

    hbm = pl.BlockSpec(memory_space=pltpu.HBM)
    return pl.pallas_call(
        body, name=name,
        out_shape=jax.ShapeDtypeStruct((nblk, m, n), blocks.dtype),
        in_specs=[hbm], out_specs=hbm,
        scratch_shapes=[pltpu.SemaphoreType.DMA((nblk,)), pltpu.SemaphoreType.DMA((nblk,))],
    )(blocks)


def _chip_all_to_all(parts, name):
    _, m, n = parts.shape

    def body(x_ref, out_ref, send_sems, recv_sems, local_sem):
        x, y, c = lax.axis_index("x"), lax.axis_index("y"), lax.axis_index("c")
        me = 2 * x + y
        mine = pltpu.make_async_copy(x_ref.at[me], out_ref.at[me], local_sem)
        mine.start()
        copies = []
        for k in range(1, N_CHIP):
            px, py = x ^ ((k >> 1) & 1), y ^ (k & 1)
            copies.append(pltpu.make_async_remote_copy(
                src_ref=x_ref.at[2 * px + py], dst_ref=out_ref.at[me],
                send_sem=send_sems.at[k - 1], recv_sem=recv_sems.at[k - 1],
                device_id=(px, py, c), device_id_type=MESH))
        for cp in copies:
            cp.start()
        for cp in copies:
            cp.wait_recv()
        for cp in copies:
            cp.wait_send()
        mine.wait()

    hbm = pl.BlockSpec(memory_space=pltpu.HBM)
    return pl.pallas_call(
        body, name=name,
        out_shape=jax.ShapeDtypeStruct(parts.shape, parts.dtype),
        in_specs=[hbm], out_specs=hbm,
        scratch_shapes=[pltpu.SemaphoreType.DMA((N_CHIP - 1,)), pltpu.SemaphoreType.DMA((N_CHIP - 1,)),
                        pltpu.SemaphoreType.DMA],
    )(parts)


def _pair_sum(core, pieces, theirs, name):
    nblk, _, m, n = pieces.shape
    tr = _row_tile(m, 536)

    def body(c_ref, p_ref, t_ref, o_ref):
        o_ref[...] = (p_ref[...] + t_ref[...]).astype(BF16)

    return pl.pallas_call(
        body, name=name,
        grid_spec=pltpu.PrefetchScalarGridSpec(
            num_scalar_prefetch=1, grid=(nblk, m // tr),
            in_specs=[pl.BlockSpec((None, None, tr, n), lambda k, i, c_ref: (k, c_ref[0], i, 0)),
                      pl.BlockSpec((None, tr, n), lambda k, i, c_ref: (k, i, 0))],
            out_specs=pl.BlockSpec((None, tr, n), lambda k, i, c_ref: (k, i, 0))),
        out_shape=jax.ShapeDtypeStruct((nblk, m, n), BF16),
        compiler_params=_cp(("parallel", "parallel")),
    )(core, pieces, theirs)


def _sum_slots(slots, name):
    k, m, n = slots.shape
    tr = _row_tile(m, 536)

    def body(s_ref, o_ref):
        acc = s_ref[0].astype(F32)
        for j in range(1, k):
            acc = acc + s_ref[j].astype(F32)
        o_ref[...] = acc

    return pl.pallas_call(
        body, name=name, grid=(m // tr,),
        out_shape=jax.ShapeDtypeStruct((m, n), F32),
        in_specs=[pl.BlockSpec((k, tr, n), lambda i: (0, i, 0))],
        out_specs=pl.BlockSpec((tr, n), lambda i: (i, 0)),
        compiler_params=_cp(("parallel",)),
    )(slots)


def _ada_fwd(c_all, w, b, name):
    r, _ = c_all.shape
    n = w.shape[1]

    def body(c_ref, w_ref, b_ref, o_ref):
        cv = c_ref[...]
        act = cv * _sigmoid(cv)
        o_ref[...] = jnp.dot(act, w_ref[...], precision=HI, preferred_element_type=F32) + b_ref[...]

    return pl.pallas_call(body, name=name, out_shape=jax.ShapeDtypeStruct((r, n), F32),
                          compiler_params=_cp())(c_all, w, b)


def _ada_bwd(c_all, dmod_chip, dmod_all, name):
    r, d = c_all.shape
    nl, _, n = dmod_chip.shape

    def body(c_ref, dm_ref, da_ref, gw_ref, gb_ref):
        cv = c_ref[...]
        act = cv * _sigmoid(cv)
        for l in range(nl):
            gw_ref[l] = lax.dot_general(act, dm_ref[l], (((0,), (0,)), ((), ())), precision=HI,
                                        preferred_element_type=F32)
        gb_ref[...] = jnp.sum(da_ref[...], axis=0, keepdims=True)

    return pl.pallas_call(
        body, name=name,
        out_shape=(jax.ShapeDtypeStruct((nl, d, n), F32), jax.ShapeDtypeStruct((1, dmod_all.shape[1]), F32)),
        compiler_params=_cp())(c_all, dmod_chip, dmod_all)


def _norm_proj(x, g, scale, shift, w, f32_cols, name):
    b, s, d = x.shape
    n = w.shape[1]
    tm = min(s, 256)

    def body(x_ref, g_ref, sc_ref, sh_ref, w_ref, proj_ref, h_ref, *aux_ref):
        xv = x_ref[...]
        rstd = lax.rsqrt(jnp.mean(xv * xv, axis=-1, keepdims=True) + EPS)
        h = (xv * rstd) * g_ref[...] * (1.0 + sc_ref[...]) + sh_ref[...]
        hb = h.astype(BF16)
        h_ref[...] = hb
        proj = _dot(hb, w_ref[...])
        proj_ref[...] = proj.astype(BF16)
        if f32_cols:
            aux_ref[0][...] = proj[:, n - f32_cols:]

    row = lambda i, j: (i, j, 0)
    out_shape = [jax.ShapeDtypeStruct((b, s, n), BF16), jax.ShapeDtypeStruct((b, s, d), BF16)]
    out_specs = [pl.BlockSpec((None, tm, n), row), pl.BlockSpec((None, tm, d), row)]
    if f32_cols:
        out_shape.append(jax.ShapeDtypeStruct((b, s, f32_cols), F32))
        out_specs.append(pl.BlockSpec((None, tm, f32_cols), row))
    return pl.pallas_call(
        body, name=name, grid=(b, s // tm),
        out_shape=tuple(out_shape),
        in_specs=[pl.BlockSpec((None, tm, d), row),
                  pl.BlockSpec((1, d), lambda i, j: (0, 0)),
                  pl.BlockSpec((None, 1, d), lambda i, j: (i, 0, 0)),
                  pl.BlockSpec((None, 1, d), lambda i, j: (i, 0, 0)),
                  pl.BlockSpec((d, n), lambda i, j: (0, 0))],
        out_specs=tuple(out_specs),
        compiler_params=_cp(("parallel", "parallel")),
    )(x, g, scale, shift, w)


def _cat_refs(refs):
    vals = [r[...] for r in refs]
    return vals[0] if len(vals) == 1 else jnp.concatenate(vals, axis=-1)


def _gate_outproj(mix_parts, proj, gate_blk, w_out, x, gmod, name):
    b, s, _ = x.shape
    wd, d = w_out.shape
    tm = min(s, 256)
    npart = len(mix_parts)

    def body(*refs):
        mix_refs = refs[:npart]
        gate_ref, w_ref, x_ref, gm_ref, xo_ref, o_ref = refs[npart:]
        gt = gate_ref[...].astype(F32)
        y = (_cat_refs(mix_refs) * (gt * _sigmoid(gt))).astype(BF16)
        o = _dot(y, w_ref[...])
        o_ref[...] = o.astype(BF16)
        xo_ref[...] = x_ref[...] + gm_ref[...] * o

    return pl.pallas_call(
        body, name=name, grid=(b, s // tm),
        out_shape=(jax.ShapeDtypeStruct((b, s, d), F32), jax.ShapeDtypeStruct((b, s, d), BF16)),
        in_specs=[pl.BlockSpec((None, tm, p.shape[2]), lambda i, j: (i, j, 0)) for p in mix_parts] + [
                  pl.BlockSpec((None, tm, wd), lambda i, j: (i, j, gate_blk)),
                  pl.BlockSpec((wd, d), lambda i, j: (0, 0)),
                  pl.BlockSpec((None, tm, d), lambda i, j: (i, j, 0)),
                  pl.BlockSpec((None, 1, d), lambda i, j: (i, 0, 0))],
        out_specs=(pl.BlockSpec((None, tm, d), lambda i, j: (i, j, 0)),
                   pl.BlockSpec((None, tm, d), lambda i, j: (i, j, 0))),
        compiler_params=_cp(("parallel", "parallel")),
    )(*mix_parts, proj, w_out, x, gmod)


def _final_loss(x, g, target, name):
    b, s, d = x.shape
    tm = min(s, 256)

    def body(x_ref, g_ref, t_ref, loss_ref, dx_ref, dg_ref):
        first = jnp.logical_and(pl.program_id(0) == 0, pl.program_id(1) == 0)

        @pl.when(first)
        def _():
            loss_ref[...] = jnp.zeros_like(loss_ref)
            dg_ref[...] = jnp.zeros_like(dg_ref)

        xv = x_ref[...]
        gv = g_ref[...]
        rstd = lax.rsqrt(jnp.mean(xv * xv, axis=-1, keepdims=True) + EPS)
        xhat = xv * rstd
        err = xhat * gv - t_ref[...]
        row = jnp.mean(err * err, axis=-1, keepdims=True)
        loss_ref[...] += 0.5 * jnp.sum(row, axis=0, keepdims=True)
        dy = err * (1.0 / d)
        dg_ref[...] += jnp.sum(dy * xhat, axis=0, keepdims=True)
        dxh = dy * gv
        dx_ref[...] = rstd * (dxh - xhat * jnp.mean(dxh * xhat, axis=-1, keepdims=True))

    return pl.pallas_call(
        body, name=name, grid=(b, s // tm),
        out_shape=(jax.ShapeDtypeStruct((1, LANE), F32), jax.ShapeDtypeStruct((b, s, d), F32),
                   jax.ShapeDtypeStruct((1, d), F32)),
        in_specs=[pl.BlockSpec((None, tm, d), lambda i, j: (i, j, 0)),
                  pl.BlockSpec((1, d), lambda i, j: (0, 0)),
                  pl.BlockSpec((None, tm, d), lambda i, j: (i, j, 0))],
        out_specs=(pl.BlockSpec((1, LANE), lambda i, j: (0, 0)),
                   pl.BlockSpec((None, tm, d), lambda i, j: (i, j, 0)),
                   pl.BlockSpec((1, d), lambda i, j: (0, 0))),
        compiler_params=_cp(("arbitrary", "arbitrary")),
    )(x, g, target)


def _bwd_out(dxo, gmod, o, mix_parts, proj, gate_blk, w_out_t, name):
    b, s, d = dxo.shape
    wd = w_out_t.shape[1]
    tm = min(s, 256)
    npart = len(mix_parts)

    def body(dx_ref, gm_ref, o_ref, *refs):
        mix_refs = refs[:npart]
        gate_ref, wt_ref, dmix_ref, dgate_ref, do_ref, y_ref, dgm_ref = refs[npart:]

        @pl.when(pl.program_id(1) == 0)
        def _():
            dgm_ref[...] = jnp.zeros_like(dgm_ref)

        dx = dx_ref[...]
        dgm_ref[...] += jnp.sum(dx * o_ref[...].astype(F32), axis=0, keepdims=True)
        dob = (gm_ref[...] * dx).astype(BF16)
        do_ref[...] = dob
        dy = _dot(dob, wt_ref[...])
        gt = gate_ref[...].astype(F32)
        sg = _sigmoid(gt)
        silu = gt * sg
        mx = _cat_refs(mix_refs)
        y_ref[...] = (mx * silu).astype(BF16)
        dmix_ref[...] = dy * silu
        dgate_ref[...] = (dy * mx * (sg * (1.0 + gt * (1.0 - sg)))).astype(BF16)

    row = lambda i, j: (i, j, 0)
    return pl.pallas_call(
        body, name=name, grid=(b, s // tm),
        out_shape=(jax.ShapeDtypeStruct((b, s, wd), F32), jax.ShapeDtypeStruct((b, s, wd), BF16),
                   jax.ShapeDtypeStruct((b, s, d), BF16), jax.ShapeDtypeStruct((b, s, wd), BF16),
                   jax.ShapeDtypeStruct((b, 1, d), F32)),
        in_specs=[pl.BlockSpec((None, tm, d), row),
                  pl.BlockSpec((None, 1, d), lambda i, j: (i, 0, 0)),
                  pl.BlockSpec((None, tm, d), row)] + [
                  pl.BlockSpec((None, tm, p.shape[2]), row) for p in mix_parts] + [
                  pl.BlockSpec((None, tm, wd), lambda i, j: (i, j, gate_blk)),
                  pl.BlockSpec((d, wd), lambda i, j: (0, 0))],
        out_specs=(pl.BlockSpec((None, tm, wd), row), pl.BlockSpec((None, tm, wd), row),
                   pl.BlockSpec((None, tm, d), row), pl.BlockSpec((None, tm, wd), row),
                   pl.BlockSpec((None, 1, d), lambda i, j: (i, 0, 0))),
        compiler_params=_cp(("parallel", "arbitrary")),
    )(dxo, gmod, o, *mix_parts, proj, w_out_t)


def _bwd_in(dproj_parts, w_in_t, x, g, scale, dxo, name):
    b, s, d = x.shape
    n = w_in_t.shape[0]
    tm = min(s, 256)
    npart = len(dproj_parts)

    def body(*refs):
        dp_refs = refs[:npart]
        wt_ref, x_ref, g_ref, sc_ref, dxo_ref, dx_ref, dsh_ref, dsc_ref, dg_ref = refs[npart:]

        @pl.when(jnp.logical_and(pl.program_id(0) == 0, pl.program_id(1) == 0))
        def _():
            dg_ref[...] = jnp.zeros_like(dg_ref)

        @pl.when(pl.program_id(1) == 0)
        def _():
            dsh_ref[...] = jnp.zeros_like(dsh_ref)
            dsc_ref[...] = jnp.zeros_like(dsc_ref)

        dh = _dot(_cat_refs(dp_refs), wt_ref[...])
        xv = x_ref[...]
        gv = g_ref[...]
        one_sc = 1.0 + sc_ref[...]
        rstd = lax.rsqrt(jnp.mean(xv * xv, axis=-1, keepdims=True) + EPS)
        xhat = xv * rstd
        dsh_ref[...] += jnp.sum(dh, axis=0, keepdims=True)
        dsc_ref[...] += jnp.sum(dh * (xhat * gv), axis=0, keepdims=True)
        dhs = dh * one_sc
        dg_ref[...] += jnp.sum(dhs * xhat, axis=0, keepdims=True)
        dxh = dhs * gv
        dx_ref[...] = dxo_ref[...] + rstd * (dxh - xhat * jnp.mean(dxh * xhat, axis=-1, keepdims=True))

    row = lambda i, j: (i, j, 0)
    per_b = lambda i, j: (i, 0, 0)
    return pl.pallas_call(
        body, name=name, grid=(b, s // tm),
        out_shape=(jax.ShapeDtypeStruct((b, s, d), F32), jax.ShapeDtypeStruct((b, 1, d), F32),
                   jax.ShapeDtypeStruct((b, 1, d), F32), jax.ShapeDtypeStruct((1, d), F32)),
        in_specs=[pl.BlockSpec((None, tm, p.shape[2]), row) for p in dproj_parts] + [
                  pl.BlockSpec((n, d), lambda i, j: (0, 0)),
                  pl.BlockSpec((None, tm, d), row),
                  pl.BlockSpec((1, d), lambda i, j: (0, 0)),
                  pl.BlockSpec((None, 1, d), per_b),
                  pl.BlockSpec((None, tm, d), row)],
        out_specs=(pl.BlockSpec((None, tm, d), row), pl.BlockSpec((None, 1, d), per_b),
                   pl.BlockSpec((None, 1, d), per_b), pl.BlockSpec((1, d), lambda i, j: (0, 0))),
        compiler_params=_cp(("arbitrary", "arbitrary")),
    )(*dproj_parts, w_in_t, x, g, scale, dxo)


def _matmul_tn(a, b_parts, name):
    bsz, s, m = a.shape
    n = sum(p.shape[2] for p in b_parts)
    tk = next(c for c in (512, 256, 128) if s % c == 0)
    npart = len(b_parts)

    def body(a_ref, *refs):
        b_refs, o_ref = refs[:npart], refs[npart]

        @pl.when(jnp.logical_and(pl.program_id(0) == 0, pl.program_id(1) == 0))
        def _():
            o_ref[...] = jnp.zeros_like(o_ref)

        o_ref[...] += _dot_tn(a_ref[...], _cat_refs(b_refs))

    row = lambda i, k: (i, k, 0)
    return pl.pallas_call(
        body, name=name, grid=(bsz, s // tk),
        out_shape=jax.ShapeDtypeStruct((m, n), F32),
        in_specs=[pl.BlockSpec((None, tk, m), row)] + [pl.BlockSpec((None, tk, p.shape[2]), row) for p in b_parts],
        out_specs=pl.BlockSpec((m, n), lambda i, k: (0, 0)),
        compiler_params=_cp(("arbitrary", "arbitrary")),
    )(a, *b_parts)


def _rel_buckets():
    qi = np.arange(BLK)[:, None]
    kj = np.arange(2 * BLK)[None, :]
    rel = qi - kj + BLK
    n = np.maximum(rel, 0)
    nf = np.maximum(n, 1).astype(np.float32)
    large = REL_MAX_EXACT + (np.log(nf / REL_MAX_EXACT) / math.log(REL_MAX_DIST / REL_MAX_EXACT)
                             * (REL_BUCKETS - REL_MAX_EXACT)).astype(np.int32)
    large = np.minimum(large, REL_BUCKETS - 1)
    bucket = np.where(n < REL_MAX_EXACT, n, large).astype(np.int32)
    valid = ((rel >= 0) & (rel < BLK)).astype(np.int32)
    return bucket, valid


def _swa_bias(rel_bias_t, bucket, valid, name):
    nh = rel_bias_t.shape[0]

    def body(rb_ref, bk_ref, vl_ref, o_ref):
        h = pl.program_id(0)
        bk = bk_ref[...]
        acc = jnp.zeros(bk.shape, F32)
        for i in range(REL_BUCKETS):
            acc = jnp.where(bk == i, rb_ref[h, i], acc)
        o_ref[...] = jnp.where(vl_ref[...] > 0, acc, NEG)

    return pl.pallas_call(
        body, name=name, grid=(nh,),
        out_shape=jax.ShapeDtypeStruct((nh, BLK, 2 * BLK), F32),
        in_specs=[pl.BlockSpec(memory_space=pltpu.SMEM),
                  pl.BlockSpec((BLK, 2 * BLK), lambda h: (0, 0)),
                  pl.BlockSpec((BLK, 2 * BLK), lambda h: (0, 0))],
        out_specs=pl.BlockSpec((None, BLK, 2 * BLK), lambda h: (h, 0, 0)),
        compiler_params=_cp(("arbitrary",)),
    )(rel_bias_t, bucket, valid)


def _swa_scores(n, q, kw, bias_ref):
    sc = _dot_nt(q, kw) * SCALE + bias_ref[...]
    second = lax.broadcasted_iota(jnp.int32, sc.shape, 1) >= BLK
    return jnp.where(jnp.logical_or(n > 0, second), sc, NEG)


def _pad_front(dst_ref, src_ref):
    dst_ref[0:BLK, :] = jnp.zeros((BLK, dst_ref.shape[1]), dst_ref.dtype)
    dst_ref[BLK:, :] = src_ref[...]


def _swa_fwd(q, k, v, bias, sinks, name):
    b, hkv, nb, rows, hd = q.shape
    s = nb * BLK

    def body(q_ref, k_ref, v_ref, bias_ref, sink_ref, o_ref, l_ref, kpad_ref, vpad_ref):
        _pad_front(kpad_ref, k_ref)
        _pad_front(vpad_ref, v_ref)
        sink = sink_ref[...]

        def step(n, carry):
            w0 = pl.multiple_of(n * BLK, BLK)
            sc = _swa_scores(n, q_ref[n], kpad_ref[pl.ds(w0, 2 * BLK), :], bias_ref)
            m = jnp.maximum(jnp.max(sc, axis=1, keepdims=True), sink)
            e = jnp.exp(sc - m)
            den = jnp.sum(e, axis=1, keepdims=True) + jnp.exp(sink - m)
            o_ref[n] = _dot((e * (1.0 / den)).astype(BF16), vpad_ref[pl.ds(w0, 2 * BLK), :])
            l_ref[n] = m + jnp.log(den)
            return carry

        lax.fori_loop(0, nb, step, 0)

    qspec = pl.BlockSpec((None, None, nb, rows, hd), lambda i, kv: (i, kv, 0, 0, 0))
    kspec = pl.BlockSpec((None, None, s, hd), lambda i, kv: (i, kv, 0, 0))
    return pl.pallas_call(
        body, name=name, grid=(b, hkv),
        out_shape=(jax.ShapeDtypeStruct((b, hkv, nb, rows, hd), F32), jax.ShapeDtypeStruct((b, hkv, nb, rows, 1), F32)),
        in_specs=[qspec, kspec, kspec,
                  pl.BlockSpec((None, rows, 2 * BLK), lambda i, kv: (kv, 0, 0)),
                  pl.BlockSpec((None, rows, 1), lambda i, kv: (kv, 0, 0))],
        out_specs=(qspec, pl.BlockSpec((None, None, nb, rows, 1), lambda i, kv: (i, kv, 0, 0, 0))),
        scratch_shapes=[pltpu.VMEM((s + BLK, hd), BF16), pltpu.VMEM((s + BLK, hd), BF16)],
        compiler_params=_cp(("parallel", "parallel")),
    )(q, k, v, bias, sinks)


def _swa_bwd(q, k, v, bias, sinks, do, lse, name):
    b, hkv, nb, rows, hd = q.shape
    s = nb * BLK

    def body(q_ref, k_ref, v_ref, bias_ref, sink_ref, do_ref, l_ref,
             dq_ref, dk_ref, dv_ref, db_ref, dsk_ref, kpad_ref, vpad_ref, dkpad_ref, dvpad_ref):
        _pad_front(kpad_ref, k_ref)
        _pad_front(vpad_ref, v_ref)
        dkpad_ref[...] = jnp.zeros_like(dkpad_ref)
        dvpad_ref[...] = jnp.zeros_like(dvpad_ref)
        db_ref[...] = jnp.zeros_like(db_ref)
        sink = sink_ref[...]

        def step(n, dsink):
            w0 = pl.multiple_of(n * BLK, BLK)
            win = pl.ds(w0, 2 * BLK)
            qn = q_ref[n]
            kw = kpad_ref[win, :]
            ln = l_ref[n]
            p = jnp.exp(_swa_scores(n, qn, kw, bias_ref) - ln)
            dob = do_ref[n]
            dp = _dot_nt(dob, vpad_ref[win, :])
            delta = jnp.sum(p * dp, axis=1, keepdims=True)
            ds = p * (dp - delta)
            db_ref[...] += ds
            dsb = ds.astype(BF16)
            dq_ref[n] = (_dot(dsb, kw) * SCALE).astype(BF16)
            dkpad_ref[win, :] += _dot_tn(dsb, qn)
            dvpad_ref[win, :] += _dot_tn(p.astype(BF16), dob)
            return dsink - jnp.exp(sink - ln) * delta

        dsink = lax.fori_loop(0, nb, step, jnp.zeros((rows, 1), F32))
        for g in range(A_GROUP):
            tot = jnp.sum(dsink[g * BLK:(g + 1) * BLK, :], axis=0, keepdims=True)
            dsk_ref[g] = jnp.broadcast_to(tot, (1, LANE))
        dk_ref[...] = (dkpad_ref[BLK:, :] * SCALE).astype(BF16)
        dv_ref[...] = dvpad_ref[BLK:, :].astype(BF16)

    qspec = pl.BlockSpec((None, None, nb, rows, hd), lambda i, kv: (i, kv, 0, 0, 0))
    kspec = pl.BlockSpec((None, None, s, hd), lambda i, kv: (i, kv, 0, 0))
    return pl.pallas_call(
        body, name=name, grid=(b, hkv),
        out_shape=(jax.ShapeDtypeStruct((b, hkv, nb, rows, hd), BF16), jax.ShapeDtypeStruct((b, hkv, s, hd), BF16),
                   jax.ShapeDtypeStruct((b, hkv, s, hd), BF16), jax.ShapeDtypeStruct((b, hkv, rows, 2 * BLK), F32),
                   jax.ShapeDtypeStruct((b, hkv, A_GROUP, 1, LANE), F32)),
        in_specs=[qspec, kspec, kspec,
                  pl.BlockSpec((None, rows, 2 * BLK), lambda i, kv: (kv, 0, 0)),
                  pl.BlockSpec((None, rows, 1), lambda i, kv: (kv, 0, 0)),
                  qspec,
                  pl.BlockSpec((None, None, nb, rows, 1), lambda i, kv: (i, kv, 0, 0, 0))],
        out_specs=(qspec, kspec, kspec,
                   pl.BlockSpec((None, None, rows, 2 * BLK), lambda i, kv: (i, kv, 0, 0)),
                   pl.BlockSpec((None, None, A_GROUP, 1, LANE), lambda i, kv: (i, kv, 0, 0, 0))),
        scratch_shapes=[pltpu.VMEM((s + BLK, hd), BF16), pltpu.VMEM((s + BLK, hd), BF16),
                        pltpu.VMEM((s + BLK, hd), F32), pltpu.VMEM((s + BLK, hd), F32)],
        compiler_params=_cp(("parallel", "parallel")),
    )(q, k, v, bias, sinks, do, lse)


def _swa_small_grads(db, dsk, bucket, name):
    b, nh = db.shape[0], db.shape[1]

    def body(db_ref, dsk_ref, bk_ref, gb_ref, gs_ref):
        acc = db_ref[0]
        sk = dsk_ref[0]
        for i in range(1, b):
            acc = acc + db_ref[i]
            sk = sk + dsk_ref[i]
        gs_ref[...] = sk
        bk = bk_ref[...]
        for i in range(REL_BUCKETS):
            part = jnp.sum(jnp.where(bk == i, acc, 0.0), axis=1, keepdims=True)
            tot = jnp.sum(part, axis=0, keepdims=True)
            gb_ref[i:i + 1, :] = jnp.broadcast_to(tot, (1, LANE))

    return pl.pallas_call(
        body, name=name, grid=(nh,),
        out_shape=(jax.ShapeDtypeStruct((nh, REL_BUCKETS, LANE), F32), jax.ShapeDtypeStruct((nh, 1, LANE), F32)),
        in_specs=[pl.BlockSpec((b, None, BLK, 2 * BLK), lambda h: (0, h, 0, 0)),
                  pl.BlockSpec((b, None, 1, LANE), lambda h: (0, h, 0, 0)),
                  pl.BlockSpec((BLK, 2 * BLK), lambda h: (0, 0))],
        out_specs=(pl.BlockSpec((None, REL_BUCKETS, LANE), lambda h: (h, 0, 0)),
                   pl.BlockSpec((None, 1, LANE), lambda h: (h, 0, 0))),
        compiler_params=_cp(("parallel",)),
    )(db, dsk, bucket)


def _log_sigmoid(z):
    return jnp.minimum(z, 0.0) - jnp.log(1.0 + jnp.exp(-jnp.abs(z)))


def _fox_decay(z, bf, name):
    b, s, w = z.shape
    nb = s // BLK

    def body(z_ref, bf_ref, f_ref):
        r = lax.broadcasted_iota(jnp.int32, (BLK, BLK), 0)
        c = lax.broadcasted_iota(jnp.int32, (BLK, BLK), 1)
        tri = (c <= r).astype(F32)

        def step(n, carry):
            r0 = pl.multiple_of(n * BLK, BLK)
            lf = _log_sigmoid(z_ref[pl.ds(r0, BLK), :] + bf_ref[...])
            f_ref[pl.ds(r0, BLK), :] = jnp.dot(tri, lf, precision=HI, preferred_element_type=F32) + carry
            return carry + jnp.sum(lf, axis=0, keepdims=True)

        lax.fori_loop(0, nb, step, jnp.zeros((1, w), F32))

    spec = pl.BlockSpec((None, s, w), lambda i: (i, 0, 0))
    return pl.pallas_call(
        body, name=name, grid=(b,), out_shape=jax.ShapeDtypeStruct((b, s, w), F32),
        in_specs=[spec, pl.BlockSpec((1, w), lambda i: (0, 0))], out_specs=spec,
        compiler_params=_cp(("parallel",)),
    )(z, bf)


def _fox_dgate(df, z, bf, nheads, name):
    b, s, w = z.shape
    nb = s // BLK

    def body(df_ref, z_ref, bf_ref, dz_ref, dbf_ref):
        @pl.when(pl.program_id(0) == 0)
        def _():
            dbf_ref[...] = jnp.zeros_like(dbf_ref)

        r = lax.broadcasted_iota(jnp.int32, (BLK, BLK), 0)
        c = lax.broadcasted_iota(jnp.int32, (BLK, BLK), 1)
        tri = (c >= r).astype(F32)
        lane = lax.broadcasted_iota(jnp.int32, (BLK, w), 1)

        def step(i, carry):
            tail, dbf = carry
            r0 = pl.multiple_of((nb - 1 - i) * BLK, BLK)
            dfb = df_ref[pl.ds(r0, BLK), :]
            dlf = jnp.dot(tri, dfb, precision=HI, preferred_element_type=F32) + tail
            dz = jnp.where(lane < nheads, dlf * _sigmoid(-(z_ref[pl.ds(r0, BLK), :] + bf_ref[...])), 0.0)
            dz_ref[pl.ds(r0, BLK), :] = dz
            return tail + jnp.sum(dfb, axis=0, keepdims=True), dbf + jnp.sum(dz, axis=0, keepdims=True)

        zero = jnp.zeros((1, w), F32)
        _, dbf = lax.fori_loop(0, nb, step, (zero, zero))
        dbf_ref[...] += dbf

    spec = pl.BlockSpec((None, s, w), lambda i: (i, 0, 0))
    one = pl.BlockSpec((1, w), lambda i: (0, 0))
    return pl.pallas_call(
        body, name=name, grid=(b,),
        out_shape=(jax.ShapeDtypeStruct((b, s, w), F32), jax.ShapeDtypeStruct((1, w), F32)),
        in_specs=[spec, spec, one], out_specs=(spec, one),
        compiler_params=_cp(("arbitrary",)),
    )(df, z, bf)


def _fox_segments(nb):
    per = max(1, nb // 4)
    return per, nb // per


def _head_masks(shape, axis):
    idx = lax.broadcasted_iota(jnp.int32, shape, axis)
    return idx < HEAD_DIM, idx >= HEAD_DIM


def _fox_fwd(proj, qblk, kblk, vblk, fcol, frow, name):
    b, s, _ = proj.shape
    nh = fcol.shape[1]
    npair = nh // 2
    BLK = min(s, FOX_BQ)
    assert s % BLK == 0
    per, nseg = _fox_segments(s // BLK)

    def body(q_ref, k_ref, v_ref, fc_ref, fr_ref, o_ref, l_ref, qm_ref, kt_ref, vb_ref):
        lo, hi = _head_masks((s, LANE), 1)
        qv = q_ref[...].astype(F32) * SCALE
        qm_ref[0] = jnp.where(lo, qv, 0.0).astype(BF16)
        qm_ref[1] = jnp.where(hi, qv, 0.0).astype(BF16)
        kt_ref[...] = k_ref[...].astype(F32).T.astype(BF16)
        vb_ref[...] = v_ref[...].astype(BF16)
        lane_lo = lax.broadcasted_iota(jnp.int32, (BLK, LANE), 1) < HEAD_DIM
        tail = per * BLK
        causal = (lax.broadcasted_iota(jnp.int32, (BLK, tail), 1)
                  - lax.broadcasted_iota(jnp.int32, (BLK, tail), 0))
        for seg in range(nseg):
            w = (seg + 1) * tail

            def qstep(n, carry):
                r0 = pl.multiple_of(n * BLK, BLK)
                outs = []
                for hh in range(2):
                    sc = _dot(qm_ref[hh, pl.ds(r0, BLK), :], kt_ref[:, :w])
                    sc = sc + (fc_ref[hh, pl.ds(r0, BLK), :] - fr_ref[hh, :, :w])
                    masked = jnp.where(causal <= (n - seg * per) * BLK, sc[:, w - tail:], NEG)
                    sc = masked if seg == 0 else jnp.concatenate([sc[:, :w - tail], masked], axis=1)
                    m = jnp.max(sc, axis=1, keepdims=True)
                    e = jnp.exp(sc - m)
                    l = jnp.sum(e, axis=1, keepdims=True)
                    outs.append(_dot((e * (1.0 / l)).astype(BF16), vb_ref[:w, :]))
                    l_ref[hh, pl.ds(r0, BLK), :] = m + jnp.log(l)
                o_ref[pl.ds(r0, BLK), :] = jnp.where(lane_lo, outs[0], outs[1])
                return carry

            lax.fori_loop(seg * per, (seg + 1) * per, qstep, 0)

    def tok(blk):
        return pl.BlockSpec((None, s, LANE), lambda i, p: (i, 0, blk + p))

    col = pl.BlockSpec((None, 2, s, 1), lambda i, p: (i, p, 0, 0))
    rowspec = pl.BlockSpec((None, 2, 1, s), lambda i, p: (i, p, 0, 0))
    return pl.pallas_call(
        body, name=name, grid=(b, npair),
        out_shape=(jax.ShapeDtypeStruct((b, s, nh * HEAD_DIM), F32), jax.ShapeDtypeStruct((b, nh, s, 1), F32)),
        in_specs=[tok(qblk), tok(kblk), tok(vblk), col, rowspec],
        out_specs=(pl.BlockSpec((None, s, LANE), lambda i, p: (i, 0, p)), col),
        scratch_shapes=[pltpu.VMEM((2, s, LANE), BF16), pltpu.VMEM((LANE, s), BF16), pltpu.VMEM((s, LANE), BF16)],
        compiler_params=_cp(("parallel", "parallel")),
    )(proj, proj, proj, fcol, frow)


def _fox_bwd(proj, qblk, kblk, vblk, dmix, doblk, fcol, frow, frowb, lse, lserowb, name):
    b, s, _ = proj.shape
    nh = fcol.shape[1]
    npair = nh // 2
    BLK = min(s, FOX_BQ)
    assert s % BLK == 0
    nb = s // BLK
    per, nseg = _fox_segments(nb)

    def body(q_ref, k_ref, v_ref, do_ref, fc_ref, fr_ref, frb_ref, l_ref, lrb_ref,
             dq_ref, dk_ref, dv_ref, dfr_ref,
             qm_ref, dom_ref, kb_ref, vb_ref, kt_ref, vt_ref, qtm_ref, dotm_ref, dka_ref, dva_ref):
        lo, hi = _head_masks((s, LANE), 1)
        qv = q_ref[...].astype(F32) * SCALE
        dov = do_ref[...]
        for hh, msk in enumerate((lo, hi)):
            qm_ref[hh] = jnp.where(msk, qv, 0.0).astype(BF16)
            dom_ref[hh] = jnp.where(msk, dov, 0.0).astype(BF16)
        kv = k_ref[...].astype(F32)
        vv = v_ref[...].astype(F32)
        kb_ref[...] = kv.astype(BF16)
        vb_ref[...] = vv.astype(BF16)
        kt_ref[...] = kv.T.astype(BF16)
        vt_ref[...] = vv.T.astype(BF16)
        rlo, rhi = _head_masks((LANE, BLK), 0)

        def tstep(n, carry):
            r0 = pl.multiple_of(n * BLK, BLK)
            qt = (q_ref[pl.ds(r0, BLK), :].astype(F32) * SCALE).T
            dt = do_ref[pl.ds(r0, BLK), :].T
            for hh, msk in enumerate((rlo, rhi)):
                qtm_ref[hh, n] = jnp.where(msk, qt, 0.0).astype(BF16)
                dotm_ref[hh, n] = jnp.where(msk, dt, 0.0).astype(BF16)
            return carry

        lax.fori_loop(0, nb, tstep, 0)
        dka_ref[...] = jnp.zeros_like(dka_ref)
        dva_ref[...] = jnp.zeros_like(dva_ref)
        dfr_ref[...] = jnp.zeros_like(dfr_ref)
        lane_lo = lax.broadcasted_iota(jnp.int32, (BLK, LANE), 1) < HEAD_DIM
        tail = per * BLK
        causal = (lax.broadcasted_iota(jnp.int32, (BLK, tail), 1)
                  - lax.broadcasted_iota(jnp.int32, (BLK, tail), 0))
        causal_t = (lax.broadcasted_iota(jnp.int32, (tail, BLK), 0)
                    - lax.broadcasted_iota(jnp.int32, (tail, BLK), 1))
        for seg in range(nseg):
            w = (seg + 1) * tail

            def nstep(n, carry):
                r0 = pl.multiple_of(n * BLK, BLK)
                lim = (n - seg * per) * BLK
                dqs = []
                for hh in range(2):
                    qn = qm_ref[hh, pl.ds(r0, BLK), :]
                    don = dom_ref[hh, pl.ds(r0, BLK), :]
                    sc = _dot(qn, kt_ref[:, :w]) + ((fc_ref[hh, pl.ds(r0, BLK), :] - l_ref[hh, pl.ds(r0, BLK), :])
                                                   - fr_ref[hh, :, :w])
                    masked = jnp.where(causal <= lim, sc[:, w - tail:], NEG)
                    p = jnp.exp(masked if seg == 0 else jnp.concatenate([sc[:, :w - tail], masked], axis=1))
                    dp = _dot(don, vt_ref[:, :w])
                    ds = p * (dp - jnp.sum(p * dp, axis=1, keepdims=True))
                    dqs.append(_dot(ds.astype(BF16), kb_ref[:w, :]))
                    dfr_ref[hh, :, :w] -= jnp.sum(ds, axis=0, keepdims=True)
                    sct = _dot(kb_ref[:w, :], qtm_ref[hh, n]) + ((frb_ref[hh, n] - lrb_ref[hh, n]) - fc_ref[hh, :w, :])
                    masked_t = jnp.where(causal_t <= lim, sct[w - tail:, :], NEG)
                    pt = jnp.exp(masked_t if seg == 0 else jnp.concatenate([sct[:w - tail, :], masked_t], axis=0))
                    dpt = _dot(vb_ref[:w, :], dotm_ref[hh, n])
                    dst = pt * (dpt - jnp.sum(pt * dpt, axis=0, keepdims=True))
                    dka_ref[:w, :] += _dot(dst.astype(BF16), qn)
                    dva_ref[:w, :] += _dot(pt.astype(BF16), don)
                dq_ref[pl.ds(r0, BLK), :] = (jnp.where(lane_lo, dqs[0], dqs[1]) * SCALE).astype(BF16)
                return carry

            lax.fori_loop(seg * per, (seg + 1) * per, nstep, 0)
        dk_ref[...] = dka_ref[...].astype(BF16)
        dv_ref[...] = dva_ref[...].astype(BF16)

    def tok(blk):
        return pl.BlockSpec((None, s, LANE), lambda i, p: (i, 0, blk + p))

    col = pl.BlockSpec((None, 2, s, 1), lambda i, p: (i, p, 0, 0))
    rowspec = pl.BlockSpec((None, 2, 1, s), lambda i, p: (i, p, 0, 0))
    rowbspec = pl.BlockSpec((None, 2, nb, 1, BLK), lambda i, p: (i, p, 0, 0, 0))
    outtok = pl.BlockSpec((None, s, LANE), lambda i, p: (i, 0, p))
    shp = jax.ShapeDtypeStruct((b, s, nh * HEAD_DIM), BF16)
    return pl.pallas_call(
        body, name=name, grid=(b, npair),
        out_shape=(shp, shp, shp, jax.ShapeDtypeStruct((b, nh, 1, s), F32)),
        in_specs=[tok(qblk), tok(kblk), tok(vblk),
                  pl.BlockSpec((None, s, LANE), lambda i, p: (i, 0, doblk + p)),
                  col, rowspec, rowbspec, col, rowbspec],
        out_specs=(outtok, outtok, outtok, rowspec),
        scratch_shapes=[pltpu.VMEM((2, s, LANE), BF16), pltpu.VMEM((2, s, LANE), BF16),
                        pltpu.VMEM((s, LANE), BF16), pltpu.VMEM((s, LANE), BF16),
                        pltpu.VMEM((LANE, s), BF16), pltpu.VMEM((LANE, s), BF16),
                        pltpu.VMEM((2, nb, LANE, BLK), BF16), pltpu.VMEM((2, nb, LANE, BLK), BF16),
                        pltpu.VMEM((s, LANE), F32), pltpu.VMEM((s, LANE), F32)],
        compiler_params=_cp(("parallel", "parallel")),
    )(proj, proj, proj, dmix, fcol, frow, frowb, lse, lserowb)


def _expm1(x):
    poly = x * (1.0 + x * (1.0 / 2 + x * (1.0 / 6 + x * (1.0 / 24 + x * (1.0 / 120 + x * (1.0 / 720))))))
    return jnp.where(x > -0.1, poly, jnp.exp(x) - 1.0)


def _softplus(z):
    return jnp.maximum(z, 0.0) + jnp.log(1.0 + jnp.exp(-jnp.abs(z)))


def _scan_rows(a, u, carry, row, up):
    tc, c = a.shape
    d = 1
    while d < tc:
        if d < SUBLANE:
            keep = (row >= d) if up else (row < tc - d)
            shift = d if up else tc - d
            a_sh = jnp.where(keep, pltpu.roll(a, shift, 0), 1.0)
            u_sh = jnp.where(keep, pltpu.roll(u, shift, 0), 0.0)
        elif up:
            a_sh = jnp.concatenate([jnp.ones((d, c), F32), a[:tc - d]], axis=0)
            u_sh = jnp.concatenate([jnp.zeros((d, c), F32), u[:tc - d]], axis=0)
        else:
            a_sh = jnp.concatenate([a[d:], jnp.ones((d, c), F32)], axis=0)
            u_sh = jnp.concatenate([u[d:], jnp.zeros((d, c), F32)], axis=0)
        u = a * u_sh + u
        a = a * a_sh
        d *= 2
    return u + a * carry


def _scan_up(a, u, carry, row):
    return _scan_rows(a, u, carry, row, True)


def _scan_down(bnext, g, carry, row):
    return _scan_rows(bnext, g, carry, row, False)


def _pick_row(val, row, which):
    return jnp.sum(jnp.where(row == which, val, 0.0), axis=0, keepdims=True)


def _lru_gates(xpad_ref, t0, tc, cw_ref, cb_ref, wa, ba_ref, wx, bx_ref, sp):
    xw = xpad_ref[pl.ds(t0, tc + SUBLANE), :]
    xc = cb_ref[...]
    for j in range(CONV_WIDTH):
        sh = CONV_WIDTH - 1 - j
        xs = xw if sh == 0 else pltpu.roll(xw, sh, 0)
        xc = xc + xs[SUBLANE:, :] * cw_ref[j:j + 1, :]
    xcb = xc.astype(BF16)
    r = _sigmoid(_dot(xcb, wa) + ba_ref[...])
    i = _sigmoid(_dot(xcb, wx) + bx_ref[...])
    la = -LRU_C * r * sp
    return xc, r, i, la


def _lru_specs(s, cb):
    seq = lambda bi, ni: (bi, 0, ni)
    return dict(
        seq=pl.BlockSpec((None, s, cb), seq),
        cw=pl.BlockSpec((CONV_WIDTH, cb), lambda bi, ni: (0, ni)),
        vec=pl.BlockSpec((1, cb), lambda bi, ni: (0, ni)),
        wblk=pl.BlockSpec((None, cb, cb), lambda bi, ni: (ni, 0, 0)),
    )


def _lru_fwd(proj, cw, cb_, wa, ba, wx, bx, lam, name):
    b, s, _ = proj.shape
    nblk, cb, _ = wa.shape
    tc = min(s, SCAN_CHUNK)
    nc = s // tc

    def body(x_ref, cw_ref, cb_ref, wa_ref, ba_ref, wx_ref, bx_ref, lam_ref, hs_ref, xpad_ref):
        xpad_ref[0:SUBLANE, :] = jnp.zeros((SUBLANE, cb), F32)
        xpad_ref[SUBLANE:, :] = x_ref[...].astype(F32)
        wa_b = wa_ref[...].astype(BF16)
        wx_b = wx_ref[...].astype(BF16)
        sp = _softplus(-lam_ref[...])
        row = lax.broadcasted_iota(jnp.int32, (tc, cb), 0)

        def chunk(ci, carry):
            t0 = pl.multiple_of(ci * tc, tc)
            xc, r, i, la = _lru_gates(xpad_ref, t0, tc, cw_ref, cb_ref, wa_b, ba_ref, wx_b, bx_ref, sp)
            a = jnp.exp(la)
            u = jnp.sqrt(-_expm1(2.0 * la)) * (i * xc)
            h = _scan_up(a, u, carry, row)
            hs_ref[pl.ds(t0, tc), :] = h
            return _pick_row(h, row, tc - 1)

        lax.fori_loop(0, nc, chunk, jnp.zeros((1, cb), F32))

    sp_ = _lru_specs(s, cb)
    return pl.pallas_call(
        body, name=name, grid=(b, nblk),
        out_shape=jax.ShapeDtypeStruct((b, s, nblk * cb), F32),
        in_specs=[sp_["seq"], sp_["cw"], sp_["vec"], sp_["wblk"], sp_["vec"], sp_["wblk"], sp_["vec"], sp_["vec"]],
        out_specs=sp_["seq"],
        scratch_shapes=[pltpu.VMEM((s + SUBLANE, cb), F32)],
        compiler_params=_cp(("parallel", "parallel")),
    )(proj, cw, cb_, wa, ba, wx, bx, lam)


def _lru_bwd(proj, hs, dhs, cw, cb_, wa, ba, wx, bx, lam, name):
    b, s, _ = proj.shape
    nblk, cb, _ = wa.shape
    tc = min(s, SCAN_CHUNK)
    nc = s // tc

    def body(x_ref, hs_ref, dhs_ref, cw_ref, cb_ref, wa_ref, ba_ref, wx_ref, bx_ref, lam_ref,
             dx_ref, dcw_ref, dcb_ref, dwa_ref, dba_ref, dwx_ref, dbx_ref, dlam_ref,
             xpad_ref, hpad_ref, dcpad_ref, xc_ref, r_ref, i_ref, a_ref):
        @pl.when(pl.program_id(1) == 0)
        def _():
            for ref in (dcw_ref, dcb_ref, dwa_ref, dba_ref, dwx_ref, dbx_ref, dlam_ref):
                ref[...] = jnp.zeros_like(ref)

        zeros8 = jnp.zeros((SUBLANE, cb), F32)
        xpad_ref[0:SUBLANE, :] = zeros8
        xpad_ref[SUBLANE:, :] = x_ref[...].astype(F32)
        hpad_ref[0:SUBLANE, :] = zeros8
        hpad_ref[SUBLANE:, :] = hs_ref[...]
        dcpad_ref[s:s + SUBLANE, :] = zeros8
        wa_b = wa_ref[...].astype(BF16)
        wx_b = wx_ref[...].astype(BF16)
        lam_v = lam_ref[...]
        sp = _softplus(-lam_v)
        dsp_dlam = -_sigmoid(-lam_v)
        row = lax.broadcasted_iota(jnp.int32, (tc, cb), 0)

        def recompute(ci, carry):
            t0 = pl.multiple_of(ci * tc, tc)
            xc, r, i, la = _lru_gates(xpad_ref, t0, tc, cw_ref, cb_ref, wa_b, ba_ref, wx_b, bx_ref, sp)
            xc_ref[pl.ds(t0, tc), :] = xc
            r_ref[pl.ds(t0, tc), :] = r
            i_ref[pl.ds(t0, tc), :] = i
            a_ref[pl.ds(t0, tc), :] = jnp.exp(la)
            return carry

        lax.fori_loop(0, nc, recompute, 0)

        def adjoint(k, carry):
            g_next, a_first_next = carry
            t0 = pl.multiple_of((nc - 1 - k) * tc, tc)
            a = a_ref[pl.ds(t0, tc), :]
            a_next = jnp.where(row == tc - 1, a_first_next, pltpu.roll(a, tc - 1, 0))
            gg = _scan_down(a_next, dhs_ref[pl.ds(t0, tc), :], g_next, row)
            h_prev = pltpu.roll(hpad_ref[pl.ds(t0, tc + SUBLANE), :], 1, 0)[SUBLANE:, :]
            xc = xc_ref[pl.ds(t0, tc), :]
            r = r_ref[pl.ds(t0, tc), :]
            i = i_ref[pl.ds(t0, tc), :]
            mult = jnp.sqrt(-_expm1(-2.0 * LRU_C * r * sp))
            d_mult = gg * i * xc
            d_i = gg * mult * xc
            d_xc = gg * mult * i
            d_la = gg * h_prev * a - d_mult * (a * a) / mult
            d_zr = (d_la * (-LRU_C * sp)) * r * (1.0 - r)
            d_zi = d_i * i * (1.0 - i)
            dlam_ref[...] += jnp.sum(d_la * (-LRU_C * r), axis=0, keepdims=True) * dsp_dlam
            dzr_b = d_zr.astype(BF16)
            dzi_b = d_zi.astype(BF16)
            xcb = xc.astype(BF16)
            d_xc = d_xc + _dot_nt(dzr_b, wa_b) + _dot_nt(dzi_b, wx_b)
            dwa_ref[...] += _dot_tn(xcb, dzr_b)
            dwx_ref[...] += _dot_tn(xcb, dzi_b)
            dba_ref[...] += jnp.sum(d_zr, axis=0, keepdims=True)
            dbx_ref[...] += jnp.sum(d_zi, axis=0, keepdims=True)
            dcb_ref[...] += jnp.sum(d_xc, axis=0, keepdims=True)
            dcpad_ref[pl.ds(t0, tc), :] = d_xc
            return _pick_row(gg, row, 0), _pick_row(a, row, 0)

        zero = jnp.zeros((1, cb), F32)
        lax.fori_loop(0, nc, adjoint, (zero, zero))

        def conv_back(ci, carry):
            t0 = pl.multiple_of(ci * tc, tc)
            dw = dcpad_ref[pl.ds(t0, tc + SUBLANE), :]
            xw = xpad_ref[pl.ds(t0, tc + SUBLANE), :]
            d_xc = dw[:tc, :]
            dxr = jnp.zeros((tc, cb), F32)
            for j in range(CONV_WIDTH):
                sh = CONV_WIDTH - 1 - j
                dsh = dw if sh == 0 else pltpu.roll(dw, tc + SUBLANE - sh, 0)
                dxr = dxr + dsh[:tc, :] * cw_ref[j:j + 1, :]
                xs = xw if sh == 0 else pltpu.roll(xw, sh, 0)
                dcw_ref[j:j + 1, :] += jnp.sum(d_xc * xs[SUBLANE:, :], axis=0, keepdims=True)
            dx_ref[pl.ds(t0, tc), :] = dxr.astype(BF16)
            return carry

        lax.fori_loop(0, nc, conv_back, 0)

    seq = lambda ni, bi: (bi, 0, ni)
    seqspec = pl.BlockSpec((None, s, cb), seq)
    cwspec = pl.BlockSpec((CONV_WIDTH, cb), lambda ni, bi: (0, ni))
    vec = pl.BlockSpec((1, cb), lambda ni, bi: (0, ni))
    wblk = pl.BlockSpec((None, cb, cb), lambda ni, bi: (ni, 0, 0))
    w = nblk * cb
    return pl.pallas_call(
        body, name=name, grid=(nblk, b),
        out_shape=(jax.ShapeDtypeStruct((b, s, w), BF16), jax.ShapeDtypeStruct((CONV_WIDTH, w), F32),
                   jax.ShapeDtypeStruct((1, w), F32), jax.ShapeDtypeStruct((nblk, cb, cb), F32),
                   jax.ShapeDtypeStruct((1, w), F32), jax.ShapeDtypeStruct((nblk, cb, cb), F32),
                   jax.ShapeDtypeStruct((1, w), F32), jax.ShapeDtypeStruct((1, w), F32)),
        in_specs=[seqspec, seqspec, seqspec, cwspec, vec, wblk, vec, wblk, vec, vec],
        out_specs=(seqspec, cwspec, vec, wblk, vec, wblk, vec, vec),
        scratch_shapes=[pltpu.VMEM((s + SUBLANE, cb), F32)] * 3 + [pltpu.VMEM((s, cb), F32)] * 4,
        compiler_params=_cp(("parallel", "arbitrary")),
    )(proj, hs, dhs, cw, cb_, wa, ba, wx, bx, lam)


def _adamw(w, g, m, v, name):
    shape = w.shape
    total = int(np.prod(shape))
    if w.ndim >= 2 and shape[-2] % SUBLANE == 0:
        rows, cols = shape[-2:]
    else:
        cols = 1024
        rows = -(-(-(-total // cols)) // SUBLANE) * SUBLANE
    lead = -(-total // (rows * cols))
    tr = _row_tile(rows, 512)
    pad = lead * rows * cols - total

    def flat(a):
        if pad:
            a = jnp.pad(a.reshape(-1), (0, pad))
        return a.reshape(lead, rows, cols)

    c1 = 1.0 - ADAM_B1 ** ADAM_STEP
    c2 = 1.0 - ADAM_B2 ** ADAM_STEP

    def body(w_ref, g_ref, m_ref, v_ref, d_ref, nm_ref, nv_ref):
        gv = g_ref[...]
        nm = ADAM_B1 * m_ref[...] + (1.0 - ADAM_B1) * gv
        nv = ADAM_B2 * v_ref[...] + (1.0 - ADAM_B2) * (gv * gv)
        nm_ref[...] = nm
        nv_ref[...] = nv
        d_ref[...] = -ADAM_LR * ((nm / c1) / (jnp.sqrt(nv / c2) + ADAM_EPS) + ADAM_WD * w_ref[...])

    spec = pl.BlockSpec((None, tr, cols), lambda l, i: (l, i, 0))
    shp = jax.ShapeDtypeStruct((lead, rows, cols), F32)
    outs = pl.pallas_call(
        body, name=name, grid=(lead, rows // tr), out_shape=(shp, shp, shp),
        in_specs=[spec] * 4, out_specs=(spec,) * 3,
        compiler_params=_cp(("parallel", "parallel")),
    )(flat(w), flat(g), flat(m), flat(v))
    if pad:
        return tuple(o.reshape(-1)[:total].reshape(shape) for o in outs)
    return tuple(o.reshape(shape) for o in outs)


def _to_heads(t, nh):
    b, s, _ = t.shape
    return t.reshape(b, s, nh, HEAD_DIM).transpose(0, 2, 1, 3)


def _stack_heads(t):
    b, s, _ = t.shape
    t = t.reshape(b, s // BLK, BLK, A_KV_HEADS, A_GROUP, HEAD_DIM).transpose(0, 3, 1, 4, 2, 5)
    return t.reshape(b, A_KV_HEADS, s // BLK, A_GROUP * BLK, HEAD_DIM)


def _unstack_heads(t):
    b, hkv, nb, rows, hd = t.shape
    t = t.reshape(b, hkv, nb, A_GROUP, BLK, hd).transpose(0, 2, 4, 1, 3, 5)
    return t.reshape(b, nb * BLK, hkv * A_GROUP * hd)


def _from_heads(t):
    b, nh, s, hd = t.shape
    return t.transpose(0, 2, 1, 3).reshape(b, s, nh * hd)


def _pad_rows(a, mult):
    r = a.shape[0]
    p = (-r) % mult
    return jnp.pad(a, ((0, p), (0, 0))) if p else a


def kernel(x, c, rel_bias, norm_g, ada_w, ada_b, attn_w_in, attn_sinks, attn_b_f, attn_w_out, lru_w_in, lru_conv_w, lru_conv_b, lru_w_a, lru_b_a, lru_w_x, lru_b_x, lru_lambda, lru_w_out, final_g, loss_target, m_rel_bias, m_norm_g, m_ada_w, m_ada_b, m_attn_w_in, m_attn_sinks, m_attn_b_f, m_attn_w_out, m_lru_w_in, m_lru_conv_w, m_lru_conv_b, m_lru_w_a, m_lru_b_a, m_lru_w_x, m_lru_b_x, m_lru_lambda, m_lru_w_out, m_final_g, v_rel_bias, v_norm_g, v_ada_w, v_ada_b, v_attn_w_in, v_attn_sinks, v_attn_b_f, v_attn_w_out, v_lru_w_in, v_lru_conv_w, v_lru_conv_b, v_lru_w_a, v_lru_b_a, v_lru_w_x, v_lru_b_x, v_lru_lambda, v_lru_w_out, v_final_g):
    bl, s, d = x.shape
    ix, iy, ic = lax.axis_index("x"), lax.axis_index("y"), lax.axis_index("c")
    chip = 2 * ix + iy
    me = 2 * chip + ic
    nb = s // BLK
    aw = A_Q_HEADS * HEAD_DIM
    akv = A_KV_HEADS * HEAD_DIM
    bw = B_HEADS * HEAD_DIM
    mixw = aw + bw
    qkv_w = aw + 2 * akv + 3 * bw
    n_in = attn_w_in.shape[2] * N_CHIP
    lw = lru_lambda.shape[1] * N_CHIP
    n0 = mixw + qkv_w + LANE

    rows_pad = -(-bl // SUBLANE) * SUBLANE
    c_all = _all_gather8(_pad_rows(c, SUBLANE), "gather_c", pltpu.VMEM)
    c_all = c_all.reshape(N_DEV, rows_pad, d)[:, :bl].reshape(N_DEV * bl, d)
    ncol = ada_w.shape[2]
    ada_w_l = lax.dynamic_index_in_dim(ada_w, ic, 0, keepdims=False)
    ada_b_l = lax.dynamic_slice(ada_b, (ic, chip * ncol), (1, ncol))
    mod_part = _ada_fwd(c_all, ada_w_l, ada_b_l, "ada_fwd")
    mod_all = _all_gather8(_pad_rows(mod_part, SUBLANE), "gather_mod", pltpu.VMEM)
    mrows = -(-(N_DEV * bl) // SUBLANE) * SUBLANE
    mod_all = mod_all.reshape(N_CHIP, 2, mrows, ncol)[:, :, :N_DEV * bl]
    mod_all = mod_all.transpose(1, 2, 0, 3).reshape(2, N_DEV * bl, N_CHIP * ncol)
    mod = lax.dynamic_slice_in_dim(mod_all, me * bl, bl, axis=1)
    shift = [mod[l, :, 0:d].reshape(bl, 1, d) for l in range(2)]
    scale = [mod[l, :, d:2 * d].reshape(bl, 1, d) for l in range(2)]
    gmod = [mod[l, :, 2 * d:3 * d].reshape(bl, 1, d) for l in range(2)]

    c_in0 = n_in // N_CHIP
    c_in1 = 2 * lw // N_CHIP
    assert c_in0 <= d and 2 * c_in1 == d
    r_in0, r_out0, r_in1, r_out1 = d // 2, mixw // N_CHIP // 2, d // 4, lw // N_CHIP // 2
    o_out0, o_in1, o_out1 = r_in0, r_in0 + r_out0, r_in0 + r_out0 + r_in1
    big_rows = o_out1 + r_out1

    def half_of(a, rows):
        return lax.dynamic_slice_in_dim(a, ic * rows, rows, axis=0)

    h_in1 = half_of(lru_w_in[0], r_in0).astype(BF16)
    my_half = jnp.concatenate([
        jnp.pad(half_of(attn_w_in[0], r_in0).astype(BF16), ((0, 0), (0, d - c_in0))),
        half_of(attn_w_out[0], r_out0).astype(BF16),
        jnp.concatenate([h_in1[:r_in1], h_in1[r_in1:]], axis=1),
        half_of(lru_w_out[0], r_out1).astype(BF16)], axis=0)
    gat = _all_gather8(my_half, "gather_weights", pltpu.HBM).reshape(N_CHIP, 2, big_rows, d)
    w_in0 = gat[:, :, :r_in0, :c_in0].transpose(1, 2, 0, 3).reshape(d, n_in)
    w_out0 = gat[:, :, o_out0:o_in1].reshape(mixw, d)
    w_in1 = gat[:, :, o_in1:o_out1].reshape(N_CHIP, 2, r_in1, 2, c_in1)
    w_in1 = w_in1.transpose(1, 3, 2, 0, 4).reshape(d, 2 * lw)
    w_out1 = gat[:, :, o_out1:].reshape(lw, d)
    w_cat0 = jnp.concatenate([w_in0[:, qkv_w + B_HEADS:], w_in0[:, :qkv_w + B_HEADS],
                              jnp.zeros((d, n0 - n_in), BF16)], axis=1)

    proj0, h0, zf = _norm_proj(x, norm_g[0:1], scale[0], shift[0], w_cat0, LANE, "norm_proj0")
    o_a = mixw
    aq = _stack_heads(proj0[:, :, o_a:o_a + aw].astype(BF16))
    ak = _to_heads(proj0[:, :, o_a + aw:o_a + aw + akv].astype(BF16), A_KV_HEADS)
    av = _to_heads(proj0[:, :, o_a + aw + akv:o_a + aw + 2 * akv].astype(BF16), A_KV_HEADS)
    o_b = o_a + aw + 2 * akv
    fox_blks = (o_b // LANE, (o_b + bw) // LANE, (o_b + 2 * bw) // LANE)
    bucket_np, valid_np = _rel_buckets()
    bucket = jnp.asarray(bucket_np)
    bias = _swa_bias(rel_bias.T, bucket, jnp.asarray(valid_np), "swa_bias")
    bias = bias.reshape(A_KV_HEADS, A_GROUP * BLK, 2 * BLK)
    sinks = jnp.repeat(attn_sinks[0].reshape(A_KV_HEADS, A_GROUP), BLK, axis=1).reshape(A_KV_HEADS, A_GROUP * BLK, 1)
    a_out, a_lse = _swa_fwd(aq, ak, av, bias, sinks, "swa_fwd")
    bf_pad = jnp.pad(attn_b_f, ((0, 0), (0, LANE - B_HEADS)))
    fsum = _fox_decay(zf, bf_pad, "fox_decay")
    fh = fsum[:, :, :B_HEADS].transpose(0, 2, 1)
    fcol = fh.reshape(bl, B_HEADS, s, 1)
    frow = fh.reshape(bl, B_HEADS, 1, s)
    fbq = min(s, FOX_BQ)
    frowb = fh.reshape(bl, B_HEADS, s // fbq, 1, fbq)
    b_out, b_lse = _fox_fwd(proj0, *fox_blks, fcol, frow, "fox_fwd")
    mix0 = [_unstack_heads(a_out), b_out]
    x1, o0 = _gate_outproj(mix0, proj0, 0, w_out0, x, gmod[0], "gate_outproj0")

    proj1, h1 = _norm_proj(x1, norm_g[1:2], scale[1], shift[1], w_in1, 0, "norm_proj1")
    vec_rows = jnp.concatenate([lru_conv_w[0], lru_conv_b, lru_b_a, lru_b_x, lru_lambda], axis=0)
    vec_all = _all_gather8(vec_rows, "gather_lru_vectors", pltpu.VMEM)
    vec_all = vec_all.reshape(N_CHIP, 2, SUBLANE, lw // N_CHIP)[:, 0]
    vec_all = vec_all.transpose(1, 0, 2).reshape(SUBLANE, lw)
    cw_f, cb_f, ba_f, bx_f, lam_f = vec_all[0:4], vec_all[4:5], vec_all[5:6], vec_all[6:7], vec_all[7:8]
    hs = _lru_fwd(proj1, cw_f, cb_f, lru_w_a[0], ba_f, lru_w_x[0], bx_f, lam_f, "lru_fwd")
    x2, o1 = _gate_outproj([hs], proj1, 1, w_out1, x1, gmod[1], "gate_outproj1")

    loss_vec, dx2, g_final = _final_loss(x2, final_g.reshape(1, d), loss_target, "final_loss")
    loss = lax.psum(loss_vec[0, 0], ("x", "y", "c"))

    dhs, dgate1, do1, y1, dgm1 = _bwd_out(dx2, gmod[1], o1, [hs], proj1, 1, w_out1.T, "bwd_out1")
    g_w_out1 = _matmul_tn(y1, [do1], "grad_w_out1")
    (dxr, g_cw, g_cb, g_wa, g_ba, g_wx, g_bx, g_lam) = _lru_bwd(
        proj1, hs, dhs, cw_f, cb_f, lru_w_a[0], ba_f, lru_w_x[0], bx_f, lam_f, "lru_bwd")
    dproj1 = [dxr, dgate1]
    g_w_in1 = _matmul_tn(h1, dproj1, "grad_w_in1")
    dx1, dsh1, dsc1, g_ng1 = _bwd_in(dproj1, w_in1.T, x1, norm_g[1:2], scale[1], dx2, "bwd_in1")

    dmix0, dgate0, do0, y0, dgm0 = _bwd_out(dx1, gmod[0], o0, mix0, proj0, 0, w_out0.T, "bwd_out0")
    g_w_out0 = _matmul_tn(y0, [do0], "grad_w_out0")
    da_out = _stack_heads(dmix0[:, :, :aw].astype(BF16))
    daq, dak, dav, dbias, dsink = _swa_bwd(aq, ak, av, bias, sinks, da_out, a_lse, "swa_bwd")
    dbq, dbk, dbv, dfrow = _fox_bwd(proj0, *fox_blks, dmix0, aw // LANE, fcol, frow, frowb, b_lse,
                                    b_lse.reshape(bl, B_HEADS, s // fbq, 1, fbq), "fox_bwd")
    df = dfrow.reshape(bl, B_HEADS, s).transpose(0, 2, 1)
    df = jnp.pad(df, ((0, 0), (0, 0), (0, LANE - B_HEADS)))
    dzf, g_bf = _fox_dgate(df, zf, bf_pad, B_HEADS, "fox_dgate")
    dproj0 = ([dgate0, _unstack_heads(daq), _from_heads(dak), _from_heads(dav)]
              + [dbq, dbk, dbv, dzf.astype(BF16)])
    g_w_cat0 = _matmul_tn(h0, dproj0, "grad_w_in0")
    g_w_in0 = jnp.concatenate([g_w_cat0[:, mixw:mixw + qkv_w + B_HEADS], g_w_cat0[:, :mixw]], axis=1)
    dx0, dsh0, dsc0, g_ng0 = _bwd_in(dproj0, w_cat0.T, x, norm_g[0:1], scale[0], dx1, "bwd_in0")
    g_relb, g_sink = _swa_small_grads(dbias.reshape(bl, A_Q_HEADS, BLK, 2 * BLK),
                                      dsink.reshape(bl, A_Q_HEADS, 1, LANE), bucket, "swa_small_grads")

    dmod = jnp.concatenate([jnp.concatenate([dsh0, dsc0, dgm0], axis=-1),
                            jnp.concatenate([dsh1, dsc1, dgm1], axis=-1)], axis=1)
    dmod_all = _all_gather8(_pad_rows(dmod.reshape(bl, 6 * d), SUBLANE), "gather_dmod", pltpu.VMEM)
    dmod_all = dmod_all.reshape(N_DEV, rows_pad, 6 * d)[:, :bl].reshape(N_DEV * bl, 6 * d)
    dmod_chip = lax.dynamic_slice_in_dim(dmod_all.reshape(N_DEV * bl, 2, 3 * d), chip * ncol, ncol, axis=2)
    g_ada_w, g_ada_b = _ada_bwd(c_all, dmod_chip.transpose(1, 0, 2), dmod_all, "ada_bwd")
    g_ada_b = g_ada_b.reshape(2, 3 * d)

    tail = jnp.concatenate([g_relb[:, :, 0].T.reshape(-1), g_sink[:, 0, 0], g_bf[0, :B_HEADS]])
    n_relb = REL_BUCKETS * A_Q_HEADS
    small_rows = [g_wa.reshape(-1, d), g_wx.reshape(-1, d), g_ng0, g_ng1, g_final, g_cw, g_cb, g_ba, g_bx, g_lam,
                  jnp.pad(tail, (0, d - tail.shape[0])).reshape(1, d)]
    small_counts = [r.shape[0] for r in small_rows]
    piece_rows = -(-(-(-sum(small_counts) // N_DEV)) // SUBLANE) * SUBLANE
    small_2d = jnp.concatenate(small_rows, axis=0)
    small_2d = jnp.pad(small_2d, ((0, N_DEV * piece_rows - small_2d.shape[0]), (0, 0)))
    small_pieces = small_2d.reshape(N_CHIP, 2, piece_rows, d)
    p_in0 = jnp.pad(g_w_in0.reshape(2, r_in0, N_CHIP, c_in0).transpose(2, 0, 1, 3),
                    ((0, 0), (0, 0), (0, 0), (0, d - c_in0)))
    p_in1 = g_w_in1.reshape(2, 2, r_in1, N_CHIP, c_in1).transpose(3, 0, 2, 1, 4).reshape(N_CHIP, 2, r_in1, d)
    pieces = jnp.concatenate([p_in0, g_w_out0.reshape(N_CHIP, 2, r_out0, d), p_in1,
                              g_w_out1.reshape(N_CHIP, 2, r_out1, d), small_pieces], axis=2)
    theirs = _sibling_push(pieces, True, "push_sibling_halves")
    partial = _pair_sum(jnp.reshape(ic, (1,)).astype(jnp.int32), pieces, theirs, "sum_chip")
    slots = _chip_all_to_all(partial, "exchange_grads")
    reduced = _sum_slots(slots, "sum_grads")
    mine_big = reduced[:big_rows]
    other_big = _sibling_push(mine_big[None], False, "swap_halves")[0]
    both = jnp.stack([jnp.where(ic == 0, mine_big, other_big), jnp.where(ic == 0, other_big, mine_big)])
    g_big = [both[:, :r_in0, :c_in0].reshape(d, c_in0),
             both[:, o_out0:o_in1].reshape(2 * r_out0, d),
             both[:, o_in1:o_out1].reshape(2, r_in1, 2, c_in1).transpose(0, 2, 1, 3).reshape(d, c_in1),
             both[:, o_out1:].reshape(2 * r_out1, d)]
    small_all = _all_gather8(reduced[big_rows:], "gather_small_grads", pltpu.VMEM)
    g_small, off = [], 0
    for cnt in small_counts:
        g_small.append(small_all[off:off + cnt])
        off += cnt
    g_w_a, g_w_x = g_small[0].reshape(lru_w_a.shape[1:]), g_small[1].reshape(lru_w_x.shape[1:])
    g_norm_g = jnp.concatenate(g_small[2:4], axis=0)
    g_fin, g_cw_r, g_cb_r, g_ba_r, g_bx_r, g_lam_r = g_small[4:10]
    tail = g_small[10][0]
    g_rel_bias = tail[:n_relb].reshape(REL_BUCKETS, A_Q_HEADS)
    g_sinks, g_b_f = tail[n_relb:n_relb + A_Q_HEADS], tail[n_relb + A_Q_HEADS:n_relb + A_Q_HEADS + B_HEADS]
    cw4 = lw // N_CHIP

    def my_cols(a):
        return lax.dynamic_slice_in_dim(a, chip * cw4, cw4, axis=1)

    grads = {
        "rel_bias": g_rel_bias, "norm_g": g_norm_g, "ada_w": g_ada_w, "ada_b": g_ada_b,
        "attn_w_in": g_big[0][None], "attn_sinks": g_sinks[None], "attn_b_f": g_b_f[None],
        "attn_w_out": g_big[1][None], "lru_w_in": g_big[2][None], "lru_conv_w": my_cols(g_cw_r)[None],
        "lru_conv_b": my_cols(g_cb_r), "lru_w_a": g_w_a[None], "lru_b_a": my_cols(g_ba_r),
        "lru_w_x": g_w_x[None], "lru_b_x": my_cols(g_bx_r), "lru_lambda": my_cols(g_lam_r),
        "lru_w_out": g_big[3][None], "final_g": g_fin.reshape(d),
    }
    weights = dict(rel_bias=rel_bias, norm_g=norm_g, ada_w=ada_w, ada_b=ada_b, attn_w_in=attn_w_in,
                   attn_sinks=attn_sinks, attn_b_f=attn_b_f, attn_w_out=attn_w_out, lru_w_in=lru_w_in,
                   lru_conv_w=lru_conv_w, lru_conv_b=lru_conv_b, lru_w_a=lru_w_a, lru_b_a=lru_b_a,
                   lru_w_x=lru_w_x, lru_b_x=lru_b_x, lru_lambda=lru_lambda, lru_w_out=lru_w_out, final_g=final_g)
    moms = dict(rel_bias=(m_rel_bias, v_rel_bias), norm_g=(m_norm_g, v_norm_g), ada_w=(m_ada_w, v_ada_w),
                ada_b=(m_ada_b, v_ada_b), attn_w_in=(m_attn_w_in, v_attn_w_in),
                attn_sinks=(m_attn_sinks, v_attn_sinks), attn_b_f=(m_attn_b_f, v_attn_b_f),
                attn_w_out=(m_attn_w_out, v_attn_w_out), lru_w_in=(m_lru_w_in, v_lru_w_in),
                lru_conv_w=(m_lru_conv_w, v_lru_conv_w), lru_conv_b=(m_lru_conv_b, v_lru_conv_b),
                lru_w_a=(m_lru_w_a, v_lru_w_a), lru_b_a=(m_lru_b_a, v_lru_b_a), lru_w_x=(m_lru_w_x, v_lru_w_x),
                lru_b_x=(m_lru_b_x, v_lru_b_x), lru_lambda=(m_lru_lambda, v_lru_lambda),
                lru_w_out=(m_lru_w_out, v_lru_w_out), final_g=(m_final_g, v_final_g))
    names = list(weights)
    big_names = [n for n in names if weights[n].size >= 65536]
    small_names = [n for n in names if weights[n].size < 65536]
    delta, new_m, new_v = {}, {}, {}
    for n in big_names:
        delta[n], new_m[n], new_v[n] = _adamw(weights[n], grads[n].reshape(weights[n].shape),
                                              moms[n][0], moms[n][1], "adamw_" + n)
    cat = lambda arrs: jnp.concatenate([a.reshape(-1) for a in arrs])
    sd, sm, sv = _adamw(cat([weights[n] for n in small_names]), cat([grads[n] for n in small_names]),
                        cat([moms[n][0] for n in small_names]), cat([moms[n][1] for n in small_names]),
                        "adamw_small")
    off = 0
    for n in small_names:
        sz = weights[n].size
        shp = weights[n].shape
        delta[n], new_m[n], new_v[n] = (sd[off:off + sz].reshape(shp), sm[off:off + sz].reshape(shp),
                                        sv[off:off + sz].reshape(shp))
        off += sz
    out_grads = [grads[n].reshape(weights[n].shape) for n in names]
    return (loss, dx0, *out_grads, *[delta[n] for n in names], *[new_m[n] for n in names],
            *[new_v[n] for n in names])
```

```python
import functools
import math

import numpy as np
import jax
import jax.numpy as jnp
from jax import lax
from jax.experimental import pallas as pl
from jax.experimental.pallas import tpu as pltpu

F32 = jnp.float32
BF16 = jnp.bfloat16
MESH = pl.DeviceIdType.MESH

N_DEV = 8
N_CHIP = 4
HEAD_DIM = 64
BLK = 128
A_Q_HEADS = 8
A_KV_HEADS = 2
A_GROUP = A_Q_HEADS // A_KV_HEADS
B_HEADS = 8
REL_BUCKETS = 32
REL_MAX_EXACT = 16
REL_MAX_DIST = 128
LRU_BLOCKS = 8
LRU_C = 8.0
CONV_WIDTH = 4
EPS = 1e-6
NEG = -1e30
SCALE = HEAD_DIM ** -0.5
LANE = 128
SUBLANE = 8
VMEM_LIMIT = 56 * 1024 * 1024
SCAN_CHUNK = 256
ROW_TILE = 512
FOX_BQ = 512
ADAM_LR = 0.001
ADAM_B1 = 0.9
ADAM_B2 = 0.999
ADAM_EPS = 1e-08
ADAM_WD = 0.01
ADAM_STEP = 10
HI = lax.Precision.HIGHEST


def _cp(sem=None):
    return pltpu.CompilerParams(dimension_semantics=sem, vmem_limit_bytes=VMEM_LIMIT)


def _dot(a, b):
    return jnp.dot(a, b, preferred_element_type=F32)


def _dot_nt(a, b):
    return lax.dot_general(a, b, (((1,), (1,)), ((), ())), preferred_element_type=F32)


def _dot_tn(a, b):
    return lax.dot_general(a, b, (((0,), (0,)), ((), ())), preferred_element_type=F32)


def _sigmoid(z):
    return 1.0 / (1.0 + jnp.exp(-z))


def _row_tile(rows, cap):
    if rows <= cap:
        return rows
    best = SUBLANE
    t = SUBLANE
    while t <= cap:
        if rows % t == 0:
            best = t
        t += SUBLANE
    return best


def _all_gather8(x_shard, name, space):
    m_per, n = x_shard.shape
    n_own = 8 if (space == pltpu.HBM and m_per % 128 == 0) else 1
    own_rows = m_per // n_own

    def body(x_ref, out_ref, send_sems, recv_sems, local_sems):
        x, y, c = lax.axis_index("x"), lax.axis_index("y"), lax.axis_index("c")
        me, sibling = (x, y, c), (x, y, 1 - c)
        chips = [(1 - x, y), (x, 1 - y), (1 - x, 1 - y)]

        def rows(px, py, pc):
            return out_ref.at[pl.ds((4 * px + 2 * py + pc) * m_per, m_per), :]

        def copy(k, block, to, src=None):
            return pltpu.make_async_remote_copy(
                src_ref=rows(*block) if src is None else src, dst_ref=rows(*block),
                send_sem=send_sems.at[k], recv_sem=recv_sems.at[k], device_id=to, device_id_type=MESH)

        base = (4 * x + 2 * y + c) * m_per
        mine = [pltpu.make_async_copy(x_ref.at[pl.ds(i * own_rows, own_rows), :],
                                      out_ref.at[pl.ds(base + i * own_rows, own_rows), :], local_sems.at[i])
                for i in range(n_own)]
        for cp in mine:
            cp.start()
        first = [copy(0, me, sibling, src=x_ref)]
        first += [copy(1 + j, me, (*chip, c), src=x_ref) for j, chip in enumerate(chips)]
        for cp in first:
            cp.start()
        passed = [copy(4 + j, (*chip, c), sibling) for j, chip in enumerate(chips)]
        for j, chip in enumerate(chips):
            copy(1 + j, (*chip, c), me).wait_recv()
            passed[j].start()
        copy(0, sibling, me).wait_recv()
        for j, chip in enumerate(chips):
            copy(4 + j, (*chip, 1 - c), me).wait_recv()
        for cp in first + passed:
            cp.wait_send()
        for cp in mine:
            cp.wait()

    return pl.pallas_call(
        body, name=name,
        out_shape=jax.ShapeDtypeStruct((N_DEV * m_per, n), x_shard.dtype),
        in_specs=[pl.BlockSpec(memory_space=space)],
        out_specs=pl.BlockSpec(memory_space=space),
        scratch_shapes=[pltpu.SemaphoreType.DMA((7,)), pltpu.SemaphoreType.DMA((7,)),
                        pltpu.SemaphoreType.DMA((n_own,))],
        compiler_params=pltpu.CompilerParams(vmem_limit_bytes=VMEM_LIMIT),
    )(x_shard)


def _sibling_push(blocks, pick_other, name):
    nblk = blocks.shape[0]
    m, n = blocks.shape[-2:]

    def body(x_ref, out_ref, send_sems, recv_sems):
        x, y, c = lax.axis_index("x"), lax.axis_index("y"), lax.axis_index("c")
        copies = []
        for k in range(nblk):
            src = x_ref.at[k, 1 - c] if pick_other else x_ref.at[k]
            copies.append(pltpu.make_async_remote_copy(
                src_ref=src, dst_ref=out_ref.at[k], send_sem=send_sems.at[k], recv_sem=recv_sems.at[k],
                device_id=(x, y, 1 - c), device_id_type=MESH))
        for cp in copies:
            cp.start()
        for cp in copies:
            cp.wait_recv()
        for cp in copies:
            cp.wait_send()

    hbm = pl.BlockSpec(memory_space=pltpu.HBM)
    return pl.pallas_call(
        body, name=name,
        out_shape=jax.ShapeDtypeStruct((nblk, m, n), blocks.dtype),
        in_specs=[hbm], out_specs=hbm,
        scratch_shapes=[pltpu.SemaphoreType.DMA((nblk,)), pltpu.SemaphoreType.DMA((nblk,))],
    )(blocks)


def _chip_all_to_all(parts, name):
    _, m, n = parts.shape

    def body(x_ref, out_ref, send_sems, recv_sems, local_sem):
        x, y, c = lax.axis_index("x"), lax.axis_index("y"), lax.axis_index("c")
        me = 2 * x + y
        mine = pltpu.make_async_copy(x_ref.at[me], out_ref.at[me], local_sem)
        mine.start()
        copies = []
        for k in range(1, N_CHIP):
            px, py = x ^ ((k >> 1) & 1), y ^ (k & 1)
            copies.append(pltpu.make_async_remote_copy(
                src_ref=x_ref.at[2 * px + py], dst_ref=out_ref.at[me],
                send_sem=send_sems.at[k - 1], recv_sem=recv_sems.at[k - 1],
                device_id=(px, py, c), device_id_type=MESH))
        for cp in copies:
            cp.start()
        for cp in copies:
            cp.wait_recv()
        for cp in copies:
            cp.wait_send()
        mine.wait()

    hbm = pl.BlockSpec(memory_space=pltpu.HBM)
    return pl.pallas_call(
        body, name=name,
        out_shape=jax.ShapeDtypeStruct(parts.shape, parts.dtype),
        in_specs=[hbm], out_specs=hbm,
        scratch_shapes=[pltpu.SemaphoreType.DMA((N_CHIP - 1,)), pltpu.SemaphoreType.DMA((N_CHIP - 1,)),
                        pltpu.SemaphoreType.DMA],
    )(parts)


def _pair_sum(core, pieces, theirs, name):
    nblk, _, m, n = pieces.shape
    tr = _row_tile(m, 536)

    def body(c_ref, p_ref, t_ref, o_ref):
        o_ref[...] = (p_ref[...] + t_ref[...]).astype(BF16)

    return pl.pallas_call(
        body, name=name,
        grid_spec=pltpu.PrefetchScalarGridSpec(
            num_scalar_prefetch=1, grid=(nblk, m // tr),
            in_specs=[pl.BlockSpec((None, None, tr, n), lambda k, i, c_ref: (k, c_ref[0], i, 0)),
                      pl.BlockSpec((None, tr, n), lambda k, i, c_ref: (k, i, 0))],
            out_specs=pl.BlockSpec((None, tr, n), lambda k, i, c_ref: (k, i, 0))),
        out_shape=jax.ShapeDtypeStruct((nblk, m, n), BF16),
        compiler_params=_cp(("parallel", "parallel")),
    )(core, pieces, theirs)


def _sum_slots(slots, name):
    k, m, n = slots.shape
    tr = _row_tile(m, 536)

    def body(s_ref, o_ref):
        acc = s_ref[0].astype(F32)
        for j in range(1, k):
            acc = acc + s_ref[j].astype(F32)
        o_ref[...] = acc

    return pl.pallas_call(
        body, name=name, grid=(m // tr,),
        out_shape=jax.ShapeDtypeStruct((m, n), F32),
        in_specs=[pl.BlockSpec((k, tr, n), lambda i: (0, i, 0))],
        out_specs=pl.BlockSpec((tr, n), lambda i: (i, 0)),
        compiler_params=_cp(("parallel",)),
    )(slots)


def _ada_fwd(c_all, w, b, name):
    r, _ = c_all.shape
    n = w.shape[1]

    def body(c_ref, w_ref, b_ref, o_ref):
        cv = c_ref[...]
        act = cv * _sigmoid(cv)
        o_ref[...] = jnp.dot(act, w_ref[...], precision=HI, preferred_element_type=F32) + b_ref[...]

    return pl.pallas_call(body, name=name, out_shape=jax.ShapeDtypeStruct((r, n), F32),
                          compiler_params=_cp())(c_all, w, b)


def _ada_bwd(c_all, dmod_chip, dmod_all, name):
    r, d = c_all.shape
    nl, _, n = dmod_chip.shape

    def body(c_ref, dm_ref, da_ref, gw_ref, gb_ref):
        cv = c_ref[...]
        act = cv * _sigmoid(cv)
        for l in range(nl):
            gw_ref[l] = lax.dot_general(act, dm_ref[l], (((0,), (0,)), ((), ())), precision=HI,
                                        preferred_element_type=F32)
        gb_ref[...] = jnp.sum(da_ref[...], axis=0, keepdims=True)

    return pl.pallas_call(
        body, name=name,
        out_shape=(jax.ShapeDtypeStruct((nl, d, n), F32), jax.ShapeDtypeStruct((1, dmod_all.shape[1]), F32)),
        compiler_params=_cp())(c_all, dmod_chip, dmod_all)


def _norm_proj(x, g, scale, shift, w, f32_cols, name):
    b, s, d = x.shape
    n = w.shape[1]
    tm = min(s, ROW_TILE)

    def body(x_ref, g_ref, sc_ref, sh_ref, w_ref, proj_ref, h_ref, *aux_ref):
        xv = x_ref[...]
        rstd = lax.rsqrt(jnp.mean(xv * xv, axis=-1, keepdims=True) + EPS)
        h = (xv * rstd) * g_ref[...] * (1.0 + sc_ref[...]) + sh_ref[...]
        hb = h.astype(BF16)
        h_ref[...] = hb
        proj = _dot(hb, w_ref[...])
        proj_ref[...] = proj.astype(BF16)
        if f32_cols:
            aux_ref[0][...] = proj[:, n - f32_cols:]

    row = lambda i, j: (i, j, 0)
    out_shape = [jax.ShapeDtypeStruct((b, s, n), BF16), jax.ShapeDtypeStruct((b, s, d), BF16)]
    out_specs = [pl.BlockSpec((None, tm, n), row), pl.BlockSpec((None, tm, d), row)]
    if f32_cols:
        out_shape.append(jax.ShapeDtypeStruct((b, s, f32_cols), F32))
        out_specs.append(pl.BlockSpec((None, tm, f32_cols), row))
    return pl.pallas_call(
        body, name=name, grid=(b, s // tm),
        out_shape=tuple(out_shape),
        in_specs=[pl.BlockSpec((None, tm, d), row),
                  pl.BlockSpec((1, d), lambda i, j: (0, 0)),
                  pl.BlockSpec((None, 1, d), lambda i, j: (i, 0, 0)),
                  pl.BlockSpec((None, 1, d), lambda i, j: (i, 0, 0)),
                  pl.BlockSpec((d, n), lambda i, j: (0, 0))],
        out_specs=tuple(out_specs),
        compiler_params=_cp(("parallel", "parallel")),
    )(x, g, scale, shift, w)


def _cat_refs(refs):
    vals = [r[...] for r in refs]
    return vals[0] if len(vals) == 1 else jnp.concatenate(vals, axis=-1)


def _gate_outproj(mix_parts, proj, gate_blk, w_out, x, gmod, name):
    b, s, _ = x.shape
    wd, d = w_out.shape
    tm = min(s, ROW_TILE)
    npart = len(mix_parts)

    def body(*refs):
        mix_refs = refs[:npart]
        gate_ref, w_ref, x_ref, gm_ref, xo_ref, o_ref = refs[npart:]
        gt = gate_ref[...].astype(F32)
        y = (_cat_refs(mix_refs) * (gt * _sigmoid(gt))).astype(BF16)
        o = _dot(y, w_ref[...])
        o_ref[...] = o.astype(BF16)
        xo_ref[...] = x_ref[...] + gm_ref[...] * o

    return pl.pallas_call(
        body, name=name, grid=(b, s // tm),
        out_shape=(jax.ShapeDtypeStruct((b, s, d), F32), jax.ShapeDtypeStruct((b, s, d), BF16)),
        in_specs=[pl.BlockSpec((None, tm, p.shape[2]), lambda i, j: (i, j, 0)) for p in mix_parts] + [
                  pl.BlockSpec((None, tm, wd), lambda i, j: (i, j, gate_blk)),
                  pl.BlockSpec((wd, d), lambda i, j: (0, 0)),
                  pl.BlockSpec((None, tm, d), lambda i, j: (i, j, 0)),
                  pl.BlockSpec((None, 1, d), lambda i, j: (i, 0, 0))],
        out_specs=(pl.BlockSpec((None, tm, d), lambda i, j: (i, j, 0)),
                   pl.BlockSpec((None, tm, d), lambda i, j: (i, j, 0))),
        compiler_params=_cp(("parallel", "parallel")),
    )(*mix_parts, proj, w_out, x, gmod)


def _final_loss(x, g, target, name):
    b, s, d = x.shape
    tm = min(s, ROW_TILE)

    def body(x_ref, g_ref, t_ref, loss_ref, dx_ref, dg_ref):
        first = jnp.logical_and(pl.program_id(0) == 0, pl.program_id(1) == 0)

        @pl.when(first)
        def _():
            loss_ref[...] = jnp.zeros_like(loss_ref)
            dg_ref[...] = jnp.zeros_like(dg_ref)

        xv = x_ref[...]
        gv = g_ref[...]
        rstd = lax.rsqrt(jnp.mean(xv * xv, axis=-1, keepdims=True) + EPS)
        xhat = xv * rstd
        err = xhat * gv - t_ref[...]
        row = jnp.mean(err * err, axis=-1, keepdims=True)
        loss_ref[...] += 0.5 * jnp.sum(row, axis=0, keepdims=True)
        dy = err * (1.0 / d)
        dg_ref[...] += jnp.sum(dy * xhat, axis=0, keepdims=True)
        dxh = dy * gv
        dx_ref[...] = rstd * (dxh - xhat * jnp.mean(dxh * xhat, axis=-1, keepdims=True))

    return pl.pallas_call(
        body, name=name, grid=(b, s // tm),
        out_shape=(jax.ShapeDtypeStruct((1, LANE), F32), jax.ShapeDtypeStruct((b, s, d), F32),
                   jax.ShapeDtypeStruct((1, d), F32)),
        in_specs=[pl.BlockSpec((None, tm, d), lambda i, j: (i, j, 0)),
                  pl.BlockSpec((1, d), lambda i, j: (0, 0)),
                  pl.BlockSpec((None, tm, d), lambda i, j: (i, j, 0))],
        out_specs=(pl.BlockSpec((1, LANE), lambda i, j: (0, 0)),
                   pl.BlockSpec((None, tm, d), lambda i, j: (i, j, 0)),
                   pl.BlockSpec((1, d), lambda i, j: (0, 0))),
        compiler_params=_cp(("arbitrary", "arbitrary")),
    )(x, g, target)


def _bwd_out(dxo, gmod, o, mix_parts, proj, gate_blk, w_out_t, dmix_dtype, name):
    b, s, d = dxo.shape
    wd = w_out_t.shape[1]
    tm = min(s, ROW_TILE)
    npart = len(mix_parts)

    def body(dx_ref, gm_ref, o_ref, *refs):
        mix_refs = refs[:npart]
        gate_ref, wt_ref, dmix_ref, dgate_ref, do_ref, y_ref, dgm_ref = refs[npart:]

        @pl.when(pl.program_id(1) == 0)
        def _():
            dgm_ref[...] = jnp.zeros_like(dgm_ref)

        dx = dx_ref[...]
        dgm_ref[...] += jnp.sum(dx * o_ref[...].astype(F32), axis=0, keepdims=True)
        dob = (gm_ref[...] * dx).astype(BF16)
        do_ref[...] = dob
        dy = _dot(dob, wt_ref[...])
        gt = gate_ref[...].astype(F32)
        sg = _sigmoid(gt)
        silu = gt * sg
        mx = _cat_refs(mix_refs)
        y_ref[...] = (mx * silu).astype(BF16)
        dmix_ref[...] = (dy * silu).astype(dmix_dtype)
        dgate_ref[...] = (dy * mx * (sg * (1.0 + gt * (1.0 - sg)))).astype(BF16)

    row = lambda i, j: (i, j, 0)
    return pl.pallas_call(
        body, name=name, grid=(b, s // tm),
        out_shape=(jax.ShapeDtypeStruct((b, s, wd), dmix_dtype), jax.ShapeDtypeStruct((b, s, wd), BF16),
                   jax.ShapeDtypeStruct((b, s, d), BF16), jax.ShapeDtypeStruct((b, s, wd), BF16),
                   jax.ShapeDtypeStruct((b, 1, d), F32)),
        in_specs=[pl.BlockSpec((None, tm, d), row),
                  pl.BlockSpec((None, 1, d), lambda i, j: (i, 0, 0)),
                  pl.BlockSpec((None, tm, d), row)] + [
                  pl.BlockSpec((None, tm, p.shape[2]), row) for p in mix_parts] + [
                  pl.BlockSpec((None, tm, wd), lambda i, j: (i, j, gate_blk)),
                  pl.BlockSpec((d, wd), lambda i, j: (0, 0))],
        out_specs=(pl.BlockSpec((None, tm, wd), row), pl.BlockSpec((None, tm, wd), row),
                   pl.BlockSpec((None, tm, d), row), pl.BlockSpec((None, tm, wd), row),
                   pl.BlockSpec((None, 1, d), lambda i, j: (i, 0, 0))),
        compiler_params=_cp(("parallel", "arbitrary")),
    )(dxo, gmod, o, *mix_parts, proj, w_out_t)


def _bwd_in(dproj_parts, w_in_t, x, g, scale, dxo, name):
    b, s, d = x.shape
    n = w_in_t.shape[0]
    tm = min(s, ROW_TILE)
    npart = len(dproj_parts)

    def body(*refs):
        dp_refs = refs[:npart]
        wt_ref, x_ref, g_ref, sc_ref, dxo_ref, dx_ref, dsh_ref, dsc_ref, dg_ref = refs[npart:]

        @pl.when(jnp.logical_and(pl.program_id(0) == 0, pl.program_id(1) == 0))
        def _():
            dg_ref[...] = jnp.zeros_like(dg_ref)

        @pl.when(pl.program_id(1) == 0)
        def _():
            dsh_ref[...] = jnp.zeros_like(dsh_ref)
            dsc_ref[...] = jnp.zeros_like(dsc_ref)

        dh = _dot(_cat_refs(dp_refs), wt_ref[...])
        xv = x_ref[...]
        gv = g_ref[...]
        one_sc = 1.0 + sc_ref[...]
        rstd = lax.rsqrt(jnp.mean(xv * xv, axis=-1, keepdims=True) + EPS)
        xhat = xv * rstd
        dsh_ref[...] += jnp.sum(dh, axis=0, keepdims=True)
        dsc_ref[...] += jnp.sum(dh * (xhat * gv), axis=0, keepdims=True)
        dhs = dh * one_sc
        dg_ref[...] += jnp.sum(dhs * xhat, axis=0, keepdims=True)
        dxh = dhs * gv
        dx_ref[...] = dxo_ref[...] + rstd * (dxh - xhat * jnp.mean(dxh * xhat, axis=-1, keepdims=True))

    row = lambda i, j: (i, j, 0)
    per_b = lambda i, j: (i, 0, 0)
    return pl.pallas_call(
        body, name=name, grid=(b, s // tm),
        out_shape=(jax.ShapeDtypeStruct((b, s, d), F32), jax.ShapeDtypeStruct((b, 1, d), F32),
                   jax.ShapeDtypeStruct((b, 1, d), F32), jax.ShapeDtypeStruct((1, d), F32)),
        in_specs=[pl.BlockSpec((None, tm, p.shape[2]), row) for p in dproj_parts] + [
                  pl.BlockSpec((n, d), lambda i, j: (0, 0)),
                  pl.BlockSpec((None, tm, d), row),
                  pl.BlockSpec((1, d), lambda i, j: (0, 0)),
                  pl.BlockSpec((None, 1, d), per_b),
                  pl.BlockSpec((None, tm, d), row)],
        out_specs=(pl.BlockSpec((None, tm, d), row), pl.BlockSpec((None, 1, d), per_b),
                   pl.BlockSpec((None, 1, d), per_b), pl.BlockSpec((1, d), lambda i, j: (0, 0))),
        compiler_params=_cp(("arbitrary", "arbitrary")),
    )(*dproj_parts, w_in_t, x, g, scale, dxo)


def _matmul_tn(a, b_parts, name):
    bsz, s, m = a.shape
    n = sum(p.shape[2] for p in b_parts)
    tk = next(c for c in (512, 256, 128) if s % c == 0)
    npart = len(b_parts)

    def body(a_ref, *refs):
        b_refs, o_ref = refs[:npart], refs[npart]

        @pl.when(jnp.logical_and(pl.program_id(0) == 0, pl.program_id(1) == 0))
        def _():
            o_ref[...] = jnp.zeros_like(o_ref)

        o_ref[...] += _dot_tn(a_ref[...], _cat_refs(b_refs))

    row = lambda i, k: (i, k, 0)
    return pl.pallas_call(
        body, name=name, grid=(bsz, s // tk),
        out_shape=jax.ShapeDtypeStruct((m, n), F32),
        in_specs=[pl.BlockSpec((None, tk, m), row)] + [pl.BlockSpec((None, tk, p.shape[2]), row) for p in b_parts],
        out_specs=pl.BlockSpec((m, n), lambda i, k: (0, 0)),
        compiler_params=_cp(("arbitrary", "arbitrary")),
    )(a, *b_parts)


def _rel_buckets():
    qi = np.arange(BLK)[:, None]
    kj = np.arange(2 * BLK)[None, :]
    rel = qi - kj + BLK
    n = np.maximum(rel, 0)
    nf = np.maximum(n, 1).astype(np.float32)
    large = REL_MAX_EXACT + (np.log(nf / REL_MAX_EXACT) / math.log(REL_MAX_DIST / REL_MAX_EXACT)
                             * (REL_BUCKETS - REL_MAX_EXACT)).astype(np.int32)
    large = np.minimum(large, REL_BUCKETS - 1)
    bucket = np.where(n < REL_MAX_EXACT, n, large).astype(np.int32)
    valid = ((rel >= 0) & (rel < BLK)).astype(np.int32)
    return bucket, valid


def _swa_bias(rel_bias_t, bucket, valid, name):
    nh = rel_bias_t.shape[0]

    def body(rb_ref, bk_ref, vl_ref, o_ref):
        h = pl.program_id(0)
        bk = bk_ref[...]
        acc = jnp.zeros(bk.shape, F32)
        for i in range(REL_BUCKETS):
            acc = jnp.where(bk == i, rb_ref[h, i], acc)
        o_ref[...] = jnp.where(vl_ref[...] > 0, acc, NEG)

    return pl.pallas_call(
        body, name=name, grid=(nh,),
        out_shape=jax.ShapeDtypeStruct((nh, BLK, 2 * BLK), F32),
        in_specs=[pl.BlockSpec(memory_space=pltpu.SMEM),
                  pl.BlockSpec((BLK, 2 * BLK), lambda h: (0, 0)),
                  pl.BlockSpec((BLK, 2 * BLK), lambda h: (0, 0))],
        out_specs=pl.BlockSpec((None, BLK, 2 * BLK), lambda h: (h, 0, 0)),
        compiler_params=_cp(("arbitrary",)),
    )(rel_bias_t, bucket, valid)


def _swa_scores(n, q, kw, bias_ref):
    sc = _dot_nt(q, kw) * SCALE + bias_ref[...]
    second = lax.broadcasted_iota(jnp.int32, sc.shape, 1) >= BLK
    return jnp.where(jnp.logical_or(n > 0, second), sc, NEG)


def _pad_front(dst_ref, src_ref):
    dst_ref[0:BLK, :] = jnp.zeros((BLK, dst_ref.shape[1]), dst_ref.dtype)
    dst_ref[BLK:, :] = src_ref[...]


def _swa_fwd(q, k, v, bias, sinks, name):
    b, hkv, nb, rows, hd = q.shape
    s = nb * BLK

    def body(q_ref, k_ref, v_ref, bias_ref, sink_ref, o_ref, l_ref, kpad_ref, vpad_ref):
        _pad_front(kpad_ref, k_ref)
        _pad_front(vpad_ref, v_ref)
        sink = sink_ref[...]

        def step(n, carry):
            w0 = pl.multiple_of(n * BLK, BLK)
            sc = _swa_scores(n, q_ref[n], kpad_ref[pl.ds(w0, 2 * BLK), :], bias_ref)
            m = jnp.maximum(jnp.max(sc, axis=1, keepdims=True), sink)
            e = jnp.exp(sc - m)
            den = jnp.sum(e, axis=1, keepdims=True) + jnp.exp(sink - m)
            o_ref[n] = _dot((e * (1.0 / den)).astype(BF16), vpad_ref[pl.ds(w0, 2 * BLK), :])
            l_ref[n] = m + jnp.log(den)
            return carry

        lax.fori_loop(0, nb, step, 0)

    qspec = pl.BlockSpec((None, None, nb, rows, hd), lambda i, kv: (i, kv, 0, 0, 0))
    kspec = pl.BlockSpec((None, None, s, hd), lambda i, kv: (i, kv, 0, 0))
    return pl.pallas_call(
        body, name=name, grid=(b, hkv),
        out_shape=(jax.ShapeDtypeStruct((b, hkv, nb, rows, hd), F32), jax.ShapeDtypeStruct((b, hkv, nb, rows, 1), F32)),
        in_specs=[qspec, kspec, kspec,
                  pl.BlockSpec((None, rows, 2 * BLK), lambda i, kv: (kv, 0, 0)),
                  pl.BlockSpec((None, rows, 1), lambda i, kv: (kv, 0, 0))],
        out_specs=(qspec, pl.BlockSpec((None, None, nb, rows, 1), lambda i, kv: (i, kv, 0, 0, 0))),
        scratch_shapes=[pltpu.VMEM((s + BLK, hd), BF16), pltpu.VMEM((s + BLK, hd), BF16)],
        compiler_params=_cp(("parallel", "parallel")),
    )(q, k, v, bias, sinks)


def _swa_bwd(q, k, v, bias, sinks, do, lse, name):
    b, hkv, nb, rows, hd = q.shape
    s = nb * BLK

    def body(q_ref, k_ref, v_ref, bias_ref, sink_ref, do_ref, l_ref,
             dq_ref, dk_ref, dv_ref, db_ref, dsk_ref, kpad_ref, vpad_ref, dkpad_ref, dvpad_ref):
        _pad_front(kpad_ref, k_ref)
        _pad_front(vpad_ref, v_ref)
        dkpad_ref[...] = jnp.zeros_like(dkpad_ref)
        dvpad_ref[...] = jnp.zeros_like(dvpad_ref)
        db_ref[...] = jnp.zeros_like(db_ref)
        sink = sink_ref[...]

        def step(n, dsink):
            w0 = pl.multiple_of(n * BLK, BLK)
            win = pl.ds(w0, 2 * BLK)
            qn = q_ref[n]
            kw = kpad_ref[win, :]
            ln = l_ref[n]
            p = jnp.exp(_swa_scores(n, qn, kw, bias_ref) - ln)
            dob = do_ref[n]
            dp = _dot_nt(dob, vpad_ref[win, :])
            delta = jnp.sum(p * dp, axis=1, keepdims=True)
            ds = p * (dp - delta)
            db_ref[...] += ds
            dsb = ds.astype(BF16)
            dq_ref[n] = (_dot(dsb, kw) * SCALE).astype(BF16)
            dkpad_ref[win, :] += _dot_tn(dsb, qn)
            dvpad_ref[win, :] += _dot_tn(p.astype(BF16), dob)
            return dsink - jnp.exp(sink - ln) * delta

        dsink = lax.fori_loop(0, nb, step, jnp.zeros((rows, 1), F32))
        for g in range(A_GROUP):
            tot = jnp.sum(dsink[g * BLK:(g + 1) * BLK, :], axis=0, keepdims=True)
            dsk_ref[g] = jnp.broadcast_to(tot, (1, LANE))
        dk_ref[...] = (dkpad_ref[BLK:, :] * SCALE).astype(BF16)
        dv_ref[...] = dvpad_ref[BLK:, :].astype(BF16)

    qspec = pl.BlockSpec((None, None, nb, rows, hd), lambda i, kv: (i, kv, 0, 0, 0))
    kspec = pl.BlockSpec((None, None, s, hd), lambda i, kv: (i, kv, 0, 0))
    return pl.pallas_call(
        body, name=name, grid=(b, hkv),
        out_shape=(jax.ShapeDtypeStruct((b, hkv, nb, rows, hd), BF16), jax.ShapeDtypeStruct((b, hkv, s, hd), BF16),
                   jax.ShapeDtypeStruct((b, hkv, s, hd), BF16), jax.ShapeDtypeStruct((b, hkv, rows, 2 * BLK), F32),
                   jax.ShapeDtypeStruct((b, hkv, A_GROUP, 1, LANE), F32)),
        in_specs=[qspec, kspec, kspec,
                  pl.BlockSpec((None, rows, 2 * BLK), lambda i, kv: (kv, 0, 0)),
                  pl.BlockSpec((None, rows, 1), lambda i, kv: (kv, 0, 0)),
                  qspec,
                  pl.BlockSpec((None, None, nb, rows, 1), lambda i, kv: (i, kv, 0, 0, 0))],
        out_specs=(qspec, kspec, kspec,
                   pl.BlockSpec((None, None, rows, 2 * BLK), lambda i, kv: (i, kv, 0, 0)),
                   pl.BlockSpec((None, None, A_GROUP, 1, LANE), lambda i, kv: (i, kv, 0, 0, 0))),
        scratch_shapes=[pltpu.VMEM((s + BLK, hd), BF16), pltpu.VMEM((s + BLK, hd), BF16),
                        pltpu.VMEM((s + BLK, hd), F32), pltpu.VMEM((s + BLK, hd), F32)],
        compiler_params=_cp(("parallel", "parallel")),
    )(q, k, v, bias, sinks, do, lse)


def _swa_small_grads(db, dsk, bucket, name):
    b, nh = db.shape[0], db.shape[1]

    def body(db_ref, dsk_ref, bk_ref, gb_ref, gs_ref):
        acc = db_ref[0]
        sk = dsk_ref[0]
        for i in range(1, b):
            acc = acc + db_ref[i]
            sk = sk + dsk_ref[i]
        gs_ref[...] = sk
        bk = bk_ref[...]
        for i in range(REL_BUCKETS):
            part = jnp.sum(jnp.where(bk == i, acc, 0.0), axis=1, keepdims=True)
            tot = jnp.sum(part, axis=0, keepdims=True)
            gb_ref[i:i + 1, :] = jnp.broadcast_to(tot, (1, LANE))

    return pl.pallas_call(
        body, name=name, grid=(nh,),
        out_shape=(jax.ShapeDtypeStruct((nh, REL_BUCKETS, LANE), F32), jax.ShapeDtypeStruct((nh, 1, LANE), F32)),
        in_specs=[pl.BlockSpec((b, None, BLK, 2 * BLK), lambda h: (0, h, 0, 0)),
                  pl.BlockSpec((b, None, 1, LANE), lambda h: (0, h, 0, 0)),
                  pl.BlockSpec((BLK, 2 * BLK), lambda h: (0, 0))],
        out_specs=(pl.BlockSpec((None, REL_BUCKETS, LANE), lambda h: (h, 0, 0)),
                   pl.BlockSpec((None, 1, LANE), lambda h: (h, 0, 0))),
        compiler_params=_cp(("parallel",)),
    )(db, dsk, bucket)


def _log_sigmoid(z):
    return jnp.minimum(z, 0.0) - jnp.log(1.0 + jnp.exp(-jnp.abs(z)))


def _fox_decay(z, bf, name):
    b, s, w = z.shape
    nb = s // BLK

    def body(z_ref, bf_ref, f_ref):
        r = lax.broadcasted_iota(jnp.int32, (BLK, BLK), 0)
        c = lax.broadcasted_iota(jnp.int32, (BLK, BLK), 1)
        tri = (c <= r).astype(F32)

        def step(n, carry):
            r0 = pl.multiple_of(n * BLK, BLK)
            lf = _log_sigmoid(z_ref[pl.ds(r0, BLK), :] + bf_ref[...])
            f_ref[pl.ds(r0, BLK), :] = jnp.dot(tri, lf, precision=HI, preferred_element_type=F32) + carry
            return carry + jnp.sum(lf, axis=0, keepdims=True)

        lax.fori_loop(0, nb, step, jnp.zeros((1, w), F32))

    spec = pl.BlockSpec((None, s, w), lambda i: (i, 0, 0))
    return pl.pallas_call(
        body, name=name, grid=(b,), out_shape=jax.ShapeDtypeStruct((b, s, w), F32),
        in_specs=[spec, pl.BlockSpec((1, w), lambda i: (0, 0))], out_specs=spec,
        compiler_params=_cp(("parallel",)),
    )(z, bf)


def _fox_dgate(df, z, bf, nheads, name):
    b, s, w = z.shape
    nb = s // BLK

    def body(df_ref, z_ref, bf_ref, dz_ref, dbf_ref):
        @pl.when(pl.program_id(0) == 0)
        def _():
            dbf_ref[...] = jnp.zeros_like(dbf_ref)

        r = lax.broadcasted_iota(jnp.int32, (BLK, BLK), 0)
        c = lax.broadcasted_iota(jnp.int32, (BLK, BLK), 1)
        tri = (c >= r).astype(F32)
        lane = lax.broadcasted_iota(jnp.int32, (BLK, w), 1)

        def step(i, carry):
            tail, dbf = carry
            r0 = pl.multiple_of((nb - 1 - i) * BLK, BLK)
            dfb = df_ref[pl.ds(r0, BLK), :]
            dlf = jnp.dot(tri, dfb, precision=HI, preferred_element_type=F32) + tail
            dz = jnp.where(lane < nheads, dlf * _sigmoid(-(z_ref[pl.ds(r0, BLK), :] + bf_ref[...])), 0.0)
            dz_ref[pl.ds(r0, BLK), :] = dz
            return tail + jnp.sum(dfb, axis=0, keepdims=True), dbf + jnp.sum(dz, axis=0, keepdims=True)

        zero = jnp.zeros((1, w), F32)
        _, dbf = lax.fori_loop(0, nb, step, (zero, zero))
        dbf_ref[...] += dbf

    spec = pl.BlockSpec((None, s, w), lambda i: (i, 0, 0))
    one = pl.BlockSpec((1, w), lambda i: (0, 0))
    return pl.pallas_call(
        body, name=name, grid=(b,),
        out_shape=(jax.ShapeDtypeStruct((b, s, w), F32), jax.ShapeDtypeStruct((1, w), F32)),
        in_specs=[spec, spec, one], out_specs=(spec, one),
        compiler_params=_cp(("arbitrary",)),
    )(df, z, bf)


def _lane_columns(src_ref, dst_ref, pair):
    val = src_ref[...]
    lane = lax.broadcasted_iota(jnp.int32, val.shape, 1)
    for hh in range(2):
        dst_ref[hh] = jnp.sum(jnp.where(lane == 2 * pair + hh, val, 0.0), axis=1, keepdims=True)


def _fox_segments(nb):
    per = max(1, nb // 4)
    return per, nb // per


def _head_masks(shape, axis):
    idx = lax.broadcasted_iota(jnp.int32, shape, axis)
    return idx < HEAD_DIM, idx >= HEAD_DIM


def _fox_fwd(proj, qblk, kblk, vblk, fsum, frow, name):
    b, s, _ = proj.shape
    nh = frow.shape[1]
    npair = nh // 2
    BLK = min(s, FOX_BQ)
    assert s % BLK == 0
    per, nseg = _fox_segments(s // BLK)

    def body(q_ref, k_ref, v_ref, fs_ref, fr_ref, o_ref, l_ref, qm_ref, kt_ref, vb_ref, fc_ref):
        pair = pl.program_id(1)
        _lane_columns(fs_ref, fc_ref, pair)

        @pl.when(pair == 0)
        def _():
            l_ref[...] = jnp.zeros_like(l_ref)

        lo, hi = _head_masks((s, LANE), 1)
        qv = q_ref[...].astype(F32) * SCALE
        qm_ref[0] = jnp.where(lo, qv, 0.0).astype(BF16)
        qm_ref[1] = jnp.where(hi, qv, 0.0).astype(BF16)
        kt_ref[...] = k_ref[...].astype(F32).T.astype(BF16)
        vb_ref[...] = v_ref[...].astype(BF16)
        lane = lax.broadcasted_iota(jnp.int32, (BLK, LANE), 1)
        lane_lo = lane < HEAD_DIM
        tail = per * BLK
        causal = (lax.broadcasted_iota(jnp.int32, (BLK, tail), 1)
                  - lax.broadcasted_iota(jnp.int32, (BLK, tail), 0))
        for seg in range(nseg):
            w = (seg + 1) * tail

            def qstep(n, carry):
                r0 = pl.multiple_of(n * BLK, BLK)
                outs = []
                lse = l_ref[pl.ds(r0, BLK), :]
                for hh in range(2):
                    sc = _dot(qm_ref[hh, pl.ds(r0, BLK), :], kt_ref[:, :w])
                    sc = sc + (fc_ref[hh, pl.ds(r0, BLK), :] - fr_ref[hh, :, :w])
                    masked = jnp.where(causal <= (n - seg * per) * BLK, sc[:, w - tail:], NEG)
                    sc = masked if seg == 0 else jnp.concatenate([sc[:, :w - tail], masked], axis=1)
                    m = jnp.max(sc, axis=1, keepdims=True)
                    e = jnp.exp(sc - m)
                    l = jnp.sum(e, axis=1, keepdims=True)
                    outs.append(_dot((e * (1.0 / l)).astype(BF16), vb_ref[:w, :]))
                    lse = jnp.where(lane == 2 * pair + hh, m + jnp.log(l), lse)
                l_ref[pl.ds(r0, BLK), :] = lse
                o_ref[pl.ds(r0, BLK), :] = jnp.where(lane_lo, outs[0], outs[1])
                return carry

            lax.fori_loop(seg * per, (seg + 1) * per, qstep, 0)

    def tok(blk):
        return pl.BlockSpec((None, s, LANE), lambda i, p: (i, 0, blk + p))

    whole = pl.BlockSpec((None, s, LANE), lambda i, p: (i, 0, 0))
    rowspec = pl.BlockSpec((None, 2, 1, s), lambda i, p: (i, p, 0, 0))
    return pl.pallas_call(
        body, name=name, grid=(b, npair),
        out_shape=(jax.ShapeDtypeStruct((b, s, nh * HEAD_DIM), F32), jax.ShapeDtypeStruct((b, s, LANE), F32)),
        in_specs=[tok(qblk), tok(kblk), tok(vblk), whole, rowspec],
        out_specs=(pl.BlockSpec((None, s, LANE), lambda i, p: (i, 0, p)), whole),
        scratch_shapes=[pltpu.VMEM((2, s, LANE), BF16), pltpu.VMEM((LANE, s), BF16), pltpu.VMEM((s, LANE), BF16),
                        pltpu.VMEM((2, s, 1), F32)],
        compiler_params=_cp(("parallel", "arbitrary")),
    )(proj, proj, proj, fsum, frow)


def _fox_bwd(proj, qblk, kblk, vblk, dmix, doblk, fsum, frow, frowb, lse, lserowb, name):
    b, s, _ = proj.shape
    nh = frow.shape[1]
    npair = nh // 2
    BLK = min(s, FOX_BQ)
    assert s % BLK == 0
    nb = s // BLK
    per, nseg = _fox_segments(nb)

    def body(q_ref, k_ref, v_ref, do_ref, fs_ref, fr_ref, frb_ref, ls_ref, lrb_ref,
             dq_ref, dk_ref, dv_ref, dfr_ref,
             qm_ref, dom_ref, kb_ref, vb_ref, kt_ref, vt_ref, qtm_ref, dotm_ref, dka_ref, dva_ref, fc_ref, l_ref):
        pair = pl.program_id(1)
        _lane_columns(fs_ref, fc_ref, pair)
        _lane_columns(ls_ref, l_ref, pair)
        lo, hi = _head_masks((s, LANE), 1)
        qv = q_ref[...].astype(F32) * SCALE
        dov = do_ref[...]
        for hh, msk in enumerate((lo, hi)):
            qm_ref[hh] = jnp.where(msk, qv, 0.0).astype(BF16)
            dom_ref[hh] = jnp.where(msk, dov, 0.0).astype(BF16)
        kv = k_ref[...].astype(F32)
        vv = v_ref[...].astype(F32)
        kb_ref[...] = kv.astype(BF16)
        vb_ref[...] = vv.astype(BF16)
        kt_ref[...] = kv.T.astype(BF16)
        vt_ref[...] = vv.T.astype(BF16)
        rlo, rhi = _head_masks((LANE, BLK), 0)

        def tstep(n, carry):
            r0 = pl.multiple_of(n * BLK, BLK)
            qt = (q_ref[pl.ds(r0, BLK), :].astype(F32) * SCALE).T
            dt = do_ref[pl.ds(r0, BLK), :].astype(F32).T
            for hh, msk in enumerate((rlo, rhi)):
                qtm_ref[hh, n] = jnp.where(msk, qt, 0.0).astype(BF16)
                dotm_ref[hh, n] = jnp.where(msk, dt, 0.0).astype(BF16)
            return carry

        lax.fori_loop(0, nb, tstep, 0)
        dka_ref[...] = jnp.zeros_like(dka_ref)
        dva_ref[...] = jnp.zeros_like(dva_ref)
        dfr_ref[...] = jnp.zeros_like(dfr_ref)
        lane_lo = lax.broadcasted_iota(jnp.int32, (BLK, LANE), 1) < HEAD_DIM
        tail = per * BLK
        causal = (lax.broadcasted_iota(jnp.int32, (BLK, tail), 1)
                  - lax.broadcasted_iota(jnp.int32, (BLK, tail), 0))
        causal_t = (lax.broadcasted_iota(jnp.int32, (tail, BLK), 0)
                    - lax.broadcasted_iota(jnp.int32, (tail, BLK), 1))
        for seg in range(nseg):
            w = (seg + 1) * tail

            def nstep(n, carry):
                r0 = pl.multiple_of(n * BLK, BLK)
                lim = (n - seg * per) * BLK
                dqs = []
                for hh in range(2):
                    qn = qm_ref[hh, pl.ds(r0, BLK), :]
                    don = dom_ref[hh, pl.ds(r0, BLK), :]
                    sc = _dot(qn, kt_ref[:, :w]) + ((fc_ref[hh, pl.ds(r0, BLK), :] - l_ref[hh, pl.ds(r0, BLK), :])
                                                   - fr_ref[hh, :, :w])
                    masked = jnp.where(causal <= lim, sc[:, w - tail:], NEG)
                    p = jnp.exp(masked if seg == 0 else jnp.concatenate([sc[:, :w - tail], masked], axis=1))
                    dp = _dot(don, vt_ref[:, :w])
                    ds = p * (dp - jnp.sum(p * dp, axis=1, keepdims=True))
                    dqs.append(_dot(ds.astype(BF16), kb_ref[:w, :]))
                    dfr_ref[hh, :, :w] -= jnp.sum(ds, axis=0, keepdims=True)
                    sct = _dot(kb_ref[:w, :], qtm_ref[hh, n]) + ((frb_ref[hh, n] - lrb_ref[hh, n]) - fc_ref[hh, :w, :])
                    masked_t = jnp.where(causal_t <= lim, sct[w - tail:, :], NEG)
                    pt = jnp.exp(masked_t if seg == 0 else jnp.concatenate([sct[:w - tail, :], masked_t], axis=0))
                    dpt = _dot(vb_ref[:w, :], dotm_ref[hh, n])
                    dst = pt * (dpt - jnp.sum(pt * dpt, axis=0, keepdims=True))
                    dka_ref[:w, :] += _dot(dst.astype(BF16), qn)
                    dva_ref[:w, :] += _dot(pt.astype(BF16), don)
                dq_ref[pl.ds(r0, BLK), :] = (jnp.where(lane_lo, dqs[0], dqs[1]) * SCALE).astype(BF16)
                return carry

            lax.fori_loop(seg * per, (seg + 1) * per, nstep, 0)
        dk_ref[...] = dka_ref[...].astype(BF16)
        dv_ref[...] = dva_ref[...].astype(BF16)

    def tok(blk):
        return pl.BlockSpec((None, s, LANE), lambda i, p: (i, 0, blk + p))

    whole = pl.BlockSpec((None, s, LANE), lambda i, p: (i, 0, 0))
    rowspec = pl.BlockSpec((None, 2, 1, s), lambda i, p: (i, p, 0, 0))
    rowbspec = pl.BlockSpec((None, 2, nb, 1, BLK), lambda i, p: (i, p, 0, 0, 0))
    outtok = pl.BlockSpec((None, s, LANE), lambda i, p: (i, 0, p))
    shp = jax.ShapeDtypeStruct((b, s, nh * HEAD_DIM), BF16)
    return pl.pallas_call(
        body, name=name, grid=(b, npair),
        out_shape=(shp, shp, shp, jax.ShapeDtypeStruct((b, nh, 1, s), F32)),
        in_specs=[tok(qblk), tok(kblk), tok(vblk),
                  pl.BlockSpec((None, s, LANE), lambda i, p: (i, 0, doblk + p)),
                  whole, rowspec, rowbspec, whole, rowbspec],
        out_specs=(outtok, outtok, outtok, rowspec),
        scratch_shapes=[pltpu.VMEM((2, s, LANE), BF16), pltpu.VMEM((2, s, LANE), BF16),
                        pltpu.VMEM((s, LANE), BF16), pltpu.VMEM((s, LANE), BF16),
                        pltpu.VMEM((LANE, s), BF16), pltpu.VMEM((LANE, s), BF16),
                        pltpu.VMEM((2, nb, LANE, BLK), BF16), pltpu.VMEM((2, nb, LANE, BLK), BF16),
                        pltpu.VMEM((s, LANE), F32), pltpu.VMEM((s, LANE), F32),
                        pltpu.VMEM((2, s, 1), F32), pltpu.VMEM((2, s, 1), F32)],
        compiler_params=_cp(("parallel", "parallel")),
    )(proj, proj, proj, dmix, fsum, frow, frowb, lse, lserowb)


def _expm1(x):
    poly = x * (1.0 + x * (1.0 / 2 + x * (1.0 / 6 + x * (1.0 / 24 + x * (1.0 / 120 + x * (1.0 / 720))))))
    return jnp.where(x > -0.1, poly, jnp.exp(x) - 1.0)


def _softplus(z):
    return jnp.maximum(z, 0.0) + jnp.log(1.0 + jnp.exp(-jnp.abs(z)))


def _scan_rows(a, u, carry, row, up):
    tc, c = a.shape
    d = 1
    while d < tc:
        if d < SUBLANE:
            keep = (row >= d) if up else (row < tc - d)
            shift = d if up else tc - d
            a_sh = jnp.where(keep, pltpu.roll(a, shift, 0), 1.0)
            u_sh = jnp.where(keep, pltpu.roll(u, shift, 0), 0.0)
        elif up:
            a_sh = jnp.concatenate([jnp.ones((d, c), F32), a[:tc - d]], axis=0)
            u_sh = jnp.concatenate([jnp.zeros((d, c), F32), u[:tc - d]], axis=0)
        else:
            a_sh = jnp.concatenate([a[d:], jnp.ones((d, c), F32)], axis=0)
            u_sh = jnp.concatenate([u[d:], jnp.zeros((d, c), F32)], axis=0)
        u = a * u_sh + u
        a = a * a_sh
        d *= 2
    return u + a * carry


def _scan_up(a, u, carry, row):
    return _scan_rows(a, u, carry, row, True)


def _scan_down(bnext, g, carry, row):
    return _scan_rows(bnext, g, carry, row, False)


def _pick_row(val, row, which):
    return jnp.sum(jnp.where(row == which, val, 0.0), axis=0, keepdims=True)


def _lru_gates(xpad_ref, t0, tc, cw_ref, cb_ref, wa, ba_ref, wx, bx_ref, sp):
    xw = xpad_ref[pl.ds(t0, tc + SUBLANE), :]
    xc = cb_ref[...]
    for j in range(CONV_WIDTH):
        sh = CONV_WIDTH - 1 - j
        xs = xw if sh == 0 else pltpu.roll(xw, sh, 0)
        xc = xc + xs[SUBLANE:, :] * cw_ref[j:j + 1, :]
    xcb = xc.astype(BF16)
    r = _sigmoid(_dot(xcb, wa) + ba_ref[...])
    i = _sigmoid(_dot(xcb, wx) + bx_ref[...])
    la = -LRU_C * r * sp
    return xc, r, i, la


def _lru_specs(s, cb):
    seq = lambda bi, ni: (bi, 0, ni)
    return dict(
        seq=pl.BlockSpec((None, s, cb), seq),
        cw=pl.BlockSpec((CONV_WIDTH, cb), lambda bi, ni: (0, ni)),
        vec=pl.BlockSpec((1, cb), lambda bi, ni: (0, ni)),
        wblk=pl.BlockSpec((None, cb, cb), lambda bi, ni: (ni, 0, 0)),
    )


def _lru_fwd(proj, cw, cb_, wa, ba, wx, bx, lam, name):
    b, s, _ = proj.shape
    nblk, cb, _ = wa.shape
    tc = min(s, SCAN_CHUNK)
    nc = s // tc

    def body(x_ref, cw_ref, cb_ref, wa_ref, ba_ref, wx_ref, bx_ref, lam_ref, hs_ref, xpad_ref):
        xpad_ref[0:SUBLANE, :] = jnp.zeros((SUBLANE, cb), F32)
        xpad_ref[SUBLANE:, :] = x_ref[...].astype(F32)
        wa_b = wa_ref[...].astype(BF16)
        wx_b = wx_ref[...].astype(BF16)
        sp = _softplus(-lam_ref[...])
        row = lax.broadcasted_iota(jnp.int32, (tc, cb), 0)

        def chunk(ci, carry):
            t0 = pl.multiple_of(ci * tc, tc)
            xc, r, i, la = _lru_gates(xpad_ref, t0, tc, cw_ref, cb_ref, wa_b, ba_ref, wx_b, bx_ref, sp)
            a = jnp.exp(la)
            u = jnp.sqrt(-_expm1(2.0 * la)) * (i * xc)
            h = _scan_up(a, u, carry, row)
            hs_ref[pl.ds(t0, tc), :] = h
            return _pick_row(h, row, tc - 1)

        lax.fori_loop(0, nc, chunk, jnp.zeros((1, cb), F32))

    sp_ = _lru_specs(s, cb)
    return pl.pallas_call(
        body, name=name, grid=(b, nblk),
        out_shape=jax.ShapeDtypeStruct((b, s, nblk * cb), F32),
        in_specs=[sp_["seq"], sp_["cw"], sp_["vec"], sp_["wblk"], sp_["vec"], sp_["wblk"], sp_["vec"], sp_["vec"]],
        out_specs=sp_["seq"],
        scratch_shapes=[pltpu.VMEM((s + SUBLANE, cb), F32)],
        compiler_params=_cp(("parallel", "parallel")),
    )(proj, cw, cb_, wa, ba, wx, bx, lam)


def _lru_bwd(proj, hs, dhs, cw, cb_, wa, ba, wx, bx, lam, name):
    b, s, _ = proj.shape
    nblk, cb, _ = wa.shape
    tc = min(s, SCAN_CHUNK)
    nc = s // tc

    def body(x_ref, hs_ref, dhs_ref, cw_ref, cb_ref, wa_ref, ba_ref, wx_ref, bx_ref, lam_ref,
             dx_ref, dcw_ref, dcb_ref, dwa_ref, dba_ref, dwx_ref, dbx_ref, dlam_ref,
             xpad_ref, hpad_ref, dcpad_ref, xc_ref, r_ref, i_ref, a_ref):
        @pl.when(pl.program_id(1) == 0)
        def _():
            for ref in (dcw_ref, dcb_ref, dwa_ref, dba_ref, dwx_ref, dbx_ref, dlam_ref):
                ref[...] = jnp.zeros_like(ref)

        zeros8 = jnp.zeros((SUBLANE, cb), F32)
        xpad_ref[0:SUBLANE, :] = zeros8
        xpad_ref[SUBLANE:, :] = x_ref[...].astype(F32)
        hpad_ref[0:SUBLANE, :] = zeros8
        hpad_ref[SUBLANE:, :] = hs_ref[...]
        dcpad_ref[s:s + SUBLANE, :] = zeros8
        wa_b = wa_ref[...].astype(BF16)
        wx_b = wx_ref[...].astype(BF16)
        lam_v = lam_ref[...]
        sp = _softplus(-lam_v)
        dsp_dlam = -_sigmoid(-lam_v)
        row = lax.broadcasted_iota(jnp.int32, (tc, cb), 0)

        def recompute(ci, carry):
            t0 = pl.multiple_of(ci * tc, tc)
            xc, r, i, la = _lru_gates(xpad_ref, t0, tc, cw_ref, cb_ref, wa_b, ba_ref, wx_b, bx_ref, sp)
            xc_ref[pl.ds(t0, tc), :] = xc
            r_ref[pl.ds(t0, tc), :] = r
            i_ref[pl.ds(t0, tc), :] = i
            a_ref[pl.ds(t0, tc), :] = jnp.exp(la)
            return carry

        lax.fori_loop(0, nc, recompute, 0)

        def adjoint(k, carry):
            g_next, a_first_next = carry
            t0 = pl.multiple_of((nc - 1 - k) * tc, tc)
            a = a_ref[pl.ds(t0, tc), :]
            a_next = jnp.where(row == tc - 1, a_first_next, pltpu.roll(a, tc - 1, 0))
            gg = _scan_down(a_next, dhs_ref[pl.ds(t0, tc), :], g_next, row)
            h_prev = pltpu.roll(hpad_ref[pl.ds(t0, tc + SUBLANE), :], 1, 0)[SUBLANE:, :]
            xc = xc_ref[pl.ds(t0, tc), :]
            r = r_ref[pl.ds(t0, tc), :]
            i = i_ref[pl.ds(t0, tc), :]
            mult = jnp.sqrt(-_expm1(-2.0 * LRU_C * r * sp))
            d_mult = gg * i * xc
            d_i = gg * mult * xc
            d_xc = gg * mult * i
            d_la = gg * h_prev * a - d_mult * (a * a) / mult
            d_zr = (d_la * (-LRU_C * sp)) * r * (1.0 - r)
            d_zi = d_i * i * (1.0 - i)
            dlam_ref[...] += jnp.sum(d_la * (-LRU_C * r), axis=0, keepdims=True) * dsp_dlam
            dzr_b = d_zr.astype(BF16)
            dzi_b = d_zi.astype(BF16)
            xcb = xc.astype(BF16)
            d_xc = d_xc + _dot_nt(dzr_b, wa_b) + _dot_nt(dzi_b, wx_b)
            dwa_ref[...] += _dot_tn(xcb, dzr_b)
            dwx_ref[...] += _dot_tn(xcb, dzi_b)
            dba_ref[...] += jnp.sum(d_zr, axis=0, keepdims=True)
            dbx_ref[...] += jnp.sum(d_zi, axis=0, keepdims=True)
            dcb_ref[...] += jnp.sum(d_xc, axis=0, keepdims=True)
            dcpad_ref[pl.ds(t0, tc), :] = d_xc
            return _pick_row(gg, row, 0), _pick_row(a, row, 0)

        zero = jnp.zeros((1, cb), F32)
        lax.fori_loop(0, nc, adjoint, (zero, zero))

        def conv_back(ci, carry):
            t0 = pl.multiple_of(ci * tc, tc)
            dw = dcpad_ref[pl.ds(t0, tc + SUBLANE), :]
            xw = xpad_ref[pl.ds(t0, tc + SUBLANE), :]
            d_xc = dw[:tc, :]
            dxr = jnp.zeros((tc, cb), F32)
            for j in range(CONV_WIDTH):
                sh = CONV_WIDTH - 1 - j
                dsh = dw if sh == 0 else pltpu.roll(dw, tc + SUBLANE - sh, 0)
                dxr = dxr + dsh[:tc, :] * cw_ref[j:j + 1, :]
                xs = xw if sh == 0 else pltpu.roll(xw, sh, 0)
                dcw_ref[j:j + 1, :] += jnp.sum(d_xc * xs[SUBLANE:, :], axis=0, keepdims=True)
            dx_ref[pl.ds(t0, tc), :] = dxr.astype(BF16)
            return carry

        lax.fori_loop(0, nc, conv_back, 0)

    seq = lambda ni, bi: (bi, 0, ni)
    seqspec = pl.BlockSpec((None, s, cb), seq)
    cwspec = pl.BlockSpec((CONV_WIDTH, cb), lambda ni, bi: (0, ni))
    vec = pl.BlockSpec((1, cb), lambda ni, bi: (0, ni))
    wblk = pl.BlockSpec((None, cb, cb), lambda ni, bi: (ni, 0, 0))
    w = nblk * cb
    return pl.pallas_call(
        body, name=name, grid=(nblk, b),
        out_shape=(jax.ShapeDtypeStruct((b, s, w), BF16), jax.ShapeDtypeStruct((CONV_WIDTH, w), F32),
                   jax.ShapeDtypeStruct((1, w), F32), jax.ShapeDtypeStruct((nblk, cb, cb), F32),
                   jax.ShapeDtypeStruct((1, w), F32), jax.ShapeDtypeStruct((nblk, cb, cb), F32),
                   jax.ShapeDtypeStruct((1, w), F32), jax.ShapeDtypeStruct((1, w), F32)),
        in_specs=[seqspec, seqspec, seqspec, cwspec, vec, wblk, vec, wblk, vec, vec],
        out_specs=(seqspec, cwspec, vec, wblk, vec, wblk, vec, vec),
        scratch_shapes=[pltpu.VMEM((s + SUBLANE, cb), F32)] * 3 + [pltpu.VMEM((s, cb), F32)] * 4,
        compiler_params=_cp(("parallel", "arbitrary")),
    )(proj, hs, dhs, cw, cb_, wa, ba, wx, bx, lam)


def _adamw(w, g, m, v, name):
    shape = w.shape
    total = int(np.prod(shape))
    if w.ndim >= 2 and shape[-2] % SUBLANE == 0:
        rows, cols = shape[-2:]
    else:
        cols = 1024
        rows = -(-(-(-total // cols)) // SUBLANE) * SUBLANE
    lead = -(-total // (rows * cols))
    tr = _row_tile(rows, 512)
    pad = lead * rows * cols - total

    def flat(a):
        if pad:
            a = jnp.pad(a.reshape(-1), (0, pad))
        return a.reshape(lead, rows, cols)

    c1 = 1.0 - ADAM_B1 ** ADAM_STEP
    c2 = 1.0 - ADAM_B2 ** ADAM_STEP

    def body(w_ref, g_ref, m_ref, v_ref, d_ref, nm_ref, nv_ref):
        gv = g_ref[...]
        nm = ADAM_B1 * m_ref[...] + (1.0 - ADAM_B1) * gv
        nv = ADAM_B2 * v_ref[...] + (1.0 - ADAM_B2) * (gv * gv)
        nm_ref[...] = nm
        nv_ref[...] = nv
        d_ref[...] = -ADAM_LR * ((nm / c1) / (jnp.sqrt(nv / c2) + ADAM_EPS) + ADAM_WD * w_ref[...])

    spec = pl.BlockSpec((None, tr, cols), lambda l, i: (l, i, 0))
    shp = jax.ShapeDtypeStruct((lead, rows, cols), F32)
    outs = pl.pallas_call(
        body, name=name, grid=(lead, rows // tr), out_shape=(shp, shp, shp),
        in_specs=[spec] * 4, out_specs=(spec,) * 3,
        compiler_params=_cp(("parallel", "parallel")),
    )(flat(w), flat(g), flat(m), flat(v))
    if pad:
        return tuple(o.reshape(-1)[:total].reshape(shape) for o in outs)
    return tuple(o.reshape(shape) for o in outs)


def _to_heads(t, nh):
    b, s, _ = t.shape
    return t.reshape(b, s, nh, HEAD_DIM).transpose(0, 2, 1, 3)


def _stack_heads(t):
    b, s, _ = t.shape
    t = t.reshape(b, s // BLK, BLK, A_KV_HEADS, A_GROUP, HEAD_DIM).transpose(0, 3, 1, 4, 2, 5)
    return t.reshape(b, A_KV_HEADS, s // BLK, A_GROUP * BLK, HEAD_DIM)


def _unstack_heads(t):
    b, hkv, nb, rows, hd = t.shape
    t = t.reshape(b, hkv, nb, A_GROUP, BLK, hd).transpose(0, 2, 4, 1, 3, 5)
    return t.reshape(b, nb * BLK, hkv * A_GROUP * hd)


def _from_heads(t):
    b, nh, s, hd = t.shape
    return t.transpose(0, 2, 1, 3).reshape(b, s, nh * hd)


def _pad_rows(a, mult):
    r = a.shape[0]
    p = (-r) % mult
    return jnp.pad(a, ((0, p), (0, 0))) if p else a


def kernel(x, c, rel_bias, norm_g, ada_w, ada_b, attn_w_in, attn_sinks, attn_b_f, attn_w_out, lru_w_in, lru_conv_w, lru_conv_b, lru_w_a, lru_b_a, lru_w_x, lru_b_x, lru_lambda, lru_w_out, final_g, loss_target, m_rel_bias, m_norm_g, m_ada_w, m_ada_b, m_attn_w_in, m_attn_sinks, m_attn_b_f, m_attn_w_out, m_lru_w_in, m_lru_conv_w, m_lru_conv_b, m_lru_w_a, m_lru_b_a, m_lru_w_x, m_lru_b_x, m_lru_lambda, m_lru_w_out, m_final_g, v_rel_bias, v_norm_g, v_ada_w, v_ada_b, v_attn_w_in, v_attn_sinks, v_attn_b_f, v_attn_w_out, v_lru_w_in, v_lru_conv_w, v_lru_conv_b, v_lru_w_a, v_lru_b_a, v_lru_w_x, v_lru_b_x, v_lru_lambda, v_lru_w_out, v_final_g):
    bl, s, d = x.shape
    ix, iy, ic = lax.axis_index("x"), lax.axis_index("y"), lax.axis_index("c")
    chip = 2 * ix + iy
    me = 2 * chip + ic
    nb = s // BLK
    aw = A_Q_HEADS * HEAD_DIM
    akv = A_KV_HEADS * HEAD_DIM
    bw = B_HEADS * HEAD_DIM
    mixw = aw + bw
    qkv_w = aw + 2 * akv + 3 * bw
    n_in = attn_w_in.shape[2] * N_CHIP
    lw = lru_lambda.shape[1] * N_CHIP
    n0 = mixw + qkv_w + LANE

    rows_pad = -(-bl // SUBLANE) * SUBLANE
    c_all = _all_gather8(_pad_rows(c, SUBLANE), "gather_c", pltpu.VMEM)
    c_all = c_all.reshape(N_DEV, rows_pad, d)[:, :bl].reshape(N_DEV * bl, d)
    ncol = ada_w.shape[2]
    ada_w_l = lax.dynamic_index_in_dim(ada_w, ic, 0, keepdims=False)
    ada_b_l = lax.dynamic_slice(ada_b, (ic, chip * ncol), (1, ncol))
    mod_part = _ada_fwd(c_all, ada_w_l, ada_b_l, "ada_fwd")
    mod_all = _all_gather8(_pad_rows(mod_part, SUBLANE), "gather_mod", pltpu.VMEM)
    mrows = -(-(N_DEV * bl) // SUBLANE) * SUBLANE
    mod_all = mod_all.reshape(N_CHIP, 2, mrows, ncol)[:, :, :N_DEV * bl]
    mod_all = mod_all.transpose(1, 2, 0, 3).reshape(2, N_DEV * bl, N_CHIP * ncol)
    mod = lax.dynamic_slice_in_dim(mod_all, me * bl, bl, axis=1)
    shift = [mod[l, :, 0:d].reshape(bl, 1, d) for l in range(2)]
    scale = [mod[l, :, d:2 * d].reshape(bl, 1, d) for l in range(2)]
    gmod = [mod[l, :, 2 * d:3 * d].reshape(bl, 1, d) for l in range(2)]

    c_in0 = n_in // N_CHIP
    c_in1 = 2 * lw // N_CHIP
    assert c_in0 <= d and 2 * c_in1 == d
    r_in0, r_out0, r_in1, r_out1 = d // 2, mixw // N_CHIP // 2, d // 4, lw // N_CHIP // 2
    o_out0, o_in1, o_out1 = r_in0, r_in0 + r_out0, r_in0 + r_out0 + r_in1
    big_rows = o_out1 + r_out1

    def half_of(a, rows):
        return lax.dynamic_slice_in_dim(a, ic * rows, rows, axis=0)

    h_in1 = half_of(lru_w_in[0], r_in0).astype(BF16)
    my_half = jnp.concatenate([
        jnp.pad(half_of(attn_w_in[0], r_in0).astype(BF16), ((0, 0), (0, d - c_in0))),
        half_of(attn_w_out[0], r_out0).astype(BF16),
        jnp.concatenate([h_in1[:r_in1], h_in1[r_in1:]], axis=1),
        half_of(lru_w_out[0], r_out1).astype(BF16)], axis=0)
    gat = _all_gather8(my_half, "gather_weights", pltpu.HBM).reshape(N_CHIP, 2, big_rows, d)
    w_in0 = gat[:, :, :r_in0, :c_in0].transpose(1, 2, 0, 3).reshape(d, n_in)
    w_out0 = gat[:, :, o_out0:o_in1].reshape(mixw, d)
    w_in1 = gat[:, :, o_in1:o_out1].reshape(N_CHIP, 2, r_in1, 2, c_in1)
    w_in1 = w_in1.transpose(1, 3, 2, 0, 4).reshape(d, 2 * lw)
    w_out1 = gat[:, :, o_out1:].reshape(lw, d)
    w_cat0 = jnp.concatenate([w_in0[:, qkv_w + B_HEADS:], w_in0[:, :qkv_w + B_HEADS],
                              jnp.zeros((d, n0 - n_in), BF16)], axis=1)

    proj0, h0, zf = _norm_proj(x, norm_g[0:1], scale[0], shift[0], w_cat0, LANE, "norm_proj0")
    o_a = mixw
    aq = _stack_heads(proj0[:, :, o_a:o_a + aw].astype(BF16))
    ak = _to_heads(proj0[:, :, o_a + aw:o_a + aw + akv].astype(BF16), A_KV_HEADS)
    av = _to_heads(proj0[:, :, o_a + aw + akv:o_a + aw + 2 * akv].astype(BF16), A_KV_HEADS)
    o_b = o_a + aw + 2 * akv
    fox_blks = (o_b // LANE, (o_b + bw) // LANE, (o_b + 2 * bw) // LANE)
    bucket_np, valid_np = _rel_buckets()
    bucket = jnp.asarray(bucket_np)
    bias = _swa_bias(rel_bias.T, bucket, jnp.asarray(valid_np), "swa_bias")
    bias = bias.reshape(A_KV_HEADS, A_GROUP * BLK, 2 * BLK)
    sinks = jnp.repeat(attn_sinks[0].reshape(A_KV_HEADS, A_GROUP), BLK, axis=1).reshape(A_KV_HEADS, A_GROUP * BLK, 1)
    a_out, a_lse = _swa_fwd(aq, ak, av, bias, sinks, "swa_fwd")
    bf_pad = jnp.pad(attn_b_f, ((0, 0), (0, LANE - B_HEADS)))
    fsum = _fox_decay(zf, bf_pad, "fox_decay")
    fh = fsum[:, :, :B_HEADS].transpose(0, 2, 1)
    frow = fh.reshape(bl, B_HEADS, 1, s)
    fbq = min(s, FOX_BQ)
    frowb = fh.reshape(bl, B_HEADS, s // fbq, 1, fbq)
    b_out, b_lse = _fox_fwd(proj0, *fox_blks, fsum, frow, "fox_fwd")
    lserowb = b_lse[:, :, :B_HEADS].transpose(0, 2, 1).reshape(bl, B_HEADS, s // fbq, 1, fbq)
    mix0 = [_unstack_heads(a_out), b_out]
    x1, o0 = _gate_outproj(mix0, proj0, 0, w_out0, x, gmod[0], "gate_outproj0")

    proj1, h1 = _norm_proj(x1, norm_g[1:2], scale[1], shift[1], w_in1, 0, "norm_proj1")
    vec_rows = jnp.concatenate([lru_conv_w[0], lru_conv_b, lru_b_a, lru_b_x, lru_lambda], axis=0)
    vec_all = _all_gather8(vec_rows, "gather_lru_vectors", pltpu.VMEM)
    vec_all = vec_all.reshape(N_CHIP, 2, SUBLANE, lw // N_CHIP)[:, 0]
    vec_all = vec_all.transpose(1, 0, 2).reshape(SUBLANE, lw)
    cw_f, cb_f, ba_f, bx_f, lam_f = vec_all[0:4], vec_all[4:5], vec_all[5:6], vec_all[6:7], vec_all[7:8]
    hs = _lru_fwd(proj1, cw_f, cb_f, lru_w_a[0], ba_f, lru_w_x[0], bx_f, lam_f, "lru_fwd")
    x2, o1 = _gate_outproj([hs], proj1, 1, w_out1, x1, gmod[1], "gate_outproj1")

    loss_vec, dx2, g_final = _final_loss(x2, final_g.reshape(1, d), loss_target, "final_loss")
    loss = lax.psum(loss_vec[0, 0], ("x", "y", "c"))

    dhs, dgate1, do1, y1, dgm1 = _bwd_out(dx2, gmod[1], o1, [hs], proj1, 1, w_out1.T, F32, "bwd_out1")
    g_w_out1 = _matmul_tn(y1, [do1], "grad_w_out1")
    (dxr, g_cw, g_cb, g_wa, g_ba, g_wx, g_bx, g_lam) = _lru_bwd(
        proj1, hs, dhs, cw_f, cb_f, lru_w_a[0], ba_f, lru_w_x[0], bx_f, lam_f, "lru_bwd")
    dproj1 = [dxr, dgate1]
    g_w_in1 = _matmul_tn(h1, dproj1, "grad_w_in1")
    dx1, dsh1, dsc1, g_ng1 = _bwd_in(dproj1, w_in1.T, x1, norm_g[1:2], scale[1], dx2, "bwd_in1")

    dmix0, dgate0, do0, y0, dgm0 = _bwd_out(dx1, gmod[0], o0, mix0, proj0, 0, w_out0.T, BF16, "bwd_out0")
    g_w_out0 = _matmul_tn(y0, [do0], "grad_w_out0")
    da_out = _stack_heads(dmix0[:, :, :aw].astype(BF16))
    daq, dak, dav, dbias, dsink = _swa_bwd(aq, ak, av, bias, sinks, da_out, a_lse, "swa_bwd")
    dbq, dbk, dbv, dfrow = _fox_bwd(proj0, *fox_blks, dmix0, aw // LANE, fsum, frow, frowb, b_lse, lserowb,
                                    "fox_bwd")
    df = dfrow.reshape(bl, B_HEADS, s).transpose(0, 2, 1)
    df = jnp.pad(df, ((0, 0), (0, 0), (0, LANE - B_HEADS)))
    dzf, g_bf = _fox_dgate(df, zf, bf_pad, B_HEADS, "fox_dgate")
    dproj0 = ([dgate0, _unstack_heads(daq), _from_heads(dak), _from_heads(dav)]
              + [dbq, dbk, dbv, dzf.astype(BF16)])
    g_w_cat0 = _matmul_tn(h0, dproj0, "grad_w_in0")
    g_w_in0 = jnp.concatenate([g_w_cat0[:, mixw:mixw + qkv_w + B_HEADS], g_w_cat0[:, :mixw]], axis=1)
    dx0, dsh0, dsc0, g_ng0 = _bwd_in(dproj0, w_cat0.T, x, norm_g[0:1], scale[0], dx1, "bwd_in0")
    g_relb, g_sink = _swa_small_grads(dbias.reshape(bl, A_Q_HEADS, BLK, 2 * BLK),
                                      dsink.reshape(bl, A_Q_HEADS, 1, LANE), bucket, "swa_small_grads")

    dmod = jnp.concatenate([jnp.concatenate([dsh0, dsc0, dgm0], axis=-1),
                            jnp.concatenate([dsh1, dsc1, dgm1], axis=-1)], axis=1)
    dmod_all = _all_gather8(_pad_rows(dmod.reshape(bl, 6 * d), SUBLANE), "gather_dmod", pltpu.VMEM)
    dmod_all = dmod_all.reshape(N_DEV, rows_pad, 6 * d)[:, :bl].reshape(N_DEV * bl, 6 * d)
    dmod_chip = lax.dynamic_slice_in_dim(dmod_all.reshape(N_DEV * bl, 2, 3 * d), chip * ncol, ncol, axis=2)
    g_ada_w, g_ada_b = _ada_bwd(c_all, dmod_chip.transpose(1, 0, 2), dmod_all, "ada_bwd")
    g_ada_b = g_ada_b.reshape(2, 3 * d)

    tail = jnp.concatenate([g_relb[:, :, 0].T.reshape(-1), g_sink[:, 0, 0], g_bf[0, :B_HEADS]])
    n_relb = REL_BUCKETS * A_Q_HEADS
    small_rows = [g_wa.reshape(-1, d), g_wx.reshape(-1, d), g_ng0, g_ng1, g_final, g_cw, g_cb, g_ba, g_bx, g_lam,
                  jnp.pad(tail, (0, d - tail.shape[0])).reshape(1, d)]
    small_counts = [r.shape[0] for r in small_rows]
    piece_rows = -(-(-(-sum(small_counts) // N_DEV)) // SUBLANE) * SUBLANE
    small_2d = jnp.concatenate(small_rows, axis=0)
    small_2d = jnp.pad(small_2d, ((0, N_DEV * piece_rows - small_2d.shape[0]), (0, 0)))
    small_pieces = small_2d.reshape(N_CHIP, 2, piece_rows, d)
    p_in0 = jnp.pad(g_w_in0.reshape(2, r_in0, N_CHIP, c_in0).transpose(2, 0, 1, 3),
                    ((0, 0), (0, 0), (0, 0), (0, d - c_in0)))
    p_in1 = g_w_in1.reshape(2, 2, r_in1, N_CHIP, c_in1).transpose(3, 0, 2, 1, 4).reshape(N_CHIP, 2, r_in1, d)
    pieces = jnp.concatenate([p_in0, g_w_out0.reshape(N_CHIP, 2, r_out0, d), p_in1,
                              g_w_out1.reshape(N_CHIP, 2, r_out1, d), small_pieces], axis=2)
    theirs = _sibling_push(pieces, True, "push_sibling_halves")
    partial = _pair_sum(jnp.reshape(ic, (1,)).astype(jnp.int32), pieces, theirs, "sum_chip")
    slots = _chip_all_to_all(partial, "exchange_grads")
    reduced = _sum_slots(slots, "sum_grads")
    mine_big = reduced[:big_rows]
    other_big = _sibling_push(mine_big[None], False, "swap_halves")[0]
    both = jnp.stack([jnp.where(ic == 0, mine_big, other_big), jnp.where(ic == 0, other_big, mine_big)])
    g_big = [both[:, :r_in0, :c_in0].reshape(d, c_in0),
             both[:, o_out0:o_in1].reshape(2 * r_out0, d),
             both[:, o_in1:o_out1].reshape(2, r_in1, 2, c_in1).transpose(0, 2, 1, 3).reshape(d, c_in1),
             both[:, o_out1:].reshape(2 * r_out1, d)]
    small_all = _all_gather8(reduced[big_rows:], "gather_small_grads", pltpu.VMEM)
    g_small, off = [], 0
    for cnt in small_counts:
        g_small.append(small_all[off:off + cnt])
        off += cnt
    g_w_a, g_w_x = g_small[0].reshape(lru_w_a.shape[1:]), g_small[1].reshape(lru_w_x.shape[1:])
    g_norm_g = jnp.concatenate(g_small[2:4], axis=0)
    g_fin, g_cw_r, g_cb_r, g_ba_r, g_bx_r, g_lam_r = g_small[4:10]
    tail = g_small[10][0]
    g_rel_bias = tail[:n_relb].reshape(REL_BUCKETS, A_Q_HEADS)
    g_sinks, g_b_f = tail[n_relb:n_relb + A_Q_HEADS], tail[n_relb + A_Q_HEADS:n_relb + A_Q_HEADS + B_HEADS]
    cw4 = lw // N_CHIP

    def my_cols(a):
        return lax.dynamic_slice_in_dim(a, chip * cw4, cw4, axis=1)

    grads = {
        "rel_bias": g_rel_bias, "norm_g": g_norm_g, "ada_w": g_ada_w, "ada_b": g_ada_b,
        "attn_w_in": g_big[0][None], "attn_sinks": g_sinks[None], "attn_b_f": g_b_f[None],
        "attn_w_out": g_big[1][None], "lru_w_in": g_big[2][None], "lru_conv_w": my_cols(g_cw_r)[None],
        "lru_conv_b": my_cols(g_cb_r), "lru_w_a": g_w_a[None], "lru_b_a": my_cols(g_ba_r),
        "lru_w_x": g_w_x[None], "lru_b_x": my_cols(g_bx_r), "lru_lambda": my_cols(g_lam_r),
        "lru_w_out": g_big[3][None], "final_g": g_fin.reshape(d),
    }
    weights = dict(rel_bias=rel_bias, norm_g=norm_g, ada_w=ada_w, ada_b=ada_b, attn_w_in=attn_w_in,
                   attn_sinks=attn_sinks, attn_b_f=attn_b_f, attn_w_out=attn_w_out, lru_w_in=lru_w_in,
                   lru_conv_w=lru_conv_w, lru_conv_b=lru_conv_b, lru_w_a=lru_w_a, lru_b_a=lru_b_a,
                   lru_w_x=lru_w_x, lru_b_x=lru_b_x, lru_lambda=lru_lambda, lru_w_out=lru_w_out, final_g=final_g)
    moms = dict(rel_bias=(m_rel_bias, v_rel_bias), norm_g=(m_norm_g, v_norm_g), ada_w=(m_ada_w, v_ada_w),
                ada_b=(m_ada_b, v_ada_b), attn_w_in=(m_attn_w_in, v_attn_w_in),
                attn_sinks=(m_attn_sinks, v_attn_sinks), attn_b_f=(m_attn_b_f, v_attn_b_f),
                attn_w_out=(m_attn_w_out, v_attn_w_out), lru_w_in=(m_lru_w_in, v_lru_w_in),
                lru_conv_w=(m_lru_conv_w, v_lru_conv_w), lru_conv_b=(m_lru_conv_b, v_lru_conv_b),
                lru_w_a=(m_lru_w_a, v_lru_w_a), lru_b_a=(m_lru_b_a, v_lru_b_a), lru_w_x=(m_lru_w_x, v_lru_w_x),
                lru_b_x=(m_lru_b_x, v_lru_b_x), lru_lambda=(m_lru_lambda, v_lru_lambda),
                lru_w_out=(m_lru_w_out, v_lru_w_out), final_g=(m_final_g, v_final_g))
    names = list(weights)
    big_names = [n for n in names if weights[n].size >= 65536]
    small_names = [n for n in names if weights[n].size < 65536]
    delta, new_m, new_v = {}, {}, {}
    for n in big_names:
        delta[n], new_m[n], new_v[n] = _adamw(weights[n], grads[n].reshape(weights[n].shape),
                                              moms[n][0], moms[n][1], "adamw_" + n)
    cat = lambda arrs: jnp.concatenate([a.reshape(-1) for a in arrs])
    sd, sm, sv = _adamw(cat([weights[n] for n in small_names]), cat([grads[n] for n in small_names]),
                        cat([moms[n][0] for n in small_names]), cat([moms[n][1] for n in small_names]),
                        "adamw_small")
    off = 0
    for n in small_names:
        sz = weights[n].size
        shp = weights[n].shape
        delta[n], new_m[n], new_v[n] = (sd[off:off + sz].reshape(shp), sm[off:off + sz].reshape(shp),
                                        sv[off:off + sz].reshape(shp))
        off += sz
    out_grads = [grads[n].reshape(weights[n].shape) for n in names]
    return (loss, dx0, *out_grads, *[delta[n] for n in names], *[new_m[n] for n in names],
            *[new_v[n] for n in names])
```

```python
import functools
import math

import numpy as np
import jax
import jax.numpy as jnp
from jax import lax
from jax.experimental import pallas as pl
from jax.experimental.pallas import tpu as pltpu

F32 = jnp.float32
BF16 = jnp.bfloat16
MESH = pl.DeviceIdType.MESH

N_DEV = 8
N_CHIP = 4
HEAD_DIM = 64
BLK = 128
A_Q_HEADS = 8
A_KV_HEADS = 2
A_GROUP = A_Q_HEADS // A_KV_HEADS
B_HEADS = 8
REL_BUCKETS = 32
REL_MAX_EXACT = 16
REL_MAX_DIST = 128
LRU_BLOCKS = 8
LRU_C = 8.0
CONV_WIDTH = 4
EPS = 1e-6
NEG = -1e30
SCALE = HEAD_DIM ** -0.5
LANE = 128
SUBLANE = 8
VMEM_LIMIT = 56 * 1024 * 1024
SCAN_CHUNK = 256
ROW_TILE = 512
FOX_BQ = 512
ADAM_LR = 0.001
ADAM_B1 = 0.9
ADAM_B2 = 0.999
ADAM_EPS = 1e-08
ADAM_WD = 0.01
ADAM_STEP = 10
HI = lax.Precision.HIGHEST


def _cp(sem=None):
    return pltpu.CompilerParams(dimension_semantics=sem, vmem_limit_bytes=VMEM_LIMIT)


def _dot(a, b):
    return jnp.dot(a, b, preferred_element_type=F32)


def _dot_nt(a, b):
    return lax.dot_general(a, b, (((1,), (1,)), ((), ())), preferred_element_type=F32)


def _dot_tn(a, b):
    return lax.dot_general(a, b, (((0,), (0,)), ((), ())), preferred_element_type=F32)


def _sigmoid(z):
    return 1.0 / (1.0 + jnp.exp(-z))


def _row_tile(rows, cap):
    if rows <= cap:
        return rows
    best = SUBLANE
    t = SUBLANE
    while t <= cap:
        if rows % t == 0:
            best = t
        t += SUBLANE
    return best


def _all_gather8(x_shard, name, space):
    m_per, n = x_shard.shape
    n_own = 8 if (space == pltpu.HBM and m_per % 128 == 0) else 1
    own_rows = m_per // n_own

    def body(x_ref, out_ref, send_sems, recv_sems, local_sems):
        x, y, c = lax.axis_index("x"), lax.axis_index("y"), lax.axis_index("c")
        me, sibling = (x, y, c), (x, y, 1 - c)
        chips = [(1 - x, y), (x, 1 - y), (1 - x, 1 - y)]

        def rows(px, py, pc):
            return out_ref.at[pl.ds((4 * px + 2 * py + pc) * m_per, m_per), :]

        def copy(k, block, to, src=None):
            return pltpu.make_async_remote_copy(
                src_ref=rows(*block) if src is None else src, dst_ref=rows(*block),
                send_sem=send_sems.at[k], recv_sem=recv_sems.at[k], device_id=to, device_id_type=MESH)

        base = (4 * x + 2 * y + c) * m_per
        mine = [pltpu.make_async_copy(x_ref.at[pl.ds(i * own_rows, own_rows), :],
                                      out_ref.at[pl.ds(base + i * own_rows, own_rows), :], local_sems.at[i])
                for i in range(n_own)]
        for cp in mine:
            cp.start()
        first = [copy(0, me, sibling, src=x_ref)]
        first += [copy(1 + j, me, (*chip, c), src=x_ref) for j, chip in enumerate(chips)]
        for cp in first:
            cp.start()
        passed = [copy(4 + j, (*chip, c), sibling) for j, chip in enumerate(chips)]
        for j, chip in enumerate(chips):
            copy(1 + j, (*chip, c), me).wait_recv()
            passed[j].start()
        copy(0, sibling, me).wait_recv()
        for j, chip in enumerate(chips):
            copy(4 + j, (*chip, 1 - c), me).wait_recv()
        for cp in first + passed:
            cp.wait_send()
        for cp in mine:
            cp.wait()

    return pl.pallas_call(
        body, name=name,
        out_shape=jax.ShapeDtypeStruct((N_DEV * m_per, n), x_shard.dtype),
        in_specs=[pl.BlockSpec(memory_space=space)],
        out_specs=pl.BlockSpec(memory_space=space),
        scratch_shapes=[pltpu.SemaphoreType.DMA((7,)), pltpu.SemaphoreType.DMA((7,)),
                        pltpu.SemaphoreType.DMA((n_own,))],
        compiler_params=pltpu.CompilerParams(vmem_limit_bytes=VMEM_LIMIT),
    )(x_shard)


def _sibling_push(blocks, pick_other, name):
    nblk = blocks.shape[0]
    m, n = blocks.shape[-2:]

    def body(x_ref, out_ref, send_sems, recv_sems):
        x, y, c = lax.axis_index("x"), lax.axis_index("y"), lax.axis_index("c")
        copies = []
        for k in range(nblk):
            src = x_ref.at[k, 1 - c] if pick_other else x_ref.at[k]
            copies.append(pltpu.make_async_remote_copy(
                src_ref=src, dst_ref=out_ref.at[k], send_sem=send_sems.at[k], recv_sem=recv_sems.at[k],
                device_id=(x, y, 1 - c), device_id_type=MESH))
        for cp in copies:
            cp.start()
        for cp in copies:
            cp.wait_recv()
        for cp in copies:
            cp.wait_send()

    hbm = pl.BlockSpec(memory_space=pltpu.HBM)
    return pl.pallas_call(
        body, name=name,
        out_shape=jax.ShapeDtypeStruct((nblk, m, n), blocks.dtype),
        in_specs=[hbm], out_specs=hbm,
        scratch_shapes=[pltpu.SemaphoreType.DMA((nblk,)), pltpu.SemaphoreType.DMA((nblk,))],
    )(blocks)


def _chip_all_to_all(parts, name):
    _, m, n = parts.shape

    def body(x_ref, out_ref, send_sems, recv_sems, local_sem):
        x, y, c = lax.axis_index("x"), lax.axis_index("y"), lax.axis_index("c")
        me = 2 * x + y
        mine = pltpu.make_async_copy(x_ref.at[me], out_ref.at[me], local_sem)
        mine.start()
        copies = []
        for k in range(1, N_CHIP):
            px, py = x ^ ((k >> 1) & 1), y ^ (k & 1)
            copies.append(pltpu.make_async_remote_copy(
                src_ref=x_ref.at[2 * px + py], dst_ref=out_ref.at[me],
                send_sem=send_sems.at[k - 1], recv_sem=recv_sems.at[k - 1],
                device_id=(px, py, c), device_id_type=MESH))
        for cp in copies:
            cp.start()
        for cp in copies:
            cp.wait_recv()
        for cp in copies:
            cp.wait_send()
        mine.wait()

    hbm = pl.BlockSpec(memory_space=pltpu.HBM)
    return pl.pallas_call(
        body, name=name,
        out_shape=jax.ShapeDtypeStruct(parts.shape, parts.dtype),
        in_specs=[hbm], out_specs=hbm,
        scratch_shapes=[pltpu.SemaphoreType.DMA((N_CHIP - 1,)), pltpu.SemaphoreType.DMA((N_CHIP - 1,)),
                        pltpu.SemaphoreType.DMA],
    )(parts)


def _pair_sum(core, pieces, theirs, name):
    nblk, _, m, n = pieces.shape
    tr = _row_tile(m, 536)

    def body(c_ref, p_ref, t_ref, o_ref):
        o_ref[...] = (p_ref[...] + t_ref[...]).astype(BF16)

    return pl.pallas_call(
        body, name=name,
        grid_spec=pltpu.PrefetchScalarGridSpec(
            num_scalar_prefetch=1, grid=(nblk, m // tr),
            in_specs=[pl.BlockSpec((None, None, tr, n), lambda k, i, c_ref: (k, c_ref[0], i, 0)),
                      pl.BlockSpec((None, tr, n), lambda k, i, c_ref: (k, i, 0))],
            out_specs=pl.BlockSpec((None, tr, n), lambda k, i, c_ref: (k, i, 0))),
        out_shape=jax.ShapeDtypeStruct((nblk, m, n), BF16),
        compiler_params=_cp(("parallel", "parallel")),
    )(core, pieces, theirs)


def _sum_slots(slots, name):
    k, m, n = slots.shape
    tr = _row_tile(m, 536)

    def body(s_ref, o_ref):
        acc = s_ref[0].astype(F32)
        for j in range(1, k):
            acc = acc + s_ref[j].astype(F32)
        o_ref[...] = acc

    return pl.pallas_call(
        body, name=name, grid=(m // tr,),
        out_shape=jax.ShapeDtypeStruct((m, n), F32),
        in_specs=[pl.BlockSpec((k, tr, n), lambda i: (0, i, 0))],
        out_specs=pl.BlockSpec((tr, n), lambda i: (i, 0)),
        compiler_params=_cp(("parallel",)),
    )(slots)


def _ada_fwd(c_all, w, b, name):
    r, _ = c_all.shape
    n = w.shape[1]

    def body(c_ref, w_ref, b_ref, o_ref):
        cv = c_ref[...]
        act = cv * _sigmoid(cv)
        o_ref[...] = jnp.dot(act, w_ref[...], precision=HI, preferred_element_type=F32) + b_ref[...]

    return pl.pallas_call(body, name=name, out_shape=jax.ShapeDtypeStruct((r, n), F32),
                          compiler_params=_cp())(c_all, w, b)


def _ada_bwd(c_all, dmod_chip, dmod_all, name):
    r, d = c_all.shape
    nl, _, n = dmod_chip.shape

    def body(c_ref, dm_ref, da_ref, gw_ref, gb_ref):
        cv = c_ref[...]
        act = cv * _sigmoid(cv)
        for l in range(nl):
            gw_ref[l] = lax.dot_general(act, dm_ref[l], (((0,), (0,)), ((), ())), precision=HI,
                                        preferred_element_type=F32)
        gb_ref[...] = jnp.sum(da_ref[...], axis=0, keepdims=True)

    return pl.pallas_call(
        body, name=name,
        out_shape=(jax.ShapeDtypeStruct((nl, d, n), F32), jax.ShapeDtypeStruct((1, dmod_all.shape[1]), F32)),
        compiler_params=_cp())(c_all, dmod_chip, dmod_all)


def _norm_proj(x, g, scale, shift, w, f32_cols, name):
    b, s, d = x.shape
    n = w.shape[1]
    tm = min(s, ROW_TILE)

    def body(x_ref, g_ref, sc_ref, sh_ref, w_ref, proj_ref, h_ref, *aux_ref):
        xv = x_ref[...]
        rstd = lax.rsqrt(jnp.mean(xv * xv, axis=-1, keepdims=True) + EPS)
        h = (xv * rstd) * g_ref[...] * (1.0 + sc_ref[...]) + sh_ref[...]
        hb = h.astype(BF16)
        h_ref[...] = hb
        proj = _dot(hb, w_ref[...])
        proj_ref[...] = proj.astype(BF16)
        if f32_cols:
            aux_ref[0][...] = proj[:, n - f32_cols:]

    row = lambda i, j: (i, j, 0)
    out_shape = [jax.ShapeDtypeStruct((b, s, n), BF16), jax.ShapeDtypeStruct((b, s, d), BF16)]
    out_specs = [pl.BlockSpec((None, tm, n), row), pl.BlockSpec((None, tm, d), row)]
    if f32_cols:
        out_shape.append(jax.ShapeDtypeStruct((b, s, f32_cols), F32))
        out_specs.append(pl.BlockSpec((None, tm, f32_cols), row))
    return pl.pallas_call(
        body, name=name, grid=(b, s // tm),
        out_shape=tuple(out_shape),
        in_specs=[pl.BlockSpec((None, tm, d), row),
                  pl.BlockSpec((1, d), lambda i, j: (0, 0)),
                  pl.BlockSpec((None, 1, d), lambda i, j: (i, 0, 0)),
                  pl.BlockSpec((None, 1, d), lambda i, j: (i, 0, 0)),
                  pl.BlockSpec((d, n), lambda i, j: (0, 0))],
        out_specs=tuple(out_specs),
        compiler_params=_cp(("parallel", "parallel")),
    )(x, g, scale, shift, w)


def _cat_refs(refs):
    vals = [r[...] for r in refs]
    return vals[0] if len(vals) == 1 else jnp.concatenate(vals, axis=-1)


def _gate_outproj(mix_parts, proj, gate_blk, w_out, x, gmod, name):
    b, s, _ = x.shape
    wd, d = w_out.shape
    tm = min(s, ROW_TILE)
    npart = len(mix_parts)

    def body(*refs):
        mix_refs = refs[:npart]
        gate_ref, w_ref, x_ref, gm_ref, xo_ref, o_ref = refs[npart:]
        gt = gate_ref[...].astype(F32)
        y = (_cat_refs(mix_refs) * (gt * _sigmoid(gt))).astype(BF16)
        o = _dot(y, w_ref[...])
        o_ref[...] = o.astype(BF16)
        xo_ref[...] = x_ref[...] + gm_ref[...] * o

    return pl.pallas_call(
        body, name=name, grid=(b, s // tm),
        out_shape=(jax.ShapeDtypeStruct((b, s, d), F32), jax.ShapeDtypeStruct((b, s, d), BF16)),
        in_specs=[pl.BlockSpec((None, tm, p.shape[2]), lambda i, j: (i, j, 0)) for p in mix_parts] + [
                  pl.BlockSpec((None, tm, wd), lambda i, j: (i, j, gate_blk)),
                  pl.BlockSpec((wd, d), lambda i, j: (0, 0)),
                  pl.BlockSpec((None, tm, d), lambda i, j: (i, j, 0)),
                  pl.BlockSpec((None, 1, d), lambda i, j: (i, 0, 0))],
        out_specs=(pl.BlockSpec((None, tm, d), lambda i, j: (i, j, 0)),
                   pl.BlockSpec((None, tm, d), lambda i, j: (i, j, 0))),
        compiler_params=_cp(("parallel", "parallel")),
    )(*mix_parts, proj, w_out, x, gmod)


def _final_loss(x, g, target, name):
    b, s, d = x.shape
    tm = min(s, ROW_TILE)

    def body(x_ref, g_ref, t_ref, loss_ref, dx_ref, dg_ref):
        first = jnp.logical_and(pl.program_id(0) == 0, pl.program_id(1) == 0)

        @pl.when(first)
        def _():
            loss_ref[...] = jnp.zeros_like(loss_ref)
            dg_ref[...] = jnp.zeros_like(dg_ref)

        xv = x_ref[...]
        gv = g_ref[...]
        rstd = lax.rsqrt(jnp.mean(xv * xv, axis=-1, keepdims=True) + EPS)
        xhat = xv * rstd
        err = xhat * gv - t_ref[...]
        row = jnp.mean(err * err, axis=-1, keepdims=True)
        loss_ref[...] += 0.5 * jnp.sum(row, axis=0, keepdims=True)
        dy = err * (1.0 / d)
        dg_ref[...] += jnp.sum(dy * xhat, axis=0, keepdims=True)
        dxh = dy * gv
        dx_ref[...] = rstd * (dxh - xhat * jnp.mean(dxh * xhat, axis=-1, keepdims=True))

    return pl.pallas_call(
        body, name=name, grid=(b, s // tm),
        out_shape=(jax.ShapeDtypeStruct((1, LANE), F32), jax.ShapeDtypeStruct((b, s, d), F32),
                   jax.ShapeDtypeStruct((1, d), F32)),
        in_specs=[pl.BlockSpec((None, tm, d), lambda i, j: (i, j, 0)),
                  pl.BlockSpec((1, d), lambda i, j: (0, 0)),
                  pl.BlockSpec((None, tm, d), lambda i, j: (i, j, 0))],
        out_specs=(pl.BlockSpec((1, LANE), lambda i, j: (0, 0)),
                   pl.BlockSpec((None, tm, d), lambda i, j: (i, j, 0)),
                   pl.BlockSpec((1, d), lambda i, j: (0, 0))),
        compiler_params=_cp(("arbitrary", "arbitrary")),
    )(x, g, target)


def _bwd_out(dxo, gmod, o, mix_parts, proj, gate_blk, w_out_t, dmix_dtype, name):
    b, s, d = dxo.shape
    wd = w_out_t.shape[1]
    tm = min(s, ROW_TILE)
    npart = len(mix_parts)

    def body(dx_ref, gm_ref, o_ref, *refs):
        mix_refs = refs[:npart]
        gate_ref, wt_ref, dmix_ref, dgate_ref, do_ref, y_ref, dgm_ref = refs[npart:]

        @pl.when(pl.program_id(1) == 0)
        def _():
            dgm_ref[...] = jnp.zeros_like(dgm_ref)

        dx = dx_ref[...]
        dgm_ref[...] += jnp.sum(dx * o_ref[...].astype(F32), axis=0, keepdims=True)
        dob = (gm_ref[...] * dx).astype(BF16)
        do_ref[...] = dob
        dy = _dot(dob, wt_ref[...])
        gt = gate_ref[...].astype(F32)
        sg = _sigmoid(gt)
        silu = gt * sg
        mx = _cat_refs(mix_refs)
        y_ref[...] = (mx * silu).astype(BF16)
        dmix_ref[...] = (dy * silu).astype(dmix_dtype)
        dgate_ref[...] = (dy * mx * (sg * (1.0 + gt * (1.0 - sg)))).astype(BF16)

    row = lambda i, j: (i, j, 0)
    return pl.pallas_call(
        body, name=name, grid=(b, s // tm),
        out_shape=(jax.ShapeDtypeStruct((b, s, wd), dmix_dtype), jax.ShapeDtypeStruct((b, s, wd), BF16),
                   jax.ShapeDtypeStruct((b, s, d), BF16), jax.ShapeDtypeStruct((b, s, wd), BF16),
                   jax.ShapeDtypeStruct((b, 1, d), F32)),
        in_specs=[pl.BlockSpec((None, tm, d), row),
                  pl.BlockSpec((None, 1, d), lambda i, j: (i, 0, 0)),
                  pl.BlockSpec((None, tm, d), row)] + [
                  pl.BlockSpec((None, tm, p.shape[2]), row) for p in mix_parts] + [
                  pl.BlockSpec((None, tm, wd), lambda i, j: (i, j, gate_blk)),
                  pl.BlockSpec((d, wd), lambda i, j: (0, 0))],
        out_specs=(pl.BlockSpec((None, tm, wd), row), pl.BlockSpec((None, tm, wd), row),
                   pl.BlockSpec((None, tm, d), row), pl.BlockSpec((None, tm, wd), row),
                   pl.BlockSpec((None, 1, d), lambda i, j: (i, 0, 0))),
        compiler_params=_cp(("parallel", "arbitrary")),
    )(dxo, gmod, o, *mix_parts, proj, w_out_t)


def _bwd_in(dproj_parts, w_in_t, x, g, scale, dxo, name):
    b, s, d = x.shape
    n = w_in_t.shape[0]
    tm = min(s, ROW_TILE)
    npart = len(dproj_parts)

    def body(*refs):
        dp_refs = refs[:npart]
        wt_ref, x_ref, g_ref, sc_ref, dxo_ref, dx_ref, dsh_ref, dsc_ref, dg_ref = refs[npart:]

        @pl.when(jnp.logical_and(pl.program_id(0) == 0, pl.program_id(1) == 0))
        def _():
            dg_ref[...] = jnp.zeros_like(dg_ref)

        @pl.when(pl.program_id(1) == 0)
        def _():
            dsh_ref[...] = jnp.zeros_like(dsh_ref)
            dsc_ref[...] = jnp.zeros_like(dsc_ref)

        dh = _dot(_cat_refs(dp_refs), wt_ref[...])
        xv = x_ref[...]
        gv = g_ref[...]
        one_sc = 1.0 + sc_ref[...]
        rstd = lax.rsqrt(jnp.mean(xv * xv, axis=-1, keepdims=True) + EPS)
        xhat = xv * rstd
        dsh_ref[...] += jnp.sum(dh, axis=0, keepdims=True)
        dsc_ref[...] += jnp.sum(dh * (xhat * gv), axis=0, keepdims=True)
        dhs = dh * one_sc
        dg_ref[...] += jnp.sum(dhs * xhat, axis=0, keepdims=True)
        dxh = dhs * gv
        dx_ref[...] = dxo_ref[...] + rstd * (dxh - xhat * jnp.mean(dxh * xhat, axis=-1, keepdims=True))

    row = lambda i, j: (i, j, 0)
    per_b = lambda i, j: (i, 0, 0)
    return pl.pallas_call(
        body, name=name, grid=(b, s // tm),
        out_shape=(jax.ShapeDtypeStruct((b, s, d), F32), jax.ShapeDtypeStruct((b, 1, d), F32),
                   jax.ShapeDtypeStruct((b, 1, d), F32), jax.ShapeDtypeStruct((1, d), F32)),
        in_specs=[pl.BlockSpec((None, tm, p.shape[2]), row) for p in dproj_parts] + [
                  pl.BlockSpec((n, d), lambda i, j: (0, 0)),
                  pl.BlockSpec((None, tm, d), row),
                  pl.BlockSpec((1, d), lambda i, j: (0, 0)),
                  pl.BlockSpec((None, 1, d), per_b),
                  pl.BlockSpec((None, tm, d), row)],
        out_specs=(pl.BlockSpec((None, tm, d), row), pl.BlockSpec((None, 1, d), per_b),
                   pl.BlockSpec((None, 1, d), per_b), pl.BlockSpec((1, d), lambda i, j: (0, 0))),
        compiler_params=_cp(("arbitrary", "arbitrary")),
    )(*dproj_parts, w_in_t, x, g, scale, dxo)


def _matmul_tn(a, b_parts, name):
    bsz, s, m = a.shape
    n = sum(p.shape[2] for p in b_parts)
    tk = next(c for c in (512, 256, 128) if s % c == 0)
    npart = len(b_parts)

    def body(a_ref, *refs):
        b_refs, o_ref = refs[:npart], refs[npart]

        @pl.when(jnp.logical_and(pl.program_id(0) == 0, pl.program_id(1) == 0))
        def _():
            o_ref[...] = jnp.zeros_like(o_ref)

        o_ref[...] += _dot_tn(a_ref[...], _cat_refs(b_refs))

    row = lambda i, k: (i, k, 0)
    return pl.pallas_call(
        body, name=name, grid=(bsz, s // tk),
        out_shape=jax.ShapeDtypeStruct((m, n), F32),
        in_specs=[pl.BlockSpec((None, tk, m), row)] + [pl.BlockSpec((None, tk, p.shape[2]), row) for p in b_parts],
        out_specs=pl.BlockSpec((m, n), lambda i, k: (0, 0)),
        compiler_params=_cp(("arbitrary", "arbitrary")),
    )(a, *b_parts)


def _rel_buckets():
    qi = np.arange(BLK)[:, None]
    kj = np.arange(2 * BLK)[None, :]
    rel = qi - kj + BLK
    n = np.maximum(rel, 0)
    nf = np.maximum(n, 1).astype(np.float32)
    large = REL_MAX_EXACT + (np.log(nf / REL_MAX_EXACT) / math.log(REL_MAX_DIST / REL_MAX_EXACT)
                             * (REL_BUCKETS - REL_MAX_EXACT)).astype(np.int32)
    large = np.minimum(large, REL_BUCKETS - 1)
    bucket = np.where(n < REL_MAX_EXACT, n, large).astype(np.int32)
    valid = ((rel >= 0) & (rel < BLK)).astype(np.int32)
    return bucket, valid


def _swa_bias(rel_bias_t, bucket, valid, name):
    nh = rel_bias_t.shape[0]

    def body(rb_ref, bk_ref, vl_ref, o_ref):
        h = pl.program_id(0)
        bk = bk_ref[...]
        acc = jnp.zeros(bk.shape, F32)
        for i in range(REL_BUCKETS):
            acc = jnp.where(bk == i, rb_ref[h, i], acc)
        o_ref[...] = jnp.where(vl_ref[...] > 0, acc, NEG)

    return pl.pallas_call(
        body, name=name, grid=(nh,),
        out_shape=jax.ShapeDtypeStruct((nh, BLK, 2 * BLK), F32),
        in_specs=[pl.BlockSpec(memory_space=pltpu.SMEM),
                  pl.BlockSpec((BLK, 2 * BLK), lambda h: (0, 0)),
                  pl.BlockSpec((BLK, 2 * BLK), lambda h: (0, 0))],
        out_specs=pl.BlockSpec((None, BLK, 2 * BLK), lambda h: (h, 0, 0)),
        compiler_params=_cp(("arbitrary",)),
    )(rel_bias_t, bucket, valid)


def _swa_scores(n, q, kw, bias_ref):
    sc = _dot_nt(q, kw) * SCALE + bias_ref[...]
    second = lax.broadcasted_iota(jnp.int32, sc.shape, 1) >= BLK
    return jnp.where(jnp.logical_or(n > 0, second), sc, NEG)


def _pad_front(dst_ref, src_ref):
    dst_ref[0:BLK, :] = jnp.zeros((BLK, dst_ref.shape[1]), dst_ref.dtype)
    dst_ref[BLK:, :] = src_ref[...]


def _swa_fwd(q, k, v, bias, sinks, name):
    b, hkv, nb, rows, hd = q.shape
    s = nb * BLK

    def body(q_ref, k_ref, v_ref, bias_ref, sink_ref, o_ref, l_ref, kpad_ref, vpad_ref):
        _pad_front(kpad_ref, k_ref)
        _pad_front(vpad_ref, v_ref)
        sink = sink_ref[...]

        def step(n, carry):
            w0 = pl.multiple_of(n * BLK, BLK)
            sc = _swa_scores(n, q_ref[n], kpad_ref[pl.ds(w0, 2 * BLK), :], bias_ref)
            m = jnp.maximum(jnp.max(sc, axis=1, keepdims=True), sink)
            e = jnp.exp(sc - m)
            den = jnp.sum(e, axis=1, keepdims=True) + jnp.exp(sink - m)
            o_ref[n] = _dot((e * (1.0 / den)).astype(BF16), vpad_ref[pl.ds(w0, 2 * BLK), :])
            l_ref[n] = m + jnp.log(den)
            return carry

        lax.fori_loop(0, nb, step, 0)

    qspec = pl.BlockSpec((None, None, nb, rows, hd), lambda i, kv: (i, kv, 0, 0, 0))
    kspec = pl.BlockSpec((None, None, s, hd), lambda i, kv: (i, kv, 0, 0))
    return pl.pallas_call(
        body, name=name, grid=(b, hkv),
        out_shape=(jax.ShapeDtypeStruct((b, hkv, nb, rows, hd), F32), jax.ShapeDtypeStruct((b, hkv, nb, rows, 1), F32)),
        in_specs=[qspec, kspec, kspec,
                  pl.BlockSpec((None, rows, 2 * BLK), lambda i, kv: (kv, 0, 0)),
                  pl.BlockSpec((None, rows, 1), lambda i, kv: (kv, 0, 0))],
        out_specs=(qspec, pl.BlockSpec((None, None, nb, rows, 1), lambda i, kv: (i, kv, 0, 0, 0))),
        scratch_shapes=[pltpu.VMEM((s + BLK, hd), BF16), pltpu.VMEM((s + BLK, hd), BF16)],
        compiler_params=_cp(("parallel", "parallel")),
    )(q, k, v, bias, sinks)


def _swa_bwd(q, k, v, bias, sinks, do, lse, name):
    b, hkv, nb, rows, hd = q.shape
    s = nb * BLK

    def body(q_ref, k_ref, v_ref, bias_ref, sink_ref, do_ref, l_ref,
             dq_ref, dk_ref, dv_ref, db_ref, dsk_ref, kpad_ref, vpad_ref, dkpad_ref, dvpad_ref):
        _pad_front(kpad_ref, k_ref)
        _pad_front(vpad_ref, v_ref)
        dkpad_ref[...] = jnp.zeros_like(dkpad_ref)
        dvpad_ref[...] = jnp.zeros_like(dvpad_ref)
        db_ref[...] = jnp.zeros_like(db_ref)
        sink = sink_ref[...]

        def step(n, dsink):
            w0 = pl.multiple_of(n * BLK, BLK)
            win = pl.ds(w0, 2 * BLK)
            qn = q_ref[n]
            kw = kpad_ref[win, :]
            ln = l_ref[n]
            p = jnp.exp(_swa_scores(n, qn, kw, bias_ref) - ln)
            dob = do_ref[n]
            dp = _dot_nt(dob, vpad_ref[win, :])
            delta = jnp.sum(p * dp, axis=1, keepdims=True)
            ds = p * (dp - delta)
            db_ref[...] += ds
            dsb = ds.astype(BF16)
            dq_ref[n] = (_dot(dsb, kw) * SCALE).astype(BF16)
            dkpad_ref[win, :] += _dot_tn(dsb, qn)
            dvpad_ref[win, :] += _dot_tn(p.astype(BF16), dob)
            return dsink - jnp.exp(sink - ln) * delta

        dsink = lax.fori_loop(0, nb, step, jnp.zeros((rows, 1), F32))
        for g in range(A_GROUP):
            tot = jnp.sum(dsink[g * BLK:(g + 1) * BLK, :], axis=0, keepdims=True)
            dsk_ref[g] = jnp.broadcast_to(tot, (1, LANE))
        dk_ref[...] = (dkpad_ref[BLK:, :] * SCALE).astype(BF16)
        dv_ref[...] = dvpad_ref[BLK:, :].astype(BF16)

    qspec = pl.BlockSpec((None, None, nb, rows, hd), lambda i, kv: (i, kv, 0, 0, 0))
    kspec = pl.BlockSpec((None, None, s, hd), lambda i, kv: (i, kv, 0, 0))
    return pl.pallas_call(
        body, name=name, grid=(b, hkv),
        out_shape=(jax.ShapeDtypeStruct((b, hkv, nb, rows, hd), BF16), jax.ShapeDtypeStruct((b, hkv, s, hd), BF16),
                   jax.ShapeDtypeStruct((b, hkv, s, hd), BF16), jax.ShapeDtypeStruct((b, hkv, rows, 2 * BLK), F32),
                   jax.ShapeDtypeStruct((b, hkv, A_GROUP, 1, LANE), F32)),
        in_specs=[qspec, kspec, kspec,
                  pl.BlockSpec((None, rows, 2 * BLK), lambda i, kv: (kv, 0, 0)),
                  pl.BlockSpec((None, rows, 1), lambda i, kv: (kv, 0, 0)),
                  qspec,
                  pl.BlockSpec((None, None, nb, rows, 1), lambda i, kv: (i, kv, 0, 0, 0))],
        out_specs=(qspec, kspec, kspec,
                   pl.BlockSpec((None, None, rows, 2 * BLK), lambda i, kv: (i, kv, 0, 0)),
                   pl.BlockSpec((None, None, A_GROUP, 1, LANE), lambda i, kv: (i, kv, 0, 0, 0))),
        scratch_shapes=[pltpu.VMEM((s + BLK, hd), BF16), pltpu.VMEM((s + BLK, hd), BF16),
                        pltpu.VMEM((s + BLK, hd), F32), pltpu.VMEM((s + BLK, hd), F32)],
        compiler_params=_cp(("parallel", "parallel")),
    )(q, k, v, bias, sinks, do, lse)


def _swa_small_grads(db, dsk, bucket, name):
    b, nh = db.shape[0], db.shape[1]

    def body(db_ref, dsk_ref, bk_ref, gb_ref, gs_ref):
        acc = db_ref[0]
        sk = dsk_ref[0]
        for i in range(1, b):
            acc = acc + db_ref[i]
            sk = sk + dsk_ref[i]
        gs_ref[...] = sk
        bk = bk_ref[...]
        for i in range(REL_BUCKETS):
            part = jnp.sum(jnp.where(bk == i, acc, 0.0), axis=1, keepdims=True)
            tot = jnp.sum(part, axis=0, keepdims=True)
            gb_ref[i:i + 1, :] = jnp.broadcast_to(tot, (1, LANE))

    return pl.pallas_call(
        body, name=name, grid=(nh,),
        out_shape=(jax.ShapeDtypeStruct((nh, REL_BUCKETS, LANE), F32), jax.ShapeDtypeStruct((nh, 1, LANE), F32)),
        in_specs=[pl.BlockSpec((b, None, BLK, 2 * BLK), lambda h: (0, h, 0, 0)),
                  pl.BlockSpec((b, None, 1, LANE), lambda h: (0, h, 0, 0)),
                  pl.BlockSpec((BLK, 2 * BLK), lambda h: (0, 0))],
        out_specs=(pl.BlockSpec((None, REL_BUCKETS, LANE), lambda h: (h, 0, 0)),
                   pl.BlockSpec((None, 1, LANE), lambda h: (h, 0, 0))),
        compiler_params=_cp(("parallel",)),
    )(db, dsk, bucket)


def _log_sigmoid(z):
    return jnp.minimum(z, 0.0) - jnp.log(1.0 + jnp.exp(-jnp.abs(z)))


def _fox_decay(z, bf, name):
    b, s, w = z.shape
    nb = s // BLK

    def body(z_ref, bf_ref, f_ref):
        r = lax.broadcasted_iota(jnp.int32, (BLK, BLK), 0)
        c = lax.broadcasted_iota(jnp.int32, (BLK, BLK), 1)
        tri = (c <= r).astype(F32)

        def step(n, carry):
            r0 = pl.multiple_of(n * BLK, BLK)
            lf = _log_sigmoid(z_ref[pl.ds(r0, BLK), :] + bf_ref[...])
            f_ref[pl.ds(r0, BLK), :] = jnp.dot(tri, lf, precision=HI, preferred_element_type=F32) + carry
            return carry + jnp.sum(lf, axis=0, keepdims=True)

        lax.fori_loop(0, nb, step, jnp.zeros((1, w), F32))

    spec = pl.BlockSpec((None, s, w), lambda i: (i, 0, 0))
    return pl.pallas_call(
        body, name=name, grid=(b,), out_shape=jax.ShapeDtypeStruct((b, s, w), F32),
        in_specs=[spec, pl.BlockSpec((1, w), lambda i: (0, 0))], out_specs=spec,
        compiler_params=_cp(("parallel",)),
    )(z, bf)


def _fox_dgate(df, z, bf, nheads, name):
    b, s, w = z.shape
    nb = s // BLK

    def body(df_ref, z_ref, bf_ref, dz_ref, dbf_ref):
        @pl.when(pl.program_id(0) == 0)
        def _():
            dbf_ref[...] = jnp.zeros_like(dbf_ref)

        r = lax.broadcasted_iota(jnp.int32, (BLK, BLK), 0)
        c = lax.broadcasted_iota(jnp.int32, (BLK, BLK), 1)
        tri = (c >= r).astype(F32)
        lane = lax.broadcasted_iota(jnp.int32, (BLK, w), 1)

        def step(i, carry):
            tail, dbf = carry
            r0 = pl.multiple_of((nb - 1 - i) * BLK, BLK)
            dfb = df_ref[pl.ds(r0, BLK), :]
            dlf = jnp.dot(tri, dfb, precision=HI, preferred_element_type=F32) + tail
            dz = jnp.where(lane < nheads, dlf * _sigmoid(-(z_ref[pl.ds(r0, BLK), :] + bf_ref[...])), 0.0)
            dz_ref[pl.ds(r0, BLK), :] = dz
            return tail + jnp.sum(dfb, axis=0, keepdims=True), dbf + jnp.sum(dz, axis=0, keepdims=True)

        zero = jnp.zeros((1, w), F32)
        _, dbf = lax.fori_loop(0, nb, step, (zero, zero))
        dbf_ref[...] += dbf

    spec = pl.BlockSpec((None, s, w), lambda i: (i, 0, 0))
    one = pl.BlockSpec((1, w), lambda i: (0, 0))
    return pl.pallas_call(
        body, name=name, grid=(b,),
        out_shape=(jax.ShapeDtypeStruct((b, s, w), F32), jax.ShapeDtypeStruct((1, w), F32)),
        in_specs=[spec, spec, one], out_specs=(spec, one),
        compiler_params=_cp(("arbitrary",)),
    )(df, z, bf)


def _lane_columns(src_ref, dst_ref, pair):
    val = src_ref[...]
    row = lax.broadcasted_iota(jnp.int32, (LANE, LANE), 0)
    for hh in range(2):
        pick = (row == 2 * pair + hh).astype(F32)
        dst_ref[hh] = jnp.dot(val, pick, precision=HI, preferred_element_type=F32)


def _fox_segments(nb):
    per = max(1, nb // 4)
    return per, nb // per


def _head_masks(shape, axis):
    idx = lax.broadcasted_iota(jnp.int32, shape, axis)
    return idx < HEAD_DIM, idx >= HEAD_DIM


def _fox_fwd(proj, qblk, kblk, vblk, fsum, frow, name):
    b, s, _ = proj.shape
    nh = frow.shape[1]
    npair = nh // 2
    BLK = min(s, FOX_BQ)
    assert s % BLK == 0
    per, nseg = _fox_segments(s // BLK)

    def body(q_ref, k_ref, v_ref, fs_ref, fr_ref, o_ref, l_ref, qm_ref, kt_ref, vb_ref, fc_ref):
        pair = pl.program_id(1)
        _lane_columns(fs_ref, fc_ref, pair)

        @pl.when(pair == 0)
        def _():
            l_ref[...] = jnp.zeros_like(l_ref)

        lo, hi = _head_masks((s, LANE), 1)
        qv = q_ref[...].astype(F32) * SCALE
        qm_ref[0] = jnp.where(lo, qv, 0.0).astype(BF16)
        qm_ref[1] = jnp.where(hi, qv, 0.0).astype(BF16)
        kt_ref[...] = k_ref[...].astype(F32).T.astype(BF16)
        vb_ref[...] = v_ref[...].astype(BF16)
        lane = lax.broadcasted_iota(jnp.int32, (BLK, LANE), 1)
        lane_lo = lane < HEAD_DIM
        tail = per * BLK
        causal = (lax.broadcasted_iota(jnp.int32, (BLK, tail), 1)
                  - lax.broadcasted_iota(jnp.int32, (BLK, tail), 0))
        for seg in range(nseg):
            w = (seg + 1) * tail

            def qstep(n, carry):
                r0 = pl.multiple_of(n * BLK, BLK)
                outs = []
                lse = l_ref[pl.ds(r0, BLK), :]
                for hh in range(2):
                    sc = _dot(qm_ref[hh, pl.ds(r0, BLK), :], kt_ref[:, :w])
                    sc = sc + (fc_ref[hh, pl.ds(r0, BLK), 0:1] - fr_ref[hh, :, :w])
                    masked = jnp.where(causal <= (n - seg * per) * BLK, sc[:, w - tail:], NEG)
                    sc = masked if seg == 0 else jnp.concatenate([sc[:, :w - tail], masked], axis=1)
                    m = jnp.max(sc, axis=1, keepdims=True)
                    e = jnp.exp(sc - m)
                    l = jnp.sum(e, axis=1, keepdims=True)
                    outs.append(_dot((e * (1.0 / l)).astype(BF16), vb_ref[:w, :]))
                    lse = jnp.where(lane == 2 * pair + hh, m + jnp.log(l), lse)
                l_ref[pl.ds(r0, BLK), :] = lse
                o_ref[pl.ds(r0, BLK), :] = jnp.where(lane_lo, outs[0], outs[1])
                return carry

            lax.fori_loop(seg * per, (seg + 1) * per, qstep, 0)

    def tok(blk):
        return pl.BlockSpec((None, s, LANE), lambda i, p: (i, 0, blk + p))

    whole = pl.BlockSpec((None, s, LANE), lambda i, p: (i, 0, 0))
    rowspec = pl.BlockSpec((None, 2, 1, s), lambda i, p: (i, p, 0, 0))
    return pl.pallas_call(
        body, name=name, grid=(b, npair),
        out_shape=(jax.ShapeDtypeStruct((b, s, nh * HEAD_DIM), F32), jax.ShapeDtypeStruct((b, s, LANE), F32)),
        in_specs=[tok(qblk), tok(kblk), tok(vblk), whole, rowspec],
        out_specs=(pl.BlockSpec((None, s, LANE), lambda i, p: (i, 0, p)), whole),
        scratch_shapes=[pltpu.VMEM((2, s, LANE), BF16), pltpu.VMEM((LANE, s), BF16), pltpu.VMEM((s, LANE), BF16),
                        pltpu.VMEM((2, s, LANE), F32)],
        compiler_params=_cp(("parallel", "arbitrary")),
    )(proj, proj, proj, fsum, frow)


def _fox_bwd(proj, qblk, kblk, vblk, dmix, doblk, fsum, frow, frowb, lse, lserowb, name):
    b, s, _ = proj.shape
    nh = frow.shape[1]
    npair = nh // 2
    BLK = min(s, FOX_BQ)
    assert s % BLK == 0
    nb = s // BLK
    per, nseg = _fox_segments(nb)

    def body(q_ref, k_ref, v_ref, do_ref, fs_ref, fr_ref, frb_ref, ls_ref, lrb_ref,
             dq_ref, dk_ref, dv_ref, dfr_ref,
             qm_ref, dom_ref, kb_ref, vb_ref, kt_ref, vt_ref, qtm_ref, dotm_ref, dka_ref, dva_ref, fc_ref, l_ref):
        pair = pl.program_id(1)
        _lane_columns(fs_ref, fc_ref, pair)
        _lane_columns(ls_ref, l_ref, pair)
        lo, hi = _head_masks((s, LANE), 1)
        qv = q_ref[...].astype(F32) * SCALE
        dov = do_ref[...]
        for hh, msk in enumerate((lo, hi)):
            qm_ref[hh] = jnp.where(msk, qv, 0.0).astype(BF16)
            dom_ref[hh] = jnp.where(msk, dov, 0.0).astype(BF16)
        kv = k_ref[...].astype(F32)
        vv = v_ref[...].astype(F32)
        kb_ref[...] = kv.astype(BF16)
        vb_ref[...] = vv.astype(BF16)
        kt_ref[...] = kv.T.astype(BF16)
        vt_ref[...] = vv.T.astype(BF16)
        rlo, rhi = _head_masks((LANE, BLK), 0)

        def tstep(n, carry):
            r0 = pl.multiple_of(n * BLK, BLK)
            qt = (q_ref[pl.ds(r0, BLK), :].astype(F32) * SCALE).T
            dt = do_ref[pl.ds(r0, BLK), :].astype(F32).T
            for hh, msk in enumerate((rlo, rhi)):
                qtm_ref[hh, n] = jnp.where(msk, qt, 0.0).astype(BF16)
                dotm_ref[hh, n] = jnp.where(msk, dt, 0.0).astype(BF16)
            return carry

        lax.fori_loop(0, nb, tstep, 0)
        dka_ref[...] = jnp.zeros_like(dka_ref)
        dva_ref[...] = jnp.zeros_like(dva_ref)
        dfr_ref[...] = jnp.zeros_like(dfr_ref)
        lane_lo = lax.broadcasted_iota(jnp.int32, (BLK, LANE), 1) < HEAD_DIM
        tail = per * BLK
        causal = (lax.broadcasted_iota(jnp.int32, (BLK, tail), 1)
                  - lax.broadcasted_iota(jnp.int32, (BLK, tail), 0))
        causal_t = (lax.broadcasted_iota(jnp.int32, (tail, BLK), 0)
                    - lax.broadcasted_iota(jnp.int32, (tail, BLK), 1))
        for seg in range(nseg):
            w = (seg + 1) * tail

            def nstep(n, carry):
                r0 = pl.multiple_of(n * BLK, BLK)
                lim = (n - seg * per) * BLK
                dqs = []
                for hh in range(2):
                    qn = qm_ref[hh, pl.ds(r0, BLK), :]
                    don = dom_ref[hh, pl.ds(r0, BLK), :]
                    sc = _dot(qn, kt_ref[:, :w]) + ((fc_ref[hh, pl.ds(r0, BLK), 0:1] - l_ref[hh, pl.ds(r0, BLK), 0:1])
                                                   - fr_ref[hh, :, :w])
                    masked = jnp.where(causal <= lim, sc[:, w - tail:], NEG)
                    p = jnp.exp(masked if seg == 0 else jnp.concatenate([sc[:, :w - tail], masked], axis=1))
                    dp = _dot(don, vt_ref[:, :w])
                    ds = p * (dp - jnp.sum(p * dp, axis=1, keepdims=True))
                    dqs.append(_dot(ds.astype(BF16), kb_ref[:w, :]))
                    dfr_ref[hh, :, :w] -= jnp.sum(ds, axis=0, keepdims=True)
                    sct = _dot(kb_ref[:w, :], qtm_ref[hh, n]) + ((frb_ref[hh, n] - lrb_ref[hh, n]) - fc_ref[hh, :w, 0:1])
                    masked_t = jnp.where(causal_t <= lim, sct[w - tail:, :], NEG)
                    pt = jnp.exp(masked_t if seg == 0 else jnp.concatenate([sct[:w - tail, :], masked_t], axis=0))
                    dpt = _dot(vb_ref[:w, :], dotm_ref[hh, n])
                    dst = pt * (dpt - jnp.sum(pt * dpt, axis=0, keepdims=True))
                    dka_ref[:w, :] += _dot(dst.astype(BF16), qn)
                    dva_ref[:w, :] += _dot(pt.astype(BF16), don)
                dq_ref[pl.ds(r0, BLK), :] = (jnp.where(lane_lo, dqs[0], dqs[1]) * SCALE).astype(BF16)
                return carry

            lax.fori_loop(seg * per, (seg + 1) * per, nstep, 0)
        dk_ref[...] = dka_ref[...].astype(BF16)
        dv_ref[...] = dva_ref[...].astype(BF16)

    def tok(blk):
        return pl.BlockSpec((None, s, LANE), lambda i, p: (i, 0, blk + p))

    whole = pl.BlockSpec((None, s, LANE), lambda i, p: (i, 0, 0))
    rowspec = pl.BlockSpec((None, 2, 1, s), lambda i, p: (i, p, 0, 0))
    rowbspec = pl.BlockSpec((None, 2, nb, 1, BLK), lambda i, p: (i, p, 0, 0, 0))
    outtok = pl.BlockSpec((None, s, LANE), lambda i, p: (i, 0, p))
    shp = jax.ShapeDtypeStruct((b, s, nh * HEAD_DIM), BF16)
    return pl.pallas_call(
        body, name=name, grid=(b, npair),
        out_shape=(shp, shp, shp, jax.ShapeDtypeStruct((b, nh, 1, s), F32)),
        in_specs=[tok(qblk), tok(kblk), tok(vblk),
                  pl.BlockSpec((None, s, LANE), lambda i, p: (i, 0, doblk + p)),
                  whole, rowspec, rowbspec, whole, rowbspec],
        out_specs=(outtok, outtok, outtok, rowspec),
        scratch_shapes=[pltpu.VMEM((2, s, LANE), BF16), pltpu.VMEM((2, s, LANE), BF16),
                        pltpu.VMEM((s, LANE), BF16), pltpu.VMEM((s, LANE), BF16),
                        pltpu.VMEM((LANE, s), BF16), pltpu.VMEM((LANE, s), BF16),
                        pltpu.VMEM((2, nb, LANE, BLK), BF16), pltpu.VMEM((2, nb, LANE, BLK), BF16),
                        pltpu.VMEM((s, LANE), F32), pltpu.VMEM((s, LANE), F32),
                        pltpu.VMEM((2, s, LANE), F32), pltpu.VMEM((2, s, LANE), F32)],
        compiler_params=_cp(("parallel", "parallel")),
    )(proj, proj, proj, dmix, fsum, frow, frowb, lse, lserowb)


def _expm1(x):
    poly = x * (1.0 + x * (1.0 / 2 + x * (1.0 / 6 + x * (1.0 / 24 + x * (1.0 / 120 + x * (1.0 / 720))))))
    return jnp.where(x > -0.1, poly, jnp.exp(x) - 1.0)


def _softplus(z):
    return jnp.maximum(z, 0.0) + jnp.log(1.0 + jnp.exp(-jnp.abs(z)))


def _scan_rows(a, u, carry, row, up):
    tc, c = a.shape
    d = 1
    while d < tc:
        if d < SUBLANE:
            keep = (row >= d) if up else (row < tc - d)
            shift = d if up else tc - d
            a_sh = jnp.where(keep, pltpu.roll(a, shift, 0), 1.0)
            u_sh = jnp.where(keep, pltpu.roll(u, shift, 0), 0.0)
        elif up:
            a_sh = jnp.concatenate([jnp.ones((d, c), F32), a[:tc - d]], axis=0)
            u_sh = jnp.concatenate([jnp.zeros((d, c), F32), u[:tc - d]], axis=0)
        else:
            a_sh = jnp.concatenate([a[d:], jnp.ones((d, c), F32)], axis=0)
            u_sh = jnp.concatenate([u[d:], jnp.zeros((d, c), F32)], axis=0)
        u = a * u_sh + u
        a = a * a_sh
        d *= 2
    return u + a * carry


def _scan_up(a, u, carry, row):
    return _scan_rows(a, u, carry, row, True)


def _scan_down(bnext, g, carry, row):
    return _scan_rows(bnext, g, carry, row, False)


def _pick_row(val, row, which):
    return jnp.sum(jnp.where(row == which, val, 0.0), axis=0, keepdims=True)


def _lru_gates(xpad_ref, t0, tc, cw_ref, cb_ref, wa, ba_ref, wx, bx_ref, sp):
    xw = xpad_ref[pl.ds(t0, tc + SUBLANE), :]
    xc = cb_ref[...]
    for j in range(CONV_WIDTH):
        sh = CONV_WIDTH - 1 - j
        xs = xw if sh == 0 else pltpu.roll(xw, sh, 0)
        xc = xc + xs[SUBLANE:, :] * cw_ref[j:j + 1, :]
    xcb = xc.astype(BF16)
    r = _sigmoid(_dot(xcb, wa) + ba_ref[...])
    i = _sigmoid(_dot(xcb, wx) + bx_ref[...])
    la = -LRU_C * r * sp
    return xc, r, i, la


def _lru_specs(s, cb):
    seq = lambda bi, ni: (bi, 0, ni)
    return dict(
        seq=pl.BlockSpec((None, s, cb), seq),
        cw=pl.BlockSpec((CONV_WIDTH, cb), lambda bi, ni: (0, ni)),
        vec=pl.BlockSpec((1, cb), lambda bi, ni: (0, ni)),
        wblk=pl.BlockSpec((None, cb, cb), lambda bi, ni: (ni, 0, 0)),
    )


def _lru_fwd(proj, cw, cb_, wa, ba, wx, bx, lam, name):
    b, s, _ = proj.shape
    nblk, cb, _ = wa.shape
    tc = min(s, SCAN_CHUNK)
    nc = s // tc

    def body(x_ref, cw_ref, cb_ref, wa_ref, ba_ref, wx_ref, bx_ref, lam_ref, hs_ref, xpad_ref):
        xpad_ref[0:SUBLANE, :] = jnp.zeros((SUBLANE, cb), F32)
        xpad_ref[SUBLANE:, :] = x_ref[...].astype(F32)
        wa_b = wa_ref[...].astype(BF16)
        wx_b = wx_ref[...].astype(BF16)
        sp = _softplus(-lam_ref[...])
        row = lax.broadcasted_iota(jnp.int32, (tc, cb), 0)

        def chunk(ci, carry):
            t0 = pl.multiple_of(ci * tc, tc)
            xc, r, i, la = _lru_gates(xpad_ref, t0, tc, cw_ref, cb_ref, wa_b, ba_ref, wx_b, bx_ref, sp)
            a = jnp.exp(la)
            u = jnp.sqrt(-_expm1(2.0 * la)) * (i * xc)
            h = _scan_up(a, u, carry, row)
            hs_ref[pl.ds(t0, tc), :] = h
            return _pick_row(h, row, tc - 1)

        lax.fori_loop(0, nc, chunk, jnp.zeros((1, cb), F32))

    sp_ = _lru_specs(s, cb)
    return pl.pallas_call(
        body, name=name, grid=(b, nblk),
        out_shape=jax.ShapeDtypeStruct((b, s, nblk * cb), F32),
        in_specs=[sp_["seq"], sp_["cw"], sp_["vec"], sp_["wblk"], sp_["vec"], sp_["wblk"], sp_["vec"], sp_["vec"]],
        out_specs=sp_["seq"],
        scratch_shapes=[pltpu.VMEM((s + SUBLANE, cb), F32)],
        compiler_params=_cp(("parallel", "parallel")),
    )(proj, cw, cb_, wa, ba, wx, bx, lam)


def _lru_bwd(proj, hs, dhs, cw, cb_, wa, ba, wx, bx, lam, name):
    b, s, _ = proj.shape
    nblk, cb, _ = wa.shape
    tc = min(s, SCAN_CHUNK)
    nc = s // tc

    def body(x_ref, hs_ref, dhs_ref, cw_ref, cb_ref, wa_ref, ba_ref, wx_ref, bx_ref, lam_ref,
             dx_ref, dcw_ref, dcb_ref, dwa_ref, dba_ref, dwx_ref, dbx_ref, dlam_ref,
             xpad_ref, hpad_ref, dcpad_ref, xc_ref, r_ref, i_ref, a_ref, mult_ref):
        @pl.when(pl.program_id(1) == 0)
        def _():
            for ref in (dcw_ref, dcb_ref, dwa_ref, dba_ref, dwx_ref, dbx_ref, dlam_ref):
                ref[...] = jnp.zeros_like(ref)

        zeros8 = jnp.zeros((SUBLANE, cb), F32)
        xpad_ref[0:SUBLANE, :] = zeros8
        xpad_ref[SUBLANE:, :] = x_ref[...].astype(F32)
        hpad_ref[0:SUBLANE, :] = zeros8
        hpad_ref[SUBLANE:, :] = hs_ref[...]
        dcpad_ref[s:s + SUBLANE, :] = zeros8
        wa_b = wa_ref[...].astype(BF16)
        wx_b = wx_ref[...].astype(BF16)
        lam_v = lam_ref[...]
        sp = _softplus(-lam_v)
        dsp_dlam = -_sigmoid(-lam_v)
        row = lax.broadcasted_iota(jnp.int32, (tc, cb), 0)

        def recompute(ci, carry):
            t0 = pl.multiple_of(ci * tc, tc)
            xc, r, i, la = _lru_gates(xpad_ref, t0, tc, cw_ref, cb_ref, wa_b, ba_ref, wx_b, bx_ref, sp)
            xc_ref[pl.ds(t0, tc), :] = xc
            r_ref[pl.ds(t0, tc), :] = r
            i_ref[pl.ds(t0, tc), :] = i
            a_ref[pl.ds(t0, tc), :] = jnp.exp(la)
            mult_ref[pl.ds(t0, tc), :] = jnp.sqrt(-_expm1(2.0 * la))
            return carry

        lax.fori_loop(0, nc, recompute, 0)

        def adjoint(k, carry):
            g_next, a_first_next = carry
            t0 = pl.multiple_of((nc - 1 - k) * tc, tc)
            a = a_ref[pl.ds(t0, tc), :]
            a_next = jnp.where(row == tc - 1, a_first_next, pltpu.roll(a, tc - 1, 0))
            gg = _scan_down(a_next, dhs_ref[pl.ds(t0, tc), :], g_next, row)
            h_prev = pltpu.roll(hpad_ref[pl.ds(t0, tc + SUBLANE), :], 1, 0)[SUBLANE:, :]
            xc = xc_ref[pl.ds(t0, tc), :]
            r = r_ref[pl.ds(t0, tc), :]
            i = i_ref[pl.ds(t0, tc), :]
            mult = mult_ref[pl.ds(t0, tc), :]
            d_mult = gg * i * xc
            d_i = gg * mult * xc
            d_xc = gg * mult * i
            d_la = gg * h_prev * a - d_mult * (a * a) / mult
            d_zr = (d_la * (-LRU_C * sp)) * r * (1.0 - r)
            d_zi = d_i * i * (1.0 - i)
            dlam_ref[...] += jnp.sum(d_la * (-LRU_C * r), axis=0, keepdims=True) * dsp_dlam
            dzr_b = d_zr.astype(BF16)
            dzi_b = d_zi.astype(BF16)
            xcb = xc.astype(BF16)
            d_xc = d_xc + _dot_nt(dzr_b, wa_b) + _dot_nt(dzi_b, wx_b)
            dwa_ref[...] += _dot_tn(xcb, dzr_b)
            dwx_ref[...] += _dot_tn(xcb, dzi_b)
            dba_ref[...] += jnp.sum(d_zr, axis=0, keepdims=True)
            dbx_ref[...] += jnp.sum(d_zi, axis=0, keepdims=True)
            dcb_ref[...] += jnp.sum(d_xc, axis=0, keepdims=True)
            dcpad_ref[pl.ds(t0, tc), :] = d_xc
            return _pick_row(gg, row, 0), _pick_row(a, row, 0)

        zero = jnp.zeros((1, cb), F32)
        lax.fori_loop(0, nc, adjoint, (zero, zero))

        def conv_back(ci, carry):
            t0 = pl.multiple_of(ci * tc, tc)
            dw = dcpad_ref[pl.ds(t0, tc + SUBLANE), :]
            xw = xpad_ref[pl.ds(t0, tc + SUBLANE), :]
            d_xc = dw[:tc, :]
            dxr = jnp.zeros((tc, cb), F32)
            for j in range(CONV_WIDTH):
                sh = CONV_WIDTH - 1 - j
                dsh = dw if sh == 0 else pltpu.roll(dw, tc + SUBLANE - sh, 0)
                dxr = dxr + dsh[:tc, :] * cw_ref[j:j + 1, :]
                xs = xw if sh == 0 else pltpu.roll(xw, sh, 0)
                dcw_ref[j:j + 1, :] += jnp.sum(d_xc * xs[SUBLANE:, :], axis=0, keepdims=True)
            dx_ref[pl.ds(t0, tc), :] = dxr.astype(BF16)
            return carry

        lax.fori_loop(0, nc, conv_back, 0)

    seq = lambda ni, bi: (bi, 0, ni)
    seqspec = pl.BlockSpec((None, s, cb), seq)
    cwspec = pl.BlockSpec((CONV_WIDTH, cb), lambda ni, bi: (0, ni))
    vec = pl.BlockSpec((1, cb), lambda ni, bi: (0, ni))
    wblk = pl.BlockSpec((None, cb, cb), lambda ni, bi: (ni, 0, 0))
    w = nblk * cb
    return pl.pallas_call(
        body, name=name, grid=(nblk, b),
        out_shape=(jax.ShapeDtypeStruct((b, s, w), BF16), jax.ShapeDtypeStruct((CONV_WIDTH, w), F32),
                   jax.ShapeDtypeStruct((1, w), F32), jax.ShapeDtypeStruct((nblk, cb, cb), F32),
                   jax.ShapeDtypeStruct((1, w), F32), jax.ShapeDtypeStruct((nblk, cb, cb), F32),
                   jax.ShapeDtypeStruct((1, w), F32), jax.ShapeDtypeStruct((1, w), F32)),
        in_specs=[seqspec, seqspec, seqspec, cwspec, vec, wblk, vec, wblk, vec, vec],
        out_specs=(seqspec, cwspec, vec, wblk, vec, wblk, vec, vec),
        scratch_shapes=[pltpu.VMEM((s + SUBLANE, cb), F32)] * 3 + [pltpu.VMEM((s, cb), F32)] * 5,
        compiler_params=_cp(("parallel", "arbitrary")),
    )(proj, hs, dhs, cw, cb_, wa, ba, wx, bx, lam)


def _adamw(w, g, m, v, name):
    shape = w.shape
    total = int(np.prod(shape))
    if w.ndim >= 2 and shape[-2] % SUBLANE == 0:
        rows, cols = shape[-2:]
    else:
        cols = 1024
        rows = -(-(-(-total // cols)) // SUBLANE) * SUBLANE
    lead = -(-total // (rows * cols))
    tr = _row_tile(rows, 512)
    pad = lead * rows * cols - total

    def flat(a):
        if pad:
            a = jnp.pad(a.reshape(-1), (0, pad))
        return a.reshape(lead, rows, cols)

    c1 = 1.0 - ADAM_B1 ** ADAM_STEP
    c2 = 1.0 - ADAM_B2 ** ADAM_STEP

    def body(w_ref, g_ref, m_ref, v_ref, d_ref, nm_ref, nv_ref):
        gv = g_ref[...]
        nm = ADAM_B1 * m_ref[...] + (1.0 - ADAM_B1) * gv
        nv = ADAM_B2 * v_ref[...] + (1.0 - ADAM_B2) * (gv * gv)
        nm_ref[...] = nm
        nv_ref[...] = nv
        d_ref[...] = -ADAM_LR * ((nm / c1) / (jnp.sqrt(nv / c2) + ADAM_EPS) + ADAM_WD * w_ref[...])

    spec = pl.BlockSpec((None, tr, cols), lambda l, i: (l, i, 0))
    shp = jax.ShapeDtypeStruct((lead, rows, cols), F32)
    outs = pl.pallas_call(
        body, name=name, grid=(lead, rows // tr), out_shape=(shp, shp, shp),
        in_specs=[spec] * 4, out_specs=(spec,) * 3,
        compiler_params=_cp(("parallel", "parallel")),
    )(flat(w), flat(g), flat(m), flat(v))
    if pad:
        return tuple(o.reshape(-1)[:total].reshape(shape) for o in outs)
    return tuple(o.reshape(shape) for o in outs)


def _to_heads(t, nh):
    b, s, _ = t.shape
    return t.reshape(b, s, nh, HEAD_DIM).transpose(0, 2, 1, 3)


def _stack_heads(t):
    b, s, _ = t.shape
    t = t.reshape(b, s // BLK, BLK, A_KV_HEADS, A_GROUP, HEAD_DIM).transpose(0, 3, 1, 4, 2, 5)
    return t.reshape(b, A_KV_HEADS, s // BLK, A_GROUP * BLK, HEAD_DIM)


def _unstack_heads(t):
    b, hkv, nb, rows, hd = t.shape
    t = t.reshape(b, hkv, nb, A_GROUP, BLK, hd).transpose(0, 2, 4, 1, 3, 5)
    return t.reshape(b, nb * BLK, hkv * A_GROUP * hd)


def _from_heads(t):
    b, nh, s, hd = t.shape
    return t.transpose(0, 2, 1, 3).reshape(b, s, nh * hd)


def _pad_rows(a, mult):
    r = a.shape[0]
    p = (-r) % mult
    return jnp.pad(a, ((0, p), (0, 0))) if p else a


def kernel(x, c, rel_bias, norm_g, ada_w, ada_b, attn_w_in, attn_sinks, attn_b_f, attn_w_out, lru_w_in, lru_conv_w, lru_conv_b, lru_w_a, lru_b_a, lru_w_x, lru_b_x, lru_lambda, lru_w_out, final_g, loss_target, m_rel_bias, m_norm_g, m_ada_w, m_ada_b, m_attn_w_in, m_attn_sinks, m_attn_b_f, m_attn_w_out, m_lru_w_in, m_lru_conv_w, m_lru_conv_b, m_lru_w_a, m_lru_b_a, m_lru_w_x, m_lru_b_x, m_lru_lambda, m_lru_w_out, m_final_g, v_rel_bias, v_norm_g, v_ada_w, v_ada_b, v_attn_w_in, v_attn_sinks, v_attn_b_f, v_attn_w_out, v_lru_w_in, v_lru_conv_w, v_lru_conv_b, v_lru_w_a, v_lru_b_a, v_lru_w_x, v_lru_b_x, v_lru_lambda, v_lru_w_out, v_final_g):
    bl, s, d = x.shape
    ix, iy, ic = lax.axis_index("x"), lax.axis_index("y"), lax.axis_index("c")
    chip = 2 * ix + iy
    me = 2 * chip + ic
    nb = s // BLK
    aw = A_Q_HEADS * HEAD_DIM
    akv = A_KV_HEADS * HEAD_DIM
    bw = B_HEADS * HEAD_DIM
    mixw = aw + bw
    qkv_w = aw + 2 * akv + 3 * bw
    n_in = attn_w_in.shape[2] * N_CHIP
    lw = lru_lambda.shape[1] * N_CHIP
    n0 = mixw + qkv_w + LANE

    rows_pad = -(-bl // SUBLANE) * SUBLANE
    vec_rows = jnp.concatenate([lru_conv_w[0], lru_conv_b, lru_b_a, lru_b_x, lru_lambda], axis=0)
    first = jnp.concatenate([_pad_rows(c, SUBLANE), jnp.pad(vec_rows, ((0, 0), (0, d - lw // N_CHIP)))], axis=0)
    first = _all_gather8(first, "gather_c", pltpu.VMEM).reshape(N_DEV, rows_pad + SUBLANE, d)
    c_all = first[:, :bl].reshape(N_DEV * bl, d)
    vec_all = first[:, rows_pad:, :lw // N_CHIP].reshape(N_CHIP, 2, SUBLANE, lw // N_CHIP)[:, 0]
    vec_all = vec_all.transpose(1, 0, 2).reshape(SUBLANE, lw)
    ncol = ada_w.shape[2]
    ada_w_l = lax.dynamic_index_in_dim(ada_w, ic, 0, keepdims=False)
    ada_b_l = lax.dynamic_slice(ada_b, (ic, chip * ncol), (1, ncol))
    mod_part = _ada_fwd(c_all, ada_w_l, ada_b_l, "ada_fwd")
    mod_all = _all_gather8(_pad_rows(mod_part, SUBLANE), "gather_mod", pltpu.VMEM)
    mrows = -(-(N_DEV * bl) // SUBLANE) * SUBLANE
    mod_all = mod_all.reshape(N_CHIP, 2, mrows, ncol)[:, :, :N_DEV * bl]
    mod_all = mod_all.transpose(1, 2, 0, 3).reshape(2, N_DEV * bl, N_CHIP * ncol)
    mod = lax.dynamic_slice_in_dim(mod_all, me * bl, bl, axis=1)
    shift = [mod[l, :, 0:d].reshape(bl, 1, d) for l in range(2)]
    scale = [mod[l, :, d:2 * d].reshape(bl, 1, d) for l in range(2)]
    gmod = [mod[l, :, 2 * d:3 * d].reshape(bl, 1, d) for l in range(2)]

    c_in0 = n_in // N_CHIP
    c_in1 = 2 * lw // N_CHIP
    assert c_in0 <= d and 2 * c_in1 == d
    r_in0, r_out0, r_in1, r_out1 = d // 2, mixw // N_CHIP // 2, d // 4, lw // N_CHIP // 2
    o_out0, o_in1, o_out1 = r_in0, r_in0 + r_out0, r_in0 + r_out0 + r_in1
    big_rows = o_out1 + r_out1

    def half_of(a, rows):
        return lax.dynamic_slice_in_dim(a, ic * rows, rows, axis=0)

    h_in1 = half_of(lru_w_in[0], r_in0).astype(BF16)
    my_half = jnp.concatenate([
        jnp.pad(half_of(attn_w_in[0], r_in0).astype(BF16), ((0, 0), (0, d - c_in0))),
        half_of(attn_w_out[0], r_out0).astype(BF16),
        jnp.concatenate([h_in1[:r_in1], h_in1[r_in1:]], axis=1),
        half_of(lru_w_out[0], r_out1).astype(BF16)], axis=0)
    gat = _all_gather8(my_half, "gather_weights", pltpu.HBM).reshape(N_CHIP, 2, big_rows, d)
    w_in0 = gat[:, :, :r_in0, :c_in0].transpose(1, 2, 0, 3).reshape(d, n_in)
    w_out0 = gat[:, :, o_out0:o_in1].reshape(mixw, d)
    w_in1 = gat[:, :, o_in1:o_out1].reshape(N_CHIP, 2, r_in1, 2, c_in1)
    w_in1 = w_in1.transpose(1, 3, 2, 0, 4).reshape(d, 2 * lw)
    w_out1 = gat[:, :, o_out1:].reshape(lw, d)
    w_cat0 = jnp.concatenate([w_in0[:, qkv_w + B_HEADS:], w_in0[:, :qkv_w + B_HEADS],
                              jnp.zeros((d, n0 - n_in), BF16)], axis=1)

    proj0, h0, zf = _norm_proj(x, norm_g[0:1], scale[0], shift[0], w_cat0, LANE, "norm_proj0")
    o_a = mixw
    aq = _stack_heads(proj0[:, :, o_a:o_a + aw].astype(BF16))
    ak = _to_heads(proj0[:, :, o_a + aw:o_a + aw + akv].astype(BF16), A_KV_HEADS)
    av = _to_heads(proj0[:, :, o_a + aw + akv:o_a + aw + 2 * akv].astype(BF16), A_KV_HEADS)
    o_b = o_a + aw + 2 * akv
    fox_blks = (o_b // LANE, (o_b + bw) // LANE, (o_b + 2 * bw) // LANE)
    bucket_np, valid_np = _rel_buckets()
    bucket = jnp.asarray(bucket_np)
    bias = _swa_bias(rel_bias.T, bucket, jnp.asarray(valid_np), "swa_bias")
    bias = bias.reshape(A_KV_HEADS, A_GROUP * BLK, 2 * BLK)
    sinks = jnp.repeat(attn_sinks[0].reshape(A_KV_HEADS, A_GROUP), BLK, axis=1).reshape(A_KV_HEADS, A_GROUP * BLK, 1)
    a_out, a_lse = _swa_fwd(aq, ak, av, bias, sinks, "swa_fwd")
    bf_pad = jnp.pad(attn_b_f, ((0, 0), (0, LANE - B_HEADS)))
    fsum = _fox_decay(zf, bf_pad, "fox_decay")
    fh = fsum[:, :, :B_HEADS].transpose(0, 2, 1)
    frow = fh.reshape(bl, B_HEADS, 1, s)
    fbq = min(s, FOX_BQ)
    frowb = fh.reshape(bl, B_HEADS, s // fbq, 1, fbq)
    b_out, b_lse = _fox_fwd(proj0, *fox_blks, fsum, frow, "fox_fwd")
    lserowb = b_lse[:, :, :B_HEADS].transpose(0, 2, 1).reshape(bl, B_HEADS, s // fbq, 1, fbq)
    mix0 = [_unstack_heads(a_out), b_out]
    x1, o0 = _gate_outproj(mix0, proj0, 0, w_out0, x, gmod[0], "gate_outproj0")

    proj1, h1 = _norm_proj(x1, norm_g[1:2], scale[1], shift[1], w_in1, 0, "norm_proj1")
    cw_f, cb_f, ba_f, bx_f, lam_f = vec_all[0:4], vec_all[4:5], vec_all[5:6], vec_all[6:7], vec_all[7:8]
    hs = _lru_fwd(proj1, cw_f, cb_f, lru_w_a[0], ba_f, lru_w_x[0], bx_f, lam_f, "lru_fwd")
    x2, o1 = _gate_outproj([hs], proj1, 1, w_out1, x1, gmod[1], "gate_outproj1")

    loss_vec, dx2, g_final = _final_loss(x2, final_g.reshape(1, d), loss_target, "final_loss")
    loss = lax.psum(loss_vec[0, 0], ("x", "y", "c"))

    dhs, dgate1, do1, y1, dgm1 = _bwd_out(dx2, gmod[1], o1, [hs], proj1, 1, w_out1.T, F32, "bwd_out1")
    g_w_out1 = _matmul_tn(y1, [do1], "grad_w_out1")
    (dxr, g_cw, g_cb, g_wa, g_ba, g_wx, g_bx, g_lam) = _lru_bwd(
        proj1, hs, dhs, cw_f, cb_f, lru_w_a[0], ba_f, lru_w_x[0], bx_f, lam_f, "lru_bwd")
    dproj1 = [dxr, dgate1]
    g_w_in1 = _matmul_tn(h1, dproj1, "grad_w_in1")
    dx1, dsh1, dsc1, g_ng1 = _bwd_in(dproj1, w_in1.T, x1, norm_g[1:2], scale[1], dx2, "bwd_in1")

    dmix0, dgate0, do0, y0, dgm0 = _bwd_out(dx1, gmod[0], o0, mix0, proj0, 0, w_out0.T, BF16, "bwd_out0")
    g_w_out0 = _matmul_tn(y0, [do0], "grad_w_out0")
    da_out = _stack_heads(dmix0[:, :, :aw].astype(BF16))
    daq, dak, dav, dbias, dsink = _swa_bwd(aq, ak, av, bias, sinks, da_out, a_lse, "swa_bwd")
    dbq, dbk, dbv, dfrow = _fox_bwd(proj0, *fox_blks, dmix0, aw // LANE, fsum, frow, frowb, b_lse, lserowb,
                                    "fox_bwd")
    df = dfrow.reshape(bl, B_HEADS, s).transpose(0, 2, 1)
    df = jnp.pad(df, ((0, 0), (0, 0), (0, LANE - B_HEADS)))
    dzf, g_bf = _fox_dgate(df, zf, bf_pad, B_HEADS, "fox_dgate")
    dproj0 = ([dgate0, _unstack_heads(daq), _from_heads(dak), _from_heads(dav)]
              + [dbq, dbk, dbv, dzf.astype(BF16)])
    g_w_cat0 = _matmul_tn(h0, dproj0, "grad_w_in0")
    g_w_in0 = jnp.concatenate([g_w_cat0[:, mixw:mixw + qkv_w + B_HEADS], g_w_cat0[:, :mixw]], axis=1)
    dx0, dsh0, dsc0, g_ng0 = _bwd_in(dproj0, w_cat0.T, x, norm_g[0:1], scale[0], dx1, "bwd_in0")
    g_relb, g_sink = _swa_small_grads(dbias.reshape(bl, A_Q_HEADS, BLK, 2 * BLK),
                                      dsink.reshape(bl, A_Q_HEADS, 1, LANE), bucket, "swa_small_grads")

    dmod = jnp.concatenate([jnp.concatenate([dsh0, dsc0, dgm0], axis=-1),
                            jnp.concatenate([dsh1, dsc1, dgm1], axis=-1)], axis=1)
    dmod_rows = _pad_rows(dmod.reshape(bl * 6, d), SUBLANE)

    tail = jnp.concatenate([g_relb[:, :, 0].T.reshape(-1), g_sink[:, 0, 0], g_bf[0, :B_HEADS]])
    n_relb = REL_BUCKETS * A_Q_HEADS
    small_rows = [g_wa.reshape(-1, d), g_wx.reshape(-1, d), g_ng0, g_ng1, g_final, g_cw, g_cb, g_ba, g_bx, g_lam,
                  jnp.pad(tail, (0, d - tail.shape[0])).reshape(1, d)]
    small_counts = [r.shape[0] for r in small_rows]
    piece_rows = -(-(-(-sum(small_counts) // N_DEV)) // SUBLANE) * SUBLANE
    small_2d = jnp.concatenate(small_rows, axis=0)
    small_2d = jnp.pad(small_2d, ((0, N_DEV * piece_rows - small_2d.shape[0]), (0, 0)))
    small_pieces = small_2d.reshape(N_CHIP, 2, piece_rows, d)
    p_in0 = jnp.pad(g_w_in0.reshape(2, r_in0, N_CHIP, c_in0).transpose(2, 0, 1, 3),
                    ((0, 0), (0, 0), (0, 0), (0, d - c_in0)))
    p_in1 = g_w_in1.reshape(2, 2, r_in1, N_CHIP, c_in1).transpose(3, 0, 2, 1, 4).reshape(N_CHIP, 2, r_in1, d)
    pieces = jnp.concatenate([p_in0, g_w_out0.reshape(N_CHIP, 2, r_out0, d), p_in1,
                              g_w_out1.reshape(N_CHIP, 2, r_out1, d), small_pieces], axis=2)
    theirs = _sibling_push(pieces, True, "push_sibling_halves")
    partial = _pair_sum(jnp.reshape(ic, (1,)).astype(jnp.int32), pieces, theirs, "sum_chip")
    slots = _chip_all_to_all(partial, "exchange_grads")
    reduced = _sum_slots(slots, "sum_grads")
    mine_big = reduced[:big_rows]
    other_big = _sibling_push(mine_big[None], False, "swap_halves")[0]
    both = jnp.stack([jnp.where(ic == 0, mine_big, other_big), jnp.where(ic == 0, other_big, mine_big)])
    g_big = [both[:, :r_in0, :c_in0].reshape(d, c_in0),
             both[:, o_out0:o_in1].reshape(2 * r_out0, d),
             both[:, o_in1:o_out1].reshape(2, r_in1, 2, c_in1).transpose(0, 2, 1, 3).reshape(d, c_in1),
             both[:, o_out1:].reshape(2 * r_out1, d)]
    last = _all_gather8(jnp.concatenate([reduced[big_rows:], dmod_rows], axis=0), "gather_small_grads", pltpu.VMEM)
    last = last.reshape(N_DEV, piece_rows + dmod_rows.shape[0], d)
    small_all = last[:, :piece_rows].reshape(N_DEV * piece_rows, d)
    dmod_all = last[:, piece_rows:piece_rows + bl * 6].reshape(N_DEV * bl, 6 * d)
    dmod_chip = lax.dynamic_slice_in_dim(dmod_all.reshape(N_DEV * bl, 2, 3 * d), chip * ncol, ncol, axis=2)
    g_ada_w, g_ada_b = _ada_bwd(c_all, dmod_chip.transpose(1, 0, 2), dmod_all, "ada_bwd")
    g_ada_b = g_ada_b.reshape(2, 3 * d)
    g_small, off = [], 0
    for cnt in small_counts:
        g_small.append(small_all[off:off + cnt])
        off += cnt
    g_w_a, g_w_x = g_small[0].reshape(lru_w_a.shape[1:]), g_small[1].reshape(lru_w_x.shape[1:])
    g_norm_g = jnp.concatenate(g_small[2:4], axis=0)
    g_fin, g_cw_r, g_cb_r, g_ba_r, g_bx_r, g_lam_r = g_small[4:10]
    tail = g_small[10][0]
    g_rel_bias = tail[:n_relb].reshape(REL_BUCKETS, A_Q_HEADS)
    g_sinks, g_b_f = tail[n_relb:n_relb + A_Q_HEADS], tail[n_relb + A_Q_HEADS:n_relb + A_Q_HEADS + B_HEADS]
    cw4 = lw // N_CHIP

    def my_cols(a):
        return lax.dynamic_slice_in_dim(a, chip * cw4, cw4, axis=1)

    grads = {
        "rel_bias": g_rel_bias, "norm_g": g_norm_g, "ada_w": g_ada_w, "ada_b": g_ada_b,
        "attn_w_in": g_big[0][None], "attn_sinks": g_sinks[None], "attn_b_f": g_b_f[None],
        "attn_w_out": g_big[1][None], "lru_w_in": g_big[2][None], "lru_conv_w": my_cols(g_cw_r)[None],
        "lru_conv_b": my_cols(g_cb_r), "lru_w_a": g_w_a[None], "lru_b_a": my_cols(g_ba_r),
        "lru_w_x": g_w_x[None], "lru_b_x": my_cols(g_bx_r), "lru_lambda": my_cols(g_lam_r),
        "lru_w_out": g_big[3][None], "final_g": g_fin.reshape(d),
    }
    weights = dict(rel_bias=rel_bias, norm_g=norm_g, ada_w=ada_w, ada_b=ada_b, attn_w_in=attn_w_in,
                   attn_sinks=attn_sinks, attn_b_f=attn_b_f, attn_w_out=attn_w_out, lru_w_in=lru_w_in,
                   lru_conv_w=lru_conv_w, lru_conv_b=lru_conv_b, lru_w_a=lru_w_a, lru_b_a=lru_b_a,
                   lru_w_x=lru_w_x, lru_b_x=lru_b_x, lru_lambda=lru_lambda, lru_w_out=lru_w_out, final_g=final_g)
    moms = dict(rel_bias=(m_rel_bias, v_rel_bias), norm_g=(m_norm_g, v_norm_g), ada_w=(m_ada_w, v_ada_w),
                ada_b=(m_ada_b, v_ada_b), attn_w_in=(m_attn_w_in, v_attn_w_in),
                attn_sinks=(m_attn_sinks, v_attn_sinks), attn_b_f=(m_attn_b_f, v_attn_b_f),
                attn_w_out=(m_attn_w_out, v_attn_w_out), lru_w_in=(m_lru_w_in, v_lru_w_in),
                lru_conv_w=(m_lru_conv_w, v_lru_conv_w), lru_conv_b=(m_lru_conv_b, v_lru_conv_b),
                lru_w_a=(m_lru_w_a, v_lru_w_a), lru_b_a=(m_lru_b_a, v_lru_b_a), lru_w_x=(m_lru_w_x, v_lru_w_x),
                lru_b_x=(m_lru_b_x, v_lru_b_x), lru_lambda=(m_lru_lambda, v_lru_lambda),
                lru_w_out=(m_lru_w_out, v_lru_w_out), final_g=(m_final_g, v_final_g))
    names = list(weights)
    big_names = [n for n in names if weights[n].size >= 65536]
    small_names = [n for n in names if weights[n].size < 65536]
    delta, new_m, new_v = {}, {}, {}
    for n in big_names:
        delta[n], new_m[n], new_v[n] = _adamw(weights[n], grads[n].reshape(weights[n].shape),
                                              moms[n][0], moms[n][1], "adamw_" + n)
    cat = lambda arrs: jnp.concatenate([a.reshape(-1) for a in arrs])
    sd, sm, sv = _adamw(cat([weights[n] for n in small_names]), cat([grads[n] for n in small_names]),
                        cat([moms[n][0] for n in small_names]), cat([moms[n][1] for n in small_names]),
                        "adamw_small")
    off = 0
    for n in small_names:
        sz = weights[n].size
        shp = weights[n].shape
        delta[n], new_m[n], new_v[n] = (sd[off:off + sz].reshape(shp), sm[off:off + sz].reshape(shp),
                                        sv[off:off + sz].reshape(shp))
        off += sz
    out_grads = [grads[n].reshape(weights[n].shape) for n in names]
    return (loss, dx0, *out_grads, *[delta[n] for n in names], *[new_m[n] for n in names],
            *[new_v[n] for n in names])
```

```python
import functools
import math

import numpy as np
import jax
import jax.numpy as jnp
from jax import lax
from jax.experimental import pallas as pl
from jax.experimental.pallas import tpu as pltpu

F32 = jnp.float32
BF16 = jnp.bfloat16
MESH = pl.DeviceIdType.MESH

N_DEV = 8
N_CHIP = 4
HEAD_DIM = 64
BLK = 128
A_Q_HEADS = 8
A_KV_HEADS = 2
A_GROUP = A_Q_HEADS // A_KV_HEADS
B_HEADS = 8
REL_BUCKETS = 32
REL_MAX_EXACT = 16
REL_MAX_DIST = 128
LRU_BLOCKS = 8
LRU_C = 8.0
CONV_WIDTH = 4
EPS = 1e-6
NEG = -1e30
SCALE = HEAD_DIM ** -0.5
LANE = 128
SUBLANE = 8
VMEM_LIMIT = 56 * 1024 * 1024
SCAN_CHUNK = 256
ROW_TILE = 512
FOX_BQ = 512
ADAM_LR = 0.001
ADAM_B1 = 0.9
ADAM_B2 = 0.999
ADAM_EPS = 1e-08
ADAM_WD = 0.01
ADAM_STEP = 10
HI = lax.Precision.HIGHEST


def _cp(sem=None):
    return pltpu.CompilerParams(dimension_semantics=sem, vmem_limit_bytes=VMEM_LIMIT)


def _dot(a, b):
    return jnp.dot(a, b, preferred_element_type=F32)


def _dot_nt(a, b):
    return lax.dot_general(a, b, (((1,), (1,)), ((), ())), preferred_element_type=F32)


def _dot_tn(a, b):
    return lax.dot_general(a, b, (((0,), (0,)), ((), ())), preferred_element_type=F32)


def _sigmoid(z):
    return 1.0 / (1.0 + jnp.exp(-z))


def _row_tile(rows, cap):
    if rows <= cap:
        return rows
    best = SUBLANE
    t = SUBLANE
    while t <= cap:
        if rows % t == 0:
            best = t
        t += SUBLANE
    return best


def _all_gather8(x_shard, name, space):
    m_per, n = x_shard.shape
    n_own = 8 if (space == pltpu.HBM and m_per % 128 == 0) else 1
    own_rows = m_per // n_own

    def body(x_ref, out_ref, send_sems, recv_sems, local_sems):
        x, y, c = lax.axis_index("x"), lax.axis_index("y"), lax.axis_index("c")
        me, sibling = (x, y, c), (x, y, 1 - c)
        chips = [(1 - x, y), (x, 1 - y), (1 - x, 1 - y)]

        def rows(px, py, pc):
            return out_ref.at[pl.ds((4 * px + 2 * py + pc) * m_per, m_per), :]

        def copy(k, block, to, src=None):
            return pltpu.make_async_remote_copy(
                src_ref=rows(*block) if src is None else src, dst_ref=rows(*block),
                send_sem=send_sems.at[k], recv_sem=recv_sems.at[k], device_id=to, device_id_type=MESH)

        base = (4 * x + 2 * y + c) * m_per
        mine = [pltpu.make_async_copy(x_ref.at[pl.ds(i * own_rows, own_rows), :],
                                      out_ref.at[pl.ds(base + i * own_rows, own_rows), :], local_sems.at[i])
                for i in range(n_own)]
        for cp in mine:
            cp.start()
        first = [copy(0, me, sibling, src=x_ref)]
        first += [copy(1 + j, me, (*chip, c), src=x_ref) for j, chip in enumerate(chips)]
        for cp in first:
            cp.start()
        passed = [copy(4 + j, (*chip, c), sibling) for j, chip in enumerate(chips)]
        for j, chip in enumerate(chips):
            copy(1 + j, (*chip, c), me).wait_recv()
            passed[j].start()
        copy(0, sibling, me).wait_recv()
        for j, chip in enumerate(chips):
            copy(4 + j, (*chip, 1 - c), me).wait_recv()
        for cp in first + passed:
            cp.wait_send()
        for cp in mine:
            cp.wait()

    return pl.pallas_call(
        body, name=name,
        out_shape=jax.ShapeDtypeStruct((N_DEV * m_per, n), x_shard.dtype),
        in_specs=[pl.BlockSpec(memory_space=space)],
        out_specs=pl.BlockSpec(memory_space=space),
        scratch_shapes=[pltpu.SemaphoreType.DMA((7,)), pltpu.SemaphoreType.DMA((7,)),
                        pltpu.SemaphoreType.DMA((n_own,))],
        compiler_params=pltpu.CompilerParams(vmem_limit_bytes=VMEM_LIMIT),
    )(x_shard)


def _sibling_push(blocks, pick_other, name):
    nblk = blocks.shape[0]
    m, n = blocks.shape[-2:]

    def body(x_ref, out_ref, send_sems, recv_sems):
        x, y, c = lax.axis_index("x"), lax.axis_index("y"), lax.axis_index("c")
        copies = []
        for k in range(nblk):
            src = x_ref.at[k, 1 - c] if pick_other else x_ref.at[k]
            copies.append(pltpu.make_async_remote_copy(
                src_ref=src, dst_ref=out_ref.at[k], send_sem=send_sems.at[k], recv_sem=recv_sems.at[k],
                device_id=(x, y, 1 - c), device_id_type=MESH))
        for cp in copies:
            cp.start()
        for cp in copies:
            cp.wait_recv()
        for cp in copies:
            cp.wait_send()

    hbm = pl.BlockSpec(memory_space=pltpu.HBM)
    return pl.pallas_call(
        body, name=name,
        out_shape=jax.ShapeDtypeStruct((nblk, m, n), blocks.dtype),
        in_specs=[hbm], out_specs=hbm,
        scratch_shapes=[pltpu.SemaphoreType.DMA((nblk,)), pltpu.SemaphoreType.DMA((nblk,))],
    )(blocks)


def _chip_all_to_all(parts, name):
    _, m, n = parts.shape

    def body(x_ref, out_ref, send_sems, recv_sems, local_sem):
        x, y, c = lax.axis_index("x"), lax.axis_index("y"), lax.axis_index("c")
        me = 2 * x + y
        mine = pltpu.make_async_copy(x_ref.at[me], out_ref.at[me], local_sem)
        mine.start()
        copies = []
        for k in range(1, N_CHIP):
            px, py = x ^ ((k >> 1) & 1), y ^ (k & 1)
            copies.append(pltpu.make_async_remote_copy(
                src_ref=x_ref.at[2 * px + py], dst_ref=out_ref.at[me],
                send_sem=send_sems.at[k - 1], recv_sem=recv_sems.at[k - 1],
                device_id=(px, py, c), device_id_type=MESH))
        for cp in copies:
            cp.start()
        for cp in copies:
            cp.wait_recv()
        for cp in copies:
            cp.wait_send()
        mine.wait()

    hbm = pl.BlockSpec(memory_space=pltpu.HBM)
    return pl.pallas_call(
        body, name=name,
        out_shape=jax.ShapeDtypeStruct(parts.shape, parts.dtype),
        in_specs=[hbm], out_specs=hbm,
        scratch_shapes=[pltpu.SemaphoreType.DMA((N_CHIP - 1,)), pltpu.SemaphoreType.DMA((N_CHIP - 1,)),
                        pltpu.SemaphoreType.DMA],
    )(parts)


def _pair_sum(core, pieces, theirs, name):
    nblk, _, m, n = pieces.shape
    tr = _row_tile(m, 536)

    def body(c_ref, p_ref, t_ref, o_ref):
        o_ref[...] = (p_ref[...] + t_ref[...]).astype(BF16)

    return pl.pallas_call(
        body, name=name,
        grid_spec=pltpu.PrefetchScalarGridSpec(
            num_scalar_prefetch=1, grid=(nblk, m // tr),
            in_specs=[pl.BlockSpec((None, None, tr, n), lambda k, i, c_ref: (k, c_ref[0], i, 0)),
                      pl.BlockSpec((None, tr, n), lambda k, i, c_ref: (k, i, 0))],
            out_specs=pl.BlockSpec((None, tr, n), lambda k, i, c_ref: (k, i, 0))),
        out_shape=jax.ShapeDtypeStruct((nblk, m, n), BF16),
        compiler_params=_cp(("parallel", "parallel")),
    )(core, pieces, theirs)


def _sum_slots(slots, name):
    k, m, n = slots.shape
    tr = _row_tile(m, 536)

    def body(s_ref, o_ref):
        acc = s_ref[0].astype(F32)
        for j in range(1, k):
            acc = acc + s_ref[j].astype(F32)
        o_ref[...] = acc

    return pl.pallas_call(
        body, name=name, grid=(m // tr,),
        out_shape=jax.ShapeDtypeStruct((m, n), F32),
        in_specs=[pl.BlockSpec((k, tr, n), lambda i: (0, i, 0))],
        out_specs=pl.BlockSpec((tr, n), lambda i: (i, 0)),
        compiler_params=_cp(("parallel",)),
    )(slots)


def _ada_fwd(c_all, w, b, name):
    r, _ = c_all.shape
    n = w.shape[1]

    def body(c_ref, w_ref, b_ref, o_ref):
        cv = c_ref[...]
        act = cv * _sigmoid(cv)
        o_ref[...] = jnp.dot(act, w_ref[...], precision=HI, preferred_element_type=F32) + b_ref[...]

    return pl.pallas_call(body, name=name, out_shape=jax.ShapeDtypeStruct((r, n), F32),
                          compiler_params=_cp())(c_all, w, b)


def _ada_bwd(c_all, dmod_chip, dmod_all, name):
    r, d = c_all.shape
    nl, _, n = dmod_chip.shape

    def body(c_ref, dm_ref, da_ref, gw_ref, gb_ref):
        cv = c_ref[...]
        act = cv * _sigmoid(cv)
        for l in range(nl):
            gw_ref[l] = lax.dot_general(act, dm_ref[l], (((0,), (0,)), ((), ())), precision=HI,
                                        preferred_element_type=F32)
        gb_ref[...] = jnp.sum(da_ref[...], axis=0, keepdims=True)

    return pl.pallas_call(
        body, name=name,
        out_shape=(jax.ShapeDtypeStruct((nl, d, n), F32), jax.ShapeDtypeStruct((1, dmod_all.shape[1]), F32)),
        compiler_params=_cp())(c_all, dmod_chip, dmod_all)


def _norm_proj(x, g, scale, shift, w, f32_cols, name):
    b, s, d = x.shape
    n = w.shape[1]
    tm = min(s, ROW_TILE)

    def body(x_ref, g_ref, sc_ref, sh_ref, w_ref, proj_ref, h_ref, *aux_ref):
        xv = x_ref[...]
        rstd = lax.rsqrt(jnp.mean(xv * xv, axis=-1, keepdims=True) + EPS)
        h = (xv * rstd) * g_ref[...] * (1.0 + sc_ref[...]) + sh_ref[...]
        hb = h.astype(BF16)
        h_ref[...] = hb
        proj = _dot(hb, w_ref[...])
        proj_ref[...] = proj.astype(BF16)
        if f32_cols:
            aux_ref[0][...] = proj[:, n - f32_cols:]

    row = lambda i, j: (i, j, 0)
    out_shape = [jax.ShapeDtypeStruct((b, s, n), BF16), jax.ShapeDtypeStruct((b, s, d), BF16)]
    out_specs = [pl.BlockSpec((None, tm, n), row), pl.BlockSpec((None, tm, d), row)]
    if f32_cols:
        out_shape.append(jax.ShapeDtypeStruct((b, s, f32_cols), F32))
        out_specs.append(pl.BlockSpec((None, tm, f32_cols), row))
    return pl.pallas_call(
        body, name=name, grid=(b, s // tm),
        out_shape=tuple(out_shape),
        in_specs=[pl.BlockSpec((None, tm, d), row),
                  pl.BlockSpec((1, d), lambda i, j: (0, 0)),
                  pl.BlockSpec((None, 1, d), lambda i, j: (i, 0, 0)),
                  pl.BlockSpec((None, 1, d), lambda i, j: (i, 0, 0)),
                  pl.BlockSpec((d, n), lambda i, j: (0, 0))],
        out_specs=tuple(out_specs),
        compiler_params=_cp(("parallel", "parallel")),
    )(x, g, scale, shift, w)


def _cat_refs(refs):
    vals = [r[...] for r in refs]
    return vals[0] if len(vals) == 1 else jnp.concatenate(vals, axis=-1)


def _gate_outproj(mix_parts, proj, gate_blk, w_out, x, gmod, name):
    b, s, _ = x.shape
    wd, d = w_out.shape
    tm = min(s, ROW_TILE)
    npart = len(mix_parts)

    def body(*refs):
        mix_refs = refs[:npart]
        gate_ref, w_ref, x_ref, gm_ref, xo_ref, o_ref = refs[npart:]
        gt = gate_ref[...].astype(F32)
        y = (_cat_refs(mix_refs) * (gt * _sigmoid(gt))).astype(BF16)
        o = _dot(y, w_ref[...])
        o_ref[...] = o.astype(BF16)
        xo_ref[...] = x_ref[...] + gm_ref[...] * o

    return pl.pallas_call(
        body, name=name, grid=(b, s // tm),
        out_shape=(jax.ShapeDtypeStruct((b, s, d), F32), jax.ShapeDtypeStruct((b, s, d), BF16)),
        in_specs=[pl.BlockSpec((None, tm, p.shape[2]), lambda i, j: (i, j, 0)) for p in mix_parts] + [
                  pl.BlockSpec((None, tm, wd), lambda i, j: (i, j, gate_blk)),
                  pl.BlockSpec((wd, d), lambda i, j: (0, 0)),
                  pl.BlockSpec((None, tm, d), lambda i, j: (i, j, 0)),
                  pl.BlockSpec((None, 1, d), lambda i, j: (i, 0, 0))],
        out_specs=(pl.BlockSpec((None, tm, d), lambda i, j: (i, j, 0)),
                   pl.BlockSpec((None, tm, d), lambda i, j: (i, j, 0))),
        compiler_params=_cp(("parallel", "parallel")),
    )(*mix_parts, proj, w_out, x, gmod)


def _final_loss(x, g, target, name):
    b, s, d = x.shape
    tm = min(s, ROW_TILE)

    def body(x_ref, g_ref, t_ref, loss_ref, dx_ref, dg_ref):
        first = jnp.logical_and(pl.program_id(0) == 0, pl.program_id(1) == 0)

        @pl.when(first)
        def _():
            loss_ref[...] = jnp.zeros_like(loss_ref)
            dg_ref[...] = jnp.zeros_like(dg_ref)

        xv = x_ref[...]
        gv = g_ref[...]
        rstd = lax.rsqrt(jnp.mean(xv * xv, axis=-1, keepdims=True) + EPS)
        xhat = xv * rstd
        err = xhat * gv - t_ref[...]
        row = jnp.mean(err * err, axis=-1, keepdims=True)
        loss_ref[...] += 0.5 * jnp.sum(row, axis=0, keepdims=True)
        dy = err * (1.0 / d)
        dg_ref[...] += jnp.sum(dy * xhat, axis=0, keepdims=True)
        dxh = dy * gv
        dx_ref[...] = rstd * (dxh - xhat * jnp.mean(dxh * xhat, axis=-1, keepdims=True))

    return pl.pallas_call(
        body, name=name, grid=(b, s // tm),
        out_shape=(jax.ShapeDtypeStruct((1, LANE), F32), jax.ShapeDtypeStruct((b, s, d), F32),
                   jax.ShapeDtypeStruct((1, d), F32)),
        in_specs=[pl.BlockSpec((None, tm, d), lambda i, j: (i, j, 0)),
                  pl.BlockSpec((1, d), lambda i, j: (0, 0)),
                  pl.BlockSpec((None, tm, d), lambda i, j: (i, j, 0))],
        out_specs=(pl.BlockSpec((1, LANE), lambda i, j: (0, 0)),
                   pl.BlockSpec((None, tm, d), lambda i, j: (i, j, 0)),
                   pl.BlockSpec((1, d), lambda i, j: (0, 0))),
        compiler_params=_cp(("arbitrary", "arbitrary")),
    )(x, g, target)


def _bwd_out(dxo, gmod, o, mix_parts, proj, gate_blk, w_out_t, dmix_dtype, name):
    b, s, d = dxo.shape
    wd = w_out_t.shape[1]
    tm = min(s, ROW_TILE)
    npart = len(mix_parts)

    def body(dx_ref, gm_ref, o_ref, *refs):
        mix_refs = refs[:npart]
        gate_ref, wt_ref, dmix_ref, dgate_ref, do_ref, y_ref, dgm_ref = refs[npart:]

        @pl.when(pl.program_id(1) == 0)
        def _():
            dgm_ref[...] = jnp.zeros_like(dgm_ref)

        dx = dx_ref[...]
        dgm_ref[...] += jnp.sum(dx * o_ref[...].astype(F32), axis=0, keepdims=True)
        dob = (gm_ref[...] * dx).astype(BF16)
        do_ref[...] = dob
        dy = _dot(dob, wt_ref[...])
        gt = gate_ref[...].astype(F32)
        sg = _sigmoid(gt)
        silu = gt * sg
        mx = _cat_refs(mix_refs)
        y_ref[...] = (mx * silu).astype(BF16)
        dmix_ref[...] = (dy * silu).astype(dmix_dtype)
        dgate_ref[...] = (dy * mx * (sg * (1.0 + gt * (1.0 - sg)))).astype(BF16)

    row = lambda i, j: (i, j, 0)
    return pl.pallas_call(
        body, name=name, grid=(b, s // tm),
        out_shape=(jax.ShapeDtypeStruct((b, s, wd), dmix_dtype), jax.ShapeDtypeStruct((b, s, wd), BF16),
                   jax.ShapeDtypeStruct((b, s, d), BF16), jax.ShapeDtypeStruct((b, s, wd), BF16),
                   jax.ShapeDtypeStruct((b, 1, d), F32)),
        in_specs=[pl.BlockSpec((None, tm, d), row),
                  pl.BlockSpec((None, 1, d), lambda i, j: (i, 0, 0)),
                  pl.BlockSpec((None, tm, d), row)] + [
                  pl.BlockSpec((None, tm, p.shape[2]), row) for p in mix_parts] + [
                  pl.BlockSpec((None, tm, wd), lambda i, j: (i, j, gate_blk)),
                  pl.BlockSpec((d, wd), lambda i, j: (0, 0))],
        out_specs=(pl.BlockSpec((None, tm, wd), row), pl.BlockSpec((None, tm, wd), row),
                   pl.BlockSpec((None, tm, d), row), pl.BlockSpec((None, tm, wd), row),
                   pl.BlockSpec((None, 1, d), lambda i, j: (i, 0, 0))),
        compiler_params=_cp(("parallel", "arbitrary")),
    )(dxo, gmod, o, *mix_parts, proj, w_out_t)


def _bwd_in(dproj_parts, w_in_t, x, g, scale, dxo, name):
    b, s, d = x.shape
    n = w_in_t.shape[0]
    tm = min(s, ROW_TILE)
    npart = len(dproj_parts)

    def body(*refs):
        dp_refs = refs[:npart]
        wt_ref, x_ref, g_ref, sc_ref, dxo_ref, dx_ref, dsh_ref, dsc_ref, dg_ref = refs[npart:]

        @pl.when(jnp.logical_and(pl.program_id(0) == 0, pl.program_id(1) == 0))
        def _():
            dg_ref[...] = jnp.zeros_like(dg_ref)

        @pl.when(pl.program_id(1) == 0)
        def _():
            dsh_ref[...] = jnp.zeros_like(dsh_ref)
            dsc_ref[...] = jnp.zeros_like(dsc_ref)

        dh = _dot(_cat_refs(dp_refs), wt_ref[...])
        xv = x_ref[...]
        gv = g_ref[...]
        one_sc = 1.0 + sc_ref[...]
        rstd = lax.rsqrt(jnp.mean(xv * xv, axis=-1, keepdims=True) + EPS)
        xhat = xv * rstd
        dsh_ref[...] += jnp.sum(dh, axis=0, keepdims=True)
        dsc_ref[...] += jnp.sum(dh * (xhat * gv), axis=0, keepdims=True)
        dhs = dh * one_sc
        dg_ref[...] += jnp.sum(dhs * xhat, axis=0, keepdims=True)
        dxh = dhs * gv
        dx_ref[...] = dxo_ref[...] + rstd * (dxh - xhat * jnp.mean(dxh * xhat, axis=-1, keepdims=True))

    row = lambda i, j: (i, j, 0)
    per_b = lambda i, j: (i, 0, 0)
    return pl.pallas_call(
        body, name=name, grid=(b, s // tm),
        out_shape=(jax.ShapeDtypeStruct((b, s, d), F32), jax.ShapeDtypeStruct((b, 1, d), F32),
                   jax.ShapeDtypeStruct((b, 1, d), F32), jax.ShapeDtypeStruct((1, d), F32)),
        in_specs=[pl.BlockSpec((None, tm, p.shape[2]), row) for p in dproj_parts] + [
                  pl.BlockSpec((n, d), lambda i, j: (0, 0)),
                  pl.BlockSpec((None, tm, d), row),
                  pl.BlockSpec((1, d), lambda i, j: (0, 0)),
                  pl.BlockSpec((None, 1, d), per_b),
                  pl.BlockSpec((None, tm, d), row)],
        out_specs=(pl.BlockSpec((None, tm, d), row), pl.BlockSpec((None, 1, d), per_b),
                   pl.BlockSpec((None, 1, d), per_b), pl.BlockSpec((1, d), lambda i, j: (0, 0))),
        compiler_params=_cp(("arbitrary", "arbitrary")),
    )(*dproj_parts, w_in_t, x, g, scale, dxo)


def _matmul_tn(a, b_parts, name):
    bsz, s, m = a.shape
    n = sum(p.shape[2] for p in b_parts)
    tk = next(c for c in (512, 256, 128) if s % c == 0)
    npart = len(b_parts)

    def body(a_ref, *refs):
        b_refs, o_ref = refs[:npart], refs[npart]

        @pl.when(jnp.logical_and(pl.program_id(0) == 0, pl.program_id(1) == 0))
        def _():
            o_ref[...] = jnp.zeros_like(o_ref)

        o_ref[...] += _dot_tn(a_ref[...], _cat_refs(b_refs))

    row = lambda i, k: (i, k, 0)
    return pl.pallas_call(
        body, name=name, grid=(bsz, s // tk),
        out_shape=jax.ShapeDtypeStruct((m, n), F32),
        in_specs=[pl.BlockSpec((None, tk, m), row)] + [pl.BlockSpec((None, tk, p.shape[2]), row) for p in b_parts],
        out_specs=pl.BlockSpec((m, n), lambda i, k: (0, 0)),
        compiler_params=_cp(("arbitrary", "arbitrary")),
    )(a, *b_parts)


def _rel_buckets():
    qi = np.arange(BLK)[:, None]
    kj = np.arange(2 * BLK)[None, :]
    rel = qi - kj + BLK
    n = np.maximum(rel, 0)
    nf = np.maximum(n, 1).astype(np.float32)
    large = REL_MAX_EXACT + (np.log(nf / REL_MAX_EXACT) / math.log(REL_MAX_DIST / REL_MAX_EXACT)
                             * (REL_BUCKETS - REL_MAX_EXACT)).astype(np.int32)
    large = np.minimum(large, REL_BUCKETS - 1)
    bucket = np.where(n < REL_MAX_EXACT, n, large).astype(np.int32)
    valid = ((rel >= 0) & (rel < BLK)).astype(np.int32)
    return bucket, valid


def _swa_bias(rel_bias_t, bucket, valid, name):
    nh = rel_bias_t.shape[0]

    def body(rb_ref, bk_ref, vl_ref, o_ref):
        h = pl.program_id(0)
        bk = bk_ref[...]
        acc = jnp.zeros(bk.shape, F32)
        for i in range(REL_BUCKETS):
            acc = jnp.where(bk == i, rb_ref[h, i], acc)
        o_ref[...] = jnp.where(vl_ref[...] > 0, acc, NEG)

    return pl.pallas_call(
        body, name=name, grid=(nh,),
        out_shape=jax.ShapeDtypeStruct((nh, BLK, 2 * BLK), F32),
        in_specs=[pl.BlockSpec(memory_space=pltpu.SMEM),
                  pl.BlockSpec((BLK, 2 * BLK), lambda h: (0, 0)),
                  pl.BlockSpec((BLK, 2 * BLK), lambda h: (0, 0))],
        out_specs=pl.BlockSpec((None, BLK, 2 * BLK), lambda h: (h, 0, 0)),
        compiler_params=_cp(("arbitrary",)),
    )(rel_bias_t, bucket, valid)


def _swa_scores(n, q, kw, bias_ref):
    sc = _dot_nt(q, kw) * SCALE + bias_ref[...]
    second = lax.broadcasted_iota(jnp.int32, sc.shape, 1) >= BLK
    return jnp.where(jnp.logical_or(n > 0, second), sc, NEG)


def _pad_front(dst_ref, src_ref):
    dst_ref[0:BLK, :] = jnp.zeros((BLK, dst_ref.shape[1]), dst_ref.dtype)
    dst_ref[BLK:, :] = src_ref[...]


def _swa_fwd(q, k, v, bias, sinks, name):
    b, hkv, nb, rows, hd = q.shape
    s = nb * BLK

    def body(q_ref, k_ref, v_ref, bias_ref, sink_ref, o_ref, l_ref, kpad_ref, vpad_ref):
        _pad_front(kpad_ref, k_ref)
        _pad_front(vpad_ref, v_ref)
        sink = sink_ref[...]

        def step(n, carry):
            w0 = pl.multiple_of(n * BLK, BLK)
            sc = _swa_scores(n, q_ref[n], kpad_ref[pl.ds(w0, 2 * BLK), :], bias_ref)
            m = jnp.maximum(jnp.max(sc, axis=1, keepdims=True), sink)
            e = jnp.exp(sc - m)
            den = jnp.sum(e, axis=1, keepdims=True) + jnp.exp(sink - m)
            o_ref[n] = _dot((e * (1.0 / den)).astype(BF16), vpad_ref[pl.ds(w0, 2 * BLK), :])
            l_ref[n] = m + jnp.log(den)
            return carry

        lax.fori_loop(0, nb, step, 0)

    qspec = pl.BlockSpec((None, None, nb, rows, hd), lambda i, kv: (i, kv, 0, 0, 0))
    kspec = pl.BlockSpec((None, None, s, hd), lambda i, kv: (i, kv, 0, 0))
    return pl.pallas_call(
        body, name=name, grid=(b, hkv),
        out_shape=(jax.ShapeDtypeStruct((b, hkv, nb, rows, hd), F32), jax.ShapeDtypeStruct((b, hkv, nb, rows, 1), F32)),
        in_specs=[qspec, kspec, kspec,
                  pl.BlockSpec((None, rows, 2 * BLK), lambda i, kv: (kv, 0, 0)),
                  pl.BlockSpec((None, rows, 1), lambda i, kv: (kv, 0, 0))],
        out_specs=(qspec, pl.BlockSpec((None, None, nb, rows, 1), lambda i, kv: (i, kv, 0, 0, 0))),
        scratch_shapes=[pltpu.VMEM((s + BLK, hd), BF16), pltpu.VMEM((s + BLK, hd), BF16)],
        compiler_params=_cp(("parallel", "parallel")),
    )(q, k, v, bias, sinks)


def _swa_bwd(q, k, v, bias, sinks, do, lse, name):
    b, hkv, nb, rows, hd = q.shape
    s = nb * BLK

    def body(q_ref, k_ref, v_ref, bias_ref, sink_ref, do_ref, l_ref,
             dq_ref, dk_ref, dv_ref, db_ref, dsk_ref, kpad_ref, vpad_ref, dkpad_ref, dvpad_ref):
        _pad_front(kpad_ref, k_ref)
        _pad_front(vpad_ref, v_ref)
        dkpad_ref[...] = jnp.zeros_like(dkpad_ref)
        dvpad_ref[...] = jnp.zeros_like(dvpad_ref)
        db_ref[...] = jnp.zeros_like(db_ref)
        sink = sink_ref[...]

        def step(n, dsink):
            w0 = pl.multiple_of(n * BLK, BLK)
            win = pl.ds(w0, 2 * BLK)
            qn = q_ref[n]
            kw = kpad_ref[win, :]
            ln = l_ref[n]
            p = jnp.exp(_swa_scores(n, qn, kw, bias_ref) - ln)
            dob = do_ref[n]
            dp = _dot_nt(dob, vpad_ref[win, :])
            delta = jnp.sum(p * dp, axis=1, keepdims=True)
            ds = p * (dp - delta)
            db_ref[...] += ds
            dsb = ds.astype(BF16)
            dq_ref[n] = (_dot(dsb, kw) * SCALE).astype(BF16)
            dkpad_ref[win, :] += _dot_tn(dsb, qn)
            dvpad_ref[win, :] += _dot_tn(p.astype(BF16), dob)
            return dsink - jnp.exp(sink - ln) * delta

        dsink = lax.fori_loop(0, nb, step, jnp.zeros((rows, 1), F32))
        for g in range(A_GROUP):
            tot = jnp.sum(dsink[g * BLK:(g + 1) * BLK, :], axis=0, keepdims=True)
            dsk_ref[g] = jnp.broadcast_to(tot, (1, LANE))
        dk_ref[...] = (dkpad_ref[BLK:, :] * SCALE).astype(BF16)
        dv_ref[...] = dvpad_ref[BLK:, :].astype(BF16)

    qspec = pl.BlockSpec((None, None, nb, rows, hd), lambda i, kv: (i, kv, 0, 0, 0))
    kspec = pl.BlockSpec((None, None, s, hd), lambda i, kv: (i, kv, 0, 0))
    return pl.pallas_call(
        body, name=name, grid=(b, hkv),
        out_shape=(jax.ShapeDtypeStruct((b, hkv, nb, rows, hd), BF16), jax.ShapeDtypeStruct((b, hkv, s, hd), BF16),
                   jax.ShapeDtypeStruct((b, hkv, s, hd), BF16), jax.ShapeDtypeStruct((b, hkv, rows, 2 * BLK), F32),
                   jax.ShapeDtypeStruct((b, hkv, A_GROUP, 1, LANE), F32)),
        in_specs=[qspec, kspec, kspec,
                  pl.BlockSpec((None, rows, 2 * BLK), lambda i, kv: (kv, 0, 0)),
                  pl.BlockSpec((None, rows, 1), lambda i, kv: (kv, 0, 0)),
                  qspec,
                  pl.BlockSpec((None, None, nb, rows, 1), lambda i, kv: (i, kv, 0, 0, 0))],
        out_specs=(qspec, kspec, kspec,
                   pl.BlockSpec((None, None, rows, 2 * BLK), lambda i, kv: (i, kv, 0, 0)),
                   pl.BlockSpec((None, None, A_GROUP, 1, LANE), lambda i, kv: (i, kv, 0, 0, 0))),
        scratch_shapes=[pltpu.VMEM((s + BLK, hd), BF16), pltpu.VMEM((s + BLK, hd), BF16),
                        pltpu.VMEM((s + BLK, hd), F32), pltpu.VMEM((s + BLK, hd), F32)],
        compiler_params=_cp(("parallel", "parallel")),
    )(q, k, v, bias, sinks, do, lse)


def _swa_small_grads(db, dsk, bucket, name):
    b, nh = db.shape[0], db.shape[1]

    def body(db_ref, dsk_ref, bk_ref, gb_ref, gs_ref):
        acc = db_ref[0]
        sk = dsk_ref[0]
        for i in range(1, b):
            acc = acc + db_ref[i]
            sk = sk + dsk_ref[i]
        gs_ref[...] = sk
        bk = bk_ref[...]
        for i in range(REL_BUCKETS):
            part = jnp.sum(jnp.where(bk == i, acc, 0.0), axis=1, keepdims=True)
            tot = jnp.sum(part, axis=0, keepdims=True)
            gb_ref[i:i + 1, :] = jnp.broadcast_to(tot, (1, LANE))

    return pl.pallas_call(
        body, name=name, grid=(nh,),
        out_shape=(jax.ShapeDtypeStruct((nh, REL_BUCKETS, LANE), F32), jax.ShapeDtypeStruct((nh, 1, LANE), F32)),
        in_specs=[pl.BlockSpec((b, None, BLK, 2 * BLK), lambda h: (0, h, 0, 0)),
                  pl.BlockSpec((b, None, 1, LANE), lambda h: (0, h, 0, 0)),
                  pl.BlockSpec((BLK, 2 * BLK), lambda h: (0, 0))],
        out_specs=(pl.BlockSpec((None, REL_BUCKETS, LANE), lambda h: (h, 0, 0)),
                   pl.BlockSpec((None, 1, LANE), lambda h: (h, 0, 0))),
        compiler_params=_cp(("parallel",)),
    )(db, dsk, bucket)


def _log_sigmoid(z):
    return jnp.minimum(z, 0.0) - jnp.log(1.0 + jnp.exp(-jnp.abs(z)))


def _fox_decay(z, bf, name):
    b, s, w = z.shape
    nb = s // BLK

    def body(z_ref, bf_ref, f_ref):
        r = lax.broadcasted_iota(jnp.int32, (BLK, BLK), 0)
        c = lax.broadcasted_iota(jnp.int32, (BLK, BLK), 1)
        tri = (c <= r).astype(F32)

        def step(n, carry):
            r0 = pl.multiple_of(n * BLK, BLK)
            lf = _log_sigmoid(z_ref[pl.ds(r0, BLK), :] + bf_ref[...])
            f_ref[pl.ds(r0, BLK), :] = jnp.dot(tri, lf, precision=HI, preferred_element_type=F32) + carry
            return carry + jnp.sum(lf, axis=0, keepdims=True)

        lax.fori_loop(0, nb, step, jnp.zeros((1, w), F32))

    spec = pl.BlockSpec((None, s, w), lambda i: (i, 0, 0))
    return pl.pallas_call(
        body, name=name, grid=(b,), out_shape=jax.ShapeDtypeStruct((b, s, w), F32),
        in_specs=[spec, pl.BlockSpec((1, w), lambda i: (0, 0))], out_specs=spec,
        compiler_params=_cp(("parallel",)),
    )(z, bf)


def _fox_dgate(df, z, bf, nheads, name):
    b, s, w = z.shape
    nb = s // BLK

    def body(df_ref, z_ref, bf_ref, dz_ref, dbf_ref):
        @pl.when(pl.program_id(0) == 0)
        def _():
            dbf_ref[...] = jnp.zeros_like(dbf_ref)

        r = lax.broadcasted_iota(jnp.int32, (BLK, BLK), 0)
        c = lax.broadcasted_iota(jnp.int32, (BLK, BLK), 1)
        tri = (c >= r).astype(F32)
        lane = lax.broadcasted_iota(jnp.int32, (BLK, w), 1)

        def step(i, carry):
            tail, dbf = carry
            r0 = pl.multiple_of((nb - 1 - i) * BLK, BLK)
            dfb = df_ref[pl.ds(r0, BLK), :]
            dlf = jnp.dot(tri, dfb, precision=HI, preferred_element_type=F32) + tail
            dz = jnp.where(lane < nheads, dlf * _sigmoid(-(z_ref[pl.ds(r0, BLK), :] + bf_ref[...])), 0.0)
            dz_ref[pl.ds(r0, BLK), :] = dz
            return tail + jnp.sum(dfb, axis=0, keepdims=True), dbf + jnp.sum(dz, axis=0, keepdims=True)

        zero = jnp.zeros((1, w), F32)
        _, dbf = lax.fori_loop(0, nb, step, (zero, zero))
        dbf_ref[...] += dbf

    spec = pl.BlockSpec((None, s, w), lambda i: (i, 0, 0))
    one = pl.BlockSpec((1, w), lambda i: (0, 0))
    return pl.pallas_call(
        body, name=name, grid=(b,),
        out_shape=(jax.ShapeDtypeStruct((b, s, w), F32), jax.ShapeDtypeStruct((1, w), F32)),
        in_specs=[spec, spec, one], out_specs=(spec, one),
        compiler_params=_cp(("arbitrary",)),
    )(df, z, bf)


def _fox_segments(nb):
    per = max(1, nb // 4)
    return per, nb // per


def _head_masks(shape, axis):
    idx = lax.broadcasted_iota(jnp.int32, shape, axis)
    return idx < HEAD_DIM, idx >= HEAD_DIM


def _fox_fwd(proj, qblk, kblk, vblk, fcol, frow, name):
    b, s, _ = proj.shape
    nh = frow.shape[1]
    npair = nh // 2
    BLK = min(s, FOX_BQ)
    assert s % BLK == 0
    per, nseg = _fox_segments(s // BLK)

    def body(q_ref, k_ref, v_ref, fc_ref, fr_ref, o_ref, l_ref, qm_ref, kt_ref, vb_ref):
        lo, hi = _head_masks((s, LANE), 1)
        qv = q_ref[...].astype(F32) * SCALE
        qm_ref[0] = jnp.where(lo, qv, 0.0).astype(BF16)
        qm_ref[1] = jnp.where(hi, qv, 0.0).astype(BF16)
        kt_ref[...] = k_ref[...].astype(F32).T.astype(BF16)
        vb_ref[...] = v_ref[...].astype(BF16)
        lane_lo = lax.broadcasted_iota(jnp.int32, (BLK, LANE), 1) < HEAD_DIM
        tail = per * BLK
        causal = (lax.broadcasted_iota(jnp.int32, (BLK, tail), 1)
                  - lax.broadcasted_iota(jnp.int32, (BLK, tail), 0))
        for seg in range(nseg):
            w = (seg + 1) * tail

            def qstep(n, carry):
                r0 = pl.multiple_of(n * BLK, BLK)
                outs = []
                for hh in range(2):
                    sc = _dot(qm_ref[hh, pl.ds(r0, BLK), :], kt_ref[:, :w])
                    sc = sc + (fc_ref[hh, pl.ds(r0, BLK), :] - fr_ref[hh, :, :w])
                    masked = jnp.where(causal <= (n - seg * per) * BLK, sc[:, w - tail:], NEG)
                    sc = masked if seg == 0 else jnp.concatenate([sc[:, :w - tail], masked], axis=1)
                    m = jnp.max(sc, axis=1, keepdims=True)
                    e = jnp.exp(sc - m)
                    l = jnp.sum(e, axis=1, keepdims=True)
                    outs.append(_dot((e * (1.0 / l)).astype(BF16), vb_ref[:w, :]))
                    l_ref[hh, pl.ds(r0, BLK), :] = m + jnp.log(l)
                o_ref[pl.ds(r0, BLK), :] = jnp.where(lane_lo, outs[0], outs[1])
                return carry

            lax.fori_loop(seg * per, (seg + 1) * per, qstep, 0)

    def tok(blk):
        return pl.BlockSpec((None, s, LANE), lambda i, p: (i, 0, blk + p))

    col = pl.BlockSpec((None, 2, s, 1), lambda i, p: (i, p, 0, 0))
    rowspec = pl.BlockSpec((None, 2, 1, s), lambda i, p: (i, p, 0, 0))
    return pl.pallas_call(
        body, name=name, grid=(b, npair),
        out_shape=(jax.ShapeDtypeStruct((b, s, nh * HEAD_DIM), F32), jax.ShapeDtypeStruct((b, nh, s, 1), F32)),
        in_specs=[tok(qblk), tok(kblk), tok(vblk), col, rowspec],
        out_specs=(pl.BlockSpec((None, s, LANE), lambda i, p: (i, 0, p)), col),
        scratch_shapes=[pltpu.VMEM((2, s, LANE), BF16), pltpu.VMEM((LANE, s), BF16), pltpu.VMEM((s, LANE), BF16)],
        compiler_params=_cp(("parallel", "parallel")),
    )(proj, proj, proj, fcol, frow)


def _fox_bwd(proj, qblk, kblk, vblk, dmix, doblk, fcol, frow, frowb, lse, lserowb, name):
    b, s, _ = proj.shape
    nh = frow.shape[1]
    npair = nh // 2
    BLK = min(s, FOX_BQ)
    assert s % BLK == 0
    nb = s // BLK
    per, nseg = _fox_segments(nb)

    def body(q_ref, k_ref, v_ref, do_ref, fc_ref, fr_ref, frb_ref, l_ref, lrb_ref,
             dq_ref, dk_ref, dv_ref, dfr_ref,
             qm_ref, dom_ref, kb_ref, vb_ref, kt_ref, vt_ref, qtm_ref, dotm_ref, dka_ref, dva_ref):
        lo, hi = _head_masks((s, LANE), 1)
        qv = q_ref[...].astype(F32) * SCALE
        dov = do_ref[...]
        for hh, msk in enumerate((lo, hi)):
            qm_ref[hh] = jnp.where(msk, qv, 0.0).astype(BF16)
            dom_ref[hh] = jnp.where(msk, dov, 0.0).astype(BF16)
        kv = k_ref[...].astype(F32)
        vv = v_ref[...].astype(F32)
        kb_ref[...] = kv.astype(BF16)
        vb_ref[...] = vv.astype(BF16)
        kt_ref[...] = kv.T.astype(BF16)
        vt_ref[...] = vv.T.astype(BF16)
        rlo, rhi = _head_masks((LANE, BLK), 0)

        def tstep(n, carry):
            r0 = pl.multiple_of(n * BLK, BLK)
            qt = (q_ref[pl.ds(r0, BLK), :].astype(F32) * SCALE).T
            dt = do_ref[pl.ds(r0, BLK), :].astype(F32).T
            for hh, msk in enumerate((rlo, rhi)):
                qtm_ref[hh, n] = jnp.where(msk, qt, 0.0).astype(BF16)
                dotm_ref[hh, n] = jnp.where(msk, dt, 0.0).astype(BF16)
            return carry

        lax.fori_loop(0, nb, tstep, 0)
        dka_ref[...] = jnp.zeros_like(dka_ref)
        dva_ref[...] = jnp.zeros_like(dva_ref)
        dfr_ref[...] = jnp.zeros_like(dfr_ref)
        lane_lo = lax.broadcasted_iota(jnp.int32, (BLK, LANE), 1) < HEAD_DIM
        tail = per * BLK
        causal = (lax.broadcasted_iota(jnp.int32, (BLK, tail), 1)
                  - lax.broadcasted_iota(jnp.int32, (BLK, tail), 0))
        causal_t = (lax.broadcasted_iota(jnp.int32, (tail, BLK), 0)
                    - lax.broadcasted_iota(jnp.int32, (tail, BLK), 1))
        for seg in range(nseg):
            w = (seg + 1) * tail

            def nstep(n, carry):
                r0 = pl.multiple_of(n * BLK, BLK)
                lim = (n - seg * per) * BLK
                dqs = []
                for hh in range(2):
                    qn = qm_ref[hh, pl.ds(r0, BLK), :]
                    don = dom_ref[hh, pl.ds(r0, BLK), :]
                    sc = _dot(qn, kt_ref[:, :w]) + ((fc_ref[hh, pl.ds(r0, BLK), :] - l_ref[hh, pl.ds(r0, BLK), :])
                                                   - fr_ref[hh, :, :w])
                    masked = jnp.where(causal <= lim, sc[:, w - tail:], NEG)
                    p = jnp.exp(masked if seg == 0 else jnp.concatenate([sc[:, :w - tail], masked], axis=1))
                    dp = _dot(don, vt_ref[:, :w])
                    ds = p * (dp - jnp.sum(p * dp, axis=1, keepdims=True))
                    dqs.append(_dot(ds.astype(BF16), kb_ref[:w, :]))
                    dfr_ref[hh, :, :w] -= jnp.sum(ds, axis=0, keepdims=True)
                    sct = _dot(kb_ref[:w, :], qtm_ref[hh, n]) + ((frb_ref[hh, n] - lrb_ref[hh, n]) - fc_ref[hh, :w, :])
                    masked_t = jnp.where(causal_t <= lim, sct[w - tail:, :], NEG)
                    pt = jnp.exp(masked_t if seg == 0 else jnp.concatenate([sct[:w - tail, :], masked_t], axis=0))
                    dpt = _dot(vb_ref[:w, :], dotm_ref[hh, n])
                    dst = pt * (dpt - jnp.sum(pt * dpt, axis=0, keepdims=True))
                    dka_ref[:w, :] += _dot(dst.astype(BF16), qn)
                    dva_ref[:w, :] += _dot(pt.astype(BF16), don)
                dq_ref[pl.ds(r0, BLK), :] = (jnp.where(lane_lo, dqs[0], dqs[1]) * SCALE).astype(BF16)
                return carry

            lax.fori_loop(seg * per, (seg + 1) * per, nstep, 0)
        dk_ref[...] = dka_ref[...].astype(BF16)
        dv_ref[...] = dva_ref[...].astype(BF16)

    def tok(blk):
        return pl.BlockSpec((None, s, LANE), lambda i, p: (i, 0, blk + p))

    col = pl.BlockSpec((None, 2, s, 1), lambda i, p: (i, p, 0, 0))
    rowspec = pl.BlockSpec((None, 2, 1, s), lambda i, p: (i, p, 0, 0))
    rowbspec = pl.BlockSpec((None, 2, nb, 1, BLK), lambda i, p: (i, p, 0, 0, 0))
    outtok = pl.BlockSpec((None, s, LANE), lambda i, p: (i, 0, p))
    shp = jax.ShapeDtypeStruct((b, s, nh * HEAD_DIM), BF16)
    return pl.pallas_call(
        body, name=name, grid=(b, npair),
        out_shape=(shp, shp, shp, jax.ShapeDtypeStruct((b, nh, 1, s), F32)),
        in_specs=[tok(qblk), tok(kblk), tok(vblk),
                  pl.BlockSpec((None, s, LANE), lambda i, p: (i, 0, doblk + p)),
                  col, rowspec, rowbspec, col, rowbspec],
        out_specs=(outtok, outtok, outtok, rowspec),
        scratch_shapes=[pltpu.VMEM((2, s, LANE), BF16), pltpu.VMEM((2, s, LANE), BF16),
                        pltpu.VMEM((s, LANE), BF16), pltpu.VMEM((s, LANE), BF16),
                        pltpu.VMEM((LANE, s), BF16), pltpu.VMEM((LANE, s), BF16),
                        pltpu.VMEM((2, nb, LANE, BLK), BF16), pltpu.VMEM((2, nb, LANE, BLK), BF16),
                        pltpu.VMEM((s, LANE), F32), pltpu.VMEM((s, LANE), F32)],
        compiler_params=_cp(("parallel", "parallel")),
    )(proj, proj, proj, dmix, fcol, frow, frowb, lse, lserowb)


def _expm1(x):
    poly = x * (1.0 + x * (1.0 / 2 + x * (1.0 / 6 + x * (1.0 / 24 + x * (1.0 / 120 + x * (1.0 / 720))))))
    return jnp.where(x > -0.1, poly, jnp.exp(x) - 1.0)


def _softplus(z):
    return jnp.maximum(z, 0.0) + jnp.log(1.0 + jnp.exp(-jnp.abs(z)))


def _scan_rows(a, u, carry, row, up):
    tc, c = a.shape
    d = 1
    while d < tc:
        if d < SUBLANE:
            keep = (row >= d) if up else (row < tc - d)
            shift = d if up else tc - d
            a_sh = jnp.where(keep, pltpu.roll(a, shift, 0), 1.0)
            u_sh = jnp.where(keep, pltpu.roll(u, shift, 0), 0.0)
        elif up:
            a_sh = jnp.concatenate([jnp.ones((d, c), F32), a[:tc - d]], axis=0)
            u_sh = jnp.concatenate([jnp.zeros((d, c), F32), u[:tc - d]], axis=0)
        else:
            a_sh = jnp.concatenate([a[d:], jnp.ones((d, c), F32)], axis=0)
            u_sh = jnp.concatenate([u[d:], jnp.zeros((d, c), F32)], axis=0)
        u = a * u_sh + u
        a = a * a_sh
        d *= 2
    return u + a * carry


def _scan_up(a, u, carry, row):
    return _scan_rows(a, u, carry, row, True)


def _scan_down(bnext, g, carry, row):
    return _scan_rows(bnext, g, carry, row, False)


def _pick_row(val, row, which):
    return jnp.sum(jnp.where(row == which, val, 0.0), axis=0, keepdims=True)


def _lru_gates(xpad_ref, t0, tc, cw_ref, cb_ref, wa, ba_ref, wx, bx_ref, sp):
    xw = xpad_ref[pl.ds(t0, tc + SUBLANE), :]
    xc = cb_ref[...]
    for j in range(CONV_WIDTH):
        sh = CONV_WIDTH - 1 - j
        xs = xw if sh == 0 else pltpu.roll(xw, sh, 0)
        xc = xc + xs[SUBLANE:, :] * cw_ref[j:j + 1, :]
    xcb = xc.astype(BF16)
    r = _sigmoid(_dot(xcb, wa) + ba_ref[...])
    i = _sigmoid(_dot(xcb, wx) + bx_ref[...])
    la = -LRU_C * r * sp
    return xc, r, i, la


def _lru_specs(s, cb):
    seq = lambda bi, ni: (bi, 0, ni)
    return dict(
        seq=pl.BlockSpec((None, s, cb), seq),
        cw=pl.BlockSpec((CONV_WIDTH, cb), lambda bi, ni: (0, ni)),
        vec=pl.BlockSpec((1, cb), lambda bi, ni: (0, ni)),
        wblk=pl.BlockSpec((None, cb, cb), lambda bi, ni: (ni, 0, 0)),
    )


def _lru_fwd(proj, cw, cb_, wa, ba, wx, bx, lam, name):
    b, s, _ = proj.shape
    nblk, cb, _ = wa.shape
    tc = min(s, SCAN_CHUNK)
    nc = s // tc

    def body(x_ref, cw_ref, cb_ref, wa_ref, ba_ref, wx_ref, bx_ref, lam_ref, hs_ref, xpad_ref):
        xpad_ref[0:SUBLANE, :] = jnp.zeros((SUBLANE, cb), F32)
        xpad_ref[SUBLANE:, :] = x_ref[...].astype(F32)
        wa_b = wa_ref[...].astype(BF16)
        wx_b = wx_ref[...].astype(BF16)
        sp = _softplus(-lam_ref[...])
        row = lax.broadcasted_iota(jnp.int32, (tc, cb), 0)

        def chunk(ci, carry):
            t0 = pl.multiple_of(ci * tc, tc)
            xc, r, i, la = _lru_gates(xpad_ref, t0, tc, cw_ref, cb_ref, wa_b, ba_ref, wx_b, bx_ref, sp)
            a = jnp.exp(la)
            u = jnp.sqrt(-_expm1(2.0 * la)) * (i * xc)
            h = _scan_up(a, u, carry, row)
            hs_ref[pl.ds(t0, tc), :] = h
            return _pick_row(h, row, tc - 1)

        lax.fori_loop(0, nc, chunk, jnp.zeros((1, cb), F32))

    sp_ = _lru_specs(s, cb)
    return pl.pallas_call(
        body, name=name, grid=(b, nblk),
        out_shape=jax.ShapeDtypeStruct((b, s, nblk * cb), F32),
        in_specs=[sp_["seq"], sp_["cw"], sp_["vec"], sp_["wblk"], sp_["vec"], sp_["wblk"], sp_["vec"], sp_["vec"]],
        out_specs=sp_["seq"],
        scratch_shapes=[pltpu.VMEM((s + SUBLANE, cb), F32)],
        compiler_params=_cp(("parallel", "parallel")),
    )(proj, cw, cb_, wa, ba, wx, bx, lam)


def _lru_bwd(proj, hs, dhs, cw, cb_, wa, ba, wx, bx, lam, name):
    b, s, _ = proj.shape
    nblk, cb, _ = wa.shape
    tc = min(s, SCAN_CHUNK)
    nc = s // tc

    def body(x_ref, hs_ref, dhs_ref, cw_ref, cb_ref, wa_ref, ba_ref, wx_ref, bx_ref, lam_ref,
             dx_ref, dcw_ref, dcb_ref, dwa_ref, dba_ref, dwx_ref, dbx_ref, dlam_ref,
             xpad_ref, hpad_ref, dcpad_ref, xc_ref, r_ref, i_ref, a_ref, mult_ref):
        @pl.when(pl.program_id(1) == 0)
        def _():
            for ref in (dcw_ref, dcb_ref, dwa_ref, dba_ref, dwx_ref, dbx_ref, dlam_ref):
                ref[...] = jnp.zeros_like(ref)

        zeros8 = jnp.zeros((SUBLANE, cb), F32)
        xpad_ref[0:SUBLANE, :] = zeros8
        xpad_ref[SUBLANE:, :] = x_ref[...].astype(F32)
        hpad_ref[0:SUBLANE, :] = zeros8
        hpad_ref[SUBLANE:, :] = hs_ref[...]
        dcpad_ref[s:s + SUBLANE, :] = zeros8
        wa_b = wa_ref[...].astype(BF16)
        wx_b = wx_ref[...].astype(BF16)
        lam_v = lam_ref[...]
        sp = _softplus(-lam_v)
        dsp_dlam = -_sigmoid(-lam_v)
        row = lax.broadcasted_iota(jnp.int32, (tc, cb), 0)

        def recompute(ci, carry):
            t0 = pl.multiple_of(ci * tc, tc)
            xc, r, i, la = _lru_gates(xpad_ref, t0, tc, cw_ref, cb_ref, wa_b, ba_ref, wx_b, bx_ref, sp)
            xc_ref[pl.ds(t0, tc), :] = xc
            r_ref[pl.ds(t0, tc), :] = r
            i_ref[pl.ds(t0, tc), :] = i
            a_ref[pl.ds(t0, tc), :] = jnp.exp(la)
            mult_ref[pl.ds(t0, tc), :] = jnp.sqrt(-_expm1(2.0 * la))
            return carry

        lax.fori_loop(0, nc, recompute, 0)

        def adjoint(k, carry):
            g_next, a_first_next = carry
            t0 = pl.multiple_of((nc - 1 - k) * tc, tc)
            a = a_ref[pl.ds(t0, tc), :]
            a_next = jnp.where(row == tc - 1, a_first_next, pltpu.roll(a, tc - 1, 0))
            gg = _scan_down(a_next, dhs_ref[pl.ds(t0, tc), :], g_next, row)
            h_prev = pltpu.roll(hpad_ref[pl.ds(t0, tc + SUBLANE), :], 1, 0)[SUBLANE:, :]
            xc = xc_ref[pl.ds(t0, tc), :]
            r = r_ref[pl.ds(t0, tc), :]
            i = i_ref[pl.ds(t0, tc), :]
            mult = mult_ref[pl.ds(t0, tc), :]
            d_mult = gg * i * xc
            d_i = gg * mult * xc
            d_xc = gg * mult * i
            d_la = gg * h_prev * a - d_mult * (a * a) / mult
            d_zr = (d_la * (-LRU_C * sp)) * r * (1.0 - r)
            d_zi = d_i * i * (1.0 - i)
            dlam_ref[...] += jnp.sum(d_la * (-LRU_C * r), axis=0, keepdims=True) * dsp_dlam
            dzr_b = d_zr.astype(BF16)
            dzi_b = d_zi.astype(BF16)
            xcb = xc.astype(BF16)
            d_xc = d_xc + _dot_nt(dzr_b, wa_b) + _dot_nt(dzi_b, wx_b)
            dwa_ref[...] += _dot_tn(xcb, dzr_b)
            dwx_ref[...] += _dot_tn(xcb, dzi_b)
            dba_ref[...] += jnp.sum(d_zr, axis=0, keepdims=True)
            dbx_ref[...] += jnp.sum(d_zi, axis=0, keepdims=True)
            dcb_ref[...] += jnp.sum(d_xc, axis=0, keepdims=True)
            dcpad_ref[pl.ds(t0, tc), :] = d_xc
            return _pick_row(gg, row, 0), _pick_row(a, row, 0)

        zero = jnp.zeros((1, cb), F32)
        lax.fori_loop(0, nc, adjoint, (zero, zero))

        def conv_back(ci, carry):
            t0 = pl.multiple_of(ci * tc, tc)
            dw = dcpad_ref[pl.ds(t0, tc + SUBLANE), :]
            xw = xpad_ref[pl.ds(t0, tc + SUBLANE), :]
            d_xc = dw[:tc, :]
            dxr = jnp.zeros((tc, cb), F32)
            for j in range(CONV_WIDTH):
                sh = CONV_WIDTH - 1 - j
                dsh = dw if sh == 0 else pltpu.roll(dw, tc + SUBLANE - sh, 0)
                dxr = dxr + dsh[:tc, :] * cw_ref[j:j + 1, :]
                xs = xw if sh == 0 else pltpu.roll(xw, sh, 0)
                dcw_ref[j:j + 1, :] += jnp.sum(d_xc * xs[SUBLANE:, :], axis=0, keepdims=True)
            dx_ref[pl.ds(t0, tc), :] = dxr.astype(BF16)
            return carry

        lax.fori_loop(0, nc, conv_back, 0)

    seq = lambda ni, bi: (bi, 0, ni)
    seqspec = pl.BlockSpec((None, s, cb), seq)
    cwspec = pl.BlockSpec((CONV_WIDTH, cb), lambda ni, bi: (0, ni))
    vec = pl.BlockSpec((1, cb), lambda ni, bi: (0, ni))
    wblk = pl.BlockSpec((None, cb, cb), lambda ni, bi: (ni, 0, 0))
    w = nblk * cb
    return pl.pallas_call(
        body, name=name, grid=(nblk, b),
        out_shape=(jax.ShapeDtypeStruct((b, s, w), BF16), jax.ShapeDtypeStruct((CONV_WIDTH, w), F32),
                   jax.ShapeDtypeStruct((1, w), F32), jax.ShapeDtypeStruct((nblk, cb, cb), F32),
                   jax.ShapeDtypeStruct((1, w), F32), jax.ShapeDtypeStruct((nblk, cb, cb), F32),
                   jax.ShapeDtypeStruct((1, w), F32), jax.ShapeDtypeStruct((1, w), F32)),
        in_specs=[seqspec, seqspec, seqspec, cwspec, vec, wblk, vec, wblk, vec, vec],
        out_specs=(seqspec, cwspec, vec, wblk, vec, wblk, vec, vec),
        scratch_shapes=[pltpu.VMEM((s + SUBLANE, cb), F32)] * 3 + [pltpu.VMEM((s, cb), F32)] * 5,
        compiler_params=_cp(("parallel", "arbitrary")),
    )(proj, hs, dhs, cw, cb_, wa, ba, wx, bx, lam)


def _adamw(w, g, m, v, name):
    shape = w.shape
    total = int(np.prod(shape))
    if w.ndim >= 2 and shape[-2] % SUBLANE == 0:
        rows, cols = shape[-2:]
    else:
        cols = 1024
        rows = -(-(-(-total // cols)) // SUBLANE) * SUBLANE
    lead = -(-total // (rows * cols))
    tr = _row_tile(rows, 512)
    pad = lead * rows * cols - total

    def flat(a):
        if pad:
            a = jnp.pad(a.reshape(-1), (0, pad))
        return a.reshape(lead, rows, cols)

    c1 = 1.0 - ADAM_B1 ** ADAM_STEP
    c2 = 1.0 - ADAM_B2 ** ADAM_STEP

    def body(w_ref, g_ref, m_ref, v_ref, d_ref, nm_ref, nv_ref):
        gv = g_ref[...]
        nm = ADAM_B1 * m_ref[...] + (1.0 - ADAM_B1) * gv
        nv = ADAM_B2 * v_ref[...] + (1.0 - ADAM_B2) * (gv * gv)
        nm_ref[...] = nm
        nv_ref[...] = nv
        d_ref[...] = -ADAM_LR * ((nm / c1) / (jnp.sqrt(nv / c2) + ADAM_EPS) + ADAM_WD * w_ref[...])

    spec = pl.BlockSpec((None, tr, cols), lambda l, i: (l, i, 0))
    shp = jax.ShapeDtypeStruct((lead, rows, cols), F32)
    outs = pl.pallas_call(
        body, name=name, grid=(lead, rows // tr), out_shape=(shp, shp, shp),
        in_specs=[spec] * 4, out_specs=(spec,) * 3,
        compiler_params=_cp(("parallel", "parallel")),
    )(flat(w), flat(g), flat(m), flat(v))
    if pad:
        return tuple(o.reshape(-1)[:total].reshape(shape) for o in outs)
    return tuple(o.reshape(shape) for o in outs)


def _to_heads(t, nh):
    b, s, _ = t.shape
    return t.reshape(b, s, nh, HEAD_DIM).transpose(0, 2, 1, 3)


def _stack_heads(t):
    b, s, _ = t.shape
    t = t.reshape(b, s // BLK, BLK, A_KV_HEADS, A_GROUP, HEAD_DIM).transpose(0, 3, 1, 4, 2, 5)
    return t.reshape(b, A_KV_HEADS, s // BLK, A_GROUP * BLK, HEAD_DIM)


def _unstack_heads(t):
    b, hkv, nb, rows, hd = t.shape
    t = t.reshape(b, hkv, nb, A_GROUP, BLK, hd).transpose(0, 2, 4, 1, 3, 5)
    return t.reshape(b, nb * BLK, hkv * A_GROUP * hd)


def _from_heads(t):
    b, nh, s, hd = t.shape
    return t.transpose(0, 2, 1, 3).reshape(b, s, nh * hd)


def _pad_rows(a, mult):
    r = a.shape[0]
    p = (-r) % mult
    return jnp.pad(a, ((0, p), (0, 0))) if p else a


def kernel(x, c, rel_bias, norm_g, ada_w, ada_b, attn_w_in, attn_sinks, attn_b_f, attn_w_out, lru_w_in, lru_conv_w, lru_conv_b, lru_w_a, lru_b_a, lru_w_x, lru_b_x, lru_lambda, lru_w_out, final_g, loss_target, m_rel_bias, m_norm_g, m_ada_w, m_ada_b, m_attn_w_in, m_attn_sinks, m_attn_b_f, m_attn_w_out, m_lru_w_in, m_lru_conv_w, m_lru_conv_b, m_lru_w_a, m_lru_b_a, m_lru_w_x, m_lru_b_x, m_lru_lambda, m_lru_w_out, m_final_g, v_rel_bias, v_norm_g, v_ada_w, v_ada_b, v_attn_w_in, v_attn_sinks, v_attn_b_f, v_attn_w_out, v_lru_w_in, v_lru_conv_w, v_lru_conv_b, v_lru_w_a, v_lru_b_a, v_lru_w_x, v_lru_b_x, v_lru_lambda, v_lru_w_out, v_final_g):
    bl, s, d = x.shape
    ix, iy, ic = lax.axis_index("x"), lax.axis_index("y"), lax.axis_index("c")
    chip = 2 * ix + iy
    me = 2 * chip + ic
    nb = s // BLK
    aw = A_Q_HEADS * HEAD_DIM
    akv = A_KV_HEADS * HEAD_DIM
    bw = B_HEADS * HEAD_DIM
    mixw = aw + bw
    qkv_w = aw + 2 * akv + 3 * bw
    n_in = attn_w_in.shape[2] * N_CHIP
    lw = lru_lambda.shape[1] * N_CHIP
    n0 = mixw + qkv_w + LANE

    rows_pad = -(-bl // SUBLANE) * SUBLANE
    vec_rows = jnp.concatenate([lru_conv_w[0], lru_conv_b, lru_b_a, lru_b_x, lru_lambda], axis=0)
    first = jnp.concatenate([_pad_rows(c, SUBLANE), jnp.pad(vec_rows, ((0, 0), (0, d - lw // N_CHIP)))], axis=0)
    first = _all_gather8(first, "gather_c", pltpu.VMEM).reshape(N_DEV, rows_pad + SUBLANE, d)
    c_all = first[:, :bl].reshape(N_DEV * bl, d)
    vec_all = first[:, rows_pad:, :lw // N_CHIP].reshape(N_CHIP, 2, SUBLANE, lw // N_CHIP)[:, 0]
    vec_all = vec_all.transpose(1, 0, 2).reshape(SUBLANE, lw)
    ncol = ada_w.shape[2]
    ada_w_l = lax.dynamic_index_in_dim(ada_w, ic, 0, keepdims=False)
    ada_b_l = lax.dynamic_slice(ada_b, (ic, chip * ncol), (1, ncol))
    mod_part = _ada_fwd(c_all, ada_w_l, ada_b_l, "ada_fwd")
    mod_all = _all_gather8(_pad_rows(mod_part, SUBLANE), "gather_mod", pltpu.VMEM)
    mrows = -(-(N_DEV * bl) // SUBLANE) * SUBLANE
    mod_all = mod_all.reshape(N_CHIP, 2, mrows, ncol)[:, :, :N_DEV * bl]
    mod_all = mod_all.transpose(1, 2, 0, 3).reshape(2, N_DEV * bl, N_CHIP * ncol)
    mod = lax.dynamic_slice_in_dim(mod_all, me * bl, bl, axis=1)
    shift = [mod[l, :, 0:d].reshape(bl, 1, d) for l in range(2)]
    scale = [mod[l, :, d:2 * d].reshape(bl, 1, d) for l in range(2)]
    gmod = [mod[l, :, 2 * d:3 * d].reshape(bl, 1, d) for l in range(2)]

    c_in0 = n_in // N_CHIP
    c_in1 = 2 * lw // N_CHIP
    assert c_in0 <= d and 2 * c_in1 == d
    r_in0, r_out0, r_in1, r_out1 = d // 2, mixw // N_CHIP // 2, d // 4, lw // N_CHIP // 2
    o_out0, o_in1, o_out1 = r_in0, r_in0 + r_out0, r_in0 + r_out0 + r_in1
    big_rows = o_out1 + r_out1

    def half_of(a, rows):
        return lax.dynamic_slice_in_dim(a, ic * rows, rows, axis=0)

    h_in1 = half_of(lru_w_in[0], r_in0).astype(BF16)
    my_half = jnp.concatenate([
        jnp.pad(half_of(attn_w_in[0], r_in0).astype(BF16), ((0, 0), (0, d - c_in0))),
        half_of(attn_w_out[0], r_out0).astype(BF16),
        jnp.concatenate([h_in1[:r_in1], h_in1[r_in1:]], axis=1),
        half_of(lru_w_out[0], r_out1).astype(BF16)], axis=0)
    gat = _all_gather8(my_half, "gather_weights", pltpu.HBM).reshape(N_CHIP, 2, big_rows, d)
    w_in0 = gat[:, :, :r_in0, :c_in0].transpose(1, 2, 0, 3).reshape(d, n_in)
    w_out0 = gat[:, :, o_out0:o_in1].reshape(mixw, d)
    w_in1 = gat[:, :, o_in1:o_out1].reshape(N_CHIP, 2, r_in1, 2, c_in1)
    w_in1 = w_in1.transpose(1, 3, 2, 0, 4).reshape(d, 2 * lw)
    w_out1 = gat[:, :, o_out1:].reshape(lw, d)
    w_cat0 = jnp.concatenate([w_in0[:, qkv_w + B_HEADS:], w_in0[:, :qkv_w + B_HEADS],
                              jnp.zeros((d, n0 - n_in), BF16)], axis=1)

    proj0, h0, zf = _norm_proj(x, norm_g[0:1], scale[0], shift[0], w_cat0, LANE, "norm_proj0")
    o_a = mixw
    aq = _stack_heads(proj0[:, :, o_a:o_a + aw].astype(BF16))
    ak = _to_heads(proj0[:, :, o_a + aw:o_a + aw + akv].astype(BF16), A_KV_HEADS)
    av = _to_heads(proj0[:, :, o_a + aw + akv:o_a + aw + 2 * akv].astype(BF16), A_KV_HEADS)
    o_b = o_a + aw + 2 * akv
    fox_blks = (o_b // LANE, (o_b + bw) // LANE, (o_b + 2 * bw) // LANE)
    bucket_np, valid_np = _rel_buckets()
    bucket = jnp.asarray(bucket_np)
    bias = _swa_bias(rel_bias.T, bucket, jnp.asarray(valid_np), "swa_bias")
    bias = bias.reshape(A_KV_HEADS, A_GROUP * BLK, 2 * BLK)
    sinks = jnp.repeat(attn_sinks[0].reshape(A_KV_HEADS, A_GROUP), BLK, axis=1).reshape(A_KV_HEADS, A_GROUP * BLK, 1)
    a_out, a_lse = _swa_fwd(aq, ak, av, bias, sinks, "swa_fwd")
    bf_pad = jnp.pad(attn_b_f, ((0, 0), (0, LANE - B_HEADS)))
    fsum = _fox_decay(zf, bf_pad, "fox_decay")
    fh = fsum[:, :, :B_HEADS].transpose(0, 2, 1)
    fcol = fh.reshape(bl, B_HEADS, s, 1)
    frow = fh.reshape(bl, B_HEADS, 1, s)
    fbq = min(s, FOX_BQ)
    frowb = fh.reshape(bl, B_HEADS, s // fbq, 1, fbq)
    b_out, b_lse = _fox_fwd(proj0, *fox_blks, fcol, frow, "fox_fwd")
    lserowb = b_lse.reshape(bl, B_HEADS, s // fbq, 1, fbq)
    mix0 = [_unstack_heads(a_out), b_out]
    x1, o0 = _gate_outproj(mix0, proj0, 0, w_out0, x, gmod[0], "gate_outproj0")

    proj1, h1 = _norm_proj(x1, norm_g[1:2], scale[1], shift[1], w_in1, 0, "norm_proj1")
    cw_f, cb_f, ba_f, bx_f, lam_f = vec_all[0:4], vec_all[4:5], vec_all[5:6], vec_all[6:7], vec_all[7:8]
    hs = _lru_fwd(proj1, cw_f, cb_f, lru_w_a[0], ba_f, lru_w_x[0], bx_f, lam_f, "lru_fwd")
    x2, o1 = _gate_outproj([hs], proj1, 1, w_out1, x1, gmod[1], "gate_outproj1")

    loss_vec, dx2, g_final = _final_loss(x2, final_g.reshape(1, d), loss_target, "final_loss")
    loss = lax.psum(loss_vec[0, 0], ("x", "y", "c"))

    dhs, dgate1, do1, y1, dgm1 = _bwd_out(dx2, gmod[1], o1, [hs], proj1, 1, w_out1.T, F32, "bwd_out1")
    g_w_out1 = _matmul_tn(y1, [do1], "grad_w_out1")
    (dxr, g_cw, g_cb, g_wa, g_ba, g_wx, g_bx, g_lam) = _lru_bwd(
        proj1, hs, dhs, cw_f, cb_f, lru_w_a[0], ba_f, lru_w_x[0], bx_f, lam_f, "lru_bwd")
    dproj1 = [dxr, dgate1]
    g_w_in1 = _matmul_tn(h1, dproj1, "grad_w_in1")
    dx1, dsh1, dsc1, g_ng1 = _bwd_in(dproj1, w_in1.T, x1, norm_g[1:2], scale[1], dx2, "bwd_in1")

    dmix0, dgate0, do0, y0, dgm0 = _bwd_out(dx1, gmod[0], o0, mix0, proj0, 0, w_out0.T, BF16, "bwd_out0")
    g_w_out0 = _matmul_tn(y0, [do0], "grad_w_out0")
    da_out = _stack_heads(dmix0[:, :, :aw].astype(BF16))
    daq, dak, dav, dbias, dsink = _swa_bwd(aq, ak, av, bias, sinks, da_out, a_lse, "swa_bwd")
    dbq, dbk, dbv, dfrow = _fox_bwd(proj0, *fox_blks, dmix0, aw // LANE, fcol, frow, frowb, b_lse, lserowb,
                                    "fox_bwd")
    df = dfrow.reshape(bl, B_HEADS, s).transpose(0, 2, 1)
    df = jnp.pad(df, ((0, 0), (0, 0), (0, LANE - B_HEADS)))
    dzf, g_bf = _fox_dgate(df, zf, bf_pad, B_HEADS, "fox_dgate")
    dproj0 = ([dgate0, _unstack_heads(daq), _from_heads(dak), _from_heads(dav)]
              + [dbq, dbk, dbv, dzf.astype(BF16)])
    g_w_cat0 = _matmul_tn(h0, dproj0, "grad_w_in0")
    g_w_in0 = jnp.concatenate([g_w_cat0[:, mixw:mixw + qkv_w + B_HEADS], g_w_cat0[:, :mixw]], axis=1)
    dx0, dsh0, dsc0, g_ng0 = _bwd_in(dproj0, w_cat0.T, x, norm_g[0:1], scale[0], dx1, "bwd_in0")
    g_relb, g_sink = _swa_small_grads(dbias.reshape(bl, A_Q_HEADS, BLK, 2 * BLK),
                                      dsink.reshape(bl, A_Q_HEADS, 1, LANE), bucket, "swa_small_grads")

    dmod = jnp.concatenate([jnp.concatenate([dsh0, dsc0, dgm0], axis=-1),
                            jnp.concatenate([dsh1, dsc1, dgm1], axis=-1)], axis=1)
    dmod_rows = _pad_rows(dmod.reshape(bl * 6, d), SUBLANE)

    tail = jnp.concatenate([g_relb[:, :, 0].T.reshape(-1), g_sink[:, 0, 0], g_bf[0, :B_HEADS]])
    n_relb = REL_BUCKETS * A_Q_HEADS
    small_rows = [g_wa.reshape(-1, d), g_wx.reshape(-1, d), g_ng0, g_ng1, g_final, g_cw, g_cb, g_ba, g_bx, g_lam,
                  jnp.pad(tail, (0, d - tail.shape[0])).reshape(1, d)]
    small_counts = [r.shape[0] for r in small_rows]
    piece_rows = -(-(-(-sum(small_counts) // N_DEV)) // SUBLANE) * SUBLANE
    small_2d = jnp.concatenate(small_rows, axis=0)
    small_2d = jnp.pad(small_2d, ((0, N_DEV * piece_rows - small_2d.shape[0]), (0, 0)))
    small_pieces = small_2d.reshape(N_CHIP, 2, piece_rows, d)
    p_in0 = jnp.pad(g_w_in0.reshape(2, r_in0, N_CHIP, c_in0).transpose(2, 0, 1, 3),
                    ((0, 0), (0, 0), (0, 0), (0, d - c_in0)))
    p_in1 = g_w_in1.reshape(2, 2, r_in1, N_CHIP, c_in1).transpose(3, 0, 2, 1, 4).reshape(N_CHIP, 2, r_in1, d)
    pieces = jnp.concatenate([p_in0, g_w_out0.reshape(N_CHIP, 2, r_out0, d), p_in1,
                              g_w_out1.reshape(N_CHIP, 2, r_out1, d), small_pieces], axis=2)
    theirs = _sibling_push(pieces, True, "push_sibling_halves")
    partial = _pair_sum(jnp.reshape(ic, (1,)).astype(jnp.int32), pieces, theirs, "sum_chip")
    slots = _chip_all_to_all(partial, "exchange_grads")
    reduced = _sum_slots(slots, "sum_grads")
    mine_big = reduced[:big_rows]
    other_big = _sibling_push(mine_big[None], False, "swap_halves")[0]
    both = jnp.stack([jnp.where(ic == 0, mine_big, other_big), jnp.where(ic == 0, other_big, mine_big)])
    g_big = [both[:, :r_in0, :c_in0].reshape(d, c_in0),
             both[:, o_out0:o_in1].reshape(2 * r_out0, d),
             both[:, o_in1:o_out1].reshape(2, r_in1, 2, c_in1).transpose(0, 2, 1, 3).reshape(d, c_in1),
             both[:, o_out1:].reshape(2 * r_out1, d)]
    last = _all_gather8(jnp.concatenate([reduced[big_rows:], dmod_rows], axis=0), "gather_small_grads", pltpu.VMEM)
    last = last.reshape(N_DEV, piece_rows + dmod_rows.shape[0], d)
    small_all = last[:, :piece_rows].reshape(N_DEV * piece_rows, d)
    dmod_all = last[:, piece_rows:piece_rows + bl * 6].reshape(N_DEV * bl, 6 * d)
    dmod_chip = lax.dynamic_slice_in_dim(dmod_all.reshape(N_DEV * bl, 2, 3 * d), chip * ncol, ncol, axis=2)
    g_ada_w, g_ada_b = _ada_bwd(c_all, dmod_chip.transpose(1, 0, 2), dmod_all, "ada_bwd")
    g_ada_b = g_ada_b.reshape(2, 3 * d)
    g_small, off = [], 0
    for cnt in small_counts:
        g_small.append(small_all[off:off + cnt])
        off += cnt
    g_w_a, g_w_x = g_small[0].reshape(lru_w_a.shape[1:]), g_small[1].reshape(lru_w_x.shape[1:])
    g_norm_g = jnp.concatenate(g_small[2:4], axis=0)
    g_fin, g_cw_r, g_cb_r, g_ba_r, g_bx_r, g_lam_r = g_small[4:10]
    tail = g_small[10][0]
    g_rel_bias = tail[:n_relb].reshape(REL_BUCKETS, A_Q_HEADS)
    g_sinks, g_b_f = tail[n_relb:n_relb + A_Q_HEADS], tail[n_relb + A_Q_HEADS:n_relb + A_Q_HEADS + B_HEADS]
    cw4 = lw // N_CHIP

    def my_cols(a):
        return lax.dynamic_slice_in_dim(a, chip * cw4, cw4, axis=1)

    grads = {
        "rel_bias": g_rel_bias, "norm_g": g_norm_g, "ada_w": g_ada_w, "ada_b": g_ada_b,
        "attn_w_in": g_big[0][None], "attn_sinks": g_sinks[None], "attn_b_f": g_b_f[None],
        "attn_w_out": g_big[1][None], "lru_w_in": g_big[2][None], "lru_conv_w": my_cols(g_cw_r)[None],
        "lru_conv_b": my_cols(g_cb_r), "lru_w_a": g_w_a[None], "lru_b_a": my_cols(g_ba_r),
        "lru_w_x": g_w_x[None], "lru_b_x": my_cols(g_bx_r), "lru_lambda": my_cols(g_lam_r),
        "lru_w_out": g_big[3][None], "final_g": g_fin.reshape(d),
    }
    weights = dict(rel_bias=rel_bias, norm_g=norm_g, ada_w=ada_w, ada_b=ada_b, attn_w_in=attn_w_in,
                   attn_sinks=attn_sinks, attn_b_f=attn_b_f, attn_w_out=attn_w_out, lru_w_in=lru_w_in,
                   lru_conv_w=lru_conv_w, lru_conv_b=lru_conv_b, lru_w_a=lru_w_a, lru_b_a=lru_b_a,
                   lru_w_x=lru_w_x, lru_b_x=lru_b_x, lru_lambda=lru_lambda, lru_w_out=lru_w_out, final_g=final_g)
    moms = dict(rel_bias=(m_rel_bias, v_rel_bias), norm_g=(m_norm_g, v_norm_g), ada_w=(m_ada_w, v_ada_w),
                ada_b=(m_ada_b, v_ada_b), attn_w_in=(m_attn_w_in, v_attn_w_in),
                attn_sinks=(m_attn_sinks, v_attn_sinks), attn_b_f=(m_attn_b_f, v_attn_b_f),
                attn_w_out=(m_attn_w_out, v_attn_w_out), lru_w_in=(m_lru_w_in, v_lru_w_in),
                lru_conv_w=(m_lru_conv_w, v_lru_conv_w), lru_conv_b=(m_lru_conv_b, v_lru_conv_b),
                lru_w_a=(m_lru_w_a, v_lru_w_a), lru_b_a=(m_lru_b_a, v_lru_b_a), lru_w_x=(m_lru_w_x, v_lru_w_x),
                lru_b_x=(m_lru_b_x, v_lru_b_x), lru_lambda=(m_lru_lambda, v_lru_lambda),
                lru_w_out=(m_lru_w_out, v_lru_w_out), final_g=(m_final_g, v_final_g))
    names = list(weights)
    big_names = [n for n in names if weights[n].size >= 65536]
    small_names = [n for n in names if weights[n].size < 65536]
    delta, new_m, new_v = {}, {}, {}
    for n in big_names:
        delta[n], new_m[n], new_v[n] = _adamw(weights[n], grads[n].reshape(weights[n].shape),
                                              moms[n][0], moms[n][1], "adamw_" + n)
    cat = lambda arrs: jnp.concatenate([a.reshape(-1) for a in arrs])
    sd, sm, sv = _adamw(cat([weights[n] for n in small_names]), cat([grads[n] for n in small_names]),
                        cat([moms[n][0] for n in small_names]), cat([moms[n][1] for n in small_names]),
                        "adamw_small")
    off = 0
    for n in small_names:
        sz = weights[n].size
        shp = weights[n].shape
        delta[n], new_m[n], new_v[n] = (sd[off:off + sz].reshape(shp), sm[off:off + sz].reshape(shp),
                                        sv[off:off + sz].reshape(shp))
        off += sz
    out_grads = [grads[n].reshape(weights[n].shape) for n in names]
    return (loss, dx0, *out_grads, *[delta[n] for n in names], *[new_m[n] for n in names],
            *[new_v[n] for n in names])
```

```python
import math

import numpy as np
import jax
import jax.numpy as jnp
from jax import lax
from jax.experimental import pallas as pl
from jax.experimental.pallas import tpu as pltpu

F32 = jnp.float32
BF16 = jnp.bfloat16
MESH = pl.DeviceIdType.MESH

N_DEV = 8
N_CHIP = 4
HEAD_DIM = 64
BLK = 128
A_Q_HEADS = 8
A_KV_HEADS = 2
A_GROUP = A_Q_HEADS // A_KV_HEADS
B_HEADS = 8
REL_BUCKETS = 32
REL_MAX_EXACT = 16
REL_MAX_DIST = 128
LRU_C = 8.0
CONV_WIDTH = 4
EPS = 1e-6
NEG = -1e30
SCALE = HEAD_DIM ** -0.5
LANE = 128
SUBLANE = 8
VMEM_LIMIT = 56 * 1024 * 1024
SCAN_CHUNK = 512
ROW_TILE = 512
SWA_BPS = 1
FOX_BQ = 512
ADAM_LR = 0.001
ADAM_B1 = 0.9
ADAM_B2 = 0.999
ADAM_EPS = 1e-08
ADAM_WD = 0.01
ADAM_STEP = 10
HI = lax.Precision.HIGHEST


def _cp(sem=None):
    return pltpu.CompilerParams(dimension_semantics=sem, vmem_limit_bytes=VMEM_LIMIT)


def _dot(a, b):
    return jnp.dot(a, b, preferred_element_type=F32)


def _dot_nt(a, b):
    return lax.dot_general(a, b, (((1,), (1,)), ((), ())), preferred_element_type=F32)


def _dot_tn(a, b):
    return lax.dot_general(a, b, (((0,), (0,)), ((), ())), preferred_element_type=F32)


def _sigmoid(z):
    return 1.0 / (1.0 + jnp.exp(-z))


def _row_tile(rows, cap):
    if rows <= cap:
        return rows
    best = SUBLANE
    t = SUBLANE
    while t <= cap:
        if rows % t == 0:
            best = t
        t += SUBLANE
    return best


def _all_gather8(x_shard, name, space):
    m_per, n = x_shard.shape
    n_own = 8 if (space == pltpu.HBM and m_per % 128 == 0) else 1
    own_rows = m_per // n_own

    def body(x_ref, out_ref, send_sems, recv_sems, local_sems):
        x, y, c = lax.axis_index("x"), lax.axis_index("y"), lax.axis_index("c")
        me, sibling = (x, y, c), (x, y, 1 - c)
        chips = [(1 - x, y), (x, 1 - y), (1 - x, 1 - y)]

        def rows(px, py, pc):
            return out_ref.at[pl.ds((4 * px + 2 * py + pc) * m_per, m_per), :]

        def copy(k, block, to, src=None):
            return pltpu.make_async_remote_copy(
                src_ref=rows(*block) if src is None else src, dst_ref=rows(*block),
                send_sem=send_sems.at[k], recv_sem=recv_sems.at[k], device_id=to, device_id_type=MESH)

        base = (4 * x + 2 * y + c) * m_per
        mine = [pltpu.make_async_copy(x_ref.at[pl.ds(i * own_rows, own_rows), :],
                                      out_ref.at[pl.ds(base + i * own_rows, own_rows), :], local_sems.at[i])
                for i in range(n_own)]
        for cp in mine:
            cp.start()
        first = [copy(0, me, sibling, src=x_ref)]
        first += [copy(1 + j, me, (*chip, c), src=x_ref) for j, chip in enumerate(chips)]
        for cp in first:
            cp.start()
        passed = [copy(4 + j, (*chip, c), sibling) for j, chip in enumerate(chips)]
        for j, chip in enumerate(chips):
            copy(1 + j, (*chip, c), me).wait_recv()
            passed[j].start()
        copy(0, sibling, me).wait_recv()
        for j, chip in enumerate(chips):
            copy(4 + j, (*chip, 1 - c), me).wait_recv()
        for cp in first + passed:
            cp.wait_send()
        for cp in mine:
            cp.wait()

    return pl.pallas_call(
        body, name=name,
        out_shape=jax.ShapeDtypeStruct((N_DEV * m_per, n), x_shard.dtype),
        in_specs=[pl.BlockSpec(memory_space=space)],
        out_specs=pl.BlockSpec(memory_space=space),
        scratch_shapes=[pltpu.SemaphoreType.DMA((7,)), pltpu.SemaphoreType.DMA((7,)),
                        pltpu.SemaphoreType.DMA((n_own,))],
        compiler_params=pltpu.CompilerParams(vmem_limit_bytes=VMEM_LIMIT),
    )(x_shard)


def _sibling_push(blocks, pick_other, name):
    nblk = blocks.shape[0]
    m, n = blocks.shape[-2:]

    def body(x_ref, out_ref, send_sems, recv_sems):
        x, y, c = lax.axis_index("x"), lax.axis_index("y"), lax.axis_index("c")
        copies = []
        for k in range(nblk):
            src = x_ref.at[k, 1 - c] if pick_other else x_ref.at[k]
            copies.append(pltpu.make_async_remote_copy(
                src_ref=src, dst_ref=out_ref.at[k], send_sem=send_sems.at[k], recv_sem=recv_sems.at[k],
                device_id=(x, y, 1 - c), device_id_type=MESH))
        for cp in copies:
            cp.start()
        for cp in copies:
            cp.wait_recv()
        for cp in copies:
            cp.wait_send()

    hbm = pl.BlockSpec(memory_space=pltpu.HBM)
    return pl.pallas_call(
        body, name=name,
        out_shape=jax.ShapeDtypeStruct((nblk, m, n), blocks.dtype),
        in_specs=[hbm], out_specs=hbm,
        scratch_shapes=[pltpu.SemaphoreType.DMA((nblk,)), pltpu.SemaphoreType.DMA((nblk,))],
    )(blocks)


def _chip_all_to_all(parts, name):
    _, m, n = parts.shape

    def body(x_ref, out_ref, send_sems, recv_sems, local_sem):
        x, y, c = lax.axis_index("x"), lax.axis_index("y"), lax.axis_index("c")
        me = 2 * x + y
        mine = pltpu.make_async_copy(x_ref.at[me], out_ref.at[me], local_sem)
        mine.start()
        copies = []
        for k in range(1, N_CHIP):
            px, py = x ^ ((k >> 1) & 1), y ^ (k & 1)
            copies.append(pltpu.make_async_remote_copy(
                src_ref=x_ref.at[2 * px + py], dst_ref=out_ref.at[me],
                send_sem=send_sems.at[k - 1], recv_sem=recv_sems.at[k - 1],
                device_id=(px, py, c), device_id_type=MESH))
        for cp in copies:
            cp.start()
        for cp in copies:
            cp.wait_recv()
        for cp in copies:
            cp.wait_send()
        mine.wait()

    hbm = pl.BlockSpec(memory_space=pltpu.HBM)
    return pl.pallas_call(
        body, name=name,
        out_shape=jax.ShapeDtypeStruct(parts.shape, parts.dtype),
        in_specs=[hbm], out_specs=hbm,
        scratch_shapes=[pltpu.SemaphoreType.DMA((N_CHIP - 1,)), pltpu.SemaphoreType.DMA((N_CHIP - 1,)),
                        pltpu.SemaphoreType.DMA],
    )(parts)


def _pair_sum(core, pieces, theirs, name):
    nblk, _, m, n = pieces.shape
    tr = _row_tile(m, 536)

    def body(c_ref, p_ref, t_ref, o_ref):
        o_ref[...] = (p_ref[...] + t_ref[...]).astype(BF16)

    return pl.pallas_call(
        body, name=name,
        grid_spec=pltpu.PrefetchScalarGridSpec(
            num_scalar_prefetch=1, grid=(nblk, m // tr),
            in_specs=[pl.BlockSpec((None, None, tr, n), lambda k, i, c_ref: (k, c_ref[0], i, 0)),
                      pl.BlockSpec((None, tr, n), lambda k, i, c_ref: (k, i, 0))],
            out_specs=pl.BlockSpec((None, tr, n), lambda k, i, c_ref: (k, i, 0))),
        out_shape=jax.ShapeDtypeStruct((nblk, m, n), BF16),
        compiler_params=_cp(("parallel", "parallel")),
    )(core, pieces, theirs)


def _sum_slots(slots, name):
    k, m, n = slots.shape
    tr = _row_tile(m, 536)

    def body(s_ref, o_ref):
        acc = s_ref[0].astype(F32)
        for j in range(1, k):
            acc = acc + s_ref[j].astype(F32)
        o_ref[...] = acc

    return pl.pallas_call(
        body, name=name, grid=(m // tr,),
        out_shape=jax.ShapeDtypeStruct((m, n), F32),
        in_specs=[pl.BlockSpec((k, tr, n), lambda i: (0, i, 0))],
        out_specs=pl.BlockSpec((tr, n), lambda i: (i, 0)),
        compiler_params=_cp(("parallel",)),
    )(slots)


def _ada_fwd(c_all, w, b, name):
    r, _ = c_all.shape
    n = w.shape[1]

    def body(c_ref, w_ref, b_ref, o_ref):
        cv = c_ref[...]
        act = cv * _sigmoid(cv)
        o_ref[...] = jnp.dot(act, w_ref[...], precision=HI, preferred_element_type=F32) + b_ref[...]

    return pl.pallas_call(body, name=name, out_shape=jax.ShapeDtypeStruct((r, n), F32),
                          compiler_params=_cp())(c_all, w, b)


def _ada_bwd(c_all, dmod_chip, dmod_all, name):
    r, d = c_all.shape
    nl, _, n = dmod_chip.shape

    def body(c_ref, dm_ref, da_ref, gw_ref, gb_ref):
        cv = c_ref[...]
        act = cv * _sigmoid(cv)
        for l in range(nl):
            gw_ref[l] = lax.dot_general(act, dm_ref[l], (((0,), (0,)), ((), ())), precision=HI,
                                        preferred_element_type=F32)
        gb_ref[...] = jnp.sum(da_ref[...], axis=0, keepdims=True)

    return pl.pallas_call(
        body, name=name,
        out_shape=(jax.ShapeDtypeStruct((nl, d, n), F32), jax.ShapeDtypeStruct((1, dmod_all.shape[1]), F32)),
        compiler_params=_cp())(c_all, dmod_chip, dmod_all)


def _norm_proj(x, g, scale, shift, w, f32_cols, name):
    b, s, d = x.shape
    n = w.shape[1]
    tm = min(s, ROW_TILE)

    def body(x_ref, g_ref, sc_ref, sh_ref, w_ref, proj_ref, h_ref, *aux_ref):
        xv = x_ref[...]
        rstd = lax.rsqrt(jnp.mean(xv * xv, axis=-1, keepdims=True) + EPS)
        h = (xv * rstd) * g_ref[...] * (1.0 + sc_ref[...]) + sh_ref[...]
        hb = h.astype(BF16)
        h_ref[...] = hb
        proj = _dot(hb, w_ref[...])
        proj_ref[...] = proj.astype(BF16)
        if f32_cols:
            aux_ref[0][...] = proj[:, n - f32_cols:]

    row = lambda i, j: (i, j, 0)
    out_shape = [jax.ShapeDtypeStruct((b, s, n), BF16), jax.ShapeDtypeStruct((b, s, d), BF16)]
    out_specs = [pl.BlockSpec((None, tm, n), row), pl.BlockSpec((None, tm, d), row)]
    if f32_cols:
        out_shape.append(jax.ShapeDtypeStruct((b, s, f32_cols), F32))
        out_specs.append(pl.BlockSpec((None, tm, f32_cols), row))
    return pl.pallas_call(
        body, name=name, grid=(b, s // tm),
        out_shape=tuple(out_shape),
        in_specs=[pl.BlockSpec((None, tm, d), row),
                  pl.BlockSpec((1, d), lambda i, j: (0, 0)),
                  pl.BlockSpec((None, 1, d), lambda i, j: (i, 0, 0)),
                  pl.BlockSpec((None, 1, d), lambda i, j: (i, 0, 0)),
                  pl.BlockSpec((d, n), lambda i, j: (0, 0))],
        out_specs=tuple(out_specs),
        compiler_params=_cp(("parallel", "parallel")),
    )(x, g, scale, shift, w)


def _cat_refs(refs):
    vals = [r[...] for r in refs]
    return vals[0] if len(vals) == 1 else jnp.concatenate(vals, axis=-1)


def _gate_outproj(mix_parts, proj, gate_blk, w_out, x, gmod, name):
    b, s, _ = x.shape
    wd, d = w_out.shape
    tm = min(s, ROW_TILE)
    npart = len(mix_parts)

    def body(*refs):
        mix_refs = refs[:npart]
        gate_ref, w_ref, x_ref, gm_ref, xo_ref, o_ref = refs[npart:]
        gt = gate_ref[...].astype(F32)
        y = (_cat_refs(mix_refs) * (gt * _sigmoid(gt))).astype(BF16)
        o = _dot(y, w_ref[...])
        o_ref[...] = o.astype(BF16)
        xo_ref[...] = x_ref[...] + gm_ref[...] * o

    return pl.pallas_call(
        body, name=name, grid=(b, s // tm),
        out_shape=(jax.ShapeDtypeStruct((b, s, d), F32), jax.ShapeDtypeStruct((b, s, d), BF16)),
        in_specs=[pl.BlockSpec((None, tm, p.shape[2]), lambda i, j: (i, j, 0)) for p in mix_parts] + [
                  pl.BlockSpec((None, tm, wd), lambda i, j: (i, j, gate_blk)),
                  pl.BlockSpec((wd, d), lambda i, j: (0, 0)),
                  pl.BlockSpec((None, tm, d), lambda i, j: (i, j, 0)),
                  pl.BlockSpec((None, 1, d), lambda i, j: (i, 0, 0))],
        out_specs=(pl.BlockSpec((None, tm, d), lambda i, j: (i, j, 0)),
                   pl.BlockSpec((None, tm, d), lambda i, j: (i, j, 0))),
        compiler_params=_cp(("parallel", "parallel")),
    )(*mix_parts, proj, w_out, x, gmod)


def _final_loss(x, g, target, name):
    b, s, d = x.shape
    tm = min(s, ROW_TILE)

    def body(x_ref, g_ref, t_ref, loss_ref, dx_ref, dg_ref):
        first = jnp.logical_and(pl.program_id(0) == 0, pl.program_id(1) == 0)

        @pl.when(first)
        def _():
            loss_ref[...] = jnp.zeros_like(loss_ref)
            dg_ref[...] = jnp.zeros_like(dg_ref)

        xv = x_ref[...]
        gv = g_ref[...]
        rstd = lax.rsqrt(jnp.mean(xv * xv, axis=-1, keepdims=True) + EPS)
        xhat = xv * rstd
        err = xhat * gv - t_ref[...]
        row = jnp.mean(err * err, axis=-1, keepdims=True)
        loss_ref[...] += 0.5 * jnp.sum(row, axis=0, keepdims=True)
        dy = err * (1.0 / d)
        dg_ref[...] += jnp.sum(dy * xhat, axis=0, keepdims=True)
        dxh = dy * gv
        dx_ref[...] = rstd * (dxh - xhat * jnp.mean(dxh * xhat, axis=-1, keepdims=True))

    return pl.pallas_call(
        body, name=name, grid=(b, s // tm),
        out_shape=(jax.ShapeDtypeStruct((1, LANE), F32), jax.ShapeDtypeStruct((b, s, d), F32),
                   jax.ShapeDtypeStruct((1, d), F32)),
        in_specs=[pl.BlockSpec((None, tm, d), lambda i, j: (i, j, 0)),
                  pl.BlockSpec((1, d), lambda i, j: (0, 0)),
                  pl.BlockSpec((None, tm, d), lambda i, j: (i, j, 0))],
        out_specs=(pl.BlockSpec((1, LANE), lambda i, j: (0, 0)),
                   pl.BlockSpec((None, tm, d), lambda i, j: (i, j, 0)),
                   pl.BlockSpec((1, d), lambda i, j: (0, 0))),
        compiler_params=_cp(("arbitrary", "arbitrary")),
    )(x, g, target)


def _bwd_out(dxo, gmod, o, mix_parts, proj, gate_blk, w_out_t, dmix_dtype, name):
    b, s, d = dxo.shape
    wd = w_out_t.shape[1]
    tm = min(s, ROW_TILE)
    npart = len(mix_parts)

    def body(dx_ref, gm_ref, o_ref, *refs):
        mix_refs = refs[:npart]
        gate_ref, wt_ref, dmix_ref, dgate_ref, do_ref, y_ref, dgm_ref = refs[npart:]

        @pl.when(pl.program_id(1) == 0)
        def _():
            dgm_ref[...] = jnp.zeros_like(dgm_ref)

        dx = dx_ref[...]
        dgm_ref[...] += jnp.sum(dx * o_ref[...].astype(F32), axis=0, keepdims=True)
        dob = (gm_ref[...] * dx).astype(BF16)
        do_ref[...] = dob
        dy = _dot(dob, wt_ref[...])
        gt = gate_ref[...].astype(F32)
        sg = _sigmoid(gt)
        silu = gt * sg
        mx = _cat_refs(mix_refs)
        y_ref[...] = (mx * silu).astype(BF16)
        dmix_ref[...] = (dy * silu).astype(dmix_dtype)
        dgate_ref[...] = (dy * mx * (sg * (1.0 + gt * (1.0 - sg)))).astype(BF16)

    row = lambda i, j: (i, j, 0)
    return pl.pallas_call(
        body, name=name, grid=(b, s // tm),
        out_shape=(jax.ShapeDtypeStruct((b, s, wd), dmix_dtype), jax.ShapeDtypeStruct((b, s, wd), BF16),
                   jax.ShapeDtypeStruct((b, s, d), BF16), jax.ShapeDtypeStruct((b, s, wd), BF16),
                   jax.ShapeDtypeStruct((b, 1, d), F32)),
        in_specs=[pl.BlockSpec((None, tm, d), row),
                  pl.BlockSpec((None, 1, d), lambda i, j: (i, 0, 0)),
                  pl.BlockSpec((None, tm, d), row)] + [
                  pl.BlockSpec((None, tm, p.shape[2]), row) for p in mix_parts] + [
                  pl.BlockSpec((None, tm, wd), lambda i, j: (i, j, gate_blk)),
                  pl.BlockSpec((d, wd), lambda i, j: (0, 0))],
        out_specs=(pl.BlockSpec((None, tm, wd), row), pl.BlockSpec((None, tm, wd), row),
                   pl.BlockSpec((None, tm, d), row), pl.BlockSpec((None, tm, wd), row),
                   pl.BlockSpec((None, 1, d), lambda i, j: (i, 0, 0))),
        compiler_params=_cp(("parallel", "arbitrary")),
    )(dxo, gmod, o, *mix_parts, proj, w_out_t)


def _bwd_in(dproj_parts, w_in_t, x, g, scale, dxo, name):
    b, s, d = x.shape
    n = w_in_t.shape[0]
    tm = min(s, ROW_TILE)
    npart = len(dproj_parts)

    def body(*refs):
        dp_refs = refs[:npart]
        wt_ref, x_ref, g_ref, sc_ref, dxo_ref, dx_ref, dsh_ref, dsc_ref, dg_ref = refs[npart:]

        @pl.when(jnp.logical_and(pl.program_id(0) == 0, pl.program_id(1) == 0))
        def _():
            dg_ref[...] = jnp.zeros_like(dg_ref)

        @pl.when(pl.program_id(1) == 0)
        def _():
            dsh_ref[...] = jnp.zeros_like(dsh_ref)
            dsc_ref[...] = jnp.zeros_like(dsc_ref)

        dh = _dot(_cat_refs(dp_refs), wt_ref[...])
        xv = x_ref[...]
        gv = g_ref[...]
        one_sc = 1.0 + sc_ref[...]
        rstd = lax.rsqrt(jnp.mean(xv * xv, axis=-1, keepdims=True) + EPS)
        xhat = xv * rstd
        dsh_ref[...] += jnp.sum(dh, axis=0, keepdims=True)
        dsc_ref[...] += jnp.sum(dh * (xhat * gv), axis=0, keepdims=True)
        dhs = dh * one_sc
        dg_ref[...] += jnp.sum(dhs * xhat, axis=0, keepdims=True)
        dxh = dhs * gv
        dx_ref[...] = dxo_ref[...] + rstd * (dxh - xhat * jnp.mean(dxh * xhat, axis=-1, keepdims=True))

    row = lambda i, j: (i, j, 0)
    per_b = lambda i, j: (i, 0, 0)
    return pl.pallas_call(
        body, name=name, grid=(b, s // tm),
        out_shape=(jax.ShapeDtypeStruct((b, s, d), F32), jax.ShapeDtypeStruct((b, 1, d), F32),
                   jax.ShapeDtypeStruct((b, 1, d), F32), jax.ShapeDtypeStruct((1, d), F32)),
        in_specs=[pl.BlockSpec((None, tm, p.shape[2]), row) for p in dproj_parts] + [
                  pl.BlockSpec((n, d), lambda i, j: (0, 0)),
                  pl.BlockSpec((None, tm, d), row),
                  pl.BlockSpec((1, d), lambda i, j: (0, 0)),
                  pl.BlockSpec((None, 1, d), per_b),
                  pl.BlockSpec((None, tm, d), row)],
        out_specs=(pl.BlockSpec((None, tm, d), row), pl.BlockSpec((None, 1, d), per_b),
                   pl.BlockSpec((None, 1, d), per_b), pl.BlockSpec((1, d), lambda i, j: (0, 0))),
        compiler_params=_cp(("arbitrary", "arbitrary")),
    )(*dproj_parts, w_in_t, x, g, scale, dxo)


def _matmul_tn(a, b_parts, name):
    bsz, s, m = a.shape
    n = sum(p.shape[2] for p in b_parts)
    tk = next(c for c in (512, 256, 128) if s % c == 0)
    npart = len(b_parts)

    def body(a_ref, *refs):
        b_refs, o_ref = refs[:npart], refs[npart]

        @pl.when(jnp.logical_and(pl.program_id(0) == 0, pl.program_id(1) == 0))
        def _():
            o_ref[...] = jnp.zeros_like(o_ref)

        o_ref[...] += _dot_tn(a_ref[...], _cat_refs(b_refs))

    row = lambda i, k: (i, k, 0)
    return pl.pallas_call(
        body, name=name, grid=(bsz, s // tk),
        out_shape=jax.ShapeDtypeStruct((m, n), F32),
        in_specs=[pl.BlockSpec((None, tk, m), row)] + [pl.BlockSpec((None, tk, p.shape[2]), row) for p in b_parts],
        out_specs=pl.BlockSpec((m, n), lambda i, k: (0, 0)),
        compiler_params=_cp(("arbitrary", "arbitrary")),
    )(a, *b_parts)


def _rel_buckets():
    qi = np.arange(BLK)[:, None]
    kj = np.arange(2 * BLK)[None, :]
    rel = qi - kj + BLK
    n = np.maximum(rel, 0)
    nf = np.maximum(n, 1).astype(np.float32)
    large = REL_MAX_EXACT + (np.log(nf / REL_MAX_EXACT) / math.log(REL_MAX_DIST / REL_MAX_EXACT)
                             * (REL_BUCKETS - REL_MAX_EXACT)).astype(np.int32)
    large = np.minimum(large, REL_BUCKETS - 1)
    bucket = np.where(n < REL_MAX_EXACT, n, large).astype(np.int32)
    valid = ((rel >= 0) & (rel < BLK)).astype(np.int32)
    return bucket, valid


def _swa_bias(rel_bias_t, bucket, valid, name):
    nh = rel_bias_t.shape[0]

    def body(rb_ref, bk_ref, vl_ref, o_ref):
        h = pl.program_id(0)
        bk = bk_ref[...]
        acc = jnp.zeros(bk.shape, F32)
        for i in range(REL_BUCKETS):
            acc = jnp.where(bk == i, rb_ref[h, i], acc)
        o_ref[...] = jnp.where(vl_ref[...] > 0, acc, NEG)

    return pl.pallas_call(
        body, name=name, grid=(nh,),
        out_shape=jax.ShapeDtypeStruct((nh, BLK, 2 * BLK), F32),
        in_specs=[pl.BlockSpec(memory_space=pltpu.SMEM),
                  pl.BlockSpec((BLK, 2 * BLK), lambda h: (0, 0)),
                  pl.BlockSpec((BLK, 2 * BLK), lambda h: (0, 0))],
        out_specs=pl.BlockSpec((None, BLK, 2 * BLK), lambda h: (h, 0, 0)),
        compiler_params=_cp(("arbitrary",)),
    )(rel_bias_t, bucket, valid)


def _swa_scores(n, q, kw, bias_ref):
    sc = _dot_nt(q, kw) * SCALE + bias_ref[...]
    second = lax.broadcasted_iota(jnp.int32, sc.shape, 1) >= BLK
    return jnp.where(jnp.logical_or(n > 0, second), sc, NEG)


def _pad_front(dst_ref, src_ref):
    dst_ref[0:BLK, :] = jnp.zeros((BLK, dst_ref.shape[1]), dst_ref.dtype)
    dst_ref[BLK:, :] = src_ref[...]


def _swa_fwd(q, k, v, bias, sinks, name):
    b, hkv, nb, rows, hd = q.shape
    wide = bias.shape[2]
    stride = wide - BLK
    s = nb * stride

    def body(q_ref, k_ref, v_ref, bias_ref, sink_ref, o_ref, l_ref, kpad_ref, vpad_ref):
        _pad_front(kpad_ref, k_ref)
        _pad_front(vpad_ref, v_ref)
        sink = sink_ref[...]

        def step(n, carry):
            w0 = pl.multiple_of(n * stride, BLK)
            sc = _swa_scores(n, q_ref[n], kpad_ref[pl.ds(w0, wide), :], bias_ref)
            m = jnp.maximum(jnp.max(sc, axis=1, keepdims=True), sink)
            e = jnp.exp(sc - m)
            den = jnp.sum(e, axis=1, keepdims=True) + jnp.exp(sink - m)
            o_ref[n] = _dot((e * (1.0 / den)).astype(BF16), vpad_ref[pl.ds(w0, wide), :])
            l_ref[n] = m + jnp.log(den)
            return carry

        lax.fori_loop(0, nb, step, 0)

    qspec = pl.BlockSpec((None, None, nb, rows, hd), lambda i, kv: (i, kv, 0, 0, 0))
    kspec = pl.BlockSpec((None, None, s, hd), lambda i, kv: (i, kv, 0, 0))
    return pl.pallas_call(
        body, name=name, grid=(b, hkv),
        out_shape=(jax.ShapeDtypeStruct((b, hkv, nb, rows, hd), F32), jax.ShapeDtypeStruct((b, hkv, nb, rows, 1), F32)),
        in_specs=[qspec, kspec, kspec,
                  pl.BlockSpec((None, rows, wide), lambda i, kv: (kv, 0, 0)),
                  pl.BlockSpec((None, rows, 1), lambda i, kv: (kv, 0, 0))],
        out_specs=(qspec, pl.BlockSpec((None, None, nb, rows, 1), lambda i, kv: (i, kv, 0, 0, 0))),
        scratch_shapes=[pltpu.VMEM((s + BLK, hd), BF16), pltpu.VMEM((s + BLK, hd), BF16)],
        compiler_params=_cp(("parallel", "parallel")),
    )(q, k, v, bias, sinks)


def _swa_bwd(q, k, v, bias, sinks, do, lse, name):
    b, hkv, nb, rows, hd = q.shape
    wide = bias.shape[2]
    stride = wide - BLK
    s = nb * stride

    def body(q_ref, k_ref, v_ref, bias_ref, sink_ref, do_ref, l_ref,
             dq_ref, dk_ref, dv_ref, db_ref, dsk_ref, kpad_ref, vpad_ref, dkpad_ref, dvpad_ref):
        _pad_front(kpad_ref, k_ref)
        _pad_front(vpad_ref, v_ref)
        dkpad_ref[...] = jnp.zeros_like(dkpad_ref)
        dvpad_ref[...] = jnp.zeros_like(dvpad_ref)
        db_ref[...] = jnp.zeros_like(db_ref)
        sink = sink_ref[...]

        def step(n, dsink):
            w0 = pl.multiple_of(n * stride, BLK)
            win = pl.ds(w0, wide)
            qn = q_ref[n]
            kw = kpad_ref[win, :]
            ln = l_ref[n]
            p = jnp.exp(_swa_scores(n, qn, kw, bias_ref) - ln)
            dob = do_ref[n]
            dp = _dot_nt(dob, vpad_ref[win, :])
            delta = jnp.sum(p * dp, axis=1, keepdims=True)
            ds = p * (dp - delta)
            db_ref[...] += ds
            dsb = ds.astype(BF16)
            dq_ref[n] = (_dot(dsb, kw) * SCALE).astype(BF16)
            dkpad_ref[win, :] += _dot_tn(dsb, qn)
            dvpad_ref[win, :] += _dot_tn(p.astype(BF16), dob)
            return dsink - jnp.exp(sink - ln) * delta

        dsink = lax.fori_loop(0, nb, step, jnp.zeros((rows, 1), F32))
        for g in range(A_GROUP):
            tot = jnp.zeros((1, 1), F32)
            for blk in range(rows // (A_GROUP * BLK)):
                r0 = (blk * A_GROUP + g) * BLK
                tot = tot + jnp.sum(dsink[r0:r0 + BLK, :], axis=0, keepdims=True)
            dsk_ref[g] = jnp.broadcast_to(tot, (1, LANE))
        dk_ref[...] = (dkpad_ref[BLK:, :] * SCALE).astype(BF16)
        dv_ref[...] = dvpad_ref[BLK:, :].astype(BF16)

    qspec = pl.BlockSpec((None, None, nb, rows, hd), lambda i, kv: (i, kv, 0, 0, 0))
    kspec = pl.BlockSpec((None, None, s, hd), lambda i, kv: (i, kv, 0, 0))
    return pl.pallas_call(
        body, name=name, grid=(b, hkv),
        out_shape=(jax.ShapeDtypeStruct((b, hkv, nb, rows, hd), BF16), jax.ShapeDtypeStruct((b, hkv, s, hd), BF16),
                   jax.ShapeDtypeStruct((b, hkv, s, hd), BF16), jax.ShapeDtypeStruct((b, hkv, rows, wide), F32),
                   jax.ShapeDtypeStruct((b, hkv, A_GROUP, 1, LANE), F32)),
        in_specs=[qspec, kspec, kspec,
                  pl.BlockSpec((None, rows, wide), lambda i, kv: (kv, 0, 0)),
                  pl.BlockSpec((None, rows, 1), lambda i, kv: (kv, 0, 0)),
                  qspec,
                  pl.BlockSpec((None, None, nb, rows, 1), lambda i, kv: (i, kv, 0, 0, 0))],
        out_specs=(qspec, kspec, kspec,
                   pl.BlockSpec((None, None, rows, wide), lambda i, kv: (i, kv, 0, 0)),
                   pl.BlockSpec((None, None, A_GROUP, 1, LANE), lambda i, kv: (i, kv, 0, 0, 0))),
        scratch_shapes=[pltpu.VMEM((s + BLK, hd), BF16), pltpu.VMEM((s + BLK, hd), BF16),
                        pltpu.VMEM((s + BLK, hd), F32), pltpu.VMEM((s + BLK, hd), F32)],
        compiler_params=_cp(("parallel", "parallel")),
    )(q, k, v, bias, sinks, do, lse)


def _swa_small_grads(db, dsk, bucket, name):
    b, nh = db.shape[0], db.shape[1]

    def body(db_ref, dsk_ref, bk_ref, gb_ref, gs_ref):
        acc = db_ref[0]
        sk = dsk_ref[0]
        for i in range(1, b):
            acc = acc + db_ref[i]
            sk = sk + dsk_ref[i]
        gs_ref[...] = sk
        bk = bk_ref[...]
        for i in range(REL_BUCKETS):
            part = jnp.sum(jnp.where(bk == i, acc, 0.0), axis=1, keepdims=True)
            tot = jnp.sum(part, axis=0, keepdims=True)
            gb_ref[i:i + 1, :] = jnp.broadcast_to(tot, (1, LANE))

    return pl.pallas_call(
        body, name=name, grid=(nh,),
        out_shape=(jax.ShapeDtypeStruct((nh, REL_BUCKETS, LANE), F32), jax.ShapeDtypeStruct((nh, 1, LANE), F32)),
        in_specs=[pl.BlockSpec((b, None, BLK, 2 * BLK), lambda h: (0, h, 0, 0)),
                  pl.BlockSpec((b, None, 1, LANE), lambda h: (0, h, 0, 0)),
                  pl.BlockSpec((BLK, 2 * BLK), lambda h: (0, 0))],
        out_specs=(pl.BlockSpec((None, REL_BUCKETS, LANE), lambda h: (h, 0, 0)),
                   pl.BlockSpec((None, 1, LANE), lambda h: (h, 0, 0))),
        compiler_params=_cp(("parallel",)),
    )(db, dsk, bucket)


def _log_sigmoid(z):
    return jnp.minimum(z, 0.0) - jnp.log(1.0 + jnp.exp(-jnp.abs(z)))


def _fox_decay(z, bf, name):
    b, s, w = z.shape
    nb = s // BLK

    def body(z_ref, bf_ref, f_ref):
        r = lax.broadcasted_iota(jnp.int32, (BLK, BLK), 0)
        c = lax.broadcasted_iota(jnp.int32, (BLK, BLK), 1)
        tri = (c <= r).astype(F32)

        def step(n, carry):
            r0 = pl.multiple_of(n * BLK, BLK)
            lf = _log_sigmoid(z_ref[pl.ds(r0, BLK), :] + bf_ref[...])
            f_ref[pl.ds(r0, BLK), :] = jnp.dot(tri, lf, precision=HI, preferred_element_type=F32) + carry
            return carry + jnp.sum(lf, axis=0, keepdims=True)

        lax.fori_loop(0, nb, step, jnp.zeros((1, w), F32))

    spec = pl.BlockSpec((None, s, w), lambda i: (i, 0, 0))
    return pl.pallas_call(
        body, name=name, grid=(b,), out_shape=jax.ShapeDtypeStruct((b, s, w), F32),
        in_specs=[spec, pl.BlockSpec((1, w), lambda i: (0, 0))], out_specs=spec,
        compiler_params=_cp(("parallel",)),
    )(z, bf)


def _fox_dgate(df, z, bf, nheads, name):
    b, s, w = z.shape
    nb = s // BLK

    def body(df_ref, z_ref, bf_ref, dz_ref, dbf_ref):
        @pl.when(pl.program_id(0) == 0)
        def _():
            dbf_ref[...] = jnp.zeros_like(dbf_ref)

        r = lax.broadcasted_iota(jnp.int32, (BLK, BLK), 0)
        c = lax.broadcasted_iota(jnp.int32, (BLK, BLK), 1)
        tri = (c >= r).astype(F32)
        lane = lax.broadcasted_iota(jnp.int32, (BLK, w), 1)

        def step(i, carry):
            tail, dbf = carry
            r0 = pl.multiple_of((nb - 1 - i) * BLK, BLK)
            dfb = df_ref[pl.ds(r0, BLK), :]
            dlf = jnp.dot(tri, dfb, precision=HI, preferred_element_type=F32) + tail
            dz = jnp.where(lane < nheads, dlf * _sigmoid(-(z_ref[pl.ds(r0, BLK), :] + bf_ref[...])), 0.0)
            dz_ref[pl.ds(r0, BLK), :] = dz
            return tail + jnp.sum(dfb, axis=0, keepdims=True), dbf + jnp.sum(dz, axis=0, keepdims=True)

        zero = jnp.zeros((1, w), F32)
        _, dbf = lax.fori_loop(0, nb, step, (zero, zero))
        dbf_ref[...] += dbf

    spec = pl.BlockSpec((None, s, w), lambda i: (i, 0, 0))
    one = pl.BlockSpec((1, w), lambda i: (0, 0))
    return pl.pallas_call(
        body, name=name, grid=(b,),
        out_shape=(jax.ShapeDtypeStruct((b, s, w), F32), jax.ShapeDtypeStruct((1, w), F32)),
        in_specs=[spec, spec, one], out_specs=(spec, one),
        compiler_params=_cp(("arbitrary",)),
    )(df, z, bf)


def _fox_segments(nb):
    per = max(1, nb // 4)
    return per, nb // per


def _head_masks(shape, axis):
    idx = lax.broadcasted_iota(jnp.int32, shape, axis)
    return idx < HEAD_DIM, idx >= HEAD_DIM


def _fox_fwd(proj, qblk, kblk, vblk, fcol, frow, name):
    b, s, _ = proj.shape
    nh = frow.shape[1]
    npair = nh // 2
    BLK = min(s, FOX_BQ)
    assert s % BLK == 0
    per, nseg = _fox_segments(s // BLK)

    def body(q_ref, k_ref, v_ref, fc_ref, fr_ref, o_ref, l_ref, qm_ref, kt_ref, vb_ref):
        lo, hi = _head_masks((s, LANE), 1)
        qv = q_ref[...].astype(F32) * SCALE
        qm_ref[0] = jnp.where(lo, qv, 0.0).astype(BF16)
        qm_ref[1] = jnp.where(hi, qv, 0.0).astype(BF16)
        kt_ref[...] = k_ref[...].astype(F32).T.astype(BF16)
        vb_ref[...] = v_ref[...].astype(BF16)
        lane_lo = lax.broadcasted_iota(jnp.int32, (BLK, LANE), 1) < HEAD_DIM
        tail = per * BLK
        causal = (lax.broadcasted_iota(jnp.int32, (BLK, tail), 1)
                  - lax.broadcasted_iota(jnp.int32, (BLK, tail), 0))
        for seg in range(nseg):
            w = (seg + 1) * tail

            def qstep(n, carry):
                r0 = pl.multiple_of(n * BLK, BLK)
                outs = []
                for hh in range(2):
                    sc = _dot(qm_ref[hh, pl.ds(r0, BLK), :], kt_ref[:, :w])
                    sc = sc + (fc_ref[hh, pl.ds(r0, BLK), :] - fr_ref[hh, :, :w])
                    masked = jnp.where(causal <= (n - seg * per) * BLK, sc[:, w - tail:], NEG)
                    sc = masked if seg == 0 else jnp.concatenate([sc[:, :w - tail], masked], axis=1)
                    m = jnp.max(sc, axis=1, keepdims=True)
                    e = jnp.exp(sc - m)
                    l = jnp.sum(e, axis=1, keepdims=True)
                    outs.append(_dot((e * (1.0 / l)).astype(BF16), vb_ref[:w, :]))
                    l_ref[hh, pl.ds(r0, BLK), :] = m + jnp.log(l)
                o_ref[pl.ds(r0, BLK), :] = jnp.where(lane_lo, outs[0], outs[1])
                return carry

            lax.fori_loop(seg * per, (seg + 1) * per, qstep, 0)

    def tok(blk):
        return pl.BlockSpec((None, s, LANE), lambda i, p: (i, 0, blk + p))

    col = pl.BlockSpec((None, 2, s, 1), lambda i, p: (i, p, 0, 0))
    rowspec = pl.BlockSpec((None, 2, 1, s), lambda i, p: (i, p, 0, 0))
    return pl.pallas_call(
        body, name=name, grid=(b, npair),
        out_shape=(jax.ShapeDtypeStruct((b, s, nh * HEAD_DIM), F32), jax.ShapeDtypeStruct((b, nh, s, 1), F32)),
        in_specs=[tok(qblk), tok(kblk), tok(vblk), col, rowspec],
        out_specs=(pl.BlockSpec((None, s, LANE), lambda i, p: (i, 0, p)), col),
        scratch_shapes=[pltpu.VMEM((2, s, LANE), BF16), pltpu.VMEM((LANE, s), BF16), pltpu.VMEM((s, LANE), BF16)],
        compiler_params=_cp(("parallel", "parallel")),
    )(proj, proj, proj, fcol, frow)


def _fox_bwd(proj, qblk, kblk, vblk, dmix, doblk, fcol, frow, frowb, lse, lserowb, name):
    b, s, _ = proj.shape
    nh = frow.shape[1]
    npair = nh // 2
    BLK = min(s, FOX_BQ)
    assert s % BLK == 0
    nb = s // BLK
    per, nseg = _fox_segments(nb)

    def body(q_ref, k_ref, v_ref, do_ref, fc_ref, fr_ref, frb_ref, l_ref, lrb_ref,
             dq_ref, dk_ref, dv_ref, dfr_ref,
             qm_ref, dom_ref, kb_ref, vb_ref, kt_ref, vt_ref, qtm_ref, dotm_ref, dka_ref, dva_ref):
        lo, hi = _head_masks((s, LANE), 1)
        qv = q_ref[...].astype(F32) * SCALE
        dov = do_ref[...]
        for hh, msk in enumerate((lo, hi)):
            qm_ref[hh] = jnp.where(msk, qv, 0.0).astype(BF16)
            dom_ref[hh] = jnp.where(msk, dov, 0.0).astype(BF16)
        kv = k_ref[...].astype(F32)
        vv = v_ref[...].astype(F32)
        kb_ref[...] = kv.astype(BF16)
        vb_ref[...] = vv.astype(BF16)
        kt_ref[...] = kv.T.astype(BF16)
        vt_ref[...] = vv.T.astype(BF16)
        rlo, rhi = _head_masks((LANE, BLK), 0)

        def tstep(n, carry):
            r0 = pl.multiple_of(n * BLK, BLK)
            qt = (q_ref[pl.ds(r0, BLK), :].astype(F32) * SCALE).T
            dt = do_ref[pl.ds(r0, BLK), :].astype(F32).T
            for hh, msk in enumerate((rlo, rhi)):
                qtm_ref[hh, n] = jnp.where(msk, qt, 0.0).astype(BF16)
                dotm_ref[hh, n] = jnp.where(msk, dt, 0.0).astype(BF16)
            return carry

        lax.fori_loop(0, nb, tstep, 0)
        dka_ref[...] = jnp.zeros_like(dka_ref)
        dva_ref[...] = jnp.zeros_like(dva_ref)
        dfr_ref[...] = jnp.zeros_like(dfr_ref)
        lane_lo = lax.broadcasted_iota(jnp.int32, (BLK, LANE), 1) < HEAD_DIM
        tail = per * BLK
        causal = (lax.broadcasted_iota(jnp.int32, (BLK, tail), 1)
                  - lax.broadcasted_iota(jnp.int32, (BLK, tail), 0))
        causal_t = (lax.broadcasted_iota(jnp.int32, (tail, BLK), 0)
                    - lax.broadcasted_iota(jnp.int32, (tail, BLK), 1))
        for seg in range(nseg):
            w = (seg + 1) * tail

            def nstep(n, carry):
                r0 = pl.multiple_of(n * BLK, BLK)
                lim = (n - seg * per) * BLK
                dqs = []
                for hh in range(2):
                    qn = qm_ref[hh, pl.ds(r0, BLK), :]
                    don = dom_ref[hh, pl.ds(r0, BLK), :]
                    sc = _dot(qn, kt_ref[:, :w]) + ((fc_ref[hh, pl.ds(r0, BLK), :] - l_ref[hh, pl.ds(r0, BLK), :])
                                                   - fr_ref[hh, :, :w])
                    masked = jnp.where(causal <= lim, sc[:, w - tail:], NEG)
                    p = jnp.exp(masked if seg == 0 else jnp.concatenate([sc[:, :w - tail], masked], axis=1))
                    dp = _dot(don, vt_ref[:, :w])
                    ds = p * (dp - jnp.sum(p * dp, axis=1, keepdims=True))
                    dqs.append(_dot(ds.astype(BF16), kb_ref[:w, :]))
                    dfr_ref[hh, :, :w] -= jnp.sum(ds, axis=0, keepdims=True)
                    sct = _dot(kb_ref[:w, :], qtm_ref[hh, n]) + ((frb_ref[hh, n] - lrb_ref[hh, n]) - fc_ref[hh, :w, :])
                    masked_t = jnp.where(causal_t <= lim, sct[w - tail:, :], NEG)
                    pt = jnp.exp(masked_t if seg == 0 else jnp.concatenate([sct[:w - tail, :], masked_t], axis=0))
                    dpt = _dot(vb_ref[:w, :], dotm_ref[hh, n])
                    dst = pt * (dpt - jnp.sum(pt * dpt, axis=0, keepdims=True))
                    dka_ref[:w, :] += _dot(dst.astype(BF16), qn)
                    dva_ref[:w, :] += _dot(pt.astype(BF16), don)
                dq_ref[pl.ds(r0, BLK), :] = (jnp.where(lane_lo, dqs[0], dqs[1]) * SCALE).astype(BF16)
                return carry

            lax.fori_loop(seg * per, (seg + 1) * per, nstep, 0)
        dk_ref[...] = dka_ref[...].astype(BF16)
        dv_ref[...] = dva_ref[...].astype(BF16)

    def tok(blk):
        return pl.BlockSpec((None, s, LANE), lambda i, p: (i, 0, blk + p))

    col = pl.BlockSpec((None, 2, s, 1), lambda i, p: (i, p, 0, 0))
    rowspec = pl.BlockSpec((None, 2, 1, s), lambda i, p: (i, p, 0, 0))
    rowbspec = pl.BlockSpec((None, 2, nb, 1, BLK), lambda i, p: (i, p, 0, 0, 0))
    outtok = pl.BlockSpec((None, s, LANE), lambda i, p: (i, 0, p))
    shp = jax.ShapeDtypeStruct((b, s, nh * HEAD_DIM), BF16)
    return pl.pallas_call(
        body, name=name, grid=(b, npair),
        out_shape=(shp, shp, shp, jax.ShapeDtypeStruct((b, nh, 1, s), F32)),
        in_specs=[tok(qblk), tok(kblk), tok(vblk),
                  pl.BlockSpec((None, s, LANE), lambda i, p: (i, 0, doblk + p)),
                  col, rowspec, rowbspec, col, rowbspec],
        out_specs=(outtok, outtok, outtok, rowspec),
        scratch_shapes=[pltpu.VMEM((2, s, LANE), BF16), pltpu.VMEM((2, s, LANE), BF16),
                        pltpu.VMEM((s, LANE), BF16), pltpu.VMEM((s, LANE), BF16),
                        pltpu.VMEM((LANE, s), BF16), pltpu.VMEM((LANE, s), BF16),
                        pltpu.VMEM((2, nb, LANE, BLK), BF16), pltpu.VMEM((2, nb, LANE, BLK), BF16),
                        pltpu.VMEM((s, LANE), F32), pltpu.VMEM((s, LANE), F32)],
        compiler_params=_cp(("parallel", "parallel")),
    )(proj, proj, proj, dmix, fcol, frow, frowb, lse, lserowb)


def _expm1(x):
    poly = x * (1.0 + x * (1.0 / 2 + x * (1.0 / 6 + x * (1.0 / 24 + x * (1.0 / 120 + x * (1.0 / 720))))))
    return jnp.where(x > -0.1, poly, jnp.exp(x) - 1.0)


def _softplus(z):
    return jnp.maximum(z, 0.0) + jnp.log(1.0 + jnp.exp(-jnp.abs(z)))


def _scan_rows(a, u, carry, row, up):
    tc, c = a.shape
    d = 1
    while d < tc:
        if d < SUBLANE:
            keep = (row >= d) if up else (row < tc - d)
            shift = d if up else tc - d
            a_sh = jnp.where(keep, pltpu.roll(a, shift, 0), 1.0)
            u_sh = jnp.where(keep, pltpu.roll(u, shift, 0), 0.0)
        elif up:
            a_sh = jnp.concatenate([jnp.ones((d, c), F32), a[:tc - d]], axis=0)
            u_sh = jnp.concatenate([jnp.zeros((d, c), F32), u[:tc - d]], axis=0)
        else:
            a_sh = jnp.concatenate([a[d:], jnp.ones((d, c), F32)], axis=0)
            u_sh = jnp.concatenate([u[d:], jnp.zeros((d, c), F32)], axis=0)
        u = a * u_sh + u
        a = a * a_sh
        d *= 2
    return u + a * carry


def _scan_up(a, u, carry, row):
    return _scan_rows(a, u, carry, row, True)


def _scan_down(bnext, g, carry, row):
    return _scan_rows(bnext, g, carry, row, False)


def _pick_row(val, row, which):
    return jnp.sum(jnp.where(row == which, val, 0.0), axis=0, keepdims=True)


def _lru_gates(xpad_ref, t0, tc, cw_ref, cb_ref, wa, ba_ref, wx, bx_ref, sp):
    xw = xpad_ref[pl.ds(t0, tc + SUBLANE), :]
    xc = cb_ref[...]
    for j in range(CONV_WIDTH):
        sh = CONV_WIDTH - 1 - j
        xs = xw if sh == 0 else pltpu.roll(xw, sh, 0)
        xc = xc + xs[SUBLANE:, :] * cw_ref[j:j + 1, :]
    xcb = xc.astype(BF16)
    r = _sigmoid(_dot(xcb, wa) + ba_ref[...])
    i = _sigmoid(_dot(xcb, wx) + bx_ref[...])
    la = -LRU_C * r * sp
    return xc, r, i, la


def _lru_specs(s, cb):
    seq = lambda bi, ni: (bi, 0, ni)
    return dict(
        seq=pl.BlockSpec((None, s, cb), seq),
        cw=pl.BlockSpec((CONV_WIDTH, cb), lambda bi, ni: (0, ni)),
        vec=pl.BlockSpec((1, cb), lambda bi, ni: (0, ni)),
        wblk=pl.BlockSpec((None, cb, cb), lambda bi, ni: (ni, 0, 0)),
    )


def _lru_fwd(proj, cw, cb_, wa, ba, wx, bx, lam, name):
    b, s, _ = proj.shape
    nblk, cb, _ = wa.shape
    tc = min(s, SCAN_CHUNK)
    nc = s // tc

    def body(x_ref, cw_ref, cb_ref, wa_ref, ba_ref, wx_ref, bx_ref, lam_ref, hs_ref, xpad_ref):
        xpad_ref[0:SUBLANE, :] = jnp.zeros((SUBLANE, cb), F32)
        xpad_ref[SUBLANE:, :] = x_ref[...].astype(F32)
        wa_b = wa_ref[...].astype(BF16)
        wx_b = wx_ref[...].astype(BF16)
        sp = _softplus(-lam_ref[...])
        row = lax.broadcasted_iota(jnp.int32, (tc, cb), 0)

        def chunk(ci, carry):
            t0 = pl.multiple_of(ci * tc, tc)
            xc, r, i, la = _lru_gates(xpad_ref, t0, tc, cw_ref, cb_ref, wa_b, ba_ref, wx_b, bx_ref, sp)
            a = jnp.exp(la)
            u = jnp.sqrt(-_expm1(2.0 * la)) * (i * xc)
            h = _scan_up(a, u, carry, row)
            hs_ref[pl.ds(t0, tc), :] = h
            return _pick_row(h, row, tc - 1)

        lax.fori_loop(0, nc, chunk, jnp.zeros((1, cb), F32))

    sp_ = _lru_specs(s, cb)
    return pl.pallas_call(
        body, name=name, grid=(b, nblk),
        out_shape=jax.ShapeDtypeStruct((b, s, nblk * cb), F32),
        in_specs=[sp_["seq"], sp_["cw"], sp_["vec"], sp_["wblk"], sp_["vec"], sp_["wblk"], sp_["vec"], sp_["vec"]],
        out_specs=sp_["seq"],
        scratch_shapes=[pltpu.VMEM((s + SUBLANE, cb), F32)],
        compiler_params=_cp(("parallel", "parallel")),
    )(proj, cw, cb_, wa, ba, wx, bx, lam)


def _lru_bwd(proj, hs, dhs, cw, cb_, wa, ba, wx, bx, lam, name):
    b, s, _ = proj.shape
    nblk, cb, _ = wa.shape
    tc = min(s, SCAN_CHUNK)
    nc = s // tc

    def body(x_ref, hs_ref, dhs_ref, cw_ref, cb_ref, wa_ref, ba_ref, wx_ref, bx_ref, lam_ref,
             dx_ref, dcw_ref, dcb_ref, dwa_ref, dba_ref, dwx_ref, dbx_ref, dlam_ref,
             xpad_ref, hpad_ref, dcpad_ref, xc_ref, r_ref, i_ref, a_ref, mult_ref):
        @pl.when(pl.program_id(1) == 0)
        def _():
            for ref in (dcw_ref, dcb_ref, dwa_ref, dba_ref, dwx_ref, dbx_ref, dlam_ref):
                ref[...] = jnp.zeros_like(ref)

        zeros8 = jnp.zeros((SUBLANE, cb), F32)
        xpad_ref[0:SUBLANE, :] = zeros8
        xpad_ref[SUBLANE:, :] = x_ref[...].astype(F32)
        hpad_ref[0:SUBLANE, :] = zeros8
        hpad_ref[SUBLANE:, :] = hs_ref[...]
        dcpad_ref[s:s + SUBLANE, :] = zeros8
        wa_b = wa_ref[...].astype(BF16)
        wx_b = wx_ref[...].astype(BF16)
        lam_v = lam_ref[...]
        sp = _softplus(-lam_v)
        dsp_dlam = -_sigmoid(-lam_v)
        row = lax.broadcasted_iota(jnp.int32, (tc, cb), 0)

        def recompute(ci, carry):
            t0 = pl.multiple_of(ci * tc, tc)
            xc, r, i, la = _lru_gates(xpad_ref, t0, tc, cw_ref, cb_ref, wa_b, ba_ref, wx_b, bx_ref, sp)
            xc_ref[pl.ds(t0, tc), :] = xc
            r_ref[pl.ds(t0, tc), :] = r
            i_ref[pl.ds(t0, tc), :] = i
            a_ref[pl.ds(t0, tc), :] = jnp.exp(la)
            mult_ref[pl.ds(t0, tc), :] = jnp.sqrt(-_expm1(2.0 * la))
            return carry

        lax.fori_loop(0, nc, recompute, 0)

        def adjoint(k, carry):
            g_next, a_first_next = carry
            t0 = pl.multiple_of((nc - 1 - k) * tc, tc)
            a = a_ref[pl.ds(t0, tc), :]
            a_next = jnp.where(row == tc - 1, a_first_next, pltpu.roll(a, tc - 1, 0))
            gg = _scan_down(a_next, dhs_ref[pl.ds(t0, tc), :], g_next, row)
            h_prev = pltpu.roll(hpad_ref[pl.ds(t0, tc + SUBLANE), :], 1, 0)[SUBLANE:, :]
            xc = xc_ref[pl.ds(t0, tc), :]
            r = r_ref[pl.ds(t0, tc), :]
            i = i_ref[pl.ds(t0, tc), :]
            mult = mult_ref[pl.ds(t0, tc), :]
            d_mult = gg * i * xc
            d_i = gg * mult * xc
            d_xc = gg * mult * i
            d_la = gg * h_prev * a - d_mult * (a * a) / mult
            d_zr = (d_la * (-LRU_C * sp)) * r * (1.0 - r)
            d_zi = d_i * i * (1.0 - i)
            dlam_ref[...] += jnp.sum(d_la * (-LRU_C * r), axis=0, keepdims=True) * dsp_dlam
            dzr_b = d_zr.astype(BF16)
            dzi_b = d_zi.astype(BF16)
            xcb = xc.astype(BF16)
            d_xc = d_xc + _dot_nt(dzr_b, wa_b) + _dot_nt(dzi_b, wx_b)
            dwa_ref[...] += _dot_tn(xcb, dzr_b)
            dwx_ref[...] += _dot_tn(xcb, dzi_b)
            dba_ref[...] += jnp.sum(d_zr, axis=0, keepdims=True)
            dbx_ref[...] += jnp.sum(d_zi, axis=0, keepdims=True)
            dcb_ref[...] += jnp.sum(d_xc, axis=0, keepdims=True)
            dcpad_ref[pl.ds(t0, tc), :] = d_xc
            return _pick_row(gg, row, 0), _pick_row(a, row, 0)

        zero = jnp.zeros((1, cb), F32)
        lax.fori_loop(0, nc, adjoint, (zero, zero))

        def conv_back(ci, carry):
            t0 = pl.multiple_of(ci * tc, tc)
            dw = dcpad_ref[pl.ds(t0, tc + SUBLANE), :]
            xw = xpad_ref[pl.ds(t0, tc + SUBLANE), :]
            d_xc = dw[:tc, :]
            dxr = jnp.zeros((tc, cb), F32)
            for j in range(CONV_WIDTH):
                sh = CONV_WIDTH - 1 - j
                dsh = dw if sh == 0 else pltpu.roll(dw, tc + SUBLANE - sh, 0)
                dxr = dxr + dsh[:tc, :] * cw_ref[j:j + 1, :]
                xs = xw if sh == 0 else pltpu.roll(xw, sh, 0)
                dcw_ref[j:j + 1, :] += jnp.sum(d_xc * xs[SUBLANE:, :], axis=0, keepdims=True)
            dx_ref[pl.ds(t0, tc), :] = dxr.astype(BF16)
            return carry

        lax.fori_loop(0, nc, conv_back, 0)

    seq = lambda ni, bi: (bi, 0, ni)
    seqspec = pl.BlockSpec((None, s, cb), seq)
    cwspec = pl.BlockSpec((CONV_WIDTH, cb), lambda ni, bi: (0, ni))
    vec = pl.BlockSpec((1, cb), lambda ni, bi: (0, ni))
    wblk = pl.BlockSpec((None, cb, cb), lambda ni, bi: (ni, 0, 0))
    w = nblk * cb
    return pl.pallas_call(
        body, name=name, grid=(nblk, b),
        out_shape=(jax.ShapeDtypeStruct((b, s, w), BF16), jax.ShapeDtypeStruct((CONV_WIDTH, w), F32),
                   jax.ShapeDtypeStruct((1, w), F32), jax.ShapeDtypeStruct((nblk, cb, cb), F32),
                   jax.ShapeDtypeStruct((1, w), F32), jax.ShapeDtypeStruct((nblk, cb, cb), F32),
                   jax.ShapeDtypeStruct((1, w), F32), jax.ShapeDtypeStruct((1, w), F32)),
        in_specs=[seqspec, seqspec, seqspec, cwspec, vec, wblk, vec, wblk, vec, vec],
        out_specs=(seqspec, cwspec, vec, wblk, vec, wblk, vec, vec),
        scratch_shapes=[pltpu.VMEM((s + SUBLANE, cb), F32)] * 3 + [pltpu.VMEM((s, cb), F32)] * 5,
        compiler_params=_cp(("parallel", "arbitrary")),
    )(proj, hs, dhs, cw, cb_, wa, ba, wx, bx, lam)


def _adamw(w, g, m, v, name):
    shape = w.shape
    total = int(np.prod(shape))
    if w.ndim >= 2 and shape[-2] % SUBLANE == 0:
        rows, cols = shape[-2:]
    else:
        cols = 1024
        rows = -(-(-(-total // cols)) // SUBLANE) * SUBLANE
    lead = -(-total // (rows * cols))
    tr = _row_tile(rows, 512)
    pad = lead * rows * cols - total

    def flat(a):
        if pad:
            a = jnp.pad(a.reshape(-1), (0, pad))
        return a.reshape(lead, rows, cols)

    c1 = 1.0 - ADAM_B1 ** ADAM_STEP
    c2 = 1.0 - ADAM_B2 ** ADAM_STEP

    def body(w_ref, g_ref, m_ref, v_ref, d_ref, nm_ref, nv_ref):
        gv = g_ref[...]
        nm = ADAM_B1 * m_ref[...] + (1.0 - ADAM_B1) * gv
        nv = ADAM_B2 * v_ref[...] + (1.0 - ADAM_B2) * (gv * gv)
        nm_ref[...] = nm
        nv_ref[...] = nv
        d_ref[...] = -ADAM_LR * ((nm / c1) / (jnp.sqrt(nv / c2) + ADAM_EPS) + ADAM_WD * w_ref[...])

    spec = pl.BlockSpec((None, tr, cols), lambda l, i: (l, i, 0))
    shp = jax.ShapeDtypeStruct((lead, rows, cols), F32)
    outs = pl.pallas_call(
        body, name=name, grid=(lead, rows // tr), out_shape=(shp, shp, shp),
        in_specs=[spec] * 4, out_specs=(spec,) * 3,
        compiler_params=_cp(("parallel", "parallel")),
    )(flat(w), flat(g), flat(m), flat(v))
    if pad:
        return tuple(o.reshape(-1)[:total].reshape(shape) for o in outs)
    return tuple(o.reshape(shape) for o in outs)


def _to_heads(t, nh):
    b, s, _ = t.shape
    return t.reshape(b, s, nh, HEAD_DIM).transpose(0, 2, 1, 3)


def _stack_heads(t):
    b, s, _ = t.shape
    steps = s // (SWA_BPS * BLK)
    t = t.reshape(b, steps, SWA_BPS, BLK, A_KV_HEADS, A_GROUP, HEAD_DIM).transpose(0, 4, 1, 2, 5, 3, 6)
    return t.reshape(b, A_KV_HEADS, steps, SWA_BPS * A_GROUP * BLK, HEAD_DIM)


def _unstack_heads(t):
    b, hkv, steps, rows, hd = t.shape
    t = t.reshape(b, hkv, steps, SWA_BPS, A_GROUP, BLK, hd).transpose(0, 2, 3, 5, 1, 4, 6)
    return t.reshape(b, steps * SWA_BPS * BLK, hkv * A_GROUP * hd)


def _from_heads(t):
    b, nh, s, hd = t.shape
    return t.transpose(0, 2, 1, 3).reshape(b, s, nh * hd)


def _pad_rows(a, mult):
    r = a.shape[0]
    p = (-r) % mult
    return jnp.pad(a, ((0, p), (0, 0))) if p else a


def kernel(x, c, rel_bias, norm_g, ada_w, ada_b, attn_w_in, attn_sinks, attn_b_f, attn_w_out, lru_w_in, lru_conv_w, lru_conv_b, lru_w_a, lru_b_a, lru_w_x, lru_b_x, lru_lambda, lru_w_out, final_g, loss_target, m_rel_bias, m_norm_g, m_ada_w, m_ada_b, m_attn_w_in, m_attn_sinks, m_attn_b_f, m_attn_w_out, m_lru_w_in, m_lru_conv_w, m_lru_conv_b, m_lru_w_a, m_lru_b_a, m_lru_w_x, m_lru_b_x, m_lru_lambda, m_lru_w_out, m_final_g, v_rel_bias, v_norm_g, v_ada_w, v_ada_b, v_attn_w_in, v_attn_sinks, v_attn_b_f, v_attn_w_out, v_lru_w_in, v_lru_conv_w, v_lru_conv_b, v_lru_w_a, v_lru_b_a, v_lru_w_x, v_lru_b_x, v_lru_lambda, v_lru_w_out, v_final_g):
    bl, s, d = x.shape
    ix, iy, ic = lax.axis_index("x"), lax.axis_index("y"), lax.axis_index("c")
    chip = 2 * ix + iy
    me = 2 * chip + ic
    nb = s // BLK
    aw = A_Q_HEADS * HEAD_DIM
    akv = A_KV_HEADS * HEAD_DIM
    bw = B_HEADS * HEAD_DIM
    mixw = aw + bw
    qkv_w = aw + 2 * akv + 3 * bw
    n_in = attn_w_in.shape[2] * N_CHIP
    lw = lru_lambda.shape[1] * N_CHIP
    n0 = mixw + qkv_w + LANE

    rows_pad = -(-bl // SUBLANE) * SUBLANE
    vec_rows = jnp.concatenate([lru_conv_w[0], lru_conv_b, lru_b_a, lru_b_x, lru_lambda], axis=0)
    first = jnp.concatenate([_pad_rows(c, SUBLANE), jnp.pad(vec_rows, ((0, 0), (0, d - lw // N_CHIP)))], axis=0)
    first = _all_gather8(first, "gather_c", pltpu.VMEM).reshape(N_DEV, rows_pad + SUBLANE, d)
    c_all = first[:, :bl].reshape(N_DEV * bl, d)
    vec_all = first[:, rows_pad:, :lw // N_CHIP].reshape(N_CHIP, 2, SUBLANE, lw // N_CHIP)[:, 0]
    vec_all = vec_all.transpose(1, 0, 2).reshape(SUBLANE, lw)
    ncol = ada_w.shape[2]
    ada_w_l = lax.dynamic_index_in_dim(ada_w, ic, 0, keepdims=False)
    ada_b_l = lax.dynamic_slice(ada_b, (ic, chip * ncol), (1, ncol))
    mod_part = _ada_fwd(c_all, ada_w_l, ada_b_l, "ada_fwd")
    mod_all = _all_gather8(_pad_rows(mod_part, SUBLANE), "gather_mod", pltpu.VMEM)
    mrows = -(-(N_DEV * bl) // SUBLANE) * SUBLANE
    mod_all = mod_all.reshape(N_CHIP, 2, mrows, ncol)[:, :, :N_DEV * bl]
    mod_all = mod_all.transpose(1, 2, 0, 3).reshape(2, N_DEV * bl, N_CHIP * ncol)
    mod = lax.dynamic_slice_in_dim(mod_all, me * bl, bl, axis=1)
    shift = [mod[l, :, 0:d].reshape(bl, 1, d) for l in range(2)]
    scale = [mod[l, :, d:2 * d].reshape(bl, 1, d) for l in range(2)]
    gmod = [mod[l, :, 2 * d:3 * d].reshape(bl, 1, d) for l in range(2)]

    c_in0 = n_in // N_CHIP
    c_in1 = 2 * lw // N_CHIP
    assert c_in0 <= d and 2 * c_in1 == d
    r_in0, r_out0, r_in1, r_out1 = d // 2, mixw // N_CHIP // 2, d // 4, lw // N_CHIP // 2
    o_out0, o_in1, o_out1 = r_in0, r_in0 + r_out0, r_in0 + r_out0 + r_in1
    big_rows = o_out1 + r_out1

    def half_of(a, rows):
        return lax.dynamic_slice_in_dim(a, ic * rows, rows, axis=0)

    h_in1 = half_of(lru_w_in[0], r_in0).astype(BF16)
    my_half = jnp.concatenate([
        jnp.pad(half_of(attn_w_in[0], r_in0).astype(BF16), ((0, 0), (0, d - c_in0))),
        half_of(attn_w_out[0], r_out0).astype(BF16),
        jnp.concatenate([h_in1[:r_in1], h_in1[r_in1:]], axis=1),
        half_of(lru_w_out[0], r_out1).astype(BF16)], axis=0)
    gat = _all_gather8(my_half, "gather_weights", pltpu.HBM).reshape(N_CHIP, 2, big_rows, d)
    w_in0 = gat[:, :, :r_in0, :c_in0].transpose(1, 2, 0, 3).reshape(d, n_in)
    w_out0 = gat[:, :, o_out0:o_in1].reshape(mixw, d)
    w_in1 = gat[:, :, o_in1:o_out1].reshape(N_CHIP, 2, r_in1, 2, c_in1)
    w_in1 = w_in1.transpose(1, 3, 2, 0, 4).reshape(d, 2 * lw)
    w_out1 = gat[:, :, o_out1:].reshape(lw, d)
    w_cat0 = jnp.concatenate([w_in0[:, qkv_w + B_HEADS:], w_in0[:, :qkv_w + B_HEADS],
                              jnp.zeros((d, n0 - n_in), BF16)], axis=1)

    proj0, h0, zf = _norm_proj(x, norm_g[0:1], scale[0], shift[0], w_cat0, LANE, "norm_proj0")
    o_a = mixw
    aq = _stack_heads(proj0[:, :, o_a:o_a + aw].astype(BF16))
    ak = _to_heads(proj0[:, :, o_a + aw:o_a + aw + akv].astype(BF16), A_KV_HEADS)
    av = _to_heads(proj0[:, :, o_a + aw + akv:o_a + aw + 2 * akv].astype(BF16), A_KV_HEADS)
    o_b = o_a + aw + 2 * akv
    fox_blks = (o_b // LANE, (o_b + bw) // LANE, (o_b + 2 * bw) // LANE)
    bucket_np, valid_np = _rel_buckets()
    bucket = jnp.asarray(bucket_np)
    bias = _swa_bias(rel_bias.T, bucket, jnp.asarray(valid_np), "swa_bias")
    bias = bias.reshape(A_KV_HEADS, A_GROUP * BLK, 2 * BLK)
    bias = jnp.concatenate([jnp.pad(bias, ((0, 0), (0, 0), (blk * BLK, (SWA_BPS - 1 - blk) * BLK)),
                                    constant_values=NEG) for blk in range(SWA_BPS)], axis=1)
    sinks = jnp.repeat(attn_sinks[0].reshape(A_KV_HEADS, A_GROUP), BLK, axis=1).reshape(A_KV_HEADS, A_GROUP * BLK, 1)
    sinks = jnp.tile(sinks, (1, SWA_BPS, 1))
    a_out, a_lse = _swa_fwd(aq, ak, av, bias, sinks, "swa_fwd")
    bf_pad = jnp.pad(attn_b_f, ((0, 0), (0, LANE - B_HEADS)))
    fsum = _fox_decay(zf, bf_pad, "fox_decay")
    fh = fsum[:, :, :B_HEADS].transpose(0, 2, 1)
    fcol = fh.reshape(bl, B_HEADS, s, 1)
    frow = fh.reshape(bl, B_HEADS, 1, s)
    fbq = min(s, FOX_BQ)
    frowb = fh.reshape(bl, B_HEADS, s // fbq, 1, fbq)
    b_out, b_lse = _fox_fwd(proj0, *fox_blks, fcol, frow, "fox_fwd")
    lserowb = b_lse.reshape(bl, B_HEADS, s // fbq, 1, fbq)
    mix0 = [_unstack_heads(a_out), b_out]
    x1, o0 = _gate_outproj(mix0, proj0, 0, w_out0, x, gmod[0], "gate_outproj0")

    proj1, h1 = _norm_proj(x1, norm_g[1:2], scale[1], shift[1], w_in1, 0, "norm_proj1")
    cw_f, cb_f, ba_f, bx_f, lam_f = vec_all[0:4], vec_all[4:5], vec_all[5:6], vec_all[6:7], vec_all[7:8]
    hs = _lru_fwd(proj1, cw_f, cb_f, lru_w_a[0], ba_f, lru_w_x[0], bx_f, lam_f, "lru_fwd")
    x2, o1 = _gate_outproj([hs], proj1, 1, w_out1, x1, gmod[1], "gate_outproj1")

    loss_vec, dx2, g_final = _final_loss(x2, final_g.reshape(1, d), loss_target, "final_loss")
    loss = lax.psum(loss_vec[0, 0], ("x", "y", "c"))

    dhs, dgate1, do1, y1, dgm1 = _bwd_out(dx2, gmod[1], o1, [hs], proj1, 1, w_out1.T, F32, "bwd_out1")
    g_w_out1 = _matmul_tn(y1, [do1], "grad_w_out1")
    (dxr, g_cw, g_cb, g_wa, g_ba, g_wx, g_bx, g_lam) = _lru_bwd(
        proj1, hs, dhs, cw_f, cb_f, lru_w_a[0], ba_f, lru_w_x[0], bx_f, lam_f, "lru_bwd")
    dproj1 = [dxr, dgate1]
    g_w_in1 = _matmul_tn(h1, dproj1, "grad_w_in1")
    dx1, dsh1, dsc1, g_ng1 = _bwd_in(dproj1, w_in1.T, x1, norm_g[1:2], scale[1], dx2, "bwd_in1")

    dmix0, dgate0, do0, y0, dgm0 = _bwd_out(dx1, gmod[0], o0, mix0, proj0, 0, w_out0.T, BF16, "bwd_out0")
    g_w_out0 = _matmul_tn(y0, [do0], "grad_w_out0")
    da_out = _stack_heads(dmix0[:, :, :aw].astype(BF16))
    daq, dak, dav, dbias, dsink = _swa_bwd(aq, ak, av, bias, sinks, da_out, a_lse, "swa_bwd")
    dbq, dbk, dbv, dfrow = _fox_bwd(proj0, *fox_blks, dmix0, aw // LANE, fcol, frow, frowb, b_lse, lserowb,
                                    "fox_bwd")
    df = dfrow.reshape(bl, B_HEADS, s).transpose(0, 2, 1)
    df = jnp.pad(df, ((0, 0), (0, 0), (0, LANE - B_HEADS)))
    dzf, g_bf = _fox_dgate(df, zf, bf_pad, B_HEADS, "fox_dgate")
    dproj0 = ([dgate0, _unstack_heads(daq), _from_heads(dak), _from_heads(dav)]
              + [dbq, dbk, dbv, dzf.astype(BF16)])
    g_w_cat0 = _matmul_tn(h0, dproj0, "grad_w_in0")
    g_w_in0 = jnp.concatenate([g_w_cat0[:, mixw:mixw + qkv_w + B_HEADS], g_w_cat0[:, :mixw]], axis=1)
    dx0, dsh0, dsc0, g_ng0 = _bwd_in(dproj0, w_cat0.T, x, norm_g[0:1], scale[0], dx1, "bwd_in0")
    dbias = sum(dbias[:, :, blk * A_GROUP * BLK:(blk + 1) * A_GROUP * BLK, blk * BLK:(blk + 2) * BLK]
                for blk in range(SWA_BPS))
    g_relb, g_sink = _swa_small_grads(dbias.reshape(bl, A_Q_HEADS, BLK, 2 * BLK),
                                      dsink.reshape(bl, A_Q_HEADS, 1, LANE), bucket, "swa_small_grads")

    dmod = jnp.concatenate([jnp.concatenate([dsh0, dsc0, dgm0], axis=-1),
                            jnp.concatenate([dsh1, dsc1, dgm1], axis=-1)], axis=1)
    dmod_rows = _pad_rows(dmod.reshape(bl * 6, d), SUBLANE)

    tail = jnp.concatenate([g_relb[:, :, 0].T.reshape(-1), g_sink[:, 0, 0], g_bf[0, :B_HEADS]])
    n_relb = REL_BUCKETS * A_Q_HEADS
    small_rows = [g_wa.reshape(-1, d), g_wx.reshape(-1, d), g_ng0, g_ng1, g_final, g_cw, g_cb, g_ba, g_bx, g_lam,
                  jnp.pad(tail, (0, d - tail.shape[0])).reshape(1, d)]
    small_counts = [r.shape[0] for r in small_rows]
    piece_rows = -(-(-(-sum(small_counts) // N_DEV)) // SUBLANE) * SUBLANE
    small_2d = jnp.concatenate(small_rows, axis=0)
    small_2d = jnp.pad(small_2d, ((0, N_DEV * piece_rows - small_2d.shape[0]), (0, 0)))
    small_pieces = small_2d.reshape(N_CHIP, 2, piece_rows, d)
    p_in0 = jnp.pad(g_w_in0.reshape(2, r_in0, N_CHIP, c_in0).transpose(2, 0, 1, 3),
                    ((0, 0), (0, 0), (0, 0), (0, d - c_in0)))
    p_in1 = g_w_in1.reshape(2, 2, r_in1, N_CHIP, c_in1).transpose(3, 0, 2, 1, 4).reshape(N_CHIP, 2, r_in1, d)
    pieces = jnp.concatenate([p_in0, g_w_out0.reshape(N_CHIP, 2, r_out0, d), p_in1,
                              g_w_out1.reshape(N_CHIP, 2, r_out1, d), small_pieces], axis=2)
    theirs = _sibling_push(pieces, True, "push_sibling_halves")
    partial = _pair_sum(jnp.reshape(ic, (1,)).astype(jnp.int32), pieces, theirs, "sum_chip")
    slots = _chip_all_to_all(partial, "exchange_grads")
    reduced = _sum_slots(slots, "sum_grads")
    mine_big = reduced[:big_rows]
    other_big = _sibling_push(mine_big[None], False, "swap_halves")[0]
    both = jnp.stack([jnp.where(ic == 0, mine_big, other_big), jnp.where(ic == 0, other_big, mine_big)])
    g_big = [both[:, :r_in0, :c_in0].reshape(d, c_in0),
             both[:, o_out0:o_in1].reshape(2 * r_out0, d),
             both[:, o_in1:o_out1].reshape(2, r_in1, 2, c_in1).transpose(0, 2, 1, 3).reshape(d, c_in1),
             both[:, o_out1:].reshape(2 * r_out1, d)]
    last = _all_gather8(jnp.concatenate([reduced[big_rows:], dmod_rows], axis=0), "gather_small_grads", pltpu.VMEM)
    last = last.reshape(N_DEV, piece_rows + dmod_rows.shape[0], d)
    small_all = last[:, :piece_rows].reshape(N_DEV * piece_rows, d)
    dmod_all = last[:, piece_rows:piece_rows + bl * 6].reshape(N_DEV * bl, 6 * d)
    dmod_chip = lax.dynamic_slice_in_dim(dmod_all.reshape(N_DEV * bl, 2, 3 * d), chip * ncol, ncol, axis=2)
    g_ada_w, g_ada_b = _ada_bwd(c_all, dmod_chip.transpose(1, 0, 2), dmod_all, "ada_bwd")
    g_ada_b = g_ada_b.reshape(2, 3 * d)
    g_small, off = [], 0
    for cnt in small_counts:
        g_small.append(small_all[off:off + cnt])
        off += cnt
    g_w_a, g_w_x = g_small[0].reshape(lru_w_a.shape[1:]), g_small[1].reshape(lru_w_x.shape[1:])
    g_norm_g = jnp.concatenate(g_small[2:4], axis=0)
    g_fin, g_cw_r, g_cb_r, g_ba_r, g_bx_r, g_lam_r = g_small[4:10]
    tail = g_small[10][0]
    g_rel_bias = tail[:n_relb].reshape(REL_BUCKETS, A_Q_HEADS)
    g_sinks, g_b_f = tail[n_relb:n_relb + A_Q_HEADS], tail[n_relb + A_Q_HEADS:n_relb + A_Q_HEADS + B_HEADS]
    cw4 = lw // N_CHIP

    def my_cols(a):
        return lax.dynamic_slice_in_dim(a, chip * cw4, cw4, axis=1)

    grads = {
        "rel_bias": g_rel_bias, "norm_g": g_norm_g, "ada_w": g_ada_w, "ada_b": g_ada_b,
        "attn_w_in": g_big[0][None], "attn_sinks": g_sinks[None], "attn_b_f": g_b_f[None],
        "attn_w_out": g_big[1][None], "lru_w_in": g_big[2][None], "lru_conv_w": my_cols(g_cw_r)[None],
        "lru_conv_b": my_cols(g_cb_r), "lru_w_a": g_w_a[None], "lru_b_a": my_cols(g_ba_r),
        "lru_w_x": g_w_x[None], "lru_b_x": my_cols(g_bx_r), "lru_lambda": my_cols(g_lam_r),
        "lru_w_out": g_big[3][None], "final_g": g_fin.reshape(d),
    }
    weights = dict(rel_bias=rel_bias, norm_g=norm_g, ada_w=ada_w, ada_b=ada_b, attn_w_in=attn_w_in,
                   attn_sinks=attn_sinks, attn_b_f=attn_b_f, attn_w_out=attn_w_out, lru_w_in=lru_w_in,
                   lru_conv_w=lru_conv_w, lru_conv_b=lru_conv_b, lru_w_a=lru_w_a, lru_b_a=lru_b_a,
                   lru_w_x=lru_w_x, lru_b_x=lru_b_x, lru_lambda=lru_lambda, lru_w_out=lru_w_out, final_g=final_g)
    moms = dict(rel_bias=(m_rel_bias, v_rel_bias), norm_g=(m_norm_g, v_norm_g), ada_w=(m_ada_w, v_ada_w),
                ada_b=(m_ada_b, v_ada_b), attn_w_in=(m_attn_w_in, v_attn_w_in),
                attn_sinks=(m_attn_sinks, v_attn_sinks), attn_b_f=(m_attn_b_f, v_attn_b_f),
                attn_w_out=(m_attn_w_out, v_attn_w_out), lru_w_in=(m_lru_w_in, v_lru_w_in),
                lru_conv_w=(m_lru_conv_w, v_lru_conv_w), lru_conv_b=(m_lru_conv_b, v_lru_conv_b),
                lru_w_a=(m_lru_w_a, v_lru_w_a), lru_b_a=(m_lru_b_a, v_lru_b_a), lru_w_x=(m_lru_w_x, v_lru_w_x),
                lru_b_x=(m_lru_b_x, v_lru_b_x), lru_lambda=(m_lru_lambda, v_lru_lambda),
                lru_w_out=(m_lru_w_out, v_lru_w_out), final_g=(m_final_g, v_final_g))
    names = list(weights)
    big_names = [n for n in names if weights[n].size >= 65536]
    small_names = [n for n in names if weights[n].size < 65536]
    delta, new_m, new_v = {}, {}, {}
    for n in big_names:
        delta[n], new_m[n], new_v[n] = _adamw(weights[n], grads[n].reshape(weights[n].shape),
                                              moms[n][0], moms[n][1], "adamw_" + n)
    cat = lambda arrs: jnp.concatenate([a.reshape(-1) for a in arrs])
    sd, sm, sv = _adamw(cat([weights[n] for n in small_names]), cat([grads[n] for n in small_names]),
                        cat([moms[n][0] for n in small_names]), cat([moms[n][1] for n in small_names]),
                        "adamw_small")
    off = 0
    for n in small_names:
        sz = weights[n].size
        shp = weights[n].shape
        delta[n], new_m[n], new_v[n] = (sd[off:off + sz].reshape(shp), sm[off:off + sz].reshape(shp),
                                        sv[off:off + sz].reshape(shp))
        off += sz
    out_grads = [grads[n].reshape(weights[n].shape) for n in names]
    return (loss, dx0, *out_grads, *[delta[n] for n in names], *[new_m[n] for n in names],
            *[new_v[n] for n in names])
```

```python
import math

import numpy as np
import jax
import jax.numpy as jnp
from jax import lax
from jax.experimental import pallas as pl
from jax.experimental.pallas import tpu as pltpu

F32 = jnp.float32
BF16 = jnp.bfloat16
MESH = pl.DeviceIdType.MESH

N_DEV = 8
N_CHIP = 4
HEAD_DIM = 64
BLK = 128
A_Q_HEADS = 8
A_KV_HEADS = 2
A_GROUP = A_Q_HEADS // A_KV_HEADS
B_HEADS = 8
REL_BUCKETS = 32
REL_MAX_EXACT = 16
REL_MAX_DIST = 128
LRU_C = 8.0
CONV_WIDTH = 4
EPS = 1e-6
NEG = -1e30
SCALE = HEAD_DIM ** -0.5
LANE = 128
SUBLANE = 8
VMEM_LIMIT = 56 * 1024 * 1024
SCAN_CHUNK = 512
ROW_TILE = 512
SWA_BPS = 1
FOX_BQ = 512
ADAM_LR = 0.001
ADAM_B1 = 0.9
ADAM_B2 = 0.999
ADAM_EPS = 1e-08
ADAM_WD = 0.01
ADAM_STEP = 10
HI = lax.Precision.HIGHEST


def _cp(sem=None):
    return pltpu.CompilerParams(dimension_semantics=sem, vmem_limit_bytes=VMEM_LIMIT)


def _dot(a, b):
    return jnp.dot(a, b, preferred_element_type=F32)


def _dot_nt(a, b):
    return lax.dot_general(a, b, (((1,), (1,)), ((), ())), preferred_element_type=F32)


def _dot_tn(a, b):
    return lax.dot_general(a, b, (((0,), (0,)), ((), ())), preferred_element_type=F32)


def _sigmoid(z):
    return 1.0 / (1.0 + jnp.exp(-z))


def _row_tile(rows, cap):
    if rows <= cap:
        return rows
    best = SUBLANE
    t = SUBLANE
    while t <= cap:
        if rows % t == 0:
            best = t
        t += SUBLANE
    return best


def _all_gather8(x_shard, name, space):
    m_per, n = x_shard.shape
    n_own = 8 if (space == pltpu.HBM and m_per % 128 == 0) else 1
    own_rows = m_per // n_own

    def body(x_ref, out_ref, send_sems, recv_sems, local_sems):
        x, y, c = lax.axis_index("x"), lax.axis_index("y"), lax.axis_index("c")
        me, sibling = (x, y, c), (x, y, 1 - c)
        chips = [(1 - x, y), (x, 1 - y), (1 - x, 1 - y)]

        def rows(px, py, pc):
            return out_ref.at[pl.ds((4 * px + 2 * py + pc) * m_per, m_per), :]

        def copy(k, block, to, src=None):
            return pltpu.make_async_remote_copy(
                src_ref=rows(*block) if src is None else src, dst_ref=rows(*block),
                send_sem=send_sems.at[k], recv_sem=recv_sems.at[k], device_id=to, device_id_type=MESH)

        base = (4 * x + 2 * y + c) * m_per
        mine = [pltpu.make_async_copy(x_ref.at[pl.ds(i * own_rows, own_rows), :],
                                      out_ref.at[pl.ds(base + i * own_rows, own_rows), :], local_sems.at[i])
                for i in range(n_own)]
        for cp in mine:
            cp.start()
        first = [copy(0, me, sibling, src=x_ref)]
        first += [copy(1 + j, me, (*chip, c), src=x_ref) for j, chip in enumerate(chips)]
        for cp in first:
            cp.start()
        passed = [copy(4 + j, (*chip, c), sibling) for j, chip in enumerate(chips)]
        for j, chip in enumerate(chips):
            copy(1 + j, (*chip, c), me).wait_recv()
            passed[j].start()
        copy(0, sibling, me).wait_recv()
        for j, chip in enumerate(chips):
            copy(4 + j, (*chip, 1 - c), me).wait_recv()
        for cp in first + passed:
            cp.wait_send()
        for cp in mine:
            cp.wait()

    return pl.pallas_call(
        body, name=name,
        out_shape=jax.ShapeDtypeStruct((N_DEV * m_per, n), x_shard.dtype),
        in_specs=[pl.BlockSpec(memory_space=space)],
        out_specs=pl.BlockSpec(memory_space=space),
        scratch_shapes=[pltpu.SemaphoreType.DMA((7,)), pltpu.SemaphoreType.DMA((7,)),
                        pltpu.SemaphoreType.DMA((n_own,))],
        compiler_params=pltpu.CompilerParams(vmem_limit_bytes=VMEM_LIMIT),
    )(x_shard)


def _sibling_push(blocks, pick_other, name):
    nblk = blocks.shape[0]
    m, n = blocks.shape[-2:]

    def body(x_ref, out_ref, send_sems, recv_sems):
        x, y, c = lax.axis_index("x"), lax.axis_index("y"), lax.axis_index("c")
        copies = []
        for k in range(nblk):
            src = x_ref.at[k, 1 - c] if pick_other else x_ref.at[k]
            copies.append(pltpu.make_async_remote_copy(
                src_ref=src, dst_ref=out_ref.at[k], send_sem=send_sems.at[k], recv_sem=recv_sems.at[k],
                device_id=(x, y, 1 - c), device_id_type=MESH))
        for cp in copies:
            cp.start()
        for cp in copies:
            cp.wait_recv()
        for cp in copies:
            cp.wait_send()

    hbm = pl.BlockSpec(memory_space=pltpu.HBM)
    return pl.pallas_call(
        body, name=name,
        out_shape=jax.ShapeDtypeStruct((nblk, m, n), blocks.dtype),
        in_specs=[hbm], out_specs=hbm,
        scratch_shapes=[pltpu.SemaphoreType.DMA((nblk,)), pltpu.SemaphoreType.DMA((nblk,))],
    )(blocks)


def _chip_all_to_all(parts, name):
    _, m, n = parts.shape

    def body(x_ref, out_ref, send_sems, recv_sems, local_sem):
        x, y, c = lax.axis_index("x"), lax.axis_index("y"), lax.axis_index("c")
        me = 2 * x + y
        mine = pltpu.make_async_copy(x_ref.at[me], out_ref.at[me], local_sem)
        mine.start()
        copies = []
        for k in range(1, N_CHIP):
            px, py = x ^ ((k >> 1) & 1), y ^ (k & 1)
            copies.append(pltpu.make_async_remote_copy(
                src_ref=x_ref.at[2 * px + py], dst_ref=out_ref.at[me],
                send_sem=send_sems.at[k - 1], recv_sem=recv_sems.at[k - 1],
                device_id=(px, py, c), device_id_type=MESH))
        for cp in copies:
            cp.start()
        for cp in copies:
            cp.wait_recv()
        for cp in copies:
            cp.wait_send()
        mine.wait()

    hbm = pl.BlockSpec(memory_space=pltpu.HBM)
    return pl.pallas_call(
        body, name=name,
        out_shape=jax.ShapeDtypeStruct(parts.shape, parts.dtype),
        in_specs=[hbm], out_specs=hbm,
        scratch_shapes=[pltpu.SemaphoreType.DMA((N_CHIP - 1,)), pltpu.SemaphoreType.DMA((N_CHIP - 1,)),
                        pltpu.SemaphoreType.DMA],
    )(parts)


def _pair_sum(core, pieces, theirs, name):
    nblk, _, m, n = pieces.shape
    tr = _row_tile(m, 536)

    def body(c_ref, p_ref, t_ref, o_ref):
        o_ref[...] = (p_ref[...] + t_ref[...]).astype(BF16)

    return pl.pallas_call(
        body, name=name,
        grid_spec=pltpu.PrefetchScalarGridSpec(
            num_scalar_prefetch=1, grid=(nblk, m // tr),
            in_specs=[pl.BlockSpec((None, None, tr, n), lambda k, i, c_ref: (k, c_ref[0], i, 0)),
                      pl.BlockSpec((None, tr, n), lambda k, i, c_ref: (k, i, 0))],
            out_specs=pl.BlockSpec((None, tr, n), lambda k, i, c_ref: (k, i, 0))),
        out_shape=jax.ShapeDtypeStruct((nblk, m, n), BF16),
        compiler_params=_cp(("parallel", "parallel")),
    )(core, pieces, theirs)


def _sum_slots(slots, name):
    k, m, n = slots.shape
    tr = _row_tile(m, 536)

    def body(s_ref, o_ref):
        acc = s_ref[0].astype(F32)
        for j in range(1, k):
            acc = acc + s_ref[j].astype(F32)
        o_ref[...] = acc

    return pl.pallas_call(
        body, name=name, grid=(m // tr,),
        out_shape=jax.ShapeDtypeStruct((m, n), F32),
        in_specs=[pl.BlockSpec((k, tr, n), lambda i: (0, i, 0))],
        out_specs=pl.BlockSpec((tr, n), lambda i: (i, 0)),
        compiler_params=_cp(("parallel",)),
    )(slots)


def _ada_fwd(c_all, w, b, name):
    r, _ = c_all.shape
    n = w.shape[1]

    def body(c_ref, w_ref, b_ref, o_ref):
        cv = c_ref[...]
        act = cv * _sigmoid(cv)
        o_ref[...] = jnp.dot(act, w_ref[...], precision=HI, preferred_element_type=F32) + b_ref[...]

    return pl.pallas_call(body, name=name, out_shape=jax.ShapeDtypeStruct((r, n), F32),
                          compiler_params=_cp())(c_all, w, b)


def _ada_bwd(c_all, dmod_chip, dmod_all, name):
    r, d = c_all.shape
    nl, _, n = dmod_chip.shape

    def body(c_ref, dm_ref, da_ref, gw_ref, gb_ref):
        cv = c_ref[...]
        act = cv * _sigmoid(cv)
        for l in range(nl):
            gw_ref[l] = lax.dot_general(act, dm_ref[l], (((0,), (0,)), ((), ())), precision=HI,
                                        preferred_element_type=F32)
        gb_ref[...] = jnp.sum(da_ref[...], axis=0, keepdims=True)

    return pl.pallas_call(
        body, name=name,
        out_shape=(jax.ShapeDtypeStruct((nl, d, n), F32), jax.ShapeDtypeStruct((1, dmod_all.shape[1]), F32)),
        compiler_params=_cp())(c_all, dmod_chip, dmod_all)


def _norm_proj(x, g, scale, shift, w, f32_cols, name):
    b, s, d = x.shape
    n = w.shape[1]
    tm = min(s, ROW_TILE)

    def body(x_ref, g_ref, sc_ref, sh_ref, w_ref, proj_ref, h_ref, *aux_ref):
        xv = x_ref[...]
        rstd = lax.rsqrt(jnp.mean(xv * xv, axis=-1, keepdims=True) + EPS)
        h = (xv * rstd) * g_ref[...] * (1.0 + sc_ref[...]) + sh_ref[...]
        hb = h.astype(BF16)
        h_ref[...] = hb
        proj = _dot(hb, w_ref[...])
        proj_ref[...] = proj.astype(BF16)
        if f32_cols:
            aux_ref[0][...] = proj[:, n - f32_cols:]

    row = lambda i, j: (i, j, 0)
    out_shape = [jax.ShapeDtypeStruct((b, s, n), BF16), jax.ShapeDtypeStruct((b, s, d), BF16)]
    out_specs = [pl.BlockSpec((None, tm, n), row), pl.BlockSpec((None, tm, d), row)]
    if f32_cols:
        out_shape.append(jax.ShapeDtypeStruct((b, s, f32_cols), F32))
        out_specs.append(pl.BlockSpec((None, tm, f32_cols), row))
    return pl.pallas_call(
        body, name=name, grid=(b, s // tm),
        out_shape=tuple(out_shape),
        in_specs=[pl.BlockSpec((None, tm, d), row),
                  pl.BlockSpec((1, d), lambda i, j: (0, 0)),
                  pl.BlockSpec((None, 1, d), lambda i, j: (i, 0, 0)),
                  pl.BlockSpec((None, 1, d), lambda i, j: (i, 0, 0)),
                  pl.BlockSpec((d, n), lambda i, j: (0, 0))],
        out_specs=tuple(out_specs),
        compiler_params=_cp(("parallel", "parallel")),
    )(x, g, scale, shift, w)


def _cat_refs(refs):
    vals = [r[...] for r in refs]
    return vals[0] if len(vals) == 1 else jnp.concatenate(vals, axis=-1)


def _gate_outproj(mix_parts, proj, gate_blk, w_out, x, gmod, name):
    b, s, _ = x.shape
    wd, d = w_out.shape
    tm = min(s, ROW_TILE)
    npart = len(mix_parts)

    def body(*refs):
        mix_refs = refs[:npart]
        gate_ref, w_ref, x_ref, gm_ref, xo_ref, o_ref = refs[npart:]
        gt = gate_ref[...].astype(F32)
        y = (_cat_refs(mix_refs) * (gt * _sigmoid(gt))).astype(BF16)
        o = _dot(y, w_ref[...])
        o_ref[...] = o.astype(BF16)
        xo_ref[...] = x_ref[...] + gm_ref[...] * o

    return pl.pallas_call(
        body, name=name, grid=(b, s // tm),
        out_shape=(jax.ShapeDtypeStruct((b, s, d), F32), jax.ShapeDtypeStruct((b, s, d), BF16)),
        in_specs=[pl.BlockSpec((None, tm, p.shape[2]), lambda i, j: (i, j, 0)) for p in mix_parts] + [
                  pl.BlockSpec((None, tm, wd), lambda i, j: (i, j, gate_blk)),
                  pl.BlockSpec((wd, d), lambda i, j: (0, 0)),
                  pl.BlockSpec((None, tm, d), lambda i, j: (i, j, 0)),
                  pl.BlockSpec((None, 1, d), lambda i, j: (i, 0, 0))],
        out_specs=(pl.BlockSpec((None, tm, d), lambda i, j: (i, j, 0)),
                   pl.BlockSpec((None, tm, d), lambda i, j: (i, j, 0))),
        compiler_params=_cp(("parallel", "parallel")),
    )(*mix_parts, proj, w_out, x, gmod)


def _final_loss(x, g, target, name):
    b, s, d = x.shape
    tm = min(s, ROW_TILE)

    def body(x_ref, g_ref, t_ref, loss_ref, dx_ref, dg_ref):
        first = jnp.logical_and(pl.program_id(0) == 0, pl.program_id(1) == 0)

        @pl.when(first)
        def _():
            loss_ref[...] = jnp.zeros_like(loss_ref)
            dg_ref[...] = jnp.zeros_like(dg_ref)

        xv = x_ref[...]
        gv = g_ref[...]
        rstd = lax.rsqrt(jnp.mean(xv * xv, axis=-1, keepdims=True) + EPS)
        xhat = xv * rstd
        err = xhat * gv - t_ref[...]
        row = jnp.mean(err * err, axis=-1, keepdims=True)
        loss_ref[...] += 0.5 * jnp.sum(row, axis=0, keepdims=True)
        dy = err * (1.0 / d)
        dg_ref[...] += jnp.sum(dy * xhat, axis=0, keepdims=True)
        dxh = dy * gv
        dx_ref[...] = rstd * (dxh - xhat * jnp.mean(dxh * xhat, axis=-1, keepdims=True))

    return pl.pallas_call(
        body, name=name, grid=(b, s // tm),
        out_shape=(jax.ShapeDtypeStruct((1, LANE), F32), jax.ShapeDtypeStruct((b, s, d), F32),
                   jax.ShapeDtypeStruct((1, d), F32)),
        in_specs=[pl.BlockSpec((None, tm, d), lambda i, j: (i, j, 0)),
                  pl.BlockSpec((1, d), lambda i, j: (0, 0)),
                  pl.BlockSpec((None, tm, d), lambda i, j: (i, j, 0))],
        out_specs=(pl.BlockSpec((1, LANE), lambda i, j: (0, 0)),
                   pl.BlockSpec((None, tm, d), lambda i, j: (i, j, 0)),
                   pl.BlockSpec((1, d), lambda i, j: (0, 0))),
        compiler_params=_cp(("arbitrary", "arbitrary")),
    )(x, g, target)


def _bwd_out(dxo, gmod, o, mix_parts, proj, gate_blk, w_out, dmix_dtype, name):
    b, s, d = dxo.shape
    wd = w_out.shape[0]
    tm = min(s, ROW_TILE)
    npart = len(mix_parts)

    def body(dx_ref, gm_ref, o_ref, *refs):
        mix_refs = refs[:npart]
        gate_ref, wt_ref, dmix_ref, dgate_ref, do_ref, y_ref, dgm_ref = refs[npart:]

        @pl.when(pl.program_id(1) == 0)
        def _():
            dgm_ref[...] = jnp.zeros_like(dgm_ref)

        dx = dx_ref[...]
        dgm_ref[...] += jnp.sum(dx * o_ref[...].astype(F32), axis=0, keepdims=True)
        dob = (gm_ref[...] * dx).astype(BF16)
        do_ref[...] = dob
        dy = _dot_nt(dob, wt_ref[...])
        gt = gate_ref[...].astype(F32)
        sg = _sigmoid(gt)
        silu = gt * sg
        mx = _cat_refs(mix_refs)
        y_ref[...] = (mx * silu).astype(BF16)
        dmix_ref[...] = (dy * silu).astype(dmix_dtype)
        dgate_ref[...] = (dy * mx * (sg * (1.0 + gt * (1.0 - sg)))).astype(BF16)

    row = lambda i, j: (i, j, 0)
    return pl.pallas_call(
        body, name=name, grid=(b, s // tm),
        out_shape=(jax.ShapeDtypeStruct((b, s, wd), dmix_dtype), jax.ShapeDtypeStruct((b, s, wd), BF16),
                   jax.ShapeDtypeStruct((b, s, d), BF16), jax.ShapeDtypeStruct((b, s, wd), BF16),
                   jax.ShapeDtypeStruct((b, 1, d), F32)),
        in_specs=[pl.BlockSpec((None, tm, d), row),
                  pl.BlockSpec((None, 1, d), lambda i, j: (i, 0, 0)),
                  pl.BlockSpec((None, tm, d), row)] + [
                  pl.BlockSpec((None, tm, p.shape[2]), row) for p in mix_parts] + [
                  pl.BlockSpec((None, tm, wd), lambda i, j: (i, j, gate_blk)),
                  pl.BlockSpec((wd, d), lambda i, j: (0, 0))],
        out_specs=(pl.BlockSpec((None, tm, wd), row), pl.BlockSpec((None, tm, wd), row),
                   pl.BlockSpec((None, tm, d), row), pl.BlockSpec((None, tm, wd), row),
                   pl.BlockSpec((None, 1, d), lambda i, j: (i, 0, 0))),
        compiler_params=_cp(("parallel", "arbitrary")),
    )(dxo, gmod, o, *mix_parts, proj, w_out)


def _bwd_in(dproj_parts, w_in, x, g, scale, dxo, name):
    b, s, d = x.shape
    n = w_in.shape[1]
    tm = min(s, ROW_TILE)
    npart = len(dproj_parts)

    def body(*refs):
        dp_refs = refs[:npart]
        wt_ref, x_ref, g_ref, sc_ref, dxo_ref, dx_ref, dsh_ref, dsc_ref, dg_ref = refs[npart:]

        @pl.when(jnp.logical_and(pl.program_id(0) == 0, pl.program_id(1) == 0))
        def _():
            dg_ref[...] = jnp.zeros_like(dg_ref)

        @pl.when(pl.program_id(1) == 0)
        def _():
            dsh_ref[...] = jnp.zeros_like(dsh_ref)
            dsc_ref[...] = jnp.zeros_like(dsc_ref)

        dh = _dot_nt(_cat_refs(dp_refs), wt_ref[...])
        xv = x_ref[...]
        gv = g_ref[...]
        one_sc = 1.0 + sc_ref[...]
        rstd = lax.rsqrt(jnp.mean(xv * xv, axis=-1, keepdims=True) + EPS)
        xhat = xv * rstd
        dsh_ref[...] += jnp.sum(dh, axis=0, keepdims=True)
        dsc_ref[...] += jnp.sum(dh * (xhat * gv), axis=0, keepdims=True)
        dhs = dh * one_sc
        dg_ref[...] += jnp.sum(dhs * xhat, axis=0, keepdims=True)
        dxh = dhs * gv
        dx_ref[...] = dxo_ref[...] + rstd * (dxh - xhat * jnp.mean(dxh * xhat, axis=-1, keepdims=True))

    row = lambda i, j: (i, j, 0)
    per_b = lambda i, j: (i, 0, 0)
    return pl.pallas_call(
        body, name=name, grid=(b, s // tm),
        out_shape=(jax.ShapeDtypeStruct((b, s, d), F32), jax.ShapeDtypeStruct((b, 1, d), F32),
                   jax.ShapeDtypeStruct((b, 1, d), F32), jax.ShapeDtypeStruct((1, d), F32)),
        in_specs=[pl.BlockSpec((None, tm, p.shape[2]), row) for p in dproj_parts] + [
                  pl.BlockSpec((d, n), lambda i, j: (0, 0)),
                  pl.BlockSpec((None, tm, d), row),
                  pl.BlockSpec((1, d), lambda i, j: (0, 0)),
                  pl.BlockSpec((None, 1, d), per_b),
                  pl.BlockSpec((None, tm, d), row)],
        out_specs=(pl.BlockSpec((None, tm, d), row), pl.BlockSpec((None, 1, d), per_b),
                   pl.BlockSpec((None, 1, d), per_b), pl.BlockSpec((1, d), lambda i, j: (0, 0))),
        compiler_params=_cp(("arbitrary", "arbitrary")),
    )(*dproj_parts, w_in, x, g, scale, dxo)


def _matmul_tn(a, b_parts, name):
    bsz, s, m = a.shape
    n = sum(p.shape[2] for p in b_parts)
    tk = next(c for c in (512, 256, 128) if s % c == 0)
    npart = len(b_parts)

    def body(a_ref, *refs):
        b_refs, o_ref = refs[:npart], refs[npart]

        @pl.when(jnp.logical_and(pl.program_id(0) == 0, pl.program_id(1) == 0))
        def _():
            o_ref[...] = jnp.zeros_like(o_ref)

        o_ref[...] += _dot_tn(a_ref[...], _cat_refs(b_refs))

    row = lambda i, k: (i, k, 0)
    return pl.pallas_call(
        body, name=name, grid=(bsz, s // tk),
        out_shape=jax.ShapeDtypeStruct((m, n), F32),
        in_specs=[pl.BlockSpec((None, tk, m), row)] + [pl.BlockSpec((None, tk, p.shape[2]), row) for p in b_parts],
        out_specs=pl.BlockSpec((m, n), lambda i, k: (0, 0)),
        compiler_params=_cp(("arbitrary", "arbitrary")),
    )(a, *b_parts)


def _rel_buckets():
    qi = np.arange(BLK)[:, None]
    kj = np.arange(2 * BLK)[None, :]
    rel = qi - kj + BLK
    n = np.maximum(rel, 0)
    nf = np.maximum(n, 1).astype(np.float32)
    large = REL_MAX_EXACT + (np.log(nf / REL_MAX_EXACT) / math.log(REL_MAX_DIST / REL_MAX_EXACT)
                             * (REL_BUCKETS - REL_MAX_EXACT)).astype(np.int32)
    large = np.minimum(large, REL_BUCKETS - 1)
    bucket = np.where(n < REL_MAX_EXACT, n, large).astype(np.int32)
    valid = ((rel >= 0) & (rel < BLK)).astype(np.int32)
    return bucket, valid


def _swa_bias(rel_bias_t, bucket, valid, name):
    nh = rel_bias_t.shape[0]

    def body(rb_ref, bk_ref, vl_ref, o_ref):
        h = pl.program_id(0)
        bk = bk_ref[...]
        acc = jnp.zeros(bk.shape, F32)
        for i in range(REL_BUCKETS):
            acc = jnp.where(bk == i, rb_ref[h, i], acc)
        o_ref[...] = jnp.where(vl_ref[...] > 0, acc, NEG)

    return pl.pallas_call(
        body, name=name, grid=(nh,),
        out_shape=jax.ShapeDtypeStruct((nh, BLK, 2 * BLK), F32),
        in_specs=[pl.BlockSpec(memory_space=pltpu.SMEM),
                  pl.BlockSpec((BLK, 2 * BLK), lambda h: (0, 0)),
                  pl.BlockSpec((BLK, 2 * BLK), lambda h: (0, 0))],
        out_specs=pl.BlockSpec((None, BLK, 2 * BLK), lambda h: (h, 0, 0)),
        compiler_params=_cp(("arbitrary",)),
    )(rel_bias_t, bucket, valid)


def _swa_scores(n, q, kw, bias_ref):
    sc = _dot_nt(q, kw) * SCALE + bias_ref[...]
    second = lax.broadcasted_iota(jnp.int32, sc.shape, 1) >= BLK
    return jnp.where(jnp.logical_or(n > 0, second), sc, NEG)


def _pad_front(dst_ref, src_ref):
    dst_ref[0:BLK, :] = jnp.zeros((BLK, dst_ref.shape[1]), dst_ref.dtype)
    dst_ref[BLK:, :] = src_ref[...]


def _swa_fwd(q, k, v, bias, sinks, name):
    b, hkv, nb, rows, hd = q.shape
    wide = bias.shape[2]
    stride = wide - BLK
    s = nb * stride

    def body(q_ref, k_ref, v_ref, bias_ref, sink_ref, o_ref, l_ref, kpad_ref, vpad_ref):
        _pad_front(kpad_ref, k_ref)
        _pad_front(vpad_ref, v_ref)
        sink = sink_ref[...]

        def step(n, carry):
            w0 = pl.multiple_of(n * stride, BLK)
            sc = _swa_scores(n, q_ref[n], kpad_ref[pl.ds(w0, wide), :], bias_ref)
            m = jnp.maximum(jnp.max(sc, axis=1, keepdims=True), sink)
            e = jnp.exp(sc - m)
            den = jnp.sum(e, axis=1, keepdims=True) + jnp.exp(sink - m)
            o_ref[n] = _dot((e * (1.0 / den)).astype(BF16), vpad_ref[pl.ds(w0, wide), :])
            l_ref[n] = m + jnp.log(den)
            return carry

        lax.fori_loop(0, nb, step, 0)

    qspec = pl.BlockSpec((None, None, nb, rows, hd), lambda i, kv: (i, kv, 0, 0, 0))
    kspec = pl.BlockSpec((None, None, s, hd), lambda i, kv: (i, kv, 0, 0))
    return pl.pallas_call(
        body, name=name, grid=(b, hkv),
        out_shape=(jax.ShapeDtypeStruct((b, hkv, nb, rows, hd), F32), jax.ShapeDtypeStruct((b, hkv, nb, rows, 1), F32)),
        in_specs=[qspec, kspec, kspec,
                  pl.BlockSpec((None, rows, wide), lambda i, kv: (kv, 0, 0)),
                  pl.BlockSpec((None, rows, 1), lambda i, kv: (kv, 0, 0))],
        out_specs=(qspec, pl.BlockSpec((None, None, nb, rows, 1), lambda i, kv: (i, kv, 0, 0, 0))),
        scratch_shapes=[pltpu.VMEM((s + BLK, hd), BF16), pltpu.VMEM((s + BLK, hd), BF16)],
        compiler_params=_cp(("parallel", "parallel")),
    )(q, k, v, bias, sinks)


def _swa_bwd(q, k, v, bias, sinks, do, lse, name):
    b, hkv, nb, rows, hd = q.shape
    wide = bias.shape[2]
    stride = wide - BLK
    s = nb * stride

    def body(q_ref, k_ref, v_ref, bias_ref, sink_ref, do_ref, l_ref,
             dq_ref, dk_ref, dv_ref, db_ref, dsk_ref, kpad_ref, vpad_ref, dkpad_ref, dvpad_ref):
        _pad_front(kpad_ref, k_ref)
        _pad_front(vpad_ref, v_ref)
        dkpad_ref[...] = jnp.zeros_like(dkpad_ref)
        dvpad_ref[...] = jnp.zeros_like(dvpad_ref)
        db_ref[...] = jnp.zeros_like(db_ref)
        sink = sink_ref[...]

        def step(n, dsink):
            w0 = pl.multiple_of(n * stride, BLK)
            win = pl.ds(w0, wide)
            qn = q_ref[n]
            kw = kpad_ref[win, :]
            ln = l_ref[n]
            p = jnp.exp(_swa_scores(n, qn, kw, bias_ref) - ln)
            dob = do_ref[n]
            dp = _dot_nt(dob, vpad_ref[win, :])
            delta = jnp.sum(p * dp, axis=1, keepdims=True)
            ds = p * (dp - delta)
            db_ref[...] += ds
            dsb = ds.astype(BF16)
            dq_ref[n] = (_dot(dsb, kw) * SCALE).astype(BF16)
            dkpad_ref[win, :] += _dot_tn(dsb, qn)
            dvpad_ref[win, :] += _dot_tn(p.astype(BF16), dob)
            return dsink - jnp.exp(sink - ln) * delta

        dsink = lax.fori_loop(0, nb, step, jnp.zeros((rows, 1), F32))
        for g in range(A_GROUP):
            tot = jnp.zeros((1, 1), F32)
            for blk in range(rows // (A_GROUP * BLK)):
                r0 = (blk * A_GROUP + g) * BLK
                tot = tot + jnp.sum(dsink[r0:r0 + BLK, :], axis=0, keepdims=True)
            dsk_ref[g] = jnp.broadcast_to(tot, (1, LANE))
        dk_ref[...] = (dkpad_ref[BLK:, :] * SCALE).astype(BF16)
        dv_ref[...] = dvpad_ref[BLK:, :].astype(BF16)

    qspec = pl.BlockSpec((None, None, nb, rows, hd), lambda i, kv: (i, kv, 0, 0, 0))
    kspec = pl.BlockSpec((None, None, s, hd), lambda i, kv: (i, kv, 0, 0))
    return pl.pallas_call(
        body, name=name, grid=(b, hkv),
        out_shape=(jax.ShapeDtypeStruct((b, hkv, nb, rows, hd), BF16), jax.ShapeDtypeStruct((b, hkv, s, hd), BF16),
                   jax.ShapeDtypeStruct((b, hkv, s, hd), BF16), jax.ShapeDtypeStruct((b, hkv, rows, wide), F32),
                   jax.ShapeDtypeStruct((b, hkv, A_GROUP, 1, LANE), F32)),
        in_specs=[qspec, kspec, kspec,
                  pl.BlockSpec((None, rows, wide), lambda i, kv: (kv, 0, 0)),
                  pl.BlockSpec((None, rows, 1), lambda i, kv: (kv, 0, 0)),
                  qspec,
                  pl.BlockSpec((None, None, nb, rows, 1), lambda i, kv: (i, kv, 0, 0, 0))],
        out_specs=(qspec, kspec, kspec,
                   pl.BlockSpec((None, None, rows, wide), lambda i, kv: (i, kv, 0, 0)),
                   pl.BlockSpec((None, None, A_GROUP, 1, LANE), lambda i, kv: (i, kv, 0, 0, 0))),
        scratch_shapes=[pltpu.VMEM((s + BLK, hd), BF16), pltpu.VMEM((s + BLK, hd), BF16),
                        pltpu.VMEM((s + BLK, hd), F32), pltpu.VMEM((s + BLK, hd), F32)],
        compiler_params=_cp(("parallel", "parallel")),
    )(q, k, v, bias, sinks, do, lse)


def _swa_small_grads(db, dsk, bucket, name):
    b, nh = db.shape[0], db.shape[1]

    def body(db_ref, dsk_ref, bk_ref, gb_ref, gs_ref):
        acc = db_ref[0]
        sk = dsk_ref[0]
        for i in range(1, b):
            acc = acc + db_ref[i]
            sk = sk + dsk_ref[i]
        gs_ref[...] = sk
        bk = bk_ref[...]
        for i in range(REL_BUCKETS):
            part = jnp.sum(jnp.where(bk == i, acc, 0.0), axis=1, keepdims=True)
            tot = jnp.sum(part, axis=0, keepdims=True)
            gb_ref[i:i + 1, :] = jnp.broadcast_to(tot, (1, LANE))

    return pl.pallas_call(
        body, name=name, grid=(nh,),
        out_shape=(jax.ShapeDtypeStruct((nh, REL_BUCKETS, LANE), F32), jax.ShapeDtypeStruct((nh, 1, LANE), F32)),
        in_specs=[pl.BlockSpec((b, None, BLK, 2 * BLK), lambda h: (0, h, 0, 0)),
                  pl.BlockSpec((b, None, 1, LANE), lambda h: (0, h, 0, 0)),
                  pl.BlockSpec((BLK, 2 * BLK), lambda h: (0, 0))],
        out_specs=(pl.BlockSpec((None, REL_BUCKETS, LANE), lambda h: (h, 0, 0)),
                   pl.BlockSpec((None, 1, LANE), lambda h: (h, 0, 0))),
        compiler_params=_cp(("parallel",)),
    )(db, dsk, bucket)


def _log_sigmoid(z):
    return jnp.minimum(z, 0.0) - jnp.log(1.0 + jnp.exp(-jnp.abs(z)))


def _fox_decay(z, bf, name):
    b, s, w = z.shape
    nb = s // BLK

    def body(z_ref, bf_ref, f_ref):
        r = lax.broadcasted_iota(jnp.int32, (BLK, BLK), 0)
        c = lax.broadcasted_iota(jnp.int32, (BLK, BLK), 1)
        tri = (c <= r).astype(F32)

        def step(n, carry):
            r0 = pl.multiple_of(n * BLK, BLK)
            lf = _log_sigmoid(z_ref[pl.ds(r0, BLK), :] + bf_ref[...])
            f_ref[pl.ds(r0, BLK), :] = jnp.dot(tri, lf, precision=HI, preferred_element_type=F32) + carry
            return carry + jnp.sum(lf, axis=0, keepdims=True)

        lax.fori_loop(0, nb, step, jnp.zeros((1, w), F32))

    spec = pl.BlockSpec((None, s, w), lambda i: (i, 0, 0))
    return pl.pallas_call(
        body, name=name, grid=(b,), out_shape=jax.ShapeDtypeStruct((b, s, w), F32),
        in_specs=[spec, pl.BlockSpec((1, w), lambda i: (0, 0))], out_specs=spec,
        compiler_params=_cp(("parallel",)),
    )(z, bf)


def _fox_dgate(df, z, bf, nheads, name):
    b, s, w = z.shape
    nb = s // BLK

    def body(df_ref, z_ref, bf_ref, dz_ref, dbf_ref):
        @pl.when(pl.program_id(0) == 0)
        def _():
            dbf_ref[...] = jnp.zeros_like(dbf_ref)

        r = lax.broadcasted_iota(jnp.int32, (BLK, BLK), 0)
        c = lax.broadcasted_iota(jnp.int32, (BLK, BLK), 1)
        tri = (c >= r).astype(F32)
        lane = lax.broadcasted_iota(jnp.int32, (BLK, w), 1)

        def step(i, carry):
            tail, dbf = carry
            r0 = pl.multiple_of((nb - 1 - i) * BLK, BLK)
            dfb = df_ref[pl.ds(r0, BLK), :]
            dlf = jnp.dot(tri, dfb, precision=HI, preferred_element_type=F32) + tail
            dz = jnp.where(lane < nheads, dlf * _sigmoid(-(z_ref[pl.ds(r0, BLK), :] + bf_ref[...])), 0.0)
            dz_ref[pl.ds(r0, BLK), :] = dz
            return tail + jnp.sum(dfb, axis=0, keepdims=True), dbf + jnp.sum(dz, axis=0, keepdims=True)

        zero = jnp.zeros((1, w), F32)
        _, dbf = lax.fori_loop(0, nb, step, (zero, zero))
        dbf_ref[...] += dbf

    spec = pl.BlockSpec((None, s, w), lambda i: (i, 0, 0))
    one = pl.BlockSpec((1, w), lambda i: (0, 0))
    return pl.pallas_call(
        body, name=name, grid=(b,),
        out_shape=(jax.ShapeDtypeStruct((b, s, w), F32), jax.ShapeDtypeStruct((1, w), F32)),
        in_specs=[spec, spec, one], out_specs=(spec, one),
        compiler_params=_cp(("arbitrary",)),
    )(df, z, bf)


def _fox_segments(nb):
    per = max(1, nb // 4)
    return per, nb // per


def _head_masks(shape, axis):
    idx = lax.broadcasted_iota(jnp.int32, shape, axis)
    return idx < HEAD_DIM, idx >= HEAD_DIM


def _fox_fwd(proj, qblk, kblk, vblk, fcol, frow, name):
    b, s, _ = proj.shape
    nh = frow.shape[1]
    npair = nh // 2
    BLK = min(s, FOX_BQ)
    assert s % BLK == 0
    per, nseg = _fox_segments(s // BLK)

    def body(q_ref, k_ref, v_ref, fc_ref, fr_ref, o_ref, l_ref, qm_ref, kt_ref, vb_ref):
        lo, hi = _head_masks((s, LANE), 1)
        qv = q_ref[...].astype(F32) * SCALE
        qm_ref[0] = jnp.where(lo, qv, 0.0).astype(BF16)
        qm_ref[1] = jnp.where(hi, qv, 0.0).astype(BF16)
        kt_ref[...] = k_ref[...].astype(F32).T.astype(BF16)
        vb_ref[...] = v_ref[...].astype(BF16)
        lane_lo = lax.broadcasted_iota(jnp.int32, (BLK, LANE), 1) < HEAD_DIM
        tail = per * BLK
        causal = (lax.broadcasted_iota(jnp.int32, (BLK, tail), 1)
                  - lax.broadcasted_iota(jnp.int32, (BLK, tail), 0))
        for seg in range(nseg):
            w = (seg + 1) * tail

            def qstep(n, carry):
                r0 = pl.multiple_of(n * BLK, BLK)
                outs = []
                for hh in range(2):
                    sc = _dot(qm_ref[hh, pl.ds(r0, BLK), :], kt_ref[:, :w])
                    sc = sc + (fc_ref[hh, pl.ds(r0, BLK), :] - fr_ref[hh, :, :w])
                    masked = jnp.where(causal <= (n - seg * per) * BLK, sc[:, w - tail:], NEG)
                    sc = masked if seg == 0 else jnp.concatenate([sc[:, :w - tail], masked], axis=1)
                    m = jnp.max(sc, axis=1, keepdims=True)
                    e = jnp.exp(sc - m)
                    l = jnp.sum(e, axis=1, keepdims=True)
                    outs.append(_dot((e * (1.0 / l)).astype(BF16), vb_ref[:w, :]))
                    l_ref[hh, pl.ds(r0, BLK), :] = m + jnp.log(l)
                o_ref[pl.ds(r0, BLK), :] = jnp.where(lane_lo, outs[0], outs[1])
                return carry

            lax.fori_loop(seg * per, (seg + 1) * per, qstep, 0)

    def tok(blk):
        return pl.BlockSpec((None, s, LANE), lambda i, p: (i, 0, blk + p))

    col = pl.BlockSpec((None, 2, s, 1), lambda i, p: (i, p, 0, 0))
    rowspec = pl.BlockSpec((None, 2, 1, s), lambda i, p: (i, p, 0, 0))
    return pl.pallas_call(
        body, name=name, grid=(b, npair),
        out_shape=(jax.ShapeDtypeStruct((b, s, nh * HEAD_DIM), F32), jax.ShapeDtypeStruct((b, nh, s, 1), F32)),
        in_specs=[tok(qblk), tok(kblk), tok(vblk), col, rowspec],
        out_specs=(pl.BlockSpec((None, s, LANE), lambda i, p: (i, 0, p)), col),
        scratch_shapes=[pltpu.VMEM((2, s, LANE), BF16), pltpu.VMEM((LANE, s), BF16), pltpu.VMEM((s, LANE), BF16)],
        compiler_params=_cp(("parallel", "parallel")),
    )(proj, proj, proj, fcol, frow)


def _fox_bwd(proj, qblk, kblk, vblk, dmix, doblk, fcol, frow, frowb, lse, lserowb, name):
    b, s, _ = proj.shape
    nh = frow.shape[1]
    npair = nh // 2
    BLK = min(s, FOX_BQ)
    assert s % BLK == 0
    nb = s // BLK
    per, nseg = _fox_segments(nb)

    def body(q_ref, k_ref, v_ref, do_ref, fc_ref, fr_ref, frb_ref, l_ref, lrb_ref,
             dq_ref, dk_ref, dv_ref, dfr_ref,
             qm_ref, dom_ref, kb_ref, vb_ref, kt_ref, vt_ref, qtm_ref, dotm_ref, dka_ref, dva_ref):
        lo, hi = _head_masks((s, LANE), 1)
        qv = q_ref[...].astype(F32) * SCALE
        dov = do_ref[...]
        for hh, msk in enumerate((lo, hi)):
            qm_ref[hh] = jnp.where(msk, qv, 0.0).astype(BF16)
            dom_ref[hh] = jnp.where(msk, dov, 0.0).astype(BF16)
        kv = k_ref[...].astype(F32)
        vv = v_ref[...].astype(F32)
        kb_ref[...] = kv.astype(BF16)
        vb_ref[...] = vv.astype(BF16)
        kt_ref[...] = kv.T.astype(BF16)
        vt_ref[...] = vv.T.astype(BF16)
        rlo, rhi = _head_masks((LANE, BLK), 0)

        def tstep(n, carry):
            r0 = pl.multiple_of(n * BLK, BLK)
            qt = (q_ref[pl.ds(r0, BLK), :].astype(F32) * SCALE).T
            dt = do_ref[pl.ds(r0, BLK), :].astype(F32).T
            for hh, msk in enumerate((rlo, rhi)):
                qtm_ref[hh, n] = jnp.where(msk, qt, 0.0).astype(BF16)
                dotm_ref[hh, n] = jnp.where(msk, dt, 0.0).astype(BF16)
            return carry

        lax.fori_loop(0, nb, tstep, 0)
        dka_ref[...] = jnp.zeros_like(dka_ref)
        dva_ref[...] = jnp.zeros_like(dva_ref)
        dfr_ref[...] = jnp.zeros_like(dfr_ref)
        lane_lo = lax.broadcasted_iota(jnp.int32, (BLK, LANE), 1) < HEAD_DIM
        tail = per * BLK
        causal = (lax.broadcasted_iota(jnp.int32, (BLK, tail), 1)
                  - lax.broadcasted_iota(jnp.int32, (BLK, tail), 0))
        causal_t = (lax.broadcasted_iota(jnp.int32, (tail, BLK), 0)
                    - lax.broadcasted_iota(jnp.int32, (tail, BLK), 1))
        for seg in range(nseg):
            w = (seg + 1) * tail

            def nstep(n, carry):
                r0 = pl.multiple_of(n * BLK, BLK)
                lim = (n - seg * per) * BLK
                dqs = []
                for hh in range(2):
                    qn = qm_ref[hh, pl.ds(r0, BLK), :]
                    don = dom_ref[hh, pl.ds(r0, BLK), :]
                    sc = _dot(qn, kt_ref[:, :w]) + ((fc_ref[hh, pl.ds(r0, BLK), :] - l_ref[hh, pl.ds(r0, BLK), :])
                                                   - fr_ref[hh, :, :w])
                    masked = jnp.where(causal <= lim, sc[:, w - tail:], NEG)
                    p = jnp.exp(masked if seg == 0 else jnp.concatenate([sc[:, :w - tail], masked], axis=1))
                    dp = _dot(don, vt_ref[:, :w])
                    ds = p * (dp - jnp.sum(p * dp, axis=1, keepdims=True))
                    dqs.append(_dot(ds.astype(BF16), kb_ref[:w, :]))
                    dfr_ref[hh, :, :w] -= jnp.sum(ds, axis=0, keepdims=True)
                    sct = _dot(kb_ref[:w, :], qtm_ref[hh, n]) + ((frb_ref[hh, n] - lrb_ref[hh, n]) - fc_ref[hh, :w, :])
                    masked_t = jnp.where(causal_t <= lim, sct[w - tail:, :], NEG)
                    pt = jnp.exp(masked_t if seg == 0 else jnp.concatenate([sct[:w - tail, :], masked_t], axis=0))
                    dpt = _dot(vb_ref[:w, :], dotm_ref[hh, n])
                    dst = pt * (dpt - jnp.sum(pt * dpt, axis=0, keepdims=True))
                    dka_ref[:w, :] += _dot(dst.astype(BF16), qn)
                    dva_ref[:w, :] += _dot(pt.astype(BF16), don)
                dq_ref[pl.ds(r0, BLK), :] = (jnp.where(lane_lo, dqs[0], dqs[1]) * SCALE).astype(BF16)
                return carry

            lax.fori_loop(seg * per, (seg + 1) * per, nstep, 0)
        dk_ref[...] = dka_ref[...].astype(BF16)
        dv_ref[...] = dva_ref[...].astype(BF16)

    def tok(blk):
        return pl.BlockSpec((None, s, LANE), lambda i, p: (i, 0, blk + p))

    col = pl.BlockSpec((None, 2, s, 1), lambda i, p: (i, p, 0, 0))
    rowspec = pl.BlockSpec((None, 2, 1, s), lambda i, p: (i, p, 0, 0))
    rowbspec = pl.BlockSpec((None, 2, nb, 1, BLK), lambda i, p: (i, p, 0, 0, 0))
    outtok = pl.BlockSpec((None, s, LANE), lambda i, p: (i, 0, p))
    shp = jax.ShapeDtypeStruct((b, s, nh * HEAD_DIM), BF16)
    return pl.pallas_call(
        body, name=name, grid=(b, npair),
        out_shape=(shp, shp, shp, jax.ShapeDtypeStruct((b, nh, 1, s), F32)),
        in_specs=[tok(qblk), tok(kblk), tok(vblk),
                  pl.BlockSpec((None, s, LANE), lambda i, p: (i, 0, doblk + p)),
                  col, rowspec, rowbspec, col, rowbspec],
        out_specs=(outtok, outtok, outtok, rowspec),
        scratch_shapes=[pltpu.VMEM((2, s, LANE), BF16), pltpu.VMEM((2, s, LANE), BF16),
                        pltpu.VMEM((s, LANE), BF16), pltpu.VMEM((s, LANE), BF16),
                        pltpu.VMEM((LANE, s), BF16), pltpu.VMEM((LANE, s), BF16),
                        pltpu.VMEM((2, nb, LANE, BLK), BF16), pltpu.VMEM((2, nb, LANE, BLK), BF16),
                        pltpu.VMEM((s, LANE), F32), pltpu.VMEM((s, LANE), F32)],
        compiler_params=_cp(("parallel", "parallel")),
    )(proj, proj, proj, dmix, fcol, frow, frowb, lse, lserowb)


def _expm1(x):
    poly = x * (1.0 + x * (1.0 / 2 + x * (1.0 / 6 + x * (1.0 / 24 + x * (1.0 / 120 + x * (1.0 / 720))))))
    return jnp.where(x > -0.1, poly, jnp.exp(x) - 1.0)


def _softplus(z):
    return jnp.maximum(z, 0.0) + jnp.log(1.0 + jnp.exp(-jnp.abs(z)))


def _scan_rows(a, u, carry, row, up):
    tc, c = a.shape
    d = 1
    while d < tc:
        if d < SUBLANE:
            keep = (row >= d) if up else (row < tc - d)
            shift = d if up else tc - d
            a_sh = jnp.where(keep, pltpu.roll(a, shift, 0), 1.0)
            u_sh = jnp.where(keep, pltpu.roll(u, shift, 0), 0.0)
        elif up:
            a_sh = jnp.concatenate([jnp.ones((d, c), F32), a[:tc - d]], axis=0)
            u_sh = jnp.concatenate([jnp.zeros((d, c), F32), u[:tc - d]], axis=0)
        else:
            a_sh = jnp.concatenate([a[d:], jnp.ones((d, c), F32)], axis=0)
            u_sh = jnp.concatenate([u[d:], jnp.zeros((d, c), F32)], axis=0)
        u = a * u_sh + u
        a = a * a_sh
        d *= 2
    return u + a * carry


def _scan_up(a, u, carry, row):
    return _scan_rows(a, u, carry, row, True)


def _scan_down(bnext, g, carry, row):
    return _scan_rows(bnext, g, carry, row, False)


def _pick_row(val, row, which):
    return jnp.sum(jnp.where(row == which, val, 0.0), axis=0, keepdims=True)


def _lru_gates(xpad_ref, t0, tc, cw_ref, cb_ref, wa, ba_ref, wx, bx_ref, sp):
    xw = xpad_ref[pl.ds(t0, tc + SUBLANE), :]
    xc = cb_ref[...]
    for j in range(CONV_WIDTH):
        sh = CONV_WIDTH - 1 - j
        xs = xw if sh == 0 else pltpu.roll(xw, sh, 0)
        xc = xc + xs[SUBLANE:, :] * cw_ref[j:j + 1, :]
    xcb = xc.astype(BF16)
    r = _sigmoid(_dot(xcb, wa) + ba_ref[...])
    i = _sigmoid(_dot(xcb, wx) + bx_ref[...])
    la = -LRU_C * r * sp
    return xc, r, i, la


def _lru_specs(s, cb):
    seq = lambda bi, ni: (bi, 0, ni)
    return dict(
        seq=pl.BlockSpec((None, s, cb), seq),
        cw=pl.BlockSpec((CONV_WIDTH, cb), lambda bi, ni: (0, ni)),
        vec=pl.BlockSpec((1, cb), lambda bi, ni: (0, ni)),
        wblk=pl.BlockSpec((None, cb, cb), lambda bi, ni: (ni, 0, 0)),
    )


def _lru_fwd(proj, cw, cb_, wa, ba, wx, bx, lam, name):
    b, s, _ = proj.shape
    nblk, cb, _ = wa.shape
    tc = min(s, SCAN_CHUNK)
    nc = s // tc

    def body(x_ref, cw_ref, cb_ref, wa_ref, ba_ref, wx_ref, bx_ref, lam_ref, hs_ref, xpad_ref):
        xpad_ref[0:SUBLANE, :] = jnp.zeros((SUBLANE, cb), F32)
        xpad_ref[SUBLANE:, :] = x_ref[...].astype(F32)
        wa_b = wa_ref[...].astype(BF16)
        wx_b = wx_ref[...].astype(BF16)
        sp = _softplus(-lam_ref[...])
        row = lax.broadcasted_iota(jnp.int32, (tc, cb), 0)

        def chunk(ci, carry):
            t0 = pl.multiple_of(ci * tc, tc)
            xc, r, i, la = _lru_gates(xpad_ref, t0, tc, cw_ref, cb_ref, wa_b, ba_ref, wx_b, bx_ref, sp)
            a = jnp.exp(la)
            u = jnp.sqrt(-_expm1(2.0 * la)) * (i * xc)
            h = _scan_up(a, u, carry, row)
            hs_ref[pl.ds(t0, tc), :] = h
            return _pick_row(h, row, tc - 1)

        lax.fori_loop(0, nc, chunk, jnp.zeros((1, cb), F32))

    sp_ = _lru_specs(s, cb)
    return pl.pallas_call(
        body, name=name, grid=(b, nblk),
        out_shape=jax.ShapeDtypeStruct((b, s, nblk * cb), F32),
        in_specs=[sp_["seq"], sp_["cw"], sp_["vec"], sp_["wblk"], sp_["vec"], sp_["wblk"], sp_["vec"], sp_["vec"]],
        out_specs=sp_["seq"],
        scratch_shapes=[pltpu.VMEM((s + SUBLANE, cb), F32)],
        compiler_params=_cp(("parallel", "parallel")),
    )(proj, cw, cb_, wa, ba, wx, bx, lam)


def _lru_bwd(proj, hs, dhs, cw, cb_, wa, ba, wx, bx, lam, name):
    b, s, _ = proj.shape
    nblk, cb, _ = wa.shape
    tc = min(s, SCAN_CHUNK)
    nc = s // tc

    def body(x_ref, hs_ref, dhs_ref, cw_ref, cb_ref, wa_ref, ba_ref, wx_ref, bx_ref, lam_ref,
             dx_ref, dcw_ref, dcb_ref, dwa_ref, dba_ref, dwx_ref, dbx_ref, dlam_ref,
             xpad_ref, hpad_ref, dcpad_ref, xc_ref, r_ref, i_ref, a_ref, mult_ref):
        @pl.when(pl.program_id(1) == 0)
        def _():
            for ref in (dcw_ref, dcb_ref, dwa_ref, dba_ref, dwx_ref, dbx_ref, dlam_ref):
                ref[...] = jnp.zeros_like(ref)

        zeros8 = jnp.zeros((SUBLANE, cb), F32)
        xpad_ref[0:SUBLANE, :] = zeros8
        xpad_ref[SUBLANE:, :] = x_ref[...].astype(F32)
        hpad_ref[0:SUBLANE, :] = zeros8
        hpad_ref[SUBLANE:, :] = hs_ref[...]
        dcpad_ref[s:s + SUBLANE, :] = zeros8
        wa_b = wa_ref[...].astype(BF16)
        wx_b = wx_ref[...].astype(BF16)
        lam_v = lam_ref[...]
        sp = _softplus(-lam_v)
        dsp_dlam = -_sigmoid(-lam_v)
        row = lax.broadcasted_iota(jnp.int32, (tc, cb), 0)

        def recompute(ci, carry):
            t0 = pl.multiple_of(ci * tc, tc)
            xc, r, i, la = _lru_gates(xpad_ref, t0, tc, cw_ref, cb_ref, wa_b, ba_ref, wx_b, bx_ref, sp)
            xc_ref[pl.ds(t0, tc), :] = xc
            r_ref[pl.ds(t0, tc), :] = r
            i_ref[pl.ds(t0, tc), :] = i
            a_ref[pl.ds(t0, tc), :] = jnp.exp(la)
            mult_ref[pl.ds(t0, tc), :] = jnp.sqrt(-_expm1(2.0 * la))
            return carry

        lax.fori_loop(0, nc, recompute, 0)

        def adjoint(k, carry):
            g_next, a_first_next = carry
            t0 = pl.multiple_of((nc - 1 - k) * tc, tc)
            a = a_ref[pl.ds(t0, tc), :]
            a_next = jnp.where(row == tc - 1, a_first_next, pltpu.roll(a, tc - 1, 0))
            gg = _scan_down(a_next, dhs_ref[pl.ds(t0, tc), :], g_next, row)
            h_prev = pltpu.roll(hpad_ref[pl.ds(t0, tc + SUBLANE), :], 1, 0)[SUBLANE:, :]
            xc = xc_ref[pl.ds(t0, tc), :]
            r = r_ref[pl.ds(t0, tc), :]
            i = i_ref[pl.ds(t0, tc), :]
            mult = mult_ref[pl.ds(t0, tc), :]
            d_mult = gg * i * xc
            d_i = gg * mult * xc
            d_xc = gg * mult * i
            d_la = gg * h_prev * a - d_mult * (a * a) / mult
            d_zr = (d_la * (-LRU_C * sp)) * r * (1.0 - r)
            d_zi = d_i * i * (1.0 - i)
            dlam_ref[...] += jnp.sum(d_la * (-LRU_C * r), axis=0, keepdims=True) * dsp_dlam
            dzr_b = d_zr.astype(BF16)
            dzi_b = d_zi.astype(BF16)
            xcb = xc.astype(BF16)
            d_xc = d_xc + _dot_nt(dzr_b, wa_b) + _dot_nt(dzi_b, wx_b)
            dwa_ref[...] += _dot_tn(xcb, dzr_b)
            dwx_ref[...] += _dot_tn(xcb, dzi_b)
            dba_ref[...] += jnp.sum(d_zr, axis=0, keepdims=True)
            dbx_ref[...] += jnp.sum(d_zi, axis=0, keepdims=True)
            dcb_ref[...] += jnp.sum(d_xc, axis=0, keepdims=True)
            dcpad_ref[pl.ds(t0, tc), :] = d_xc
            return _pick_row(gg, row, 0), _pick_row(a, row, 0)

        zero = jnp.zeros((1, cb), F32)
        lax.fori_loop(0, nc, adjoint, (zero, zero))

        def conv_back(ci, carry):
            t0 = pl.multiple_of(ci * tc, tc)
            dw = dcpad_ref[pl.ds(t0, tc + SUBLANE), :]
            xw = xpad_ref[pl.ds(t0, tc + SUBLANE), :]
            d_xc = dw[:tc, :]
            dxr = jnp.zeros((tc, cb), F32)
            for j in range(CONV_WIDTH):
                sh = CONV_WIDTH - 1 - j
                dsh = dw if sh == 0 else pltpu.roll(dw, tc + SUBLANE - sh, 0)
                dxr = dxr + dsh[:tc, :] * cw_ref[j:j + 1, :]
                xs = xw if sh == 0 else pltpu.roll(xw, sh, 0)
                dcw_ref[j:j + 1, :] += jnp.sum(d_xc * xs[SUBLANE:, :], axis=0, keepdims=True)
            dx_ref[pl.ds(t0, tc), :] = dxr.astype(BF16)
            return carry

        lax.fori_loop(0, nc, conv_back, 0)

    seq = lambda ni, bi: (bi, 0, ni)
    seqspec = pl.BlockSpec((None, s, cb), seq)
    cwspec = pl.BlockSpec((CONV_WIDTH, cb), lambda ni, bi: (0, ni))
    vec = pl.BlockSpec((1, cb), lambda ni, bi: (0, ni))
    wblk = pl.BlockSpec((None, cb, cb), lambda ni, bi: (ni, 0, 0))
    w = nblk * cb
    return pl.pallas_call(
        body, name=name, grid=(nblk, b),
        out_shape=(jax.ShapeDtypeStruct((b, s, w), BF16), jax.ShapeDtypeStruct((CONV_WIDTH, w), F32),
                   jax.ShapeDtypeStruct((1, w), F32), jax.ShapeDtypeStruct((nblk, cb, cb), F32),
                   jax.ShapeDtypeStruct((1, w), F32), jax.ShapeDtypeStruct((nblk, cb, cb), F32),
                   jax.ShapeDtypeStruct((1, w), F32), jax.ShapeDtypeStruct((1, w), F32)),
        in_specs=[seqspec, seqspec, seqspec, cwspec, vec, wblk, vec, wblk, vec, vec],
        out_specs=(seqspec, cwspec, vec, wblk, vec, wblk, vec, vec),
        scratch_shapes=[pltpu.VMEM((s + SUBLANE, cb), F32)] * 3 + [pltpu.VMEM((s, cb), F32)] * 5,
        compiler_params=_cp(("parallel", "arbitrary")),
    )(proj, hs, dhs, cw, cb_, wa, ba, wx, bx, lam)


def _adamw(w, g, m, v, name):
    shape = w.shape
    total = int(np.prod(shape))
    if w.ndim >= 2 and shape[-2] % SUBLANE == 0:
        rows, cols = shape[-2:]
    else:
        cols = 1024
        rows = -(-(-(-total // cols)) // SUBLANE) * SUBLANE
    lead = -(-total // (rows * cols))
    tr = _row_tile(rows, 512)
    pad = lead * rows * cols - total

    def flat(a):
        if pad:
            a = jnp.pad(a.reshape(-1), (0, pad))
        return a.reshape(lead, rows, cols)

    c1 = 1.0 - ADAM_B1 ** ADAM_STEP
    c2 = 1.0 - ADAM_B2 ** ADAM_STEP

    def body(w_ref, g_ref, m_ref, v_ref, d_ref, nm_ref, nv_ref):
        gv = g_ref[...]
        nm = ADAM_B1 * m_ref[...] + (1.0 - ADAM_B1) * gv
        nv = ADAM_B2 * v_ref[...] + (1.0 - ADAM_B2) * (gv * gv)
        nm_ref[...] = nm
        nv_ref[...] = nv
        d_ref[...] = -ADAM_LR * ((nm / c1) / (jnp.sqrt(nv / c2) + ADAM_EPS) + ADAM_WD * w_ref[...])

    spec = pl.BlockSpec((None, tr, cols), lambda l, i: (l, i, 0))
    shp = jax.ShapeDtypeStruct((lead, rows, cols), F32)
    outs = pl.pallas_call(
        body, name=name, grid=(lead, rows // tr), out_shape=(shp, shp, shp),
        in_specs=[spec] * 4, out_specs=(spec,) * 3,
        compiler_params=_cp(("parallel", "parallel")),
    )(flat(w), flat(g), flat(m), flat(v))
    if pad:
        return tuple(o.reshape(-1)[:total].reshape(shape) for o in outs)
    return tuple(o.reshape(shape) for o in outs)


def _to_heads(t, nh):
    b, s, _ = t.shape
    return t.reshape(b, s, nh, HEAD_DIM).transpose(0, 2, 1, 3)


def _stack_heads(t):
    b, s, _ = t.shape
    steps = s // (SWA_BPS * BLK)
    t = t.reshape(b, steps, SWA_BPS, BLK, A_KV_HEADS, A_GROUP, HEAD_DIM).transpose(0, 4, 1, 2, 5, 3, 6)
    return t.reshape(b, A_KV_HEADS, steps, SWA_BPS * A_GROUP * BLK, HEAD_DIM)


def _unstack_heads(t):
    b, hkv, steps, rows, hd = t.shape
    t = t.reshape(b, hkv, steps, SWA_BPS, A_GROUP, BLK, hd).transpose(0, 2, 3, 5, 1, 4, 6)
    return t.reshape(b, steps * SWA_BPS * BLK, hkv * A_GROUP * hd)


def _from_heads(t):
    b, nh, s, hd = t.shape
    return t.transpose(0, 2, 1, 3).reshape(b, s, nh * hd)


def _pad_rows(a, mult):
    r = a.shape[0]
    p = (-r) % mult
    return jnp.pad(a, ((0, p), (0, 0))) if p else a


def kernel(x, c, rel_bias, norm_g, ada_w, ada_b, attn_w_in, attn_sinks, attn_b_f, attn_w_out, lru_w_in, lru_conv_w, lru_conv_b, lru_w_a, lru_b_a, lru_w_x, lru_b_x, lru_lambda, lru_w_out, final_g, loss_target, m_rel_bias, m_norm_g, m_ada_w, m_ada_b, m_attn_w_in, m_attn_sinks, m_attn_b_f, m_attn_w_out, m_lru_w_in, m_lru_conv_w, m_lru_conv_b, m_lru_w_a, m_lru_b_a, m_lru_w_x, m_lru_b_x, m_lru_lambda, m_lru_w_out, m_final_g, v_rel_bias, v_norm_g, v_ada_w, v_ada_b, v_attn_w_in, v_attn_sinks, v_attn_b_f, v_attn_w_out, v_lru_w_in, v_lru_conv_w, v_lru_conv_b, v_lru_w_a, v_lru_b_a, v_lru_w_x, v_lru_b_x, v_lru_lambda, v_lru_w_out, v_final_g):
    bl, s, d = x.shape
    ix, iy, ic = lax.axis_index("x"), lax.axis_index("y"), lax.axis_index("c")
    chip = 2 * ix + iy
    me = 2 * chip + ic
    nb = s // BLK
    aw = A_Q_HEADS * HEAD_DIM
    akv = A_KV_HEADS * HEAD_DIM
    bw = B_HEADS * HEAD_DIM
    mixw = aw + bw
    qkv_w = aw + 2 * akv + 3 * bw
    n_in = attn_w_in.shape[2] * N_CHIP
    lw = lru_lambda.shape[1] * N_CHIP
    n0 = mixw + qkv_w + LANE

    rows_pad = -(-bl // SUBLANE) * SUBLANE
    vec_rows = jnp.concatenate([lru_conv_w[0], lru_conv_b, lru_b_a, lru_b_x, lru_lambda], axis=0)
    first = jnp.concatenate([_pad_rows(c, SUBLANE), jnp.pad(vec_rows, ((0, 0), (0, d - lw // N_CHIP)))], axis=0)
    first = _all_gather8(first, "gather_c", pltpu.VMEM).reshape(N_DEV, rows_pad + SUBLANE, d)
    c_all = first[:, :bl].reshape(N_DEV * bl, d)
    vec_all = first[:, rows_pad:, :lw // N_CHIP].reshape(N_CHIP, 2, SUBLANE, lw // N_CHIP)[:, 0]
    vec_all = vec_all.transpose(1, 0, 2).reshape(SUBLANE, lw)
    ncol = ada_w.shape[2]
    ada_w_l = lax.dynamic_index_in_dim(ada_w, ic, 0, keepdims=False)
    ada_b_l = lax.dynamic_slice(ada_b, (ic, chip * ncol), (1, ncol))
    mod_part = _ada_fwd(c_all, ada_w_l, ada_b_l, "ada_fwd")
    mod_all = _all_gather8(_pad_rows(mod_part, SUBLANE), "gather_mod", pltpu.VMEM)
    mrows = -(-(N_DEV * bl) // SUBLANE) * SUBLANE
    mod_all = mod_all.reshape(N_CHIP, 2, mrows, ncol)[:, :, :N_DEV * bl]
    mod_all = mod_all.transpose(1, 2, 0, 3).reshape(2, N_DEV * bl, N_CHIP * ncol)
    mod = lax.dynamic_slice_in_dim(mod_all, me * bl, bl, axis=1)
    shift = [mod[l, :, 0:d].reshape(bl, 1, d) for l in range(2)]
    scale = [mod[l, :, d:2 * d].reshape(bl, 1, d) for l in range(2)]
    gmod = [mod[l, :, 2 * d:3 * d].reshape(bl, 1, d) for l in range(2)]

    c_in0 = n_in // N_CHIP
    c_in1 = 2 * lw // N_CHIP
    assert c_in0 <= d and 2 * c_in1 == d
    r_in0, r_out0, r_in1, r_out1 = d // 2, mixw // N_CHIP // 2, d // 4, lw // N_CHIP // 2
    o_out0, o_in1, o_out1 = r_in0, r_in0 + r_out0, r_in0 + r_out0 + r_in1
    big_rows = o_out1 + r_out1

    def half_of(a, rows):
        return lax.dynamic_slice_in_dim(a, ic * rows, rows, axis=0)

    h_in1 = half_of(lru_w_in[0], r_in0).astype(BF16)
    my_half = jnp.concatenate([
        jnp.pad(half_of(attn_w_in[0], r_in0).astype(BF16), ((0, 0), (0, d - c_in0))),
        half_of(attn_w_out[0], r_out0).astype(BF16),
        jnp.concatenate([h_in1[:r_in1], h_in1[r_in1:]], axis=1),
        half_of(lru_w_out[0], r_out1).astype(BF16)], axis=0)
    gat = _all_gather8(my_half, "gather_weights", pltpu.HBM).reshape(N_CHIP, 2, big_rows, d)
    w_in0 = gat[:, :, :r_in0, :c_in0].transpose(1, 2, 0, 3).reshape(d, n_in)
    w_out0 = gat[:, :, o_out0:o_in1].reshape(mixw, d)
    w_in1 = gat[:, :, o_in1:o_out1].reshape(N_CHIP, 2, r_in1, 2, c_in1)
    w_in1 = w_in1.transpose(1, 3, 2, 0, 4).reshape(d, 2 * lw)
    w_out1 = gat[:, :, o_out1:].reshape(lw, d)
    w_cat0 = jnp.concatenate([w_in0[:, qkv_w + B_HEADS:], w_in0[:, :qkv_w + B_HEADS],
                              jnp.zeros((d, n0 - n_in), BF16)], axis=1)

    proj0, h0, zf = _norm_proj(x, norm_g[0:1], scale[0], shift[0], w_cat0, LANE, "norm_proj0")
    o_a = mixw
    aq = _stack_heads(proj0[:, :, o_a:o_a + aw].astype(BF16))
    ak = _to_heads(proj0[:, :, o_a + aw:o_a + aw + akv].astype(BF16), A_KV_HEADS)
    av = _to_heads(proj0[:, :, o_a + aw + akv:o_a + aw + 2 * akv].astype(BF16), A_KV_HEADS)
    o_b = o_a + aw + 2 * akv
    fox_blks = (o_b // LANE, (o_b + bw) // LANE, (o_b + 2 * bw) // LANE)
    bucket_np, valid_np = _rel_buckets()
    bucket = jnp.asarray(bucket_np)
    bias = _swa_bias(rel_bias.T, bucket, jnp.asarray(valid_np), "swa_bias")
    bias = bias.reshape(A_KV_HEADS, A_GROUP * BLK, 2 * BLK)
    bias = jnp.concatenate([jnp.pad(bias, ((0, 0), (0, 0), (blk * BLK, (SWA_BPS - 1 - blk) * BLK)),
                                    constant_values=NEG) for blk in range(SWA_BPS)], axis=1)
    sinks = jnp.repeat(attn_sinks[0].reshape(A_KV_HEADS, A_GROUP), BLK, axis=1).reshape(A_KV_HEADS, A_GROUP * BLK, 1)
    sinks = jnp.tile(sinks, (1, SWA_BPS, 1))
    a_out, a_lse = _swa_fwd(aq, ak, av, bias, sinks, "swa_fwd")
    bf_pad = jnp.pad(attn_b_f, ((0, 0), (0, LANE - B_HEADS)))
    fsum = _fox_decay(zf, bf_pad, "fox_decay")
    fh = fsum[:, :, :B_HEADS].transpose(0, 2, 1)
    fcol = fh.reshape(bl, B_HEADS, s, 1)
    frow = fh.reshape(bl, B_HEADS, 1, s)
    fbq = min(s, FOX_BQ)
    frowb = fh.reshape(bl, B_HEADS, s // fbq, 1, fbq)
    b_out, b_lse = _fox_fwd(proj0, *fox_blks, fcol, frow, "fox_fwd")
    lserowb = b_lse.reshape(bl, B_HEADS, s // fbq, 1, fbq)
    mix0 = [_unstack_heads(a_out), b_out]
    x1, o0 = _gate_outproj(mix0, proj0, 0, w_out0, x, gmod[0], "gate_outproj0")

    proj1, h1 = _norm_proj(x1, norm_g[1:2], scale[1], shift[1], w_in1, 0, "norm_proj1")
    cw_f, cb_f, ba_f, bx_f, lam_f = vec_all[0:4], vec_all[4:5], vec_all[5:6], vec_all[6:7], vec_all[7:8]
    hs = _lru_fwd(proj1, cw_f, cb_f, lru_w_a[0], ba_f, lru_w_x[0], bx_f, lam_f, "lru_fwd")
    x2, o1 = _gate_outproj([hs], proj1, 1, w_out1, x1, gmod[1], "gate_outproj1")

    loss_vec, dx2, g_final = _final_loss(x2, final_g.reshape(1, d), loss_target, "final_loss")
    loss = lax.psum(loss_vec[0, 0], ("x", "y", "c"))

    dhs, dgate1, do1, y1, dgm1 = _bwd_out(dx2, gmod[1], o1, [hs], proj1, 1, w_out1, F32, "bwd_out1")
    g_w_out1 = _matmul_tn(y1, [do1], "grad_w_out1")
    (dxr, g_cw, g_cb, g_wa, g_ba, g_wx, g_bx, g_lam) = _lru_bwd(
        proj1, hs, dhs, cw_f, cb_f, lru_w_a[0], ba_f, lru_w_x[0], bx_f, lam_f, "lru_bwd")
    dproj1 = [dxr, dgate1]
    g_w_in1 = _matmul_tn(h1, dproj1, "grad_w_in1")
    dx1, dsh1, dsc1, g_ng1 = _bwd_in(dproj1, w_in1, x1, norm_g[1:2], scale[1], dx2, "bwd_in1")

    dmix0, dgate0, do0, y0, dgm0 = _bwd_out(dx1, gmod[0], o0, mix0, proj0, 0, w_out0, BF16, "bwd_out0")
    g_w_out0 = _matmul_tn(y0, [do0], "grad_w_out0")
    da_out = _stack_heads(dmix0[:, :, :aw].astype(BF16))
    daq, dak, dav, dbias, dsink = _swa_bwd(aq, ak, av, bias, sinks, da_out, a_lse, "swa_bwd")
    dbq, dbk, dbv, dfrow = _fox_bwd(proj0, *fox_blks, dmix0, aw // LANE, fcol, frow, frowb, b_lse, lserowb,
                                    "fox_bwd")
    df = dfrow.reshape(bl, B_HEADS, s).transpose(0, 2, 1)
    df = jnp.pad(df, ((0, 0), (0, 0), (0, LANE - B_HEADS)))
    dzf, g_bf = _fox_dgate(df, zf, bf_pad, B_HEADS, "fox_dgate")
    dproj0 = ([dgate0, _unstack_heads(daq), _from_heads(dak), _from_heads(dav)]
              + [dbq, dbk, dbv, dzf.astype(BF16)])
    g_w_cat0 = _matmul_tn(h0, dproj0, "grad_w_in0")
    g_w_in0 = jnp.concatenate([g_w_cat0[:, mixw:mixw + qkv_w + B_HEADS], g_w_cat0[:, :mixw]], axis=1)
    dx0, dsh0, dsc0, g_ng0 = _bwd_in(dproj0, w_cat0, x, norm_g[0:1], scale[0], dx1, "bwd_in0")
    dbias = sum(dbias[:, :, blk * A_GROUP * BLK:(blk + 1) * A_GROUP * BLK, blk * BLK:(blk + 2) * BLK]
                for blk in range(SWA_BPS))
    g_relb, g_sink = _swa_small_grads(dbias.reshape(bl, A_Q_HEADS, BLK, 2 * BLK),
                                      dsink.reshape(bl, A_Q_HEADS, 1, LANE), bucket, "swa_small_grads")

    dmod = jnp.concatenate([jnp.concatenate([dsh0, dsc0, dgm0], axis=-1),
                            jnp.concatenate([dsh1, dsc1, dgm1], axis=-1)], axis=1)
    dmod_rows = _pad_rows(dmod.reshape(bl * 6, d), SUBLANE)

    tail = jnp.concatenate([g_relb[:, :, 0].T.reshape(-1), g_sink[:, 0, 0], g_bf[0, :B_HEADS]])
    n_relb = REL_BUCKETS * A_Q_HEADS
    small_rows = [g_wa.reshape(-1, d), g_wx.reshape(-1, d), g_ng0, g_ng1, g_final, g_cw, g_cb, g_ba, g_bx, g_lam,
                  jnp.pad(tail, (0, d - tail.shape[0])).reshape(1, d)]
    small_counts = [r.shape[0] for r in small_rows]
    piece_rows = -(-(-(-sum(small_counts) // N_DEV)) // SUBLANE) * SUBLANE
    small_2d = jnp.concatenate(small_rows, axis=0)
    small_2d = jnp.pad(small_2d, ((0, N_DEV * piece_rows - small_2d.shape[0]), (0, 0)))
    small_pieces = small_2d.reshape(N_CHIP, 2, piece_rows, d)
    p_in0 = jnp.pad(g_w_in0.reshape(2, r_in0, N_CHIP, c_in0).transpose(2, 0, 1, 3),
                    ((0, 0), (0, 0), (0, 0), (0, d - c_in0)))
    p_in1 = g_w_in1.reshape(2, 2, r_in1, N_CHIP, c_in1).transpose(3, 0, 2, 1, 4).reshape(N_CHIP, 2, r_in1, d)
    pieces = jnp.concatenate([p_in0, g_w_out0.reshape(N_CHIP, 2, r_out0, d), p_in1,
                              g_w_out1.reshape(N_CHIP, 2, r_out1, d), small_pieces], axis=2)
    theirs = _sibling_push(pieces, True, "push_sibling_halves")
    partial = _pair_sum(jnp.reshape(ic, (1,)).astype(jnp.int32), pieces, theirs, "sum_chip")
    slots = _chip_all_to_all(partial, "exchange_grads")
    reduced = _sum_slots(slots, "sum_grads")
    mine_big = reduced[:big_rows]
    other_big = _sibling_push(mine_big[None], False, "swap_halves")[0]
    both = jnp.stack([jnp.where(ic == 0, mine_big, other_big), jnp.where(ic == 0, other_big, mine_big)])
    g_big = [both[:, :r_in0, :c_in0].reshape(d, c_in0),
             both[:, o_out0:o_in1].reshape(2 * r_out0, d),
             both[:, o_in1:o_out1].reshape(2, r_in1, 2, c_in1).transpose(0, 2, 1, 3).reshape(d, c_in1),
             both[:, o_out1:].reshape(2 * r_out1, d)]
    last = _all_gather8(jnp.concatenate([reduced[big_rows:], dmod_rows], axis=0), "gather_small_grads", pltpu.VMEM)
    last = last.reshape(N_DEV, piece_rows + dmod_rows.shape[0], d)
    small_all = last[:, :piece_rows].reshape(N_DEV * piece_rows, d)
    dmod_all = last[:, piece_rows:piece_rows + bl * 6].reshape(N_DEV * bl, 6 * d)
    dmod_chip = lax.dynamic_slice_in_dim(dmod_all.reshape(N_DEV * bl, 2, 3 * d), chip * ncol, ncol, axis=2)
    g_ada_w, g_ada_b = _ada_bwd(c_all, dmod_chip.transpose(1, 0, 2), dmod_all, "ada_bwd")
    g_ada_b = g_ada_b.reshape(2, 3 * d)
    g_small, off = [], 0
    for cnt in small_counts:
        g_small.append(small_all[off:off + cnt])
        off += cnt
    g_w_a, g_w_x = g_small[0].reshape(lru_w_a.shape[1:]), g_small[1].reshape(lru_w_x.shape[1:])
    g_norm_g = jnp.concatenate(g_small[2:4], axis=0)
    g_fin, g_cw_r, g_cb_r, g_ba_r, g_bx_r, g_lam_r = g_small[4:10]
    tail = g_small[10][0]
    g_rel_bias = tail[:n_relb].reshape(REL_BUCKETS, A_Q_HEADS)
    g_sinks, g_b_f = tail[n_relb:n_relb + A_Q_HEADS], tail[n_relb + A_Q_HEADS:n_relb + A_Q_HEADS + B_HEADS]
    cw4 = lw // N_CHIP

    def my_cols(a):
        return lax.dynamic_slice_in_dim(a, chip * cw4, cw4, axis=1)

    grads = {
        "rel_bias": g_rel_bias, "norm_g": g_norm_g, "ada_w": g_ada_w, "ada_b": g_ada_b,
        "attn_w_in": g_big[0][None], "attn_sinks": g_sinks[None], "attn_b_f": g_b_f[None],
        "attn_w_out": g_big[1][None], "lru_w_in": g_big[2][None], "lru_conv_w": my_cols(g_cw_r)[None],
        "lru_conv_b": my_cols(g_cb_r), "lru_w_a": g_w_a[None], "lru_b_a": my_cols(g_ba_r),
        "lru_w_x": g_w_x[None], "lru_b_x": my_cols(g_bx_r), "lru_lambda": my_cols(g_lam_r),
        "lru_w_out": g_big[3][None], "final_g": g_fin.reshape(d),
    }
    weights = dict(rel_bias=rel_bias, norm_g=norm_g, ada_w=ada_w, ada_b=ada_b, attn_w_in=attn_w_in,
                   attn_sinks=attn_sinks, attn_b_f=attn_b_f, attn_w_out=attn_w_out, lru_w_in=lru_w_in,
                   lru_conv_w=lru_conv_w, lru_conv_b=lru_conv_b, lru_w_a=lru_w_a, lru_b_a=lru_b_a,
                   lru_w_x=lru_w_x, lru_b_x=lru_b_x, lru_lambda=lru_lambda, lru_w_out=lru_w_out, final_g=final_g)
    moms = dict(rel_bias=(m_rel_bias, v_rel_bias), norm_g=(m_norm_g, v_norm_g), ada_w=(m_ada_w, v_ada_w),
                ada_b=(m_ada_b, v_ada_b), attn_w_in=(m_attn_w_in, v_attn_w_in),
                attn_sinks=(m_attn_sinks, v_attn_sinks), attn_b_f=(m_attn_b_f, v_attn_b_f),
                attn_w_out=(m_attn_w_out, v_attn_w_out), lru_w_in=(m_lru_w_in, v_lru_w_in),
                lru_conv_w=(m_lru_conv_w, v_lru_conv_w), lru_conv_b=(m_lru_conv_b, v_lru_conv_b),
                lru_w_a=(m_lru_w_a, v_lru_w_a), lru_b_a=(m_lru_b_a, v_lru_b_a), lru_w_x=(m_lru_w_x, v_lru_w_x),
                lru_b_x=(m_lru_b_x, v_lru_b_x), lru_lambda=(m_lru_lambda, v_lru_lambda),
                lru_w_out=(m_lru_w_out, v_lru_w_out), final_g=(m_final_g, v_final_g))
    names = list(weights)
    big_names = [n for n in names if weights[n].size >= 65536]
    small_names = [n for n in names if weights[n].size < 65536]
    delta, new_m, new_v = {}, {}, {}
    for n in big_names:
        delta[n], new_m[n], new_v[n] = _adamw(weights[n], grads[n].reshape(weights[n].shape),
                                              moms[n][0], moms[n][1], "adamw_" + n)
    cat = lambda arrs: jnp.concatenate([a.reshape(-1) for a in arrs])
    sd, sm, sv = _adamw(cat([weights[n] for n in small_names]), cat([grads[n] for n in small_names]),
                        cat([moms[n][0] for n in small_names]), cat([moms[n][1] for n in small_names]),
                        "adamw_small")
    off = 0
    for n in small_names:
        sz = weights[n].size
        shp = weights[n].shape
        delta[n], new_m[n], new_v[n] = (sd[off:off + sz].reshape(shp), sm[off:off + sz].reshape(shp),
                                        sv[off:off + sz].reshape(shp))
        off += sz
    out_grads = [grads[n].reshape(weights[n].shape) for n in names]
    return (loss, dx0, *out_grads, *[delta[n] for n in names], *[new_m[n] for n in names],
            *[new_v[n] for n in names])
```

```python
import math

import numpy as np
import jax
import jax.numpy as jnp
from jax import lax
from jax.experimental import pallas as pl
from jax.experimental.pallas import tpu as pltpu

F32 = jnp.float32
BF16 = jnp.bfloat16
MESH = pl.DeviceIdType.MESH

N_DEV = 8
N_CHIP = 4
HEAD_DIM = 64
BLK = 128
A_Q_HEADS = 8
A_KV_HEADS = 2
A_GROUP = A_Q_HEADS // A_KV_HEADS
B_HEADS = 8
REL_BUCKETS = 32
REL_MAX_EXACT = 16
REL_MAX_DIST = 128
LRU_C = 8.0
CONV_WIDTH = 4
EPS = 1e-6
NEG = -1e30
SCALE = HEAD_DIM ** -0.5
LANE = 128
SUBLANE = 8
VMEM_LIMIT = 56 * 1024 * 1024
SCAN_CHUNK = 512
ROW_TILE = 512
SWA_BPS = 1
FOX_BQ = 512
ADAM_LR = 0.001
ADAM_B1 = 0.9
ADAM_B2 = 0.999
ADAM_EPS = 1e-08
ADAM_WD = 0.01
ADAM_STEP = 10
HI = lax.Precision.HIGHEST


def _cp(sem=None):
    return pltpu.CompilerParams(dimension_semantics=sem, vmem_limit_bytes=VMEM_LIMIT)


def _dot(a, b):
    return jnp.dot(a, b, preferred_element_type=F32)


def _dot_nt(a, b):
    return lax.dot_general(a, b, (((1,), (1,)), ((), ())), preferred_element_type=F32)


def _dot_tn(a, b):
    return lax.dot_general(a, b, (((0,), (0,)), ((), ())), preferred_element_type=F32)


def _sigmoid(z):
    return 1.0 / (1.0 + jnp.exp(-z))


def _row_tile(rows, cap):
    if rows <= cap:
        return rows
    best = SUBLANE
    t = SUBLANE
    while t <= cap:
        if rows % t == 0:
            best = t
        t += SUBLANE
    return best


def _all_gather8(x_shard, name, space):
    m_per, n = x_shard.shape
    n_own = 8 if (space == pltpu.HBM and m_per % 128 == 0) else 1
    own_rows = m_per // n_own

    def body(x_ref, out_ref, send_sems, recv_sems, local_sems):
        x, y, c = lax.axis_index("x"), lax.axis_index("y"), lax.axis_index("c")
        me, sibling = (x, y, c), (x, y, 1 - c)
        chips = [(1 - x, y), (x, 1 - y), (1 - x, 1 - y)]

        def rows(px, py, pc):
            return out_ref.at[pl.ds((4 * px + 2 * py + pc) * m_per, m_per), :]

        def copy(k, block, to, src=None):
            return pltpu.make_async_remote_copy(
                src_ref=rows(*block) if src is None else src, dst_ref=rows(*block),
                send_sem=send_sems.at[k], recv_sem=recv_sems.at[k], device_id=to, device_id_type=MESH)

        base = (4 * x + 2 * y + c) * m_per
        mine = [pltpu.make_async_copy(x_ref.at[pl.ds(i * own_rows, own_rows), :],
                                      out_ref.at[pl.ds(base + i * own_rows, own_rows), :], local_sems.at[i])
                for i in range(n_own)]
        for cp in mine:
            cp.start()
        first = [copy(0, me, sibling, src=x_ref)]
        first += [copy(1 + j, me, (*chip, c), src=x_ref) for j, chip in enumerate(chips)]
        for cp in first:
            cp.start()
        passed = [copy(4 + j, (*chip, c), sibling) for j, chip in enumerate(chips)]
        for j, chip in enumerate(chips):
            copy(1 + j, (*chip, c), me).wait_recv()
            passed[j].start()
        copy(0, sibling, me).wait_recv()
        for j, chip in enumerate(chips):
            copy(4 + j, (*chip, 1 - c), me).wait_recv()
        for cp in first + passed:
            cp.wait_send()
        for cp in mine:
            cp.wait()

    return pl.pallas_call(
        body, name=name,
        out_shape=jax.ShapeDtypeStruct((N_DEV * m_per, n), x_shard.dtype),
        in_specs=[pl.BlockSpec(memory_space=space)],
        out_specs=pl.BlockSpec(memory_space=space),
        scratch_shapes=[pltpu.SemaphoreType.DMA((7,)), pltpu.SemaphoreType.DMA((7,)),
                        pltpu.SemaphoreType.DMA((n_own,))],
        compiler_params=pltpu.CompilerParams(vmem_limit_bytes=VMEM_LIMIT),
    )(x_shard)


def _sibling_push(blocks, pick_other, name):
    nblk = blocks.shape[0]
    m, n = blocks.shape[-2:]

    def body(x_ref, out_ref, send_sems, recv_sems):
        x, y, c = lax.axis_index("x"), lax.axis_index("y"), lax.axis_index("c")
        copies = []
        for k in range(nblk):
            src = x_ref.at[k, 1 - c] if pick_other else x_ref.at[k]
            copies.append(pltpu.make_async_remote_copy(
                src_ref=src, dst_ref=out_ref.at[k], send_sem=send_sems.at[k], recv_sem=recv_sems.at[k],
                device_id=(x, y, 1 - c), device_id_type=MESH))
        for cp in copies:
            cp.start()
        for cp in copies:
            cp.wait_recv()
        for cp in copies:
            cp.wait_send()

    hbm = pl.BlockSpec(memory_space=pltpu.HBM)
    return pl.pallas_call(
        body, name=name,
        out_shape=jax.ShapeDtypeStruct((nblk, m, n), blocks.dtype),
        in_specs=[hbm], out_specs=hbm,
        scratch_shapes=[pltpu.SemaphoreType.DMA((nblk,)), pltpu.SemaphoreType.DMA((nblk,))],
    )(blocks)


def _chip_all_to_all(parts, name):
    _, m, n = parts.shape

    def body(x_ref, out_ref, send_sems, recv_sems, local_sem):
        x, y, c = lax.axis_index("x"), lax.axis_index("y"), lax.axis_index("c")
        me = 2 * x + y
        mine = pltpu.make_async_copy(x_ref.at[me], out_ref.at[me], local_sem)
        mine.start()
        copies = []
        for k in range(1, N_CHIP):
            px, py = x ^ ((k >> 1) & 1), y ^ (k & 1)
            copies.append(pltpu.make_async_remote_copy(
                src_ref=x_ref.at[2 * px + py], dst_ref=out_ref.at[me],
                send_sem=send_sems.at[k - 1], recv_sem=recv_sems.at[k - 1],
                device_id=(px, py, c), device_id_type=MESH))
        for cp in copies:
            cp.start()
        for cp in copies:
            cp.wait_recv()
        for cp in copies:
            cp.wait_send()
        mine.wait()

    hbm = pl.BlockSpec(memory_space=pltpu.HBM)
    return pl.pallas_call(
        body, name=name,
        out_shape=jax.ShapeDtypeStruct(parts.shape, parts.dtype),
        in_specs=[hbm], out_specs=hbm,
        scratch_shapes=[pltpu.SemaphoreType.DMA((N_CHIP - 1,)), pltpu.SemaphoreType.DMA((N_CHIP - 1,)),
                        pltpu.SemaphoreType.DMA],
    )(parts)


def _pair_sum(core, pieces, theirs, name):
    nblk, _, m, n = pieces.shape
    tr = _row_tile(m, 536)

    def body(c_ref, p_ref, t_ref, o_ref):
        o_ref[...] = (p_ref[...] + t_ref[...]).astype(BF16)

    return pl.pallas_call(
        body, name=name,
        grid_spec=pltpu.PrefetchScalarGridSpec(
            num_scalar_prefetch=1, grid=(nblk, m // tr),
            in_specs=[pl.BlockSpec((None, None, tr, n), lambda k, i, c_ref: (k, c_ref[0], i, 0)),
                      pl.BlockSpec((None, tr, n), lambda k, i, c_ref: (k, i, 0))],
            out_specs=pl.BlockSpec((None, tr, n), lambda k, i, c_ref: (k, i, 0))),
        out_shape=jax.ShapeDtypeStruct((nblk, m, n), BF16),
        compiler_params=_cp(("parallel", "parallel")),
    )(core, pieces, theirs)


def _sum_slots(slots, name):
    k, m, n = slots.shape
    tr = _row_tile(m, 536)

    def body(s_ref, o_ref):
        acc = s_ref[0].astype(F32)
        for j in range(1, k):
            acc = acc + s_ref[j].astype(F32)
        o_ref[...] = acc

    return pl.pallas_call(
        body, name=name, grid=(m // tr,),
        out_shape=jax.ShapeDtypeStruct((m, n), F32),
        in_specs=[pl.BlockSpec((k, tr, n), lambda i: (0, i, 0))],
        out_specs=pl.BlockSpec((tr, n), lambda i: (i, 0)),
        compiler_params=_cp(("parallel",)),
    )(slots)


def _ada_fwd(c_all, w, b, name):
    r, _ = c_all.shape
    n = w.shape[1]

    def body(c_ref, w_ref, b_ref, o_ref):
        cv = c_ref[...]
        act = cv * _sigmoid(cv)
        o_ref[...] = jnp.dot(act, w_ref[...], precision=HI, preferred_element_type=F32) + b_ref[...]

    return pl.pallas_call(body, name=name, out_shape=jax.ShapeDtypeStruct((r, n), F32),
                          compiler_params=_cp())(c_all, w, b)


def _ada_bwd(c_all, dmod_chip, dmod_all, name):
    r, d = c_all.shape
    nl, _, n = dmod_chip.shape

    def body(c_ref, dm_ref, da_ref, gw_ref, gb_ref):
        cv = c_ref[...]
        act = cv * _sigmoid(cv)
        for l in range(nl):
            gw_ref[l] = lax.dot_general(act, dm_ref[l], (((0,), (0,)), ((), ())), precision=HI,
                                        preferred_element_type=F32)
        gb_ref[...] = jnp.sum(da_ref[...], axis=0, keepdims=True)

    return pl.pallas_call(
        body, name=name,
        out_shape=(jax.ShapeDtypeStruct((nl, d, n), F32), jax.ShapeDtypeStruct((1, dmod_all.shape[1]), F32)),
        compiler_params=_cp())(c_all, dmod_chip, dmod_all)


def _norm_proj(x, g, scale, shift, w, f32_cols, name):
    b, s, d = x.shape
    n = w.shape[1]
    tm = min(s, ROW_TILE)

    def body(x_ref, g_ref, sc_ref, sh_ref, w_ref, proj_ref, h_ref, *aux_ref):
        xv = x_ref[...]
        rstd = lax.rsqrt(jnp.mean(xv * xv, axis=-1, keepdims=True) + EPS)
        h = (xv * rstd) * g_ref[...] * (1.0 + sc_ref[...]) + sh_ref[...]
        hb = h.astype(BF16)
        h_ref[...] = hb
        proj = _dot(hb, w_ref[...])
        proj_ref[...] = proj.astype(BF16)
        if f32_cols:
            aux_ref[0][...] = proj[:, n - f32_cols:]

    row = lambda i, j: (i, j, 0)
    out_shape = [jax.ShapeDtypeStruct((b, s, n), BF16), jax.ShapeDtypeStruct((b, s, d), BF16)]
    out_specs = [pl.BlockSpec((None, tm, n), row), pl.BlockSpec((None, tm, d), row)]
    if f32_cols:
        out_shape.append(jax.ShapeDtypeStruct((b, s, f32_cols), F32))
        out_specs.append(pl.BlockSpec((None, tm, f32_cols), row))
    return pl.pallas_call(
        body, name=name, grid=(b, s // tm),
        out_shape=tuple(out_shape),
        in_specs=[pl.BlockSpec((None, tm, d), row),
                  pl.BlockSpec((1, d), lambda i, j: (0, 0)),
                  pl.BlockSpec((None, 1, d), lambda i, j: (i, 0, 0)),
                  pl.BlockSpec((None, 1, d), lambda i, j: (i, 0, 0)),
                  pl.BlockSpec((d, n), lambda i, j: (0, 0))],
        out_specs=tuple(out_specs),
        compiler_params=_cp(("parallel", "parallel")),
    )(x, g, scale, shift, w)


def _cat_refs(refs):
    vals = [r[...] for r in refs]
    return vals[0] if len(vals) == 1 else jnp.concatenate(vals, axis=-1)


def _gate_outproj(mix_parts, proj, gate_blk, w_out, x, gmod, name):
    b, s, _ = x.shape
    wd, d = w_out.shape
    tm = min(s, ROW_TILE)
    npart = len(mix_parts)

    def body(*refs):
        mix_refs = refs[:npart]
        gate_ref, w_ref, x_ref, gm_ref, xo_ref, o_ref = refs[npart:]
        gt = gate_ref[...].astype(F32)
        y = (_cat_refs(mix_refs) * (gt * _sigmoid(gt))).astype(BF16)
        o = _dot(y, w_ref[...])
        o_ref[...] = o.astype(BF16)
        xo_ref[...] = x_ref[...] + gm_ref[...] * o

    return pl.pallas_call(
        body, name=name, grid=(b, s // tm),
        out_shape=(jax.ShapeDtypeStruct((b, s, d), F32), jax.ShapeDtypeStruct((b, s, d), BF16)),
        in_specs=[pl.BlockSpec((None, tm, p.shape[2]), lambda i, j: (i, j, 0)) for p in mix_parts] + [
                  pl.BlockSpec((None, tm, wd), lambda i, j: (i, j, gate_blk)),
                  pl.BlockSpec((wd, d), lambda i, j: (0, 0)),
                  pl.BlockSpec((None, tm, d), lambda i, j: (i, j, 0)),
                  pl.BlockSpec((None, 1, d), lambda i, j: (i, 0, 0))],
        out_specs=(pl.BlockSpec((None, tm, d), lambda i, j: (i, j, 0)),
                   pl.BlockSpec((None, tm, d), lambda i, j: (i, j, 0))),
        compiler_params=_cp(("parallel", "parallel")),
    )(*mix_parts, proj, w_out, x, gmod)


def _final_loss(x, g, target, name):
    b, s, d = x.shape
    tm = min(s, ROW_TILE)

    def body(x_ref, g_ref, t_ref, loss_ref, dx_ref, dg_ref):
        first = jnp.logical_and(pl.program_id(0) == 0, pl.program_id(1) == 0)

        @pl.when(first)
        def _():
            loss_ref[...] = jnp.zeros_like(loss_ref)
            dg_ref[...] = jnp.zeros_like(dg_ref)

        xv = x_ref[...]
        gv = g_ref[...]
        rstd = lax.rsqrt(jnp.mean(xv * xv, axis=-1, keepdims=True) + EPS)
        xhat = xv * rstd
        err = xhat * gv - t_ref[...]
        row = jnp.mean(err * err, axis=-1, keepdims=True)
        loss_ref[...] += 0.5 * jnp.sum(row, axis=0, keepdims=True)
        dy = err * (1.0 / d)
        dg_ref[...] += jnp.sum(dy * xhat, axis=0, keepdims=True)
        dxh = dy * gv
        dx_ref[...] = rstd * (dxh - xhat * jnp.mean(dxh * xhat, axis=-1, keepdims=True))

    return pl.pallas_call(
        body, name=name, grid=(b, s // tm),
        out_shape=(jax.ShapeDtypeStruct((1, LANE), F32), jax.ShapeDtypeStruct((b, s, d), F32),
                   jax.ShapeDtypeStruct((1, d), F32)),
        in_specs=[pl.BlockSpec((None, tm, d), lambda i, j: (i, j, 0)),
                  pl.BlockSpec((1, d), lambda i, j: (0, 0)),
                  pl.BlockSpec((None, tm, d), lambda i, j: (i, j, 0))],
        out_specs=(pl.BlockSpec((1, LANE), lambda i, j: (0, 0)),
                   pl.BlockSpec((None, tm, d), lambda i, j: (i, j, 0)),
                   pl.BlockSpec((1, d), lambda i, j: (0, 0))),
        compiler_params=_cp(("arbitrary", "arbitrary")),
    )(x, g, target)


def _bwd_out(dxo, gmod, o, mix_parts, proj, gate_blk, w_out, dmix_dtype, name):
    b, s, d = dxo.shape
    wd = w_out.shape[0]
    tm = min(s, ROW_TILE)
    npart = len(mix_parts)

    def body(dx_ref, gm_ref, o_ref, *refs):
        mix_refs = refs[:npart]
        gate_ref, wt_ref, dmix_ref, dgate_ref, do_ref, y_ref, dgm_ref = refs[npart:]

        @pl.when(pl.program_id(1) == 0)
        def _():
            dgm_ref[...] = jnp.zeros_like(dgm_ref)

        dx = dx_ref[...]
        dgm_ref[...] += jnp.sum(dx * o_ref[...].astype(F32), axis=0, keepdims=True)
        dob = (gm_ref[...] * dx).astype(BF16)
        do_ref[...] = dob
        dy = _dot_nt(dob, wt_ref[...])
        gt = gate_ref[...].astype(F32)
        sg = _sigmoid(gt)
        silu = gt * sg
        mx = _cat_refs(mix_refs)
        y_ref[...] = (mx * silu).astype(BF16)
        dmix_ref[...] = (dy * silu).astype(dmix_dtype)
        dgate_ref[...] = (dy * mx * (sg * (1.0 + gt * (1.0 - sg)))).astype(BF16)

    row = lambda i, j: (i, j, 0)
    return pl.pallas_call(
        body, name=name, grid=(b, s // tm),
        out_shape=(jax.ShapeDtypeStruct((b, s, wd), dmix_dtype), jax.ShapeDtypeStruct((b, s, wd), BF16),
                   jax.ShapeDtypeStruct((b, s, d), BF16), jax.ShapeDtypeStruct((b, s, wd), BF16),
                   jax.ShapeDtypeStruct((b, 1, d), F32)),
        in_specs=[pl.BlockSpec((None, tm, d), row),
                  pl.BlockSpec((None, 1, d), lambda i, j: (i, 0, 0)),
                  pl.BlockSpec((None, tm, d), row)] + [
                  pl.BlockSpec((None, tm, p.shape[2]), row) for p in mix_parts] + [
                  pl.BlockSpec((None, tm, wd), lambda i, j: (i, j, gate_blk)),
                  pl.BlockSpec((wd, d), lambda i, j: (0, 0))],
        out_specs=(pl.BlockSpec((None, tm, wd), row), pl.BlockSpec((None, tm, wd), row),
                   pl.BlockSpec((None, tm, d), row), pl.BlockSpec((None, tm, wd), row),
                   pl.BlockSpec((None, 1, d), lambda i, j: (i, 0, 0))),
        compiler_params=_cp(("parallel", "arbitrary")),
    )(dxo, gmod, o, *mix_parts, proj, w_out)


def _bwd_in(dproj_parts, w_in, x, g, scale, dxo, name):
    b, s, d = x.shape
    n = w_in.shape[1]
    tm = min(s, ROW_TILE)
    npart = len(dproj_parts)

    def body(*refs):
        dp_refs = refs[:npart]
        wt_ref, x_ref, g_ref, sc_ref, dxo_ref, dx_ref, dsh_ref, dsc_ref, dg_ref = refs[npart:]

        @pl.when(jnp.logical_and(pl.program_id(0) == 0, pl.program_id(1) == 0))
        def _():
            dg_ref[...] = jnp.zeros_like(dg_ref)

        @pl.when(pl.program_id(1) == 0)
        def _():
            dsh_ref[...] = jnp.zeros_like(dsh_ref)
            dsc_ref[...] = jnp.zeros_like(dsc_ref)

        dh = _dot_nt(_cat_refs(dp_refs), wt_ref[...])
        xv = x_ref[...]
        gv = g_ref[...]
        one_sc = 1.0 + sc_ref[...]
        rstd = lax.rsqrt(jnp.mean(xv * xv, axis=-1, keepdims=True) + EPS)
        xhat = xv * rstd
        dsh_ref[...] += jnp.sum(dh, axis=0, keepdims=True)
        dsc_ref[...] += jnp.sum(dh * (xhat * gv), axis=0, keepdims=True)
        dhs = dh * one_sc
        dg_ref[...] += jnp.sum(dhs * xhat, axis=0, keepdims=True)
        dxh = dhs * gv
        dx_ref[...] = dxo_ref[...] + rstd * (dxh - xhat * jnp.mean(dxh * xhat, axis=-1, keepdims=True))

    row = lambda i, j: (i, j, 0)
    per_b = lambda i, j: (i, 0, 0)
    return pl.pallas_call(
        body, name=name, grid=(b, s // tm),
        out_shape=(jax.ShapeDtypeStruct((b, s, d), F32), jax.ShapeDtypeStruct((b, 1, d), F32),
                   jax.ShapeDtypeStruct((b, 1, d), F32), jax.ShapeDtypeStruct((1, d), F32)),
        in_specs=[pl.BlockSpec((None, tm, p.shape[2]), row) for p in dproj_parts] + [
                  pl.BlockSpec((d, n), lambda i, j: (0, 0)),
                  pl.BlockSpec((None, tm, d), row),
                  pl.BlockSpec((1, d), lambda i, j: (0, 0)),
                  pl.BlockSpec((None, 1, d), per_b),
                  pl.BlockSpec((None, tm, d), row)],
        out_specs=(pl.BlockSpec((None, tm, d), row), pl.BlockSpec((None, 1, d), per_b),
                   pl.BlockSpec((None, 1, d), per_b), pl.BlockSpec((1, d), lambda i, j: (0, 0))),
        compiler_params=_cp(("arbitrary", "arbitrary")),
    )(*dproj_parts, w_in, x, g, scale, dxo)


def _matmul_tn(a, b_parts, name):
    bsz, s, m = a.shape
    n = sum(p.shape[2] for p in b_parts)
    tk = next(c for c in (512, 256, 128) if s % c == 0)
    npart = len(b_parts)

    def body(a_ref, *refs):
        b_refs, o_ref = refs[:npart], refs[npart]

        @pl.when(jnp.logical_and(pl.program_id(0) == 0, pl.program_id(1) == 0))
        def _():
            o_ref[...] = jnp.zeros_like(o_ref)

        o_ref[...] += _dot_tn(a_ref[...], _cat_refs(b_refs))

    row = lambda i, k: (i, k, 0)
    return pl.pallas_call(
        body, name=name, grid=(bsz, s // tk),
        out_shape=jax.ShapeDtypeStruct((m, n), F32),
        in_specs=[pl.BlockSpec((None, tk, m), row)] + [pl.BlockSpec((None, tk, p.shape[2]), row) for p in b_parts],
        out_specs=pl.BlockSpec((m, n), lambda i, k: (0, 0)),
        compiler_params=_cp(("arbitrary", "arbitrary")),
    )(a, *b_parts)


def _rel_buckets():
    qi = np.arange(BLK)[:, None]
    kj = np.arange(2 * BLK)[None, :]
    rel = qi - kj + BLK
    n = np.maximum(rel, 0)
    nf = np.maximum(n, 1).astype(np.float32)
    large = REL_MAX_EXACT + (np.log(nf / REL_MAX_EXACT) / math.log(REL_MAX_DIST / REL_MAX_EXACT)
                             * (REL_BUCKETS - REL_MAX_EXACT)).astype(np.int32)
    large = np.minimum(large, REL_BUCKETS - 1)
    bucket = np.where(n < REL_MAX_EXACT, n, large).astype(np.int32)
    valid = ((rel >= 0) & (rel < BLK)).astype(np.int32)
    return bucket, valid


def _swa_bias(rel_bias_t, bucket, valid, name):
    nh = rel_bias_t.shape[0]

    def body(rb_ref, bk_ref, vl_ref, o_ref):
        h = pl.program_id(0)
        bk = bk_ref[...]
        acc = jnp.zeros(bk.shape, F32)
        for i in range(REL_BUCKETS):
            acc = jnp.where(bk == i, rb_ref[h, i], acc)
        o_ref[...] = jnp.where(vl_ref[...] > 0, acc, NEG)

    return pl.pallas_call(
        body, name=name, grid=(nh,),
        out_shape=jax.ShapeDtypeStruct((nh, BLK, 2 * BLK), F32),
        in_specs=[pl.BlockSpec(memory_space=pltpu.SMEM),
                  pl.BlockSpec((BLK, 2 * BLK), lambda h: (0, 0)),
                  pl.BlockSpec((BLK, 2 * BLK), lambda h: (0, 0))],
        out_specs=pl.BlockSpec((None, BLK, 2 * BLK), lambda h: (h, 0, 0)),
        compiler_params=_cp(("arbitrary",)),
    )(rel_bias_t, bucket, valid)


def _swa_scores(n, q, kw, bias_ref):
    sc = _dot_nt(q, kw) * SCALE + bias_ref[...]
    second = lax.broadcasted_iota(jnp.int32, sc.shape, 1) >= BLK
    return jnp.where(jnp.logical_or(n > 0, second), sc, NEG)


def _pad_front(dst_ref, src_ref):
    dst_ref[0:BLK, :] = jnp.zeros((BLK, dst_ref.shape[1]), dst_ref.dtype)
    dst_ref[BLK:, :] = src_ref[...]


def _swa_fwd(q, k, v, bias, sinks, name):
    b, hkv, nb, rows, hd = q.shape
    wide = bias.shape[2]
    stride = wide - BLK
    s = nb * stride

    def body(q_ref, k_ref, v_ref, bias_ref, sink_ref, o_ref, l_ref, kpad_ref, vpad_ref):
        _pad_front(kpad_ref, k_ref)
        _pad_front(vpad_ref, v_ref)
        sink = sink_ref[...]

        def step(n, carry):
            w0 = pl.multiple_of(n * stride, BLK)
            sc = _swa_scores(n, q_ref[n], kpad_ref[pl.ds(w0, wide), :], bias_ref)
            m = jnp.maximum(jnp.max(sc, axis=1, keepdims=True), sink)
            e = jnp.exp(sc - m)
            den = jnp.sum(e, axis=1, keepdims=True) + jnp.exp(sink - m)
            o_ref[n] = _dot((e * (1.0 / den)).astype(BF16), vpad_ref[pl.ds(w0, wide), :]).astype(BF16)
            l_ref[n] = m + jnp.log(den)
            return carry

        lax.fori_loop(0, nb, step, 0)

    qspec = pl.BlockSpec((None, None, nb, rows, hd), lambda i, kv: (i, kv, 0, 0, 0))
    kspec = pl.BlockSpec((None, None, s, hd), lambda i, kv: (i, kv, 0, 0))
    return pl.pallas_call(
        body, name=name, grid=(b, hkv),
        out_shape=(jax.ShapeDtypeStruct((b, hkv, nb, rows, hd), BF16), jax.ShapeDtypeStruct((b, hkv, nb, rows, 1), F32)),
        in_specs=[qspec, kspec, kspec,
                  pl.BlockSpec((None, rows, wide), lambda i, kv: (kv, 0, 0)),
                  pl.BlockSpec((None, rows, 1), lambda i, kv: (kv, 0, 0))],
        out_specs=(qspec, pl.BlockSpec((None, None, nb, rows, 1), lambda i, kv: (i, kv, 0, 0, 0))),
        scratch_shapes=[pltpu.VMEM((s + BLK, hd), BF16), pltpu.VMEM((s + BLK, hd), BF16)],
        compiler_params=_cp(("parallel", "parallel")),
    )(q, k, v, bias, sinks)


def _swa_bwd(q, k, v, bias, sinks, do, lse, name):
    b, hkv, nb, rows, hd = q.shape
    wide = bias.shape[2]
    stride = wide - BLK
    s = nb * stride

    def body(q_ref, k_ref, v_ref, bias_ref, sink_ref, do_ref, l_ref,
             dq_ref, dk_ref, dv_ref, db_ref, dsk_ref, kpad_ref, vpad_ref, dkpad_ref, dvpad_ref):
        _pad_front(kpad_ref, k_ref)
        _pad_front(vpad_ref, v_ref)
        dkpad_ref[...] = jnp.zeros_like(dkpad_ref)
        dvpad_ref[...] = jnp.zeros_like(dvpad_ref)
        db_ref[...] = jnp.zeros_like(db_ref)
        sink = sink_ref[...]

        def step(n, dsink):
            w0 = pl.multiple_of(n * stride, BLK)
            win = pl.ds(w0, wide)
            qn = q_ref[n]
            kw = kpad_ref[win, :]
            ln = l_ref[n]
            p = jnp.exp(_swa_scores(n, qn, kw, bias_ref) - ln)
            dob = do_ref[n]
            dp = _dot_nt(dob, vpad_ref[win, :])
            delta = jnp.sum(p * dp, axis=1, keepdims=True)
            ds = p * (dp - delta)
            db_ref[...] += ds
            dsb = ds.astype(BF16)
            dq_ref[n] = (_dot(dsb, kw) * SCALE).astype(BF16)
            dkpad_ref[win, :] += _dot_tn(dsb, qn)
            dvpad_ref[win, :] += _dot_tn(p.astype(BF16), dob)
            return dsink - jnp.exp(sink - ln) * delta

        dsink = lax.fori_loop(0, nb, step, jnp.zeros((rows, 1), F32))
        for g in range(A_GROUP):
            tot = jnp.zeros((1, 1), F32)
            for blk in range(rows // (A_GROUP * BLK)):
                r0 = (blk * A_GROUP + g) * BLK
                tot = tot + jnp.sum(dsink[r0:r0 + BLK, :], axis=0, keepdims=True)
            dsk_ref[g] = jnp.broadcast_to(tot, (1, LANE))
        dk_ref[...] = (dkpad_ref[BLK:, :] * SCALE).astype(BF16)
        dv_ref[...] = dvpad_ref[BLK:, :].astype(BF16)

    qspec = pl.BlockSpec((None, None, nb, rows, hd), lambda i, kv: (i, kv, 0, 0, 0))
    kspec = pl.BlockSpec((None, None, s, hd), lambda i, kv: (i, kv, 0, 0))
    return pl.pallas_call(
        body, name=name, grid=(b, hkv),
        out_shape=(jax.ShapeDtypeStruct((b, hkv, nb, rows, hd), BF16), jax.ShapeDtypeStruct((b, hkv, s, hd), BF16),
                   jax.ShapeDtypeStruct((b, hkv, s, hd), BF16), jax.ShapeDtypeStruct((b, hkv, rows, wide), F32),
                   jax.ShapeDtypeStruct((b, hkv, A_GROUP, 1, LANE), F32)),
        in_specs=[qspec, kspec, kspec,
                  pl.BlockSpec((None, rows, wide), lambda i, kv: (kv, 0, 0)),
                  pl.BlockSpec((None, rows, 1), lambda i, kv: (kv, 0, 0)),
                  qspec,
                  pl.BlockSpec((None, None, nb, rows, 1), lambda i, kv: (i, kv, 0, 0, 0))],
        out_specs=(qspec, kspec, kspec,
                   pl.BlockSpec((None, None, rows, wide), lambda i, kv: (i, kv, 0, 0)),
                   pl.BlockSpec((None, None, A_GROUP, 1, LANE), lambda i, kv: (i, kv, 0, 0, 0))),
        scratch_shapes=[pltpu.VMEM((s + BLK, hd), BF16), pltpu.VMEM((s + BLK, hd), BF16),
                        pltpu.VMEM((s + BLK, hd), F32), pltpu.VMEM((s + BLK, hd), F32)],
        compiler_params=_cp(("parallel", "parallel")),
    )(q, k, v, bias, sinks, do, lse)


def _swa_small_grads(db, dsk, bucket, name):
    b, nh = db.shape[0], db.shape[1]

    def body(db_ref, dsk_ref, bk_ref, gb_ref, gs_ref):
        acc = db_ref[0]
        sk = dsk_ref[0]
        for i in range(1, b):
            acc = acc + db_ref[i]
            sk = sk + dsk_ref[i]
        gs_ref[...] = sk
        bk = bk_ref[...]
        for i in range(REL_BUCKETS):
            part = jnp.sum(jnp.where(bk == i, acc, 0.0), axis=1, keepdims=True)
            tot = jnp.sum(part, axis=0, keepdims=True)
            gb_ref[i:i + 1, :] = jnp.broadcast_to(tot, (1, LANE))

    return pl.pallas_call(
        body, name=name, grid=(nh,),
        out_shape=(jax.ShapeDtypeStruct((nh, REL_BUCKETS, LANE), F32), jax.ShapeDtypeStruct((nh, 1, LANE), F32)),
        in_specs=[pl.BlockSpec((b, None, BLK, 2 * BLK), lambda h: (0, h, 0, 0)),
                  pl.BlockSpec((b, None, 1, LANE), lambda h: (0, h, 0, 0)),
                  pl.BlockSpec((BLK, 2 * BLK), lambda h: (0, 0))],
        out_specs=(pl.BlockSpec((None, REL_BUCKETS, LANE), lambda h: (h, 0, 0)),
                   pl.BlockSpec((None, 1, LANE), lambda h: (h, 0, 0))),
        compiler_params=_cp(("parallel",)),
    )(db, dsk, bucket)


def _log_sigmoid(z):
    return jnp.minimum(z, 0.0) - jnp.log(1.0 + jnp.exp(-jnp.abs(z)))


def _fox_decay(z, bf, name):
    b, s, w = z.shape
    nb = s // BLK

    def body(z_ref, bf_ref, f_ref):
        r = lax.broadcasted_iota(jnp.int32, (BLK, BLK), 0)
        c = lax.broadcasted_iota(jnp.int32, (BLK, BLK), 1)
        tri = (c <= r).astype(F32)

        def step(n, carry):
            r0 = pl.multiple_of(n * BLK, BLK)
            lf = _log_sigmoid(z_ref[pl.ds(r0, BLK), :] + bf_ref[...])
            f_ref[pl.ds(r0, BLK), :] = jnp.dot(tri, lf, precision=HI, preferred_element_type=F32) + carry
            return carry + jnp.sum(lf, axis=0, keepdims=True)

        lax.fori_loop(0, nb, step, jnp.zeros((1, w), F32))

    spec = pl.BlockSpec((None, s, w), lambda i: (i, 0, 0))
    return pl.pallas_call(
        body, name=name, grid=(b,), out_shape=jax.ShapeDtypeStruct((b, s, w), F32),
        in_specs=[spec, pl.BlockSpec((1, w), lambda i: (0, 0))], out_specs=spec,
        compiler_params=_cp(("parallel",)),
    )(z, bf)


def _fox_dgate(df, z, bf, nheads, name):
    b, s, w = z.shape
    nb = s // BLK

    def body(df_ref, z_ref, bf_ref, dz_ref, dbf_ref):
        @pl.when(pl.program_id(0) == 0)
        def _():
            dbf_ref[...] = jnp.zeros_like(dbf_ref)

        r = lax.broadcasted_iota(jnp.int32, (BLK, BLK), 0)
        c = lax.broadcasted_iota(jnp.int32, (BLK, BLK), 1)
        tri = (c >= r).astype(F32)
        lane = lax.broadcasted_iota(jnp.int32, (BLK, w), 1)

        def step(i, carry):
            tail, dbf = carry
            r0 = pl.multiple_of((nb - 1 - i) * BLK, BLK)
            dfb = df_ref[pl.ds(r0, BLK), :]
            dlf = jnp.dot(tri, dfb, precision=HI, preferred_element_type=F32) + tail
            dz = jnp.where(lane < nheads, dlf * _sigmoid(-(z_ref[pl.ds(r0, BLK), :] + bf_ref[...])), 0.0)
            dz_ref[pl.ds(r0, BLK), :] = dz
            return tail + jnp.sum(dfb, axis=0, keepdims=True), dbf + jnp.sum(dz, axis=0, keepdims=True)

        zero = jnp.zeros((1, w), F32)
        _, dbf = lax.fori_loop(0, nb, step, (zero, zero))
        dbf_ref[...] += dbf

    spec = pl.BlockSpec((None, s, w), lambda i: (i, 0, 0))
    one = pl.BlockSpec((1, w), lambda i: (0, 0))
    return pl.pallas_call(
        body, name=name, grid=(b,),
        out_shape=(jax.ShapeDtypeStruct((b, s, w), F32), jax.ShapeDtypeStruct((1, w), F32)),
        in_specs=[spec, spec, one], out_specs=(spec, one),
        compiler_params=_cp(("arbitrary",)),
    )(df, z, bf)


def _fox_segments(nb):
    per = max(1, nb // 4)
    return per, nb // per


def _head_masks(shape, axis):
    idx = lax.broadcasted_iota(jnp.int32, shape, axis)
    return idx < HEAD_DIM, idx >= HEAD_DIM


def _fox_fwd(proj, qblk, kblk, vblk, fcol, frow, name):
    b, s, _ = proj.shape
    nh = frow.shape[1]
    npair = nh // 2
    BLK = min(s, FOX_BQ)
    assert s % BLK == 0
    per, nseg = _fox_segments(s // BLK)

    def body(q_ref, k_ref, v_ref, fc_ref, fr_ref, o_ref, l_ref, qm_ref, kt_ref, vb_ref):
        lo, hi = _head_masks((s, LANE), 1)
        qv = q_ref[...].astype(F32) * SCALE
        qm_ref[0] = jnp.where(lo, qv, 0.0).astype(BF16)
        qm_ref[1] = jnp.where(hi, qv, 0.0).astype(BF16)
        kt_ref[...] = k_ref[...].astype(F32).T.astype(BF16)
        vb_ref[...] = v_ref[...].astype(BF16)
        lane_lo = lax.broadcasted_iota(jnp.int32, (BLK, LANE), 1) < HEAD_DIM
        tail = per * BLK
        causal = (lax.broadcasted_iota(jnp.int32, (BLK, tail), 1)
                  - lax.broadcasted_iota(jnp.int32, (BLK, tail), 0))
        for seg in range(nseg):
            w = (seg + 1) * tail

            def qstep(n, carry):
                r0 = pl.multiple_of(n * BLK, BLK)
                outs = []
                for hh in range(2):
                    sc = _dot(qm_ref[hh, pl.ds(r0, BLK), :], kt_ref[:, :w])
                    sc = sc + (fc_ref[hh, pl.ds(r0, BLK), :] - fr_ref[hh, :, :w])
                    masked = jnp.where(causal <= (n - seg * per) * BLK, sc[:, w - tail:], NEG)
                    sc = masked if seg == 0 else jnp.concatenate([sc[:, :w - tail], masked], axis=1)
                    m = jnp.max(sc, axis=1, keepdims=True)
                    e = jnp.exp(sc - m)
                    l = jnp.sum(e, axis=1, keepdims=True)
                    outs.append(_dot((e * (1.0 / l)).astype(BF16), vb_ref[:w, :]))
                    l_ref[hh, pl.ds(r0, BLK), :] = m + jnp.log(l)
                o_ref[pl.ds(r0, BLK), :] = jnp.where(lane_lo, outs[0], outs[1]).astype(BF16)
                return carry

            lax.fori_loop(seg * per, (seg + 1) * per, qstep, 0)

    def tok(blk):
        return pl.BlockSpec((None, s, LANE), lambda i, p: (i, 0, blk + p))

    col = pl.BlockSpec((None, 2, s, 1), lambda i, p: (i, p, 0, 0))
    rowspec = pl.BlockSpec((None, 2, 1, s), lambda i, p: (i, p, 0, 0))
    return pl.pallas_call(
        body, name=name, grid=(b, npair),
        out_shape=(jax.ShapeDtypeStruct((b, s, nh * HEAD_DIM), BF16), jax.ShapeDtypeStruct((b, nh, s, 1), F32)),
        in_specs=[tok(qblk), tok(kblk), tok(vblk), col, rowspec],
        out_specs=(pl.BlockSpec((None, s, LANE), lambda i, p: (i, 0, p)), col),
        scratch_shapes=[pltpu.VMEM((2, s, LANE), BF16), pltpu.VMEM((LANE, s), BF16), pltpu.VMEM((s, LANE), BF16)],
        compiler_params=_cp(("parallel", "parallel")),
    )(proj, proj, proj, fcol, frow)


def _fox_bwd(proj, qblk, kblk, vblk, dmix, doblk, fcol, frow, frowb, lse, lserowb, name):
    b, s, _ = proj.shape
    nh = frow.shape[1]
    npair = nh // 2
    BLK = min(s, FOX_BQ)
    assert s % BLK == 0
    nb = s // BLK
    per, nseg = _fox_segments(nb)

    def body(q_ref, k_ref, v_ref, do_ref, fc_ref, fr_ref, frb_ref, l_ref, lrb_ref,
             dq_ref, dk_ref, dv_ref, dfr_ref,
             qm_ref, dom_ref, kb_ref, vb_ref, kt_ref, vt_ref, qtm_ref, dotm_ref, dka_ref, dva_ref):
        lo, hi = _head_masks((s, LANE), 1)
        qv = q_ref[...].astype(F32) * SCALE
        dov = do_ref[...]
        for hh, msk in enumerate((lo, hi)):
            qm_ref[hh] = jnp.where(msk, qv, 0.0).astype(BF16)
            dom_ref[hh] = jnp.where(msk, dov, 0.0).astype(BF16)
        kv = k_ref[...].astype(F32)
        vv = v_ref[...].astype(F32)
        kb_ref[...] = kv.astype(BF16)
        vb_ref[...] = vv.astype(BF16)
        kt_ref[...] = kv.T.astype(BF16)
        vt_ref[...] = vv.T.astype(BF16)
        rlo, rhi = _head_masks((LANE, BLK), 0)

        def tstep(n, carry):
            r0 = pl.multiple_of(n * BLK, BLK)
            qt = (q_ref[pl.ds(r0, BLK), :].astype(F32) * SCALE).T
            dt = do_ref[pl.ds(r0, BLK), :].astype(F32).T
            for hh, msk in enumerate((rlo, rhi)):
                qtm_ref[hh, n] = jnp.where(msk, qt, 0.0).astype(BF16)
                dotm_ref[hh, n] = jnp.where(msk, dt, 0.0).astype(BF16)
            return carry

        lax.fori_loop(0, nb, tstep, 0)
        dka_ref[...] = jnp.zeros_like(dka_ref)
        dva_ref[...] = jnp.zeros_like(dva_ref)
        dfr_ref[...] = jnp.zeros_like(dfr_ref)
        lane_lo = lax.broadcasted_iota(jnp.int32, (BLK, LANE), 1) < HEAD_DIM
        tail = per * BLK
        causal = (lax.broadcasted_iota(jnp.int32, (BLK, tail), 1)
                  - lax.broadcasted_iota(jnp.int32, (BLK, tail), 0))
        causal_t = (lax.broadcasted_iota(jnp.int32, (tail, BLK), 0)
                    - lax.broadcasted_iota(jnp.int32, (tail, BLK), 1))
        for seg in range(nseg):
            w = (seg + 1) * tail

            def nstep(n, carry):
                r0 = pl.multiple_of(n * BLK, BLK)
                lim = (n - seg * per) * BLK
                dqs = []
                for hh in range(2):
                    qn = qm_ref[hh, pl.ds(r0, BLK), :]
                    don = dom_ref[hh, pl.ds(r0, BLK), :]
                    sc = _dot(qn, kt_ref[:, :w]) + ((fc_ref[hh, pl.ds(r0, BLK), :] - l_ref[hh, pl.ds(r0, BLK), :])
                                                   - fr_ref[hh, :, :w])
                    masked = jnp.where(causal <= lim, sc[:, w - tail:], NEG)
                    p = jnp.exp(masked if seg == 0 else jnp.concatenate([sc[:, :w - tail], masked], axis=1))
                    dp = _dot(don, vt_ref[:, :w])
                    ds = p * (dp - jnp.sum(p * dp, axis=1, keepdims=True))
                    dqs.append(_dot(ds.astype(BF16), kb_ref[:w, :]))
                    dfr_ref[hh, :, :w] -= jnp.sum(ds, axis=0, keepdims=True)
                    sct = _dot(kb_ref[:w, :], qtm_ref[hh, n]) + ((frb_ref[hh, n] - lrb_ref[hh, n]) - fc_ref[hh, :w, :])
                    masked_t = jnp.where(causal_t <= lim, sct[w - tail:, :], NEG)
                    pt = jnp.exp(masked_t if seg == 0 else jnp.concatenate([sct[:w - tail, :], masked_t], axis=0))
                    dpt = _dot(vb_ref[:w, :], dotm_ref[hh, n])
                    dst = pt * (dpt - jnp.sum(pt * dpt, axis=0, keepdims=True))
                    dka_ref[:w, :] += _dot(dst.astype(BF16), qn)
                    dva_ref[:w, :] += _dot(pt.astype(BF16), don)
                dq_ref[pl.ds(r0, BLK), :] = (jnp.where(lane_lo, dqs[0], dqs[1]) * SCALE).astype(BF16)
                return carry

            lax.fori_loop(seg * per, (seg + 1) * per, nstep, 0)
        dk_ref[...] = dka_ref[...].astype(BF16)
        dv_ref[...] = dva_ref[...].astype(BF16)

    def tok(blk):
        return pl.BlockSpec((None, s, LANE), lambda i, p: (i, 0, blk + p))

    col = pl.BlockSpec((None, 2, s, 1), lambda i, p: (i, p, 0, 0))
    rowspec = pl.BlockSpec((None, 2, 1, s), lambda i, p: (i, p, 0, 0))
    rowbspec = pl.BlockSpec((None, 2, nb, 1, BLK), lambda i, p: (i, p, 0, 0, 0))
    outtok = pl.BlockSpec((None, s, LANE), lambda i, p: (i, 0, p))
    shp = jax.ShapeDtypeStruct((b, s, nh * HEAD_DIM), BF16)
    return pl.pallas_call(
        body, name=name, grid=(b, npair),
        out_shape=(shp, shp, shp, jax.ShapeDtypeStruct((b, nh, 1, s), F32)),
        in_specs=[tok(qblk), tok(kblk), tok(vblk),
                  pl.BlockSpec((None, s, LANE), lambda i, p: (i, 0, doblk + p)),
                  col, rowspec, rowbspec, col, rowbspec],
        out_specs=(outtok, outtok, outtok, rowspec),
        scratch_shapes=[pltpu.VMEM((2, s, LANE), BF16), pltpu.VMEM((2, s, LANE), BF16),
                        pltpu.VMEM((s, LANE), BF16), pltpu.VMEM((s, LANE), BF16),
                        pltpu.VMEM((LANE, s), BF16), pltpu.VMEM((LANE, s), BF16),
                        pltpu.VMEM((2, nb, LANE, BLK), BF16), pltpu.VMEM((2, nb, LANE, BLK), BF16),
                        pltpu.VMEM((s, LANE), F32), pltpu.VMEM((s, LANE), F32)],
        compiler_params=_cp(("parallel", "parallel")),
    )(proj, proj, proj, dmix, fcol, frow, frowb, lse, lserowb)


def _expm1(x):
    poly = x * (1.0 + x * (1.0 / 2 + x * (1.0 / 6 + x * (1.0 / 24 + x * (1.0 / 120 + x * (1.0 / 720))))))
    return jnp.where(x > -0.1, poly, jnp.exp(x) - 1.0)


def _softplus(z):
    return jnp.maximum(z, 0.0) + jnp.log(1.0 + jnp.exp(-jnp.abs(z)))


def _scan_rows(a, u, carry, row, up):
    tc, c = a.shape
    d = 1
    while d < tc:
        if d < SUBLANE:
            keep = (row >= d) if up else (row < tc - d)
            shift = d if up else tc - d
            a_sh = jnp.where(keep, pltpu.roll(a, shift, 0), 1.0)
            u_sh = jnp.where(keep, pltpu.roll(u, shift, 0), 0.0)
        elif up:
            a_sh = jnp.concatenate([jnp.ones((d, c), F32), a[:tc - d]], axis=0)
            u_sh = jnp.concatenate([jnp.zeros((d, c), F32), u[:tc - d]], axis=0)
        else:
            a_sh = jnp.concatenate([a[d:], jnp.ones((d, c), F32)], axis=0)
            u_sh = jnp.concatenate([u[d:], jnp.zeros((d, c), F32)], axis=0)
        u = a * u_sh + u
        a = a * a_sh
        d *= 2
    return u + a * carry


def _scan_up(a, u, carry, row):
    return _scan_rows(a, u, carry, row, True)


def _scan_down(bnext, g, carry, row):
    return _scan_rows(bnext, g, carry, row, False)


def _pick_row(val, row, which):
    return jnp.sum(jnp.where(row == which, val, 0.0), axis=0, keepdims=True)


def _lru_gates(xpad_ref, t0, tc, cw_ref, cb_ref, wa, ba_ref, wx, bx_ref, sp):
    xw = xpad_ref[pl.ds(t0, tc + SUBLANE), :]
    xc = cb_ref[...]
    for j in range(CONV_WIDTH):
        sh = CONV_WIDTH - 1 - j
        xs = xw if sh == 0 else pltpu.roll(xw, sh, 0)
        xc = xc + xs[SUBLANE:, :] * cw_ref[j:j + 1, :]
    xcb = xc.astype(BF16)
    r = _sigmoid(_dot(xcb, wa) + ba_ref[...])
    i = _sigmoid(_dot(xcb, wx) + bx_ref[...])
    la = -LRU_C * r * sp
    return xc, r, i, la


def _lru_specs(s, cb):
    seq = lambda bi, ni: (bi, 0, ni)
    return dict(
        seq=pl.BlockSpec((None, s, cb), seq),
        cw=pl.BlockSpec((CONV_WIDTH, cb), lambda bi, ni: (0, ni)),
        vec=pl.BlockSpec((1, cb), lambda bi, ni: (0, ni)),
        wblk=pl.BlockSpec((None, cb, cb), lambda bi, ni: (ni, 0, 0)),
    )


def _lru_fwd(proj, cw, cb_, wa, ba, wx, bx, lam, name):
    b, s, _ = proj.shape
    nblk, cb, _ = wa.shape
    tc = min(s, SCAN_CHUNK)
    nc = s // tc

    def body(x_ref, cw_ref, cb_ref, wa_ref, ba_ref, wx_ref, bx_ref, lam_ref, hs_ref, xpad_ref):
        xpad_ref[0:SUBLANE, :] = jnp.zeros((SUBLANE, cb), F32)
        xpad_ref[SUBLANE:, :] = x_ref[...].astype(F32)
        wa_b = wa_ref[...].astype(BF16)
        wx_b = wx_ref[...].astype(BF16)
        sp = _softplus(-lam_ref[...])
        row = lax.broadcasted_iota(jnp.int32, (tc, cb), 0)

        def chunk(ci, carry):
            t0 = pl.multiple_of(ci * tc, tc)
            xc, r, i, la = _lru_gates(xpad_ref, t0, tc, cw_ref, cb_ref, wa_b, ba_ref, wx_b, bx_ref, sp)
            a = jnp.exp(la)
            u = jnp.sqrt(-_expm1(2.0 * la)) * (i * xc)
            h = _scan_up(a, u, carry, row)
            hs_ref[pl.ds(t0, tc), :] = h
            return _pick_row(h, row, tc - 1)

        lax.fori_loop(0, nc, chunk, jnp.zeros((1, cb), F32))

    sp_ = _lru_specs(s, cb)
    return pl.pallas_call(
        body, name=name, grid=(b, nblk),
        out_shape=jax.ShapeDtypeStruct((b, s, nblk * cb), F32),
        in_specs=[sp_["seq"], sp_["cw"], sp_["vec"], sp_["wblk"], sp_["vec"], sp_["wblk"], sp_["vec"], sp_["vec"]],
        out_specs=sp_["seq"],
        scratch_shapes=[pltpu.VMEM((s + SUBLANE, cb), F32)],
        compiler_params=_cp(("parallel", "parallel")),
    )(proj, cw, cb_, wa, ba, wx, bx, lam)


def _lru_bwd(proj, hs, dhs, cw, cb_, wa, ba, wx, bx, lam, name):
    b, s, _ = proj.shape
    nblk, cb, _ = wa.shape
    tc = min(s, SCAN_CHUNK)
    nc = s // tc

    def body(x_ref, hs_ref, dhs_ref, cw_ref, cb_ref, wa_ref, ba_ref, wx_ref, bx_ref, lam_ref,
             dx_ref, dcw_ref, dcb_ref, dwa_ref, dba_ref, dwx_ref, dbx_ref, dlam_ref,
             xpad_ref, hpad_ref, dcpad_ref, xc_ref, r_ref, i_ref, a_ref, mult_ref):
        @pl.when(pl.program_id(1) == 0)
        def _():
            for ref in (dcw_ref, dcb_ref, dwa_ref, dba_ref, dwx_ref, dbx_ref, dlam_ref):
                ref[...] = jnp.zeros_like(ref)

        zeros8 = jnp.zeros((SUBLANE, cb), F32)
        xpad_ref[0:SUBLANE, :] = zeros8
        xpad_ref[SUBLANE:, :] = x_ref[...].astype(F32)
        hpad_ref[0:SUBLANE, :] = zeros8
        hpad_ref[SUBLANE:, :] = hs_ref[...]
        dcpad_ref[s:s + SUBLANE, :] = zeros8
        wa_b = wa_ref[...].astype(BF16)
        wx_b = wx_ref[...].astype(BF16)
        lam_v = lam_ref[...]
        sp = _softplus(-lam_v)
        dsp_dlam = -_sigmoid(-lam_v)
        row = lax.broadcasted_iota(jnp.int32, (tc, cb), 0)

        def recompute(ci, carry):
            t0 = pl.multiple_of(ci * tc, tc)
            xc, r, i, la = _lru_gates(xpad_ref, t0, tc, cw_ref, cb_ref, wa_b, ba_ref, wx_b, bx_ref, sp)
            xc_ref[pl.ds(t0, tc), :] = xc
            r_ref[pl.ds(t0, tc), :] = r
            i_ref[pl.ds(t0, tc), :] = i
            a_ref[pl.ds(t0, tc), :] = jnp.exp(la)
            mult_ref[pl.ds(t0, tc), :] = jnp.sqrt(-_expm1(2.0 * la))
            return carry

        lax.fori_loop(0, nc, recompute, 0)

        def adjoint(k, carry):
            g_next, a_first_next = carry
            t0 = pl.multiple_of((nc - 1 - k) * tc, tc)
            a = a_ref[pl.ds(t0, tc), :]
            a_next = jnp.where(row == tc - 1, a_first_next, pltpu.roll(a, tc - 1, 0))
            gg = _scan_down(a_next, dhs_ref[pl.ds(t0, tc), :], g_next, row)
            h_prev = pltpu.roll(hpad_ref[pl.ds(t0, tc + SUBLANE), :], 1, 0)[SUBLANE:, :]
            xc = xc_ref[pl.ds(t0, tc), :]
            r = r_ref[pl.ds(t0, tc), :]
            i = i_ref[pl.ds(t0, tc), :]
            mult = mult_ref[pl.ds(t0, tc), :]
            d_mult = gg * i * xc
            d_i = gg * mult * xc
            d_xc = gg * mult * i
            d_la = gg * h_prev * a - d_mult * (a * a) / mult
            d_zr = (d_la * (-LRU_C * sp)) * r * (1.0 - r)
            d_zi = d_i * i * (1.0 - i)
            dlam_ref[...] += jnp.sum(d_la * (-LRU_C * r), axis=0, keepdims=True) * dsp_dlam
            dzr_b = d_zr.astype(BF16)
            dzi_b = d_zi.astype(BF16)
            xcb = xc.astype(BF16)
            d_xc = d_xc + _dot_nt(dzr_b, wa_b) + _dot_nt(dzi_b, wx_b)
            dwa_ref[...] += _dot_tn(xcb, dzr_b)
            dwx_ref[...] += _dot_tn(xcb, dzi_b)
            dba_ref[...] += jnp.sum(d_zr, axis=0, keepdims=True)
            dbx_ref[...] += jnp.sum(d_zi, axis=0, keepdims=True)
            dcb_ref[...] += jnp.sum(d_xc, axis=0, keepdims=True)
            dcpad_ref[pl.ds(t0, tc), :] = d_xc
            return _pick_row(gg, row, 0), _pick_row(a, row, 0)

        zero = jnp.zeros((1, cb), F32)
        lax.fori_loop(0, nc, adjoint, (zero, zero))

        def conv_back(ci, carry):
            t0 = pl.multiple_of(ci * tc, tc)
            dw = dcpad_ref[pl.ds(t0, tc + SUBLANE), :]
            xw = xpad_ref[pl.ds(t0, tc + SUBLANE), :]
            d_xc = dw[:tc, :]
            dxr = jnp.zeros((tc, cb), F32)
            for j in range(CONV_WIDTH):
                sh = CONV_WIDTH - 1 - j
                dsh = dw if sh == 0 else pltpu.roll(dw, tc + SUBLANE - sh, 0)
                dxr = dxr + dsh[:tc, :] * cw_ref[j:j + 1, :]
                xs = xw if sh == 0 else pltpu.roll(xw, sh, 0)
                dcw_ref[j:j + 1, :] += jnp.sum(d_xc * xs[SUBLANE:, :], axis=0, keepdims=True)
            dx_ref[pl.ds(t0, tc), :] = dxr.astype(BF16)
            return carry

        lax.fori_loop(0, nc, conv_back, 0)

    seq = lambda ni, bi: (bi, 0, ni)
    seqspec = pl.BlockSpec((None, s, cb), seq)
    cwspec = pl.BlockSpec((CONV_WIDTH, cb), lambda ni, bi: (0, ni))
    vec = pl.BlockSpec((1, cb), lambda ni, bi: (0, ni))
    wblk = pl.BlockSpec((None, cb, cb), lambda ni, bi: (ni, 0, 0))
    w = nblk * cb
    return pl.pallas_call(
        body, name=name, grid=(nblk, b),
        out_shape=(jax.ShapeDtypeStruct((b, s, w), BF16), jax.ShapeDtypeStruct((CONV_WIDTH, w), F32),
                   jax.ShapeDtypeStruct((1, w), F32), jax.ShapeDtypeStruct((nblk, cb, cb), F32),
                   jax.ShapeDtypeStruct((1, w), F32), jax.ShapeDtypeStruct((nblk, cb, cb), F32),
                   jax.ShapeDtypeStruct((1, w), F32), jax.ShapeDtypeStruct((1, w), F32)),
        in_specs=[seqspec, seqspec, seqspec, cwspec, vec, wblk, vec, wblk, vec, vec],
        out_specs=(seqspec, cwspec, vec, wblk, vec, wblk, vec, vec),
        scratch_shapes=[pltpu.VMEM((s + SUBLANE, cb), F32)] * 3 + [pltpu.VMEM((s, cb), F32)] * 5,
        compiler_params=_cp(("parallel", "arbitrary")),
    )(proj, hs, dhs, cw, cb_, wa, ba, wx, bx, lam)


def _adamw(w, g, m, v, name):
    shape = w.shape
    total = int(np.prod(shape))
    if w.ndim >= 2 and shape[-2] % SUBLANE == 0:
        rows, cols = shape[-2:]
    else:
        cols = 1024
        rows = -(-(-(-total // cols)) // SUBLANE) * SUBLANE
    lead = -(-total // (rows * cols))
    tr = _row_tile(rows, 512)
    pad = lead * rows * cols - total

    def flat(a):
        if pad:
            a = jnp.pad(a.reshape(-1), (0, pad))
        return a.reshape(lead, rows, cols)

    c1 = 1.0 - ADAM_B1 ** ADAM_STEP
    c2 = 1.0 - ADAM_B2 ** ADAM_STEP

    def body(w_ref, g_ref, m_ref, v_ref, d_ref, nm_ref, nv_ref):
        gv = g_ref[...]
        nm = ADAM_B1 * m_ref[...] + (1.0 - ADAM_B1) * gv
        nv = ADAM_B2 * v_ref[...] + (1.0 - ADAM_B2) * (gv * gv)
        nm_ref[...] = nm
        nv_ref[...] = nv
        d_ref[...] = -ADAM_LR * ((nm / c1) / (jnp.sqrt(nv / c2) + ADAM_EPS) + ADAM_WD * w_ref[...])

    spec = pl.BlockSpec((None, tr, cols), lambda l, i: (l, i, 0))
    shp = jax.ShapeDtypeStruct((lead, rows, cols), F32)
    outs = pl.pallas_call(
        body, name=name, grid=(lead, rows // tr), out_shape=(shp, shp, shp),
        in_specs=[spec] * 4, out_specs=(spec,) * 3,
        compiler_params=_cp(("parallel", "parallel")),
    )(flat(w), flat(g), flat(m), flat(v))
    if pad:
        return tuple(o.reshape(-1)[:total].reshape(shape) for o in outs)
    return tuple(o.reshape(shape) for o in outs)


def _to_heads(t, nh):
    b, s, _ = t.shape
    return t.reshape(b, s, nh, HEAD_DIM).transpose(0, 2, 1, 3)


def _stack_heads(t):
    b, s, _ = t.shape
    steps = s // (SWA_BPS * BLK)
    t = t.reshape(b, steps, SWA_BPS, BLK, A_KV_HEADS, A_GROUP, HEAD_DIM).transpose(0, 4, 1, 2, 5, 3, 6)
    return t.reshape(b, A_KV_HEADS, steps, SWA_BPS * A_GROUP * BLK, HEAD_DIM)


def _unstack_heads(t):
    b, hkv, steps, rows, hd = t.shape
    t = t.reshape(b, hkv, steps, SWA_BPS, A_GROUP, BLK, hd).transpose(0, 2, 3, 5, 1, 4, 6)
    return t.reshape(b, steps * SWA_BPS * BLK, hkv * A_GROUP * hd)


def _from_heads(t):
    b, nh, s, hd = t.shape
    return t.transpose(0, 2, 1, 3).reshape(b, s, nh * hd)


def _pad_rows(a, mult):
    r = a.shape[0]
    p = (-r) % mult
    return jnp.pad(a, ((0, p), (0, 0))) if p else a


def kernel(x, c, rel_bias, norm_g, ada_w, ada_b, attn_w_in, attn_sinks, attn_b_f, attn_w_out, lru_w_in, lru_conv_w, lru_conv_b, lru_w_a, lru_b_a, lru_w_x, lru_b_x, lru_lambda, lru_w_out, final_g, loss_target, m_rel_bias, m_norm_g, m_ada_w, m_ada_b, m_attn_w_in, m_attn_sinks, m_attn_b_f, m_attn_w_out, m_lru_w_in, m_lru_conv_w, m_lru_conv_b, m_lru_w_a, m_lru_b_a, m_lru_w_x, m_lru_b_x, m_lru_lambda, m_lru_w_out, m_final_g, v_rel_bias, v_norm_g, v_ada_w, v_ada_b, v_attn_w_in, v_attn_sinks, v_attn_b_f, v_attn_w_out, v_lru_w_in, v_lru_conv_w, v_lru_conv_b, v_lru_w_a, v_lru_b_a, v_lru_w_x, v_lru_b_x, v_lru_lambda, v_lru_w_out, v_final_g):
    bl, s, d = x.shape
    ix, iy, ic = lax.axis_index("x"), lax.axis_index("y"), lax.axis_index("c")
    chip = 2 * ix + iy
    me = 2 * chip + ic
    nb = s // BLK
    aw = A_Q_HEADS * HEAD_DIM
    akv = A_KV_HEADS * HEAD_DIM
    bw = B_HEADS * HEAD_DIM
    mixw = aw + bw
    qkv_w = aw + 2 * akv + 3 * bw
    n_in = attn_w_in.shape[2] * N_CHIP
    lw = lru_lambda.shape[1] * N_CHIP
    n0 = mixw + qkv_w + LANE

    rows_pad = -(-bl // SUBLANE) * SUBLANE
    vec_rows = jnp.concatenate([lru_conv_w[0], lru_conv_b, lru_b_a, lru_b_x, lru_lambda], axis=0)
    first = jnp.concatenate([_pad_rows(c, SUBLANE), jnp.pad(vec_rows, ((0, 0), (0, d - lw // N_CHIP)))], axis=0)
    first = _all_gather8(first, "gather_c", pltpu.VMEM).reshape(N_DEV, rows_pad + SUBLANE, d)
    c_all = first[:, :bl].reshape(N_DEV * bl, d)
    vec_all = first[:, rows_pad:, :lw // N_CHIP].reshape(N_CHIP, 2, SUBLANE, lw // N_CHIP)[:, 0]
    vec_all = vec_all.transpose(1, 0, 2).reshape(SUBLANE, lw)
    ncol = ada_w.shape[2]
    ada_w_l = lax.dynamic_index_in_dim(ada_w, ic, 0, keepdims=False)
    ada_b_l = lax.dynamic_slice(ada_b, (ic, chip * ncol), (1, ncol))
    mod_part = _ada_fwd(c_all, ada_w_l, ada_b_l, "ada_fwd")
    mod_all = _all_gather8(_pad_rows(mod_part, SUBLANE), "gather_mod", pltpu.VMEM)
    mrows = -(-(N_DEV * bl) // SUBLANE) * SUBLANE
    mod_all = mod_all.reshape(N_CHIP, 2, mrows, ncol)[:, :, :N_DEV * bl]
    mod_all = mod_all.transpose(1, 2, 0, 3).reshape(2, N_DEV * bl, N_CHIP * ncol)
    mod = lax.dynamic_slice_in_dim(mod_all, me * bl, bl, axis=1)
    shift = [mod[l, :, 0:d].reshape(bl, 1, d) for l in range(2)]
    scale = [mod[l, :, d:2 * d].reshape(bl, 1, d) for l in range(2)]
    gmod = [mod[l, :, 2 * d:3 * d].reshape(bl, 1, d) for l in range(2)]

    c_in0 = n_in // N_CHIP
    c_in1 = 2 * lw // N_CHIP
    assert c_in0 <= d and 2 * c_in1 == d
    r_in0, r_out0, r_in1, r_out1 = d // 2, mixw // N_CHIP // 2, d // 4, lw // N_CHIP // 2
    o_out0, o_in1, o_out1 = r_in0, r_in0 + r_out0, r_in0 + r_out0 + r_in1
    big_rows = o_out1 + r_out1

    def half_of(a, rows):
        return lax.dynamic_slice_in_dim(a, ic * rows, rows, axis=0)

    h_in1 = half_of(lru_w_in[0], r_in0).astype(BF16)
    my_half = jnp.concatenate([
        jnp.pad(half_of(attn_w_in[0], r_in0).astype(BF16), ((0, 0), (0, d - c_in0))),
        half_of(attn_w_out[0], r_out0).astype(BF16),
        jnp.concatenate([h_in1[:r_in1], h_in1[r_in1:]], axis=1),
        half_of(lru_w_out[0], r_out1).astype(BF16)], axis=0)
    gat = _all_gather8(my_half, "gather_weights", pltpu.HBM).reshape(N_CHIP, 2, big_rows, d)
    w_in0 = gat[:, :, :r_in0, :c_in0].transpose(1, 2, 0, 3).reshape(d, n_in)
    w_out0 = gat[:, :, o_out0:o_in1].reshape(mixw, d)
    w_in1 = gat[:, :, o_in1:o_out1].reshape(N_CHIP, 2, r_in1, 2, c_in1)
    w_in1 = w_in1.transpose(1, 3, 2, 0, 4).reshape(d, 2 * lw)
    w_out1 = gat[:, :, o_out1:].reshape(lw, d)
    w_cat0 = jnp.concatenate([w_in0[:, qkv_w + B_HEADS:], w_in0[:, :qkv_w + B_HEADS],
                              jnp.zeros((d, n0 - n_in), BF16)], axis=1)

    proj0, h0, zf = _norm_proj(x, norm_g[0:1], scale[0], shift[0], w_cat0, LANE, "norm_proj0")
    o_a = mixw
    aq = _stack_heads(proj0[:, :, o_a:o_a + aw].astype(BF16))
    ak = _to_heads(proj0[:, :, o_a + aw:o_a + aw + akv].astype(BF16), A_KV_HEADS)
    av = _to_heads(proj0[:, :, o_a + aw + akv:o_a + aw + 2 * akv].astype(BF16), A_KV_HEADS)
    o_b = o_a + aw + 2 * akv
    fox_blks = (o_b // LANE, (o_b + bw) // LANE, (o_b + 2 * bw) // LANE)
    bucket_np, valid_np = _rel_buckets()
    bucket = jnp.asarray(bucket_np)
    bias = _swa_bias(rel_bias.T, bucket, jnp.asarray(valid_np), "swa_bias")
    bias = bias.reshape(A_KV_HEADS, A_GROUP * BLK, 2 * BLK)
    bias = jnp.concatenate([jnp.pad(bias, ((0, 0), (0, 0), (blk * BLK, (SWA_BPS - 1 - blk) * BLK)),
                                    constant_values=NEG) for blk in range(SWA_BPS)], axis=1)
    sinks = jnp.repeat(attn_sinks[0].reshape(A_KV_HEADS, A_GROUP), BLK, axis=1).reshape(A_KV_HEADS, A_GROUP * BLK, 1)
    sinks = jnp.tile(sinks, (1, SWA_BPS, 1))
    a_out, a_lse = _swa_fwd(aq, ak, av, bias, sinks, "swa_fwd")
    bf_pad = jnp.pad(attn_b_f, ((0, 0), (0, LANE - B_HEADS)))
    fsum = _fox_decay(zf, bf_pad, "fox_decay")
    fh = fsum[:, :, :B_HEADS].transpose(0, 2, 1)
    fcol = fh.reshape(bl, B_HEADS, s, 1)
    frow = fh.reshape(bl, B_HEADS, 1, s)
    fbq = min(s, FOX_BQ)
    frowb = fh.reshape(bl, B_HEADS, s // fbq, 1, fbq)
    b_out, b_lse = _fox_fwd(proj0, *fox_blks, fcol, frow, "fox_fwd")
    lserowb = b_lse.reshape(bl, B_HEADS, s // fbq, 1, fbq)
    mix0 = [_unstack_heads(a_out), b_out]
    x1, o0 = _gate_outproj(mix0, proj0, 0, w_out0, x, gmod[0], "gate_outproj0")

    proj1, h1 = _norm_proj(x1, norm_g[1:2], scale[1], shift[1], w_in1, 0, "norm_proj1")
    cw_f, cb_f, ba_f, bx_f, lam_f = vec_all[0:4], vec_all[4:5], vec_all[5:6], vec_all[6:7], vec_all[7:8]
    hs = _lru_fwd(proj1, cw_f, cb_f, lru_w_a[0], ba_f, lru_w_x[0], bx_f, lam_f, "lru_fwd")
    x2, o1 = _gate_outproj([hs], proj1, 1, w_out1, x1, gmod[1], "gate_outproj1")

    loss_vec, dx2, g_final = _final_loss(x2, final_g.reshape(1, d), loss_target, "final_loss")
    loss = lax.psum(loss_vec[0, 0], ("x", "y", "c"))

    dhs, dgate1, do1, y1, dgm1 = _bwd_out(dx2, gmod[1], o1, [hs], proj1, 1, w_out1, F32, "bwd_out1")
    g_w_out1 = _matmul_tn(y1, [do1], "grad_w_out1")
    (dxr, g_cw, g_cb, g_wa, g_ba, g_wx, g_bx, g_lam) = _lru_bwd(
        proj1, hs, dhs, cw_f, cb_f, lru_w_a[0], ba_f, lru_w_x[0], bx_f, lam_f, "lru_bwd")
    dproj1 = [dxr, dgate1]
    g_w_in1 = _matmul_tn(h1, dproj1, "grad_w_in1")
    dx1, dsh1, dsc1, g_ng1 = _bwd_in(dproj1, w_in1, x1, norm_g[1:2], scale[1], dx2, "bwd_in1")

    dmix0, dgate0, do0, y0, dgm0 = _bwd_out(dx1, gmod[0], o0, mix0, proj0, 0, w_out0, BF16, "bwd_out0")
    g_w_out0 = _matmul_tn(y0, [do0], "grad_w_out0")
    da_out = _stack_heads(dmix0[:, :, :aw].astype(BF16))
    daq, dak, dav, dbias, dsink = _swa_bwd(aq, ak, av, bias, sinks, da_out, a_lse, "swa_bwd")
    dbq, dbk, dbv, dfrow = _fox_bwd(proj0, *fox_blks, dmix0, aw // LANE, fcol, frow, frowb, b_lse, lserowb,
                                    "fox_bwd")
    df = dfrow.reshape(bl, B_HEADS, s).transpose(0, 2, 1)
    df = jnp.pad(df, ((0, 0), (0, 0), (0, LANE - B_HEADS)))
    dzf, g_bf = _fox_dgate(df, zf, bf_pad, B_HEADS, "fox_dgate")
    dproj0 = ([dgate0, _unstack_heads(daq), _from_heads(dak), _from_heads(dav)]
              + [dbq, dbk, dbv, dzf.astype(BF16)])
    g_w_cat0 = _matmul_tn(h0, dproj0, "grad_w_in0")
    g_w_in0 = jnp.concatenate([g_w_cat0[:, mixw:mixw + qkv_w + B_HEADS], g_w_cat0[:, :mixw]], axis=1)
    dx0, dsh0, dsc0, g_ng0 = _bwd_in(dproj0, w_cat0, x, norm_g[0:1], scale[0], dx1, "bwd_in0")
    dbias = sum(dbias[:, :, blk * A_GROUP * BLK:(blk + 1) * A_GROUP * BLK, blk * BLK:(blk + 2) * BLK]
                for blk in range(SWA_BPS))
    g_relb, g_sink = _swa_small_grads(dbias.reshape(bl, A_Q_HEADS, BLK, 2 * BLK),
                                      dsink.reshape(bl, A_Q_HEADS, 1, LANE), bucket, "swa_small_grads")

    dmod = jnp.concatenate([jnp.concatenate([dsh0, dsc0, dgm0], axis=-1),
                            jnp.concatenate([dsh1, dsc1, dgm1], axis=-1)], axis=1)
    dmod_rows = _pad_rows(dmod.reshape(bl * 6, d), SUBLANE)

    tail = jnp.concatenate([g_relb[:, :, 0].T.reshape(-1), g_sink[:, 0, 0], g_bf[0, :B_HEADS]])
    n_relb = REL_BUCKETS * A_Q_HEADS
    small_rows = [g_wa.reshape(-1, d), g_wx.reshape(-1, d), g_ng0, g_ng1, g_final, g_cw, g_cb, g_ba, g_bx, g_lam,
                  jnp.pad(tail, (0, d - tail.shape[0])).reshape(1, d)]
    small_counts = [r.shape[0] for r in small_rows]
    piece_rows = -(-(-(-sum(small_counts) // N_DEV)) // SUBLANE) * SUBLANE
    small_2d = jnp.concatenate(small_rows, axis=0)
    small_2d = jnp.pad(small_2d, ((0, N_DEV * piece_rows - small_2d.shape[0]), (0, 0)))
    small_pieces = small_2d.reshape(N_CHIP, 2, piece_rows, d)
    p_in0 = jnp.pad(g_w_in0.reshape(2, r_in0, N_CHIP, c_in0).transpose(2, 0, 1, 3),
                    ((0, 0), (0, 0), (0, 0), (0, d - c_in0)))
    p_in1 = g_w_in1.reshape(2, 2, r_in1, N_CHIP, c_in1).transpose(3, 0, 2, 1, 4).reshape(N_CHIP, 2, r_in1, d)
    pieces = jnp.concatenate([p_in0, g_w_out0.reshape(N_CHIP, 2, r_out0, d), p_in1,
                              g_w_out1.reshape(N_CHIP, 2, r_out1, d), small_pieces], axis=2)
    theirs = _sibling_push(pieces, True, "push_sibling_halves")
    partial = _pair_sum(jnp.reshape(ic, (1,)).astype(jnp.int32), pieces, theirs, "sum_chip")
    slots = _chip_all_to_all(partial, "exchange_grads")
    reduced = _sum_slots(slots, "sum_grads")
    mine_big = reduced[:big_rows]
    other_big = _sibling_push(mine_big[None], False, "swap_halves")[0]
    both = jnp.stack([jnp.where(ic == 0, mine_big, other_big), jnp.where(ic == 0, other_big, mine_big)])
    g_big = [both[:, :r_in0, :c_in0].reshape(d, c_in0),
             both[:, o_out0:o_in1].reshape(2 * r_out0, d),
             both[:, o_in1:o_out1].reshape(2, r_in1, 2, c_in1).transpose(0, 2, 1, 3).reshape(d, c_in1),
             both[:, o_out1:].reshape(2 * r_out1, d)]
    last = _all_gather8(jnp.concatenate([reduced[big_rows:], dmod_rows], axis=0), "gather_small_grads", pltpu.VMEM)
    last = last.reshape(N_DEV, piece_rows + dmod_rows.shape[0], d)
    small_all = last[:, :piece_rows].reshape(N_DEV * piece_rows, d)
    dmod_all = last[:, piece_rows:piece_rows + bl * 6].reshape(N_DEV * bl, 6 * d)
    dmod_chip = lax.dynamic_slice_in_dim(dmod_all.reshape(N_DEV * bl, 2, 3 * d), chip * ncol, ncol, axis=2)
    g_ada_w, g_ada_b = _ada_bwd(c_all, dmod_chip.transpose(1, 0, 2), dmod_all, "ada_bwd")
    g_ada_b = g_ada_b.reshape(2, 3 * d)
    g_small, off = [], 0
    for cnt in small_counts:
        g_small.append(small_all[off:off + cnt])
        off += cnt
    g_w_a, g_w_x = g_small[0].reshape(lru_w_a.shape[1:]), g_small[1].reshape(lru_w_x.shape[1:])
    g_norm_g = jnp.concatenate(g_small[2:4], axis=0)
    g_fin, g_cw_r, g_cb_r, g_ba_r, g_bx_r, g_lam_r = g_small[4:10]
    tail = g_small[10][0]
    g_rel_bias = tail[:n_relb].reshape(REL_BUCKETS, A_Q_HEADS)
    g_sinks, g_b_f = tail[n_relb:n_relb + A_Q_HEADS], tail[n_relb + A_Q_HEADS:n_relb + A_Q_HEADS + B_HEADS]
    cw4 = lw // N_CHIP

    def my_cols(a):
        return lax.dynamic_slice_in_dim(a, chip * cw4, cw4, axis=1)

    grads = {
        "rel_bias": g_rel_bias, "norm_g": g_norm_g, "ada_w": g_ada_w, "ada_b": g_ada_b,
        "attn_w_in": g_big[0][None], "attn_sinks": g_sinks[None], "attn_b_f": g_b_f[None],
        "attn_w_out": g_big[1][None], "lru_w_in": g_big[2][None], "lru_conv_w": my_cols(g_cw_r)[None],
        "lru_conv_b": my_cols(g_cb_r), "lru_w_a": g_w_a[None], "lru_b_a": my_cols(g_ba_r),
        "lru_w_x": g_w_x[None], "lru_b_x": my_cols(g_bx_r), "lru_lambda": my_cols(g_lam_r),
        "lru_w_out": g_big[3][None], "final_g": g_fin.reshape(d),
    }
    weights = dict(rel_bias=rel_bias, norm_g=norm_g, ada_w=ada_w, ada_b=ada_b, attn_w_in=attn_w_in,
                   attn_sinks=attn_sinks, attn_b_f=attn_b_f, attn_w_out=attn_w_out, lru_w_in=lru_w_in,
                   lru_conv_w=lru_conv_w, lru_conv_b=lru_conv_b, lru_w_a=lru_w_a, lru_b_a=lru_b_a,
                   lru_w_x=lru_w_x, lru_b_x=lru_b_x, lru_lambda=lru_lambda, lru_w_out=lru_w_out, final_g=final_g)
    moms = dict(rel_bias=(m_rel_bias, v_rel_bias), norm_g=(m_norm_g, v_norm_g), ada_w=(m_ada_w, v_ada_w),
                ada_b=(m_ada_b, v_ada_b), attn_w_in=(m_attn_w_in, v_attn_w_in),
                attn_sinks=(m_attn_sinks, v_attn_sinks), attn_b_f=(m_attn_b_f, v_attn_b_f),
                attn_w_out=(m_attn_w_out, v_attn_w_out), lru_w_in=(m_lru_w_in, v_lru_w_in),
                lru_conv_w=(m_lru_conv_w, v_lru_conv_w), lru_conv_b=(m_lru_conv_b, v_lru_conv_b),
                lru_w_a=(m_lru_w_a, v_lru_w_a), lru_b_a=(m_lru_b_a, v_lru_b_a), lru_w_x=(m_lru_w_x, v_lru_w_x),
                lru_b_x=(m_lru_b_x, v_lru_b_x), lru_lambda=(m_lru_lambda, v_lru_lambda),
                lru_w_out=(m_lru_w_out, v_lru_w_out), final_g=(m_final_g, v_final_g))
    names = list(weights)
    big_names = [n for n in names if weights[n].size >= 65536]
    small_names = [n for n in names if weights[n].size < 65536]
    delta, new_m, new_v = {}, {}, {}
    for n in big_names:
        delta[n], new_m[n], new_v[n] = _adamw(weights[n], grads[n].reshape(weights[n].shape),
                                              moms[n][0], moms[n][1], "adamw_" + n)
    cat = lambda arrs: jnp.concatenate([a.reshape(-1) for a in arrs])
    sd, sm, sv = _adamw(cat([weights[n] for n in small_names]), cat([grads[n] for n in small_names]),
                        cat([moms[n][0] for n in small_names]), cat([moms[n][1] for n in small_names]),
                        "adamw_small")
    off = 0
    for n in small_names:
        sz = weights[n].size
        shp = weights[n].shape
        delta[n], new_m[n], new_v[n] = (sd[off:off + sz].reshape(shp), sm[off:off + sz].reshape(shp),
                                        sv[off:off + sz].reshape(shp))
        off += sz
    out_grads = [grads[n].reshape(weights[n].shape) for n in names]
    return (loss, dx0, *out_grads, *[delta[n] for n in names], *[new_m[n] for n in names],
            *[new_v[n] for n in names])
```

```python
import math

import numpy as np
import jax
import jax.numpy as jnp
from jax import lax
from jax.experimental import pallas as pl
from jax.experimental.pallas import tpu as pltpu

F32 = jnp.float32
BF16 = jnp.bfloat16
MESH = pl.DeviceIdType.MESH

N_DEV = 8
N_CHIP = 4
HEAD_DIM = 64
BLK = 128
A_Q_HEADS = 8
A_KV_HEADS = 2
A_GROUP = A_Q_HEADS // A_KV_HEADS
B_HEADS = 8
REL_BUCKETS = 32
REL_MAX_EXACT = 16
REL_MAX_DIST = 128
LRU_C = 8.0
CONV_WIDTH = 4
EPS = 1e-6
NEG = -1e30
SCALE = HEAD_DIM ** -0.5
LANE = 128
SUBLANE = 8
VMEM_LIMIT = 56 * 1024 * 1024
SCAN_CHUNK = 512
ROW_TILE = 1024
SWA_BPS = 1
FOX_BQ = 512
ADAM_LR = 0.001
ADAM_B1 = 0.9
ADAM_B2 = 0.999
ADAM_EPS = 1e-08
ADAM_WD = 0.01
ADAM_STEP = 10
HI = lax.Precision.HIGHEST


def _cp(sem=None):
    return pltpu.CompilerParams(dimension_semantics=sem, vmem_limit_bytes=VMEM_LIMIT)


def _dot(a, b):
    return jnp.dot(a, b, preferred_element_type=F32)


def _dot_nt(a, b):
    return lax.dot_general(a, b, (((1,), (1,)), ((), ())), preferred_element_type=F32)


def _dot_tn(a, b):
    return lax.dot_general(a, b, (((0,), (0,)), ((), ())), preferred_element_type=F32)


def _sigmoid(z):
    return 1.0 / (1.0 + jnp.exp(-z))


def _row_tile(rows, cap):
    if rows <= cap:
        return rows
    best = SUBLANE
    t = SUBLANE
    while t <= cap:
        if rows % t == 0:
            best = t
        t += SUBLANE
    return best


def _all_gather8(x_shard, name, space):
    m_per, n = x_shard.shape
    n_own = 8 if (space == pltpu.HBM and m_per % 128 == 0) else 1
    own_rows = m_per // n_own

    def body(x_ref, out_ref, send_sems, recv_sems, local_sems):
        x, y, c = lax.axis_index("x"), lax.axis_index("y"), lax.axis_index("c")
        me, sibling = (x, y, c), (x, y, 1 - c)
        chips = [(1 - x, y), (x, 1 - y), (1 - x, 1 - y)]

        def rows(px, py, pc):
            return out_ref.at[pl.ds((4 * px + 2 * py + pc) * m_per, m_per), :]

        def copy(k, block, to, src=None):
            return pltpu.make_async_remote_copy(
                src_ref=rows(*block) if src is None else src, dst_ref=rows(*block),
                send_sem=send_sems.at[k], recv_sem=recv_sems.at[k], device_id=to, device_id_type=MESH)

        base = (4 * x + 2 * y + c) * m_per
        mine = [pltpu.make_async_copy(x_ref.at[pl.ds(i * own_rows, own_rows), :],
                                      out_ref.at[pl.ds(base + i * own_rows, own_rows), :], local_sems.at[i])
                for i in range(n_own)]
        for cp in mine:
            cp.start()
        first = [copy(0, me, sibling, src=x_ref)]
        first += [copy(1 + j, me, (*chip, c), src=x_ref) for j, chip in enumerate(chips)]
        for cp in first:
            cp.start()
        passed = [copy(4 + j, (*chip, c), sibling) for j, chip in enumerate(chips)]
        for j, chip in enumerate(chips):
            copy(1 + j, (*chip, c), me).wait_recv()
            passed[j].start()
        copy(0, sibling, me).wait_recv()
        for j, chip in enumerate(chips):
            copy(4 + j, (*chip, 1 - c), me).wait_recv()
        for cp in first + passed:
            cp.wait_send()
        for cp in mine:
            cp.wait()

    return pl.pallas_call(
        body, name=name,
        out_shape=jax.ShapeDtypeStruct((N_DEV * m_per, n), x_shard.dtype),
        in_specs=[pl.BlockSpec(memory_space=space)],
        out_specs=pl.BlockSpec(memory_space=space),
        scratch_shapes=[pltpu.SemaphoreType.DMA((7,)), pltpu.SemaphoreType.DMA((7,)),
                        pltpu.SemaphoreType.DMA((n_own,))],
        compiler_params=pltpu.CompilerParams(vmem_limit_bytes=VMEM_LIMIT),
    )(x_shard)


def _sibling_push(blocks, pick_other, name):
    nblk = blocks.shape[0]
    m, n = blocks.shape[-2:]

    def body(x_ref, out_ref, send_sems, recv_sems):
        x, y, c = lax.axis_index("x"), lax.axis_index("y"), lax.axis_index("c")
        copies = []
        for k in range(nblk):
            src = x_ref.at[k, 1 - c] if pick_other else x_ref.at[k]
            copies.append(pltpu.make_async_remote_copy(
                src_ref=src, dst_ref=out_ref.at[k], send_sem=send_sems.at[k], recv_sem=recv_sems.at[k],
                device_id=(x, y, 1 - c), device_id_type=MESH))
        for cp in copies:
            cp.start()
        for cp in copies:
            cp.wait_recv()
        for cp in copies:
            cp.wait_send()

    hbm = pl.BlockSpec(memory_space=pltpu.HBM)
    return pl.pallas_call(
        body, name=name,
        out_shape=jax.ShapeDtypeStruct((nblk, m, n), blocks.dtype),
        in_specs=[hbm], out_specs=hbm,
        scratch_shapes=[pltpu.SemaphoreType.DMA((nblk,)), pltpu.SemaphoreType.DMA((nblk,))],
    )(blocks)


def _chip_all_to_all(parts, name):
    _, m, n = parts.shape

    def body(x_ref, out_ref, send_sems, recv_sems, local_sem):
        x, y, c = lax.axis_index("x"), lax.axis_index("y"), lax.axis_index("c")
        me = 2 * x + y
        mine = pltpu.make_async_copy(x_ref.at[me], out_ref.at[me], local_sem)
        mine.start()
        copies = []
        for k in range(1, N_CHIP):
            px, py = x ^ ((k >> 1) & 1), y ^ (k & 1)
            copies.append(pltpu.make_async_remote_copy(
                src_ref=x_ref.at[2 * px + py], dst_ref=out_ref.at[me],
                send_sem=send_sems.at[k - 1], recv_sem=recv_sems.at[k - 1],
                device_id=(px, py, c), device_id_type=MESH))
        for cp in copies:
            cp.start()
        for cp in copies:
            cp.wait_recv()
        for cp in copies:
            cp.wait_send()
        mine.wait()

    hbm = pl.BlockSpec(memory_space=pltpu.HBM)
    return pl.pallas_call(
        body, name=name,
        out_shape=jax.ShapeDtypeStruct(parts.shape, parts.dtype),
        in_specs=[hbm], out_specs=hbm,
        scratch_shapes=[pltpu.SemaphoreType.DMA((N_CHIP - 1,)), pltpu.SemaphoreType.DMA((N_CHIP - 1,)),
                        pltpu.SemaphoreType.DMA],
    )(parts)


def _pair_sum(core, pieces, theirs, name):
    nblk, _, m, n = pieces.shape
    tr = _row_tile(m, 536)

    def body(c_ref, p_ref, t_ref, o_ref):
        o_ref[...] = (p_ref[...] + t_ref[...]).astype(BF16)

    return pl.pallas_call(
        body, name=name,
        grid_spec=pltpu.PrefetchScalarGridSpec(
            num_scalar_prefetch=1, grid=(nblk, m // tr),
            in_specs=[pl.BlockSpec((None, None, tr, n), lambda k, i, c_ref: (k, c_ref[0], i, 0)),
                      pl.BlockSpec((None, tr, n), lambda k, i, c_ref: (k, i, 0))],
            out_specs=pl.BlockSpec((None, tr, n), lambda k, i, c_ref: (k, i, 0))),
        out_shape=jax.ShapeDtypeStruct((nblk, m, n), BF16),
        compiler_params=_cp(("parallel", "parallel")),
    )(core, pieces, theirs)


def _sum_slots(slots, name):
    k, m, n = slots.shape
    tr = _row_tile(m, 536)

    def body(s_ref, o_ref):
        acc = s_ref[0].astype(F32)
        for j in range(1, k):
            acc = acc + s_ref[j].astype(F32)
        o_ref[...] = acc

    return pl.pallas_call(
        body, name=name, grid=(m // tr,),
        out_shape=jax.ShapeDtypeStruct((m, n), F32),
        in_specs=[pl.BlockSpec((k, tr, n), lambda i: (0, i, 0))],
        out_specs=pl.BlockSpec((tr, n), lambda i: (i, 0)),
        compiler_params=_cp(("parallel",)),
    )(slots)


def _ada_fwd(c_all, w, b, name):
    r, _ = c_all.shape
    n = w.shape[1]

    def body(c_ref, w_ref, b_ref, o_ref):
        cv = c_ref[...]
        act = cv * _sigmoid(cv)
        o_ref[...] = jnp.dot(act, w_ref[...], precision=HI, preferred_element_type=F32) + b_ref[...]

    return pl.pallas_call(body, name=name, out_shape=jax.ShapeDtypeStruct((r, n), F32),
                          compiler_params=_cp())(c_all, w, b)


def _ada_bwd(c_all, dmod_chip, dmod_all, name):
    r, d = c_all.shape
    nl, _, n = dmod_chip.shape

    def body(c_ref, dm_ref, da_ref, gw_ref, gb_ref):
        cv = c_ref[...]
        act = cv * _sigmoid(cv)
        for l in range(nl):
            gw_ref[l] = lax.dot_general(act, dm_ref[l], (((0,), (0,)), ((), ())), precision=HI,
                                        preferred_element_type=F32)
        gb_ref[...] = jnp.sum(da_ref[...], axis=0, keepdims=True)

    return pl.pallas_call(
        body, name=name,
        out_shape=(jax.ShapeDtypeStruct((nl, d, n), F32), jax.ShapeDtypeStruct((1, dmod_all.shape[1]), F32)),
        compiler_params=_cp())(c_all, dmod_chip, dmod_all)


def _norm_proj(x, g, scale, shift, w, f32_cols, name):
    b, s, d = x.shape
    n = w.shape[1]
    tm = min(s, ROW_TILE)

    def body(x_ref, g_ref, sc_ref, sh_ref, w_ref, proj_ref, h_ref, *aux_ref):
        xv = x_ref[...]
        rstd = lax.rsqrt(jnp.mean(xv * xv, axis=-1, keepdims=True) + EPS)
        h = (xv * rstd) * g_ref[...] * (1.0 + sc_ref[...]) + sh_ref[...]
        hb = h.astype(BF16)
        h_ref[...] = hb
        proj = _dot(hb, w_ref[...])
        proj_ref[...] = proj.astype(BF16)
        if f32_cols:
            aux_ref[0][...] = proj[:, n - f32_cols:]

    row = lambda i, j: (i, j, 0)
    out_shape = [jax.ShapeDtypeStruct((b, s, n), BF16), jax.ShapeDtypeStruct((b, s, d), BF16)]
    out_specs = [pl.BlockSpec((None, tm, n), row), pl.BlockSpec((None, tm, d), row)]
    if f32_cols:
        out_shape.append(jax.ShapeDtypeStruct((b, s, f32_cols), F32))
        out_specs.append(pl.BlockSpec((None, tm, f32_cols), row))
    return pl.pallas_call(
        body, name=name, grid=(b, s // tm),
        out_shape=tuple(out_shape),
        in_specs=[pl.BlockSpec((None, tm, d), row),
                  pl.BlockSpec((1, d), lambda i, j: (0, 0)),
                  pl.BlockSpec((None, 1, d), lambda i, j: (i, 0, 0)),
                  pl.BlockSpec((None, 1, d), lambda i, j: (i, 0, 0)),
                  pl.BlockSpec((d, n), lambda i, j: (0, 0))],
        out_specs=tuple(out_specs),
        compiler_params=_cp(("parallel", "parallel")),
    )(x, g, scale, shift, w)


def _cat_refs(refs):
    vals = [r[...] for r in refs]
    return vals[0] if len(vals) == 1 else jnp.concatenate(vals, axis=-1)


def _gate_outproj(mix_parts, proj, gate_blk, w_out, x, gmod, name):
    b, s, _ = x.shape
    wd, d = w_out.shape
    tm = min(s, ROW_TILE)
    npart = len(mix_parts)

    def body(*refs):
        mix_refs = refs[:npart]
        gate_ref, w_ref, x_ref, gm_ref, xo_ref, o_ref = refs[npart:]
        gt = gate_ref[...].astype(F32)
        y = (_cat_refs(mix_refs) * (gt * _sigmoid(gt))).astype(BF16)
        o = _dot(y, w_ref[...])
        o_ref[...] = o.astype(BF16)
        xo_ref[...] = x_ref[...] + gm_ref[...] * o

    return pl.pallas_call(
        body, name=name, grid=(b, s // tm),
        out_shape=(jax.ShapeDtypeStruct((b, s, d), F32), jax.ShapeDtypeStruct((b, s, d), BF16)),
        in_specs=[pl.BlockSpec((None, tm, p.shape[2]), lambda i, j: (i, j, 0)) for p in mix_parts] + [
                  pl.BlockSpec((None, tm, wd), lambda i, j: (i, j, gate_blk)),
                  pl.BlockSpec((wd, d), lambda i, j: (0, 0)),
                  pl.BlockSpec((None, tm, d), lambda i, j: (i, j, 0)),
                  pl.BlockSpec((None, 1, d), lambda i, j: (i, 0, 0))],
        out_specs=(pl.BlockSpec((None, tm, d), lambda i, j: (i, j, 0)),
                   pl.BlockSpec((None, tm, d), lambda i, j: (i, j, 0))),
        compiler_params=_cp(("parallel", "parallel")),
    )(*mix_parts, proj, w_out, x, gmod)


def _final_loss(x, g, target, name):
    b, s, d = x.shape
    tm = min(s, ROW_TILE)

    def body(x_ref, g_ref, t_ref, loss_ref, dx_ref, dg_ref):
        first = jnp.logical_and(pl.program_id(0) == 0, pl.program_id(1) == 0)

        @pl.when(first)
        def _():
            loss_ref[...] = jnp.zeros_like(loss_ref)
            dg_ref[...] = jnp.zeros_like(dg_ref)

        xv = x_ref[...]
        gv = g_ref[...]
        rstd = lax.rsqrt(jnp.mean(xv * xv, axis=-1, keepdims=True) + EPS)
        xhat = xv * rstd
        err = xhat * gv - t_ref[...]
        row = jnp.mean(err * err, axis=-1, keepdims=True)
        loss_ref[...] += 0.5 * jnp.sum(row, axis=0, keepdims=True)
        dy = err * (1.0 / d)
        dg_ref[...] += jnp.sum(dy * xhat, axis=0, keepdims=True)
        dxh = dy * gv
        dx_ref[...] = rstd * (dxh - xhat * jnp.mean(dxh * xhat, axis=-1, keepdims=True))

    return pl.pallas_call(
        body, name=name, grid=(b, s // tm),
        out_shape=(jax.ShapeDtypeStruct((1, LANE), F32), jax.ShapeDtypeStruct((b, s, d), F32),
                   jax.ShapeDtypeStruct((1, d), F32)),
        in_specs=[pl.BlockSpec((None, tm, d), lambda i, j: (i, j, 0)),
                  pl.BlockSpec((1, d), lambda i, j: (0, 0)),
                  pl.BlockSpec((None, tm, d), lambda i, j: (i, j, 0))],
        out_specs=(pl.BlockSpec((1, LANE), lambda i, j: (0, 0)),
                   pl.BlockSpec((None, tm, d), lambda i, j: (i, j, 0)),
                   pl.BlockSpec((1, d), lambda i, j: (0, 0))),
        compiler_params=_cp(("arbitrary", "arbitrary")),
    )(x, g, target)


def _bwd_out(dxo, gmod, o, mix_parts, proj, gate_blk, w_out, dmix_dtype, name):
    b, s, d = dxo.shape
    wd = w_out.shape[0]
    tm = min(s, ROW_TILE)
    npart = len(mix_parts)

    def body(dx_ref, gm_ref, o_ref, *refs):
        mix_refs = refs[:npart]
        gate_ref, wt_ref, dmix_ref, dgate_ref, do_ref, y_ref, dgm_ref = refs[npart:]

        @pl.when(pl.program_id(1) == 0)
        def _():
            dgm_ref[...] = jnp.zeros_like(dgm_ref)

        dx = dx_ref[...]
        dgm_ref[...] += jnp.sum(dx * o_ref[...].astype(F32), axis=0, keepdims=True)
        dob = (gm_ref[...] * dx).astype(BF16)
        do_ref[...] = dob
        dy = _dot_nt(dob, wt_ref[...])
        gt = gate_ref[...].astype(F32)
        sg = _sigmoid(gt)
        silu = gt * sg
        mx = _cat_refs(mix_refs)
        y_ref[...] = (mx * silu).astype(BF16)
        dmix_ref[...] = (dy * silu).astype(dmix_dtype)
        dgate_ref[...] = (dy * mx * (sg * (1.0 + gt * (1.0 - sg)))).astype(BF16)

    row = lambda i, j: (i, j, 0)
    return pl.pallas_call(
        body, name=name, grid=(b, s // tm),
        out_shape=(jax.ShapeDtypeStruct((b, s, wd), dmix_dtype), jax.ShapeDtypeStruct((b, s, wd), BF16),
                   jax.ShapeDtypeStruct((b, s, d), BF16), jax.ShapeDtypeStruct((b, s, wd), BF16),
                   jax.ShapeDtypeStruct((b, 1, d), F32)),
        in_specs=[pl.BlockSpec((None, tm, d), row),
                  pl.BlockSpec((None, 1, d), lambda i, j: (i, 0, 0)),
                  pl.BlockSpec((None, tm, d), row)] + [
                  pl.BlockSpec((None, tm, p.shape[2]), row) for p in mix_parts] + [
                  pl.BlockSpec((None, tm, wd), lambda i, j: (i, j, gate_blk)),
                  pl.BlockSpec((wd, d), lambda i, j: (0, 0))],
        out_specs=(pl.BlockSpec((None, tm, wd), row), pl.BlockSpec((None, tm, wd), row),
                   pl.BlockSpec((None, tm, d), row), pl.BlockSpec((None, tm, wd), row),
                   pl.BlockSpec((None, 1, d), lambda i, j: (i, 0, 0))),
        compiler_params=_cp(("parallel", "arbitrary")),
    )(dxo, gmod, o, *mix_parts, proj, w_out)


def _bwd_in(dproj_parts, w_in, x, g, scale, dxo, name):
    b, s, d = x.shape
    n = w_in.shape[1]
    tm = min(s, ROW_TILE)
    npart = len(dproj_parts)

    def body(*refs):
        dp_refs = refs[:npart]
        wt_ref, x_ref, g_ref, sc_ref, dxo_ref, dx_ref, dsh_ref, dsc_ref, dg_ref = refs[npart:]

        @pl.when(jnp.logical_and(pl.program_id(0) == 0, pl.program_id(1) == 0))
        def _():
            dg_ref[...] = jnp.zeros_like(dg_ref)

        @pl.when(pl.program_id(1) == 0)
        def _():
            dsh_ref[...] = jnp.zeros_like(dsh_ref)
            dsc_ref[...] = jnp.zeros_like(dsc_ref)

        dh = _dot_nt(_cat_refs(dp_refs), wt_ref[...])
        xv = x_ref[...]
        gv = g_ref[...]
        one_sc = 1.0 + sc_ref[...]
        rstd = lax.rsqrt(jnp.mean(xv * xv, axis=-1, keepdims=True) + EPS)
        xhat = xv * rstd
        dsh_ref[...] += jnp.sum(dh, axis=0, keepdims=True)
        dsc_ref[...] += jnp.sum(dh * (xhat * gv), axis=0, keepdims=True)
        dhs = dh * one_sc
        dg_ref[...] += jnp.sum(dhs * xhat, axis=0, keepdims=True)
        dxh = dhs * gv
        dx_ref[...] = dxo_ref[...] + rstd * (dxh - xhat * jnp.mean(dxh * xhat, axis=-1, keepdims=True))

    row = lambda i, j: (i, j, 0)
    per_b = lambda i, j: (i, 0, 0)
    return pl.pallas_call(
        body, name=name, grid=(b, s // tm),
        out_shape=(jax.ShapeDtypeStruct((b, s, d), F32), jax.ShapeDtypeStruct((b, 1, d), F32),
                   jax.ShapeDtypeStruct((b, 1, d), F32), jax.ShapeDtypeStruct((1, d), F32)),
        in_specs=[pl.BlockSpec((None, tm, p.shape[2]), row) for p in dproj_parts] + [
                  pl.BlockSpec((d, n), lambda i, j: (0, 0)),
                  pl.BlockSpec((None, tm, d), row),
                  pl.BlockSpec((1, d), lambda i, j: (0, 0)),
                  pl.BlockSpec((None, 1, d), per_b),
                  pl.BlockSpec((None, tm, d), row)],
        out_specs=(pl.BlockSpec((None, tm, d), row), pl.BlockSpec((None, 1, d), per_b),
                   pl.BlockSpec((None, 1, d), per_b), pl.BlockSpec((1, d), lambda i, j: (0, 0))),
        compiler_params=_cp(("arbitrary", "arbitrary")),
    )(*dproj_parts, w_in, x, g, scale, dxo)


def _matmul_tn(a, b_parts, name):
    bsz, s, m = a.shape
    n = sum(p.shape[2] for p in b_parts)
    tk = next(c for c in (512, 256, 128) if s % c == 0)
    npart = len(b_parts)

    def body(a_ref, *refs):
        b_refs, o_ref = refs[:npart], refs[npart]

        @pl.when(jnp.logical_and(pl.program_id(0) == 0, pl.program_id(1) == 0))
        def _():
            o_ref[...] = jnp.zeros_like(o_ref)

        o_ref[...] += _dot_tn(a_ref[...], _cat_refs(b_refs))

    row = lambda i, k: (i, k, 0)
    return pl.pallas_call(
        body, name=name, grid=(bsz, s // tk),
        out_shape=jax.ShapeDtypeStruct((m, n), F32),
        in_specs=[pl.BlockSpec((None, tk, m), row)] + [pl.BlockSpec((None, tk, p.shape[2]), row) for p in b_parts],
        out_specs=pl.BlockSpec((m, n), lambda i, k: (0, 0)),
        compiler_params=_cp(("arbitrary", "arbitrary")),
    )(a, *b_parts)


def _rel_buckets():
    qi = np.arange(BLK)[:, None]
    kj = np.arange(2 * BLK)[None, :]
    rel = qi - kj + BLK
    n = np.maximum(rel, 0)
    nf = np.maximum(n, 1).astype(np.float32)
    large = REL_MAX_EXACT + (np.log(nf / REL_MAX_EXACT) / math.log(REL_MAX_DIST / REL_MAX_EXACT)
                             * (REL_BUCKETS - REL_MAX_EXACT)).astype(np.int32)
    large = np.minimum(large, REL_BUCKETS - 1)
    bucket = np.where(n < REL_MAX_EXACT, n, large).astype(np.int32)
    valid = ((rel >= 0) & (rel < BLK)).astype(np.int32)
    return bucket, valid


def _swa_bias(rel_bias_t, bucket, valid, name):
    nh = rel_bias_t.shape[0]

    def body(rb_ref, bk_ref, vl_ref, o_ref):
        h = pl.program_id(0)
        bk = bk_ref[...]
        acc = jnp.zeros(bk.shape, F32)
        for i in range(REL_BUCKETS):
            acc = jnp.where(bk == i, rb_ref[h, i], acc)
        o_ref[...] = jnp.where(vl_ref[...] > 0, acc, NEG)

    return pl.pallas_call(
        body, name=name, grid=(nh,),
        out_shape=jax.ShapeDtypeStruct((nh, BLK, 2 * BLK), F32),
        in_specs=[pl.BlockSpec(memory_space=pltpu.SMEM),
                  pl.BlockSpec((BLK, 2 * BLK), lambda h: (0, 0)),
                  pl.BlockSpec((BLK, 2 * BLK), lambda h: (0, 0))],
        out_specs=pl.BlockSpec((None, BLK, 2 * BLK), lambda h: (h, 0, 0)),
        compiler_params=_cp(("arbitrary",)),
    )(rel_bias_t, bucket, valid)


def _swa_scores(n, q, kw, bias_ref):
    sc = _dot_nt(q, kw) * SCALE + bias_ref[...]
    second = lax.broadcasted_iota(jnp.int32, sc.shape, 1) >= BLK
    return jnp.where(jnp.logical_or(n > 0, second), sc, NEG)


def _pad_front(dst_ref, src_ref):
    dst_ref[0:BLK, :] = jnp.zeros((BLK, dst_ref.shape[1]), dst_ref.dtype)
    dst_ref[BLK:, :] = src_ref[...]


def _swa_fwd(q, k, v, bias, sinks, name):
    b, hkv, nb, rows, hd = q.shape
    wide = bias.shape[2]
    stride = wide - BLK
    s = nb * stride

    def body(q_ref, k_ref, v_ref, bias_ref, sink_ref, o_ref, l_ref, kpad_ref, vpad_ref):
        _pad_front(kpad_ref, k_ref)
        _pad_front(vpad_ref, v_ref)
        sink = sink_ref[...]

        def step(n, carry):
            w0 = pl.multiple_of(n * stride, BLK)
            sc = _swa_scores(n, q_ref[n], kpad_ref[pl.ds(w0, wide), :], bias_ref)
            m = jnp.maximum(jnp.max(sc, axis=1, keepdims=True), sink)
            e = jnp.exp(sc - m)
            den = jnp.sum(e, axis=1, keepdims=True) + jnp.exp(sink - m)
            o_ref[n] = _dot((e * (1.0 / den)).astype(BF16), vpad_ref[pl.ds(w0, wide), :]).astype(BF16)
            l_ref[n] = m + jnp.log(den)
            return carry

        lax.fori_loop(0, nb, step, 0)

    qspec = pl.BlockSpec((None, None, nb, rows, hd), lambda i, kv: (i, kv, 0, 0, 0))
    kspec = pl.BlockSpec((None, None, s, hd), lambda i, kv: (i, kv, 0, 0))
    return pl.pallas_call(
        body, name=name, grid=(b, hkv),
        out_shape=(jax.ShapeDtypeStruct((b, hkv, nb, rows, hd), BF16), jax.ShapeDtypeStruct((b, hkv, nb, rows, 1), F32)),
        in_specs=[qspec, kspec, kspec,
                  pl.BlockSpec((None, rows, wide), lambda i, kv: (kv, 0, 0)),
                  pl.BlockSpec((None, rows, 1), lambda i, kv: (kv, 0, 0))],
        out_specs=(qspec, pl.BlockSpec((None, None, nb, rows, 1), lambda i, kv: (i, kv, 0, 0, 0))),
        scratch_shapes=[pltpu.VMEM((s + BLK, hd), BF16), pltpu.VMEM((s + BLK, hd), BF16)],
        compiler_params=_cp(("parallel", "parallel")),
    )(q, k, v, bias, sinks)


def _swa_bwd(q, k, v, bias, sinks, do, lse, name):
    b, hkv, nb, rows, hd = q.shape
    wide = bias.shape[2]
    stride = wide - BLK
    s = nb * stride

    def body(q_ref, k_ref, v_ref, bias_ref, sink_ref, do_ref, l_ref,
             dq_ref, dk_ref, dv_ref, db_ref, dsk_ref, kpad_ref, vpad_ref, dkpad_ref, dvpad_ref):
        _pad_front(kpad_ref, k_ref)
        _pad_front(vpad_ref, v_ref)
        dkpad_ref[...] = jnp.zeros_like(dkpad_ref)
        dvpad_ref[...] = jnp.zeros_like(dvpad_ref)
        db_ref[...] = jnp.zeros_like(db_ref)
        sink = sink_ref[...]

        def step(n, dsink):
            w0 = pl.multiple_of(n * stride, BLK)
            win = pl.ds(w0, wide)
            qn = q_ref[n]
            kw = kpad_ref[win, :]
            ln = l_ref[n]
            p = jnp.exp(_swa_scores(n, qn, kw, bias_ref) - ln)
            dob = do_ref[n]
            dp = _dot_nt(dob, vpad_ref[win, :])
            delta = jnp.sum(p * dp, axis=1, keepdims=True)
            ds = p * (dp - delta)
            db_ref[...] += ds
            dsb = ds.astype(BF16)
            dq_ref[n] = (_dot(dsb, kw) * SCALE).astype(BF16)
            dkpad_ref[win, :] += _dot_tn(dsb, qn)
            dvpad_ref[win, :] += _dot_tn(p.astype(BF16), dob)
            return dsink - jnp.exp(sink - ln) * delta

        dsink = lax.fori_loop(0, nb, step, jnp.zeros((rows, 1), F32))
        for g in range(A_GROUP):
            tot = jnp.zeros((1, 1), F32)
            for blk in range(rows // (A_GROUP * BLK)):
                r0 = (blk * A_GROUP + g) * BLK
                tot = tot + jnp.sum(dsink[r0:r0 + BLK, :], axis=0, keepdims=True)
            dsk_ref[g] = jnp.broadcast_to(tot, (1, LANE))
        dk_ref[...] = (dkpad_ref[BLK:, :] * SCALE).astype(BF16)
        dv_ref[...] = dvpad_ref[BLK:, :].astype(BF16)

    qspec = pl.BlockSpec((None, None, nb, rows, hd), lambda i, kv: (i, kv, 0, 0, 0))
    kspec = pl.BlockSpec((None, None, s, hd), lambda i, kv: (i, kv, 0, 0))
    return pl.pallas_call(
        body, name=name, grid=(b, hkv),
        out_shape=(jax.ShapeDtypeStruct((b, hkv, nb, rows, hd), BF16), jax.ShapeDtypeStruct((b, hkv, s, hd), BF16),
                   jax.ShapeDtypeStruct((b, hkv, s, hd), BF16), jax.ShapeDtypeStruct((b, hkv, rows, wide), F32),
                   jax.ShapeDtypeStruct((b, hkv, A_GROUP, 1, LANE), F32)),
        in_specs=[qspec, kspec, kspec,
                  pl.BlockSpec((None, rows, wide), lambda i, kv: (kv, 0, 0)),
                  pl.BlockSpec((None, rows, 1), lambda i, kv: (kv, 0, 0)),
                  qspec,
                  pl.BlockSpec((None, None, nb, rows, 1), lambda i, kv: (i, kv, 0, 0, 0))],
        out_specs=(qspec, kspec, kspec,
                   pl.BlockSpec((None, None, rows, wide), lambda i, kv: (i, kv, 0, 0)),
                   pl.BlockSpec((None, None, A_GROUP, 1, LANE), lambda i, kv: (i, kv, 0, 0, 0))),
        scratch_shapes=[pltpu.VMEM((s + BLK, hd), BF16), pltpu.VMEM((s + BLK, hd), BF16),
                        pltpu.VMEM((s + BLK, hd), F32), pltpu.VMEM((s + BLK, hd), F32)],
        compiler_params=_cp(("parallel", "parallel")),
    )(q, k, v, bias, sinks, do, lse)


def _swa_small_grads(db, dsk, bucket, name):
    b, nh = db.shape[0], db.shape[1]

    def body(db_ref, dsk_ref, bk_ref, gb_ref, gs_ref):
        acc = db_ref[0]
        sk = dsk_ref[0]
        for i in range(1, b):
            acc = acc + db_ref[i]
            sk = sk + dsk_ref[i]
        gs_ref[...] = sk
        bk = bk_ref[...]
        for i in range(REL_BUCKETS):
            part = jnp.sum(jnp.where(bk == i, acc, 0.0), axis=1, keepdims=True)
            tot = jnp.sum(part, axis=0, keepdims=True)
            gb_ref[i:i + 1, :] = jnp.broadcast_to(tot, (1, LANE))

    return pl.pallas_call(
        body, name=name, grid=(nh,),
        out_shape=(jax.ShapeDtypeStruct((nh, REL_BUCKETS, LANE), F32), jax.ShapeDtypeStruct((nh, 1, LANE), F32)),
        in_specs=[pl.BlockSpec((b, None, BLK, 2 * BLK), lambda h: (0, h, 0, 0)),
                  pl.BlockSpec((b, None, 1, LANE), lambda h: (0, h, 0, 0)),
                  pl.BlockSpec((BLK, 2 * BLK), lambda h: (0, 0))],
        out_specs=(pl.BlockSpec((None, REL_BUCKETS, LANE), lambda h: (h, 0, 0)),
                   pl.BlockSpec((None, 1, LANE), lambda h: (h, 0, 0))),
        compiler_params=_cp(("parallel",)),
    )(db, dsk, bucket)


def _log_sigmoid(z):
    return jnp.minimum(z, 0.0) - jnp.log(1.0 + jnp.exp(-jnp.abs(z)))


def _fox_decay(z, bf, name):
    b, s, w = z.shape
    nb = s // BLK

    def body(z_ref, bf_ref, f_ref):
        r = lax.broadcasted_iota(jnp.int32, (BLK, BLK), 0)
        c = lax.broadcasted_iota(jnp.int32, (BLK, BLK), 1)
        tri = (c <= r).astype(F32)

        def step(n, carry):
            r0 = pl.multiple_of(n * BLK, BLK)
            lf = _log_sigmoid(z_ref[pl.ds(r0, BLK), :] + bf_ref[...])
            f_ref[pl.ds(r0, BLK), :] = jnp.dot(tri, lf, precision=HI, preferred_element_type=F32) + carry
            return carry + jnp.sum(lf, axis=0, keepdims=True)

        lax.fori_loop(0, nb, step, jnp.zeros((1, w), F32))

    spec = pl.BlockSpec((None, s, w), lambda i: (i, 0, 0))
    return pl.pallas_call(
        body, name=name, grid=(b,), out_shape=jax.ShapeDtypeStruct((b, s, w), F32),
        in_specs=[spec, pl.BlockSpec((1, w), lambda i: (0, 0))], out_specs=spec,
        compiler_params=_cp(("parallel",)),
    )(z, bf)


def _fox_dgate(df, z, bf, nheads, name):
    b, s, w = z.shape
    nb = s // BLK

    def body(df_ref, z_ref, bf_ref, dz_ref, dbf_ref):
        @pl.when(pl.program_id(0) == 0)
        def _():
            dbf_ref[...] = jnp.zeros_like(dbf_ref)

        r = lax.broadcasted_iota(jnp.int32, (BLK, BLK), 0)
        c = lax.broadcasted_iota(jnp.int32, (BLK, BLK), 1)
        tri = (c >= r).astype(F32)
        lane = lax.broadcasted_iota(jnp.int32, (BLK, w), 1)

        def step(i, carry):
            tail, dbf = carry
            r0 = pl.multiple_of((nb - 1 - i) * BLK, BLK)
            dfb = df_ref[pl.ds(r0, BLK), :]
            dlf = jnp.dot(tri, dfb, precision=HI, preferred_element_type=F32) + tail
            dz = jnp.where(lane < nheads, dlf * _sigmoid(-(z_ref[pl.ds(r0, BLK), :] + bf_ref[...])), 0.0)
            dz_ref[pl.ds(r0, BLK), :] = dz
            return tail + jnp.sum(dfb, axis=0, keepdims=True), dbf + jnp.sum(dz, axis=0, keepdims=True)

        zero = jnp.zeros((1, w), F32)
        _, dbf = lax.fori_loop(0, nb, step, (zero, zero))
        dbf_ref[...] += dbf

    spec = pl.BlockSpec((None, s, w), lambda i: (i, 0, 0))
    one = pl.BlockSpec((1, w), lambda i: (0, 0))
    return pl.pallas_call(
        body, name=name, grid=(b,),
        out_shape=(jax.ShapeDtypeStruct((b, s, w), F32), jax.ShapeDtypeStruct((1, w), F32)),
        in_specs=[spec, spec, one], out_specs=(spec, one),
        compiler_params=_cp(("arbitrary",)),
    )(df, z, bf)


def _fox_segments(nb):
    per = max(1, nb // 4)
    return per, nb // per


def _head_masks(shape, axis):
    idx = lax.broadcasted_iota(jnp.int32, shape, axis)
    return idx < HEAD_DIM, idx >= HEAD_DIM


def _fox_fwd(proj, qblk, kblk, vblk, fcol, frow, name):
    b, s, _ = proj.shape
    nh = frow.shape[1]
    npair = nh // 2
    BLK = min(s, FOX_BQ)
    assert s % BLK == 0
    per, nseg = _fox_segments(s // BLK)

    def body(q_ref, k_ref, v_ref, fc_ref, fr_ref, o_ref, l_ref, qm_ref, kt_ref, vb_ref):
        lo, hi = _head_masks((s, LANE), 1)
        qv = q_ref[...].astype(F32) * SCALE
        qm_ref[0] = jnp.where(lo, qv, 0.0).astype(BF16)
        qm_ref[1] = jnp.where(hi, qv, 0.0).astype(BF16)
        kt_ref[...] = k_ref[...].astype(F32).T.astype(BF16)
        vb_ref[...] = v_ref[...].astype(BF16)
        lane_lo = lax.broadcasted_iota(jnp.int32, (BLK, LANE), 1) < HEAD_DIM
        tail = per * BLK
        causal = (lax.broadcasted_iota(jnp.int32, (BLK, tail), 1)
                  - lax.broadcasted_iota(jnp.int32, (BLK, tail), 0))
        for seg in range(nseg):
            w = (seg + 1) * tail

            def qstep(n, carry):
                r0 = pl.multiple_of(n * BLK, BLK)
                outs = []
                for hh in range(2):
                    sc = _dot(qm_ref[hh, pl.ds(r0, BLK), :], kt_ref[:, :w])
                    sc = sc + (fc_ref[hh, pl.ds(r0, BLK), :] - fr_ref[hh, :, :w])
                    masked = jnp.where(causal <= (n - seg * per) * BLK, sc[:, w - tail:], NEG)
                    sc = masked if seg == 0 else jnp.concatenate([sc[:, :w - tail], masked], axis=1)
                    m = jnp.max(sc, axis=1, keepdims=True)
                    e = jnp.exp(sc - m)
                    l = jnp.sum(e, axis=1, keepdims=True)
                    outs.append(_dot((e * (1.0 / l)).astype(BF16), vb_ref[:w, :]))
                    l_ref[hh, pl.ds(r0, BLK), :] = m + jnp.log(l)
                o_ref[pl.ds(r0, BLK), :] = jnp.where(lane_lo, outs[0], outs[1]).astype(BF16)
                return carry

            lax.fori_loop(seg * per, (seg + 1) * per, qstep, 0)

    def tok(blk):
        return pl.BlockSpec((None, s, LANE), lambda i, p: (i, 0, blk + p))

    col = pl.BlockSpec((None, 2, s, 1), lambda i, p: (i, p, 0, 0))
    rowspec = pl.BlockSpec((None, 2, 1, s), lambda i, p: (i, p, 0, 0))
    return pl.pallas_call(
        body, name=name, grid=(b, npair),
        out_shape=(jax.ShapeDtypeStruct((b, s, nh * HEAD_DIM), BF16), jax.ShapeDtypeStruct((b, nh, s, 1), F32)),
        in_specs=[tok(qblk), tok(kblk), tok(vblk), col, rowspec],
        out_specs=(pl.BlockSpec((None, s, LANE), lambda i, p: (i, 0, p)), col),
        scratch_shapes=[pltpu.VMEM((2, s, LANE), BF16), pltpu.VMEM((LANE, s), BF16), pltpu.VMEM((s, LANE), BF16)],
        compiler_params=_cp(("parallel", "parallel")),
    )(proj, proj, proj, fcol, frow)


def _fox_bwd(proj, qblk, kblk, vblk, dmix, doblk, fcol, frow, frowb, lse, lserowb, name):
    b, s, _ = proj.shape
    nh = frow.shape[1]
    npair = nh // 2
    BLK = min(s, FOX_BQ)
    assert s % BLK == 0
    nb = s // BLK
    per, nseg = _fox_segments(nb)

    def body(q_ref, k_ref, v_ref, do_ref, fc_ref, fr_ref, frb_ref, l_ref, lrb_ref,
             dq_ref, dk_ref, dv_ref, dfr_ref,
             qm_ref, dom_ref, kb_ref, vb_ref, kt_ref, vt_ref, qtm_ref, dotm_ref, dka_ref, dva_ref):
        lo, hi = _head_masks((s, LANE), 1)
        qv = q_ref[...].astype(F32) * SCALE
        dov = do_ref[...]
        for hh, msk in enumerate((lo, hi)):
            qm_ref[hh] = jnp.where(msk, qv, 0.0).astype(BF16)
            dom_ref[hh] = jnp.where(msk, dov, 0.0).astype(BF16)
        kv = k_ref[...].astype(F32)
        vv = v_ref[...].astype(F32)
        kb_ref[...] = kv.astype(BF16)
        vb_ref[...] = vv.astype(BF16)
        kt_ref[...] = kv.T.astype(BF16)
        vt_ref[...] = vv.T.astype(BF16)
        rlo, rhi = _head_masks((LANE, BLK), 0)

        def tstep(n, carry):
            r0 = pl.multiple_of(n * BLK, BLK)
            qt = (q_ref[pl.ds(r0, BLK), :].astype(F32) * SCALE).T
            dt = do_ref[pl.ds(r0, BLK), :].astype(F32).T
            for hh, msk in enumerate((rlo, rhi)):
                qtm_ref[hh, n] = jnp.where(msk, qt, 0.0).astype(BF16)
                dotm_ref[hh, n] = jnp.where(msk, dt, 0.0).astype(BF16)
            return carry

        lax.fori_loop(0, nb, tstep, 0)
        dka_ref[...] = jnp.zeros_like(dka_ref)
        dva_ref[...] = jnp.zeros_like(dva_ref)
        dfr_ref[...] = jnp.zeros_like(dfr_ref)
        lane_lo = lax.broadcasted_iota(jnp.int32, (BLK, LANE), 1) < HEAD_DIM
        tail = per * BLK
        causal = (lax.broadcasted_iota(jnp.int32, (BLK, tail), 1)
                  - lax.broadcasted_iota(jnp.int32, (BLK, tail), 0))
        causal_t = (lax.broadcasted_iota(jnp.int32, (tail, BLK), 0)
                    - lax.broadcasted_iota(jnp.int32, (tail, BLK), 1))
        for seg in range(nseg):
            w = (seg + 1) * tail

            def nstep(n, carry):
                r0 = pl.multiple_of(n * BLK, BLK)
                lim = (n - seg * per) * BLK
                dqs = []
                for hh in range(2):
                    qn = qm_ref[hh, pl.ds(r0, BLK), :]
                    don = dom_ref[hh, pl.ds(r0, BLK), :]
                    sc = _dot(qn, kt_ref[:, :w]) + ((fc_ref[hh, pl.ds(r0, BLK), :] - l_ref[hh, pl.ds(r0, BLK), :])
                                                   - fr_ref[hh, :, :w])
                    masked = jnp.where(causal <= lim, sc[:, w - tail:], NEG)
                    p = jnp.exp(masked if seg == 0 else jnp.concatenate([sc[:, :w - tail], masked], axis=1))
                    dp = _dot(don, vt_ref[:, :w])
                    ds = p * (dp - jnp.sum(p * dp, axis=1, keepdims=True))
                    dqs.append(_dot(ds.astype(BF16), kb_ref[:w, :]))
                    dfr_ref[hh, :, :w] -= jnp.sum(ds, axis=0, keepdims=True)
                    sct = _dot(kb_ref[:w, :], qtm_ref[hh, n]) + ((frb_ref[hh, n] - lrb_ref[hh, n]) - fc_ref[hh, :w, :])
                    masked_t = jnp.where(causal_t <= lim, sct[w - tail:, :], NEG)
                    pt = jnp.exp(masked_t if seg == 0 else jnp.concatenate([sct[:w - tail, :], masked_t], axis=0))
                    dpt = _dot(vb_ref[:w, :], dotm_ref[hh, n])
                    dst = pt * (dpt - jnp.sum(pt * dpt, axis=0, keepdims=True))
                    dka_ref[:w, :] += _dot(dst.astype(BF16), qn)
                    dva_ref[:w, :] += _dot(pt.astype(BF16), don)
                dq_ref[pl.ds(r0, BLK), :] = (jnp.where(lane_lo, dqs[0], dqs[1]) * SCALE).astype(BF16)
                return carry

            lax.fori_loop(seg * per, (seg + 1) * per, nstep, 0)
        dk_ref[...] = dka_ref[...].astype(BF16)
        dv_ref[...] = dva_ref[...].astype(BF16)

    def tok(blk):
        return pl.BlockSpec((None, s, LANE), lambda i, p: (i, 0, blk + p))

    col = pl.BlockSpec((None, 2, s, 1), lambda i, p: (i, p, 0, 0))
    rowspec = pl.BlockSpec((None, 2, 1, s), lambda i, p: (i, p, 0, 0))
    rowbspec = pl.BlockSpec((None, 2, nb, 1, BLK), lambda i, p: (i, p, 0, 0, 0))
    outtok = pl.BlockSpec((None, s, LANE), lambda i, p: (i, 0, p))
    shp = jax.ShapeDtypeStruct((b, s, nh * HEAD_DIM), BF16)
    return pl.pallas_call(
        body, name=name, grid=(b, npair),
        out_shape=(shp, shp, shp, jax.ShapeDtypeStruct((b, nh, 1, s), F32)),
        in_specs=[tok(qblk), tok(kblk), tok(vblk),
                  pl.BlockSpec((None, s, LANE), lambda i, p: (i, 0, doblk + p)),
                  col, rowspec, rowbspec, col, rowbspec],
        out_specs=(outtok, outtok, outtok, rowspec),
        scratch_shapes=[pltpu.VMEM((2, s, LANE), BF16), pltpu.VMEM((2, s, LANE), BF16),
                        pltpu.VMEM((s, LANE), BF16), pltpu.VMEM((s, LANE), BF16),
                        pltpu.VMEM((LANE, s), BF16), pltpu.VMEM((LANE, s), BF16),
                        pltpu.VMEM((2, nb, LANE, BLK), BF16), pltpu.VMEM((2, nb, LANE, BLK), BF16),
                        pltpu.VMEM((s, LANE), F32), pltpu.VMEM((s, LANE), F32)],
        compiler_params=_cp(("parallel", "parallel")),
    )(proj, proj, proj, dmix, fcol, frow, frowb, lse, lserowb)


def _expm1(x):
    poly = x * (1.0 + x * (1.0 / 2 + x * (1.0 / 6 + x * (1.0 / 24 + x * (1.0 / 120 + x * (1.0 / 720))))))
    return jnp.where(x > -0.1, poly, jnp.exp(x) - 1.0)


def _softplus(z):
    return jnp.maximum(z, 0.0) + jnp.log(1.0 + jnp.exp(-jnp.abs(z)))


def _scan_rows(a, u, carry, row, up):
    tc, c = a.shape
    d = 1
    while d < tc:
        if d < SUBLANE:
            keep = (row >= d) if up else (row < tc - d)
            shift = d if up else tc - d
            a_sh = jnp.where(keep, pltpu.roll(a, shift, 0), 1.0)
            u_sh = jnp.where(keep, pltpu.roll(u, shift, 0), 0.0)
        elif up:
            a_sh = jnp.concatenate([jnp.ones((d, c), F32), a[:tc - d]], axis=0)
            u_sh = jnp.concatenate([jnp.zeros((d, c), F32), u[:tc - d]], axis=0)
        else:
            a_sh = jnp.concatenate([a[d:], jnp.ones((d, c), F32)], axis=0)
            u_sh = jnp.concatenate([u[d:], jnp.zeros((d, c), F32)], axis=0)
        u = a * u_sh + u
        a = a * a_sh
        d *= 2
    return u + a * carry


def _scan_up(a, u, carry, row):
    return _scan_rows(a, u, carry, row, True)


def _scan_down(bnext, g, carry, row):
    return _scan_rows(bnext, g, carry, row, False)


def _pick_row(val, row, which):
    return jnp.sum(jnp.where(row == which, val, 0.0), axis=0, keepdims=True)


def _lru_gates(xpad_ref, t0, tc, cw_ref, cb_ref, wa, ba_ref, wx, bx_ref, sp):
    xw = xpad_ref[pl.ds(t0, tc + SUBLANE), :]
    xc = cb_ref[...]
    for j in range(CONV_WIDTH):
        sh = CONV_WIDTH - 1 - j
        xs = xw if sh == 0 else pltpu.roll(xw, sh, 0)
        xc = xc + xs[SUBLANE:, :] * cw_ref[j:j + 1, :]
    xcb = xc.astype(BF16)
    r = _sigmoid(_dot(xcb, wa) + ba_ref[...])
    i = _sigmoid(_dot(xcb, wx) + bx_ref[...])
    la = -LRU_C * r * sp
    return xc, r, i, la


def _lru_specs(s, cb):
    seq = lambda bi, ni: (bi, 0, ni)
    return dict(
        seq=pl.BlockSpec((None, s, cb), seq),
        cw=pl.BlockSpec((CONV_WIDTH, cb), lambda bi, ni: (0, ni)),
        vec=pl.BlockSpec((1, cb), lambda bi, ni: (0, ni)),
        wblk=pl.BlockSpec((None, cb, cb), lambda bi, ni: (ni, 0, 0)),
    )


def _lru_fwd(proj, cw, cb_, wa, ba, wx, bx, lam, name):
    b, s, _ = proj.shape
    nblk, cb, _ = wa.shape
    tc = min(s, SCAN_CHUNK)
    nc = s // tc

    def body(x_ref, cw_ref, cb_ref, wa_ref, ba_ref, wx_ref, bx_ref, lam_ref, hs_ref, xpad_ref):
        xpad_ref[0:SUBLANE, :] = jnp.zeros((SUBLANE, cb), F32)
        xpad_ref[SUBLANE:, :] = x_ref[...].astype(F32)
        wa_b = wa_ref[...].astype(BF16)
        wx_b = wx_ref[...].astype(BF16)
        sp = _softplus(-lam_ref[...])
        row = lax.broadcasted_iota(jnp.int32, (tc, cb), 0)

        def chunk(ci, carry):
            t0 = pl.multiple_of(ci * tc, tc)
            xc, r, i, la = _lru_gates(xpad_ref, t0, tc, cw_ref, cb_ref, wa_b, ba_ref, wx_b, bx_ref, sp)
            a = jnp.exp(la)
            u = jnp.sqrt(-_expm1(2.0 * la)) * (i * xc)
            h = _scan_up(a, u, carry, row)
            hs_ref[pl.ds(t0, tc), :] = h
            return _pick_row(h, row, tc - 1)

        lax.fori_loop(0, nc, chunk, jnp.zeros((1, cb), F32))

    sp_ = _lru_specs(s, cb)
    return pl.pallas_call(
        body, name=name, grid=(b, nblk),
        out_shape=jax.ShapeDtypeStruct((b, s, nblk * cb), F32),
        in_specs=[sp_["seq"], sp_["cw"], sp_["vec"], sp_["wblk"], sp_["vec"], sp_["wblk"], sp_["vec"], sp_["vec"]],
        out_specs=sp_["seq"],
        scratch_shapes=[pltpu.VMEM((s + SUBLANE, cb), F32)],
        compiler_params=_cp(("parallel", "parallel")),
    )(proj, cw, cb_, wa, ba, wx, bx, lam)


def _lru_bwd(proj, hs, dhs, cw, cb_, wa, ba, wx, bx, lam, name):
    b, s, _ = proj.shape
    nblk, cb, _ = wa.shape
    tc = min(s, SCAN_CHUNK)
    nc = s // tc

    def body(x_ref, hs_ref, dhs_ref, cw_ref, cb_ref, wa_ref, ba_ref, wx_ref, bx_ref, lam_ref,
             dx_ref, dcw_ref, dcb_ref, dwa_ref, dba_ref, dwx_ref, dbx_ref, dlam_ref,
             xpad_ref, hpad_ref, dcpad_ref, xc_ref, r_ref, i_ref, a_ref, mult_ref):
        @pl.when(pl.program_id(1) == 0)
        def _():
            for ref in (dcw_ref, dcb_ref, dwa_ref, dba_ref, dwx_ref, dbx_ref, dlam_ref):
                ref[...] = jnp.zeros_like(ref)

        zeros8 = jnp.zeros((SUBLANE, cb), F32)
        xpad_ref[0:SUBLANE, :] = zeros8
        xpad_ref[SUBLANE:, :] = x_ref[...].astype(F32)
        hpad_ref[0:SUBLANE, :] = zeros8
        hpad_ref[SUBLANE:, :] = hs_ref[...]
        dcpad_ref[s:s + SUBLANE, :] = zeros8
        wa_b = wa_ref[...].astype(BF16)
        wx_b = wx_ref[...].astype(BF16)
        lam_v = lam_ref[...]
        sp = _softplus(-lam_v)
        dsp_dlam = -_sigmoid(-lam_v)
        row = lax.broadcasted_iota(jnp.int32, (tc, cb), 0)

        def recompute(ci, carry):
            t0 = pl.multiple_of(ci * tc, tc)
            xc, r, i, la = _lru_gates(xpad_ref, t0, tc, cw_ref, cb_ref, wa_b, ba_ref, wx_b, bx_ref, sp)
            xc_ref[pl.ds(t0, tc), :] = xc
            r_ref[pl.ds(t0, tc), :] = r
            i_ref[pl.ds(t0, tc), :] = i
            a_ref[pl.ds(t0, tc), :] = jnp.exp(la)
            mult_ref[pl.ds(t0, tc), :] = jnp.sqrt(-_expm1(2.0 * la))
            return carry

        lax.fori_loop(0, nc, recompute, 0)

        def adjoint(k, carry):
            g_next, a_first_next = carry
            t0 = pl.multiple_of((nc - 1 - k) * tc, tc)
            a = a_ref[pl.ds(t0, tc), :]
            a_next = jnp.where(row == tc - 1, a_first_next, pltpu.roll(a, tc - 1, 0))
            gg = _scan_down(a_next, dhs_ref[pl.ds(t0, tc), :], g_next, row)
            h_prev = pltpu.roll(hpad_ref[pl.ds(t0, tc + SUBLANE), :], 1, 0)[SUBLANE:, :]
            xc = xc_ref[pl.ds(t0, tc), :]
            r = r_ref[pl.ds(t0, tc), :]
            i = i_ref[pl.ds(t0, tc), :]
            mult = mult_ref[pl.ds(t0, tc), :]
            d_mult = gg * i * xc
            d_i = gg * mult * xc
            d_xc = gg * mult * i
            d_la = gg * h_prev * a - d_mult * (a * a) / mult
            d_zr = (d_la * (-LRU_C * sp)) * r * (1.0 - r)
            d_zi = d_i * i * (1.0 - i)
            dlam_ref[...] += jnp.sum(d_la * (-LRU_C * r), axis=0, keepdims=True) * dsp_dlam
            dzr_b = d_zr.astype(BF16)
            dzi_b = d_zi.astype(BF16)
            xcb = xc.astype(BF16)
            d_xc = d_xc + _dot_nt(dzr_b, wa_b) + _dot_nt(dzi_b, wx_b)
            dwa_ref[...] += _dot_tn(xcb, dzr_b)
            dwx_ref[...] += _dot_tn(xcb, dzi_b)
            dba_ref[...] += jnp.sum(d_zr, axis=0, keepdims=True)
            dbx_ref[...] += jnp.sum(d_zi, axis=0, keepdims=True)
            dcb_ref[...] += jnp.sum(d_xc, axis=0, keepdims=True)
            dcpad_ref[pl.ds(t0, tc), :] = d_xc
            return _pick_row(gg, row, 0), _pick_row(a, row, 0)

        zero = jnp.zeros((1, cb), F32)
        lax.fori_loop(0, nc, adjoint, (zero, zero))

        def conv_back(ci, carry):
            t0 = pl.multiple_of(ci * tc, tc)
            dw = dcpad_ref[pl.ds(t0, tc + SUBLANE), :]
            xw = xpad_ref[pl.ds(t0, tc + SUBLANE), :]
            d_xc = dw[:tc, :]
            dxr = jnp.zeros((tc, cb), F32)
            for j in range(CONV_WIDTH):
                sh = CONV_WIDTH - 1 - j
                dsh = dw if sh == 0 else pltpu.roll(dw, tc + SUBLANE - sh, 0)
                dxr = dxr + dsh[:tc, :] * cw_ref[j:j + 1, :]
                xs = xw if sh == 0 else pltpu.roll(xw, sh, 0)
                dcw_ref[j:j + 1, :] += jnp.sum(d_xc * xs[SUBLANE:, :], axis=0, keepdims=True)
            dx_ref[pl.ds(t0, tc), :] = dxr.astype(BF16)
            return carry

        lax.fori_loop(0, nc, conv_back, 0)

    seq = lambda ni, bi: (bi, 0, ni)
    seqspec = pl.BlockSpec((None, s, cb), seq)
    cwspec = pl.BlockSpec((CONV_WIDTH, cb), lambda ni, bi: (0, ni))
    vec = pl.BlockSpec((1, cb), lambda ni, bi: (0, ni))
    wblk = pl.BlockSpec((None, cb, cb), lambda ni, bi: (ni, 0, 0))
    w = nblk * cb
    return pl.pallas_call(
        body, name=name, grid=(nblk, b),
        out_shape=(jax.ShapeDtypeStruct((b, s, w), BF16), jax.ShapeDtypeStruct((CONV_WIDTH, w), F32),
                   jax.ShapeDtypeStruct((1, w), F32), jax.ShapeDtypeStruct((nblk, cb, cb), F32),
                   jax.ShapeDtypeStruct((1, w), F32), jax.ShapeDtypeStruct((nblk, cb, cb), F32),
                   jax.ShapeDtypeStruct((1, w), F32), jax.ShapeDtypeStruct((1, w), F32)),
        in_specs=[seqspec, seqspec, seqspec, cwspec, vec, wblk, vec, wblk, vec, vec],
        out_specs=(seqspec, cwspec, vec, wblk, vec, wblk, vec, vec),
        scratch_shapes=[pltpu.VMEM((s + SUBLANE, cb), F32)] * 3 + [pltpu.VMEM((s, cb), F32)] * 5,
        compiler_params=_cp(("parallel", "arbitrary")),
    )(proj, hs, dhs, cw, cb_, wa, ba, wx, bx, lam)


def _adamw(w, g, m, v, name):
    shape = w.shape
    total = int(np.prod(shape))
    if w.ndim >= 2 and shape[-2] % SUBLANE == 0:
        rows, cols = shape[-2:]
    else:
        cols = 1024
        rows = -(-(-(-total // cols)) // SUBLANE) * SUBLANE
    lead = -(-total // (rows * cols))
    tr = _row_tile(rows, 512)
    pad = lead * rows * cols - total

    def flat(a):
        if pad:
            a = jnp.pad(a.reshape(-1), (0, pad))
        return a.reshape(lead, rows, cols)

    c1 = 1.0 - ADAM_B1 ** ADAM_STEP
    c2 = 1.0 - ADAM_B2 ** ADAM_STEP

    def body(w_ref, g_ref, m_ref, v_ref, d_ref, nm_ref, nv_ref):
        gv = g_ref[...]
        nm = ADAM_B1 * m_ref[...] + (1.0 - ADAM_B1) * gv
        nv = ADAM_B2 * v_ref[...] + (1.0 - ADAM_B2) * (gv * gv)
        nm_ref[...] = nm
        nv_ref[...] = nv
        d_ref[...] = -ADAM_LR * ((nm / c1) / (jnp.sqrt(nv / c2) + ADAM_EPS) + ADAM_WD * w_ref[...])

    spec = pl.BlockSpec((None, tr, cols), lambda l, i: (l, i, 0))
    shp = jax.ShapeDtypeStruct((lead, rows, cols), F32)
    outs = pl.pallas_call(
        body, name=name, grid=(lead, rows // tr), out_shape=(shp, shp, shp),
        in_specs=[spec] * 4, out_specs=(spec,) * 3,
        compiler_params=_cp(("parallel", "parallel")),
    )(flat(w), flat(g), flat(m), flat(v))
    if pad:
        return tuple(o.reshape(-1)[:total].reshape(shape) for o in outs)
    return tuple(o.reshape(shape) for o in outs)


def _to_heads(t, nh):
    b, s, _ = t.shape
    return t.reshape(b, s, nh, HEAD_DIM).transpose(0, 2, 1, 3)


def _stack_heads(t):
    b, s, _ = t.shape
    steps = s // (SWA_BPS * BLK)
    t = t.reshape(b, steps, SWA_BPS, BLK, A_KV_HEADS, A_GROUP, HEAD_DIM).transpose(0, 4, 1, 2, 5, 3, 6)
    return t.reshape(b, A_KV_HEADS, steps, SWA_BPS * A_GROUP * BLK, HEAD_DIM)


def _unstack_heads(t):
    b, hkv, steps, rows, hd = t.shape
    t = t.reshape(b, hkv, steps, SWA_BPS, A_GROUP, BLK, hd).transpose(0, 2, 3, 5, 1, 4, 6)
    return t.reshape(b, steps * SWA_BPS * BLK, hkv * A_GROUP * hd)


def _from_heads(t):
    b, nh, s, hd = t.shape
    return t.transpose(0, 2, 1, 3).reshape(b, s, nh * hd)


def _pad_rows(a, mult):
    r = a.shape[0]
    p = (-r) % mult
    return jnp.pad(a, ((0, p), (0, 0))) if p else a


def kernel(x, c, rel_bias, norm_g, ada_w, ada_b, attn_w_in, attn_sinks, attn_b_f, attn_w_out, lru_w_in, lru_conv_w, lru_conv_b, lru_w_a, lru_b_a, lru_w_x, lru_b_x, lru_lambda, lru_w_out, final_g, loss_target, m_rel_bias, m_norm_g, m_ada_w, m_ada_b, m_attn_w_in, m_attn_sinks, m_attn_b_f, m_attn_w_out, m_lru_w_in, m_lru_conv_w, m_lru_conv_b, m_lru_w_a, m_lru_b_a, m_lru_w_x, m_lru_b_x, m_lru_lambda, m_lru_w_out, m_final_g, v_rel_bias, v_norm_g, v_ada_w, v_ada_b, v_attn_w_in, v_attn_sinks, v_attn_b_f, v_attn_w_out, v_lru_w_in, v_lru_conv_w, v_lru_conv_b, v_lru_w_a, v_lru_b_a, v_lru_w_x, v_lru_b_x, v_lru_lambda, v_lru_w_out, v_final_g):
    bl, s, d = x.shape
    ix, iy, ic = lax.axis_index("x"), lax.axis_index("y"), lax.axis_index("c")
    chip = 2 * ix + iy
    me = 2 * chip + ic
    nb = s // BLK
    aw = A_Q_HEADS * HEAD_DIM
    akv = A_KV_HEADS * HEAD_DIM
    bw = B_HEADS * HEAD_DIM
    mixw = aw + bw
    qkv_w = aw + 2 * akv + 3 * bw
    n_in = attn_w_in.shape[2] * N_CHIP
    lw = lru_lambda.shape[1] * N_CHIP
    n0 = mixw + qkv_w + LANE

    rows_pad = -(-bl // SUBLANE) * SUBLANE
    vec_rows = jnp.concatenate([lru_conv_w[0], lru_conv_b, lru_b_a, lru_b_x, lru_lambda], axis=0)
    first = jnp.concatenate([_pad_rows(c, SUBLANE), jnp.pad(vec_rows, ((0, 0), (0, d - lw // N_CHIP)))], axis=0)
    first = _all_gather8(first, "gather_c", pltpu.VMEM).reshape(N_DEV, rows_pad + SUBLANE, d)
    c_all = first[:, :bl].reshape(N_DEV * bl, d)
    vec_all = first[:, rows_pad:, :lw // N_CHIP].reshape(N_CHIP, 2, SUBLANE, lw // N_CHIP)[:, 0]
    vec_all = vec_all.transpose(1, 0, 2).reshape(SUBLANE, lw)
    ncol = ada_w.shape[2]
    ada_w_l = lax.dynamic_index_in_dim(ada_w, ic, 0, keepdims=False)
    ada_b_l = lax.dynamic_slice(ada_b, (ic, chip * ncol), (1, ncol))
    mod_part = _ada_fwd(c_all, ada_w_l, ada_b_l, "ada_fwd")
    mod_all = _all_gather8(_pad_rows(mod_part, SUBLANE), "gather_mod", pltpu.VMEM)
    mrows = -(-(N_DEV * bl) // SUBLANE) * SUBLANE
    mod_all = mod_all.reshape(N_CHIP, 2, mrows, ncol)[:, :, :N_DEV * bl]
    mod_all = mod_all.transpose(1, 2, 0, 3).reshape(2, N_DEV * bl, N_CHIP * ncol)
    mod = lax.dynamic_slice_in_dim(mod_all, me * bl, bl, axis=1)
    shift = [mod[l, :, 0:d].reshape(bl, 1, d) for l in range(2)]
    scale = [mod[l, :, d:2 * d].reshape(bl, 1, d) for l in range(2)]
    gmod = [mod[l, :, 2 * d:3 * d].reshape(bl, 1, d) for l in range(2)]

    c_in0 = n_in // N_CHIP
    c_in1 = 2 * lw // N_CHIP
    assert c_in0 <= d and 2 * c_in1 == d
    r_in0, r_out0, r_in1, r_out1 = d // 2, mixw // N_CHIP // 2, d // 4, lw // N_CHIP // 2
    o_out0, o_in1, o_out1 = r_in0, r_in0 + r_out0, r_in0 + r_out0 + r_in1
    big_rows = o_out1 + r_out1

    def half_of(a, rows):
        return lax.dynamic_slice_in_dim(a, ic * rows, rows, axis=0)

    h_in1 = half_of(lru_w_in[0], r_in0).astype(BF16)
    my_half = jnp.concatenate([
        jnp.pad(half_of(attn_w_in[0], r_in0).astype(BF16), ((0, 0), (0, d - c_in0))),
        half_of(attn_w_out[0], r_out0).astype(BF16),
        jnp.concatenate([h_in1[:r_in1], h_in1[r_in1:]], axis=1),
        half_of(lru_w_out[0], r_out1).astype(BF16)], axis=0)
    gat = _all_gather8(my_half, "gather_weights", pltpu.HBM).reshape(N_CHIP, 2, big_rows, d)
    w_in0 = gat[:, :, :r_in0, :c_in0].transpose(1, 2, 0, 3).reshape(d, n_in)
    w_out0 = gat[:, :, o_out0:o_in1].reshape(mixw, d)
    w_in1 = gat[:, :, o_in1:o_out1].reshape(N_CHIP, 2, r_in1, 2, c_in1)
    w_in1 = w_in1.transpose(1, 3, 2, 0, 4).reshape(d, 2 * lw)
    w_out1 = gat[:, :, o_out1:].reshape(lw, d)
    w_cat0 = jnp.concatenate([w_in0[:, qkv_w + B_HEADS:], w_in0[:, :qkv_w + B_HEADS],
                              jnp.zeros((d, n0 - n_in), BF16)], axis=1)

    proj0, h0, zf = _norm_proj(x, norm_g[0:1], scale[0], shift[0], w_cat0, LANE, "norm_proj0")
    o_a = mixw
    aq = _stack_heads(proj0[:, :, o_a:o_a + aw].astype(BF16))
    ak = _to_heads(proj0[:, :, o_a + aw:o_a + aw + akv].astype(BF16), A_KV_HEADS)
    av = _to_heads(proj0[:, :, o_a + aw + akv:o_a + aw + 2 * akv].astype(BF16), A_KV_HEADS)
    o_b = o_a + aw + 2 * akv
    fox_blks = (o_b // LANE, (o_b + bw) // LANE, (o_b + 2 * bw) // LANE)
    bucket_np, valid_np = _rel_buckets()
    bucket = jnp.asarray(bucket_np)
    bias = _swa_bias(rel_bias.T, bucket, jnp.asarray(valid_np), "swa_bias")
    bias = bias.reshape(A_KV_HEADS, A_GROUP * BLK, 2 * BLK)
    bias = jnp.concatenate([jnp.pad(bias, ((0, 0), (0, 0), (blk * BLK, (SWA_BPS - 1 - blk) * BLK)),
                                    constant_values=NEG) for blk in range(SWA_BPS)], axis=1)
    sinks = jnp.repeat(attn_sinks[0].reshape(A_KV_HEADS, A_GROUP), BLK, axis=1).reshape(A_KV_HEADS, A_GROUP * BLK, 1)
    sinks = jnp.tile(sinks, (1, SWA_BPS, 1))
    a_out, a_lse = _swa_fwd(aq, ak, av, bias, sinks, "swa_fwd")
    bf_pad = jnp.pad(attn_b_f, ((0, 0), (0, LANE - B_HEADS)))
    fsum = _fox_decay(zf, bf_pad, "fox_decay")
    fh = fsum[:, :, :B_HEADS].transpose(0, 2, 1)
    fcol = fh.reshape(bl, B_HEADS, s, 1)
    frow = fh.reshape(bl, B_HEADS, 1, s)
    fbq = min(s, FOX_BQ)
    frowb = fh.reshape(bl, B_HEADS, s // fbq, 1, fbq)
    b_out, b_lse = _fox_fwd(proj0, *fox_blks, fcol, frow, "fox_fwd")
    lserowb = b_lse.reshape(bl, B_HEADS, s // fbq, 1, fbq)
    mix0 = [_unstack_heads(a_out), b_out]
    x1, o0 = _gate_outproj(mix0, proj0, 0, w_out0, x, gmod[0], "gate_outproj0")

    proj1, h1 = _norm_proj(x1, norm_g[1:2], scale[1], shift[1], w_in1, 0, "norm_proj1")
    cw_f, cb_f, ba_f, bx_f, lam_f = vec_all[0:4], vec_all[4:5], vec_all[5:6], vec_all[6:7], vec_all[7:8]
    hs = _lru_fwd(proj1, cw_f, cb_f, lru_w_a[0], ba_f, lru_w_x[0], bx_f, lam_f, "lru_fwd")
    x2, o1 = _gate_outproj([hs], proj1, 1, w_out1, x1, gmod[1], "gate_outproj1")

    loss_vec, dx2, g_final = _final_loss(x2, final_g.reshape(1, d), loss_target, "final_loss")
    loss = lax.psum(loss_vec[0, 0], ("x", "y", "c"))

    dhs, dgate1, do1, y1, dgm1 = _bwd_out(dx2, gmod[1], o1, [hs], proj1, 1, w_out1, F32, "bwd_out1")
    g_w_out1 = _matmul_tn(y1, [do1], "grad_w_out1")
    (dxr, g_cw, g_cb, g_wa, g_ba, g_wx, g_bx, g_lam) = _lru_bwd(
        proj1, hs, dhs, cw_f, cb_f, lru_w_a[0], ba_f, lru_w_x[0], bx_f, lam_f, "lru_bwd")
    dproj1 = [dxr, dgate1]
    g_w_in1 = _matmul_tn(h1, dproj1, "grad_w_in1")
    dx1, dsh1, dsc1, g_ng1 = _bwd_in(dproj1, w_in1, x1, norm_g[1:2], scale[1], dx2, "bwd_in1")

    dmix0, dgate0, do0, y0, dgm0 = _bwd_out(dx1, gmod[0], o0, mix0, proj0, 0, w_out0, BF16, "bwd_out0")
    g_w_out0 = _matmul_tn(y0, [do0], "grad_w_out0")
    da_out = _stack_heads(dmix0[:, :, :aw].astype(BF16))
    daq, dak, dav, dbias, dsink = _swa_bwd(aq, ak, av, bias, sinks, da_out, a_lse, "swa_bwd")
    dbq, dbk, dbv, dfrow = _fox_bwd(proj0, *fox_blks, dmix0, aw // LANE, fcol, frow, frowb, b_lse, lserowb,
                                    "fox_bwd")
    df = dfrow.reshape(bl, B_HEADS, s).transpose(0, 2, 1)
    df = jnp.pad(df, ((0, 0), (0, 0), (0, LANE - B_HEADS)))
    dzf, g_bf = _fox_dgate(df, zf, bf_pad, B_HEADS, "fox_dgate")
    dproj0 = ([dgate0, _unstack_heads(daq), _from_heads(dak), _from_heads(dav)]
              + [dbq, dbk, dbv, dzf.astype(BF16)])
    g_w_cat0 = _matmul_tn(h0, dproj0, "grad_w_in0")
    g_w_in0 = jnp.concatenate([g_w_cat0[:, mixw:mixw + qkv_w + B_HEADS], g_w_cat0[:, :mixw]], axis=1)
    dx0, dsh0, dsc0, g_ng0 = _bwd_in(dproj0, w_cat0, x, norm_g[0:1], scale[0], dx1, "bwd_in0")
    dbias = sum(dbias[:, :, blk * A_GROUP * BLK:(blk + 1) * A_GROUP * BLK, blk * BLK:(blk + 2) * BLK]
                for blk in range(SWA_BPS))
    g_relb, g_sink = _swa_small_grads(dbias.reshape(bl, A_Q_HEADS, BLK, 2 * BLK),
                                      dsink.reshape(bl, A_Q_HEADS, 1, LANE), bucket, "swa_small_grads")

    dmod = jnp.concatenate([jnp.concatenate([dsh0, dsc0, dgm0], axis=-1),
                            jnp.concatenate([dsh1, dsc1, dgm1], axis=-1)], axis=1)
    dmod_rows = _pad_rows(dmod.reshape(bl * 6, d), SUBLANE)

    tail = jnp.concatenate([g_relb[:, :, 0].T.reshape(-1), g_sink[:, 0, 0], g_bf[0, :B_HEADS]])
    n_relb = REL_BUCKETS * A_Q_HEADS
    small_rows = [g_wa.reshape(-1, d), g_wx.reshape(-1, d), g_ng0, g_ng1, g_final, g_cw, g_cb, g_ba, g_bx, g_lam,
                  jnp.pad(tail, (0, d - tail.shape[0])).reshape(1, d)]
    small_counts = [r.shape[0] for r in small_rows]
    piece_rows = -(-(-(-sum(small_counts) // N_DEV)) // SUBLANE) * SUBLANE
    small_2d = jnp.concatenate(small_rows, axis=0)
    small_2d = jnp.pad(small_2d, ((0, N_DEV * piece_rows - small_2d.shape[0]), (0, 0)))
    small_pieces = small_2d.reshape(N_CHIP, 2, piece_rows, d)
    p_in0 = jnp.pad(g_w_in0.reshape(2, r_in0, N_CHIP, c_in0).transpose(2, 0, 1, 3),
                    ((0, 0), (0, 0), (0, 0), (0, d - c_in0)))
    p_in1 = g_w_in1.reshape(2, 2, r_in1, N_CHIP, c_in1).transpose(3, 0, 2, 1, 4).reshape(N_CHIP, 2, r_in1, d)
    pieces = jnp.concatenate([p_in0, g_w_out0.reshape(N_CHIP, 2, r_out0, d), p_in1,
                              g_w_out1.reshape(N_CHIP, 2, r_out1, d), small_pieces], axis=2)
    theirs = _sibling_push(pieces, True, "push_sibling_halves")
    partial = _pair_sum(jnp.reshape(ic, (1,)).astype(jnp.int32), pieces, theirs, "sum_chip")
    slots = _chip_all_to_all(partial, "exchange_grads")
    reduced = _sum_slots(slots, "sum_grads")
    mine_big = reduced[:big_rows]
    other_big = _sibling_push(mine_big[None], False, "swap_halves")[0]
    both = jnp.stack([jnp.where(ic == 0, mine_big, other_big), jnp.where(ic == 0, other_big, mine_big)])
    g_big = [both[:, :r_in0, :c_in0].reshape(d, c_in0),
             both[:, o_out0:o_in1].reshape(2 * r_out0, d),
             both[:, o_in1:o_out1].reshape(2, r_in1, 2, c_in1).transpose(0, 2, 1, 3).reshape(d, c_in1),
             both[:, o_out1:].reshape(2 * r_out1, d)]
    last = _all_gather8(jnp.concatenate([reduced[big_rows:], dmod_rows], axis=0), "gather_small_grads", pltpu.VMEM)
    last = last.reshape(N_DEV, piece_rows + dmod_rows.shape[0], d)
    small_all = last[:, :piece_rows].reshape(N_DEV * piece_rows, d)
    dmod_all = last[:, piece_rows:piece_rows + bl * 6].reshape(N_DEV * bl, 6 * d)
    dmod_chip = lax.dynamic_slice_in_dim(dmod_all.reshape(N_DEV * bl, 2, 3 * d), chip * ncol, ncol, axis=2)
    g_ada_w, g_ada_b = _ada_bwd(c_all, dmod_chip.transpose(1, 0, 2), dmod_all, "ada_bwd")
    g_ada_b = g_ada_b.reshape(2, 3 * d)
    g_small, off = [], 0
    for cnt in small_counts:
        g_small.append(small_all[off:off + cnt])
        off += cnt
    g_w_a, g_w_x = g_small[0].reshape(lru_w_a.shape[1:]), g_small[1].reshape(lru_w_x.shape[1:])
    g_norm_g = jnp.concatenate(g_small[2:4], axis=0)
    g_fin, g_cw_r, g_cb_r, g_ba_r, g_bx_r, g_lam_r = g_small[4:10]
    tail = g_small[10][0]
    g_rel_bias = tail[:n_relb].reshape(REL_BUCKETS, A_Q_HEADS)
    g_sinks, g_b_f = tail[n_relb:n_relb + A_Q_HEADS], tail[n_relb + A_Q_HEADS:n_relb + A_Q_HEADS + B_HEADS]
    cw4 = lw // N_CHIP

    def my_cols(a):
        return lax.dynamic_slice_in_dim(a, chip * cw4, cw4, axis=1)

    grads = {
        "rel_bias": g_rel_bias, "norm_g": g_norm_g, "ada_w": g_ada_w, "ada_b": g_ada_b,
        "attn_w_in": g_big[0][None], "attn_sinks": g_sinks[None], "attn_b_f": g_b_f[None],
        "attn_w_out": g_big[1][None], "lru_w_in": g_big[2][None], "lru_conv_w": my_cols(g_cw_r)[None],
        "lru_conv_b": my_cols(g_cb_r), "lru_w_a": g_w_a[None], "lru_b_a": my_cols(g_ba_r),
        "lru_w_x": g_w_x[None], "lru_b_x": my_cols(g_bx_r), "lru_lambda": my_cols(g_lam_r),
        "lru_w_out": g_big[3][None], "final_g": g_fin.reshape(d),
    }
    weights = dict(rel_bias=rel_bias, norm_g=norm_g, ada_w=ada_w, ada_b=ada_b, attn_w_in=attn_w_in,
                   attn_sinks=attn_sinks, attn_b_f=attn_b_f, attn_w_out=attn_w_out, lru_w_in=lru_w_in,
                   lru_conv_w=lru_conv_w, lru_conv_b=lru_conv_b, lru_w_a=lru_w_a, lru_b_a=lru_b_a,
                   lru_w_x=lru_w_x, lru_b_x=lru_b_x, lru_lambda=lru_lambda, lru_w_out=lru_w_out, final_g=final_g)
    moms = dict(rel_bias=(m_rel_bias, v_rel_bias), norm_g=(m_norm_g, v_norm_g), ada_w=(m_ada_w, v_ada_w),
                ada_b=(m_ada_b, v_ada_b), attn_w_in=(m_attn_w_in, v_attn_w_in),
                attn_sinks=(m_attn_sinks, v_attn_sinks), attn_b_f=(m_attn_b_f, v_attn_b_f),
                attn_w_out=(m_attn_w_out, v_attn_w_out), lru_w_in=(m_lru_w_in, v_lru_w_in),
                lru_conv_w=(m_lru_conv_w, v_lru_conv_w), lru_conv_b=(m_lru_conv_b, v_lru_conv_b),
                lru_w_a=(m_lru_w_a, v_lru_w_a), lru_b_a=(m_lru_b_a, v_lru_b_a), lru_w_x=(m_lru_w_x, v_lru_w_x),
                lru_b_x=(m_lru_b_x, v_lru_b_x), lru_lambda=(m_lru_lambda, v_lru_lambda),
                lru_w_out=(m_lru_w_out, v_lru_w_out), final_g=(m_final_g, v_final_g))
    names = list(weights)
    big_names = [n for n in names if weights[n].size >= 65536]
    small_names = [n for n in names if weights[n].size < 65536]
    delta, new_m, new_v = {}, {}, {}
    for n in big_names:
        delta[n], new_m[n], new_v[n] = _adamw(weights[n], grads[n].reshape(weights[n].shape),
                                              moms[n][0], moms[n][1], "adamw_" + n)
    cat = lambda arrs: jnp.concatenate([a.reshape(-1) for a in arrs])
    sd, sm, sv = _adamw(cat([weights[n] for n in small_names]), cat([grads[n] for n in small_names]),
                        cat([moms[n][0] for n in small_names]), cat([moms[n][1] for n in small_names]),
                        "adamw_small")
    off = 0
    for n in small_names:
        sz = weights[n].size
        shp = weights[n].shape
        delta[n], new_m[n], new_v[n] = (sd[off:off + sz].reshape(shp), sm[off:off + sz].reshape(shp),
                                        sv[off:off + sz].reshape(shp))
        off += sz
    out_grads = [grads[n].reshape(weights[n].shape) for n in names]
    return (loss, dx0, *out_grads, *[delta[n] for n in names], *[new_m[n] for n in names],
            *[new_v[n] for n in names])
```

```python
import math

import numpy as np
import jax
import jax.numpy as jnp
from jax import lax
from jax.experimental import pallas as pl
from jax.experimental.pallas import tpu as pltpu

F32 = jnp.float32
BF16 = jnp.bfloat16
MESH = pl.DeviceIdType.MESH

N_DEV = 8
N_CHIP = 4
HEAD_DIM = 64
BLK = 128
A_Q_HEADS = 8
A_KV_HEADS = 2
A_GROUP = A_Q_HEADS // A_KV_HEADS
B_HEADS = 8
REL_BUCKETS = 32
REL_MAX_EXACT = 16
REL_MAX_DIST = 128
LRU_C = 8.0
CONV_WIDTH = 4
EPS = 1e-6
NEG = -1e30
SCALE = HEAD_DIM ** -0.5
LANE = 128
SUBLANE = 8
VMEM_LIMIT = 56 * 1024 * 1024
SCAN_CHUNK = 512
ROW_TILE = 1024
SWA_BPS = 1
FOX_BQ = 512
ADAM_LR = 0.001
ADAM_B1 = 0.9
ADAM_B2 = 0.999
ADAM_EPS = 1e-08
ADAM_WD = 0.01
ADAM_STEP = 10
HI = lax.Precision.HIGHEST


def _cp(sem=None):
    return pltpu.CompilerParams(dimension_semantics=sem, vmem_limit_bytes=VMEM_LIMIT)


def _dot(a, b):
    return jnp.dot(a, b, preferred_element_type=F32)


def _dot_nt(a, b):
    return lax.dot_general(a, b, (((1,), (1,)), ((), ())), preferred_element_type=F32)


def _dot_tn(a, b):
    return lax.dot_general(a, b, (((0,), (0,)), ((), ())), preferred_element_type=F32)


def _sigmoid(z):
    return 1.0 / (1.0 + jnp.exp(-z))


def _row_tile(rows, cap):
    if rows <= cap:
        return rows
    best = SUBLANE
    t = SUBLANE
    while t <= cap:
        if rows % t == 0:
            best = t
        t += SUBLANE
    return best


def _all_gather8(x_shard, name, space):
    m_per, n = x_shard.shape
    n_own = 8 if (space == pltpu.HBM and m_per % 128 == 0) else 1
    own_rows = m_per // n_own

    def body(x_ref, out_ref, send_sems, recv_sems, local_sems):
        x, y, c = lax.axis_index("x"), lax.axis_index("y"), lax.axis_index("c")
        me, sibling = (x, y, c), (x, y, 1 - c)
        chips = [(1 - x, y), (x, 1 - y), (1 - x, 1 - y)]

        def rows(px, py, pc):
            return out_ref.at[pl.ds((4 * px + 2 * py + pc) * m_per, m_per), :]

        def copy(k, block, to, src=None):
            return pltpu.make_async_remote_copy(
                src_ref=rows(*block) if src is None else src, dst_ref=rows(*block),
                send_sem=send_sems.at[k], recv_sem=recv_sems.at[k], device_id=to, device_id_type=MESH)

        base = (4 * x + 2 * y + c) * m_per
        mine = [pltpu.make_async_copy(x_ref.at[pl.ds(i * own_rows, own_rows), :],
                                      out_ref.at[pl.ds(base + i * own_rows, own_rows), :], local_sems.at[i])
                for i in range(n_own)]
        for cp in mine:
            cp.start()
        first = [copy(0, me, sibling, src=x_ref)]
        first += [copy(1 + j, me, (*chip, c), src=x_ref) for j, chip in enumerate(chips)]
        for cp in first:
            cp.start()
        passed = [copy(4 + j, (*chip, c), sibling) for j, chip in enumerate(chips)]
        for j, chip in enumerate(chips):
            copy(1 + j, (*chip, c), me).wait_recv()
            passed[j].start()
        copy(0, sibling, me).wait_recv()
        for j, chip in enumerate(chips):
            copy(4 + j, (*chip, 1 - c), me).wait_recv()
        for cp in first + passed:
            cp.wait_send()
        for cp in mine:
            cp.wait()

    return pl.pallas_call(
        body, name=name,
        out_shape=jax.ShapeDtypeStruct((N_DEV * m_per, n), x_shard.dtype),
        in_specs=[pl.BlockSpec(memory_space=space)],
        out_specs=pl.BlockSpec(memory_space=space),
        scratch_shapes=[pltpu.SemaphoreType.DMA((7,)), pltpu.SemaphoreType.DMA((7,)),
                        pltpu.SemaphoreType.DMA((n_own,))],
        compiler_params=pltpu.CompilerParams(vmem_limit_bytes=VMEM_LIMIT),
    )(x_shard)


def _sibling_push(blocks, pick_other, name):
    nblk = blocks.shape[0]
    m, n = blocks.shape[-2:]

    def body(x_ref, out_ref, send_sems, recv_sems):
        x, y, c = lax.axis_index("x"), lax.axis_index("y"), lax.axis_index("c")
        copies = []
        for k in range(nblk):
            src = x_ref.at[k, 1 - c] if pick_other else x_ref.at[k]
            copies.append(pltpu.make_async_remote_copy(
                src_ref=src, dst_ref=out_ref.at[k], send_sem=send_sems.at[k], recv_sem=recv_sems.at[k],
                device_id=(x, y, 1 - c), device_id_type=MESH))
        for cp in copies:
            cp.start()
        for cp in copies:
            cp.wait_recv()
        for cp in copies:
            cp.wait_send()

    hbm = pl.BlockSpec(memory_space=pltpu.HBM)
    return pl.pallas_call(
        body, name=name,
        out_shape=jax.ShapeDtypeStruct((nblk, m, n), blocks.dtype),
        in_specs=[hbm], out_specs=hbm,
        scratch_shapes=[pltpu.SemaphoreType.DMA((nblk,)), pltpu.SemaphoreType.DMA((nblk,))],
    )(blocks)


def _chip_all_to_all(parts, name):
    _, m, n = parts.shape

    def body(x_ref, out_ref, send_sems, recv_sems, local_sem):
        x, y, c = lax.axis_index("x"), lax.axis_index("y"), lax.axis_index("c")
        me = 2 * x + y
        mine = pltpu.make_async_copy(x_ref.at[me], out_ref.at[me], local_sem)
        mine.start()
        copies = []
        for k in range(1, N_CHIP):
            px, py = x ^ ((k >> 1) & 1), y ^ (k & 1)
            copies.append(pltpu.make_async_remote_copy(
                src_ref=x_ref.at[2 * px + py], dst_ref=out_ref.at[me],
                send_sem=send_sems.at[k - 1], recv_sem=recv_sems.at[k - 1],
                device_id=(px, py, c), device_id_type=MESH))
        for cp in copies:
            cp.start()
        for cp in copies:
            cp.wait_recv()
        for cp in copies:
            cp.wait_send()
        mine.wait()

    hbm = pl.BlockSpec(memory_space=pltpu.HBM)
    return pl.pallas_call(
        body, name=name,
        out_shape=jax.ShapeDtypeStruct(parts.shape, parts.dtype),
        in_specs=[hbm], out_specs=hbm,
        scratch_shapes=[pltpu.SemaphoreType.DMA((N_CHIP - 1,)), pltpu.SemaphoreType.DMA((N_CHIP - 1,)),
                        pltpu.SemaphoreType.DMA],
    )(parts)


def _pair_sum(core, pieces, theirs, name):
    nblk, _, m, n = pieces.shape
    tr = _row_tile(m, 536)

    def body(c_ref, p_ref, t_ref, o_ref):
        o_ref[...] = (p_ref[...] + t_ref[...]).astype(BF16)

    return pl.pallas_call(
        body, name=name,
        grid_spec=pltpu.PrefetchScalarGridSpec(
            num_scalar_prefetch=1, grid=(nblk, m // tr),
            in_specs=[pl.BlockSpec((None, None, tr, n), lambda k, i, c_ref: (k, c_ref[0], i, 0)),
                      pl.BlockSpec((None, tr, n), lambda k, i, c_ref: (k, i, 0))],
            out_specs=pl.BlockSpec((None, tr, n), lambda k, i, c_ref: (k, i, 0))),
        out_shape=jax.ShapeDtypeStruct((nblk, m, n), BF16),
        compiler_params=_cp(("parallel", "parallel")),
    )(core, pieces, theirs)


def _sum_slots(slots, name):
    k, m, n = slots.shape
    tr = _row_tile(m, 536)

    def body(s_ref, o_ref):
        acc = s_ref[0].astype(F32)
        for j in range(1, k):
            acc = acc + s_ref[j].astype(F32)
        o_ref[...] = acc

    return pl.pallas_call(
        body, name=name, grid=(m // tr,),
        out_shape=jax.ShapeDtypeStruct((m, n), F32),
        in_specs=[pl.BlockSpec((k, tr, n), lambda i: (0, i, 0))],
        out_specs=pl.BlockSpec((tr, n), lambda i: (i, 0)),
        compiler_params=_cp(("parallel",)),
    )(slots)


def _ada_fwd(c_all, w, b, name):
    r, _ = c_all.shape
    n = w.shape[1]

    def body(c_ref, w_ref, b_ref, o_ref):
        cv = c_ref[...]
        act = cv * _sigmoid(cv)
        o_ref[...] = jnp.dot(act, w_ref[...], precision=HI, preferred_element_type=F32) + b_ref[...]

    return pl.pallas_call(body, name=name, out_shape=jax.ShapeDtypeStruct((r, n), F32),
                          compiler_params=_cp())(c_all, w, b)


def _ada_bwd(c_all, dmod_chip, dmod_all, name):
    r, d = c_all.shape
    nl, _, n = dmod_chip.shape

    def body(c_ref, dm_ref, da_ref, gw_ref, gb_ref):
        cv = c_ref[...]
        act = cv * _sigmoid(cv)
        for l in range(nl):
            gw_ref[l] = lax.dot_general(act, dm_ref[l], (((0,), (0,)), ((), ())), precision=HI,
                                        preferred_element_type=F32)
        gb_ref[...] = jnp.sum(da_ref[...], axis=0, keepdims=True)

    return pl.pallas_call(
        body, name=name,
        out_shape=(jax.ShapeDtypeStruct((nl, d, n), F32), jax.ShapeDtypeStruct((1, dmod_all.shape[1]), F32)),
        compiler_params=_cp())(c_all, dmod_chip, dmod_all)


def _norm_proj(x, g, scale, shift, w, f32_cols, name):
    b, s, d = x.shape
    n = w.shape[1]
    tm = min(s, ROW_TILE)

    def body(x_ref, g_ref, sc_ref, sh_ref, w_ref, proj_ref, h_ref, *aux_ref):
        xv = x_ref[...]
        rstd = lax.rsqrt(jnp.mean(xv * xv, axis=-1, keepdims=True) + EPS)
        h = (xv * rstd) * g_ref[...] * (1.0 + sc_ref[...]) + sh_ref[...]
        hb = h.astype(BF16)
        h_ref[...] = hb
        proj = _dot(hb, w_ref[...])
        proj_ref[...] = proj.astype(BF16)
        if f32_cols:
            aux_ref[0][...] = proj[:, n - f32_cols:]

    row = lambda i, j: (i, j, 0)
    out_shape = [jax.ShapeDtypeStruct((b, s, n), BF16), jax.ShapeDtypeStruct((b, s, d), BF16)]
    out_specs = [pl.BlockSpec((None, tm, n), row), pl.BlockSpec((None, tm, d), row)]
    if f32_cols:
        out_shape.append(jax.ShapeDtypeStruct((b, s, f32_cols), F32))
        out_specs.append(pl.BlockSpec((None, tm, f32_cols), row))
    return pl.pallas_call(
        body, name=name, grid=(b, s // tm),
        out_shape=tuple(out_shape),
        in_specs=[pl.BlockSpec((None, tm, d), row),
                  pl.BlockSpec((1, d), lambda i, j: (0, 0)),
                  pl.BlockSpec((None, 1, d), lambda i, j: (i, 0, 0)),
                  pl.BlockSpec((None, 1, d), lambda i, j: (i, 0, 0)),
                  pl.BlockSpec((d, n), lambda i, j: (0, 0))],
        out_specs=tuple(out_specs),
        compiler_params=_cp(("parallel", "parallel")),
    )(x, g, scale, shift, w)


def _cat_refs(refs):
    vals = [r[...] for r in refs]
    return vals[0] if len(vals) == 1 else jnp.concatenate(vals, axis=-1)


def _gate_outproj(mix_parts, proj, gate_blk, w_out, x, gmod, name):
    b, s, _ = x.shape
    wd, d = w_out.shape
    tm = min(s, ROW_TILE)
    npart = len(mix_parts)

    def body(*refs):
        mix_refs = refs[:npart]
        gate_ref, w_ref, x_ref, gm_ref, xo_ref, o_ref = refs[npart:]
        gt = gate_ref[...].astype(F32)
        y = (_cat_refs(mix_refs) * (gt * _sigmoid(gt))).astype(BF16)
        o = _dot(y, w_ref[...])
        o_ref[...] = o.astype(BF16)
        xo_ref[...] = x_ref[...] + gm_ref[...] * o

    return pl.pallas_call(
        body, name=name, grid=(b, s // tm),
        out_shape=(jax.ShapeDtypeStruct((b, s, d), F32), jax.ShapeDtypeStruct((b, s, d), BF16)),
        in_specs=[pl.BlockSpec((None, tm, p.shape[2]), lambda i, j: (i, j, 0)) for p in mix_parts] + [
                  pl.BlockSpec((None, tm, wd), lambda i, j: (i, j, gate_blk)),
                  pl.BlockSpec((wd, d), lambda i, j: (0, 0)),
                  pl.BlockSpec((None, tm, d), lambda i, j: (i, j, 0)),
                  pl.BlockSpec((None, 1, d), lambda i, j: (i, 0, 0))],
        out_specs=(pl.BlockSpec((None, tm, d), lambda i, j: (i, j, 0)),
                   pl.BlockSpec((None, tm, d), lambda i, j: (i, j, 0))),
        compiler_params=_cp(("parallel", "parallel")),
    )(*mix_parts, proj, w_out, x, gmod)


def _final_loss(x, g, target, name):
    b, s, d = x.shape
    tm = min(s, ROW_TILE)

    def body(x_ref, g_ref, t_ref, loss_ref, dx_ref, dg_ref):
        first = jnp.logical_and(pl.program_id(0) == 0, pl.program_id(1) == 0)

        @pl.when(first)
        def _():
            loss_ref[...] = jnp.zeros_like(loss_ref)
            dg_ref[...] = jnp.zeros_like(dg_ref)

        xv = x_ref[...]
        gv = g_ref[...]
        rstd = lax.rsqrt(jnp.mean(xv * xv, axis=-1, keepdims=True) + EPS)
        xhat = xv * rstd
        err = xhat * gv - t_ref[...]
        row = jnp.mean(err * err, axis=-1, keepdims=True)
        loss_ref[...] += 0.5 * jnp.sum(row, axis=0, keepdims=True)
        dy = err * (1.0 / d)
        dg_ref[...] += jnp.sum(dy * xhat, axis=0, keepdims=True)
        dxh = dy * gv
        dx_ref[...] = rstd * (dxh - xhat * jnp.mean(dxh * xhat, axis=-1, keepdims=True))

    return pl.pallas_call(
        body, name=name, grid=(b, s // tm),
        out_shape=(jax.ShapeDtypeStruct((1, LANE), F32), jax.ShapeDtypeStruct((b, s, d), F32),
                   jax.ShapeDtypeStruct((1, d), F32)),
        in_specs=[pl.BlockSpec((None, tm, d), lambda i, j: (i, j, 0)),
                  pl.BlockSpec((1, d), lambda i, j: (0, 0)),
                  pl.BlockSpec((None, tm, d), lambda i, j: (i, j, 0))],
        out_specs=(pl.BlockSpec((1, LANE), lambda i, j: (0, 0)),
                   pl.BlockSpec((None, tm, d), lambda i, j: (i, j, 0)),
                   pl.BlockSpec((1, d), lambda i, j: (0, 0))),
        compiler_params=_cp(("arbitrary", "arbitrary")),
    )(x, g, target)


def _bwd_out(dxo, gmod, o, mix_parts, proj, gate_blk, w_out, dmix_dtype, name):
    b, s, d = dxo.shape
    wd = w_out.shape[0]
    tm = min(s, ROW_TILE)
    npart = len(mix_parts)

    def body(dx_ref, gm_ref, o_ref, *refs):
        mix_refs = refs[:npart]
        gate_ref, wt_ref, dmix_ref, dgate_ref, do_ref, y_ref, dgm_ref = refs[npart:]

        @pl.when(pl.program_id(1) == 0)
        def _():
            dgm_ref[...] = jnp.zeros_like(dgm_ref)

        dx = dx_ref[...]
        dgm_ref[...] += jnp.sum(dx * o_ref[...].astype(F32), axis=0, keepdims=True)
        dob = (gm_ref[...] * dx).astype(BF16)
        do_ref[...] = dob
        dy = _dot_nt(dob, wt_ref[...])
        gt = gate_ref[...].astype(F32)
        sg = _sigmoid(gt)
        silu = gt * sg
        mx = _cat_refs(mix_refs)
        y_ref[...] = (mx * silu).astype(BF16)
        dmix_ref[...] = (dy * silu).astype(dmix_dtype)
        dgate_ref[...] = (dy * mx * (sg * (1.0 + gt * (1.0 - sg)))).astype(BF16)

    row = lambda i, j: (i, j, 0)
    return pl.pallas_call(
        body, name=name, grid=(b, s // tm),
        out_shape=(jax.ShapeDtypeStruct((b, s, wd), dmix_dtype), jax.ShapeDtypeStruct((b, s, wd), BF16),
                   jax.ShapeDtypeStruct((b, s, d), BF16), jax.ShapeDtypeStruct((b, s, wd), BF16),
                   jax.ShapeDtypeStruct((b, 1, d), F32)),
        in_specs=[pl.BlockSpec((None, tm, d), row),
                  pl.BlockSpec((None, 1, d), lambda i, j: (i, 0, 0)),
                  pl.BlockSpec((None, tm, d), row)] + [
                  pl.BlockSpec((None, tm, p.shape[2]), row) for p in mix_parts] + [
                  pl.BlockSpec((None, tm, wd), lambda i, j: (i, j, gate_blk)),
                  pl.BlockSpec((wd, d), lambda i, j: (0, 0))],
        out_specs=(pl.BlockSpec((None, tm, wd), row), pl.BlockSpec((None, tm, wd), row),
                   pl.BlockSpec((None, tm, d), row), pl.BlockSpec((None, tm, wd), row),
                   pl.BlockSpec((None, 1, d), lambda i, j: (i, 0, 0))),
        compiler_params=_cp(("parallel", "arbitrary")),
    )(dxo, gmod, o, *mix_parts, proj, w_out)


def _bwd_in(dproj_parts, w_in, x, g, scale, dxo, name):
    b, s, d = x.shape
    n = w_in.shape[1]
    tm = min(s, ROW_TILE)
    npart = len(dproj_parts)

    def body(*refs):
        dp_refs = refs[:npart]
        wt_ref, x_ref, g_ref, sc_ref, dxo_ref, dx_ref, dsh_ref, dsc_ref, dg_ref = refs[npart:]

        @pl.when(jnp.logical_and(pl.program_id(0) == 0, pl.program_id(1) == 0))
        def _():
            dg_ref[...] = jnp.zeros_like(dg_ref)

        @pl.when(pl.program_id(1) == 0)
        def _():
            dsh_ref[...] = jnp.zeros_like(dsh_ref)
            dsc_ref[...] = jnp.zeros_like(dsc_ref)

        dh = _dot_nt(_cat_refs(dp_refs), wt_ref[...])
        xv = x_ref[...]
        gv = g_ref[...]
        one_sc = 1.0 + sc_ref[...]
        rstd = lax.rsqrt(jnp.mean(xv * xv, axis=-1, keepdims=True) + EPS)
        xhat = xv * rstd
        dsh_ref[...] += jnp.sum(dh, axis=0, keepdims=True)
        dsc_ref[...] += jnp.sum(dh * (xhat * gv), axis=0, keepdims=True)
        dhs = dh * one_sc
        dg_ref[...] += jnp.sum(dhs * xhat, axis=0, keepdims=True)
        dxh = dhs * gv
        dx_ref[...] = dxo_ref[...] + rstd * (dxh - xhat * jnp.mean(dxh * xhat, axis=-1, keepdims=True))

    row = lambda i, j: (i, j, 0)
    per_b = lambda i, j: (i, 0, 0)
    return pl.pallas_call(
        body, name=name, grid=(b, s // tm),
        out_shape=(jax.ShapeDtypeStruct((b, s, d), F32), jax.ShapeDtypeStruct((b, 1, d), F32),
                   jax.ShapeDtypeStruct((b, 1, d), F32), jax.ShapeDtypeStruct((1, d), F32)),
        in_specs=[pl.BlockSpec((None, tm, p.shape[2]), row) for p in dproj_parts] + [
                  pl.BlockSpec((d, n), lambda i, j: (0, 0)),
                  pl.BlockSpec((None, tm, d), row),
                  pl.BlockSpec((1, d), lambda i, j: (0, 0)),
                  pl.BlockSpec((None, 1, d), per_b),
                  pl.BlockSpec((None, tm, d), row)],
        out_specs=(pl.BlockSpec((None, tm, d), row), pl.BlockSpec((None, 1, d), per_b),
                   pl.BlockSpec((None, 1, d), per_b), pl.BlockSpec((1, d), lambda i, j: (0, 0))),
        compiler_params=_cp(("arbitrary", "arbitrary")),
    )(*dproj_parts, w_in, x, g, scale, dxo)


def _matmul_tn(a, b_parts, name):
    bsz, s, m = a.shape
    n = sum(p.shape[2] for p in b_parts)
    tk = next(c for c in (ROW_TILE, 512, 256, 128) if s % c == 0)
    npart = len(b_parts)

    def body(a_ref, *refs):
        b_refs, o_ref = refs[:npart], refs[npart]

        @pl.when(jnp.logical_and(pl.program_id(0) == 0, pl.program_id(1) == 0))
        def _():
            o_ref[...] = jnp.zeros_like(o_ref)

        o_ref[...] += _dot_tn(a_ref[...], _cat_refs(b_refs))

    row = lambda i, k: (i, k, 0)
    return pl.pallas_call(
        body, name=name, grid=(bsz, s // tk),
        out_shape=jax.ShapeDtypeStruct((m, n), F32),
        in_specs=[pl.BlockSpec((None, tk, m), row)] + [pl.BlockSpec((None, tk, p.shape[2]), row) for p in b_parts],
        out_specs=pl.BlockSpec((m, n), lambda i, k: (0, 0)),
        compiler_params=_cp(("arbitrary", "arbitrary")),
    )(a, *b_parts)


def _rel_buckets():
    qi = np.arange(BLK)[:, None]
    kj = np.arange(2 * BLK)[None, :]
    rel = qi - kj + BLK
    n = np.maximum(rel, 0)
    nf = np.maximum(n, 1).astype(np.float32)
    large = REL_MAX_EXACT + (np.log(nf / REL_MAX_EXACT) / math.log(REL_MAX_DIST / REL_MAX_EXACT)
                             * (REL_BUCKETS - REL_MAX_EXACT)).astype(np.int32)
    large = np.minimum(large, REL_BUCKETS - 1)
    bucket = np.where(n < REL_MAX_EXACT, n, large).astype(np.int32)
    valid = ((rel >= 0) & (rel < BLK)).astype(np.int32)
    return bucket, valid


def _swa_bias(rel_bias_t, bucket, valid, name):
    nh = rel_bias_t.shape[0]

    def body(rb_ref, bk_ref, vl_ref, o_ref):
        h = pl.program_id(0)
        bk = bk_ref[...]
        acc = jnp.zeros(bk.shape, F32)
        for i in range(REL_BUCKETS):
            acc = jnp.where(bk == i, rb_ref[h, i], acc)
        o_ref[...] = jnp.where(vl_ref[...] > 0, acc, NEG)

    return pl.pallas_call(
        body, name=name, grid=(nh,),
        out_shape=jax.ShapeDtypeStruct((nh, BLK, 2 * BLK), F32),
        in_specs=[pl.BlockSpec(memory_space=pltpu.SMEM),
                  pl.BlockSpec((BLK, 2 * BLK), lambda h: (0, 0)),
                  pl.BlockSpec((BLK, 2 * BLK), lambda h: (0, 0))],
        out_specs=pl.BlockSpec((None, BLK, 2 * BLK), lambda h: (h, 0, 0)),
        compiler_params=_cp(("arbitrary",)),
    )(rel_bias_t, bucket, valid)


def _swa_scores(n, q, kw, bias_ref):
    sc = _dot_nt(q, kw) * SCALE + bias_ref[...]
    second = lax.broadcasted_iota(jnp.int32, sc.shape, 1) >= BLK
    return jnp.where(jnp.logical_or(n > 0, second), sc, NEG)


def _pad_front(dst_ref, src_ref):
    dst_ref[0:BLK, :] = jnp.zeros((BLK, dst_ref.shape[1]), dst_ref.dtype)
    dst_ref[BLK:, :] = src_ref[...]


def _swa_fwd(q, k, v, bias, sinks, name):
    b, hkv, nb, rows, hd = q.shape
    wide = bias.shape[2]
    stride = wide - BLK
    s = nb * stride

    def body(q_ref, k_ref, v_ref, bias_ref, sink_ref, o_ref, l_ref, kpad_ref, vpad_ref):
        _pad_front(kpad_ref, k_ref)
        _pad_front(vpad_ref, v_ref)
        sink = sink_ref[...]

        def step(n, carry):
            w0 = pl.multiple_of(n * stride, BLK)
            sc = _swa_scores(n, q_ref[n], kpad_ref[pl.ds(w0, wide), :], bias_ref)
            m = jnp.maximum(jnp.max(sc, axis=1, keepdims=True), sink)
            e = jnp.exp(sc - m)
            den = jnp.sum(e, axis=1, keepdims=True) + jnp.exp(sink - m)
            o_ref[n] = _dot((e * (1.0 / den)).astype(BF16), vpad_ref[pl.ds(w0, wide), :]).astype(BF16)
            l_ref[n] = m + jnp.log(den)
            return carry

        lax.fori_loop(0, nb, step, 0)

    qspec = pl.BlockSpec((None, None, nb, rows, hd), lambda i, kv: (i, kv, 0, 0, 0))
    kspec = pl.BlockSpec((None, None, s, hd), lambda i, kv: (i, kv, 0, 0))
    return pl.pallas_call(
        body, name=name, grid=(b, hkv),
        out_shape=(jax.ShapeDtypeStruct((b, hkv, nb, rows, hd), BF16), jax.ShapeDtypeStruct((b, hkv, nb, rows, 1), F32)),
        in_specs=[qspec, kspec, kspec,
                  pl.BlockSpec((None, rows, wide), lambda i, kv: (kv, 0, 0)),
                  pl.BlockSpec((None, rows, 1), lambda i, kv: (kv, 0, 0))],
        out_specs=(qspec, pl.BlockSpec((None, None, nb, rows, 1), lambda i, kv: (i, kv, 0, 0, 0))),
        scratch_shapes=[pltpu.VMEM((s + BLK, hd), BF16), pltpu.VMEM((s + BLK, hd), BF16)],
        compiler_params=_cp(("parallel", "parallel")),
    )(q, k, v, bias, sinks)


def _swa_bwd(q, k, v, bias, sinks, do, lse, name):
    b, hkv, nb, rows, hd = q.shape
    wide = bias.shape[2]
    stride = wide - BLK
    s = nb * stride

    def body(q_ref, k_ref, v_ref, bias_ref, sink_ref, do_ref, l_ref,
             dq_ref, dk_ref, dv_ref, db_ref, dsk_ref, kpad_ref, vpad_ref, dkpad_ref, dvpad_ref):
        _pad_front(kpad_ref, k_ref)
        _pad_front(vpad_ref, v_ref)
        dkpad_ref[...] = jnp.zeros_like(dkpad_ref)
        dvpad_ref[...] = jnp.zeros_like(dvpad_ref)
        db_ref[...] = jnp.zeros_like(db_ref)
        sink = sink_ref[...]

        def step(n, dsink):
            w0 = pl.multiple_of(n * stride, BLK)
            win = pl.ds(w0, wide)
            qn = q_ref[n]
            kw = kpad_ref[win, :]
            ln = l_ref[n]
            p = jnp.exp(_swa_scores(n, qn, kw, bias_ref) - ln)
            dob = do_ref[n]
            dp = _dot_nt(dob, vpad_ref[win, :])
            delta = jnp.sum(p * dp, axis=1, keepdims=True)
            ds = p * (dp - delta)
            db_ref[...] += ds
            dsb = ds.astype(BF16)
            dq_ref[n] = (_dot(dsb, kw) * SCALE).astype(BF16)
            dkpad_ref[win, :] += _dot_tn(dsb, qn)
            dvpad_ref[win, :] += _dot_tn(p.astype(BF16), dob)
            return dsink - jnp.exp(sink - ln) * delta

        dsink = lax.fori_loop(0, nb, step, jnp.zeros((rows, 1), F32))
        for g in range(A_GROUP):
            tot = jnp.zeros((1, 1), F32)
            for blk in range(rows // (A_GROUP * BLK)):
                r0 = (blk * A_GROUP + g) * BLK
                tot = tot + jnp.sum(dsink[r0:r0 + BLK, :], axis=0, keepdims=True)
            dsk_ref[g] = jnp.broadcast_to(tot, (1, LANE))
        dk_ref[...] = (dkpad_ref[BLK:, :] * SCALE).astype(BF16)
        dv_ref[...] = dvpad_ref[BLK:, :].astype(BF16)

    qspec = pl.BlockSpec((None, None, nb, rows, hd), lambda i, kv: (i, kv, 0, 0, 0))
    kspec = pl.BlockSpec((None, None, s, hd), lambda i, kv: (i, kv, 0, 0))
    return pl.pallas_call(
        body, name=name, grid=(b, hkv),
        out_shape=(jax.ShapeDtypeStruct((b, hkv, nb, rows, hd), BF16), jax.ShapeDtypeStruct((b, hkv, s, hd), BF16),
                   jax.ShapeDtypeStruct((b, hkv, s, hd), BF16), jax.ShapeDtypeStruct((b, hkv, rows, wide), F32),
                   jax.ShapeDtypeStruct((b, hkv, A_GROUP, 1, LANE), F32)),
        in_specs=[qspec, kspec, kspec,
                  pl.BlockSpec((None, rows, wide), lambda i, kv: (kv, 0, 0)),
                  pl.BlockSpec((None, rows, 1), lambda i, kv: (kv, 0, 0)),
                  qspec,
                  pl.BlockSpec((None, None, nb, rows, 1), lambda i, kv: (i, kv, 0, 0, 0))],
        out_specs=(qspec, kspec, kspec,
                   pl.BlockSpec((None, None, rows, wide), lambda i, kv: (i, kv, 0, 0)),
                   pl.BlockSpec((None, None, A_GROUP, 1, LANE), lambda i, kv: (i, kv, 0, 0, 0))),
        scratch_shapes=[pltpu.VMEM((s + BLK, hd), BF16), pltpu.VMEM((s + BLK, hd), BF16),
                        pltpu.VMEM((s + BLK, hd), F32), pltpu.VMEM((s + BLK, hd), F32)],
        compiler_params=_cp(("parallel", "parallel")),
    )(q, k, v, bias, sinks, do, lse)


def _swa_small_grads(db, dsk, bucket, name):
    b, nh = db.shape[0], db.shape[1]

    def body(db_ref, dsk_ref, bk_ref, gb_ref, gs_ref):
        acc = db_ref[0]
        sk = dsk_ref[0]
        for i in range(1, b):
            acc = acc + db_ref[i]
            sk = sk + dsk_ref[i]
        gs_ref[...] = sk
        bk = bk_ref[...]
        for i in range(REL_BUCKETS):
            part = jnp.sum(jnp.where(bk == i, acc, 0.0), axis=1, keepdims=True)
            tot = jnp.sum(part, axis=0, keepdims=True)
            gb_ref[i:i + 1, :] = jnp.broadcast_to(tot, (1, LANE))

    return pl.pallas_call(
        body, name=name, grid=(nh,),
        out_shape=(jax.ShapeDtypeStruct((nh, REL_BUCKETS, LANE), F32), jax.ShapeDtypeStruct((nh, 1, LANE), F32)),
        in_specs=[pl.BlockSpec((b, None, BLK, 2 * BLK), lambda h: (0, h, 0, 0)),
                  pl.BlockSpec((b, None, 1, LANE), lambda h: (0, h, 0, 0)),
                  pl.BlockSpec((BLK, 2 * BLK), lambda h: (0, 0))],
        out_specs=(pl.BlockSpec((None, REL_BUCKETS, LANE), lambda h: (h, 0, 0)),
                   pl.BlockSpec((None, 1, LANE), lambda h: (h, 0, 0))),
        compiler_params=_cp(("parallel",)),
    )(db, dsk, bucket)


def _log_sigmoid(z):
    return jnp.minimum(z, 0.0) - jnp.log(1.0 + jnp.exp(-jnp.abs(z)))


def _fox_decay(z, bf, name):
    b, s, w = z.shape
    nb = s // BLK

    def body(z_ref, bf_ref, f_ref):
        r = lax.broadcasted_iota(jnp.int32, (BLK, BLK), 0)
        c = lax.broadcasted_iota(jnp.int32, (BLK, BLK), 1)
        tri = (c <= r).astype(F32)

        def step(n, carry):
            r0 = pl.multiple_of(n * BLK, BLK)
            lf = _log_sigmoid(z_ref[pl.ds(r0, BLK), :] + bf_ref[...])
            f_ref[pl.ds(r0, BLK), :] = jnp.dot(tri, lf, precision=HI, preferred_element_type=F32) + carry
            return carry + jnp.sum(lf, axis=0, keepdims=True)

        lax.fori_loop(0, nb, step, jnp.zeros((1, w), F32))

    spec = pl.BlockSpec((None, s, w), lambda i: (i, 0, 0))
    return pl.pallas_call(
        body, name=name, grid=(b,), out_shape=jax.ShapeDtypeStruct((b, s, w), F32),
        in_specs=[spec, pl.BlockSpec((1, w), lambda i: (0, 0))], out_specs=spec,
        compiler_params=_cp(("parallel",)),
    )(z, bf)


def _fox_dgate(df, z, bf, nheads, name):
    b, s, w = z.shape
    nb = s // BLK

    def body(df_ref, z_ref, bf_ref, dz_ref, dbf_ref):
        @pl.when(pl.program_id(0) == 0)
        def _():
            dbf_ref[...] = jnp.zeros_like(dbf_ref)

        r = lax.broadcasted_iota(jnp.int32, (BLK, BLK), 0)
        c = lax.broadcasted_iota(jnp.int32, (BLK, BLK), 1)
        tri = (c >= r).astype(F32)
        lane = lax.broadcasted_iota(jnp.int32, (BLK, w), 1)

        def step(i, carry):
            tail, dbf = carry
            r0 = pl.multiple_of((nb - 1 - i) * BLK, BLK)
            dfb = df_ref[pl.ds(r0, BLK), :]
            dlf = jnp.dot(tri, dfb, precision=HI, preferred_element_type=F32) + tail
            dz = jnp.where(lane < nheads, dlf * _sigmoid(-(z_ref[pl.ds(r0, BLK), :] + bf_ref[...])), 0.0)
            dz_ref[pl.ds(r0, BLK), :] = dz
            return tail + jnp.sum(dfb, axis=0, keepdims=True), dbf + jnp.sum(dz, axis=0, keepdims=True)

        zero = jnp.zeros((1, w), F32)
        _, dbf = lax.fori_loop(0, nb, step, (zero, zero))
        dbf_ref[...] += dbf

    spec = pl.BlockSpec((None, s, w), lambda i: (i, 0, 0))
    one = pl.BlockSpec((1, w), lambda i: (0, 0))
    return pl.pallas_call(
        body, name=name, grid=(b,),
        out_shape=(jax.ShapeDtypeStruct((b, s, w), F32), jax.ShapeDtypeStruct((1, w), F32)),
        in_specs=[spec, spec, one], out_specs=(spec, one),
        compiler_params=_cp(("arbitrary",)),
    )(df, z, bf)


def _fox_segments(nb):
    per = max(1, nb // 4)
    return per, nb // per


def _head_masks(shape, axis):
    idx = lax.broadcasted_iota(jnp.int32, shape, axis)
    return idx < HEAD_DIM, idx >= HEAD_DIM


def _fox_fwd(proj, qblk, kblk, vblk, fcol, frow, name):
    b, s, _ = proj.shape
    nh = frow.shape[1]
    npair = nh // 2
    BLK = min(s, FOX_BQ)
    assert s % BLK == 0
    per, nseg = _fox_segments(s // BLK)

    def body(q_ref, k_ref, v_ref, fc_ref, fr_ref, o_ref, l_ref, qm_ref, kt_ref, vb_ref):
        lo, hi = _head_masks((s, LANE), 1)
        qv = q_ref[...].astype(F32) * SCALE
        qm_ref[0] = jnp.where(lo, qv, 0.0).astype(BF16)
        qm_ref[1] = jnp.where(hi, qv, 0.0).astype(BF16)
        kt_ref[...] = k_ref[...].astype(F32).T.astype(BF16)
        vb_ref[...] = v_ref[...].astype(BF16)
        lane_lo = lax.broadcasted_iota(jnp.int32, (BLK, LANE), 1) < HEAD_DIM
        tail = per * BLK
        causal = (lax.broadcasted_iota(jnp.int32, (BLK, tail), 1)
                  - lax.broadcasted_iota(jnp.int32, (BLK, tail), 0))
        for seg in range(nseg):
            w = (seg + 1) * tail

            def qstep(n, carry):
                r0 = pl.multiple_of(n * BLK, BLK)
                outs = []
                for hh in range(2):
                    sc = _dot(qm_ref[hh, pl.ds(r0, BLK), :], kt_ref[:, :w])
                    sc = sc + (fc_ref[hh, pl.ds(r0, BLK), :] - fr_ref[hh, :, :w])
                    masked = jnp.where(causal <= (n - seg * per) * BLK, sc[:, w - tail:], NEG)
                    sc = masked if seg == 0 else jnp.concatenate([sc[:, :w - tail], masked], axis=1)
                    m = jnp.max(sc, axis=1, keepdims=True)
                    e = jnp.exp(sc - m)
                    l = jnp.sum(e, axis=1, keepdims=True)
                    outs.append(_dot((e * (1.0 / l)).astype(BF16), vb_ref[:w, :]))
                    l_ref[hh, pl.ds(r0, BLK), :] = m + jnp.log(l)
                o_ref[pl.ds(r0, BLK), :] = jnp.where(lane_lo, outs[0], outs[1]).astype(BF16)
                return carry

            lax.fori_loop(seg * per, (seg + 1) * per, qstep, 0)

    def tok(blk):
        return pl.BlockSpec((None, s, LANE), lambda i, p: (i, 0, blk + p))

    col = pl.BlockSpec((None, 2, s, 1), lambda i, p: (i, p, 0, 0))
    rowspec = pl.BlockSpec((None, 2, 1, s), lambda i, p: (i, p, 0, 0))
    return pl.pallas_call(
        body, name=name, grid=(b, npair),
        out_shape=(jax.ShapeDtypeStruct((b, s, nh * HEAD_DIM), BF16), jax.ShapeDtypeStruct((b, nh, s, 1), F32)),
        in_specs=[tok(qblk), tok(kblk), tok(vblk), col, rowspec],
        out_specs=(pl.BlockSpec((None, s, LANE), lambda i, p: (i, 0, p)), col),
        scratch_shapes=[pltpu.VMEM((2, s, LANE), BF16), pltpu.VMEM((LANE, s), BF16), pltpu.VMEM((s, LANE), BF16)],
        compiler_params=_cp(("parallel", "parallel")),
    )(proj, proj, proj, fcol, frow)


def _fox_bwd(proj, qblk, kblk, vblk, dmix, doblk, fcol, frow, frowb, lse, lserowb, name):
    b, s, _ = proj.shape
    nh = frow.shape[1]
    npair = nh // 2
    BLK = min(s, FOX_BQ)
    assert s % BLK == 0
    nb = s // BLK
    per, nseg = _fox_segments(nb)

    def body(q_ref, k_ref, v_ref, do_ref, fc_ref, fr_ref, frb_ref, l_ref, lrb_ref,
             dq_ref, dk_ref, dv_ref, dfr_ref,
             qm_ref, dom_ref, kb_ref, vb_ref, kt_ref, vt_ref, qtm_ref, dotm_ref, dka_ref, dva_ref):
        lo, hi = _head_masks((s, LANE), 1)
        qv = q_ref[...].astype(F32) * SCALE
        dov = do_ref[...]
        for hh, msk in enumerate((lo, hi)):
            qm_ref[hh] = jnp.where(msk, qv, 0.0).astype(BF16)
            dom_ref[hh] = jnp.where(msk, dov, 0.0).astype(BF16)
        kv = k_ref[...].astype(F32)
        vv = v_ref[...].astype(F32)
        kb_ref[...] = kv.astype(BF16)
        vb_ref[...] = vv.astype(BF16)
        kt_ref[...] = kv.T.astype(BF16)
        vt_ref[...] = vv.T.astype(BF16)
        rlo, rhi = _head_masks((LANE, BLK), 0)

        def tstep(n, carry):
            r0 = pl.multiple_of(n * BLK, BLK)
            qt = (q_ref[pl.ds(r0, BLK), :].astype(F32) * SCALE).T
            dt = do_ref[pl.ds(r0, BLK), :].astype(F32).T
            for hh, msk in enumerate((rlo, rhi)):
                qtm_ref[hh, n] = jnp.where(msk, qt, 0.0).astype(BF16)
                dotm_ref[hh, n] = jnp.where(msk, dt, 0.0).astype(BF16)
            return carry

        lax.fori_loop(0, nb, tstep, 0)
        dka_ref[...] = jnp.zeros_like(dka_ref)
        dva_ref[...] = jnp.zeros_like(dva_ref)
        dfr_ref[...] = jnp.zeros_like(dfr_ref)
        lane_lo = lax.broadcasted_iota(jnp.int32, (BLK, LANE), 1) < HEAD_DIM
        tail = per * BLK
        causal = (lax.broadcasted_iota(jnp.int32, (BLK, tail), 1)
                  - lax.broadcasted_iota(jnp.int32, (BLK, tail), 0))
        causal_t = (lax.broadcasted_iota(jnp.int32, (tail, BLK), 0)
                    - lax.broadcasted_iota(jnp.int32, (tail, BLK), 1))
        for seg in range(nseg):
            w = (seg + 1) * tail

            def nstep(n, carry):
                r0 = pl.multiple_of(n * BLK, BLK)
                lim = (n - seg * per) * BLK
                dqs = []
                for hh in range(2):
                    qn = qm_ref[hh, pl.ds(r0, BLK), :]
                    don = dom_ref[hh, pl.ds(r0, BLK), :]
                    sc = _dot(qn, kt_ref[:, :w]) + ((fc_ref[hh, pl.ds(r0, BLK), :] - l_ref[hh, pl.ds(r0, BLK), :])
                                                   - fr_ref[hh, :, :w])
                    masked = jnp.where(causal <= lim, sc[:, w - tail:], NEG)
                    p = jnp.exp(masked if seg == 0 else jnp.concatenate([sc[:, :w - tail], masked], axis=1))
                    dp = _dot(don, vt_ref[:, :w])
                    ds = p * (dp - jnp.sum(p * dp, axis=1, keepdims=True))
                    dqs.append(_dot(ds.astype(BF16), kb_ref[:w, :]))
                    dfr_ref[hh, :, :w] -= jnp.sum(ds, axis=0, keepdims=True)
                    sct = _dot(kb_ref[:w, :], qtm_ref[hh, n]) + ((frb_ref[hh, n] - lrb_ref[hh, n]) - fc_ref[hh, :w, :])
                    masked_t = jnp.where(causal_t <= lim, sct[w - tail:, :], NEG)
                    pt = jnp.exp(masked_t if seg == 0 else jnp.concatenate([sct[:w - tail, :], masked_t], axis=0))
                    dpt = _dot(vb_ref[:w, :], dotm_ref[hh, n])
                    dst = pt * (dpt - jnp.sum(pt * dpt, axis=0, keepdims=True))
                    dka_ref[:w, :] += _dot(dst.astype(BF16), qn)
                    dva_ref[:w, :] += _dot(pt.astype(BF16), don)
                dq_ref[pl.ds(r0, BLK), :] = (jnp.where(lane_lo, dqs[0], dqs[1]) * SCALE).astype(BF16)
                return carry

            lax.fori_loop(seg * per, (seg + 1) * per, nstep, 0)
        dk_ref[...] = dka_ref[...].astype(BF16)
        dv_ref[...] = dva_ref[...].astype(BF16)

    def tok(blk):
        return pl.BlockSpec((None, s, LANE), lambda i, p: (i, 0, blk + p))

    col = pl.BlockSpec((None, 2, s, 1), lambda i, p: (i, p, 0, 0))
    rowspec = pl.BlockSpec((None, 2, 1, s), lambda i, p: (i, p, 0, 0))
    rowbspec = pl.BlockSpec((None, 2, nb, 1, BLK), lambda i, p: (i, p, 0, 0, 0))
    outtok = pl.BlockSpec((None, s, LANE), lambda i, p: (i, 0, p))
    shp = jax.ShapeDtypeStruct((b, s, nh * HEAD_DIM), BF16)
    return pl.pallas_call(
        body, name=name, grid=(b, npair),
        out_shape=(shp, shp, shp, jax.ShapeDtypeStruct((b, nh, 1, s), F32)),
        in_specs=[tok(qblk), tok(kblk), tok(vblk),
                  pl.BlockSpec((None, s, LANE), lambda i, p: (i, 0, doblk + p)),
                  col, rowspec, rowbspec, col, rowbspec],
        out_specs=(outtok, outtok, outtok, rowspec),
        scratch_shapes=[pltpu.VMEM((2, s, LANE), BF16), pltpu.VMEM((2, s, LANE), BF16),
                        pltpu.VMEM((s, LANE), BF16), pltpu.VMEM((s, LANE), BF16),
                        pltpu.VMEM((LANE, s), BF16), pltpu.VMEM((LANE, s), BF16),
                        pltpu.VMEM((2, nb, LANE, BLK), BF16), pltpu.VMEM((2, nb, LANE, BLK), BF16),
                        pltpu.VMEM((s, LANE), F32), pltpu.VMEM((s, LANE), F32)],
        compiler_params=_cp(("parallel", "parallel")),
    )(proj, proj, proj, dmix, fcol, frow, frowb, lse, lserowb)


def _expm1(x):
    poly = x * (1.0 + x * (1.0 / 2 + x * (1.0 / 6 + x * (1.0 / 24 + x * (1.0 / 120 + x * (1.0 / 720))))))
    return jnp.where(x > -0.1, poly, jnp.exp(x) - 1.0)


def _softplus(z):
    return jnp.maximum(z, 0.0) + jnp.log(1.0 + jnp.exp(-jnp.abs(z)))


def _scan_rows(a, u, carry, row, up):
    tc, c = a.shape
    d = 1
    while d < tc:
        if d < SUBLANE:
            keep = (row >= d) if up else (row < tc - d)
            shift = d if up else tc - d
            a_sh = jnp.where(keep, pltpu.roll(a, shift, 0), 1.0)
            u_sh = jnp.where(keep, pltpu.roll(u, shift, 0), 0.0)
        elif up:
            a_sh = jnp.concatenate([jnp.ones((d, c), F32), a[:tc - d]], axis=0)
            u_sh = jnp.concatenate([jnp.zeros((d, c), F32), u[:tc - d]], axis=0)
        else:
            a_sh = jnp.concatenate([a[d:], jnp.ones((d, c), F32)], axis=0)
            u_sh = jnp.concatenate([u[d:], jnp.zeros((d, c), F32)], axis=0)
        u = a * u_sh + u
        a = a * a_sh
        d *= 2
    return u + a * carry


def _scan_up(a, u, carry, row):
    return _scan_rows(a, u, carry, row, True)


def _scan_down(bnext, g, carry, row):
    return _scan_rows(bnext, g, carry, row, False)


def _pick_row(val, row, which):
    return jnp.sum(jnp.where(row == which, val, 0.0), axis=0, keepdims=True)


def _lru_gates(xpad_ref, t0, tc, cw_ref, cb_ref, wa, ba_ref, wx, bx_ref, sp):
    xw = xpad_ref[pl.ds(t0, tc + SUBLANE), :]
    xc = cb_ref[...]
    for j in range(CONV_WIDTH):
        sh = CONV_WIDTH - 1 - j
        xs = xw if sh == 0 else pltpu.roll(xw, sh, 0)
        xc = xc + xs[SUBLANE:, :] * cw_ref[j:j + 1, :]
    xcb = xc.astype(BF16)
    r = _sigmoid(_dot(xcb, wa) + ba_ref[...])
    i = _sigmoid(_dot(xcb, wx) + bx_ref[...])
    la = -LRU_C * r * sp
    return xc, r, i, la


def _lru_specs(s, cb):
    seq = lambda bi, ni: (bi, 0, ni)
    return dict(
        seq=pl.BlockSpec((None, s, cb), seq),
        cw=pl.BlockSpec((CONV_WIDTH, cb), lambda bi, ni: (0, ni)),
        vec=pl.BlockSpec((1, cb), lambda bi, ni: (0, ni)),
        wblk=pl.BlockSpec((None, cb, cb), lambda bi, ni: (ni, 0, 0)),
    )


def _lru_fwd(proj, cw, cb_, wa, ba, wx, bx, lam, name):
    b, s, _ = proj.shape
    nblk, cb, _ = wa.shape
    tc = min(s, SCAN_CHUNK)
    nc = s // tc

    def body(x_ref, cw_ref, cb_ref, wa_ref, ba_ref, wx_ref, bx_ref, lam_ref, hs_ref, xpad_ref):
        xpad_ref[0:SUBLANE, :] = jnp.zeros((SUBLANE, cb), F32)
        xpad_ref[SUBLANE:, :] = x_ref[...].astype(F32)
        wa_b = wa_ref[...].astype(BF16)
        wx_b = wx_ref[...].astype(BF16)
        sp = _softplus(-lam_ref[...])
        row = lax.broadcasted_iota(jnp.int32, (tc, cb), 0)

        def chunk(ci, carry):
            t0 = pl.multiple_of(ci * tc, tc)
            xc, r, i, la = _lru_gates(xpad_ref, t0, tc, cw_ref, cb_ref, wa_b, ba_ref, wx_b, bx_ref, sp)
            a = jnp.exp(la)
            u = jnp.sqrt(-_expm1(2.0 * la)) * (i * xc)
            h = _scan_up(a, u, carry, row)
            hs_ref[pl.ds(t0, tc), :] = h
            return _pick_row(h, row, tc - 1)

        lax.fori_loop(0, nc, chunk, jnp.zeros((1, cb), F32))

    sp_ = _lru_specs(s, cb)
    return pl.pallas_call(
        body, name=name, grid=(b, nblk),
        out_shape=jax.ShapeDtypeStruct((b, s, nblk * cb), F32),
        in_specs=[sp_["seq"], sp_["cw"], sp_["vec"], sp_["wblk"], sp_["vec"], sp_["wblk"], sp_["vec"], sp_["vec"]],
        out_specs=sp_["seq"],
        scratch_shapes=[pltpu.VMEM((s + SUBLANE, cb), F32)],
        compiler_params=_cp(("parallel", "parallel")),
    )(proj, cw, cb_, wa, ba, wx, bx, lam)


def _lru_bwd(proj, hs, dhs, cw, cb_, wa, ba, wx, bx, lam, name):
    b, s, _ = proj.shape
    nblk, cb, _ = wa.shape
    tc = min(s, SCAN_CHUNK)
    nc = s // tc

    def body(x_ref, hs_ref, dhs_ref, cw_ref, cb_ref, wa_ref, ba_ref, wx_ref, bx_ref, lam_ref,
             dx_ref, dcw_ref, dcb_ref, dwa_ref, dba_ref, dwx_ref, dbx_ref, dlam_ref,
             xpad_ref, hpad_ref, dcpad_ref, xc_ref, r_ref, i_ref, a_ref, mult_ref):
        @pl.when(pl.program_id(1) == 0)
        def _():
            for ref in (dcw_ref, dcb_ref, dwa_ref, dba_ref, dwx_ref, dbx_ref, dlam_ref):
                ref[...] = jnp.zeros_like(ref)

        zeros8 = jnp.zeros((SUBLANE, cb), F32)
        xpad_ref[0:SUBLANE, :] = zeros8
        xpad_ref[SUBLANE:, :] = x_ref[...].astype(F32)
        hpad_ref[0:SUBLANE, :] = zeros8
        hpad_ref[SUBLANE:, :] = hs_ref[...]
        dcpad_ref[s:s + SUBLANE, :] = zeros8
        wa_b = wa_ref[...].astype(BF16)
        wx_b = wx_ref[...].astype(BF16)
        lam_v = lam_ref[...]
        sp = _softplus(-lam_v)
        dsp_dlam = -_sigmoid(-lam_v)
        row = lax.broadcasted_iota(jnp.int32, (tc, cb), 0)

        def recompute(ci, carry):
            t0 = pl.multiple_of(ci * tc, tc)
            xc, r, i, la = _lru_gates(xpad_ref, t0, tc, cw_ref, cb_ref, wa_b, ba_ref, wx_b, bx_ref, sp)
            xc_ref[pl.ds(t0, tc), :] = xc
            r_ref[pl.ds(t0, tc), :] = r
            i_ref[pl.ds(t0, tc), :] = i
            a_ref[pl.ds(t0, tc), :] = jnp.exp(la)
            mult_ref[pl.ds(t0, tc), :] = jnp.sqrt(-_expm1(2.0 * la))
            return carry

        lax.fori_loop(0, nc, recompute, 0)

        def adjoint(k, carry):
            g_next, a_first_next = carry
            t0 = pl.multiple_of((nc - 1 - k) * tc, tc)
            a = a_ref[pl.ds(t0, tc), :]
            a_next = jnp.where(row == tc - 1, a_first_next, pltpu.roll(a, tc - 1, 0))
            gg = _scan_down(a_next, dhs_ref[pl.ds(t0, tc), :], g_next, row)
            h_prev = pltpu.roll(hpad_ref[pl.ds(t0, tc + SUBLANE), :], 1, 0)[SUBLANE:, :]
            xc = xc_ref[pl.ds(t0, tc), :]
            r = r_ref[pl.ds(t0, tc), :]
            i = i_ref[pl.ds(t0, tc), :]
            mult = mult_ref[pl.ds(t0, tc), :]
            d_mult = gg * i * xc
            d_i = gg * mult * xc
            d_xc = gg * mult * i
            d_la = gg * h_prev * a - d_mult * (a * a) / mult
            d_zr = (d_la * (-LRU_C * sp)) * r * (1.0 - r)
            d_zi = d_i * i * (1.0 - i)
            dlam_ref[...] += jnp.sum(d_la * (-LRU_C * r), axis=0, keepdims=True) * dsp_dlam
            dzr_b = d_zr.astype(BF16)
            dzi_b = d_zi.astype(BF16)
            xcb = xc.astype(BF16)
            d_xc = d_xc + _dot_nt(dzr_b, wa_b) + _dot_nt(dzi_b, wx_b)
            dwa_ref[...] += _dot_tn(xcb, dzr_b)
            dwx_ref[...] += _dot_tn(xcb, dzi_b)
            dba_ref[...] += jnp.sum(d_zr, axis=0, keepdims=True)
            dbx_ref[...] += jnp.sum(d_zi, axis=0, keepdims=True)
            dcb_ref[...] += jnp.sum(d_xc, axis=0, keepdims=True)
            dcpad_ref[pl.ds(t0, tc), :] = d_xc
            return _pick_row(gg, row, 0), _pick_row(a, row, 0)

        zero = jnp.zeros((1, cb), F32)
        lax.fori_loop(0, nc, adjoint, (zero, zero))

        def conv_back(ci, carry):
            t0 = pl.multiple_of(ci * tc, tc)
            dw = dcpad_ref[pl.ds(t0, tc + SUBLANE), :]
            xw = xpad_ref[pl.ds(t0, tc + SUBLANE), :]
            d_xc = dw[:tc, :]
            dxr = jnp.zeros((tc, cb), F32)
            for j in range(CONV_WIDTH):
                sh = CONV_WIDTH - 1 - j
                dsh = dw if sh == 0 else pltpu.roll(dw, tc + SUBLANE - sh, 0)
                dxr = dxr + dsh[:tc, :] * cw_ref[j:j + 1, :]
                xs = xw if sh == 0 else pltpu.roll(xw, sh, 0)
                dcw_ref[j:j + 1, :] += jnp.sum(d_xc * xs[SUBLANE:, :], axis=0, keepdims=True)
            dx_ref[pl.ds(t0, tc), :] = dxr.astype(BF16)
            return carry

        lax.fori_loop(0, nc, conv_back, 0)

    seq = lambda ni, bi: (bi, 0, ni)
    seqspec = pl.BlockSpec((None, s, cb), seq)
    cwspec = pl.BlockSpec((CONV_WIDTH, cb), lambda ni, bi: (0, ni))
    vec = pl.BlockSpec((1, cb), lambda ni, bi: (0, ni))
    wblk = pl.BlockSpec((None, cb, cb), lambda ni, bi: (ni, 0, 0))
    w = nblk * cb
    return pl.pallas_call(
        body, name=name, grid=(nblk, b),
        out_shape=(jax.ShapeDtypeStruct((b, s, w), BF16), jax.ShapeDtypeStruct((CONV_WIDTH, w), F32),
                   jax.ShapeDtypeStruct((1, w), F32), jax.ShapeDtypeStruct((nblk, cb, cb), F32),
                   jax.ShapeDtypeStruct((1, w), F32), jax.ShapeDtypeStruct((nblk, cb, cb), F32),
                   jax.ShapeDtypeStruct((1, w), F32), jax.ShapeDtypeStruct((1, w), F32)),
        in_specs=[seqspec, seqspec, seqspec, cwspec, vec, wblk, vec, wblk, vec, vec],
        out_specs=(seqspec, cwspec, vec, wblk, vec, wblk, vec, vec),
        scratch_shapes=[pltpu.VMEM((s + SUBLANE, cb), F32)] * 3 + [pltpu.VMEM((s, cb), F32)] * 5,
        compiler_params=_cp(("parallel", "arbitrary")),
    )(proj, hs, dhs, cw, cb_, wa, ba, wx, bx, lam)


def _adamw(w, g, m, v, name):
    shape = w.shape
    total = int(np.prod(shape))
    if w.ndim >= 2 and shape[-2] % SUBLANE == 0:
        rows, cols = shape[-2:]
    else:
        cols = 1024
        rows = -(-(-(-total // cols)) // SUBLANE) * SUBLANE
    lead = -(-total // (rows * cols))
    tr = _row_tile(rows, 512)
    pad = lead * rows * cols - total

    def flat(a):
        if pad:
            a = jnp.pad(a.reshape(-1), (0, pad))
        return a.reshape(lead, rows, cols)

    c1 = 1.0 - ADAM_B1 ** ADAM_STEP
    c2 = 1.0 - ADAM_B2 ** ADAM_STEP

    def body(w_ref, g_ref, m_ref, v_ref, d_ref, nm_ref, nv_ref):
        gv = g_ref[...]
        nm = ADAM_B1 * m_ref[...] + (1.0 - ADAM_B1) * gv
        nv = ADAM_B2 * v_ref[...] + (1.0 - ADAM_B2) * (gv * gv)
        nm_ref[...] = nm
        nv_ref[...] = nv
        d_ref[...] = -ADAM_LR * ((nm / c1) / (jnp.sqrt(nv / c2) + ADAM_EPS) + ADAM_WD * w_ref[...])

    spec = pl.BlockSpec((None, tr, cols), lambda l, i: (l, i, 0))
    shp = jax.ShapeDtypeStruct((lead, rows, cols), F32)
    outs = pl.pallas_call(
        body, name=name, grid=(lead, rows // tr), out_shape=(shp, shp, shp),
        in_specs=[spec] * 4, out_specs=(spec,) * 3,
        compiler_params=_cp(("parallel", "parallel")),
    )(flat(w), flat(g), flat(m), flat(v))
    if pad:
        return tuple(o.reshape(-1)[:total].reshape(shape) for o in outs)
    return tuple(o.reshape(shape) for o in outs)


def _to_heads(t, nh):
    b, s, _ = t.shape
    return t.reshape(b, s, nh, HEAD_DIM).transpose(0, 2, 1, 3)


def _stack_heads(t):
    b, s, _ = t.shape
    steps = s // (SWA_BPS * BLK)
    t = t.reshape(b, steps, SWA_BPS, BLK, A_KV_HEADS, A_GROUP, HEAD_DIM).transpose(0, 4, 1, 2, 5, 3, 6)
    return t.reshape(b, A_KV_HEADS, steps, SWA_BPS * A_GROUP * BLK, HEAD_DIM)


def _unstack_heads(t):
    b, hkv, steps, rows, hd = t.shape
    t = t.reshape(b, hkv, steps, SWA_BPS, A_GROUP, BLK, hd).transpose(0, 2, 3, 5, 1, 4, 6)
    return t.reshape(b, steps * SWA_BPS * BLK, hkv * A_GROUP * hd)


def _from_heads(t):
    b, nh, s, hd = t.shape
    return t.transpose(0, 2, 1, 3).reshape(b, s, nh * hd)


def _pad_rows(a, mult):
    r = a.shape[0]
    p = (-r) % mult
    return jnp.pad(a, ((0, p), (0, 0))) if p else a


def kernel(x, c, rel_bias, norm_g, ada_w, ada_b, attn_w_in, attn_sinks, attn_b_f, attn_w_out, lru_w_in, lru_conv_w, lru_conv_b, lru_w_a, lru_b_a, lru_w_x, lru_b_x, lru_lambda, lru_w_out, final_g, loss_target, m_rel_bias, m_norm_g, m_ada_w, m_ada_b, m_attn_w_in, m_attn_sinks, m_attn_b_f, m_attn_w_out, m_lru_w_in, m_lru_conv_w, m_lru_conv_b, m_lru_w_a, m_lru_b_a, m_lru_w_x, m_lru_b_x, m_lru_lambda, m_lru_w_out, m_final_g, v_rel_bias, v_norm_g, v_ada_w, v_ada_b, v_attn_w_in, v_attn_sinks, v_attn_b_f, v_attn_w_out, v_lru_w_in, v_lru_conv_w, v_lru_conv_b, v_lru_w_a, v_lru_b_a, v_lru_w_x, v_lru_b_x, v_lru_lambda, v_lru_w_out, v_final_g):
    bl, s, d = x.shape
    ix, iy, ic = lax.axis_index("x"), lax.axis_index("y"), lax.axis_index("c")
    chip = 2 * ix + iy
    me = 2 * chip + ic
    nb = s // BLK
    aw = A_Q_HEADS * HEAD_DIM
    akv = A_KV_HEADS * HEAD_DIM
    bw = B_HEADS * HEAD_DIM
    mixw = aw + bw
    qkv_w = aw + 2 * akv + 3 * bw
    n_in = attn_w_in.shape[2] * N_CHIP
    lw = lru_lambda.shape[1] * N_CHIP
    n0 = mixw + qkv_w + LANE

    rows_pad = -(-bl // SUBLANE) * SUBLANE
    vec_rows = jnp.concatenate([lru_conv_w[0], lru_conv_b, lru_b_a, lru_b_x, lru_lambda], axis=0)
    first = jnp.concatenate([_pad_rows(c, SUBLANE), jnp.pad(vec_rows, ((0, 0), (0, d - lw // N_CHIP)))], axis=0)
    first = _all_gather8(first, "gather_c", pltpu.VMEM).reshape(N_DEV, rows_pad + SUBLANE, d)
    c_all = first[:, :bl].reshape(N_DEV * bl, d)
    vec_all = first[:, rows_pad:, :lw // N_CHIP].reshape(N_CHIP, 2, SUBLANE, lw // N_CHIP)[:, 0]
    vec_all = vec_all.transpose(1, 0, 2).reshape(SUBLANE, lw)
    ncol = ada_w.shape[2]
    ada_w_l = lax.dynamic_index_in_dim(ada_w, ic, 0, keepdims=False)
    ada_b_l = lax.dynamic_slice(ada_b, (ic, chip * ncol), (1, ncol))
    mod_part = _ada_fwd(c_all, ada_w_l, ada_b_l, "ada_fwd")
    mod_all = _all_gather8(_pad_rows(mod_part, SUBLANE), "gather_mod", pltpu.VMEM)
    mrows = -(-(N_DEV * bl) // SUBLANE) * SUBLANE
    mod_all = mod_all.reshape(N_CHIP, 2, mrows, ncol)[:, :, :N_DEV * bl]
    mod_all = mod_all.transpose(1, 2, 0, 3).reshape(2, N_DEV * bl, N_CHIP * ncol)
    mod = lax.dynamic_slice_in_dim(mod_all, me * bl, bl, axis=1)
    shift = [mod[l, :, 0:d].reshape(bl, 1, d) for l in range(2)]
    scale = [mod[l, :, d:2 * d].reshape(bl, 1, d) for l in range(2)]
    gmod = [mod[l, :, 2 * d:3 * d].reshape(bl, 1, d) for l in range(2)]

    c_in0 = n_in // N_CHIP
    c_in1 = 2 * lw // N_CHIP
    assert c_in0 <= d and 2 * c_in1 == d
    r_in0, r_out0, r_in1, r_out1 = d // 2, mixw // N_CHIP // 2, d // 4, lw // N_CHIP // 2
    o_out0, o_in1, o_out1 = r_in0, r_in0 + r_out0, r_in0 + r_out0 + r_in1
    big_rows = o_out1 + r_out1

    def half_of(a, rows):
        return lax.dynamic_slice_in_dim(a, ic * rows, rows, axis=0)

    h_in1 = half_of(lru_w_in[0], r_in0).astype(BF16)
    my_half = jnp.concatenate([
        jnp.pad(half_of(attn_w_in[0], r_in0).astype(BF16), ((0, 0), (0, d - c_in0))),
        half_of(attn_w_out[0], r_out0).astype(BF16),
        jnp.concatenate([h_in1[:r_in1], h_in1[r_in1:]], axis=1),
        half_of(lru_w_out[0], r_out1).astype(BF16)], axis=0)
    gat = _all_gather8(my_half, "gather_weights", pltpu.HBM).reshape(N_CHIP, 2, big_rows, d)
    w_in0 = gat[:, :, :r_in0, :c_in0].transpose(1, 2, 0, 3).reshape(d, n_in)
    w_out0 = gat[:, :, o_out0:o_in1].reshape(mixw, d)
    w_in1 = gat[:, :, o_in1:o_out1].reshape(N_CHIP, 2, r_in1, 2, c_in1)
    w_in1 = w_in1.transpose(1, 3, 2, 0, 4).reshape(d, 2 * lw)
    w_out1 = gat[:, :, o_out1:].reshape(lw, d)
    w_cat0 = jnp.concatenate([w_in0[:, qkv_w + B_HEADS:], w_in0[:, :qkv_w + B_HEADS],
                              jnp.zeros((d, n0 - n_in), BF16)], axis=1)

    proj0, h0, zf = _norm_proj(x, norm_g[0:1], scale[0], shift[0], w_cat0, LANE, "norm_proj0")
    o_a = mixw
    aq = _stack_heads(proj0[:, :, o_a:o_a + aw].astype(BF16))
    ak = _to_heads(proj0[:, :, o_a + aw:o_a + aw + akv].astype(BF16), A_KV_HEADS)
    av = _to_heads(proj0[:, :, o_a + aw + akv:o_a + aw + 2 * akv].astype(BF16), A_KV_HEADS)
    o_b = o_a + aw + 2 * akv
    fox_blks = (o_b // LANE, (o_b + bw) // LANE, (o_b + 2 * bw) // LANE)
    bucket_np, valid_np = _rel_buckets()
    bucket = jnp.asarray(bucket_np)
    bias = _swa_bias(rel_bias.T, bucket, jnp.asarray(valid_np), "swa_bias")
    bias = bias.reshape(A_KV_HEADS, A_GROUP * BLK, 2 * BLK)
    bias = jnp.concatenate([jnp.pad(bias, ((0, 0), (0, 0), (blk * BLK, (SWA_BPS - 1 - blk) * BLK)),
                                    constant_values=NEG) for blk in range(SWA_BPS)], axis=1)
    sinks = jnp.repeat(attn_sinks[0].reshape(A_KV_HEADS, A_GROUP), BLK, axis=1).reshape(A_KV_HEADS, A_GROUP * BLK, 1)
    sinks = jnp.tile(sinks, (1, SWA_BPS, 1))
    a_out, a_lse = _swa_fwd(aq, ak, av, bias, sinks, "swa_fwd")
    bf_pad = jnp.pad(attn_b_f, ((0, 0), (0, LANE - B_HEADS)))
    fsum = _fox_decay(zf, bf_pad, "fox_decay")
    fh = fsum[:, :, :B_HEADS].transpose(0, 2, 1)
    fcol = fh.reshape(bl, B_HEADS, s, 1)
    frow = fh.reshape(bl, B_HEADS, 1, s)
    fbq = min(s, FOX_BQ)
    frowb = fh.reshape(bl, B_HEADS, s // fbq, 1, fbq)
    b_out, b_lse = _fox_fwd(proj0, *fox_blks, fcol, frow, "fox_fwd")
    lserowb = b_lse.reshape(bl, B_HEADS, s // fbq, 1, fbq)
    mix0 = [_unstack_heads(a_out), b_out]
    x1, o0 = _gate_outproj(mix0, proj0, 0, w_out0, x, gmod[0], "gate_outproj0")

    proj1, h1 = _norm_proj(x1, norm_g[1:2], scale[1], shift[1], w_in1, 0, "norm_proj1")
    cw_f, cb_f, ba_f, bx_f, lam_f = vec_all[0:4], vec_all[4:5], vec_all[5:6], vec_all[6:7], vec_all[7:8]
    hs = _lru_fwd(proj1, cw_f, cb_f, lru_w_a[0], ba_f, lru_w_x[0], bx_f, lam_f, "lru_fwd")
    x2, o1 = _gate_outproj([hs], proj1, 1, w_out1, x1, gmod[1], "gate_outproj1")

    loss_vec, dx2, g_final = _final_loss(x2, final_g.reshape(1, d), loss_target, "final_loss")
    loss = lax.psum(loss_vec[0, 0], ("x", "y", "c"))

    dhs, dgate1, do1, y1, dgm1 = _bwd_out(dx2, gmod[1], o1, [hs], proj1, 1, w_out1, F32, "bwd_out1")
    g_w_out1 = _matmul_tn(y1, [do1], "grad_w_out1")
    (dxr, g_cw, g_cb, g_wa, g_ba, g_wx, g_bx, g_lam) = _lru_bwd(
        proj1, hs, dhs, cw_f, cb_f, lru_w_a[0], ba_f, lru_w_x[0], bx_f, lam_f, "lru_bwd")
    dproj1 = [dxr, dgate1]
    g_w_in1 = _matmul_tn(h1, dproj1, "grad_w_in1")
    dx1, dsh1, dsc1, g_ng1 = _bwd_in(dproj1, w_in1, x1, norm_g[1:2], scale[1], dx2, "bwd_in1")

    dmix0, dgate0, do0, y0, dgm0 = _bwd_out(dx1, gmod[0], o0, mix0, proj0, 0, w_out0, BF16, "bwd_out0")
    g_w_out0 = _matmul_tn(y0, [do0], "grad_w_out0")
    da_out = _stack_heads(dmix0[:, :, :aw].astype(BF16))
    daq, dak, dav, dbias, dsink = _swa_bwd(aq, ak, av, bias, sinks, da_out, a_lse, "swa_bwd")
    dbq, dbk, dbv, dfrow = _fox_bwd(proj0, *fox_blks, dmix0, aw // LANE, fcol, frow, frowb, b_lse, lserowb,
                                    "fox_bwd")
    df = dfrow.reshape(bl, B_HEADS, s).transpose(0, 2, 1)
    df = jnp.pad(df, ((0, 0), (0, 0), (0, LANE - B_HEADS)))
    dzf, g_bf = _fox_dgate(df, zf, bf_pad, B_HEADS, "fox_dgate")
    dproj0 = ([dgate0, _unstack_heads(daq), _from_heads(dak), _from_heads(dav)]
              + [dbq, dbk, dbv, dzf.astype(BF16)])
    g_w_cat0 = _matmul_tn(h0, dproj0, "grad_w_in0")
    g_w_in0 = jnp.concatenate([g_w_cat0[:, mixw:mixw + qkv_w + B_HEADS], g_w_cat0[:, :mixw]], axis=1)
    dx0, dsh0, dsc0, g_ng0 = _bwd_in(dproj0, w_cat0, x, norm_g[0:1], scale[0], dx1, "bwd_in0")
    dbias = sum(dbias[:, :, blk * A_GROUP * BLK:(blk + 1) * A_GROUP * BLK, blk * BLK:(blk + 2) * BLK]
                for blk in range(SWA_BPS))
    g_relb, g_sink = _swa_small_grads(dbias.reshape(bl, A_Q_HEADS, BLK, 2 * BLK),
                                      dsink.reshape(bl, A_Q_HEADS, 1, LANE), bucket, "swa_small_grads")

    dmod = jnp.concatenate([jnp.concatenate([dsh0, dsc0, dgm0], axis=-1),
                            jnp.concatenate([dsh1, dsc1, dgm1], axis=-1)], axis=1)
    dmod_rows = _pad_rows(dmod.reshape(bl * 6, d), SUBLANE)

    tail = jnp.concatenate([g_relb[:, :, 0].T.reshape(-1), g_sink[:, 0, 0], g_bf[0, :B_HEADS]])
    n_relb = REL_BUCKETS * A_Q_HEADS
    small_rows = [g_wa.reshape(-1, d), g_wx.reshape(-1, d), g_ng0, g_ng1, g_final, g_cw, g_cb, g_ba, g_bx, g_lam,
                  jnp.pad(tail, (0, d - tail.shape[0])).reshape(1, d)]
    small_counts = [r.shape[0] for r in small_rows]
    piece_rows = -(-(-(-sum(small_counts) // N_DEV)) // SUBLANE) * SUBLANE
    small_2d = jnp.concatenate(small_rows, axis=0)
    small_2d = jnp.pad(small_2d, ((0, N_DEV * piece_rows - small_2d.shape[0]), (0, 0)))
    small_pieces = small_2d.reshape(N_CHIP, 2, piece_rows, d)
    p_in0 = jnp.pad(g_w_in0.reshape(2, r_in0, N_CHIP, c_in0).transpose(2, 0, 1, 3),
                    ((0, 0), (0, 0), (0, 0), (0, d - c_in0)))
    p_in1 = g_w_in1.reshape(2, 2, r_in1, N_CHIP, c_in1).transpose(3, 0, 2, 1, 4).reshape(N_CHIP, 2, r_in1, d)
    pieces = jnp.concatenate([p_in0, g_w_out0.reshape(N_CHIP, 2, r_out0, d), p_in1,
                              g_w_out1.reshape(N_CHIP, 2, r_out1, d), small_pieces], axis=2)
    theirs = _sibling_push(pieces, True, "push_sibling_halves")
    partial = _pair_sum(jnp.reshape(ic, (1,)).astype(jnp.int32), pieces, theirs, "sum_chip")
    slots = _chip_all_to_all(partial, "exchange_grads")
    reduced = _sum_slots(slots, "sum_grads")
    mine_big = reduced[:big_rows]
    other_big = _sibling_push(mine_big[None], False, "swap_halves")[0]
    both = jnp.stack([jnp.where(ic == 0, mine_big, other_big), jnp.where(ic == 0, other_big, mine_big)])
    g_big = [both[:, :r_in0, :c_in0].reshape(d, c_in0),
             both[:, o_out0:o_in1].reshape(2 * r_out0, d),
             both[:, o_in1:o_out1].reshape(2, r_in1, 2, c_in1).transpose(0, 2, 1, 3).reshape(d, c_in1),
             both[:, o_out1:].reshape(2 * r_out1, d)]
    last = _all_gather8(jnp.concatenate([reduced[big_rows:], dmod_rows], axis=0), "gather_small_grads", pltpu.VMEM)
    last = last.reshape(N_DEV, piece_rows + dmod_rows.shape[0], d)
    small_all = last[:, :piece_rows].reshape(N_DEV * piece_rows, d)
    dmod_all = last[:, piece_rows:piece_rows + bl * 6].reshape(N_DEV * bl, 6 * d)
    dmod_chip = lax.dynamic_slice_in_dim(dmod_all.reshape(N_DEV * bl, 2, 3 * d), chip * ncol, ncol, axis=2)
    g_ada_w, g_ada_b = _ada_bwd(c_all, dmod_chip.transpose(1, 0, 2), dmod_all, "ada_bwd")
    g_ada_b = g_ada_b.reshape(2, 3 * d)
    g_small, off = [], 0
    for cnt in small_counts:
        g_small.append(small_all[off:off + cnt])
        off += cnt
    g_w_a, g_w_x = g_small[0].reshape(lru_w_a.shape[1:]), g_small[1].reshape(lru_w_x.shape[1:])
    g_norm_g = jnp.concatenate(g_small[2:4], axis=0)
    g_fin, g_cw_r, g_cb_r, g_ba_r, g_bx_r, g_lam_r = g_small[4:10]
    tail = g_small[10][0]
    g_rel_bias = tail[:n_relb].reshape(REL_BUCKETS, A_Q_HEADS)
    g_sinks, g_b_f = tail[n_relb:n_relb + A_Q_HEADS], tail[n_relb + A_Q_HEADS:n_relb + A_Q_HEADS + B_HEADS]
    cw4 = lw // N_CHIP

    def my_cols(a):
        return lax.dynamic_slice_in_dim(a, chip * cw4, cw4, axis=1)

    grads = {
        "rel_bias": g_rel_bias, "norm_g": g_norm_g, "ada_w": g_ada_w, "ada_b": g_ada_b,
        "attn_w_in": g_big[0][None], "attn_sinks": g_sinks[None], "attn_b_f": g_b_f[None],
        "attn_w_out": g_big[1][None], "lru_w_in": g_big[2][None], "lru_conv_w": my_cols(g_cw_r)[None],
        "lru_conv_b": my_cols(g_cb_r), "lru_w_a": g_w_a[None], "lru_b_a": my_cols(g_ba_r),
        "lru_w_x": g_w_x[None], "lru_b_x": my_cols(g_bx_r), "lru_lambda": my_cols(g_lam_r),
        "lru_w_out": g_big[3][None], "final_g": g_fin.reshape(d),
    }
    weights = dict(rel_bias=rel_bias, norm_g=norm_g, ada_w=ada_w, ada_b=ada_b, attn_w_in=attn_w_in,
                   attn_sinks=attn_sinks, attn_b_f=attn_b_f, attn_w_out=attn_w_out, lru_w_in=lru_w_in,
                   lru_conv_w=lru_conv_w, lru_conv_b=lru_conv_b, lru_w_a=lru_w_a, lru_b_a=lru_b_a,
                   lru_w_x=lru_w_x, lru_b_x=lru_b_x, lru_lambda=lru_lambda, lru_w_out=lru_w_out, final_g=final_g)
    moms = dict(rel_bias=(m_rel_bias, v_rel_bias), norm_g=(m_norm_g, v_norm_g), ada_w=(m_ada_w, v_ada_w),
                ada_b=(m_ada_b, v_ada_b), attn_w_in=(m_attn_w_in, v_attn_w_in),
                attn_sinks=(m_attn_sinks, v_attn_sinks), attn_b_f=(m_attn_b_f, v_attn_b_f),
                attn_w_out=(m_attn_w_out, v_attn_w_out), lru_w_in=(m_lru_w_in, v_lru_w_in),
                lru_conv_w=(m_lru_conv_w, v_lru_conv_w), lru_conv_b=(m_lru_conv_b, v_lru_conv_b),
                lru_w_a=(m_lru_w_a, v_lru_w_a), lru_b_a=(m_lru_b_a, v_lru_b_a), lru_w_x=(m_lru_w_x, v_lru_w_x),
                lru_b_x=(m_lru_b_x, v_lru_b_x), lru_lambda=(m_lru_lambda, v_lru_lambda),
                lru_w_out=(m_lru_w_out, v_lru_w_out), final_g=(m_final_g, v_final_g))
    names = list(weights)
    big_names = [n for n in names if weights[n].size >= 65536]
    small_names = [n for n in names if weights[n].size < 65536]
    delta, new_m, new_v = {}, {}, {}
    for n in big_names:
        delta[n], new_m[n], new_v[n] = _adamw(weights[n], grads[n].reshape(weights[n].shape),
                                              moms[n][0], moms[n][1], "adamw_" + n)
    cat = lambda arrs: jnp.concatenate([a.reshape(-1) for a in arrs])
    sd, sm, sv = _adamw(cat([weights[n] for n in small_names]), cat([grads[n] for n in small_names]),
                        cat([moms[n][0] for n in small_names]), cat([moms[n][1] for n in small_names]),
                        "adamw_small")
    off = 0
    for n in small_names:
        sz = weights[n].size
        shp = weights[n].shape
        delta[n], new_m[n], new_v[n] = (sd[off:off + sz].reshape(shp), sm[off:off + sz].reshape(shp),
                                        sv[off:off + sz].reshape(shp))
        off += sz
    out_grads = [grads[n].reshape(weights[n].shape) for n in names]
    return (loss, dx0, *out_grads, *[delta[n] for n in names], *[new_m[n] for n in names],
            *[new_v[n] for n in names])
```

```python
import math

import numpy as np
import jax
import jax.numpy as jnp
from jax import lax
from jax.experimental import pallas as pl
from jax.experimental.pallas import tpu as pltpu

F32 = jnp.float32
BF16 = jnp.bfloat16
MESH = pl.DeviceIdType.MESH

N_DEV = 8
N_CHIP = 4
HEAD_DIM = 64
BLK = 128
A_Q_HEADS = 8
A_KV_HEADS = 2
A_GROUP = A_Q_HEADS // A_KV_HEADS
B_HEADS = 8
REL_BUCKETS = 32
REL_MAX_EXACT = 16
REL_MAX_DIST = 128
LRU_C = 8.0
CONV_WIDTH = 4
EPS = 1e-6
NEG = -1e30
SCALE = HEAD_DIM ** -0.5
LANE = 128
SUBLANE = 8
VMEM_LIMIT = 56 * 1024 * 1024
SCAN_CHUNK = 512
ROW_TILE = 1024
SWA_BPS = 1
FOX_BQ = 512
ADAM_LR = 0.001
ADAM_B1 = 0.9
ADAM_B2 = 0.999
ADAM_EPS = 1e-08
ADAM_WD = 0.01
ADAM_STEP = 10
HI = lax.Precision.HIGHEST


def _cp(sem=None):
    return pltpu.CompilerParams(dimension_semantics=sem, vmem_limit_bytes=VMEM_LIMIT)


def _dot(a, b):
    return jnp.dot(a, b, preferred_element_type=F32)


def _dot_nt(a, b):
    return lax.dot_general(a, b, (((1,), (1,)), ((), ())), preferred_element_type=F32)


def _dot_tn(a, b):
    return lax.dot_general(a, b, (((0,), (0,)), ((), ())), preferred_element_type=F32)


def _sigmoid(z):
    return 1.0 / (1.0 + jnp.exp(-z))


def _row_tile(rows, cap):
    if rows <= cap:
        return rows
    best = SUBLANE
    t = SUBLANE
    while t <= cap:
        if rows % t == 0:
            best = t
        t += SUBLANE
    return best


def _all_gather8(x_shard, name, space):
    m_per, n = x_shard.shape
    n_own = 8 if (space == pltpu.HBM and m_per % 128 == 0) else 1
    own_rows = m_per // n_own

    def body(x_ref, out_ref, send_sems, recv_sems, local_sems):
        x, y, c = lax.axis_index("x"), lax.axis_index("y"), lax.axis_index("c")
        me, sibling = (x, y, c), (x, y, 1 - c)
        chips = [(1 - x, y), (x, 1 - y), (1 - x, 1 - y)]

        def rows(px, py, pc):
            return out_ref.at[pl.ds((4 * px + 2 * py + pc) * m_per, m_per), :]

        def copy(k, block, to, src=None):
            return pltpu.make_async_remote_copy(
                src_ref=rows(*block) if src is None else src, dst_ref=rows(*block),
                send_sem=send_sems.at[k], recv_sem=recv_sems.at[k], device_id=to, device_id_type=MESH)

        base = (4 * x + 2 * y + c) * m_per
        mine = [pltpu.make_async_copy(x_ref.at[pl.ds(i * own_rows, own_rows), :],
                                      out_ref.at[pl.ds(base + i * own_rows, own_rows), :], local_sems.at[i])
                for i in range(n_own)]
        for cp in mine:
            cp.start()
        first = [copy(0, me, sibling, src=x_ref)]
        first += [copy(1 + j, me, (*chip, c), src=x_ref) for j, chip in enumerate(chips)]
        for cp in first:
            cp.start()
        passed = [copy(4 + j, (*chip, c), sibling) for j, chip in enumerate(chips)]
        for j, chip in enumerate(chips):
            copy(1 + j, (*chip, c), me).wait_recv()
            passed[j].start()
        copy(0, sibling, me).wait_recv()
        for j, chip in enumerate(chips):
            copy(4 + j, (*chip, 1 - c), me).wait_recv()
        for cp in first + passed:
            cp.wait_send()
        for cp in mine:
            cp.wait()

    return pl.pallas_call(
        body, name=name,
        out_shape=jax.ShapeDtypeStruct((N_DEV * m_per, n), x_shard.dtype),
        in_specs=[pl.BlockSpec(memory_space=space)],
        out_specs=pl.BlockSpec(memory_space=space),
        scratch_shapes=[pltpu.SemaphoreType.DMA((7,)), pltpu.SemaphoreType.DMA((7,)),
                        pltpu.SemaphoreType.DMA((n_own,))],
        compiler_params=pltpu.CompilerParams(vmem_limit_bytes=VMEM_LIMIT),
    )(x_shard)


def _sibling_push(blocks, pick_other, name):
    nblk = blocks.shape[0]
    m, n = blocks.shape[-2:]

    def body(x_ref, out_ref, send_sems, recv_sems):
        x, y, c = lax.axis_index("x"), lax.axis_index("y"), lax.axis_index("c")
        copies = []
        for k in range(nblk):
            src = x_ref.at[k, 1 - c] if pick_other else x_ref.at[k]
            copies.append(pltpu.make_async_remote_copy(
                src_ref=src, dst_ref=out_ref.at[k], send_sem=send_sems.at[k], recv_sem=recv_sems.at[k],
                device_id=(x, y, 1 - c), device_id_type=MESH))
        for cp in copies:
            cp.start()
        for cp in copies:
            cp.wait_recv()
        for cp in copies:
            cp.wait_send()

    hbm = pl.BlockSpec(memory_space=pltpu.HBM)
    return pl.pallas_call(
        body, name=name,
        out_shape=jax.ShapeDtypeStruct((nblk, m, n), blocks.dtype),
        in_specs=[hbm], out_specs=hbm,
        scratch_shapes=[pltpu.SemaphoreType.DMA((nblk,)), pltpu.SemaphoreType.DMA((nblk,))],
    )(blocks)


def _chip_all_to_all(parts, name):
    _, m, n = parts.shape

    def body(x_ref, out_ref, send_sems, recv_sems, local_sem):
        x, y, c = lax.axis_index("x"), lax.axis_index("y"), lax.axis_index("c")
        me = 2 * x + y
        mine = pltpu.make_async_copy(x_ref.at[me], out_ref.at[me], local_sem)
        mine.start()
        copies = []
        for k in range(1, N_CHIP):
            px, py = x ^ ((k >> 1) & 1), y ^ (k & 1)
            copies.append(pltpu.make_async_remote_copy(
                src_ref=x_ref.at[2 * px + py], dst_ref=out_ref.at[me],
                send_sem=send_sems.at[k - 1], recv_sem=recv_sems.at[k - 1],
                device_id=(px, py, c), device_id_type=MESH))
        for cp in copies:
            cp.start()
        for cp in copies:
            cp.wait_recv()
        for cp in copies:
            cp.wait_send()
        mine.wait()

    hbm = pl.BlockSpec(memory_space=pltpu.HBM)
    return pl.pallas_call(
        body, name=name,
        out_shape=jax.ShapeDtypeStruct(parts.shape, parts.dtype),
        in_specs=[hbm], out_specs=hbm,
        scratch_shapes=[pltpu.SemaphoreType.DMA((N_CHIP - 1,)), pltpu.SemaphoreType.DMA((N_CHIP - 1,)),
                        pltpu.SemaphoreType.DMA],
    )(parts)


def _pair_sum(core, pieces, theirs, name):
    nblk, _, m, n = pieces.shape
    tr = _row_tile(m, 536)

    def body(c_ref, p_ref, t_ref, o_ref):
        o_ref[...] = (p_ref[...] + t_ref[...]).astype(BF16)

    return pl.pallas_call(
        body, name=name,
        grid_spec=pltpu.PrefetchScalarGridSpec(
            num_scalar_prefetch=1, grid=(nblk, m // tr),
            in_specs=[pl.BlockSpec((None, None, tr, n), lambda k, i, c_ref: (k, c_ref[0], i, 0)),
                      pl.BlockSpec((None, tr, n), lambda k, i, c_ref: (k, i, 0))],
            out_specs=pl.BlockSpec((None, tr, n), lambda k, i, c_ref: (k, i, 0))),
        out_shape=jax.ShapeDtypeStruct((nblk, m, n), BF16),
        compiler_params=_cp(("parallel", "parallel")),
    )(core, pieces, theirs)


def _sum_slots(slots, name):
    k, m, n = slots.shape
    tr = _row_tile(m, 536)

    def body(s_ref, o_ref):
        acc = s_ref[0].astype(F32)
        for j in range(1, k):
            acc = acc + s_ref[j].astype(F32)
        o_ref[...] = acc

    return pl.pallas_call(
        body, name=name, grid=(m // tr,),
        out_shape=jax.ShapeDtypeStruct((m, n), F32),
        in_specs=[pl.BlockSpec((k, tr, n), lambda i: (0, i, 0))],
        out_specs=pl.BlockSpec((tr, n), lambda i: (i, 0)),
        compiler_params=_cp(("parallel",)),
    )(slots)


def _ada_fwd(c_all, w, b, name):
    r, _ = c_all.shape
    n = w.shape[1]

    def body(c_ref, w_ref, b_ref, o_ref):
        cv = c_ref[...]
        act = cv * _sigmoid(cv)
        o_ref[...] = jnp.dot(act, w_ref[...], precision=HI, preferred_element_type=F32) + b_ref[...]

    return pl.pallas_call(body, name=name, out_shape=jax.ShapeDtypeStruct((r, n), F32),
                          compiler_params=_cp())(c_all, w, b)


def _ada_bwd(c_all, dmod_chip, dmod_all, name):
    r, d = c_all.shape
    nl, _, n = dmod_chip.shape

    def body(c_ref, dm_ref, da_ref, gw_ref, gb_ref):
        cv = c_ref[...]
        act = cv * _sigmoid(cv)
        for l in range(nl):
            gw_ref[l] = lax.dot_general(act, dm_ref[l], (((0,), (0,)), ((), ())), precision=HI,
                                        preferred_element_type=F32)
        gb_ref[...] = jnp.sum(da_ref[...], axis=0, keepdims=True)

    return pl.pallas_call(
        body, name=name,
        out_shape=(jax.ShapeDtypeStruct((nl, d, n), F32), jax.ShapeDtypeStruct((1, dmod_all.shape[1]), F32)),
        compiler_params=_cp())(c_all, dmod_chip, dmod_all)


def _norm_proj(x, g, scale, shift, w, f32_cols, name):
    b, s, d = x.shape
    n = w.shape[1]
    tm = min(s, ROW_TILE)

    def body(x_ref, g_ref, sc_ref, sh_ref, w_ref, proj_ref, h_ref, *aux_ref):
        xv = x_ref[...]
        rstd = lax.rsqrt(jnp.mean(xv * xv, axis=-1, keepdims=True) + EPS)
        h = (xv * rstd) * g_ref[...] * (1.0 + sc_ref[...]) + sh_ref[...]
        hb = h.astype(BF16)
        h_ref[...] = hb
        proj = _dot(hb, w_ref[...])
        proj_ref[...] = proj.astype(BF16)
        if f32_cols:
            aux_ref[0][...] = proj[:, n - f32_cols:]

    row = lambda i, j: (i, j, 0)
    out_shape = [jax.ShapeDtypeStruct((b, s, n), BF16), jax.ShapeDtypeStruct((b, s, d), BF16)]
    out_specs = [pl.BlockSpec((None, tm, n), row), pl.BlockSpec((None, tm, d), row)]
    if f32_cols:
        out_shape.append(jax.ShapeDtypeStruct((b, s, f32_cols), F32))
        out_specs.append(pl.BlockSpec((None, tm, f32_cols), row))
    return pl.pallas_call(
        body, name=name, grid=(b, s // tm),
        out_shape=tuple(out_shape),
        in_specs=[pl.BlockSpec((None, tm, d), row),
                  pl.BlockSpec((1, d), lambda i, j: (0, 0)),
                  pl.BlockSpec((None, 1, d), lambda i, j: (i, 0, 0)),
                  pl.BlockSpec((None, 1, d), lambda i, j: (i, 0, 0)),
                  pl.BlockSpec((d, n), lambda i, j: (0, 0))],
        out_specs=tuple(out_specs),
        compiler_params=_cp(("parallel", "parallel")),
    )(x, g, scale, shift, w)


def _cat_refs(refs):
    vals = [r[...] for r in refs]
    return vals[0] if len(vals) == 1 else jnp.concatenate(vals, axis=-1)


def _gate_outproj(mix_parts, proj, gate_blk, w_out, x, gmod, name):
    b, s, _ = x.shape
    wd, d = w_out.shape
    tm = min(s, ROW_TILE)
    npart = len(mix_parts)

    def body(*refs):
        mix_refs = refs[:npart]
        gate_ref, w_ref, x_ref, gm_ref, xo_ref, o_ref = refs[npart:]
        gt = gate_ref[...].astype(F32)
        y = (_cat_refs(mix_refs) * (gt * _sigmoid(gt))).astype(BF16)
        o = _dot(y, w_ref[...])
        o_ref[...] = o.astype(BF16)
        xo_ref[...] = x_ref[...] + gm_ref[...] * o

    return pl.pallas_call(
        body, name=name, grid=(b, s // tm),
        out_shape=(jax.ShapeDtypeStruct((b, s, d), F32), jax.ShapeDtypeStruct((b, s, d), BF16)),
        in_specs=[pl.BlockSpec((None, tm, p.shape[2]), lambda i, j: (i, j, 0)) for p in mix_parts] + [
                  pl.BlockSpec((None, tm, wd), lambda i, j: (i, j, gate_blk)),
                  pl.BlockSpec((wd, d), lambda i, j: (0, 0)),
                  pl.BlockSpec((None, tm, d), lambda i, j: (i, j, 0)),
                  pl.BlockSpec((None, 1, d), lambda i, j: (i, 0, 0))],
        out_specs=(pl.BlockSpec((None, tm, d), lambda i, j: (i, j, 0)),
                   pl.BlockSpec((None, tm, d), lambda i, j: (i, j, 0))),
        compiler_params=_cp(("parallel", "parallel")),
    )(*mix_parts, proj, w_out, x, gmod)


def _last_exit_loss(mix, proj, gate_blk, w_out, x, gmod, g, target, name):
    b, s, d = x.shape
    wd = w_out.shape[0]
    tm = min(s, ROW_TILE // 2)

    def body(mix_ref, gate_ref, w_ref, x_ref, gm_ref, g_ref, t_ref, o_ref, loss_ref, dx_ref, dg_ref):
        first = jnp.logical_and(pl.program_id(0) == 0, pl.program_id(1) == 0)

        @pl.when(first)
        def _():
            loss_ref[...] = jnp.zeros_like(loss_ref)
            dg_ref[...] = jnp.zeros_like(dg_ref)

        gt = gate_ref[...].astype(F32)
        o = _dot((mix_ref[...] * (gt * _sigmoid(gt))).astype(BF16), w_ref[...])
        o_ref[...] = o.astype(BF16)
        xv = x_ref[...] + gm_ref[...] * o
        gv = g_ref[...]
        rstd = lax.rsqrt(jnp.mean(xv * xv, axis=-1, keepdims=True) + EPS)
        xhat = xv * rstd
        err = xhat * gv - t_ref[...]
        row = jnp.mean(err * err, axis=-1, keepdims=True)
        loss_ref[...] += 0.5 * jnp.sum(row, axis=0, keepdims=True)
        dy = err * (1.0 / d)
        dg_ref[...] += jnp.sum(dy * xhat, axis=0, keepdims=True)
        dxh = dy * gv
        dx_ref[...] = rstd * (dxh - xhat * jnp.mean(dxh * xhat, axis=-1, keepdims=True))

    row = lambda i, j: (i, j, 0)
    return pl.pallas_call(
        body, name=name, grid=(b, s // tm),
        out_shape=(jax.ShapeDtypeStruct((b, s, d), BF16), jax.ShapeDtypeStruct((1, LANE), F32),
                   jax.ShapeDtypeStruct((b, s, d), F32), jax.ShapeDtypeStruct((1, d), F32)),
        in_specs=[pl.BlockSpec((None, tm, wd), row),
                  pl.BlockSpec((None, tm, wd), lambda i, j: (i, j, gate_blk)),
                  pl.BlockSpec((wd, d), lambda i, j: (0, 0)),
                  pl.BlockSpec((None, tm, d), row),
                  pl.BlockSpec((None, 1, d), lambda i, j: (i, 0, 0)),
                  pl.BlockSpec((1, d), lambda i, j: (0, 0)),
                  pl.BlockSpec((None, tm, d), row)],
        out_specs=(pl.BlockSpec((None, tm, d), row),
                   pl.BlockSpec((1, LANE), lambda i, j: (0, 0)),
                   pl.BlockSpec((None, tm, d), row),
                   pl.BlockSpec((1, d), lambda i, j: (0, 0))),
        compiler_params=_cp(("arbitrary", "arbitrary")),
    )(mix, proj, w_out, x, gmod, g, target)


def _bwd_out(dxo, gmod, o, mix_parts, proj, gate_blk, w_out, dmix_dtype, name):
    b, s, d = dxo.shape
    wd = w_out.shape[0]
    tm = min(s, ROW_TILE)
    npart = len(mix_parts)

    def body(dx_ref, gm_ref, o_ref, *refs):
        mix_refs = refs[:npart]
        gate_ref, wt_ref, dmix_ref, dgate_ref, do_ref, y_ref, dgm_ref = refs[npart:]

        @pl.when(pl.program_id(1) == 0)
        def _():
            dgm_ref[...] = jnp.zeros_like(dgm_ref)

        dx = dx_ref[...]
        dgm_ref[...] += jnp.sum(dx * o_ref[...].astype(F32), axis=0, keepdims=True)
        dob = (gm_ref[...] * dx).astype(BF16)
        do_ref[...] = dob
        dy = _dot_nt(dob, wt_ref[...])
        gt = gate_ref[...].astype(F32)
        sg = _sigmoid(gt)
        silu = gt * sg
        mx = _cat_refs(mix_refs)
        y_ref[...] = (mx * silu).astype(BF16)
        dmix_ref[...] = (dy * silu).astype(dmix_dtype)
        dgate_ref[...] = (dy * mx * (sg * (1.0 + gt * (1.0 - sg)))).astype(BF16)

    row = lambda i, j: (i, j, 0)
    return pl.pallas_call(
        body, name=name, grid=(b, s // tm),
        out_shape=(jax.ShapeDtypeStruct((b, s, wd), dmix_dtype), jax.ShapeDtypeStruct((b, s, wd), BF16),
                   jax.ShapeDtypeStruct((b, s, d), BF16), jax.ShapeDtypeStruct((b, s, wd), BF16),
                   jax.ShapeDtypeStruct((b, 1, d), F32)),
        in_specs=[pl.BlockSpec((None, tm, d), row),
                  pl.BlockSpec((None, 1, d), lambda i, j: (i, 0, 0)),
                  pl.BlockSpec((None, tm, d), row)] + [
                  pl.BlockSpec((None, tm, p.shape[2]), row) for p in mix_parts] + [
                  pl.BlockSpec((None, tm, wd), lambda i, j: (i, j, gate_blk)),
                  pl.BlockSpec((wd, d), lambda i, j: (0, 0))],
        out_specs=(pl.BlockSpec((None, tm, wd), row), pl.BlockSpec((None, tm, wd), row),
                   pl.BlockSpec((None, tm, d), row), pl.BlockSpec((None, tm, wd), row),
                   pl.BlockSpec((None, 1, d), lambda i, j: (i, 0, 0))),
        compiler_params=_cp(("parallel", "arbitrary")),
    )(dxo, gmod, o, *mix_parts, proj, w_out)


def _bwd_in(dproj_parts, w_in, x, g, scale, dxo, name):
    b, s, d = x.shape
    n = w_in.shape[1]
    tm = min(s, ROW_TILE)
    npart = len(dproj_parts)

    def body(*refs):
        dp_refs = refs[:npart]
        wt_ref, x_ref, g_ref, sc_ref, dxo_ref, dx_ref, dsh_ref, dsc_ref, dg_ref = refs[npart:]

        @pl.when(jnp.logical_and(pl.program_id(0) == 0, pl.program_id(1) == 0))
        def _():
            dg_ref[...] = jnp.zeros_like(dg_ref)

        @pl.when(pl.program_id(1) == 0)
        def _():
            dsh_ref[...] = jnp.zeros_like(dsh_ref)
            dsc_ref[...] = jnp.zeros_like(dsc_ref)

        dh = _dot_nt(_cat_refs(dp_refs), wt_ref[...])
        xv = x_ref[...]
        gv = g_ref[...]
        one_sc = 1.0 + sc_ref[...]
        rstd = lax.rsqrt(jnp.mean(xv * xv, axis=-1, keepdims=True) + EPS)
        xhat = xv * rstd
        dsh_ref[...] += jnp.sum(dh, axis=0, keepdims=True)
        dsc_ref[...] += jnp.sum(dh * (xhat * gv), axis=0, keepdims=True)
        dhs = dh * one_sc
        dg_ref[...] += jnp.sum(dhs * xhat, axis=0, keepdims=True)
        dxh = dhs * gv
        dx_ref[...] = dxo_ref[...] + rstd * (dxh - xhat * jnp.mean(dxh * xhat, axis=-1, keepdims=True))

    row = lambda i, j: (i, j, 0)
    per_b = lambda i, j: (i, 0, 0)
    return pl.pallas_call(
        body, name=name, grid=(b, s // tm),
        out_shape=(jax.ShapeDtypeStruct((b, s, d), F32), jax.ShapeDtypeStruct((b, 1, d), F32),
                   jax.ShapeDtypeStruct((b, 1, d), F32), jax.ShapeDtypeStruct((1, d), F32)),
        in_specs=[pl.BlockSpec((None, tm, p.shape[2]), row) for p in dproj_parts] + [
                  pl.BlockSpec((d, n), lambda i, j: (0, 0)),
                  pl.BlockSpec((None, tm, d), row),
                  pl.BlockSpec((1, d), lambda i, j: (0, 0)),
                  pl.BlockSpec((None, 1, d), per_b),
                  pl.BlockSpec((None, tm, d), row)],
        out_specs=(pl.BlockSpec((None, tm, d), row), pl.BlockSpec((None, 1, d), per_b),
                   pl.BlockSpec((None, 1, d), per_b), pl.BlockSpec((1, d), lambda i, j: (0, 0))),
        compiler_params=_cp(("arbitrary", "arbitrary")),
    )(*dproj_parts, w_in, x, g, scale, dxo)


def _matmul_tn(a, b_parts, name):
    bsz, s, m = a.shape
    n = sum(p.shape[2] for p in b_parts)
    tk = next(c for c in (ROW_TILE, 512, 256, 128) if s % c == 0)
    npart = len(b_parts)

    def body(a_ref, *refs):
        b_refs, o_ref = refs[:npart], refs[npart]

        @pl.when(jnp.logical_and(pl.program_id(0) == 0, pl.program_id(1) == 0))
        def _():
            o_ref[...] = jnp.zeros_like(o_ref)

        o_ref[...] += _dot_tn(a_ref[...], _cat_refs(b_refs))

    row = lambda i, k: (i, k, 0)
    return pl.pallas_call(
        body, name=name, grid=(bsz, s // tk),
        out_shape=jax.ShapeDtypeStruct((m, n), F32),
        in_specs=[pl.BlockSpec((None, tk, m), row)] + [pl.BlockSpec((None, tk, p.shape[2]), row) for p in b_parts],
        out_specs=pl.BlockSpec((m, n), lambda i, k: (0, 0)),
        compiler_params=_cp(("arbitrary", "arbitrary")),
    )(a, *b_parts)


def _rel_buckets():
    qi = np.arange(BLK)[:, None]
    kj = np.arange(2 * BLK)[None, :]
    rel = qi - kj + BLK
    n = np.maximum(rel, 0)
    nf = np.maximum(n, 1).astype(np.float32)
    large = REL_MAX_EXACT + (np.log(nf / REL_MAX_EXACT) / math.log(REL_MAX_DIST / REL_MAX_EXACT)
                             * (REL_BUCKETS - REL_MAX_EXACT)).astype(np.int32)
    large = np.minimum(large, REL_BUCKETS - 1)
    bucket = np.where(n < REL_MAX_EXACT, n, large).astype(np.int32)
    valid = ((rel >= 0) & (rel < BLK)).astype(np.int32)
    return bucket, valid


def _swa_bias(rel_bias_t, bucket, valid, name):
    nh = rel_bias_t.shape[0]

    def body(rb_ref, bk_ref, vl_ref, o_ref):
        h = pl.program_id(0)
        bk = bk_ref[...]
        acc = jnp.zeros(bk.shape, F32)
        for i in range(REL_BUCKETS):
            acc = jnp.where(bk == i, rb_ref[h, i], acc)
        o_ref[...] = jnp.where(vl_ref[...] > 0, acc, NEG)

    return pl.pallas_call(
        body, name=name, grid=(nh,),
        out_shape=jax.ShapeDtypeStruct((nh, BLK, 2 * BLK), F32),
        in_specs=[pl.BlockSpec(memory_space=pltpu.SMEM),
                  pl.BlockSpec((BLK, 2 * BLK), lambda h: (0, 0)),
                  pl.BlockSpec((BLK, 2 * BLK), lambda h: (0, 0))],
        out_specs=pl.BlockSpec((None, BLK, 2 * BLK), lambda h: (h, 0, 0)),
        compiler_params=_cp(("arbitrary",)),
    )(rel_bias_t, bucket, valid)


def _swa_scores(n, q, kw, bias_ref):
    sc = _dot_nt(q, kw) * SCALE + bias_ref[...]
    second = lax.broadcasted_iota(jnp.int32, sc.shape, 1) >= BLK
    return jnp.where(jnp.logical_or(n > 0, second), sc, NEG)


def _pad_front(dst_ref, src_ref):
    dst_ref[0:BLK, :] = jnp.zeros((BLK, dst_ref.shape[1]), dst_ref.dtype)
    dst_ref[BLK:, :] = src_ref[...]


def _swa_fwd(q, k, v, bias, sinks, name):
    b, hkv, nb, rows, hd = q.shape
    wide = bias.shape[2]
    stride = wide - BLK
    s = nb * stride

    def body(q_ref, k_ref, v_ref, bias_ref, sink_ref, o_ref, l_ref, kpad_ref, vpad_ref):
        _pad_front(kpad_ref, k_ref)
        _pad_front(vpad_ref, v_ref)
        sink = sink_ref[...]

        def step(n, carry):
            w0 = pl.multiple_of(n * stride, BLK)
            sc = _swa_scores(n, q_ref[n], kpad_ref[pl.ds(w0, wide), :], bias_ref)
            m = jnp.maximum(jnp.max(sc, axis=1, keepdims=True), sink)
            e = jnp.exp(sc - m)
            den = jnp.sum(e, axis=1, keepdims=True) + jnp.exp(sink - m)
            o_ref[n] = _dot((e * (1.0 / den)).astype(BF16), vpad_ref[pl.ds(w0, wide), :]).astype(BF16)
            l_ref[n] = m + jnp.log(den)
            return carry

        lax.fori_loop(0, nb, step, 0)

    qspec = pl.BlockSpec((None, None, nb, rows, hd), lambda i, kv: (i, kv, 0, 0, 0))
    kspec = pl.BlockSpec((None, None, s, hd), lambda i, kv: (i, kv, 0, 0))
    return pl.pallas_call(
        body, name=name, grid=(b, hkv),
        out_shape=(jax.ShapeDtypeStruct((b, hkv, nb, rows, hd), BF16), jax.ShapeDtypeStruct((b, hkv, nb, rows, 1), F32)),
        in_specs=[qspec, kspec, kspec,
                  pl.BlockSpec((None, rows, wide), lambda i, kv: (kv, 0, 0)),
                  pl.BlockSpec((None, rows, 1), lambda i, kv: (kv, 0, 0))],
        out_specs=(qspec, pl.BlockSpec((None, None, nb, rows, 1), lambda i, kv: (i, kv, 0, 0, 0))),
        scratch_shapes=[pltpu.VMEM((s + BLK, hd), BF16), pltpu.VMEM((s + BLK, hd), BF16)],
        compiler_params=_cp(("parallel", "parallel")),
    )(q, k, v, bias, sinks)


def _swa_bwd(q, k, v, bias, sinks, do, lse, name):
    b, hkv, nb, rows, hd = q.shape
    wide = bias.shape[2]
    stride = wide - BLK
    s = nb * stride

    def body(q_ref, k_ref, v_ref, bias_ref, sink_ref, do_ref, l_ref,
             dq_ref, dk_ref, dv_ref, db_ref, dsk_ref, kpad_ref, vpad_ref, dkpad_ref, dvpad_ref):
        _pad_front(kpad_ref, k_ref)
        _pad_front(vpad_ref, v_ref)
        dkpad_ref[...] = jnp.zeros_like(dkpad_ref)
        dvpad_ref[...] = jnp.zeros_like(dvpad_ref)
        db_ref[...] = jnp.zeros_like(db_ref)
        sink = sink_ref[...]

        def step(n, dsink):
            w0 = pl.multiple_of(n * stride, BLK)
            win = pl.ds(w0, wide)
            qn = q_ref[n]
            kw = kpad_ref[win, :]
            ln = l_ref[n]
            p = jnp.exp(_swa_scores(n, qn, kw, bias_ref) - ln)
            dob = do_ref[n]
            dp = _dot_nt(dob, vpad_ref[win, :])
            delta = jnp.sum(p * dp, axis=1, keepdims=True)
            ds = p * (dp - delta)
            db_ref[...] += ds
            dsb = ds.astype(BF16)
            dq_ref[n] = (_dot(dsb, kw) * SCALE).astype(BF16)
            dkpad_ref[win, :] += _dot_tn(dsb, qn)
            dvpad_ref[win, :] += _dot_tn(p.astype(BF16), dob)
            return dsink - jnp.exp(sink - ln) * delta

        dsink = lax.fori_loop(0, nb, step, jnp.zeros((rows, 1), F32))
        for g in range(A_GROUP):
            tot = jnp.zeros((1, 1), F32)
            for blk in range(rows // (A_GROUP * BLK)):
                r0 = (blk * A_GROUP + g) * BLK
                tot = tot + jnp.sum(dsink[r0:r0 + BLK, :], axis=0, keepdims=True)
            dsk_ref[g] = jnp.broadcast_to(tot, (1, LANE))
        dk_ref[...] = (dkpad_ref[BLK:, :] * SCALE).astype(BF16)
        dv_ref[...] = dvpad_ref[BLK:, :].astype(BF16)

    qspec = pl.BlockSpec((None, None, nb, rows, hd), lambda i, kv: (i, kv, 0, 0, 0))
    kspec = pl.BlockSpec((None, None, s, hd), lambda i, kv: (i, kv, 0, 0))
    return pl.pallas_call(
        body, name=name, grid=(b, hkv),
        out_shape=(jax.ShapeDtypeStruct((b, hkv, nb, rows, hd), BF16), jax.ShapeDtypeStruct((b, hkv, s, hd), BF16),
                   jax.ShapeDtypeStruct((b, hkv, s, hd), BF16), jax.ShapeDtypeStruct((b, hkv, rows, wide), F32),
                   jax.ShapeDtypeStruct((b, hkv, A_GROUP, 1, LANE), F32)),
        in_specs=[qspec, kspec, kspec,
                  pl.BlockSpec((None, rows, wide), lambda i, kv: (kv, 0, 0)),
                  pl.BlockSpec((None, rows, 1), lambda i, kv: (kv, 0, 0)),
                  qspec,
                  pl.BlockSpec((None, None, nb, rows, 1), lambda i, kv: (i, kv, 0, 0, 0))],
        out_specs=(qspec, kspec, kspec,
                   pl.BlockSpec((None, None, rows, wide), lambda i, kv: (i, kv, 0, 0)),
                   pl.BlockSpec((None, None, A_GROUP, 1, LANE), lambda i, kv: (i, kv, 0, 0, 0))),
        scratch_shapes=[pltpu.VMEM((s + BLK, hd), BF16), pltpu.VMEM((s + BLK, hd), BF16),
                        pltpu.VMEM((s + BLK, hd), F32), pltpu.VMEM((s + BLK, hd), F32)],
        compiler_params=_cp(("parallel", "parallel")),
    )(q, k, v, bias, sinks, do, lse)


def _swa_small_grads(db, dsk, bucket, name):
    b, nh = db.shape[0], db.shape[1]

    def body(db_ref, dsk_ref, bk_ref, gb_ref, gs_ref):
        acc = db_ref[0]
        sk = dsk_ref[0]
        for i in range(1, b):
            acc = acc + db_ref[i]
            sk = sk + dsk_ref[i]
        gs_ref[...] = sk
        bk = bk_ref[...]
        for i in range(REL_BUCKETS):
            part = jnp.sum(jnp.where(bk == i, acc, 0.0), axis=1, keepdims=True)
            tot = jnp.sum(part, axis=0, keepdims=True)
            gb_ref[i:i + 1, :] = jnp.broadcast_to(tot, (1, LANE))

    return pl.pallas_call(
        body, name=name, grid=(nh,),
        out_shape=(jax.ShapeDtypeStruct((nh, REL_BUCKETS, LANE), F32), jax.ShapeDtypeStruct((nh, 1, LANE), F32)),
        in_specs=[pl.BlockSpec((b, None, BLK, 2 * BLK), lambda h: (0, h, 0, 0)),
                  pl.BlockSpec((b, None, 1, LANE), lambda h: (0, h, 0, 0)),
                  pl.BlockSpec((BLK, 2 * BLK), lambda h: (0, 0))],
        out_specs=(pl.BlockSpec((None, REL_BUCKETS, LANE), lambda h: (h, 0, 0)),
                   pl.BlockSpec((None, 1, LANE), lambda h: (h, 0, 0))),
        compiler_params=_cp(("parallel",)),
    )(db, dsk, bucket)


def _log_sigmoid(z):
    return jnp.minimum(z, 0.0) - jnp.log(1.0 + jnp.exp(-jnp.abs(z)))


def _fox_decay(z, bf, name):
    b, s, w = z.shape
    nb = s // BLK

    def body(z_ref, bf_ref, f_ref):
        r = lax.broadcasted_iota(jnp.int32, (BLK, BLK), 0)
        c = lax.broadcasted_iota(jnp.int32, (BLK, BLK), 1)
        tri = (c <= r).astype(F32)

        def step(n, carry):
            r0 = pl.multiple_of(n * BLK, BLK)
            lf = _log_sigmoid(z_ref[pl.ds(r0, BLK), :] + bf_ref[...])
            f_ref[pl.ds(r0, BLK), :] = jnp.dot(tri, lf, precision=HI, preferred_element_type=F32) + carry
            return carry + jnp.sum(lf, axis=0, keepdims=True)

        lax.fori_loop(0, nb, step, jnp.zeros((1, w), F32))

    spec = pl.BlockSpec((None, s, w), lambda i: (i, 0, 0))
    return pl.pallas_call(
        body, name=name, grid=(b,), out_shape=jax.ShapeDtypeStruct((b, s, w), F32),
        in_specs=[spec, pl.BlockSpec((1, w), lambda i: (0, 0))], out_specs=spec,
        compiler_params=_cp(("parallel",)),
    )(z, bf)


def _fox_dgate(df, z, bf, nheads, name):
    b, s, w = z.shape
    nb = s // BLK

    def body(df_ref, z_ref, bf_ref, dz_ref, dbf_ref):
        @pl.when(pl.program_id(0) == 0)
        def _():
            dbf_ref[...] = jnp.zeros_like(dbf_ref)

        r = lax.broadcasted_iota(jnp.int32, (BLK, BLK), 0)
        c = lax.broadcasted_iota(jnp.int32, (BLK, BLK), 1)
        tri = (c >= r).astype(F32)
        lane = lax.broadcasted_iota(jnp.int32, (BLK, w), 1)

        def step(i, carry):
            tail, dbf = carry
            r0 = pl.multiple_of((nb - 1 - i) * BLK, BLK)
            dfb = df_ref[pl.ds(r0, BLK), :]
            dlf = jnp.dot(tri, dfb, precision=HI, preferred_element_type=F32) + tail
            dz = jnp.where(lane < nheads, dlf * _sigmoid(-(z_ref[pl.ds(r0, BLK), :] + bf_ref[...])), 0.0)
            dz_ref[pl.ds(r0, BLK), :] = dz
            return tail + jnp.sum(dfb, axis=0, keepdims=True), dbf + jnp.sum(dz, axis=0, keepdims=True)

        zero = jnp.zeros((1, w), F32)
        _, dbf = lax.fori_loop(0, nb, step, (zero, zero))
        dbf_ref[...] += dbf

    spec = pl.BlockSpec((None, s, w), lambda i: (i, 0, 0))
    one = pl.BlockSpec((1, w), lambda i: (0, 0))
    return pl.pallas_call(
        body, name=name, grid=(b,),
        out_shape=(jax.ShapeDtypeStruct((b, s, w), F32), jax.ShapeDtypeStruct((1, w), F32)),
        in_specs=[spec, spec, one], out_specs=(spec, one),
        compiler_params=_cp(("arbitrary",)),
    )(df, z, bf)


def _fox_segments(nb):
    per = max(1, nb // 4)
    return per, nb // per


def _head_masks(shape, axis):
    idx = lax.broadcasted_iota(jnp.int32, shape, axis)
    return idx < HEAD_DIM, idx >= HEAD_DIM


def _fox_fwd(proj, qblk, kblk, vblk, fcol, frow, name):
    b, s, _ = proj.shape
    nh = frow.shape[1]
    npair = nh // 2
    BLK = min(s, FOX_BQ)
    assert s % BLK == 0
    per, nseg = _fox_segments(s // BLK)

    def body(q_ref, k_ref, v_ref, fc_ref, fr_ref, o_ref, l_ref, qm_ref, kt_ref, vb_ref):
        lo, hi = _head_masks((s, LANE), 1)
        qv = q_ref[...].astype(F32) * SCALE
        qm_ref[0] = jnp.where(lo, qv, 0.0).astype(BF16)
        qm_ref[1] = jnp.where(hi, qv, 0.0).astype(BF16)
        kt_ref[...] = k_ref[...].astype(F32).T.astype(BF16)
        vb_ref[...] = v_ref[...].astype(BF16)
        lane_lo = lax.broadcasted_iota(jnp.int32, (BLK, LANE), 1) < HEAD_DIM
        tail = per * BLK
        causal = (lax.broadcasted_iota(jnp.int32, (BLK, tail), 1)
                  - lax.broadcasted_iota(jnp.int32, (BLK, tail), 0))
        for seg in range(nseg):
            w = (seg + 1) * tail

            def qstep(n, carry):
                r0 = pl.multiple_of(n * BLK, BLK)
                outs = []
                for hh in range(2):
                    sc = _dot(qm_ref[hh, pl.ds(r0, BLK), :], kt_ref[:, :w])
                    sc = sc + (fc_ref[hh, pl.ds(r0, BLK), :] - fr_ref[hh, :, :w])
                    masked = jnp.where(causal <= (n - seg * per) * BLK, sc[:, w - tail:], NEG)
                    sc = masked if seg == 0 else jnp.concatenate([sc[:, :w - tail], masked], axis=1)
                    m = jnp.max(sc, axis=1, keepdims=True)
                    e = jnp.exp(sc - m)
                    l = jnp.sum(e, axis=1, keepdims=True)
                    outs.append(_dot((e * (1.0 / l)).astype(BF16), vb_ref[:w, :]))
                    l_ref[hh, pl.ds(r0, BLK), :] = m + jnp.log(l)
                o_ref[pl.ds(r0, BLK), :] = jnp.where(lane_lo, outs[0], outs[1]).astype(BF16)
                return carry

            lax.fori_loop(seg * per, (seg + 1) * per, qstep, 0)

    def tok(blk):
        return pl.BlockSpec((None, s, LANE), lambda i, p: (i, 0, blk + p))

    col = pl.BlockSpec((None, 2, s, 1), lambda i, p: (i, p, 0, 0))
    rowspec = pl.BlockSpec((None, 2, 1, s), lambda i, p: (i, p, 0, 0))
    return pl.pallas_call(
        body, name=name, grid=(b, npair),
        out_shape=(jax.ShapeDtypeStruct((b, s, nh * HEAD_DIM), BF16), jax.ShapeDtypeStruct((b, nh, s, 1), F32)),
        in_specs=[tok(qblk), tok(kblk), tok(vblk), col, rowspec],
        out_specs=(pl.BlockSpec((None, s, LANE), lambda i, p: (i, 0, p)), col),
        scratch_shapes=[pltpu.VMEM((2, s, LANE), BF16), pltpu.VMEM((LANE, s), BF16), pltpu.VMEM((s, LANE), BF16)],
        compiler_params=_cp(("parallel", "parallel")),
    )(proj, proj, proj, fcol, frow)


def _fox_bwd(proj, qblk, kblk, vblk, dmix, doblk, fcol, frow, frowb, lse, lserowb, name):
    b, s, _ = proj.shape
    nh = frow.shape[1]
    npair = nh // 2
    BLK = min(s, FOX_BQ)
    assert s % BLK == 0
    nb = s // BLK
    per, nseg = _fox_segments(nb)

    def body(q_ref, k_ref, v_ref, do_ref, fc_ref, fr_ref, frb_ref, l_ref, lrb_ref,
             dq_ref, dk_ref, dv_ref, dfr_ref,
             qm_ref, dom_ref, kb_ref, vb_ref, kt_ref, vt_ref, qtm_ref, dotm_ref, dka_ref, dva_ref):
        lo, hi = _head_masks((s, LANE), 1)
        qv = q_ref[...].astype(F32) * SCALE
        dov = do_ref[...]
        for hh, msk in enumerate((lo, hi)):
            qm_ref[hh] = jnp.where(msk, qv, 0.0).astype(BF16)
            dom_ref[hh] = jnp.where(msk, dov, 0.0).astype(BF16)
        kv = k_ref[...].astype(F32)
        vv = v_ref[...].astype(F32)
        kb_ref[...] = kv.astype(BF16)
        vb_ref[...] = vv.astype(BF16)
        kt_ref[...] = kv.T.astype(BF16)
        vt_ref[...] = vv.T.astype(BF16)
        rlo, rhi = _head_masks((LANE, BLK), 0)

        def tstep(n, carry):
            r0 = pl.multiple_of(n * BLK, BLK)
            qt = (q_ref[pl.ds(r0, BLK), :].astype(F32) * SCALE).T
            dt = do_ref[pl.ds(r0, BLK), :].astype(F32).T
            for hh, msk in enumerate((rlo, rhi)):
                qtm_ref[hh, n] = jnp.where(msk, qt, 0.0).astype(BF16)
                dotm_ref[hh, n] = jnp.where(msk, dt, 0.0).astype(BF16)
            return carry

        lax.fori_loop(0, nb, tstep, 0)
        dka_ref[...] = jnp.zeros_like(dka_ref)
        dva_ref[...] = jnp.zeros_like(dva_ref)
        dfr_ref[...] = jnp.zeros_like(dfr_ref)
        lane_lo = lax.broadcasted_iota(jnp.int32, (BLK, LANE), 1) < HEAD_DIM
        tail = per * BLK
        causal = (lax.broadcasted_iota(jnp.int32, (BLK, tail), 1)
                  - lax.broadcasted_iota(jnp.int32, (BLK, tail), 0))
        causal_t = (lax.broadcasted_iota(jnp.int32, (tail, BLK), 0)
                    - lax.broadcasted_iota(jnp.int32, (tail, BLK), 1))
        for seg in range(nseg):
            w = (seg + 1) * tail

            def nstep(n, carry):
                r0 = pl.multiple_of(n * BLK, BLK)
                lim = (n - seg * per) * BLK
                dqs = []
                for hh in range(2):
                    qn = qm_ref[hh, pl.ds(r0, BLK), :]
                    don = dom_ref[hh, pl.ds(r0, BLK), :]
                    sc = _dot(qn, kt_ref[:, :w]) + ((fc_ref[hh, pl.ds(r0, BLK), :] - l_ref[hh, pl.ds(r0, BLK), :])
                                                   - fr_ref[hh, :, :w])
                    masked = jnp.where(causal <= lim, sc[:, w - tail:], NEG)
                    p = jnp.exp(masked if seg == 0 else jnp.concatenate([sc[:, :w - tail], masked], axis=1))
                    dp = _dot(don, vt_ref[:, :w])
                    ds = p * (dp - jnp.sum(p * dp, axis=1, keepdims=True))
                    dqs.append(_dot(ds.astype(BF16), kb_ref[:w, :]))
                    dfr_ref[hh, :, :w] -= jnp.sum(ds, axis=0, keepdims=True)
                    sct = _dot(kb_ref[:w, :], qtm_ref[hh, n]) + ((frb_ref[hh, n] - lrb_ref[hh, n]) - fc_ref[hh, :w, :])
                    masked_t = jnp.where(causal_t <= lim, sct[w - tail:, :], NEG)
                    pt = jnp.exp(masked_t if seg == 0 else jnp.concatenate([sct[:w - tail, :], masked_t], axis=0))
                    dpt = _dot(vb_ref[:w, :], dotm_ref[hh, n])
                    dst = pt * (dpt - jnp.sum(pt * dpt, axis=0, keepdims=True))
                    dka_ref[:w, :] += _dot(dst.astype(BF16), qn)
                    dva_ref[:w, :] += _dot(pt.astype(BF16), don)
                dq_ref[pl.ds(r0, BLK), :] = (jnp.where(lane_lo, dqs[0], dqs[1]) * SCALE).astype(BF16)
                return carry

            lax.fori_loop(seg * per, (seg + 1) * per, nstep, 0)
        dk_ref[...] = dka_ref[...].astype(BF16)
        dv_ref[...] = dva_ref[...].astype(BF16)

    def tok(blk):
        return pl.BlockSpec((None, s, LANE), lambda i, p: (i, 0, blk + p))

    col = pl.BlockSpec((None, 2, s, 1), lambda i, p: (i, p, 0, 0))
    rowspec = pl.BlockSpec((None, 2, 1, s), lambda i, p: (i, p, 0, 0))
    rowbspec = pl.BlockSpec((None, 2, nb, 1, BLK), lambda i, p: (i, p, 0, 0, 0))
    outtok = pl.BlockSpec((None, s, LANE), lambda i, p: (i, 0, p))
    shp = jax.ShapeDtypeStruct((b, s, nh * HEAD_DIM), BF16)
    return pl.pallas_call(
        body, name=name, grid=(b, npair),
        out_shape=(shp, shp, shp, jax.ShapeDtypeStruct((b, nh, 1, s), F32)),
        in_specs=[tok(qblk), tok(kblk), tok(vblk),
                  pl.BlockSpec((None, s, LANE), lambda i, p: (i, 0, doblk + p)),
                  col, rowspec, rowbspec, col, rowbspec],
        out_specs=(outtok, outtok, outtok, rowspec),
        scratch_shapes=[pltpu.VMEM((2, s, LANE), BF16), pltpu.VMEM((2, s, LANE), BF16),
                        pltpu.VMEM((s, LANE), BF16), pltpu.VMEM((s, LANE), BF16),
                        pltpu.VMEM((LANE, s), BF16), pltpu.VMEM((LANE, s), BF16),
                        pltpu.VMEM((2, nb, LANE, BLK), BF16), pltpu.VMEM((2, nb, LANE, BLK), BF16),
                        pltpu.VMEM((s, LANE), F32), pltpu.VMEM((s, LANE), F32)],
        compiler_params=_cp(("parallel", "parallel")),
    )(proj, proj, proj, dmix, fcol, frow, frowb, lse, lserowb)


def _expm1(x):
    poly = x * (1.0 + x * (1.0 / 2 + x * (1.0 / 6 + x * (1.0 / 24 + x * (1.0 / 120 + x * (1.0 / 720))))))
    return jnp.where(x > -0.1, poly, jnp.exp(x) - 1.0)


def _softplus(z):
    return jnp.maximum(z, 0.0) + jnp.log(1.0 + jnp.exp(-jnp.abs(z)))


def _scan_rows(a, u, carry, row, up):
    tc, c = a.shape
    d = 1
    while d < tc:
        if d < SUBLANE:
            keep = (row >= d) if up else (row < tc - d)
            shift = d if up else tc - d
            a_sh = jnp.where(keep, pltpu.roll(a, shift, 0), 1.0)
            u_sh = jnp.where(keep, pltpu.roll(u, shift, 0), 0.0)
        elif up:
            a_sh = jnp.concatenate([jnp.ones((d, c), F32), a[:tc - d]], axis=0)
            u_sh = jnp.concatenate([jnp.zeros((d, c), F32), u[:tc - d]], axis=0)
        else:
            a_sh = jnp.concatenate([a[d:], jnp.ones((d, c), F32)], axis=0)
            u_sh = jnp.concatenate([u[d:], jnp.zeros((d, c), F32)], axis=0)
        u = a * u_sh + u
        a = a * a_sh
        d *= 2
    return u + a * carry


def _scan_up(a, u, carry, row):
    return _scan_rows(a, u, carry, row, True)


def _scan_down(bnext, g, carry, row):
    return _scan_rows(bnext, g, carry, row, False)


def _pick_row(val, row, which):
    return jnp.sum(jnp.where(row == which, val, 0.0), axis=0, keepdims=True)


def _lru_gates(xpad_ref, t0, tc, cw_ref, cb_ref, wa, ba_ref, wx, bx_ref, sp):
    xw = xpad_ref[pl.ds(t0, tc + SUBLANE), :]
    xc = cb_ref[...]
    for j in range(CONV_WIDTH):
        sh = CONV_WIDTH - 1 - j
        xs = xw if sh == 0 else pltpu.roll(xw, sh, 0)
        xc = xc + xs[SUBLANE:, :] * cw_ref[j:j + 1, :]
    xcb = xc.astype(BF16)
    r = _sigmoid(_dot(xcb, wa) + ba_ref[...])
    i = _sigmoid(_dot(xcb, wx) + bx_ref[...])
    la = -LRU_C * r * sp
    return xc, r, i, la


def _lru_specs(s, cb):
    seq = lambda bi, ni: (bi, 0, ni)
    return dict(
        seq=pl.BlockSpec((None, s, cb), seq),
        cw=pl.BlockSpec((CONV_WIDTH, cb), lambda bi, ni: (0, ni)),
        vec=pl.BlockSpec((1, cb), lambda bi, ni: (0, ni)),
        wblk=pl.BlockSpec((None, cb, cb), lambda bi, ni: (ni, 0, 0)),
    )


def _lru_fwd(proj, cw, cb_, wa, ba, wx, bx, lam, name):
    b, s, _ = proj.shape
    nblk, cb, _ = wa.shape
    tc = min(s, SCAN_CHUNK)
    nc = s // tc

    def body(x_ref, cw_ref, cb_ref, wa_ref, ba_ref, wx_ref, bx_ref, lam_ref, hs_ref, xpad_ref):
        xpad_ref[0:SUBLANE, :] = jnp.zeros((SUBLANE, cb), F32)
        xpad_ref[SUBLANE:, :] = x_ref[...].astype(F32)
        wa_b = wa_ref[...].astype(BF16)
        wx_b = wx_ref[...].astype(BF16)
        sp = _softplus(-lam_ref[...])
        row = lax.broadcasted_iota(jnp.int32, (tc, cb), 0)

        def chunk(ci, carry):
            t0 = pl.multiple_of(ci * tc, tc)
            xc, r, i, la = _lru_gates(xpad_ref, t0, tc, cw_ref, cb_ref, wa_b, ba_ref, wx_b, bx_ref, sp)
            a = jnp.exp(la)
            u = jnp.sqrt(-_expm1(2.0 * la)) * (i * xc)
            h = _scan_up(a, u, carry, row)
            hs_ref[pl.ds(t0, tc), :] = h
            return _pick_row(h, row, tc - 1)

        lax.fori_loop(0, nc, chunk, jnp.zeros((1, cb), F32))

    sp_ = _lru_specs(s, cb)
    return pl.pallas_call(
        body, name=name, grid=(b, nblk),
        out_shape=jax.ShapeDtypeStruct((b, s, nblk * cb), F32),
        in_specs=[sp_["seq"], sp_["cw"], sp_["vec"], sp_["wblk"], sp_["vec"], sp_["wblk"], sp_["vec"], sp_["vec"]],
        out_specs=sp_["seq"],
        scratch_shapes=[pltpu.VMEM((s + SUBLANE, cb), F32)],
        compiler_params=_cp(("parallel", "parallel")),
    )(proj, cw, cb_, wa, ba, wx, bx, lam)


def _lru_bwd(proj, hs, dhs, cw, cb_, wa, ba, wx, bx, lam, name):
    b, s, _ = proj.shape
    nblk, cb, _ = wa.shape
    tc = min(s, SCAN_CHUNK)
    nc = s // tc

    def body(x_ref, hs_ref, dhs_ref, cw_ref, cb_ref, wa_ref, ba_ref, wx_ref, bx_ref, lam_ref,
             dx_ref, dcw_ref, dcb_ref, dwa_ref, dba_ref, dwx_ref, dbx_ref, dlam_ref,
             xpad_ref, hpad_ref, dcpad_ref, xc_ref, r_ref, i_ref, a_ref, mult_ref):
        @pl.when(pl.program_id(1) == 0)
        def _():
            for ref in (dcw_ref, dcb_ref, dwa_ref, dba_ref, dwx_ref, dbx_ref, dlam_ref):
                ref[...] = jnp.zeros_like(ref)

        zeros8 = jnp.zeros((SUBLANE, cb), F32)
        xpad_ref[0:SUBLANE, :] = zeros8
        xpad_ref[SUBLANE:, :] = x_ref[...].astype(F32)
        hpad_ref[0:SUBLANE, :] = zeros8
        hpad_ref[SUBLANE:, :] = hs_ref[...]
        dcpad_ref[s:s + SUBLANE, :] = zeros8
        wa_b = wa_ref[...].astype(BF16)
        wx_b = wx_ref[...].astype(BF16)
        lam_v = lam_ref[...]
        sp = _softplus(-lam_v)
        dsp_dlam = -_sigmoid(-lam_v)
        row = lax.broadcasted_iota(jnp.int32, (tc, cb), 0)

        def recompute(ci, carry):
            t0 = pl.multiple_of(ci * tc, tc)
            xc, r, i, la = _lru_gates(xpad_ref, t0, tc, cw_ref, cb_ref, wa_b, ba_ref, wx_b, bx_ref, sp)
            xc_ref[pl.ds(t0, tc), :] = xc
            r_ref[pl.ds(t0, tc), :] = r
            i_ref[pl.ds(t0, tc), :] = i
            a_ref[pl.ds(t0, tc), :] = jnp.exp(la)
            mult_ref[pl.ds(t0, tc), :] = jnp.sqrt(-_expm1(2.0 * la))
            return carry

        lax.fori_loop(0, nc, recompute, 0)

        def adjoint(k, carry):
            g_next, a_first_next = carry
            t0 = pl.multiple_of((nc - 1 - k) * tc, tc)
            a = a_ref[pl.ds(t0, tc), :]
            a_next = jnp.where(row == tc - 1, a_first_next, pltpu.roll(a, tc - 1, 0))
            gg = _scan_down(a_next, dhs_ref[pl.ds(t0, tc), :], g_next, row)
            h_prev = pltpu.roll(hpad_ref[pl.ds(t0, tc + SUBLANE), :], 1, 0)[SUBLANE:, :]
            xc = xc_ref[pl.ds(t0, tc), :]
            r = r_ref[pl.ds(t0, tc), :]
            i = i_ref[pl.ds(t0, tc), :]
            mult = mult_ref[pl.ds(t0, tc), :]
            d_mult = gg * i * xc
            d_i = gg * mult * xc
            d_xc = gg * mult * i
            d_la = gg * h_prev * a - d_mult * (a * a) / mult
            d_zr = (d_la * (-LRU_C * sp)) * r * (1.0 - r)
            d_zi = d_i * i * (1.0 - i)
            dlam_ref[...] += jnp.sum(d_la * (-LRU_C * r), axis=0, keepdims=True) * dsp_dlam
            dzr_b = d_zr.astype(BF16)
            dzi_b = d_zi.astype(BF16)
            xcb = xc.astype(BF16)
            d_xc = d_xc + _dot_nt(dzr_b, wa_b) + _dot_nt(dzi_b, wx_b)
            dwa_ref[...] += _dot_tn(xcb, dzr_b)
            dwx_ref[...] += _dot_tn(xcb, dzi_b)
            dba_ref[...] += jnp.sum(d_zr, axis=0, keepdims=True)
            dbx_ref[...] += jnp.sum(d_zi, axis=0, keepdims=True)
            dcb_ref[...] += jnp.sum(d_xc, axis=0, keepdims=True)
            dcpad_ref[pl.ds(t0, tc), :] = d_xc
            return _pick_row(gg, row, 0), _pick_row(a, row, 0)

        zero = jnp.zeros((1, cb), F32)
        lax.fori_loop(0, nc, adjoint, (zero, zero))

        def conv_back(ci, carry):
            t0 = pl.multiple_of(ci * tc, tc)
            dw = dcpad_ref[pl.ds(t0, tc + SUBLANE), :]
            xw = xpad_ref[pl.ds(t0, tc + SUBLANE), :]
            d_xc = dw[:tc, :]
            dxr = jnp.zeros((tc, cb), F32)
            for j in range(CONV_WIDTH):
                sh = CONV_WIDTH - 1 - j
                dsh = dw if sh == 0 else pltpu.roll(dw, tc + SUBLANE - sh, 0)
                dxr = dxr + dsh[:tc, :] * cw_ref[j:j + 1, :]
                xs = xw if sh == 0 else pltpu.roll(xw, sh, 0)
                dcw_ref[j:j + 1, :] += jnp.sum(d_xc * xs[SUBLANE:, :], axis=0, keepdims=True)
            dx_ref[pl.ds(t0, tc), :] = dxr.astype(BF16)
            return carry

        lax.fori_loop(0, nc, conv_back, 0)

    seq = lambda ni, bi: (bi, 0, ni)
    seqspec = pl.BlockSpec((None, s, cb), seq)
    cwspec = pl.BlockSpec((CONV_WIDTH, cb), lambda ni, bi: (0, ni))
    vec = pl.BlockSpec((1, cb), lambda ni, bi: (0, ni))
    wblk = pl.BlockSpec((None, cb, cb), lambda ni, bi: (ni, 0, 0))
    w = nblk * cb
    return pl.pallas_call(
        body, name=name, grid=(nblk, b),
        out_shape=(jax.ShapeDtypeStruct((b, s, w), BF16), jax.ShapeDtypeStruct((CONV_WIDTH, w), F32),
                   jax.ShapeDtypeStruct((1, w), F32), jax.ShapeDtypeStruct((nblk, cb, cb), F32),
                   jax.ShapeDtypeStruct((1, w), F32), jax.ShapeDtypeStruct((nblk, cb, cb), F32),
                   jax.ShapeDtypeStruct((1, w), F32), jax.ShapeDtypeStruct((1, w), F32)),
        in_specs=[seqspec, seqspec, seqspec, cwspec, vec, wblk, vec, wblk, vec, vec],
        out_specs=(seqspec, cwspec, vec, wblk, vec, wblk, vec, vec),
        scratch_shapes=[pltpu.VMEM((s + SUBLANE, cb), F32)] * 3 + [pltpu.VMEM((s, cb), F32)] * 5,
        compiler_params=_cp(("parallel", "arbitrary")),
    )(proj, hs, dhs, cw, cb_, wa, ba, wx, bx, lam)


def _adamw(w, g, m, v, name):
    shape = w.shape
    total = int(np.prod(shape))
    if w.ndim >= 2 and shape[-2] % SUBLANE == 0:
        rows, cols = shape[-2:]
    else:
        cols = 1024
        rows = -(-(-(-total // cols)) // SUBLANE) * SUBLANE
    lead = -(-total // (rows * cols))
    tr = _row_tile(rows, 512)
    pad = lead * rows * cols - total

    def flat(a):
        if pad:
            a = jnp.pad(a.reshape(-1), (0, pad))
        return a.reshape(lead, rows, cols)

    c1 = 1.0 - ADAM_B1 ** ADAM_STEP
    c2 = 1.0 - ADAM_B2 ** ADAM_STEP

    def body(w_ref, g_ref, m_ref, v_ref, d_ref, nm_ref, nv_ref):
        gv = g_ref[...]
        nm = ADAM_B1 * m_ref[...] + (1.0 - ADAM_B1) * gv
        nv = ADAM_B2 * v_ref[...] + (1.0 - ADAM_B2) * (gv * gv)
        nm_ref[...] = nm
        nv_ref[...] = nv
        d_ref[...] = -ADAM_LR * ((nm / c1) / (jnp.sqrt(nv / c2) + ADAM_EPS) + ADAM_WD * w_ref[...])

    spec = pl.BlockSpec((None, tr, cols), lambda l, i: (l, i, 0))
    shp = jax.ShapeDtypeStruct((lead, rows, cols), F32)
    outs = pl.pallas_call(
        body, name=name, grid=(lead, rows // tr), out_shape=(shp, shp, shp),
        in_specs=[spec] * 4, out_specs=(spec,) * 3,
        compiler_params=_cp(("parallel", "parallel")),
    )(flat(w), flat(g), flat(m), flat(v))
    if pad:
        return tuple(o.reshape(-1)[:total].reshape(shape) for o in outs)
    return tuple(o.reshape(shape) for o in outs)


def _to_heads(t, nh):
    b, s, _ = t.shape
    return t.reshape(b, s, nh, HEAD_DIM).transpose(0, 2, 1, 3)


def _stack_heads(t):
    b, s, _ = t.shape
    steps = s // (SWA_BPS * BLK)
    t = t.reshape(b, steps, SWA_BPS, BLK, A_KV_HEADS, A_GROUP, HEAD_DIM).transpose(0, 4, 1, 2, 5, 3, 6)
    return t.reshape(b, A_KV_HEADS, steps, SWA_BPS * A_GROUP * BLK, HEAD_DIM)


def _unstack_heads(t):
    b, hkv, steps, rows, hd = t.shape
    t = t.reshape(b, hkv, steps, SWA_BPS, A_GROUP, BLK, hd).transpose(0, 2, 3, 5, 1, 4, 6)
    return t.reshape(b, steps * SWA_BPS * BLK, hkv * A_GROUP * hd)


def _from_heads(t):
    b, nh, s, hd = t.shape
    return t.transpose(0, 2, 1, 3).reshape(b, s, nh * hd)


def _pad_rows(a, mult):
    r = a.shape[0]
    p = (-r) % mult
    return jnp.pad(a, ((0, p), (0, 0))) if p else a


def kernel(x, c, rel_bias, norm_g, ada_w, ada_b, attn_w_in, attn_sinks, attn_b_f, attn_w_out, lru_w_in, lru_conv_w, lru_conv_b, lru_w_a, lru_b_a, lru_w_x, lru_b_x, lru_lambda, lru_w_out, final_g, loss_target, m_rel_bias, m_norm_g, m_ada_w, m_ada_b, m_attn_w_in, m_attn_sinks, m_attn_b_f, m_attn_w_out, m_lru_w_in, m_lru_conv_w, m_lru_conv_b, m_lru_w_a, m_lru_b_a, m_lru_w_x, m_lru_b_x, m_lru_lambda, m_lru_w_out, m_final_g, v_rel_bias, v_norm_g, v_ada_w, v_ada_b, v_attn_w_in, v_attn_sinks, v_attn_b_f, v_attn_w_out, v_lru_w_in, v_lru_conv_w, v_lru_conv_b, v_lru_w_a, v_lru_b_a, v_lru_w_x, v_lru_b_x, v_lru_lambda, v_lru_w_out, v_final_g):
    bl, s, d = x.shape
    ix, iy, ic = lax.axis_index("x"), lax.axis_index("y"), lax.axis_index("c")
    chip = 2 * ix + iy
    me = 2 * chip + ic
    nb = s // BLK
    aw = A_Q_HEADS * HEAD_DIM
    akv = A_KV_HEADS * HEAD_DIM
    bw = B_HEADS * HEAD_DIM
    mixw = aw + bw
    qkv_w = aw + 2 * akv + 3 * bw
    n_in = attn_w_in.shape[2] * N_CHIP
    lw = lru_lambda.shape[1] * N_CHIP
    n0 = mixw + qkv_w + LANE

    rows_pad = -(-bl // SUBLANE) * SUBLANE
    vec_rows = jnp.concatenate([lru_conv_w[0], lru_conv_b, lru_b_a, lru_b_x, lru_lambda], axis=0)
    first = jnp.concatenate([_pad_rows(c, SUBLANE), jnp.pad(vec_rows, ((0, 0), (0, d - lw // N_CHIP)))], axis=0)
    first = _all_gather8(first, "gather_c", pltpu.VMEM).reshape(N_DEV, rows_pad + SUBLANE, d)
    c_all = first[:, :bl].reshape(N_DEV * bl, d)
    vec_all = first[:, rows_pad:, :lw // N_CHIP].reshape(N_CHIP, 2, SUBLANE, lw // N_CHIP)[:, 0]
    vec_all = vec_all.transpose(1, 0, 2).reshape(SUBLANE, lw)
    ncol = ada_w.shape[2]
    ada_w_l = lax.dynamic_index_in_dim(ada_w, ic, 0, keepdims=False)
    ada_b_l = lax.dynamic_slice(ada_b, (ic, chip * ncol), (1, ncol))
    mod_part = _ada_fwd(c_all, ada_w_l, ada_b_l, "ada_fwd")
    mod_all = _all_gather8(_pad_rows(mod_part, SUBLANE), "gather_mod", pltpu.VMEM)
    mrows = -(-(N_DEV * bl) // SUBLANE) * SUBLANE
    mod_all = mod_all.reshape(N_CHIP, 2, mrows, ncol)[:, :, :N_DEV * bl]
    mod_all = mod_all.transpose(1, 2, 0, 3).reshape(2, N_DEV * bl, N_CHIP * ncol)
    mod = lax.dynamic_slice_in_dim(mod_all, me * bl, bl, axis=1)
    shift = [mod[l, :, 0:d].reshape(bl, 1, d) for l in range(2)]
    scale = [mod[l, :, d:2 * d].reshape(bl, 1, d) for l in range(2)]
    gmod = [mod[l, :, 2 * d:3 * d].reshape(bl, 1, d) for l in range(2)]

    c_in0 = n_in // N_CHIP
    c_in1 = 2 * lw // N_CHIP
    assert c_in0 <= d and 2 * c_in1 == d
    r_in0, r_out0, r_in1, r_out1 = d // 2, mixw // N_CHIP // 2, d // 4, lw // N_CHIP // 2
    o_out0, o_in1, o_out1 = r_in0, r_in0 + r_out0, r_in0 + r_out0 + r_in1
    big_rows = o_out1 + r_out1

    def half_of(a, rows):
        return lax.dynamic_slice_in_dim(a, ic * rows, rows, axis=0)

    h_in1 = half_of(lru_w_in[0], r_in0).astype(BF16)
    my_half = jnp.concatenate([
        jnp.pad(half_of(attn_w_in[0], r_in0).astype(BF16), ((0, 0), (0, d - c_in0))),
        half_of(attn_w_out[0], r_out0).astype(BF16),
        jnp.concatenate([h_in1[:r_in1], h_in1[r_in1:]], axis=1),
        half_of(lru_w_out[0], r_out1).astype(BF16)], axis=0)
    gat = _all_gather8(my_half, "gather_weights", pltpu.HBM).reshape(N_CHIP, 2, big_rows, d)
    w_in0 = gat[:, :, :r_in0, :c_in0].transpose(1, 2, 0, 3).reshape(d, n_in)
    w_out0 = gat[:, :, o_out0:o_in1].reshape(mixw, d)
    w_in1 = gat[:, :, o_in1:o_out1].reshape(N_CHIP, 2, r_in1, 2, c_in1)
    w_in1 = w_in1.transpose(1, 3, 2, 0, 4).reshape(d, 2 * lw)
    w_out1 = gat[:, :, o_out1:].reshape(lw, d)
    w_cat0 = jnp.concatenate([w_in0[:, qkv_w + B_HEADS:], w_in0[:, :qkv_w + B_HEADS],
                              jnp.zeros((d, n0 - n_in), BF16)], axis=1)

    proj0, h0, zf = _norm_proj(x, norm_g[0:1], scale[0], shift[0], w_cat0, LANE, "norm_proj0")
    o_a = mixw
    aq = _stack_heads(proj0[:, :, o_a:o_a + aw].astype(BF16))
    ak = _to_heads(proj0[:, :, o_a + aw:o_a + aw + akv].astype(BF16), A_KV_HEADS)
    av = _to_heads(proj0[:, :, o_a + aw + akv:o_a + aw + 2 * akv].astype(BF16), A_KV_HEADS)
    o_b = o_a + aw + 2 * akv
    fox_blks = (o_b // LANE, (o_b + bw) // LANE, (o_b + 2 * bw) // LANE)
    bucket_np, valid_np = _rel_buckets()
    bucket = jnp.asarray(bucket_np)
    bias = _swa_bias(rel_bias.T, bucket, jnp.asarray(valid_np), "swa_bias")
    bias = bias.reshape(A_KV_HEADS, A_GROUP * BLK, 2 * BLK)
    bias = jnp.concatenate([jnp.pad(bias, ((0, 0), (0, 0), (blk * BLK, (SWA_BPS - 1 - blk) * BLK)),
                                    constant_values=NEG) for blk in range(SWA_BPS)], axis=1)
    sinks = jnp.repeat(attn_sinks[0].reshape(A_KV_HEADS, A_GROUP), BLK, axis=1).reshape(A_KV_HEADS, A_GROUP * BLK, 1)
    sinks = jnp.tile(sinks, (1, SWA_BPS, 1))
    a_out, a_lse = _swa_fwd(aq, ak, av, bias, sinks, "swa_fwd")
    bf_pad = jnp.pad(attn_b_f, ((0, 0), (0, LANE - B_HEADS)))
    fsum = _fox_decay(zf, bf_pad, "fox_decay")
    fh = fsum[:, :, :B_HEADS].transpose(0, 2, 1)
    fcol = fh.reshape(bl, B_HEADS, s, 1)
    frow = fh.reshape(bl, B_HEADS, 1, s)
    fbq = min(s, FOX_BQ)
    frowb = fh.reshape(bl, B_HEADS, s // fbq, 1, fbq)
    b_out, b_lse = _fox_fwd(proj0, *fox_blks, fcol, frow, "fox_fwd")
    lserowb = b_lse.reshape(bl, B_HEADS, s // fbq, 1, fbq)
    mix0 = [_unstack_heads(a_out), b_out]
    x1, o0 = _gate_outproj(mix0, proj0, 0, w_out0, x, gmod[0], "gate_outproj0")

    proj1, h1 = _norm_proj(x1, norm_g[1:2], scale[1], shift[1], w_in1, 0, "norm_proj1")
    cw_f, cb_f, ba_f, bx_f, lam_f = vec_all[0:4], vec_all[4:5], vec_all[5:6], vec_all[6:7], vec_all[7:8]
    hs = _lru_fwd(proj1, cw_f, cb_f, lru_w_a[0], ba_f, lru_w_x[0], bx_f, lam_f, "lru_fwd")
    o1, loss_vec, dx2, g_final = _last_exit_loss(hs, proj1, 1, w_out1, x1, gmod[1], final_g.reshape(1, d),
                                                 loss_target, "exit_loss")
    loss = lax.psum(loss_vec[0, 0], ("x", "y", "c"))

    dhs, dgate1, do1, y1, dgm1 = _bwd_out(dx2, gmod[1], o1, [hs], proj1, 1, w_out1, F32, "bwd_out1")
    g_w_out1 = _matmul_tn(y1, [do1], "grad_w_out1")
    (dxr, g_cw, g_cb, g_wa, g_ba, g_wx, g_bx, g_lam) = _lru_bwd(
        proj1, hs, dhs, cw_f, cb_f, lru_w_a[0], ba_f, lru_w_x[0], bx_f, lam_f, "lru_bwd")
    dproj1 = [dxr, dgate1]
    g_w_in1 = _matmul_tn(h1, dproj1, "grad_w_in1")
    dx1, dsh1, dsc1, g_ng1 = _bwd_in(dproj1, w_in1, x1, norm_g[1:2], scale[1], dx2, "bwd_in1")

    dmix0, dgate0, do0, y0, dgm0 = _bwd_out(dx1, gmod[0], o0, mix0, proj0, 0, w_out0, BF16, "bwd_out0")
    g_w_out0 = _matmul_tn(y0, [do0], "grad_w_out0")
    da_out = _stack_heads(dmix0[:, :, :aw].astype(BF16))
    daq, dak, dav, dbias, dsink = _swa_bwd(aq, ak, av, bias, sinks, da_out, a_lse, "swa_bwd")
    dbq, dbk, dbv, dfrow = _fox_bwd(proj0, *fox_blks, dmix0, aw // LANE, fcol, frow, frowb, b_lse, lserowb,
                                    "fox_bwd")
    df = dfrow.reshape(bl, B_HEADS, s).transpose(0, 2, 1)
    df = jnp.pad(df, ((0, 0), (0, 0), (0, LANE - B_HEADS)))
    dzf, g_bf = _fox_dgate(df, zf, bf_pad, B_HEADS, "fox_dgate")
    dproj0 = ([dgate0, _unstack_heads(daq), _from_heads(dak), _from_heads(dav)]
              + [dbq, dbk, dbv, dzf.astype(BF16)])
    g_w_cat0 = _matmul_tn(h0, dproj0, "grad_w_in0")
    g_w_in0 = jnp.concatenate([g_w_cat0[:, mixw:mixw + qkv_w + B_HEADS], g_w_cat0[:, :mixw]], axis=1)
    dx0, dsh0, dsc0, g_ng0 = _bwd_in(dproj0, w_cat0, x, norm_g[0:1], scale[0], dx1, "bwd_in0")
    dbias = sum(dbias[:, :, blk * A_GROUP * BLK:(blk + 1) * A_GROUP * BLK, blk * BLK:(blk + 2) * BLK]
                for blk in range(SWA_BPS))
    g_relb, g_sink = _swa_small_grads(dbias.reshape(bl, A_Q_HEADS, BLK, 2 * BLK),
                                      dsink.reshape(bl, A_Q_HEADS, 1, LANE), bucket, "swa_small_grads")

    dmod = jnp.concatenate([jnp.concatenate([dsh0, dsc0, dgm0], axis=-1),
                            jnp.concatenate([dsh1, dsc1, dgm1], axis=-1)], axis=1)
    dmod_rows = _pad_rows(dmod.reshape(bl * 6, d), SUBLANE)

    tail = jnp.concatenate([g_relb[:, :, 0].T.reshape(-1), g_sink[:, 0, 0], g_bf[0, :B_HEADS]])
    n_relb = REL_BUCKETS * A_Q_HEADS
    small_rows = [g_wa.reshape(-1, d), g_wx.reshape(-1, d), g_ng0, g_ng1, g_final, g_cw, g_cb, g_ba, g_bx, g_lam,
                  jnp.pad(tail, (0, d - tail.shape[0])).reshape(1, d)]
    small_counts = [r.shape[0] for r in small_rows]
    piece_rows = -(-(-(-sum(small_counts) // N_DEV)) // SUBLANE) * SUBLANE
    small_2d = jnp.concatenate(small_rows, axis=0)
    small_2d = jnp.pad(small_2d, ((0, N_DEV * piece_rows - small_2d.shape[0]), (0, 0)))
    small_pieces = small_2d.reshape(N_CHIP, 2, piece_rows, d)
    p_in0 = jnp.pad(g_w_in0.reshape(2, r_in0, N_CHIP, c_in0).transpose(2, 0, 1, 3),
                    ((0, 0), (0, 0), (0, 0), (0, d - c_in0)))
    p_in1 = g_w_in1.reshape(2, 2, r_in1, N_CHIP, c_in1).transpose(3, 0, 2, 1, 4).reshape(N_CHIP, 2, r_in1, d)
    pieces = jnp.concatenate([p_in0, g_w_out0.reshape(N_CHIP, 2, r_out0, d), p_in1,
                              g_w_out1.reshape(N_CHIP, 2, r_out1, d), small_pieces], axis=2)
    theirs = _sibling_push(pieces, True, "push_sibling_halves")
    partial = _pair_sum(jnp.reshape(ic, (1,)).astype(jnp.int32), pieces, theirs, "sum_chip")
    slots = _chip_all_to_all(partial, "exchange_grads")
    reduced = _sum_slots(slots, "sum_grads")
    mine_big = reduced[:big_rows]
    other_big = _sibling_push(mine_big[None], False, "swap_halves")[0]
    both = jnp.stack([jnp.where(ic == 0, mine_big, other_big), jnp.where(ic == 0, other_big, mine_big)])
    g_big = [both[:, :r_in0, :c_in0].reshape(d, c_in0),
             both[:, o_out0:o_in1].reshape(2 * r_out0, d),
             both[:, o_in1:o_out1].reshape(2, r_in1, 2, c_in1).transpose(0, 2, 1, 3).reshape(d, c_in1),
             both[:, o_out1:].reshape(2 * r_out1, d)]
    last = _all_gather8(jnp.concatenate([reduced[big_rows:], dmod_rows], axis=0), "gather_small_grads", pltpu.VMEM)
    last = last.reshape(N_DEV, piece_rows + dmod_rows.shape[0], d)
    small_all = last[:, :piece_rows].reshape(N_DEV * piece_rows, d)
    dmod_all = last[:, piece_rows:piece_rows + bl * 6].reshape(N_DEV * bl, 6 * d)
    dmod_chip = lax.dynamic_slice_in_dim(dmod_all.reshape(N_DEV * bl, 2, 3 * d), chip * ncol, ncol, axis=2)
    g_ada_w, g_ada_b = _ada_bwd(c_all, dmod_chip.transpose(1, 0, 2), dmod_all, "ada_bwd")
    g_ada_b = g_ada_b.reshape(2, 3 * d)
    g_small, off = [], 0
    for cnt in small_counts:
        g_small.append(small_all[off:off + cnt])
        off += cnt
    g_w_a, g_w_x = g_small[0].reshape(lru_w_a.shape[1:]), g_small[1].reshape(lru_w_x.shape[1:])
    g_norm_g = jnp.concatenate(g_small[2:4], axis=0)
    g_fin, g_cw_r, g_cb_r, g_ba_r, g_bx_r, g_lam_r = g_small[4:10]
    tail = g_small[10][0]
    g_rel_bias = tail[:n_relb].reshape(REL_BUCKETS, A_Q_HEADS)
    g_sinks, g_b_f = tail[n_relb:n_relb + A_Q_HEADS], tail[n_relb + A_Q_HEADS:n_relb + A_Q_HEADS + B_HEADS]
    cw4 = lw // N_CHIP

    def my_cols(a):
        return lax.dynamic_slice_in_dim(a, chip * cw4, cw4, axis=1)

    grads = {
        "rel_bias": g_rel_bias, "norm_g": g_norm_g, "ada_w": g_ada_w, "ada_b": g_ada_b,
        "attn_w_in": g_big[0][None], "attn_sinks": g_sinks[None], "attn_b_f": g_b_f[None],
        "attn_w_out": g_big[1][None], "lru_w_in": g_big[2][None], "lru_conv_w": my_cols(g_cw_r)[None],
        "lru_conv_b": my_cols(g_cb_r), "lru_w_a": g_w_a[None], "lru_b_a": my_cols(g_ba_r),
        "lru_w_x": g_w_x[None], "lru_b_x": my_cols(g_bx_r), "lru_lambda": my_cols(g_lam_r),
        "lru_w_out": g_big[3][None], "final_g": g_fin.reshape(d),
    }
    weights = dict(rel_bias=rel_bias, norm_g=norm_g, ada_w=ada_w, ada_b=ada_b, attn_w_in=attn_w_in,
                   attn_sinks=attn_sinks, attn_b_f=attn_b_f, attn_w_out=attn_w_out, lru_w_in=lru_w_in,
                   lru_conv_w=lru_conv_w, lru_conv_b=lru_conv_b, lru_w_a=lru_w_a, lru_b_a=lru_b_a,
                   lru_w_x=lru_w_x, lru_b_x=lru_b_x, lru_lambda=lru_lambda, lru_w_out=lru_w_out, final_g=final_g)
    moms = dict(rel_bias=(m_rel_bias, v_rel_bias), norm_g=(m_norm_g, v_norm_g), ada_w=(m_ada_w, v_ada_w),
                ada_b=(m_ada_b, v_ada_b), attn_w_in=(m_attn_w_in, v_attn_w_in),
                attn_sinks=(m_attn_sinks, v_attn_sinks), attn_b_f=(m_attn_b_f, v_attn_b_f),
                attn_w_out=(m_attn_w_out, v_attn_w_out), lru_w_in=(m_lru_w_in, v_lru_w_in),
                lru_conv_w=(m_lru_conv_w, v_lru_conv_w), lru_conv_b=(m_lru_conv_b, v_lru_conv_b),
                lru_w_a=(m_lru_w_a, v_lru_w_a), lru_b_a=(m_lru_b_a, v_lru_b_a), lru_w_x=(m_lru_w_x, v_lru_w_x),
                lru_b_x=(m_lru_b_x, v_lru_b_x), lru_lambda=(m_lru_lambda, v_lru_lambda),
                lru_w_out=(m_lru_w_out, v_lru_w_out), final_g=(m_final_g, v_final_g))
    names = list(weights)
    big_names = [n for n in names if weights[n].size >= 65536]
    small_names = [n for n in names if weights[n].size < 65536]
    delta, new_m, new_v = {}, {}, {}
    for n in big_names:
        delta[n], new_m[n], new_v[n] = _adamw(weights[n], grads[n].reshape(weights[n].shape),
                                              moms[n][0], moms[n][1], "adamw_" + n)
    cat = lambda arrs: jnp.concatenate([a.reshape(-1) for a in arrs])
    sd, sm, sv = _adamw(cat([weights[n] for n in small_names]), cat([grads[n] for n in small_names]),
                        cat([moms[n][0] for n in small_names]), cat([moms[n][1] for n in small_names]),
                        "adamw_small")
    off = 0
    for n in small_names:
        sz = weights[n].size
        shp = weights[n].shape
        delta[n], new_m[n], new_v[n] = (sd[off:off + sz].reshape(shp), sm[off:off + sz].reshape(shp),
                                        sv[off:off + sz].reshape(shp))
        off += sz
    out_grads = [grads[n].reshape(weights[n].shape) for n in names]
    return (loss, dx0, *out_grads, *[delta[n] for n in names], *[new_m[n] for n in names],
            *[new_v[n] for n in names])
```

```python
import math

import numpy as np
import jax
import jax.numpy as jnp
from jax import lax
from jax.experimental import pallas as pl
from jax.experimental.pallas import tpu as pltpu

F32 = jnp.float32
BF16 = jnp.bfloat16
MESH = pl.DeviceIdType.MESH

N_DEV = 8
N_CHIP = 4
HEAD_DIM = 64
BLK = 128
A_Q_HEADS = 8
A_KV_HEADS = 2
A_GROUP = A_Q_HEADS // A_KV_HEADS
B_HEADS = 8
REL_BUCKETS = 32
REL_MAX_EXACT = 16
REL_MAX_DIST = 128
LRU_C = 8.0
CONV_WIDTH = 4
EPS = 1e-6
NEG = -1e30
SCALE = HEAD_DIM ** -0.5
LANE = 128
SUBLANE = 8
VMEM_LIMIT = 56 * 1024 * 1024
SCAN_CHUNK = 512
ROW_TILE = 1024
SWA_BPS = 1
FOX_BQ = 512
ADAM_LR = 0.001
ADAM_B1 = 0.9
ADAM_B2 = 0.999
ADAM_EPS = 1e-08
ADAM_WD = 0.01
ADAM_STEP = 10
HI = lax.Precision.HIGHEST


def _cp(sem=None):
    return pltpu.CompilerParams(dimension_semantics=sem, vmem_limit_bytes=VMEM_LIMIT)


def _dot(a, b):
    return jnp.dot(a, b, preferred_element_type=F32)


def _dot_nt(a, b):
    return lax.dot_general(a, b, (((1,), (1,)), ((), ())), preferred_element_type=F32)


def _dot_tn(a, b):
    return lax.dot_general(a, b, (((0,), (0,)), ((), ())), preferred_element_type=F32)


def _sigmoid(z):
    return 1.0 / (1.0 + jnp.exp(-z))


def _row_tile(rows, cap):
    if rows <= cap:
        return rows
    best = SUBLANE
    t = SUBLANE
    while t <= cap:
        if rows % t == 0:
            best = t
        t += SUBLANE
    return best


def _all_gather8(x_shard, name, space):
    m_per, n = x_shard.shape
    n_own = 8 if (space == pltpu.HBM and m_per % 128 == 0) else 1
    own_rows = m_per // n_own

    def body(x_ref, out_ref, send_sems, recv_sems, local_sems):
        x, y, c = lax.axis_index("x"), lax.axis_index("y"), lax.axis_index("c")
        me, sibling = (x, y, c), (x, y, 1 - c)
        chips = [(1 - x, y), (x, 1 - y), (1 - x, 1 - y)]

        def rows(px, py, pc):
            return out_ref.at[pl.ds((4 * px + 2 * py + pc) * m_per, m_per), :]

        def copy(k, block, to, src=None):
            return pltpu.make_async_remote_copy(
                src_ref=rows(*block) if src is None else src, dst_ref=rows(*block),
                send_sem=send_sems.at[k], recv_sem=recv_sems.at[k], device_id=to, device_id_type=MESH)

        base = (4 * x + 2 * y + c) * m_per
        mine = [pltpu.make_async_copy(x_ref.at[pl.ds(i * own_rows, own_rows), :],
                                      out_ref.at[pl.ds(base + i * own_rows, own_rows), :], local_sems.at[i])
                for i in range(n_own)]
        for cp in mine:
            cp.start()
        first = [copy(0, me, sibling, src=x_ref)]
        first += [copy(1 + j, me, (*chip, c), src=x_ref) for j, chip in enumerate(chips)]
        for cp in first:
            cp.start()
        passed = [copy(4 + j, (*chip, c), sibling) for j, chip in enumerate(chips)]
        for j, chip in enumerate(chips):
            copy(1 + j, (*chip, c), me).wait_recv()
            passed[j].start()
        copy(0, sibling, me).wait_recv()
        for j, chip in enumerate(chips):
            copy(4 + j, (*chip, 1 - c), me).wait_recv()
        for cp in first + passed:
            cp.wait_send()
        for cp in mine:
            cp.wait()

    return pl.pallas_call(
        body, name=name,
        out_shape=jax.ShapeDtypeStruct((N_DEV * m_per, n), x_shard.dtype),
        in_specs=[pl.BlockSpec(memory_space=space)],
        out_specs=pl.BlockSpec(memory_space=space),
        scratch_shapes=[pltpu.SemaphoreType.DMA((7,)), pltpu.SemaphoreType.DMA((7,)),
                        pltpu.SemaphoreType.DMA((n_own,))],
        compiler_params=pltpu.CompilerParams(vmem_limit_bytes=VMEM_LIMIT),
    )(x_shard)


def _sibling_push(blocks, pick_other, name):
    nblk = blocks.shape[0]
    m, n = blocks.shape[-2:]

    def body(x_ref, out_ref, send_sems, recv_sems):
        x, y, c = lax.axis_index("x"), lax.axis_index("y"), lax.axis_index("c")
        copies = []
        for k in range(nblk):
            src = x_ref.at[k, 1 - c] if pick_other else x_ref.at[k]
            copies.append(pltpu.make_async_remote_copy(
                src_ref=src, dst_ref=out_ref.at[k], send_sem=send_sems.at[k], recv_sem=recv_sems.at[k],
                device_id=(x, y, 1 - c), device_id_type=MESH))
        for cp in copies:
            cp.start()
        for cp in copies:
            cp.wait_recv()
        for cp in copies:
            cp.wait_send()

    hbm = pl.BlockSpec(memory_space=pltpu.HBM)
    return pl.pallas_call(
        body, name=name,
        out_shape=jax.ShapeDtypeStruct((nblk, m, n), blocks.dtype),
        in_specs=[hbm], out_specs=hbm,
        scratch_shapes=[pltpu.SemaphoreType.DMA((nblk,)), pltpu.SemaphoreType.DMA((nblk,))],
    )(blocks)


def _chip_all_to_all(parts, name):
    _, m, n = parts.shape

    def body(x_ref, out_ref, send_sems, recv_sems, local_sem):
        x, y, c = lax.axis_index("x"), lax.axis_index("y"), lax.axis_index("c")
        me = 2 * x + y
        mine = pltpu.make_async_copy(x_ref.at[me], out_ref.at[me], local_sem)
        mine.start()
        copies = []
        for k in range(1, N_CHIP):
            px, py = x ^ ((k >> 1) & 1), y ^ (k & 1)
            copies.append(pltpu.make_async_remote_copy(
                src_ref=x_ref.at[2 * px + py], dst_ref=out_ref.at[me],
                send_sem=send_sems.at[k - 1], recv_sem=recv_sems.at[k - 1],
                device_id=(px, py, c), device_id_type=MESH))
        for cp in copies:
            cp.start()
        for cp in copies:
            cp.wait_recv()
        for cp in copies:
            cp.wait_send()
        mine.wait()

    hbm = pl.BlockSpec(memory_space=pltpu.HBM)
    return pl.pallas_call(
        body, name=name,
        out_shape=jax.ShapeDtypeStruct(parts.shape, parts.dtype),
        in_specs=[hbm], out_specs=hbm,
        scratch_shapes=[pltpu.SemaphoreType.DMA((N_CHIP - 1,)), pltpu.SemaphoreType.DMA((N_CHIP - 1,)),
                        pltpu.SemaphoreType.DMA],
    )(parts)


def _pair_sum(core, pieces, theirs, name):
    nblk, _, m, n = pieces.shape
    tr = _row_tile(m, 536)

    def body(c_ref, p_ref, t_ref, o_ref):
        o_ref[...] = (p_ref[...] + t_ref[...]).astype(BF16)

    return pl.pallas_call(
        body, name=name,
        grid_spec=pltpu.PrefetchScalarGridSpec(
            num_scalar_prefetch=1, grid=(nblk, m // tr),
            in_specs=[pl.BlockSpec((None, None, tr, n), lambda k, i, c_ref: (k, c_ref[0], i, 0)),
                      pl.BlockSpec((None, tr, n), lambda k, i, c_ref: (k, i, 0))],
            out_specs=pl.BlockSpec((None, tr, n), lambda k, i, c_ref: (k, i, 0))),
        out_shape=jax.ShapeDtypeStruct((nblk, m, n), BF16),
        compiler_params=_cp(("parallel", "parallel")),
    )(core, pieces, theirs)


def _sum_slots(slots, name):
    k, m, n = slots.shape
    tr = _row_tile(m, 536)

    def body(s_ref, o_ref):
        acc = s_ref[0].astype(F32)
        for j in range(1, k):
            acc = acc + s_ref[j].astype(F32)
        o_ref[...] = acc

    return pl.pallas_call(
        body, name=name, grid=(m // tr,),
        out_shape=jax.ShapeDtypeStruct((m, n), F32),
        in_specs=[pl.BlockSpec((k, tr, n), lambda i: (0, i, 0))],
        out_specs=pl.BlockSpec((tr, n), lambda i: (i, 0)),
        compiler_params=_cp(("parallel",)),
    )(slots)


def _ada_fwd(c_all, w, b, name):
    r, _ = c_all.shape
    n = w.shape[1]

    def body(c_ref, w_ref, b_ref, o_ref):
        cv = c_ref[...]
        act = cv * _sigmoid(cv)
        o_ref[...] = jnp.dot(act, w_ref[...], precision=HI, preferred_element_type=F32) + b_ref[...]

    return pl.pallas_call(body, name=name, out_shape=jax.ShapeDtypeStruct((r, n), F32),
                          compiler_params=_cp())(c_all, w, b)


def _ada_bwd(c_all, dmod_chip, dmod_all, name):
    r, d = c_all.shape
    nl, _, n = dmod_chip.shape

    def body(c_ref, dm_ref, da_ref, gw_ref, gb_ref):
        cv = c_ref[...]
        act = cv * _sigmoid(cv)
        for l in range(nl):
            gw_ref[l] = lax.dot_general(act, dm_ref[l], (((0,), (0,)), ((), ())), precision=HI,
                                        preferred_element_type=F32)
        gb_ref[...] = jnp.sum(da_ref[...], axis=0, keepdims=True)

    return pl.pallas_call(
        body, name=name,
        out_shape=(jax.ShapeDtypeStruct((nl, d, n), F32), jax.ShapeDtypeStruct((1, dmod_all.shape[1]), F32)),
        compiler_params=_cp())(c_all, dmod_chip, dmod_all)


def _norm_proj(x, g, scale, shift, w, f32_cols, name):
    b, s, d = x.shape
    n = w.shape[1]
    tm = min(s, ROW_TILE)

    def body(x_ref, g_ref, sc_ref, sh_ref, w_ref, proj_ref, h_ref, *aux_ref):
        xv = x_ref[...]
        rstd = lax.rsqrt(jnp.mean(xv * xv, axis=-1, keepdims=True) + EPS)
        h = (xv * rstd) * g_ref[...] * (1.0 + sc_ref[...]) + sh_ref[...]
        hb = h.astype(BF16)
        h_ref[...] = hb
        proj = _dot(hb, w_ref[...])
        proj_ref[...] = proj.astype(BF16)
        if f32_cols:
            aux_ref[0][...] = proj[:, n - f32_cols:]

    row = lambda i, j: (i, j, 0)
    out_shape = [jax.ShapeDtypeStruct((b, s, n), BF16), jax.ShapeDtypeStruct((b, s, d), BF16)]
    out_specs = [pl.BlockSpec((None, tm, n), row), pl.BlockSpec((None, tm, d), row)]
    if f32_cols:
        out_shape.append(jax.ShapeDtypeStruct((b, s, f32_cols), F32))
        out_specs.append(pl.BlockSpec((None, tm, f32_cols), row))
    return pl.pallas_call(
        body, name=name, grid=(b, s // tm),
        out_shape=tuple(out_shape),
        in_specs=[pl.BlockSpec((None, tm, d), row),
                  pl.BlockSpec((1, d), lambda i, j: (0, 0)),
                  pl.BlockSpec((None, 1, d), lambda i, j: (i, 0, 0)),
                  pl.BlockSpec((None, 1, d), lambda i, j: (i, 0, 0)),
                  pl.BlockSpec((d, n), lambda i, j: (0, 0))],
        out_specs=tuple(out_specs),
        compiler_params=_cp(("parallel", "parallel")),
    )(x, g, scale, shift, w)


def _cat_refs(refs):
    vals = [r[...] for r in refs]
    return vals[0] if len(vals) == 1 else jnp.concatenate(vals, axis=-1)


def _exit_norm_proj(mix_parts, proj, gate_blk, w_out, x, gmod, g, scale, shift, w_next, name):
    b, s, d = x.shape
    wd = w_out.shape[0]
    n = w_next.shape[1]
    tm = min(s, ROW_TILE // 2)
    npart = len(mix_parts)

    def body(*refs):
        mix_refs = refs[:npart]
        gate_ref, w_ref, x_ref, gm_ref, g_ref, sc_ref, sh_ref, wn_ref, xo_ref, o_ref, proj_ref, h_ref = refs[npart:]
        gt = gate_ref[...].astype(F32)
        o = _dot((_cat_refs(mix_refs) * (gt * _sigmoid(gt))).astype(BF16), w_ref[...])
        o_ref[...] = o.astype(BF16)
        xv = x_ref[...] + gm_ref[...] * o
        xo_ref[...] = xv
        rstd = lax.rsqrt(jnp.mean(xv * xv, axis=-1, keepdims=True) + EPS)
        hb = ((xv * rstd) * g_ref[...] * (1.0 + sc_ref[...]) + sh_ref[...]).astype(BF16)
        h_ref[...] = hb
        proj_ref[...] = _dot(hb, wn_ref[...]).astype(BF16)

    row = lambda i, j: (i, j, 0)
    per_b = lambda i, j: (i, 0, 0)
    return pl.pallas_call(
        body, name=name, grid=(b, s // tm),
        out_shape=(jax.ShapeDtypeStruct((b, s, d), F32), jax.ShapeDtypeStruct((b, s, d), BF16),
                   jax.ShapeDtypeStruct((b, s, n), BF16), jax.ShapeDtypeStruct((b, s, d), BF16)),
        in_specs=[pl.BlockSpec((None, tm, p.shape[2]), row) for p in mix_parts] + [
                  pl.BlockSpec((None, tm, wd), lambda i, j: (i, j, gate_blk)),
                  pl.BlockSpec((wd, d), lambda i, j: (0, 0)),
                  pl.BlockSpec((None, tm, d), row),
                  pl.BlockSpec((None, 1, d), per_b),
                  pl.BlockSpec((1, d), lambda i, j: (0, 0)),
                  pl.BlockSpec((None, 1, d), per_b),
                  pl.BlockSpec((None, 1, d), per_b),
                  pl.BlockSpec((d, n), lambda i, j: (0, 0))],
        out_specs=(pl.BlockSpec((None, tm, d), row), pl.BlockSpec((None, tm, d), row),
                   pl.BlockSpec((None, tm, n), row), pl.BlockSpec((None, tm, d), row)),
        compiler_params=_cp(("parallel", "parallel")),
    )(*mix_parts, proj, w_out, x, gmod, g, scale, shift, w_next)


def _last_exit_loss(mix, proj, gate_blk, w_out, x, gmod, g, target, name):
    b, s, d = x.shape
    wd = w_out.shape[0]
    tm = min(s, ROW_TILE // 2)

    def body(mix_ref, gate_ref, w_ref, x_ref, gm_ref, g_ref, t_ref, o_ref, loss_ref, dx_ref, dg_ref):
        first = jnp.logical_and(pl.program_id(0) == 0, pl.program_id(1) == 0)

        @pl.when(first)
        def _():
            loss_ref[...] = jnp.zeros_like(loss_ref)
            dg_ref[...] = jnp.zeros_like(dg_ref)

        gt = gate_ref[...].astype(F32)
        o = _dot((mix_ref[...] * (gt * _sigmoid(gt))).astype(BF16), w_ref[...])
        o_ref[...] = o.astype(BF16)
        xv = x_ref[...] + gm_ref[...] * o
        gv = g_ref[...]
        rstd = lax.rsqrt(jnp.mean(xv * xv, axis=-1, keepdims=True) + EPS)
        xhat = xv * rstd
        err = xhat * gv - t_ref[...]
        row = jnp.mean(err * err, axis=-1, keepdims=True)
        loss_ref[...] += 0.5 * jnp.sum(row, axis=0, keepdims=True)
        dy = err * (1.0 / d)
        dg_ref[...] += jnp.sum(dy * xhat, axis=0, keepdims=True)
        dxh = dy * gv
        dx_ref[...] = rstd * (dxh - xhat * jnp.mean(dxh * xhat, axis=-1, keepdims=True))

    row = lambda i, j: (i, j, 0)
    return pl.pallas_call(
        body, name=name, grid=(b, s // tm),
        out_shape=(jax.ShapeDtypeStruct((b, s, d), BF16), jax.ShapeDtypeStruct((1, LANE), F32),
                   jax.ShapeDtypeStruct((b, s, d), F32), jax.ShapeDtypeStruct((1, d), F32)),
        in_specs=[pl.BlockSpec((None, tm, wd), row),
                  pl.BlockSpec((None, tm, wd), lambda i, j: (i, j, gate_blk)),
                  pl.BlockSpec((wd, d), lambda i, j: (0, 0)),
                  pl.BlockSpec((None, tm, d), row),
                  pl.BlockSpec((None, 1, d), lambda i, j: (i, 0, 0)),
                  pl.BlockSpec((1, d), lambda i, j: (0, 0)),
                  pl.BlockSpec((None, tm, d), row)],
        out_specs=(pl.BlockSpec((None, tm, d), row),
                   pl.BlockSpec((1, LANE), lambda i, j: (0, 0)),
                   pl.BlockSpec((None, tm, d), row),
                   pl.BlockSpec((1, d), lambda i, j: (0, 0))),
        compiler_params=_cp(("arbitrary", "arbitrary")),
    )(mix, proj, w_out, x, gmod, g, target)


def _bwd_out(dxo, gmod, o, mix_parts, proj, gate_blk, w_out, dmix_dtype, name):
    b, s, d = dxo.shape
    wd = w_out.shape[0]
    tm = min(s, ROW_TILE)
    npart = len(mix_parts)

    def body(dx_ref, gm_ref, o_ref, *refs):
        mix_refs = refs[:npart]
        gate_ref, wt_ref, dmix_ref, dgate_ref, do_ref, y_ref, dgm_ref = refs[npart:]

        @pl.when(pl.program_id(1) == 0)
        def _():
            dgm_ref[...] = jnp.zeros_like(dgm_ref)

        dx = dx_ref[...]
        dgm_ref[...] += jnp.sum(dx * o_ref[...].astype(F32), axis=0, keepdims=True)
        dob = (gm_ref[...] * dx).astype(BF16)
        do_ref[...] = dob
        dy = _dot_nt(dob, wt_ref[...])
        gt = gate_ref[...].astype(F32)
        sg = _sigmoid(gt)
        silu = gt * sg
        mx = _cat_refs(mix_refs)
        y_ref[...] = (mx * silu).astype(BF16)
        dmix_ref[...] = (dy * silu).astype(dmix_dtype)
        dgate_ref[...] = (dy * mx * (sg * (1.0 + gt * (1.0 - sg)))).astype(BF16)

    row = lambda i, j: (i, j, 0)
    return pl.pallas_call(
        body, name=name, grid=(b, s // tm),
        out_shape=(jax.ShapeDtypeStruct((b, s, wd), dmix_dtype), jax.ShapeDtypeStruct((b, s, wd), BF16),
                   jax.ShapeDtypeStruct((b, s, d), BF16), jax.ShapeDtypeStruct((b, s, wd), BF16),
                   jax.ShapeDtypeStruct((b, 1, d), F32)),
        in_specs=[pl.BlockSpec((None, tm, d), row),
                  pl.BlockSpec((None, 1, d), lambda i, j: (i, 0, 0)),
                  pl.BlockSpec((None, tm, d), row)] + [
                  pl.BlockSpec((None, tm, p.shape[2]), row) for p in mix_parts] + [
                  pl.BlockSpec((None, tm, wd), lambda i, j: (i, j, gate_blk)),
                  pl.BlockSpec((wd, d), lambda i, j: (0, 0))],
        out_specs=(pl.BlockSpec((None, tm, wd), row), pl.BlockSpec((None, tm, wd), row),
                   pl.BlockSpec((None, tm, d), row), pl.BlockSpec((None, tm, wd), row),
                   pl.BlockSpec((None, 1, d), lambda i, j: (i, 0, 0))),
        compiler_params=_cp(("parallel", "arbitrary")),
    )(dxo, gmod, o, *mix_parts, proj, w_out)


def _bwd_in(dproj_parts, w_in, x, g, scale, dxo, name):
    b, s, d = x.shape
    n = w_in.shape[1]
    tm = min(s, ROW_TILE)
    npart = len(dproj_parts)

    def body(*refs):
        dp_refs = refs[:npart]
        wt_ref, x_ref, g_ref, sc_ref, dxo_ref, dx_ref, dsh_ref, dsc_ref, dg_ref = refs[npart:]

        @pl.when(jnp.logical_and(pl.program_id(0) == 0, pl.program_id(1) == 0))
        def _():
            dg_ref[...] = jnp.zeros_like(dg_ref)

        @pl.when(pl.program_id(1) == 0)
        def _():
            dsh_ref[...] = jnp.zeros_like(dsh_ref)
            dsc_ref[...] = jnp.zeros_like(dsc_ref)

        dh = _dot_nt(_cat_refs(dp_refs), wt_ref[...])
        xv = x_ref[...]
        gv = g_ref[...]
        one_sc = 1.0 + sc_ref[...]
        rstd = lax.rsqrt(jnp.mean(xv * xv, axis=-1, keepdims=True) + EPS)
        xhat = xv * rstd
        dsh_ref[...] += jnp.sum(dh, axis=0, keepdims=True)
        dsc_ref[...] += jnp.sum(dh * (xhat * gv), axis=0, keepdims=True)
        dhs = dh * one_sc
        dg_ref[...] += jnp.sum(dhs * xhat, axis=0, keepdims=True)
        dxh = dhs * gv
        dx_ref[...] = dxo_ref[...] + rstd * (dxh - xhat * jnp.mean(dxh * xhat, axis=-1, keepdims=True))

    row = lambda i, j: (i, j, 0)
    per_b = lambda i, j: (i, 0, 0)
    return pl.pallas_call(
        body, name=name, grid=(b, s // tm),
        out_shape=(jax.ShapeDtypeStruct((b, s, d), F32), jax.ShapeDtypeStruct((b, 1, d), F32),
                   jax.ShapeDtypeStruct((b, 1, d), F32), jax.ShapeDtypeStruct((1, d), F32)),
        in_specs=[pl.BlockSpec((None, tm, p.shape[2]), row) for p in dproj_parts] + [
                  pl.BlockSpec((d, n), lambda i, j: (0, 0)),
                  pl.BlockSpec((None, tm, d), row),
                  pl.BlockSpec((1, d), lambda i, j: (0, 0)),
                  pl.BlockSpec((None, 1, d), per_b),
                  pl.BlockSpec((None, tm, d), row)],
        out_specs=(pl.BlockSpec((None, tm, d), row), pl.BlockSpec((None, 1, d), per_b),
                   pl.BlockSpec((None, 1, d), per_b), pl.BlockSpec((1, d), lambda i, j: (0, 0))),
        compiler_params=_cp(("arbitrary", "arbitrary")),
    )(*dproj_parts, w_in, x, g, scale, dxo)


def _matmul_tn(a, b_parts, name):
    bsz, s, m = a.shape
    n = sum(p.shape[2] for p in b_parts)
    tk = next(c for c in (ROW_TILE, 512, 256, 128) if s % c == 0)
    npart = len(b_parts)

    def body(a_ref, *refs):
        b_refs, o_ref = refs[:npart], refs[npart]

        @pl.when(jnp.logical_and(pl.program_id(0) == 0, pl.program_id(1) == 0))
        def _():
            o_ref[...] = jnp.zeros_like(o_ref)

        o_ref[...] += _dot_tn(a_ref[...], _cat_refs(b_refs))

    row = lambda i, k: (i, k, 0)
    return pl.pallas_call(
        body, name=name, grid=(bsz, s // tk),
        out_shape=jax.ShapeDtypeStruct((m, n), F32),
        in_specs=[pl.BlockSpec((None, tk, m), row)] + [pl.BlockSpec((None, tk, p.shape[2]), row) for p in b_parts],
        out_specs=pl.BlockSpec((m, n), lambda i, k: (0, 0)),
        compiler_params=_cp(("arbitrary", "arbitrary")),
    )(a, *b_parts)


def _rel_buckets():
    qi = np.arange(BLK)[:, None]
    kj = np.arange(2 * BLK)[None, :]
    rel = qi - kj + BLK
    n = np.maximum(rel, 0)
    nf = np.maximum(n, 1).astype(np.float32)
    large = REL_MAX_EXACT + (np.log(nf / REL_MAX_EXACT) / math.log(REL_MAX_DIST / REL_MAX_EXACT)
                             * (REL_BUCKETS - REL_MAX_EXACT)).astype(np.int32)
    large = np.minimum(large, REL_BUCKETS - 1)
    bucket = np.where(n < REL_MAX_EXACT, n, large).astype(np.int32)
    valid = ((rel >= 0) & (rel < BLK)).astype(np.int32)
    return bucket, valid


def _swa_bias(rel_bias_t, bucket, valid, name):
    nh = rel_bias_t.shape[0]

    def body(rb_ref, bk_ref, vl_ref, o_ref):
        h = pl.program_id(0)
        bk = bk_ref[...]
        acc = jnp.zeros(bk.shape, F32)
        for i in range(REL_BUCKETS):
            acc = jnp.where(bk == i, rb_ref[h, i], acc)
        o_ref[...] = jnp.where(vl_ref[...] > 0, acc, NEG)

    return pl.pallas_call(
        body, name=name, grid=(nh,),
        out_shape=jax.ShapeDtypeStruct((nh, BLK, 2 * BLK), F32),
        in_specs=[pl.BlockSpec(memory_space=pltpu.SMEM),
                  pl.BlockSpec((BLK, 2 * BLK), lambda h: (0, 0)),
                  pl.BlockSpec((BLK, 2 * BLK), lambda h: (0, 0))],
        out_specs=pl.BlockSpec((None, BLK, 2 * BLK), lambda h: (h, 0, 0)),
        compiler_params=_cp(("arbitrary",)),
    )(rel_bias_t, bucket, valid)


def _swa_scores(n, q, kw, bias_ref):
    sc = _dot_nt(q, kw) * SCALE + bias_ref[...]
    second = lax.broadcasted_iota(jnp.int32, sc.shape, 1) >= BLK
    return jnp.where(jnp.logical_or(n > 0, second), sc, NEG)


def _pad_front(dst_ref, src_ref):
    dst_ref[0:BLK, :] = jnp.zeros((BLK, dst_ref.shape[1]), dst_ref.dtype)
    dst_ref[BLK:, :] = src_ref[...]


def _swa_fwd(q, k, v, bias, sinks, name):
    b, hkv, nb, rows, hd = q.shape
    wide = bias.shape[2]
    stride = wide - BLK
    s = nb * stride

    def body(q_ref, k_ref, v_ref, bias_ref, sink_ref, o_ref, l_ref, kpad_ref, vpad_ref):
        _pad_front(kpad_ref, k_ref)
        _pad_front(vpad_ref, v_ref)
        sink = sink_ref[...]

        def step(n, carry):
            w0 = pl.multiple_of(n * stride, BLK)
            sc = _swa_scores(n, q_ref[n], kpad_ref[pl.ds(w0, wide), :], bias_ref)
            m = jnp.maximum(jnp.max(sc, axis=1, keepdims=True), sink)
            e = jnp.exp(sc - m)
            den = jnp.sum(e, axis=1, keepdims=True) + jnp.exp(sink - m)
            o_ref[n] = _dot((e * (1.0 / den)).astype(BF16), vpad_ref[pl.ds(w0, wide), :]).astype(BF16)
            l_ref[n] = m + jnp.log(den)
            return carry

        lax.fori_loop(0, nb, step, 0)

    qspec = pl.BlockSpec((None, None, nb, rows, hd), lambda i, kv: (i, kv, 0, 0, 0))
    kspec = pl.BlockSpec((None, None, s, hd), lambda i, kv: (i, kv, 0, 0))
    return pl.pallas_call(
        body, name=name, grid=(b, hkv),
        out_shape=(jax.ShapeDtypeStruct((b, hkv, nb, rows, hd), BF16), jax.ShapeDtypeStruct((b, hkv, nb, rows, 1), F32)),
        in_specs=[qspec, kspec, kspec,
                  pl.BlockSpec((None, rows, wide), lambda i, kv: (kv, 0, 0)),
                  pl.BlockSpec((None, rows, 1), lambda i, kv: (kv, 0, 0))],
        out_specs=(qspec, pl.BlockSpec((None, None, nb, rows, 1), lambda i, kv: (i, kv, 0, 0, 0))),
        scratch_shapes=[pltpu.VMEM((s + BLK, hd), BF16), pltpu.VMEM((s + BLK, hd), BF16)],
        compiler_params=_cp(("parallel", "parallel")),
    )(q, k, v, bias, sinks)


def _swa_bwd(q, k, v, bias, sinks, do, lse, name):
    b, hkv, nb, rows, hd = q.shape
    wide = bias.shape[2]
    stride = wide - BLK
    s = nb * stride

    def body(q_ref, k_ref, v_ref, bias_ref, sink_ref, do_ref, l_ref,
             dq_ref, dk_ref, dv_ref, db_ref, dsk_ref, kpad_ref, vpad_ref, dkpad_ref, dvpad_ref):
        _pad_front(kpad_ref, k_ref)
        _pad_front(vpad_ref, v_ref)
        dkpad_ref[...] = jnp.zeros_like(dkpad_ref)
        dvpad_ref[...] = jnp.zeros_like(dvpad_ref)
        db_ref[...] = jnp.zeros_like(db_ref)
        sink = sink_ref[...]

        def step(n, dsink):
            w0 = pl.multiple_of(n * stride, BLK)
            win = pl.ds(w0, wide)
            qn = q_ref[n]
            kw = kpad_ref[win, :]
            ln = l_ref[n]
            p = jnp.exp(_swa_scores(n, qn, kw, bias_ref) - ln)
            dob = do_ref[n]
            dp = _dot_nt(dob, vpad_ref[win, :])
            delta = jnp.sum(p * dp, axis=1, keepdims=True)
            ds = p * (dp - delta)
            db_ref[...] += ds
            dsb = ds.astype(BF16)
            dq_ref[n] = (_dot(dsb, kw) * SCALE).astype(BF16)
            dkpad_ref[win, :] += _dot_tn(dsb, qn)
            dvpad_ref[win, :] += _dot_tn(p.astype(BF16), dob)
            return dsink - jnp.exp(sink - ln) * delta

        dsink = lax.fori_loop(0, nb, step, jnp.zeros((rows, 1), F32))
        for g in range(A_GROUP):
            tot = jnp.zeros((1, 1), F32)
            for blk in range(rows // (A_GROUP * BLK)):
                r0 = (blk * A_GROUP + g) * BLK
                tot = tot + jnp.sum(dsink[r0:r0 + BLK, :], axis=0, keepdims=True)
            dsk_ref[g] = jnp.broadcast_to(tot, (1, LANE))
        dk_ref[...] = (dkpad_ref[BLK:, :] * SCALE).astype(BF16)
        dv_ref[...] = dvpad_ref[BLK:, :].astype(BF16)

    qspec = pl.BlockSpec((None, None, nb, rows, hd), lambda i, kv: (i, kv, 0, 0, 0))
    kspec = pl.BlockSpec((None, None, s, hd), lambda i, kv: (i, kv, 0, 0))
    return pl.pallas_call(
        body, name=name, grid=(b, hkv),
        out_shape=(jax.ShapeDtypeStruct((b, hkv, nb, rows, hd), BF16), jax.ShapeDtypeStruct((b, hkv, s, hd), BF16),
                   jax.ShapeDtypeStruct((b, hkv, s, hd), BF16), jax.ShapeDtypeStruct((b, hkv, rows, wide), F32),
                   jax.ShapeDtypeStruct((b, hkv, A_GROUP, 1, LANE), F32)),
        in_specs=[qspec, kspec, kspec,
                  pl.BlockSpec((None, rows, wide), lambda i, kv: (kv, 0, 0)),
                  pl.BlockSpec((None, rows, 1), lambda i, kv: (kv, 0, 0)),
                  qspec,
                  pl.BlockSpec((None, None, nb, rows, 1), lambda i, kv: (i, kv, 0, 0, 0))],
        out_specs=(qspec, kspec, kspec,
                   pl.BlockSpec((None, None, rows, wide), lambda i, kv: (i, kv, 0, 0)),
                   pl.BlockSpec((None, None, A_GROUP, 1, LANE), lambda i, kv: (i, kv, 0, 0, 0))),
        scratch_shapes=[pltpu.VMEM((s + BLK, hd), BF16), pltpu.VMEM((s + BLK, hd), BF16),
                        pltpu.VMEM((s + BLK, hd), F32), pltpu.VMEM((s + BLK, hd), F32)],
        compiler_params=_cp(("parallel", "parallel")),
    )(q, k, v, bias, sinks, do, lse)


def _swa_small_grads(db, dsk, bucket, name):
    b, nh = db.shape[0], db.shape[1]

    def body(db_ref, dsk_ref, bk_ref, gb_ref, gs_ref):
        acc = db_ref[0]
        sk = dsk_ref[0]
        for i in range(1, b):
            acc = acc + db_ref[i]
            sk = sk + dsk_ref[i]
        gs_ref[...] = sk
        bk = bk_ref[...]
        for i in range(REL_BUCKETS):
            part = jnp.sum(jnp.where(bk == i, acc, 0.0), axis=1, keepdims=True)
            tot = jnp.sum(part, axis=0, keepdims=True)
            gb_ref[i:i + 1, :] = jnp.broadcast_to(tot, (1, LANE))

    return pl.pallas_call(
        body, name=name, grid=(nh,),
        out_shape=(jax.ShapeDtypeStruct((nh, REL_BUCKETS, LANE), F32), jax.ShapeDtypeStruct((nh, 1, LANE), F32)),
        in_specs=[pl.BlockSpec((b, None, BLK, 2 * BLK), lambda h: (0, h, 0, 0)),
                  pl.BlockSpec((b, None, 1, LANE), lambda h: (0, h, 0, 0)),
                  pl.BlockSpec((BLK, 2 * BLK), lambda h: (0, 0))],
        out_specs=(pl.BlockSpec((None, REL_BUCKETS, LANE), lambda h: (h, 0, 0)),
                   pl.BlockSpec((None, 1, LANE), lambda h: (h, 0, 0))),
        compiler_params=_cp(("parallel",)),
    )(db, dsk, bucket)


def _log_sigmoid(z):
    return jnp.minimum(z, 0.0) - jnp.log(1.0 + jnp.exp(-jnp.abs(z)))


def _fox_decay(z, bf, name):
    b, s, w = z.shape
    nb = s // BLK

    def body(z_ref, bf_ref, f_ref):
        r = lax.broadcasted_iota(jnp.int32, (BLK, BLK), 0)
        c = lax.broadcasted_iota(jnp.int32, (BLK, BLK), 1)
        tri = (c <= r).astype(F32)

        def step(n, carry):
            r0 = pl.multiple_of(n * BLK, BLK)
            lf = _log_sigmoid(z_ref[pl.ds(r0, BLK), :] + bf_ref[...])
            f_ref[pl.ds(r0, BLK), :] = jnp.dot(tri, lf, precision=HI, preferred_element_type=F32) + carry
            return carry + jnp.sum(lf, axis=0, keepdims=True)

        lax.fori_loop(0, nb, step, jnp.zeros((1, w), F32))

    spec = pl.BlockSpec((None, s, w), lambda i: (i, 0, 0))
    return pl.pallas_call(
        body, name=name, grid=(b,), out_shape=jax.ShapeDtypeStruct((b, s, w), F32),
        in_specs=[spec, pl.BlockSpec((1, w), lambda i: (0, 0))], out_specs=spec,
        compiler_params=_cp(("parallel",)),
    )(z, bf)


def _fox_dgate(df, z, bf, nheads, name):
    b, s, w = z.shape
    nb = s // BLK

    def body(df_ref, z_ref, bf_ref, dz_ref, dbf_ref):
        @pl.when(pl.program_id(0) == 0)
        def _():
            dbf_ref[...] = jnp.zeros_like(dbf_ref)

        r = lax.broadcasted_iota(jnp.int32, (BLK, BLK), 0)
        c = lax.broadcasted_iota(jnp.int32, (BLK, BLK), 1)
        tri = (c >= r).astype(F32)
        lane = lax.broadcasted_iota(jnp.int32, (BLK, w), 1)

        def step(i, carry):
            tail, dbf = carry
            r0 = pl.multiple_of((nb - 1 - i) * BLK, BLK)
            dfb = df_ref[pl.ds(r0, BLK), :]
            dlf = jnp.dot(tri, dfb, precision=HI, preferred_element_type=F32) + tail
            dz = jnp.where(lane < nheads, dlf * _sigmoid(-(z_ref[pl.ds(r0, BLK), :] + bf_ref[...])), 0.0)
            dz_ref[pl.ds(r0, BLK), :] = dz
            return tail + jnp.sum(dfb, axis=0, keepdims=True), dbf + jnp.sum(dz, axis=0, keepdims=True)

        zero = jnp.zeros((1, w), F32)
        _, dbf = lax.fori_loop(0, nb, step, (zero, zero))
        dbf_ref[...] += dbf

    spec = pl.BlockSpec((None, s, w), lambda i: (i, 0, 0))
    one = pl.BlockSpec((1, w), lambda i: (0, 0))
    return pl.pallas_call(
        body, name=name, grid=(b,),
        out_shape=(jax.ShapeDtypeStruct((b, s, w), F32), jax.ShapeDtypeStruct((1, w), F32)),
        in_specs=[spec, spec, one], out_specs=(spec, one),
        compiler_params=_cp(("arbitrary",)),
    )(df, z, bf)


def _fox_segments(nb):
    per = max(1, nb // 4)
    return per, nb // per


def _head_masks(shape, axis):
    idx = lax.broadcasted_iota(jnp.int32, shape, axis)
    return idx < HEAD_DIM, idx >= HEAD_DIM


def _fox_fwd(proj, qblk, kblk, vblk, fcol, frow, name):
    b, s, _ = proj.shape
    nh = frow.shape[1]
    npair = nh // 2
    BLK = min(s, FOX_BQ)
    assert s % BLK == 0
    per, nseg = _fox_segments(s // BLK)

    def body(q_ref, k_ref, v_ref, fc_ref, fr_ref, o_ref, l_ref, qm_ref, kt_ref, vb_ref):
        lo, hi = _head_masks((s, LANE), 1)
        qv = q_ref[...].astype(F32) * SCALE
        qm_ref[0] = jnp.where(lo, qv, 0.0).astype(BF16)
        qm_ref[1] = jnp.where(hi, qv, 0.0).astype(BF16)
        kt_ref[...] = k_ref[...].astype(F32).T.astype(BF16)
        vb_ref[...] = v_ref[...].astype(BF16)
        lane_lo = lax.broadcasted_iota(jnp.int32, (BLK, LANE), 1) < HEAD_DIM
        tail = per * BLK
        causal = (lax.broadcasted_iota(jnp.int32, (BLK, tail), 1)
                  - lax.broadcasted_iota(jnp.int32, (BLK, tail), 0))
        for seg in range(nseg):
            w = (seg + 1) * tail

            def qstep(n, carry):
                r0 = pl.multiple_of(n * BLK, BLK)
                outs = []
                for hh in range(2):
                    sc = _dot(qm_ref[hh, pl.ds(r0, BLK), :], kt_ref[:, :w])
                    sc = sc + (fc_ref[hh, pl.ds(r0, BLK), :] - fr_ref[hh, :, :w])
                    masked = jnp.where(causal <= (n - seg * per) * BLK, sc[:, w - tail:], NEG)
                    sc = masked if seg == 0 else jnp.concatenate([sc[:, :w - tail], masked], axis=1)
                    m = jnp.max(sc, axis=1, keepdims=True)
                    e = jnp.exp(sc - m)
                    l = jnp.sum(e, axis=1, keepdims=True)
                    outs.append(_dot((e * (1.0 / l)).astype(BF16), vb_ref[:w, :]))
                    l_ref[hh, pl.ds(r0, BLK), :] = m + jnp.log(l)
                o_ref[pl.ds(r0, BLK), :] = jnp.where(lane_lo, outs[0], outs[1]).astype(BF16)
                return carry

            lax.fori_loop(seg * per, (seg + 1) * per, qstep, 0)

    def tok(blk):
        return pl.BlockSpec((None, s, LANE), lambda i, p: (i, 0, blk + p))

    col = pl.BlockSpec((None, 2, s, 1), lambda i, p: (i, p, 0, 0))
    rowspec = pl.BlockSpec((None, 2, 1, s), lambda i, p: (i, p, 0, 0))
    return pl.pallas_call(
        body, name=name, grid=(b, npair),
        out_shape=(jax.ShapeDtypeStruct((b, s, nh * HEAD_DIM), BF16), jax.ShapeDtypeStruct((b, nh, s, 1), F32)),
        in_specs=[tok(qblk), tok(kblk), tok(vblk), col, rowspec],
        out_specs=(pl.BlockSpec((None, s, LANE), lambda i, p: (i, 0, p)), col),
        scratch_shapes=[pltpu.VMEM((2, s, LANE), BF16), pltpu.VMEM((LANE, s), BF16), pltpu.VMEM((s, LANE), BF16)],
        compiler_params=_cp(("parallel", "parallel")),
    )(proj, proj, proj, fcol, frow)


def _fox_bwd(proj, qblk, kblk, vblk, dmix, doblk, fcol, frow, frowb, lse, lserowb, name):
    b, s, _ = proj.shape
    nh = frow.shape[1]
    npair = nh // 2
    BLK = min(s, FOX_BQ)
    assert s % BLK == 0
    nb = s // BLK
    per, nseg = _fox_segments(nb)

    def body(q_ref, k_ref, v_ref, do_ref, fc_ref, fr_ref, frb_ref, l_ref, lrb_ref,
             dq_ref, dk_ref, dv_ref, dfr_ref,
             qm_ref, dom_ref, kb_ref, vb_ref, kt_ref, vt_ref, qtm_ref, dotm_ref, dka_ref, dva_ref):
        lo, hi = _head_masks((s, LANE), 1)
        qv = q_ref[...].astype(F32) * SCALE
        dov = do_ref[...]
        for hh, msk in enumerate((lo, hi)):
            qm_ref[hh] = jnp.where(msk, qv, 0.0).astype(BF16)
            dom_ref[hh] = jnp.where(msk, dov, 0.0).astype(BF16)
        kv = k_ref[...].astype(F32)
        vv = v_ref[...].astype(F32)
        kb_ref[...] = kv.astype(BF16)
        vb_ref[...] = vv.astype(BF16)
        kt_ref[...] = kv.T.astype(BF16)
        vt_ref[...] = vv.T.astype(BF16)
        rlo, rhi = _head_masks((LANE, BLK), 0)

        def tstep(n, carry):
            r0 = pl.multiple_of(n * BLK, BLK)
            qt = (q_ref[pl.ds(r0, BLK), :].astype(F32) * SCALE).T
            dt = do_ref[pl.ds(r0, BLK), :].astype(F32).T
            for hh, msk in enumerate((rlo, rhi)):
                qtm_ref[hh, n] = jnp.where(msk, qt, 0.0).astype(BF16)
                dotm_ref[hh, n] = jnp.where(msk, dt, 0.0).astype(BF16)
            return carry

        lax.fori_loop(0, nb, tstep, 0)
        dka_ref[...] = jnp.zeros_like(dka_ref)
        dva_ref[...] = jnp.zeros_like(dva_ref)
        dfr_ref[...] = jnp.zeros_like(dfr_ref)
        lane_lo = lax.broadcasted_iota(jnp.int32, (BLK, LANE), 1) < HEAD_DIM
        tail = per * BLK
        causal = (lax.broadcasted_iota(jnp.int32, (BLK, tail), 1)
                  - lax.broadcasted_iota(jnp.int32, (BLK, tail), 0))
        causal_t = (lax.broadcasted_iota(jnp.int32, (tail, BLK), 0)
                    - lax.broadcasted_iota(jnp.int32, (tail, BLK), 1))
        for seg in range(nseg):
            w = (seg + 1) * tail

            def nstep(n, carry):
                r0 = pl.multiple_of(n * BLK, BLK)
                lim = (n - seg * per) * BLK
                dqs = []
                for hh in range(2):
                    qn = qm_ref[hh, pl.ds(r0, BLK), :]
                    don = dom_ref[hh, pl.ds(r0, BLK), :]
                    sc = _dot(qn, kt_ref[:, :w]) + ((fc_ref[hh, pl.ds(r0, BLK), :] - l_ref[hh, pl.ds(r0, BLK), :])
                                                   - fr_ref[hh, :, :w])
                    masked = jnp.where(causal <= lim, sc[:, w - tail:], NEG)
                    p = jnp.exp(masked if seg == 0 else jnp.concatenate([sc[:, :w - tail], masked], axis=1))
                    dp = _dot(don, vt_ref[:, :w])
                    ds = p * (dp - jnp.sum(p * dp, axis=1, keepdims=True))
                    dqs.append(_dot(ds.astype(BF16), kb_ref[:w, :]))
                    dfr_ref[hh, :, :w] -= jnp.sum(ds, axis=0, keepdims=True)
                    sct = _dot(kb_ref[:w, :], qtm_ref[hh, n]) + ((frb_ref[hh, n] - lrb_ref[hh, n]) - fc_ref[hh, :w, :])
                    masked_t = jnp.where(causal_t <= lim, sct[w - tail:, :], NEG)
                    pt = jnp.exp(masked_t if seg == 0 else jnp.concatenate([sct[:w - tail, :], masked_t], axis=0))
                    dpt = _dot(vb_ref[:w, :], dotm_ref[hh, n])
                    dst = pt * (dpt - jnp.sum(pt * dpt, axis=0, keepdims=True))
                    dka_ref[:w, :] += _dot(dst.astype(BF16), qn)
                    dva_ref[:w, :] += _dot(pt.astype(BF16), don)
                dq_ref[pl.ds(r0, BLK), :] = (jnp.where(lane_lo, dqs[0], dqs[1]) * SCALE).astype(BF16)
                return carry

            lax.fori_loop(seg * per, (seg + 1) * per, nstep, 0)
        dk_ref[...] = dka_ref[...].astype(BF16)
        dv_ref[...] = dva_ref[...].astype(BF16)

    def tok(blk):
        return pl.BlockSpec((None, s, LANE), lambda i, p: (i, 0, blk + p))

    col = pl.BlockSpec((None, 2, s, 1), lambda i, p: (i, p, 0, 0))
    rowspec = pl.BlockSpec((None, 2, 1, s), lambda i, p: (i, p, 0, 0))
    rowbspec = pl.BlockSpec((None, 2, nb, 1, BLK), lambda i, p: (i, p, 0, 0, 0))
    outtok = pl.BlockSpec((None, s, LANE), lambda i, p: (i, 0, p))
    shp = jax.ShapeDtypeStruct((b, s, nh * HEAD_DIM), BF16)
    return pl.pallas_call(
        body, name=name, grid=(b, npair),
        out_shape=(shp, shp, shp, jax.ShapeDtypeStruct((b, nh, 1, s), F32)),
        in_specs=[tok(qblk), tok(kblk), tok(vblk),
                  pl.BlockSpec((None, s, LANE), lambda i, p: (i, 0, doblk + p)),
                  col, rowspec, rowbspec, col, rowbspec],
        out_specs=(outtok, outtok, outtok, rowspec),
        scratch_shapes=[pltpu.VMEM((2, s, LANE), BF16), pltpu.VMEM((2, s, LANE), BF16),
                        pltpu.VMEM((s, LANE), BF16), pltpu.VMEM((s, LANE), BF16),
                        pltpu.VMEM((LANE, s), BF16), pltpu.VMEM((LANE, s), BF16),
                        pltpu.VMEM((2, nb, LANE, BLK), BF16), pltpu.VMEM((2, nb, LANE, BLK), BF16),
                        pltpu.VMEM((s, LANE), F32), pltpu.VMEM((s, LANE), F32)],
        compiler_params=_cp(("parallel", "parallel")),
    )(proj, proj, proj, dmix, fcol, frow, frowb, lse, lserowb)


def _expm1(x):
    poly = x * (1.0 + x * (1.0 / 2 + x * (1.0 / 6 + x * (1.0 / 24 + x * (1.0 / 120 + x * (1.0 / 720))))))
    return jnp.where(x > -0.1, poly, jnp.exp(x) - 1.0)


def _softplus(z):
    return jnp.maximum(z, 0.0) + jnp.log(1.0 + jnp.exp(-jnp.abs(z)))


def _scan_rows(a, u, carry, row, up):
    tc, c = a.shape
    d = 1
    while d < tc:
        if d < SUBLANE:
            keep = (row >= d) if up else (row < tc - d)
            shift = d if up else tc - d
            a_sh = jnp.where(keep, pltpu.roll(a, shift, 0), 1.0)
            u_sh = jnp.where(keep, pltpu.roll(u, shift, 0), 0.0)
        elif up:
            a_sh = jnp.concatenate([jnp.ones((d, c), F32), a[:tc - d]], axis=0)
            u_sh = jnp.concatenate([jnp.zeros((d, c), F32), u[:tc - d]], axis=0)
        else:
            a_sh = jnp.concatenate([a[d:], jnp.ones((d, c), F32)], axis=0)
            u_sh = jnp.concatenate([u[d:], jnp.zeros((d, c), F32)], axis=0)
        u = a * u_sh + u
        a = a * a_sh
        d *= 2
    return u + a * carry


def _scan_up(a, u, carry, row):
    return _scan_rows(a, u, carry, row, True)


def _scan_down(bnext, g, carry, row):
    return _scan_rows(bnext, g, carry, row, False)


def _pick_row(val, row, which):
    return jnp.sum(jnp.where(row == which, val, 0.0), axis=0, keepdims=True)


def _lru_gates(xpad_ref, t0, tc, cw_ref, cb_ref, wa, ba_ref, wx, bx_ref, sp):
    xw = xpad_ref[pl.ds(t0, tc + SUBLANE), :]
    xc = cb_ref[...]
    for j in range(CONV_WIDTH):
        sh = CONV_WIDTH - 1 - j
        xs = xw if sh == 0 else pltpu.roll(xw, sh, 0)
        xc = xc + xs[SUBLANE:, :] * cw_ref[j:j + 1, :]
    xcb = xc.astype(BF16)
    r = _sigmoid(_dot(xcb, wa) + ba_ref[...])
    i = _sigmoid(_dot(xcb, wx) + bx_ref[...])
    la = -LRU_C * r * sp
    return xc, r, i, la


def _lru_specs(s, cb):
    seq = lambda bi, ni: (bi, 0, ni)
    return dict(
        seq=pl.BlockSpec((None, s, cb), seq),
        cw=pl.BlockSpec((CONV_WIDTH, cb), lambda bi, ni: (0, ni)),
        vec=pl.BlockSpec((1, cb), lambda bi, ni: (0, ni)),
        wblk=pl.BlockSpec((None, cb, cb), lambda bi, ni: (ni, 0, 0)),
    )


def _lru_fwd(proj, cw, cb_, wa, ba, wx, bx, lam, name):
    b, s, _ = proj.shape
    nblk, cb, _ = wa.shape
    tc = min(s, SCAN_CHUNK)
    nc = s // tc

    def body(x_ref, cw_ref, cb_ref, wa_ref, ba_ref, wx_ref, bx_ref, lam_ref, hs_ref, xpad_ref):
        xpad_ref[0:SUBLANE, :] = jnp.zeros((SUBLANE, cb), F32)
        xpad_ref[SUBLANE:, :] = x_ref[...].astype(F32)
        wa_b = wa_ref[...].astype(BF16)
        wx_b = wx_ref[...].astype(BF16)
        sp = _softplus(-lam_ref[...])
        row = lax.broadcasted_iota(jnp.int32, (tc, cb), 0)

        def chunk(ci, carry):
            t0 = pl.multiple_of(ci * tc, tc)
            xc, r, i, la = _lru_gates(xpad_ref, t0, tc, cw_ref, cb_ref, wa_b, ba_ref, wx_b, bx_ref, sp)
            a = jnp.exp(la)
            u = jnp.sqrt(-_expm1(2.0 * la)) * (i * xc)
            h = _scan_up(a, u, carry, row)
            hs_ref[pl.ds(t0, tc), :] = h
            return _pick_row(h, row, tc - 1)

        lax.fori_loop(0, nc, chunk, jnp.zeros((1, cb), F32))

    sp_ = _lru_specs(s, cb)
    return pl.pallas_call(
        body, name=name, grid=(b, nblk),
        out_shape=jax.ShapeDtypeStruct((b, s, nblk * cb), F32),
        in_specs=[sp_["seq"], sp_["cw"], sp_["vec"], sp_["wblk"], sp_["vec"], sp_["wblk"], sp_["vec"], sp_["vec"]],
        out_specs=sp_["seq"],
        scratch_shapes=[pltpu.VMEM((s + SUBLANE, cb), F32)],
        compiler_params=_cp(("parallel", "parallel")),
    )(proj, cw, cb_, wa, ba, wx, bx, lam)


def _lru_bwd(proj, hs, dhs, cw, cb_, wa, ba, wx, bx, lam, name):
    b, s, _ = proj.shape
    nblk, cb, _ = wa.shape
    tc = min(s, SCAN_CHUNK)
    nc = s // tc

    def body(x_ref, hs_ref, dhs_ref, cw_ref, cb_ref, wa_ref, ba_ref, wx_ref, bx_ref, lam_ref,
             dx_ref, dcw_ref, dcb_ref, dwa_ref, dba_ref, dwx_ref, dbx_ref, dlam_ref,
             xpad_ref, hpad_ref, dcpad_ref, xc_ref, r_ref, i_ref, a_ref, mult_ref):
        @pl.when(pl.program_id(1) == 0)
        def _():
            for ref in (dcw_ref, dcb_ref, dwa_ref, dba_ref, dwx_ref, dbx_ref, dlam_ref):
                ref[...] = jnp.zeros_like(ref)

        zeros8 = jnp.zeros((SUBLANE, cb), F32)
        xpad_ref[0:SUBLANE, :] = zeros8
        xpad_ref[SUBLANE:, :] = x_ref[...].astype(F32)
        hpad_ref[0:SUBLANE, :] = zeros8
        hpad_ref[SUBLANE:, :] = hs_ref[...]
        dcpad_ref[s:s + SUBLANE, :] = zeros8
        wa_b = wa_ref[...].astype(BF16)
        wx_b = wx_ref[...].astype(BF16)
        lam_v = lam_ref[...]
        sp = _softplus(-lam_v)
        dsp_dlam = -_sigmoid(-lam_v)
        row = lax.broadcasted_iota(jnp.int32, (tc, cb), 0)

        def recompute(ci, carry):
            t0 = pl.multiple_of(ci * tc, tc)
            xc, r, i, la = _lru_gates(xpad_ref, t0, tc, cw_ref, cb_ref, wa_b, ba_ref, wx_b, bx_ref, sp)
            xc_ref[pl.ds(t0, tc), :] = xc
            r_ref[pl.ds(t0, tc), :] = r
            i_ref[pl.ds(t0, tc), :] = i
            a_ref[pl.ds(t0, tc), :] = jnp.exp(la)
            mult_ref[pl.ds(t0, tc), :] = jnp.sqrt(-_expm1(2.0 * la))
            return carry

        lax.fori_loop(0, nc, recompute, 0)

        def adjoint(k, carry):
            g_next, a_first_next = carry
            t0 = pl.multiple_of((nc - 1 - k) * tc, tc)
            a = a_ref[pl.ds(t0, tc), :]
            a_next = jnp.where(row == tc - 1, a_first_next, pltpu.roll(a, tc - 1, 0))
            gg = _scan_down(a_next, dhs_ref[pl.ds(t0, tc), :], g_next, row)
            h_prev = pltpu.roll(hpad_ref[pl.ds(t0, tc + SUBLANE), :], 1, 0)[SUBLANE:, :]
            xc = xc_ref[pl.ds(t0, tc), :]
            r = r_ref[pl.ds(t0, tc), :]
            i = i_ref[pl.ds(t0, tc), :]
            mult = mult_ref[pl.ds(t0, tc), :]
            d_mult = gg * i * xc
            d_i = gg * mult * xc
            d_xc = gg * mult * i
            d_la = gg * h_prev * a - d_mult * (a * a) / mult
            d_zr = (d_la * (-LRU_C * sp)) * r * (1.0 - r)
            d_zi = d_i * i * (1.0 - i)
            dlam_ref[...] += jnp.sum(d_la * (-LRU_C * r), axis=0, keepdims=True) * dsp_dlam
            dzr_b = d_zr.astype(BF16)
            dzi_b = d_zi.astype(BF16)
            xcb = xc.astype(BF16)
            d_xc = d_xc + _dot_nt(dzr_b, wa_b) + _dot_nt(dzi_b, wx_b)
            dwa_ref[...] += _dot_tn(xcb, dzr_b)
            dwx_ref[...] += _dot_tn(xcb, dzi_b)
            dba_ref[...] += jnp.sum(d_zr, axis=0, keepdims=True)
            dbx_ref[...] += jnp.sum(d_zi, axis=0, keepdims=True)
            dcb_ref[...] += jnp.sum(d_xc, axis=0, keepdims=True)
            dcpad_ref[pl.ds(t0, tc), :] = d_xc
            return _pick_row(gg, row, 0), _pick_row(a, row, 0)

        zero = jnp.zeros((1, cb), F32)
        lax.fori_loop(0, nc, adjoint, (zero, zero))

        def conv_back(ci, carry):
            t0 = pl.multiple_of(ci * tc, tc)
            dw = dcpad_ref[pl.ds(t0, tc + SUBLANE), :]
            xw = xpad_ref[pl.ds(t0, tc + SUBLANE), :]
            d_xc = dw[:tc, :]
            dxr = jnp.zeros((tc, cb), F32)
            for j in range(CONV_WIDTH):
                sh = CONV_WIDTH - 1 - j
                dsh = dw if sh == 0 else pltpu.roll(dw, tc + SUBLANE - sh, 0)
                dxr = dxr + dsh[:tc, :] * cw_ref[j:j + 1, :]
                xs = xw if sh == 0 else pltpu.roll(xw, sh, 0)
                dcw_ref[j:j + 1, :] += jnp.sum(d_xc * xs[SUBLANE:, :], axis=0, keepdims=True)
            dx_ref[pl.ds(t0, tc), :] = dxr.astype(BF16)
            return carry

        lax.fori_loop(0, nc, conv_back, 0)

    seq = lambda ni, bi: (bi, 0, ni)
    seqspec = pl.BlockSpec((None, s, cb), seq)
    cwspec = pl.BlockSpec((CONV_WIDTH, cb), lambda ni, bi: (0, ni))
    vec = pl.BlockSpec((1, cb), lambda ni, bi: (0, ni))
    wblk = pl.BlockSpec((None, cb, cb), lambda ni, bi: (ni, 0, 0))
    w = nblk * cb
    return pl.pallas_call(
        body, name=name, grid=(nblk, b),
        out_shape=(jax.ShapeDtypeStruct((b, s, w), BF16), jax.ShapeDtypeStruct((CONV_WIDTH, w), F32),
                   jax.ShapeDtypeStruct((1, w), F32), jax.ShapeDtypeStruct((nblk, cb, cb), F32),
                   jax.ShapeDtypeStruct((1, w), F32), jax.ShapeDtypeStruct((nblk, cb, cb), F32),
                   jax.ShapeDtypeStruct((1, w), F32), jax.ShapeDtypeStruct((1, w), F32)),
        in_specs=[seqspec, seqspec, seqspec, cwspec, vec, wblk, vec, wblk, vec, vec],
        out_specs=(seqspec, cwspec, vec, wblk, vec, wblk, vec, vec),
        scratch_shapes=[pltpu.VMEM((s + SUBLANE, cb), F32)] * 3 + [pltpu.VMEM((s, cb), F32)] * 5,
        compiler_params=_cp(("parallel", "arbitrary")),
    )(proj, hs, dhs, cw, cb_, wa, ba, wx, bx, lam)


def _adamw(w, g, m, v, name):
    shape = w.shape
    total = int(np.prod(shape))
    if w.ndim >= 2 and shape[-2] % SUBLANE == 0:
        rows, cols = shape[-2:]
    else:
        cols = 1024
        rows = -(-(-(-total // cols)) // SUBLANE) * SUBLANE
    lead = -(-total // (rows * cols))
    tr = _row_tile(rows, 512)
    pad = lead * rows * cols - total

    def flat(a):
        if pad:
            a = jnp.pad(a.reshape(-1), (0, pad))
        return a.reshape(lead, rows, cols)

    c1 = 1.0 - ADAM_B1 ** ADAM_STEP
    c2 = 1.0 - ADAM_B2 ** ADAM_STEP

    def body(w_ref, g_ref, m_ref, v_ref, d_ref, nm_ref, nv_ref):
        gv = g_ref[...]
        nm = ADAM_B1 * m_ref[...] + (1.0 - ADAM_B1) * gv
        nv = ADAM_B2 * v_ref[...] + (1.0 - ADAM_B2) * (gv * gv)
        nm_ref[...] = nm
        nv_ref[...] = nv
        d_ref[...] = -ADAM_LR * ((nm / c1) / (jnp.sqrt(nv / c2) + ADAM_EPS) + ADAM_WD * w_ref[...])

    spec = pl.BlockSpec((None, tr, cols), lambda l, i: (l, i, 0))
    shp = jax.ShapeDtypeStruct((lead, rows, cols), F32)
    outs = pl.pallas_call(
        body, name=name, grid=(lead, rows // tr), out_shape=(shp, shp, shp),
        in_specs=[spec] * 4, out_specs=(spec,) * 3,
        compiler_params=_cp(("parallel", "parallel")),
    )(flat(w), flat(g), flat(m), flat(v))
    if pad:
        return tuple(o.reshape(-1)[:total].reshape(shape) for o in outs)
    return tuple(o.reshape(shape) for o in outs)


def _to_heads(t, nh):
    b, s, _ = t.shape
    return t.reshape(b, s, nh, HEAD_DIM).transpose(0, 2, 1, 3)


def _stack_heads(t):
    b, s, _ = t.shape
    steps = s // (SWA_BPS * BLK)
    t = t.reshape(b, steps, SWA_BPS, BLK, A_KV_HEADS, A_GROUP, HEAD_DIM).transpose(0, 4, 1, 2, 5, 3, 6)
    return t.reshape(b, A_KV_HEADS, steps, SWA_BPS * A_GROUP * BLK, HEAD_DIM)


def _unstack_heads(t):
    b, hkv, steps, rows, hd = t.shape
    t = t.reshape(b, hkv, steps, SWA_BPS, A_GROUP, BLK, hd).transpose(0, 2, 3, 5, 1, 4, 6)
    return t.reshape(b, steps * SWA_BPS * BLK, hkv * A_GROUP * hd)


def _from_heads(t):
    b, nh, s, hd = t.shape
    return t.transpose(0, 2, 1, 3).reshape(b, s, nh * hd)


def _pad_rows(a, mult):
    r = a.shape[0]
    p = (-r) % mult
    return jnp.pad(a, ((0, p), (0, 0))) if p else a


def kernel(x, c, rel_bias, norm_g, ada_w, ada_b, attn_w_in, attn_sinks, attn_b_f, attn_w_out, lru_w_in, lru_conv_w, lru_conv_b, lru_w_a, lru_b_a, lru_w_x, lru_b_x, lru_lambda, lru_w_out, final_g, loss_target, m_rel_bias, m_norm_g, m_ada_w, m_ada_b, m_attn_w_in, m_attn_sinks, m_attn_b_f, m_attn_w_out, m_lru_w_in, m_lru_conv_w, m_lru_conv_b, m_lru_w_a, m_lru_b_a, m_lru_w_x, m_lru_b_x, m_lru_lambda, m_lru_w_out, m_final_g, v_rel_bias, v_norm_g, v_ada_w, v_ada_b, v_attn_w_in, v_attn_sinks, v_attn_b_f, v_attn_w_out, v_lru_w_in, v_lru_conv_w, v_lru_conv_b, v_lru_w_a, v_lru_b_a, v_lru_w_x, v_lru_b_x, v_lru_lambda, v_lru_w_out, v_final_g):
    bl, s, d = x.shape
    ix, iy, ic = lax.axis_index("x"), lax.axis_index("y"), lax.axis_index("c")
    chip = 2 * ix + iy
    me = 2 * chip + ic
    nb = s // BLK
    aw = A_Q_HEADS * HEAD_DIM
    akv = A_KV_HEADS * HEAD_DIM
    bw = B_HEADS * HEAD_DIM
    mixw = aw + bw
    qkv_w = aw + 2 * akv + 3 * bw
    n_in = attn_w_in.shape[2] * N_CHIP
    lw = lru_lambda.shape[1] * N_CHIP
    n0 = mixw + qkv_w + LANE

    rows_pad = -(-bl // SUBLANE) * SUBLANE
    vec_rows = jnp.concatenate([lru_conv_w[0], lru_conv_b, lru_b_a, lru_b_x, lru_lambda], axis=0)
    first = jnp.concatenate([_pad_rows(c, SUBLANE), jnp.pad(vec_rows, ((0, 0), (0, d - lw // N_CHIP)))], axis=0)
    first = _all_gather8(first, "gather_c", pltpu.VMEM).reshape(N_DEV, rows_pad + SUBLANE, d)
    c_all = first[:, :bl].reshape(N_DEV * bl, d)
    vec_all = first[:, rows_pad:, :lw // N_CHIP].reshape(N_CHIP, 2, SUBLANE, lw // N_CHIP)[:, 0]
    vec_all = vec_all.transpose(1, 0, 2).reshape(SUBLANE, lw)
    ncol = ada_w.shape[2]
    ada_w_l = lax.dynamic_index_in_dim(ada_w, ic, 0, keepdims=False)
    ada_b_l = lax.dynamic_slice(ada_b, (ic, chip * ncol), (1, ncol))
    mod_part = _ada_fwd(c_all, ada_w_l, ada_b_l, "ada_fwd")
    mod_all = _all_gather8(_pad_rows(mod_part, SUBLANE), "gather_mod", pltpu.VMEM)
    mrows = -(-(N_DEV * bl) // SUBLANE) * SUBLANE
    mod_all = mod_all.reshape(N_CHIP, 2, mrows, ncol)[:, :, :N_DEV * bl]
    mod_all = mod_all.transpose(1, 2, 0, 3).reshape(2, N_DEV * bl, N_CHIP * ncol)
    mod = lax.dynamic_slice_in_dim(mod_all, me * bl, bl, axis=1)
    shift = [mod[l, :, 0:d].reshape(bl, 1, d) for l in range(2)]
    scale = [mod[l, :, d:2 * d].reshape(bl, 1, d) for l in range(2)]
    gmod = [mod[l, :, 2 * d:3 * d].reshape(bl, 1, d) for l in range(2)]

    c_in0 = n_in // N_CHIP
    c_in1 = 2 * lw // N_CHIP
    assert c_in0 <= d and 2 * c_in1 == d
    r_in0, r_out0, r_in1, r_out1 = d // 2, mixw // N_CHIP // 2, d // 4, lw // N_CHIP // 2
    o_out0, o_in1, o_out1 = r_in0, r_in0 + r_out0, r_in0 + r_out0 + r_in1
    big_rows = o_out1 + r_out1

    def half_of(a, rows):
        return lax.dynamic_slice_in_dim(a, ic * rows, rows, axis=0)

    h_in1 = half_of(lru_w_in[0], r_in0).astype(BF16)
    my_half = jnp.concatenate([
        jnp.pad(half_of(attn_w_in[0], r_in0).astype(BF16), ((0, 0), (0, d - c_in0))),
        half_of(attn_w_out[0], r_out0).astype(BF16),
        jnp.concatenate([h_in1[:r_in1], h_in1[r_in1:]], axis=1),
        half_of(lru_w_out[0], r_out1).astype(BF16)], axis=0)
    gat = _all_gather8(my_half, "gather_weights", pltpu.HBM).reshape(N_CHIP, 2, big_rows, d)
    w_in0 = gat[:, :, :r_in0, :c_in0].transpose(1, 2, 0, 3).reshape(d, n_in)
    w_out0 = gat[:, :, o_out0:o_in1].reshape(mixw, d)
    w_in1 = gat[:, :, o_in1:o_out1].reshape(N_CHIP, 2, r_in1, 2, c_in1)
    w_in1 = w_in1.transpose(1, 3, 2, 0, 4).reshape(d, 2 * lw)
    w_out1 = gat[:, :, o_out1:].reshape(lw, d)
    w_cat0 = jnp.concatenate([w_in0[:, qkv_w + B_HEADS:], w_in0[:, :qkv_w + B_HEADS],
                              jnp.zeros((d, n0 - n_in), BF16)], axis=1)

    proj0, h0, zf = _norm_proj(x, norm_g[0:1], scale[0], shift[0], w_cat0, LANE, "norm_proj0")
    o_a = mixw
    aq = _stack_heads(proj0[:, :, o_a:o_a + aw].astype(BF16))
    ak = _to_heads(proj0[:, :, o_a + aw:o_a + aw + akv].astype(BF16), A_KV_HEADS)
    av = _to_heads(proj0[:, :, o_a + aw + akv:o_a + aw + 2 * akv].astype(BF16), A_KV_HEADS)
    o_b = o_a + aw + 2 * akv
    fox_blks = (o_b // LANE, (o_b + bw) // LANE, (o_b + 2 * bw) // LANE)
    bucket_np, valid_np = _rel_buckets()
    bucket = jnp.asarray(bucket_np)
    bias = _swa_bias(rel_bias.T, bucket, jnp.asarray(valid_np), "swa_bias")
    bias = bias.reshape(A_KV_HEADS, A_GROUP * BLK, 2 * BLK)
    bias = jnp.concatenate([jnp.pad(bias, ((0, 0), (0, 0), (blk * BLK, (SWA_BPS - 1 - blk) * BLK)),
                                    constant_values=NEG) for blk in range(SWA_BPS)], axis=1)
    sinks = jnp.repeat(attn_sinks[0].reshape(A_KV_HEADS, A_GROUP), BLK, axis=1).reshape(A_KV_HEADS, A_GROUP * BLK, 1)
    sinks = jnp.tile(sinks, (1, SWA_BPS, 1))
    a_out, a_lse = _swa_fwd(aq, ak, av, bias, sinks, "swa_fwd")
    bf_pad = jnp.pad(attn_b_f, ((0, 0), (0, LANE - B_HEADS)))
    fsum = _fox_decay(zf, bf_pad, "fox_decay")
    fh = fsum[:, :, :B_HEADS].transpose(0, 2, 1)
    fcol = fh.reshape(bl, B_HEADS, s, 1)
    frow = fh.reshape(bl, B_HEADS, 1, s)
    fbq = min(s, FOX_BQ)
    frowb = fh.reshape(bl, B_HEADS, s // fbq, 1, fbq)
    b_out, b_lse = _fox_fwd(proj0, *fox_blks, fcol, frow, "fox_fwd")
    lserowb = b_lse.reshape(bl, B_HEADS, s // fbq, 1, fbq)
    mix0 = [_unstack_heads(a_out), b_out]
    x1, o0, proj1, h1 = _exit_norm_proj(mix0, proj0, 0, w_out0, x, gmod[0], norm_g[1:2], scale[1], shift[1], w_in1,
                                        "exit0_norm_proj1")
    cw_f, cb_f, ba_f, bx_f, lam_f = vec_all[0:4], vec_all[4:5], vec_all[5:6], vec_all[6:7], vec_all[7:8]
    hs = _lru_fwd(proj1, cw_f, cb_f, lru_w_a[0], ba_f, lru_w_x[0], bx_f, lam_f, "lru_fwd")
    o1, loss_vec, dx2, g_final = _last_exit_loss(hs, proj1, 1, w_out1, x1, gmod[1], final_g.reshape(1, d),
                                                 loss_target, "exit_loss")
    loss = lax.psum(loss_vec[0, 0], ("x", "y", "c"))

    dhs, dgate1, do1, y1, dgm1 = _bwd_out(dx2, gmod[1], o1, [hs], proj1, 1, w_out1, F32, "bwd_out1")
    g_w_out1 = _matmul_tn(y1, [do1], "grad_w_out1")
    (dxr, g_cw, g_cb, g_wa, g_ba, g_wx, g_bx, g_lam) = _lru_bwd(
        proj1, hs, dhs, cw_f, cb_f, lru_w_a[0], ba_f, lru_w_x[0], bx_f, lam_f, "lru_bwd")
    dproj1 = [dxr, dgate1]
    g_w_in1 = _matmul_tn(h1, dproj1, "grad_w_in1")
    dx1, dsh1, dsc1, g_ng1 = _bwd_in(dproj1, w_in1, x1, norm_g[1:2], scale[1], dx2, "bwd_in1")

    dmix0, dgate0, do0, y0, dgm0 = _bwd_out(dx1, gmod[0], o0, mix0, proj0, 0, w_out0, BF16, "bwd_out0")
    g_w_out0 = _matmul_tn(y0, [do0], "grad_w_out0")
    da_out = _stack_heads(dmix0[:, :, :aw].astype(BF16))
    daq, dak, dav, dbias, dsink = _swa_bwd(aq, ak, av, bias, sinks, da_out, a_lse, "swa_bwd")
    dbq, dbk, dbv, dfrow = _fox_bwd(proj0, *fox_blks, dmix0, aw // LANE, fcol, frow, frowb, b_lse, lserowb,
                                    "fox_bwd")
    df = dfrow.reshape(bl, B_HEADS, s).transpose(0, 2, 1)
    df = jnp.pad(df, ((0, 0), (0, 0), (0, LANE - B_HEADS)))
    dzf, g_bf = _fox_dgate(df, zf, bf_pad, B_HEADS, "fox_dgate")
    dproj0 = ([dgate0, _unstack_heads(daq), _from_heads(dak), _from_heads(dav)]
              + [dbq, dbk, dbv, dzf.astype(BF16)])
    g_w_cat0 = _matmul_tn(h0, dproj0, "grad_w_in0")
    g_w_in0 = jnp.concatenate([g_w_cat0[:, mixw:mixw + qkv_w + B_HEADS], g_w_cat0[:, :mixw]], axis=1)
    dx0, dsh0, dsc0, g_ng0 = _bwd_in(dproj0, w_cat0, x, norm_g[0:1], scale[0], dx1, "bwd_in0")
    dbias = sum(dbias[:, :, blk * A_GROUP * BLK:(blk + 1) * A_GROUP * BLK, blk * BLK:(blk + 2) * BLK]
                for blk in range(SWA_BPS))
    g_relb, g_sink = _swa_small_grads(dbias.reshape(bl, A_Q_HEADS, BLK, 2 * BLK),
                                      dsink.reshape(bl, A_Q_HEADS, 1, LANE), bucket, "swa_small_grads")

    dmod = jnp.concatenate([jnp.concatenate([dsh0, dsc0, dgm0], axis=-1),
                            jnp.concatenate([dsh1, dsc1, dgm1], axis=-1)], axis=1)
    dmod_rows = _pad_rows(dmod.reshape(bl * 6, d), SUBLANE)

    tail = jnp.concatenate([g_relb[:, :, 0].T.reshape(-1), g_sink[:, 0, 0], g_bf[0, :B_HEADS]])
    n_relb = REL_BUCKETS * A_Q_HEADS
    small_rows = [g_wa.reshape(-1, d), g_wx.reshape(-1, d), g_ng0, g_ng1, g_final, g_cw, g_cb, g_ba, g_bx, g_lam,
                  jnp.pad(tail, (0, d - tail.shape[0])).reshape(1, d)]
    small_counts = [r.shape[0] for r in small_rows]
    piece_rows = -(-(-(-sum(small_counts) // N_DEV)) // SUBLANE) * SUBLANE
    small_2d = jnp.concatenate(small_rows, axis=0)
    small_2d = jnp.pad(small_2d, ((0, N_DEV * piece_rows - small_2d.shape[0]), (0, 0)))
    small_pieces = small_2d.reshape(N_CHIP, 2, piece_rows, d)
    p_in0 = jnp.pad(g_w_in0.reshape(2, r_in0, N_CHIP, c_in0).transpose(2, 0, 1, 3),
                    ((0, 0), (0, 0), (0, 0), (0, d - c_in0)))
    p_in1 = g_w_in1.reshape(2, 2, r_in1, N_CHIP, c_in1).transpose(3, 0, 2, 1, 4).reshape(N_CHIP, 2, r_in1, d)
    pieces = jnp.concatenate([p_in0, g_w_out0.reshape(N_CHIP, 2, r_out0, d), p_in1,
                              g_w_out1.reshape(N_CHIP, 2, r_out1, d), small_pieces], axis=2)
    theirs = _sibling_push(pieces, True, "push_sibling_halves")
    partial = _pair_sum(jnp.reshape(ic, (1,)).astype(jnp.int32), pieces, theirs, "sum_chip")
    slots = _chip_all_to_all(partial, "exchange_grads")
    reduced = _sum_slots(slots, "sum_grads")
    mine_big = reduced[:big_rows]
    other_big = _sibling_push(mine_big[None], False, "swap_halves")[0]
    both = jnp.stack([jnp.where(ic == 0, mine_big, other_big), jnp.where(ic == 0, other_big, mine_big)])
    g_big = [both[:, :r_in0, :c_in0].reshape(d, c_in0),
             both[:, o_out0:o_in1].reshape(2 * r_out0, d),
             both[:, o_in1:o_out1].reshape(2, r_in1, 2, c_in1).transpose(0, 2, 1, 3).reshape(d, c_in1),
             both[:, o_out1:].reshape(2 * r_out1, d)]
    last = _all_gather8(jnp.concatenate([reduced[big_rows:], dmod_rows], axis=0), "gather_small_grads", pltpu.VMEM)
    last = last.reshape(N_DEV, piece_rows + dmod_rows.shape[0], d)
    small_all = last[:, :piece_rows].reshape(N_DEV * piece_rows, d)
    dmod_all = last[:, piece_rows:piece_rows + bl * 6].reshape(N_DEV * bl, 6 * d)
    dmod_chip = lax.dynamic_slice_in_dim(dmod_all.reshape(N_DEV * bl, 2, 3 * d), chip * ncol, ncol, axis=2)
    g_ada_w, g_ada_b = _ada_bwd(c_all, dmod_chip.transpose(1, 0, 2), dmod_all, "ada_bwd")
    g_ada_b = g_ada_b.reshape(2, 3 * d)
    g_small, off = [], 0
    for cnt in small_counts:
        g_small.append(small_all[off:off + cnt])
        off += cnt
    g_w_a, g_w_x = g_small[0].reshape(lru_w_a.shape[1:]), g_small[1].reshape(lru_w_x.shape[1:])
    g_norm_g = jnp.concatenate(g_small[2:4], axis=0)
    g_fin, g_cw_r, g_cb_r, g_ba_r, g_bx_r, g_lam_r = g_small[4:10]
    tail = g_small[10][0]
    g_rel_bias = tail[:n_relb].reshape(REL_BUCKETS, A_Q_HEADS)
    g_sinks, g_b_f = tail[n_relb:n_relb + A_Q_HEADS], tail[n_relb + A_Q_HEADS:n_relb + A_Q_HEADS + B_HEADS]
    cw4 = lw // N_CHIP

    def my_cols(a):
        return lax.dynamic_slice_in_dim(a, chip * cw4, cw4, axis=1)

    grads = {
        "rel_bias": g_rel_bias, "norm_g": g_norm_g, "ada_w": g_ada_w, "ada_b": g_ada_b,
        "attn_w_in": g_big[0][None], "attn_sinks": g_sinks[None], "attn_b_f": g_b_f[None],
        "attn_w_out": g_big[1][None], "lru_w_in": g_big[2][None], "lru_conv_w": my_cols(g_cw_r)[None],
        "lru_conv_b": my_cols(g_cb_r), "lru_w_a": g_w_a[None], "lru_b_a": my_cols(g_ba_r),
        "lru_w_x": g_w_x[None], "lru_b_x": my_cols(g_bx_r), "lru_lambda": my_cols(g_lam_r),
        "lru_w_out": g_big[3][None], "final_g": g_fin.reshape(d),
    }
    weights = dict(rel_bias=rel_bias, norm_g=norm_g, ada_w=ada_w, ada_b=ada_b, attn_w_in=attn_w_in,
                   attn_sinks=attn_sinks, attn_b_f=attn_b_f, attn_w_out=attn_w_out, lru_w_in=lru_w_in,
                   lru_conv_w=lru_conv_w, lru_conv_b=lru_conv_b, lru_w_a=lru_w_a, lru_b_a=lru_b_a,
                   lru_w_x=lru_w_x, lru_b_x=lru_b_x, lru_lambda=lru_lambda, lru_w_out=lru_w_out, final_g=final_g)
    moms = dict(rel_bias=(m_rel_bias, v_rel_bias), norm_g=(m_norm_g, v_norm_g), ada_w=(m_ada_w, v_ada_w),
                ada_b=(m_ada_b, v_ada_b), attn_w_in=(m_attn_w_in, v_attn_w_in),
                attn_sinks=(m_attn_sinks, v_attn_sinks), attn_b_f=(m_attn_b_f, v_attn_b_f),
                attn_w_out=(m_attn_w_out, v_attn_w_out), lru_w_in=(m_lru_w_in, v_lru_w_in),
                lru_conv_w=(m_lru_conv_w, v_lru_conv_w), lru_conv_b=(m_lru_conv_b, v_lru_conv_b),
                lru_w_a=(m_lru_w_a, v_lru_w_a), lru_b_a=(m_lru_b_a, v_lru_b_a), lru_w_x=(m_lru_w_x, v_lru_w_x),
                lru_b_x=(m_lru_b_x, v_lru_b_x), lru_lambda=(m_lru_lambda, v_lru_lambda),
                lru_w_out=(m_lru_w_out, v_lru_w_out), final_g=(m_final_g, v_final_g))
    names = list(weights)
    big_names = [n for n in names if weights[n].size >= 65536]
    small_names = [n for n in names if weights[n].size < 65536]
    delta, new_m, new_v = {}, {}, {}
    for n in big_names:
        delta[n], new_m[n], new_v[n] = _adamw(weights[n], grads[n].reshape(weights[n].shape),
                                              moms[n][0], moms[n][1], "adamw_" + n)
    cat = lambda arrs: jnp.concatenate([a.reshape(-1) for a in arrs])
    sd, sm, sv = _adamw(cat([weights[n] for n in small_names]), cat([grads[n] for n in small_names]),
                        cat([moms[n][0] for n in small_names]), cat([moms[n][1] for n in small_names]),
                        "adamw_small")
    off = 0
    for n in small_names:
        sz = weights[n].size
        shp = weights[n].shape
        delta[n], new_m[n], new_v[n] = (sd[off:off + sz].reshape(shp), sm[off:off + sz].reshape(shp),
                                        sv[off:off + sz].reshape(shp))
        off += sz
    out_grads = [grads[n].reshape(weights[n].shape) for n in names]
    return (loss, dx0, *out_grads, *[delta[n] for n in names], *[new_m[n] for n in names],
            *[new_v[n] for n in names])
```
